```python
import jax, jax.numpy as jnp
from jax import lax
import numpy as np

D_MODEL = 2048
BATCH = 8
SEQ = 2048
DEPTH = 2

CHUNK = 64
N_MEM = 256
SB_HEAD_DIM = 128
SB_WIDTH = D_MODEL // 2
SB_HEADS = SB_WIDTH // SB_HEAD_DIM
SB_QBLOCK = 128
GM_GROUP_DIM = 128
GM_WIDTH = D_MODEL // 2
GM_GROUPS = GM_WIDTH // GM_GROUP_DIM
GM_BLOCK = 128
XA_HEADS = 4
XA_WIDTH = D_MODEL // 2
XA_HEAD_DIM = XA_WIDTH // XA_HEADS
N_BRANCH = 3
IN_WIDTH = 3 * SB_WIDTH + 2 * GM_WIDTH + XA_WIDTH
D_FF = 5632
CONV_W = 3
EPS = 1e-6

kernel_name = "hybrid_stickbreak_gmlp_memxattn_convffn"


def rmsnorm(x, g):
    xf = x.astype(jnp.float32)
    xf = xf * lax.rsqrt(jnp.mean(xf * xf, axis=-1, keepdims=True) + EPS)
    return xf.astype(x.dtype) * g


def stick_breaking_attention(q, k, v):
    B, S, H, Dh = q.shape
    scale = Dh ** -0.5
    outs = []
    for i in range(S // SB_QBLOCK):
        q0 = i * SB_QBLOCK
        q1 = q0 + SB_QBLOCK
        qb = q[:, q0:q1]
        kb = k[:, :q1]
        vb = v[:, :q1]
        z = jnp.einsum('bthd,bshd->bhts', qb, kb).astype(jnp.float32) * scale
        t_pos = q0 + jnp.arange(SB_QBLOCK)[:, None]
        s_pos = jnp.arange(q1)[None, :]
        strict = s_pos < t_pos
        log_keep = jnp.where(strict, jax.nn.log_sigmoid(-z), 0.0)
        suffix = lax.cumsum(log_keep, axis=3, reverse=True) - log_keep
        log_a = jax.nn.log_sigmoid(z) + suffix
        a = jnp.where(strict, jnp.exp(log_a), 0.0)
        outs.append(jnp.einsum('bhts,bshd->bthd', a.astype(v.dtype), vb))
    return jnp.concatenate(outs, axis=1)


def spatial_gating(u, v, g_vnorm, w_s, b_s):
    B, S, _ = u.shape
    u = jax.nn.gelu(u)
    v = rmsnorm(jax.nn.gelu(v), g_vnorm)
    pos = jnp.arange(GM_BLOCK)
    mask = (pos[None, :] // CHUNK) <= (pos[:, None] // CHUNK)
    w = jnp.where(mask[None], w_s, jnp.zeros_like(w_s))
    vb = v.reshape(B, S // GM_BLOCK, GM_BLOCK, GM_GROUPS, GM_GROUP_DIM)
    mixed = jnp.einsum('gts,bcsgd->bctgd', w, vb) + b_s.T[None, None, :, :, None]
    return u * mixed.reshape(B, S, GM_WIDTH)


def memory_cross_attention(q, mem_kv):
    B, M, _ = mem_kv.shape
    k, v = jnp.split(mem_kv, 2, axis=-1)
    k = k.reshape(B, M, XA_HEADS, XA_HEAD_DIM)
    v = v.reshape(B, M, XA_HEADS, XA_HEAD_DIM)
    z = jnp.einsum('bthd,bmhd->bhtm', q, k).astype(jnp.float32) * (XA_HEAD_DIM ** -0.5)
    p = jax.nn.softmax(z, axis=-1)
    return jnp.einsum('bhtm,bmhd->bthd', p.astype(v.dtype), v)


def conv_ffn(h, w_up, conv_w, conv_b, w_down):
    S = h.shape[1]
    up = h @ w_up
    gate, val = jnp.split(up, 2, axis=-1)
    gp = jnp.pad(gate, ((0, 0), (CONV_W - 1, 0), (0, 0)))
    conv = conv_b + sum(conv_w[i] * gp[:, i:i + S] for i in range(CONV_W))
    return (jax.nn.gelu(conv) * val) @ w_down


def _fwd_setup_inputs(seed: int = 0) -> dict:
    key = jax.random.key(seed)
    ks = jax.random.split(key, 24)
    f32 = jnp.float32

    def nrm(k, shape, fan_in):
        return jax.random.normal(k, shape, f32) * (fan_in ** -0.5)

    def gain(k, n):
        return 1.0 + 0.05 * jax.random.normal(k, (DEPTH, n), f32)

    L = DEPTH
    return {
        "x": jax.random.normal(ks[0], (BATCH, SEQ, D_MODEL), f32),
        "mem": jax.random.normal(ks[1], (BATCH, N_MEM, D_MODEL), f32),
        "g_mix_pre": gain(ks[2], D_MODEL),
        "w_in": nrm(ks[3], (L, D_MODEL, IN_WIDTH), D_MODEL),
        "g_vnorm": gain(ks[4], GM_WIDTH),
        "w_s": nrm(ks[5], (L, GM_GROUPS, GM_BLOCK, GM_BLOCK), GM_BLOCK),
        "b_s": 1.0 + 0.01 * jax.random.normal(ks[6], (L, GM_GROUPS, GM_BLOCK), f32),
        "g_mem": gain(ks[7], D_MODEL),
        "w_mem_kv": nrm(ks[8], (L, D_MODEL, 2 * XA_WIDTH), D_MODEL),
        "w_gate": nrm(ks[9], (L, D_MODEL, N_BRANCH * D_MODEL), D_MODEL),
        "b_gate": 0.01 * jax.random.normal(ks[10], (L, N_BRANCH * D_MODEL), f32),
        "w_br_sb": nrm(ks[11], (L, SB_WIDTH, D_MODEL), SB_WIDTH),
        "w_br_gm": nrm(ks[12], (L, GM_WIDTH, D_MODEL), GM_WIDTH),
        "w_br_xa": nrm(ks[13], (L, XA_WIDTH, D_MODEL), XA_WIDTH),
        "w_out": nrm(ks[14], (L, D_MODEL, D_MODEL), D_MODEL),
        "g_mix_post": gain(ks[15], D_MODEL),
        "g_ffn_pre": gain(ks[16], D_MODEL),
        "w_up": nrm(ks[17], (L, D_MODEL, 2 * D_FF), D_MODEL),
        "conv_w": nrm(ks[18], (L, CONV_W, D_FF), CONV_W),
        "conv_b": 0.01 * jax.random.normal(ks[19], (L, D_FF), f32),
        "w_down": nrm(ks[20], (L, D_FF, D_MODEL), D_FF),
        "g_ffn_post": gain(ks[21], D_MODEL),
    }


def _fwd_reference(x, mem, g_mix_pre, w_in, g_vnorm, w_s, b_s, g_mem, w_mem_kv, w_gate, b_gate,
              w_br_sb, w_br_gm, w_br_xa, w_out, g_mix_post, g_ffn_pre, w_up, conv_w, conv_b,
              w_down, g_ffn_post):
    B, S, D = x.shape
    splits = [SB_WIDTH, 2 * SB_WIDTH, 3 * SB_WIDTH,
              3 * SB_WIDTH + GM_WIDTH, 3 * SB_WIDTH + 2 * GM_WIDTH]
    for l in range(DEPTH):
        h = rmsnorm(x, g_mix_pre[l])
        proj = h @ w_in[l]
        q_sb, k_sb, v_sb, u_gm, v_gm, q_xa = jnp.split(proj, splits, axis=-1)
        hs = (B, S, SB_HEADS, SB_HEAD_DIM)
        o_sb = stick_breaking_attention(q_sb.reshape(hs), k_sb.reshape(hs),
                                        v_sb.reshape(hs)).reshape(B, S, SB_WIDTH)
        o_gm = spatial_gating(u_gm, v_gm, g_vnorm[l], w_s[l], b_s[l])
        mem_kv = rmsnorm(mem, g_mem[l]) @ w_mem_kv[l]
        o_xa = memory_cross_attention(q_xa.reshape(B, S, XA_HEADS, XA_HEAD_DIM),
                                      mem_kv).reshape(B, S, XA_WIDTH)
        gates = jax.nn.sigmoid(h @ w_gate[l] + b_gate[l]).reshape(B, S, N_BRANCH, D)
        merged = (gates[:, :, 0] * (o_sb @ w_br_sb[l])
                  + gates[:, :, 1] * (o_gm @ w_br_gm[l])
                  + gates[:, :, 2] * (o_xa @ w_br_xa[l]))
        x = x + rmsnorm(merged @ w_out[l], g_mix_post[l])
        h = rmsnorm(x, g_ffn_pre[l])
        x = x + rmsnorm(conv_ffn(h, w_up[l], conv_w[l], conv_b[l], w_down[l]), g_ffn_post[l])
    return x


import jax as _jax
import jax.numpy as _jnp

TWIN_FORMAT = 'train_step'
FWD_PARAMS = ['x', 'mem', 'g_mix_pre', 'w_in', 'g_vnorm', 'w_s', 'b_s', 'g_mem', 'w_mem_kv', 'w_gate', 'b_gate', 'w_br_sb', 'w_br_gm', 'w_br_xa', 'w_out', 'g_mix_post', 'g_ffn_pre', 'w_up', 'conv_w', 'conv_b', 'w_down', 'g_ffn_post']
TWIN_WEIGHTS = ['g_mix_pre', 'w_in', 'g_vnorm', 'w_s', 'b_s', 'g_mem', 'w_mem_kv', 'w_gate', 'b_gate', 'w_br_sb', 'w_br_gm', 'w_br_xa', 'w_out', 'g_mix_post', 'g_ffn_pre', 'w_up', 'conv_w', 'conv_b', 'w_down', 'g_ffn_post']
TWIN_DIFF_INPUT = 'x'
TWIN_INPUTS = ['x', 'mem', 'g_mix_pre', 'w_in', 'g_vnorm', 'w_s', 'b_s', 'g_mem', 'w_mem_kv', 'w_gate', 'b_gate', 'w_br_sb', 'w_br_gm', 'w_br_xa', 'w_out', 'g_mix_post', 'g_ffn_pre', 'w_up', 'conv_w', 'conv_b', 'w_down', 'g_ffn_post', 'loss_target', 'm_g_mix_pre', 'm_w_in', 'm_g_vnorm', 'm_w_s', 'm_b_s', 'm_g_mem', 'm_w_mem_kv', 'm_w_gate', 'm_b_gate', 'm_w_br_sb', 'm_w_br_gm', 'm_w_br_xa', 'm_w_out', 'm_g_mix_post', 'm_g_ffn_pre', 'm_w_up', 'm_conv_w', 'm_conv_b', 'm_w_down', 'm_g_ffn_post', 'v_g_mix_pre', 'v_w_in', 'v_g_vnorm', 'v_w_s', 'v_b_s', 'v_g_mem', 'v_w_mem_kv', 'v_w_gate', 'v_b_gate', 'v_w_br_sb', 'v_w_br_gm', 'v_w_br_xa', 'v_w_out', 'v_g_mix_post', 'v_g_ffn_pre', 'v_w_up', 'v_conv_w', 'v_conv_b', 'v_w_down', 'v_g_ffn_post']
TWIN_OUTPUTS = ['loss', 'grad_x', 'grad_g_mix_pre', 'grad_w_in', 'grad_g_vnorm', 'grad_w_s', 'grad_b_s', 'grad_g_mem', 'grad_w_mem_kv', 'grad_w_gate', 'grad_b_gate', 'grad_w_br_sb', 'grad_w_br_gm', 'grad_w_br_xa', 'grad_w_out', 'grad_g_mix_post', 'grad_g_ffn_pre', 'grad_w_up', 'grad_conv_w', 'grad_conv_b', 'grad_w_down', 'grad_g_ffn_post', 'delta_g_mix_pre', 'delta_w_in', 'delta_g_vnorm', 'delta_w_s', 'delta_b_s', 'delta_g_mem', 'delta_w_mem_kv', 'delta_w_gate', 'delta_b_gate', 'delta_w_br_sb', 'delta_w_br_gm', 'delta_w_br_xa', 'delta_w_out', 'delta_g_mix_post', 'delta_g_ffn_pre', 'delta_w_up', 'delta_conv_w', 'delta_conv_b', 'delta_w_down', 'delta_g_ffn_post', 'new_m_g_mix_pre', 'new_m_w_in', 'new_m_g_vnorm', 'new_m_w_s', 'new_m_b_s', 'new_m_g_mem', 'new_m_w_mem_kv', 'new_m_w_gate', 'new_m_b_gate', 'new_m_w_br_sb', 'new_m_w_br_gm', 'new_m_w_br_xa', 'new_m_w_out', 'new_m_g_mix_post', 'new_m_g_ffn_pre', 'new_m_w_up', 'new_m_conv_w', 'new_m_conv_b', 'new_m_w_down', 'new_m_g_ffn_post', 'new_v_g_mix_pre', 'new_v_w_in', 'new_v_g_vnorm', 'new_v_w_s', 'new_v_b_s', 'new_v_g_mem', 'new_v_w_mem_kv', 'new_v_w_gate', 'new_v_b_gate', 'new_v_w_br_sb', 'new_v_w_br_gm', 'new_v_w_br_xa', 'new_v_w_out', 'new_v_g_mix_post', 'new_v_g_ffn_pre', 'new_v_w_up', 'new_v_conv_w', 'new_v_conv_b', 'new_v_w_down', 'new_v_g_ffn_post']
TWIN_LEAF_KINDS = {'loss': 'loss', 'grad_x': 'grad_x', 'grad_g_mix_pre': 'grad_w', 'grad_w_in': 'grad_w', 'grad_g_vnorm': 'grad_w', 'grad_w_s': 'grad_w', 'grad_b_s': 'grad_w', 'grad_g_mem': 'grad_w', 'grad_w_mem_kv': 'grad_w', 'grad_w_gate': 'grad_w', 'grad_b_gate': 'grad_w', 'grad_w_br_sb': 'grad_w', 'grad_w_br_gm': 'grad_w', 'grad_w_br_xa': 'grad_w', 'grad_w_out': 'grad_w', 'grad_g_mix_post': 'grad_w', 'grad_g_ffn_pre': 'grad_w', 'grad_w_up': 'grad_w', 'grad_conv_w': 'grad_w', 'grad_conv_b': 'grad_w', 'grad_w_down': 'grad_w', 'grad_g_ffn_post': 'grad_w', 'delta_g_mix_pre': 'delta_w', 'delta_w_in': 'delta_w', 'delta_g_vnorm': 'delta_w', 'delta_w_s': 'delta_w', 'delta_b_s': 'delta_w', 'delta_g_mem': 'delta_w', 'delta_w_mem_kv': 'delta_w', 'delta_w_gate': 'delta_w', 'delta_b_gate': 'delta_w', 'delta_w_br_sb': 'delta_w', 'delta_w_br_gm': 'delta_w', 'delta_w_br_xa': 'delta_w', 'delta_w_out': 'delta_w', 'delta_g_mix_post': 'delta_w', 'delta_g_ffn_pre': 'delta_w', 'delta_w_up': 'delta_w', 'delta_conv_w': 'delta_w', 'delta_conv_b': 'delta_w', 'delta_w_down': 'delta_w', 'delta_g_ffn_post': 'delta_w', 'new_m_g_mix_pre': 'new_m', 'new_m_w_in': 'new_m', 'new_m_g_vnorm': 'new_m', 'new_m_w_s': 'new_m', 'new_m_b_s': 'new_m', 'new_m_g_mem': 'new_m', 'new_m_w_mem_kv': 'new_m', 'new_m_w_gate': 'new_m', 'new_m_b_gate': 'new_m', 'new_m_w_br_sb': 'new_m', 'new_m_w_br_gm': 'new_m', 'new_m_w_br_xa': 'new_m', 'new_m_w_out': 'new_m', 'new_m_g_mix_post': 'new_m', 'new_m_g_ffn_pre': 'new_m', 'new_m_w_up': 'new_m', 'new_m_conv_w': 'new_m', 'new_m_conv_b': 'new_m', 'new_m_w_down': 'new_m', 'new_m_g_ffn_post': 'new_m', 'new_v_g_mix_pre': 'new_v', 'new_v_w_in': 'new_v', 'new_v_g_vnorm': 'new_v', 'new_v_w_s': 'new_v', 'new_v_b_s': 'new_v', 'new_v_g_mem': 'new_v', 'new_v_w_mem_kv': 'new_v', 'new_v_w_gate': 'new_v', 'new_v_b_gate': 'new_v', 'new_v_w_br_sb': 'new_v', 'new_v_w_br_gm': 'new_v', 'new_v_w_br_xa': 'new_v', 'new_v_w_out': 'new_v', 'new_v_g_mix_post': 'new_v', 'new_v_g_ffn_pre': 'new_v', 'new_v_w_up': 'new_v', 'new_v_conv_w': 'new_v', 'new_v_conv_b': 'new_v', 'new_v_w_down': 'new_v', 'new_v_g_ffn_post': 'new_v'}


def _forward(args):
    return _fwd_reference(*[args[k] for k in FWD_PARAMS])


def _output_shape():
    out = _jax.eval_shape(lambda: _forward(_fwd_setup_inputs(0)))
    return out.shape, out.dtype

N_MICROBATCH = 1
ADAM_LR = 0.001
ADAM_B1 = 0.9
ADAM_B2 = 0.999
ADAM_EPS = 1e-08
ADAM_WD = 0.01
ADAM_STEP = 10
PER_EXAMPLE_BATCH_AXIS = {'x': 0, 'mem': 0, 'loss_target': 0}
SHARED_INPUTS = []
_WEIGHT_DTYPES = {'g_mix_pre': _jnp.float32, 'w_in': _jnp.float32, 'g_vnorm': _jnp.float32, 'w_s': _jnp.float32, 'b_s': _jnp.float32, 'g_mem': _jnp.float32, 'w_mem_kv': _jnp.float32, 'w_gate': _jnp.float32, 'b_gate': _jnp.float32, 'w_br_sb': _jnp.float32, 'w_br_gm': _jnp.float32, 'w_br_xa': _jnp.float32, 'w_out': _jnp.float32, 'g_mix_post': _jnp.float32, 'g_ffn_pre': _jnp.float32, 'w_up': _jnp.float32, 'conv_w': _jnp.float32, 'conv_b': _jnp.float32, 'w_down': _jnp.float32, 'g_ffn_post': _jnp.float32}
MOMENT_SCALE = {'g_mix_pre': 3.838070e-01, 'w_in': 2.087300e-01, 'g_vnorm': 1.853351e-01, 'w_s': 1.810228e-01, 'b_s': 2.079281e-01, 'g_mem': 6.964759e-02, 'w_mem_kv': 7.168249e-02, 'w_gate': 5.926497e-02, 'b_gate': 1.532017e-01, 'w_br_sb': 2.266907e-01, 'w_br_gm': 6.290532e-01, 'w_br_xa': 7.264118e-02, 'w_out': 6.660589e-01, 'g_mix_post': 8.061902e+00, 'g_ffn_pre': 3.798312e-01, 'w_up': 1.594479e-01, 'conv_w': 1.675409e-01, 'conv_b': 2.995090e-01, 'w_down': 3.237891e-01, 'g_ffn_post': 8.014222e+00}


def _to_microbatches(a, axis):
    t = _jnp.moveaxis(a, axis, 0)
    t = t.reshape((N_MICROBATCH, t.shape[0] // N_MICROBATCH) + t.shape[1:])
    return _jnp.moveaxis(t, 1, axis + 1)


def setup_inputs(seed: int = 0) -> dict:
    inp = _fwd_setup_inputs(seed)
    key = _jax.random.fold_in(_jax.random.key(seed), 7919)
    shape, _ = _output_shape()
    out = dict(inp)
    out["loss_target"] = _jax.random.normal(_jax.random.fold_in(key, 0), shape, _jnp.float32)
    for i, name in enumerate(TWIN_WEIGHTS):
        w = inp[name].astype(_jnp.float32)
        if MOMENT_SCALE is None:
            s = _jnp.sqrt(_jnp.mean(_jnp.square(w)) + 1e-30)
        else:
            s = MOMENT_SCALE[name]
        km, kv = _jax.random.split(_jax.random.fold_in(key, i + 1))
        out[name] = w
        out["m_" + name] = s * _jax.random.normal(km, w.shape, _jnp.float32)
        out["v_" + name] = (s * s) * _jax.random.uniform(kv, w.shape, _jnp.float32, 0.5, 1.5)
    if N_MICROBATCH > 1:
        for name, axis in PER_EXAMPLE_BATCH_AXIS.items():
            out[name] = _to_microbatches(out[name], axis)
    return {'x': out['x'], 'mem': out['mem'], 'g_mix_pre': out['g_mix_pre'], 'w_in': out['w_in'], 'g_vnorm': out['g_vnorm'], 'w_s': out['w_s'], 'b_s': out['b_s'], 'g_mem': out['g_mem'], 'w_mem_kv': out['w_mem_kv'], 'w_gate': out['w_gate'], 'b_gate': out['b_gate'], 'w_br_sb': out['w_br_sb'], 'w_br_gm': out['w_br_gm'], 'w_br_xa': out['w_br_xa'], 'w_out': out['w_out'], 'g_mix_post': out['g_mix_post'], 'g_ffn_pre': out['g_ffn_pre'], 'w_up': out['w_up'], 'conv_w': out['conv_w'], 'conv_b': out['conv_b'], 'w_down': out['w_down'], 'g_ffn_post': out['g_ffn_post'], 'loss_target': out['loss_target'], 'm_g_mix_pre': out['m_g_mix_pre'], 'm_w_in': out['m_w_in'], 'm_g_vnorm': out['m_g_vnorm'], 'm_w_s': out['m_w_s'], 'm_b_s': out['m_b_s'], 'm_g_mem': out['m_g_mem'], 'm_w_mem_kv': out['m_w_mem_kv'], 'm_w_gate': out['m_w_gate'], 'm_b_gate': out['m_b_gate'], 'm_w_br_sb': out['m_w_br_sb'], 'm_w_br_gm': out['m_w_br_gm'], 'm_w_br_xa': out['m_w_br_xa'], 'm_w_out': out['m_w_out'], 'm_g_mix_post': out['m_g_mix_post'], 'm_g_ffn_pre': out['m_g_ffn_pre'], 'm_w_up': out['m_w_up'], 'm_conv_w': out['m_conv_w'], 'm_conv_b': out['m_conv_b'], 'm_w_down': out['m_w_down'], 'm_g_ffn_post': out['m_g_ffn_post'], 'v_g_mix_pre': out['v_g_mix_pre'], 'v_w_in': out['v_w_in'], 'v_g_vnorm': out['v_g_vnorm'], 'v_w_s': out['v_w_s'], 'v_b_s': out['v_b_s'], 'v_g_mem': out['v_g_mem'], 'v_w_mem_kv': out['v_w_mem_kv'], 'v_w_gate': out['v_w_gate'], 'v_b_gate': out['v_b_gate'], 'v_w_br_sb': out['v_w_br_sb'], 'v_w_br_gm': out['v_w_br_gm'], 'v_w_br_xa': out['v_w_br_xa'], 'v_w_out': out['v_w_out'], 'v_g_mix_post': out['v_g_mix_post'], 'v_g_ffn_pre': out['v_g_ffn_pre'], 'v_w_up': out['v_w_up'], 'v_conv_w': out['v_conv_w'], 'v_conv_b': out['v_conv_b'], 'v_w_down': out['v_w_down'], 'v_g_ffn_post': out['v_g_ffn_post']}


def _loss(weights, diff, rest, loss_target):
    with _jax.named_scope("forward"):
        args = {**rest, TWIN_DIFF_INPUT: diff, **{k: w.astype(_WEIGHT_DTYPES[k]) for k, w in weights.items()}}
        y = _forward(args)
    with _jax.named_scope("loss_head"):
        err = _jnp.square(y.astype(_jnp.float32) - loss_target)
        return 0.5 * _jnp.sum(_jnp.mean(err, axis=-1)) if err.ndim else 0.5 * err


def _adamw(w, g, m, v):
    m = ADAM_B1 * m + (1.0 - ADAM_B1) * g
    v = ADAM_B2 * v + (1.0 - ADAM_B2) * _jnp.square(g)
    m_hat = m / (1.0 - ADAM_B1 ** ADAM_STEP)
    v_hat = v / (1.0 - ADAM_B2 ** ADAM_STEP)
    delta = -ADAM_LR * (m_hat / (_jnp.sqrt(v_hat) + ADAM_EPS) + ADAM_WD * w)
    return delta, m, v


def reference(x, mem, g_mix_pre, w_in, g_vnorm, w_s, b_s, g_mem, w_mem_kv, w_gate, b_gate, w_br_sb, w_br_gm, w_br_xa, w_out, g_mix_post, g_ffn_pre, w_up, conv_w, conv_b, w_down, g_ffn_post, loss_target, m_g_mix_pre, m_w_in, m_g_vnorm, m_w_s, m_b_s, m_g_mem, m_w_mem_kv, m_w_gate, m_b_gate, m_w_br_sb, m_w_br_gm, m_w_br_xa, m_w_out, m_g_mix_post, m_g_ffn_pre, m_w_up, m_conv_w, m_conv_b, m_w_down, m_g_ffn_post, v_g_mix_pre, v_w_in, v_g_vnorm, v_w_s, v_b_s, v_g_mem, v_w_mem_kv, v_w_gate, v_b_gate, v_w_br_sb, v_w_br_gm, v_w_br_xa, v_w_out, v_g_mix_post, v_g_ffn_pre, v_w_up, v_conv_w, v_conv_b, v_w_down, v_g_ffn_post):
    given = dict(x=x, mem=mem, g_mix_pre=g_mix_pre, w_in=w_in, g_vnorm=g_vnorm, w_s=w_s, b_s=b_s, g_mem=g_mem, w_mem_kv=w_mem_kv, w_gate=w_gate, b_gate=b_gate, w_br_sb=w_br_sb, w_br_gm=w_br_gm, w_br_xa=w_br_xa, w_out=w_out, g_mix_post=g_mix_post, g_ffn_pre=g_ffn_pre, w_up=w_up, conv_w=conv_w, conv_b=conv_b, w_down=w_down, g_ffn_post=g_ffn_post, loss_target=loss_target, m_g_mix_pre=m_g_mix_pre, m_w_in=m_w_in, m_g_vnorm=m_g_vnorm, m_w_s=m_w_s, m_b_s=m_b_s, m_g_mem=m_g_mem, m_w_mem_kv=m_w_mem_kv, m_w_gate=m_w_gate, m_b_gate=m_b_gate, m_w_br_sb=m_w_br_sb, m_w_br_gm=m_w_br_gm, m_w_br_xa=m_w_br_xa, m_w_out=m_w_out, m_g_mix_post=m_g_mix_post, m_g_ffn_pre=m_g_ffn_pre, m_w_up=m_w_up, m_conv_w=m_conv_w, m_conv_b=m_conv_b, m_w_down=m_w_down, m_g_ffn_post=m_g_ffn_post, v_g_mix_pre=v_g_mix_pre, v_w_in=v_w_in, v_g_vnorm=v_g_vnorm, v_w_s=v_w_s, v_b_s=v_b_s, v_g_mem=v_g_mem, v_w_mem_kv=v_w_mem_kv, v_w_gate=v_w_gate, v_b_gate=v_b_gate, v_w_br_sb=v_w_br_sb, v_w_br_gm=v_w_br_gm, v_w_br_xa=v_w_br_xa, v_w_out=v_w_out, v_g_mix_post=v_g_mix_post, v_g_ffn_pre=v_g_ffn_pre, v_w_up=v_w_up, v_conv_w=v_conv_w, v_conv_b=v_conv_b, v_w_down=v_w_down, v_g_ffn_post=v_g_ffn_post)
    weights = {n: given[n] for n in TWIN_WEIGHTS}
    shared = {n: given[n] for n in SHARED_INPUTS}
    per_example = {n: given[n] for n in ['x', 'mem']}
    grad_fn = _jax.value_and_grad(_loss, argnums=(0, 1))

    def one_microbatch(ex, loss_target):
        ex = dict(ex)
        diff = ex.pop(TWIN_DIFF_INPUT)
        return grad_fn(weights, diff, {**shared, **ex}, loss_target)

    if N_MICROBATCH == 1:
        loss, (grad_w, grad_x) = one_microbatch(per_example, given["loss_target"])
    else:
        def body(carry, xs):
            loss_sum, grad_sum = carry
            l_k, (gw_k, gx_k) = one_microbatch(xs[0], xs[1])
            with _jax.named_scope("update"):
                return (loss_sum + l_k, _jax.tree.map(_jnp.add, grad_sum, gw_k)), gx_k

        init = (_jnp.zeros((), _jnp.float32), _jax.tree.map(_jnp.zeros_like, weights))
        (loss, grad_w), grad_x = _jax.lax.scan(body, init, (per_example, given["loss_target"]))
    with _jax.named_scope("update"):
        delta_w, new_m, new_v = {}, {}, {}
        for n in TWIN_WEIGHTS:
            delta_w[n], new_m[n], new_v[n] = _adamw(weights[n], grad_w[n], given["m_" + n], given["v_" + n])
    return (loss, grad_x, *[grad_w[n] for n in TWIN_WEIGHTS], *[delta_w[n] for n in TWIN_WEIGHTS],
            *[new_m[n] for n in TWIN_WEIGHTS], *[new_v[n] for n in TWIN_WEIGHTS])
```

```python
import functools
import math

import jax
import jax.numpy as jnp
from jax import lax
from jax.experimental import pallas as pl
from jax.experimental.pallas import tpu as pltpu

F32 = jnp.float32
BF16 = jnp.bfloat16
EPS = 1e-6
HEAD_SB = 128
GROUP_GM = 128
CHUNK = 64
HEAD_XA = 256
CONV_TAPS = 3
N_CHIPS = 4
N_DEV = 8
LANES = 128
MIB = 1024 * 1024
VMEM_LIMIT = 48 * MIB
SPLITS = 3

ADAM_LR = 0.001
ADAM_B1 = 0.9
ADAM_B2 = 0.999
ADAM_EPS = 1e-08
ADAM_WD = 0.01
ADAM_STEP = 10

WEIGHTS = ['g_mix_pre', 'w_in', 'g_vnorm', 'w_s', 'b_s', 'g_mem', 'w_mem_kv', 'w_gate', 'b_gate', 'w_br_sb',
           'w_br_gm', 'w_br_xa', 'w_out', 'g_mix_post', 'g_ffn_pre', 'w_up', 'conv_w', 'conv_b', 'w_down',
           'g_ffn_post']
BIG_AXIS = {'w_in': 2, 'w_mem_kv': 1, 'w_gate': 2, 'w_br_sb': 2, 'w_br_gm': 2, 'w_br_xa': 2, 'w_out': 1,
            'w_up': 2, 'w_down': 1}
BIG = list(BIG_AXIS)
SMALL = ['g_mix_pre', 'g_vnorm', 'w_s', 'b_s', 'g_mem', 'b_gate', 'g_mix_post', 'g_ffn_pre', 'conv_b', 'g_ffn_post']
MESH = pl.DeviceIdType.MESH


def _pcall(body, **kw):
    return pl.pallas_call(body, **kw)


def _params(sem=None, vmem=VMEM_LIMIT):
    return pltpu.CompilerParams(dimension_semantics=sem, vmem_limit_bytes=vmem)


def _tile(n, cands):
    for c in cands:
        if n % c == 0:
            return c
    return n


_GELU_C = math.sqrt(2.0 / math.pi)
_GELU_A = 0.044715


def _gelu(x):
    return 0.5 * x * (1.0 + jnp.tanh(_GELU_C * (x + _GELU_A * (x * x * x))))


def _gelu_and_grad(x):
    x2 = x * x
    t = jnp.tanh(_GELU_C * (x + _GELU_A * (x2 * x)))
    val = 0.5 * x * (1.0 + t)
    grad = 0.5 * (1.0 + t) + 0.5 * x * (1.0 - t * t) * (_GELU_C * (1.0 + 3.0 * _GELU_A * x2))
    return val, grad


def _softplus(z):
    return jnp.maximum(z, 0.0) + jnp.log1p(jnp.exp(-jnp.abs(z)))


def _dot(a, b):
    return jnp.dot(a, b, preferred_element_type=F32)


def _dot_nt(a, b):
    return lax.dot_general(a, b, (((1,), (1,)), ((), ())), preferred_element_type=F32)


def _dot_tn(a, b):
    return lax.dot_general(a, b, (((0,), (0,)), ((), ())), preferred_element_type=F32)


def _split_dot(a, m):
    out = None
    rest = a
    for _ in range(SPLITS):
        piece = rest.astype(BF16)
        rest = rest - piece.astype(F32)
        term = _dot(piece, m)
        out = term if out is None else out + term
    return out


def _mm(a, b, mode, out_dtype, name, tm=None, tn=None, tk=None):
    if mode == 'nn':
        (m, kc), (kc2, n) = a.shape, b.shape
    elif mode == 'nt':
        (m, kc), (n, kc2) = a.shape, b.shape
    else:
        (kc, m), (kc2, n) = a.shape, b.shape
    assert kc == kc2, (a.shape, b.shape, mode)
    tm = tm or _tile(m, (1024, 512, 256, 128))
    tn = tn or _tile(n, (1024, 512, 256, 128))
    tk = tk or (kc if kc <= 2048 else _tile(kc, (2048, 1536, 1408, 1024, 512)))
    nk = kc // tk
    dot = {'nn': _dot, 'nt': _dot_nt, 'tn': _dot_tn}[mode]
    a_spec = pl.BlockSpec((tk, tm), lambda i, j, k: (k, i)) if mode == 'tn' else pl.BlockSpec((tm, tk), lambda i, j, k: (i, k))
    b_spec = pl.BlockSpec((tn, tk), lambda i, j, k: (j, k)) if mode == 'nt' else pl.BlockSpec((tk, tn), lambda i, j, k: (k, j))

    if nk == 1:
        def body(a_ref, b_ref, o_ref):
            o_ref[...] = dot(a_ref[...].astype(BF16), b_ref[...].astype(BF16)).astype(o_ref.dtype)
        scratch = []
    else:
        def body(a_ref, b_ref, o_ref, acc_ref):
            k = pl.program_id(2)
            part = dot(a_ref[...].astype(BF16), b_ref[...].astype(BF16))

            @pl.when(k == 0)
            def _():
                acc_ref[...] = part

            @pl.when(k > 0)
            def _():
                acc_ref[...] += part

            @pl.when(k == nk - 1)
            def _():
                o_ref[...] = acc_ref[...].astype(o_ref.dtype)
        scratch = [pltpu.VMEM((tm, tn), F32)]

    return _pcall(
        body, grid=(m // tm, n // tn, nk), in_specs=[a_spec, b_spec],
        out_specs=pl.BlockSpec((tm, tn), lambda i, j, k: (i, j)),
        out_shape=jax.ShapeDtypeStruct((m, n), out_dtype), scratch_shapes=scratch, name=name,
        compiler_params=_params(("parallel", "parallel", "arbitrary")))(a, b)


def _norm_fwd(x, g, res, out_dtype, name):
    s, d = x.shape
    tr = _tile(s, (256, 128))
    has_res = res is not None

    def body(*refs):
        x_ref, g_ref = refs[0], refs[1]
        o_ref = refs[-1]
        xv = x_ref[...]
        y = xv * lax.rsqrt(jnp.mean(xv * xv, axis=-1, keepdims=True) + EPS) * g_ref[...]
        if has_res:
            y = y + refs[2][...]
        o_ref[...] = y.astype(o_ref.dtype)

    row = pl.BlockSpec((tr, d), lambda i: (i, 0))
    ins = [x, g] + ([res] if has_res else [])
    return _pcall(
        body, grid=(s // tr,), in_specs=[row, pl.BlockSpec((1, d), lambda i: (0, 0))] + ([row] if has_res else []),
        out_specs=row, out_shape=jax.ShapeDtypeStruct((s, d), out_dtype), name=name,
        compiler_params=_params(("parallel",)))(*ins)


def _norm_bwd(x, g, douts, dres, out_dtype, name):
    s, d = x.shape
    tr = _tile(s, (256, 128))
    nd = len(douts)
    has_res = dres is not None

    def body(*refs):
        x_ref, g_ref = refs[0], refs[1]
        dx_ref, dg_ref = refs[-2], refs[-1]
        dout = refs[2][...].astype(F32)
        for r in refs[3:2 + nd]:
            dout = dout + r[...].astype(F32)
        xv = x_ref[...]
        r = lax.rsqrt(jnp.mean(xv * xv, axis=-1, keepdims=True) + EPS)
        n = xv * r
        dn = dout * g_ref[...]
        dx = r * (dn - n * jnp.mean(dn * n, axis=-1, keepdims=True))
        if has_res:
            dx = dx + refs[2 + nd][...]
        dx_ref[...] = dx.astype(dx_ref.dtype)

        @pl.when(pl.program_id(0) == 0)
        def _():
            dg_ref[...] = jnp.zeros_like(dg_ref)

        dg_ref[...] += jnp.sum(dout * n, axis=0, keepdims=True)

    row = pl.BlockSpec((tr, d), lambda i: (i, 0))
    vec = pl.BlockSpec((1, d), lambda i: (0, 0))
    ins = [x, g] + list(douts) + ([dres] if has_res else [])
    return _pcall(
        body, grid=(s // tr,), in_specs=[row, vec] + [row] * (nd + int(has_res)), out_specs=[row, vec],
        out_shape=[jax.ShapeDtypeStruct((s, d), out_dtype), jax.ShapeDtypeStruct((1, d), F32)], name=name,
        compiler_params=_params(("arbitrary",)))(*ins)


def _loss_head(y, target):
    s, d = y.shape
    tr = _tile(s, (256, 128))

    def body(y_ref, t_ref, sq_ref, dy_ref):
        e = y_ref[...] - t_ref[...]
        dy_ref[...] = e * (1.0 / d)

        @pl.when(pl.program_id(0) == 0)
        def _():
            sq_ref[...] = jnp.zeros_like(sq_ref)

        sq_ref[...] += jnp.sum(e * e, axis=0, keepdims=True)

    row = pl.BlockSpec((tr, d), lambda i: (i, 0))
    return _pcall(
        body, grid=(s // tr,), in_specs=[row, row], out_specs=[pl.BlockSpec((1, d), lambda i: (0, 0)), row],
        out_shape=[jax.ShapeDtypeStruct((1, d), F32), jax.ShapeDtypeStruct((s, d), F32)], name="loss_head",
        compiler_params=_params(("arbitrary",)))(y, target)


def _sb_masks():
    row = lax.broadcasted_iota(jnp.int32, (HEAD_SB, HEAD_SB), 0)
    col = lax.broadcasted_iota(jnp.int32, (HEAD_SB, HEAD_SB), 1)
    return row, col


def _sb_fwd(proj, n_heads):
    s = proj.shape[0]
    nq = s // HEAD_SB
    scale = HEAD_SB ** -0.5

    def body(q_ref, k_ref, v_ref, o_ref):
        i = pl.program_id(1)
        q = q_ref[...].astype(BF16)
        row, col = _sb_masks()
        m_strict = (row > col).astype(BF16)

        def step(jj, carry):
            c, acc = carry
            off = pl.multiple_of((i - jj) * HEAD_SB, HEAD_SB)
            kb = k_ref[pl.ds(off, HEAD_SB), :].astype(BF16)
            vb = v_ref[pl.ds(off, HEAD_SB), :].astype(BF16)
            z = _dot_nt(q, kb) * scale
            mask = (off + col) < (i * HEAD_SB + row)
            sp = _softplus(z)
            lk = jnp.where(mask, -sp, 0.0)
            suffix = _split_dot(lk, m_strict) + c
            a = jnp.where(mask, jnp.exp(z - sp + suffix), 0.0)
            acc = acc + _dot(a.astype(BF16), vb)
            return c + jnp.sum(lk, axis=1, keepdims=True), acc

        _, acc = lax.fori_loop(0, i + 1, step, (jnp.zeros((HEAD_SB, 1), F32), jnp.zeros((HEAD_SB, HEAD_SB), F32)))
        o_ref[...] = acc.astype(BF16)

    h = n_heads
    blk = pl.BlockSpec((HEAD_SB, HEAD_SB), lambda hh, i: (i, hh))
    return _pcall(
        body, grid=(h, nq),
        in_specs=[blk, pl.BlockSpec((s, HEAD_SB), lambda hh, i: (0, h + hh)),
                  pl.BlockSpec((s, HEAD_SB), lambda hh, i: (0, 2 * h + hh))],
        out_specs=blk, out_shape=jax.ShapeDtypeStruct((s, h * HEAD_SB), BF16),
        name="sb_fwd", compiler_params=_params(("parallel", "arbitrary")))(proj, proj, proj)


def _sb_bwd(proj, do, n_heads):
    s = proj.shape[0]
    nq = s // HEAD_SB
    scale = HEAD_SB ** -0.5

    def body(q_ref, k_ref, v_ref, do_ref, dq_ref, dk_ref, dv_ref, g_ref, beta_ref):
        i = pl.program_id(1)

        @pl.when(i == 0)
        def _():
            dk_ref[...] = jnp.zeros_like(dk_ref)
            dv_ref[...] = jnp.zeros_like(dv_ref)

        q = q_ref[...].astype(BF16)
        dob = do_ref[...].astype(BF16)
        row, col = _sb_masks()
        m_strict = (row > col).astype(BF16)
        m_before = (row < col).astype(BF16)

        def sweep_left(jj, c):
            j = i - jj
            off = pl.multiple_of(j * HEAD_SB, HEAD_SB)
            kb = k_ref[pl.ds(off, HEAD_SB), :].astype(BF16)
            vb = v_ref[pl.ds(off, HEAD_SB), :].astype(BF16)
            z = _dot_nt(q, kb) * scale
            mask = (off + col) < (i * HEAD_SB + row)
            sp = _softplus(z)
            lk = jnp.where(mask, -sp, 0.0)
            suffix = _split_dot(lk, m_strict) + c
            beta = jnp.exp(z - sp)
            a = jnp.where(mask, beta * jnp.exp(suffix), 0.0)
            g_ref[j] = a * _dot_nt(dob, vb)
            beta_ref[j] = beta
            dv_ref[pl.ds(off, HEAD_SB), :] += _dot_tn(a.astype(BF16), dob)
            return c + jnp.sum(lk, axis=1, keepdims=True)

        lax.fori_loop(0, i + 1, sweep_left, jnp.zeros((HEAD_SB, 1), F32))

        def sweep_right(j, carry):
            p, dq = carry
            off = pl.multiple_of(j * HEAD_SB, HEAD_SB)
            kb = k_ref[pl.ds(off, HEAD_SB), :].astype(BF16)
            g = g_ref[j]
            beta = beta_ref[j]
            mask = (off + col) < (i * HEAD_SB + row)
            before = _split_dot(g, m_before) + p
            dz = jnp.where(mask, g * (1.0 - beta) - beta * before, 0.0) * scale
            dzb = dz.astype(BF16)
            dk_ref[pl.ds(off, HEAD_SB), :] += _dot_tn(dzb, q)
            return p + jnp.sum(g, axis=1, keepdims=True), dq + _dot(dzb, kb)

        _, dq = lax.fori_loop(0, i + 1, sweep_right, (jnp.zeros((HEAD_SB, 1), F32), jnp.zeros((HEAD_SB, HEAD_SB), F32)))
        dq_ref[...] = dq

    h = n_heads
    blk = pl.BlockSpec((HEAD_SB, HEAD_SB), lambda hh, i: (i, hh))
    col_blk = pl.BlockSpec((s, HEAD_SB), lambda hh, i: (0, hh))
    shape = jax.ShapeDtypeStruct((s, h * HEAD_SB), F32)
    return _pcall(
        body, grid=(h, nq),
        in_specs=[blk, pl.BlockSpec((s, HEAD_SB), lambda hh, i: (0, h + hh)),
                  pl.BlockSpec((s, HEAD_SB), lambda hh, i: (0, 2 * h + hh)), blk],
        out_specs=[blk, col_blk, col_blk], out_shape=[shape, shape, shape],
        scratch_shapes=[pltpu.VMEM((nq, HEAD_SB, HEAD_SB), F32), pltpu.VMEM((nq, HEAD_SB, HEAD_SB), F32)],
        name="sb_bwd", compiler_params=_params(("parallel", "arbitrary")))(proj, proj, proj, do)


def _gm_mask():
    t = lax.broadcasted_iota(jnp.int32, (GROUP_GM, GROUP_GM), 0)
    s = lax.broadcasted_iota(jnp.int32, (GROUP_GM, GROUP_GM), 1)
    shift = CHUNK.bit_length() - 1
    return (s >> shift) <= (t >> shift)


def _gm_fwd(proj, g_vnorm, w_s, b_st, u_blk):
    s = proj.shape[0]
    groups = w_s.shape[0]
    w = groups * GROUP_GM

    def body(u_ref, v_ref, gv_ref, ws_ref, bst_ref, o_ref):
        ug = _gelu(u_ref[...])
        vg = _gelu(v_ref[...])
        vn = vg * lax.rsqrt(jnp.mean(vg * vg, axis=-1, keepdims=True) + EPS) * gv_ref[...]
        vnb = vn.astype(BF16)
        mask = _gm_mask()
        for g in range(groups):
            sl = slice(g * GROUP_GM, (g + 1) * GROUP_GM)
            wm = jnp.where(mask, ws_ref[g], 0.0).astype(BF16)
            mixed = _dot(wm, vnb[:, sl]) + bst_ref[:, g:g + 1]
            o_ref[:, sl] = (ug[:, sl] * mixed).astype(o_ref.dtype)

    return _pcall(
        body, grid=(s // GROUP_GM,),
        in_specs=[pl.BlockSpec((GROUP_GM, w), lambda c: (c, u_blk)), pl.BlockSpec((GROUP_GM, w), lambda c: (c, u_blk + 1)),
                  pl.BlockSpec((1, w), lambda c: (0, 0)), pl.BlockSpec((groups, GROUP_GM, GROUP_GM), lambda c: (0, 0, 0)),
                  pl.BlockSpec((GROUP_GM, groups), lambda c: (0, 0))],
        out_specs=pl.BlockSpec((GROUP_GM, w), lambda c: (c, 0)),
        out_shape=jax.ShapeDtypeStruct((s, w), BF16), name="gm_fwd",
        compiler_params=_params(("parallel",)))(proj, proj, g_vnorm, w_s, b_st)


def _gm_bwd(proj, g_vnorm, w_s, b_st, do, u_blk):
    s = proj.shape[0]
    groups = w_s.shape[0]
    w = groups * GROUP_GM

    def body(u_ref, v_ref, gv_ref, ws_ref, bst_ref, do_ref, du_ref, dv_ref, dgv_ref, dws_ref, dbst_ref, dvn_ref):
        @pl.when(pl.program_id(0) == 0)
        def _():
            dgv_ref[...] = jnp.zeros_like(dgv_ref)
            dws_ref[...] = jnp.zeros_like(dws_ref)
            dbst_ref[...] = jnp.zeros_like(dbst_ref)

        ug, ugrad = _gelu_and_grad(u_ref[...])
        vg, vgrad = _gelu_and_grad(v_ref[...])
        r = lax.rsqrt(jnp.mean(vg * vg, axis=-1, keepdims=True) + EPS)
        n = vg * r
        gv = gv_ref[...]
        vnb = (n * gv).astype(BF16)
        dout = do_ref[...]
        mask = _gm_mask()
        for g in range(groups):
            sl = slice(g * GROUP_GM, (g + 1) * GROUP_GM)
            wm = jnp.where(mask, ws_ref[g], 0.0).astype(BF16)
            mixed = _dot(wm, vnb[:, sl]) + bst_ref[:, g:g + 1]
            dmixed = dout[:, sl] * ug[:, sl]
            du_ref[:, sl] = dout[:, sl] * mixed * ugrad[:, sl]
            dbst_ref[:, g:g + 1] += jnp.sum(dmixed, axis=1, keepdims=True)
            dmb = dmixed.astype(BF16)
            dws_ref[g] += jnp.where(mask, _dot_nt(dmb, vnb[:, sl]), 0.0)
            dvn_ref[:, sl] = _dot_tn(wm, dmb)
        dvn = dvn_ref[...]
        dgv_ref[...] += jnp.sum(dvn * n, axis=0, keepdims=True)
        dn = dvn * gv
        dvg = r * (dn - n * jnp.mean(dn * n, axis=-1, keepdims=True))
        dv_ref[...] = dvg * vgrad

    rowb = pl.BlockSpec((GROUP_GM, w), lambda c: (c, 0))
    vec = pl.BlockSpec((1, w), lambda c: (0, 0))
    wsb = pl.BlockSpec((groups, GROUP_GM, GROUP_GM), lambda c: (0, 0, 0))
    bsb = pl.BlockSpec((GROUP_GM, groups), lambda c: (0, 0))
    return _pcall(
        body, grid=(s // GROUP_GM,),
        in_specs=[pl.BlockSpec((GROUP_GM, w), lambda c: (c, u_blk)), pl.BlockSpec((GROUP_GM, w), lambda c: (c, u_blk + 1)),
                  vec, wsb, bsb, rowb],
        out_specs=[rowb, rowb, vec, wsb, bsb],
        out_shape=[jax.ShapeDtypeStruct((s, w), F32), jax.ShapeDtypeStruct((s, w), F32), jax.ShapeDtypeStruct((1, w), F32),
                   jax.ShapeDtypeStruct((groups, GROUP_GM, GROUP_GM), F32), jax.ShapeDtypeStruct((GROUP_GM, groups), F32)],
        scratch_shapes=[pltpu.VMEM((GROUP_GM, w), F32)], name="gm_bwd",
        compiler_params=_params(("arbitrary",)))(proj, proj, g_vnorm, w_s, b_st, do)


def _xa_fwd(proj, mem_kv, q_blk, n_heads):
    s = proj.shape[0]
    nm = mem_kv.shape[0]
    tq = _tile(s, (512, 256, 128))
    scale = HEAD_XA ** -0.5

    def body(q_ref, k_ref, v_ref, o_ref):
        z = _dot_nt(q_ref[...].astype(BF16), k_ref[...].astype(BF16)) * scale
        z = z - jnp.max(z, axis=-1, keepdims=True)
        e = jnp.exp(z)
        p = e / jnp.sum(e, axis=-1, keepdims=True)
        o_ref[...] = _dot(p.astype(BF16), v_ref[...].astype(BF16)).astype(o_ref.dtype)

    h = n_heads
    return _pcall(
        body, grid=(h, s // tq),
        in_specs=[pl.BlockSpec((tq, HEAD_XA), lambda hh, i: (i, q_blk + hh)),
                  pl.BlockSpec((nm, HEAD_XA), lambda hh, i: (0, hh)), pl.BlockSpec((nm, HEAD_XA), lambda hh, i: (0, h + hh))],
        out_specs=pl.BlockSpec((tq, HEAD_XA), lambda hh, i: (i, hh)),
        out_shape=jax.ShapeDtypeStruct((s, h * HEAD_XA), BF16), name="xa_fwd",
        compiler_params=_params(("parallel", "parallel")))(proj, mem_kv, mem_kv)


def _xa_bwd(proj, mem_kv, do, q_blk, n_heads):
    s = proj.shape[0]
    nm = mem_kv.shape[0]
    tq = _tile(s, (512, 256, 128))
    scale = HEAD_XA ** -0.5
    h = n_heads

    def body(q_ref, k_ref, v_ref, do_ref, dq_ref, dk_ref, dv_ref):
        @pl.when(pl.program_id(1) == 0)
        def _():
            dk_ref[...] = jnp.zeros_like(dk_ref)
            dv_ref[...] = jnp.zeros_like(dv_ref)

        qb = q_ref[...].astype(BF16)
        kb = k_ref[...].astype(BF16)
        vb = v_ref[...].astype(BF16)
        dob = do_ref[...].astype(BF16)
        z = _dot_nt(qb, kb) * scale
        z = z - jnp.max(z, axis=-1, keepdims=True)
        e = jnp.exp(z)
        p = e / jnp.sum(e, axis=-1, keepdims=True)
        dp = _dot_nt(dob, vb)
        dz = (p * (dp - jnp.sum(dp * p, axis=-1, keepdims=True)) * scale).astype(BF16)
        dq_ref[...] = _dot(dz, kb)
        dk_ref[...] += _dot_tn(dz, qb)
        dv_ref[...] += _dot_tn(p.astype(BF16), dob)

    qspec = pl.BlockSpec((tq, HEAD_XA), lambda hh, i: (i, hh))
    dk, dv = None, None
    dq, dk, dv = _pcall(
        body, grid=(h, s // tq),
        in_specs=[pl.BlockSpec((tq, HEAD_XA), lambda hh, i: (i, q_blk + hh)),
                  pl.BlockSpec((nm, HEAD_XA), lambda hh, i: (0, hh)), pl.BlockSpec((nm, HEAD_XA), lambda hh, i: (0, h + hh)),
                  qspec],
        out_specs=[qspec, pl.BlockSpec((nm, HEAD_XA), lambda hh, i: (0, hh)), pl.BlockSpec((nm, HEAD_XA), lambda hh, i: (0, hh))],
        out_shape=[jax.ShapeDtypeStruct((s, h * HEAD_XA), F32), jax.ShapeDtypeStruct((nm, h * HEAD_XA), F32),
                   jax.ShapeDtypeStruct((nm, h * HEAD_XA), F32)],
        name="xa_bwd", compiler_params=_params(("parallel", "arbitrary")))(proj, mem_kv, mem_kv, do)
    return dq, dk, dv


def _merge_fwd(zg, b_gate, branches):
    s, d = branches[0].shape
    tr = _tile(s, (128,))

    def body(z0, z1, z2, g0, g1, g2, b0, b1, b2, o_ref):
        acc = None
        for z, g, b in ((z0, g0, b0), (z1, g1, b1), (z2, g2, b2)):
            term = jax.nn.sigmoid(z[...] + g[...]) * b[...]
            acc = term if acc is None else acc + term
        o_ref[...] = acc.astype(o_ref.dtype)

    zs = [pl.BlockSpec((tr, d), functools.partial(lambda i, k: (i, k), k=k)) for k in range(3)]
    gs = [pl.BlockSpec((1, d), functools.partial(lambda i, k: (0, k), k=k)) for k in range(3)]
    row = pl.BlockSpec((tr, d), lambda i: (i, 0))
    return _pcall(
        body, grid=(s // tr,), in_specs=zs + gs + [row] * 3, out_specs=row,
        out_shape=jax.ShapeDtypeStruct((s, d), BF16), name="merge_fwd",
        compiler_params=_params(("parallel",)))(zg, zg, zg, b_gate, b_gate, b_gate, *branches)


def _merge_bwd(zg, b_gate, branches, dmerged):
    s, d = branches[0].shape
    tr = _tile(s, (128,))

    def body(z0, z1, z2, g0, g1, g2, b0, b1, b2, dm_ref, dz_ref, d0, d1, d2, dbg_ref):
        @pl.when(pl.program_id(0) == 0)
        def _():
            dbg_ref[...] = jnp.zeros_like(dbg_ref)

        dm = dm_ref[...]
        for k, (z, g, b, dbr) in enumerate(((z0, g0, b0, d0), (z1, g1, b1, d1), (z2, g2, b2, d2))):
            sg = jax.nn.sigmoid(z[...] + g[...])
            dbr[...] = (dm * sg).astype(dbr.dtype)
            dz = dm * b[...] * sg * (1.0 - sg)
            dz_ref[:, k * d:(k + 1) * d] = dz.astype(dz_ref.dtype)
            dbg_ref[:, k * d:(k + 1) * d] += jnp.sum(dz, axis=0, keepdims=True)

    zs = [pl.BlockSpec((tr, d), functools.partial(lambda i, k: (i, k), k=k)) for k in range(3)]
    gs = [pl.BlockSpec((1, d), functools.partial(lambda i, k: (0, k), k=k)) for k in range(3)]
    row = pl.BlockSpec((tr, d), lambda i: (i, 0))
    outs = _pcall(
        body, grid=(s // tr,), in_specs=zs + gs + [row] * 4,
        out_specs=[pl.BlockSpec((tr, 3 * d), lambda i: (i, 0)), row, row, row, pl.BlockSpec((1, 3 * d), lambda i: (0, 0))],
        out_shape=[jax.ShapeDtypeStruct((s, 3 * d), BF16)] + [jax.ShapeDtypeStruct((s, d), BF16)] * 3
        + [jax.ShapeDtypeStruct((1, 3 * d), F32)],
        name="merge_bwd", compiler_params=_params(("arbitrary",)))(zg, zg, zg, b_gate, b_gate, b_gate, *branches, dmerged)
    return outs[0], list(outs[1:4]), outs[4]


def _shift_down(x, k, row):
    return jnp.where(row >= k, pltpu.roll(x, k, 0), 0.0)


def _shift_up(x, k, row, s):
    return jnp.where(row < s - k, pltpu.roll(x, s - k, 0), 0.0)


def _conv_pre(gate, cw_ref, cb_ref, row):
    conv = cb_ref[...] + cw_ref[CONV_TAPS - 1:CONV_TAPS, :] * gate
    for k in range(1, CONV_TAPS):
        conv = conv + cw_ref[CONV_TAPS - 1 - k:CONV_TAPS - k, :] * _shift_down(gate, k, row)
    return conv


def _cg_fwd(up, conv_w, conv_b):
    s = up.shape[0]
    f = conv_w.shape[1]
    tc = _tile(f, (256, 128))
    nb = f // tc

    def body(g_ref, v_ref, cw_ref, cb_ref, o_ref):
        row = lax.broadcasted_iota(jnp.int32, (s, tc), 0)
        conv = _conv_pre(g_ref[...], cw_ref, cb_ref, row)
        o_ref[...] = (_gelu(conv) * v_ref[...]).astype(o_ref.dtype)

    return _pcall(
        body, grid=(nb,),
        in_specs=[pl.BlockSpec((s, tc), lambda j: (0, j)), pl.BlockSpec((s, tc), lambda j: (0, nb + j)),
                  pl.BlockSpec((CONV_TAPS, tc), lambda j: (0, j)), pl.BlockSpec((1, tc), lambda j: (0, j))],
        out_specs=pl.BlockSpec((s, tc), lambda j: (0, j)),
        out_shape=jax.ShapeDtypeStruct((s, f), BF16), name="cg_fwd",
        compiler_params=_params(("parallel",)))(up, up, conv_w, conv_b)


def _cg_bwd(up, conv_w, conv_b, dact):
    s = up.shape[0]
    f = conv_w.shape[1]
    tc = _tile(f, (256, 128))
    nb = f // tc

    def body(g_ref, v_ref, cw_ref, cb_ref, da_ref, dg_ref, dv_ref, dcw_ref, dcb_ref):
        row = lax.broadcasted_iota(jnp.int32, (s, tc), 0)
        gate = g_ref[...]
        conv = _conv_pre(gate, cw_ref, cb_ref, row)
        gel, ggrad = _gelu_and_grad(conv)
        da = da_ref[...]
        dv_ref[...] = (da * gel).astype(dv_ref.dtype)
        dconv = da * v_ref[...] * ggrad
        dgate = cw_ref[CONV_TAPS - 1:CONV_TAPS, :] * dconv
        dcw_ref[CONV_TAPS - 1:CONV_TAPS, :] = jnp.sum(dconv * gate, axis=0, keepdims=True)
        for k in range(1, CONV_TAPS):
            dgate = dgate + cw_ref[CONV_TAPS - 1 - k:CONV_TAPS - k, :] * _shift_up(dconv, k, row, s)
            dcw_ref[CONV_TAPS - 1 - k:CONV_TAPS - k, :] = jnp.sum(dconv * _shift_down(gate, k, row), axis=0, keepdims=True)
        dg_ref[...] = dgate.astype(dg_ref.dtype)
        dcb_ref[...] = jnp.sum(dconv, axis=0, keepdims=True)

    colb = pl.BlockSpec((s, tc), lambda j: (0, j))
    return _pcall(
        body, grid=(nb,),
        in_specs=[colb, pl.BlockSpec((s, tc), lambda j: (0, nb + j)), pl.BlockSpec((CONV_TAPS, tc), lambda j: (0, j)),
                  pl.BlockSpec((1, tc), lambda j: (0, j)), colb],
        out_specs=[colb, colb, pl.BlockSpec((CONV_TAPS, tc), lambda j: (0, j)), pl.BlockSpec((1, tc), lambda j: (0, j))],
        out_shape=[jax.ShapeDtypeStruct((s, f), BF16), jax.ShapeDtypeStruct((s, f), BF16),
                   jax.ShapeDtypeStruct((CONV_TAPS, f), F32), jax.ShapeDtypeStruct((1, f), F32)],
        name="cg_bwd", compiler_params=_params(("parallel",)))(up, up, conv_w, conv_b, dact)


def _row_tile(rows, cols):
    want = max(8, (256 * 1024) // cols)
    for c in (512, 256, 128, 64, 32, 16, 8):
        if c <= want and rows % c == 0:
            return c
    return rows


def _sum_chips(recv, name):
    _, nl, r, c = recv.shape
    tr = _row_tile(r, c)

    def body(r_ref, o_ref):
        acc = r_ref[0, 0].astype(F32)
        for k in range(1, N_CHIPS):
            acc = acc + r_ref[k, 0].astype(F32)
        o_ref[0] = acc

    return _pcall(
        body, grid=(nl, r // tr), in_specs=[pl.BlockSpec((N_CHIPS, 1, tr, c), lambda l, i: (0, l, i, 0))],
        out_specs=pl.BlockSpec((1, tr, c), lambda l, i: (l, i, 0)),
        out_shape=jax.ShapeDtypeStruct((nl, r, c), F32), name=name,
        compiler_params=_params(("parallel", "parallel")))(recv)


def _adamw(w, m, v, grads, name):
    nl, r, c = w.shape
    tr = _row_tile(r, c)
    ng = len(grads)
    c1 = 1.0 - ADAM_B1 ** ADAM_STEP
    c2 = 1.0 - ADAM_B2 ** ADAM_STEP

    def body(*refs):
        w_ref, m_ref, v_ref = refs[:3]
        g_ref, d_ref, nm_ref, nv_ref = refs[3 + ng:]
        g = refs[3][...]
        for gr in refs[4:3 + ng]:
            g = g + gr[...]
        mm = ADAM_B1 * m_ref[...] + (1.0 - ADAM_B1) * g
        vv = ADAM_B2 * v_ref[...] + (1.0 - ADAM_B2) * (g * g)
        g_ref[...] = g
        nm_ref[...] = mm
        nv_ref[...] = vv
        d_ref[...] = -ADAM_LR * ((mm / c1) / (jnp.sqrt(vv / c2) + ADAM_EPS) + ADAM_WD * w_ref[...])

    blk = pl.BlockSpec((1, tr, c), lambda l, i: (l, i, 0))
    shape = jax.ShapeDtypeStruct((nl, r, c), F32)
    return _pcall(
        body, grid=(nl, r // tr), in_specs=[blk] * (3 + ng), out_specs=[blk] * 4, out_shape=[shape] * 4, name=name,
        compiler_params=_params(("parallel", "parallel")))(w, m, v, *grads)


HBM_SPEC = pl.BlockSpec(memory_space=pltpu.HBM)


def _position():
    x, y, c = lax.axis_index("x"), lax.axis_index("y"), lax.axis_index("c")
    chips = [(1 - x, y), (x, 1 - y), (1 - x, 1 - y)]
    return x, y, c, chips


def _cut(ref, axis, j, size):
    start = pl.multiple_of(j * size, size)
    return ref.at[pl.ds(start, size), :] if axis == 0 else ref.at[:, pl.ds(start, size)]


def _all_gather(shards, axes):
    n = len(shards)
    out_shapes, index = [], []
    for t, (sh, ax) in enumerate(zip(shards, axes)):
        nl, r, c = sh.shape
        full = (r * N_CHIPS, c) if ax == 1 else (r, c * N_CHIPS)
        for l in range(nl):
            index.append((t, l, ax - 1, r if ax == 1 else c))
            out_shapes.append(jax.ShapeDtypeStruct(full, sh.dtype))
    n_out = len(out_shapes)

    def body(*refs):
        ins, outs = refs[:n], refs[n:n + n_out]
        send_sems, recv_sems, local_sems = refs[n + n_out:]
        x, y, c, chips = _position()
        me = 2 * x + y
        local, remote = [], []
        for o, (t, l, ax, size) in enumerate(index):
            src = ins[t].at[l]
            mine = _cut(outs[o], ax, me, size)
            local.append(pltpu.make_async_copy(src, mine, local_sems.at[o]))
            local[-1].start()
            for p, (px, py) in enumerate(chips):
                k = o * 3 + p
                remote.append(pltpu.make_async_remote_copy(src_ref=src, dst_ref=mine, send_sem=send_sems.at[k],
                                                           recv_sem=recv_sems.at[k], device_id=(px, py, c), device_id_type=MESH))
                remote[-1].start()
        for o, (t, l, ax, size) in enumerate(index):
            src = ins[t].at[l]
            for p, (px, py) in enumerate(chips):
                k = o * 3 + p
                theirs = _cut(outs[o], ax, 2 * px + py, size)
                pltpu.make_async_remote_copy(src_ref=src, dst_ref=theirs, send_sem=send_sems.at[k],
                                             recv_sem=recv_sems.at[k], device_id=(px, py, c), device_id_type=MESH).wait_recv()
        for cp in remote:
            cp.wait_send()
        for cp in local:
            cp.wait()

    return _pcall(
        body, in_specs=[HBM_SPEC] * n, out_specs=[HBM_SPEC] * n_out, out_shape=out_shapes,
        scratch_shapes=[pltpu.SemaphoreType.DMA((3 * n_out,)), pltpu.SemaphoreType.DMA((3 * n_out,)),
                        pltpu.SemaphoreType.DMA((n_out,))],
        name="all_gather_weights", compiler_params=pltpu.CompilerParams(has_side_effects=True))(*shards)


def _scatter_grads(grads, axes):
    n = len(grads)
    flat, index, out_shapes = [], [], []
    for t, (gl, ax) in enumerate(zip(grads, axes)):
        r, c = gl[0].shape
        shard = (r // N_CHIPS, c) if ax == 1 else (r, c // N_CHIPS)
        out_shapes.append(jax.ShapeDtypeStruct((N_CHIPS, len(gl)) + shard, gl[0].dtype))
        for l, g in enumerate(gl):
            index.append((t, l, ax - 1, shard[ax - 1]))
            flat.append(g)
    n_in = len(flat)

    def body(*refs):
        ins, outs = refs[:n_in], refs[n_in:n_in + n]
        send_sems, recv_sems, local_sems = refs[n_in + n:]
        x, y, c, chips = _position()
        me = 2 * x + y
        local, remote = [], []
        for o, (t, l, ax, size) in enumerate(index):
            local.append(pltpu.make_async_copy(_cut(ins[o], ax, me, size), outs[t].at[me, l], local_sems.at[o]))
            local[-1].start()
            for p, (px, py) in enumerate(chips):
                k = o * 3 + p
                remote.append(pltpu.make_async_remote_copy(src_ref=_cut(ins[o], ax, 2 * px + py, size), dst_ref=outs[t].at[me, l],
                                                           send_sem=send_sems.at[k], recv_sem=recv_sems.at[k],
                                                           device_id=(px, py, c), device_id_type=MESH))
                remote[-1].start()
        for o, (t, l, ax, size) in enumerate(index):
            for p, (px, py) in enumerate(chips):
                k = o * 3 + p
                peer = 2 * px + py
                pltpu.make_async_remote_copy(src_ref=_cut(ins[o], ax, peer, size), dst_ref=outs[t].at[peer, l],
                                             send_sem=send_sems.at[k], recv_sem=recv_sems.at[k],
                                             device_id=(px, py, c), device_id_type=MESH).wait_recv()
        for cp in remote:
            cp.wait_send()
        for cp in local:
            cp.wait()

    return _pcall(
        body, in_specs=[HBM_SPEC] * n_in, out_specs=[HBM_SPEC] * n, out_shape=out_shapes,
        scratch_shapes=[pltpu.SemaphoreType.DMA((3 * n_in,)), pltpu.SemaphoreType.DMA((3 * n_in,)),
                        pltpu.SemaphoreType.DMA((n_in,))],
        name="scatter_grads", compiler_params=pltpu.CompilerParams(has_side_effects=True))(*flat)


def _swap_cores(parts):
    n = len(parts)

    def body(*refs):
        ins, outs = refs[:n], refs[n:2 * n]
        send_sems, recv_sems = refs[2 * n:]
        x, y, c, _ = _position()
        copies = [pltpu.make_async_remote_copy(src_ref=ins[t], dst_ref=outs[t], send_sem=send_sems.at[t],
                                               recv_sem=recv_sems.at[t], device_id=(x, y, 1 - c), device_id_type=MESH)
                  for t in range(n)]
        for cp in copies:
            cp.start()
        for cp in copies:
            cp.wait()

    return _pcall(
        body, in_specs=[HBM_SPEC] * n, out_specs=[HBM_SPEC] * n,
        out_shape=[jax.ShapeDtypeStruct(p.shape, p.dtype) for p in parts],
        scratch_shapes=[pltpu.SemaphoreType.DMA((n,)), pltpu.SemaphoreType.DMA((n,))],
        name="swap_cores", compiler_params=pltpu.CompilerParams(has_side_effects=True))(*parts)


def _all_reduce_small(pack):
    r, c = pack.shape

    def body(in_ref, out_ref, slots, send_sems, recv_sems, local_sem):
        x, y, cc, _ = _position()
        me = 4 * x + 2 * y + cc
        peers = []
        for k in range(1, N_DEV):
            fx, fy, fc = (k >> 2) & 1, (k >> 1) & 1, k & 1
            px, py, pc = x ^ fx, y ^ fy, cc ^ fc
            peers.append((k - 1, (px, py, pc), 4 * px + 2 * py + pc))
        loc = pltpu.make_async_copy(in_ref, slots.at[me], local_sem)
        loc.start()
        copies = [pltpu.make_async_remote_copy(src_ref=in_ref, dst_ref=slots.at[me], send_sem=send_sems.at[k],
                                               recv_sem=recv_sems.at[k], device_id=dev, device_id_type=MESH)
                  for k, dev, _ in peers]
        for cp in copies:
            cp.start()
        for k, dev, idx in peers:
            pltpu.make_async_remote_copy(src_ref=in_ref, dst_ref=slots.at[idx], send_sem=send_sems.at[k],
                                         recv_sem=recv_sems.at[k], device_id=dev, device_id_type=MESH).wait_recv()
        for cp in copies:
            cp.wait_send()
        loc.wait()
        acc = slots[0]
        for k in range(1, N_DEV):
            acc = acc + slots[k]
        out_ref[...] = acc

    vm = pl.BlockSpec(memory_space=pltpu.VMEM)
    return _pcall(
        body, in_specs=[vm], out_specs=vm, out_shape=jax.ShapeDtypeStruct((r, c), F32),
        scratch_shapes=[pltpu.VMEM((N_DEV, r, c), F32), pltpu.SemaphoreType.DMA((N_DEV - 1,)),
                        pltpu.SemaphoreType.DMA((N_DEV - 1,)), pltpu.SemaphoreType.DMA(())],
        name="all_reduce_small", compiler_params=pltpu.CompilerParams(has_side_effects=True, vmem_limit_bytes=VMEM_LIMIT))(pack)


def _local_step(x, mem, target, full, small):
    s, d = x.shape
    n_layers = len(full['w_in'])
    half = d // 2
    h_sb = half // HEAD_SB
    h_xa = half // HEAD_XA
    u_blk = 3
    q_blk = (5 * half) // HEAD_XA

    def vec(name, l):
        return small[name][l].reshape(1, -1)

    saved = []
    for l in range(n_layers):
        tag = f"_l{l}"
        h1 = _norm_fwd(x, vec('g_mix_pre', l), None, BF16, "norm_mix_pre")
        proj = _mm(h1, full['w_in'][l], 'nn', F32, "mm_proj")
        o_sb = _sb_fwd(proj, h_sb)
        b_st = small['b_s'][l].T
        o_gm = _gm_fwd(proj, vec('g_vnorm', l), small['w_s'][l], b_st, u_blk)
        memn = _norm_fwd(mem, vec('g_mem', l), None, BF16, "norm_mem")
        mem_kv = _mm(memn, full['w_mem_kv'][l], 'nn', F32, "mm_mem_kv")
        o_xa = _xa_fwd(proj, mem_kv, q_blk, h_xa)
        zg = _mm(h1, full['w_gate'][l], 'nn', F32, "mm_gate")
        branches = [_mm(o, full[wn][l], 'nn', F32, "mm_branch")
                    for o, wn in ((o_sb, 'w_br_sb'), (o_gm, 'w_br_gm'), (o_xa, 'w_br_xa'))]
        merged = _merge_fwd(zg, vec('b_gate', l), branches)
        y1 = _mm(merged, full['w_out'][l], 'nn', F32, "mm_out")
        x1 = _norm_fwd(y1, vec('g_mix_post', l), x, F32, "norm_mix_post")
        h2 = _norm_fwd(x1, vec('g_ffn_pre', l), None, BF16, "norm_ffn_pre")
        up = _mm(h2, full['w_up'][l], 'nn', F32, "mm_up")
        act = _cg_fwd(up, full['conv_w'][l], vec('conv_b', l))
        y2 = _mm(act, full['w_down'][l], 'nn', F32, "mm_down")
        x2 = _norm_fwd(y2, vec('g_ffn_post', l), x1, F32, "norm_ffn_post")
        saved.append(dict(x0=x, h1=h1, proj=proj, o_sb=o_sb, o_gm=o_gm, o_xa=o_xa, memn=memn, mem_kv=mem_kv,
                          zg=zg, branches=branches, merged=merged, y1=y1, x1=x1, h2=h2, up=up, act=act, y2=y2, b_st=b_st))
        x = x2
        del tag

    sq, dx = _loss_head(x, target)

    gbig = {n: [None] * n_layers for n in BIG}
    gsmall = {n: [None] * n_layers for n in SMALL + ['conv_w']}
    for l in reversed(range(n_layers)):
        sv = saved[l]
        dy2, gsmall['g_ffn_post'][l] = _norm_bwd(sv['y2'], vec('g_ffn_post', l), [dx], None, BF16, "norm_ffn_post_bwd")
        gbig['w_down'][l] = _mm(sv['act'], dy2, 'tn', BF16, "mm_down_dw")
        dact = _mm(dy2, full['w_down'][l], 'nt', F32, "mm_down_dx")
        dgate, dval, gsmall['conv_w'][l], gsmall['conv_b'][l] = _cg_bwd(sv['up'], full['conv_w'][l], vec('conv_b', l), dact)
        dup = jnp.concatenate([dgate, dval], axis=1)
        gbig['w_up'][l] = _mm(sv['h2'], dup, 'tn', BF16, "mm_up_dw")
        dh2 = _mm(dup, full['w_up'][l], 'nt', F32, "mm_up_dx")
        dx1, gsmall['g_ffn_pre'][l] = _norm_bwd(sv['x1'], vec('g_ffn_pre', l), [dh2], dx, F32, "norm_ffn_pre_bwd")
        dy1, gsmall['g_mix_post'][l] = _norm_bwd(sv['y1'], vec('g_mix_post', l), [dx1], None, BF16, "norm_mix_post_bwd")
        gbig['w_out'][l] = _mm(sv['merged'], dy1, 'tn', BF16, "mm_out_dw")
        dmerged = _mm(dy1, full['w_out'][l], 'nt', F32, "mm_out_dx")
        dzg, dbr, gsmall['b_gate'][l] = _merge_bwd(sv['zg'], vec('b_gate', l), sv['branches'], dmerged)
        douts = []
        for o, db, wn in ((sv['o_sb'], dbr[0], 'w_br_sb'), (sv['o_gm'], dbr[1], 'w_br_gm'), (sv['o_xa'], dbr[2], 'w_br_xa')):
            gbig[wn][l] = _mm(o, db, 'tn', BF16, "mm_branch_dw")
            douts.append(_mm(db, full[wn][l], 'nt', F32, "mm_branch_dx"))
        gbig['w_gate'][l] = _mm(sv['h1'], dzg, 'tn', BF16, "mm_gate_dw")
        dh1_gate = _mm(dzg, full['w_gate'][l], 'nt', F32, "mm_gate_dx")
        dq_xa, dk_xa, dv_xa = _xa_bwd(sv['proj'], sv['mem_kv'], douts[2], q_blk, h_xa)
        dmem_kv = jnp.concatenate([dk_xa, dv_xa], axis=1).astype(BF16)
        gbig['w_mem_kv'][l] = _mm(sv['memn'], dmem_kv, 'tn', BF16, "mm_mem_kv_dw")
        dmemn = _mm(dmem_kv, full['w_mem_kv'][l], 'nt', F32, "mm_mem_kv_dx")
        _, gsmall['g_mem'][l] = _norm_bwd(mem, vec('g_mem', l), [dmemn], None, BF16, "norm_mem_bwd")
        du, dv, gsmall['g_vnorm'][l], gsmall['w_s'][l], db_st = _gm_bwd(
            sv['proj'], vec('g_vnorm', l), small['w_s'][l], sv['b_st'], douts[1], u_blk)
        gsmall['b_s'][l] = db_st.T
        dq, dk, dvv = _sb_bwd(sv['proj'], douts[0], h_sb)
        dproj = jnp.concatenate([dq, dk, dvv, du, dv, dq_xa], axis=1).astype(BF16)
        gbig['w_in'][l] = _mm(sv['h1'], dproj, 'tn', BF16, "mm_proj_dw")
        dh1_proj = _mm(dproj, full['w_in'][l], 'nt', F32, "mm_proj_dx")
        dx, gsmall['g_mix_pre'][l] = _norm_bwd(sv['x0'], vec('g_mix_pre', l), [dh1_gate, dh1_proj], dx1, F32, "norm_mix_pre_bwd")
    return sq, dx, gbig, gsmall


def _pack(arrays, rows_multiple):
    flat = jnp.concatenate([a.reshape(-1).astype(F32) for a in arrays])
    rows = -(-flat.shape[0] // LANES)
    rows = -(-rows // rows_multiple) * rows_multiple
    return jnp.pad(flat, (0, rows * LANES - flat.shape[0])).reshape(rows, LANES)


def _unpack(pack, like):
    flat = pack.reshape(-1)
    out, off = [], 0
    for a in like:
        out.append(flat[off:off + a.size].reshape(a.shape))
        off += a.size
    return out


def kernel(x, mem, g_mix_pre, w_in, g_vnorm, w_s, b_s, g_mem, w_mem_kv, w_gate, b_gate, w_br_sb, w_br_gm, w_br_xa, w_out, g_mix_post, g_ffn_pre, w_up, conv_w, conv_b, w_down, g_ffn_post, loss_target, m_g_mix_pre, m_w_in, m_g_vnorm, m_w_s, m_b_s, m_g_mem, m_w_mem_kv, m_w_gate, m_b_gate, m_w_br_sb, m_w_br_gm, m_w_br_xa, m_w_out, m_g_mix_post, m_g_ffn_pre, m_w_up, m_conv_w, m_conv_b, m_w_down, m_g_ffn_post, v_g_mix_pre, v_w_in, v_g_vnorm, v_w_s, v_b_s, v_g_mem, v_w_mem_kv, v_w_gate, v_b_gate, v_w_br_sb, v_w_br_gm, v_w_br_xa, v_w_out, v_g_mix_post, v_g_ffn_pre, v_w_up, v_conv_w, v_conv_b, v_w_down, v_g_ffn_post):
    w = dict(g_mix_pre=g_mix_pre, w_in=w_in, g_vnorm=g_vnorm, w_s=w_s, b_s=b_s, g_mem=g_mem, w_mem_kv=w_mem_kv,
             w_gate=w_gate, b_gate=b_gate, w_br_sb=w_br_sb, w_br_gm=w_br_gm, w_br_xa=w_br_xa, w_out=w_out,
             g_mix_post=g_mix_post, g_ffn_pre=g_ffn_pre, w_up=w_up, conv_w=conv_w, conv_b=conv_b, w_down=w_down,
             g_ffn_post=g_ffn_post)
    m = dict(g_mix_pre=m_g_mix_pre, w_in=m_w_in, g_vnorm=m_g_vnorm, w_s=m_w_s, b_s=m_b_s, g_mem=m_g_mem,
             w_mem_kv=m_w_mem_kv, w_gate=m_w_gate, b_gate=m_b_gate, w_br_sb=m_w_br_sb, w_br_gm=m_w_br_gm,
             w_br_xa=m_w_br_xa, w_out=m_w_out, g_mix_post=m_g_mix_post, g_ffn_pre=m_g_ffn_pre, w_up=m_w_up,
             conv_w=m_conv_w, conv_b=m_conv_b, w_down=m_w_down, g_ffn_post=m_g_ffn_post)
    v = dict(g_mix_pre=v_g_mix_pre, w_in=v_w_in, g_vnorm=v_g_vnorm, w_s=v_w_s, b_s=v_b_s, g_mem=v_g_mem,
             w_mem_kv=v_w_mem_kv, w_gate=v_w_gate, b_gate=v_b_gate, w_br_sb=v_w_br_sb, w_br_gm=v_w_br_gm,
             w_br_xa=v_w_br_xa, w_out=v_w_out, g_mix_post=v_g_mix_post, g_ffn_pre=v_g_ffn_pre, w_up=v_w_up,
             conv_w=v_conv_w, conv_b=v_conv_b, w_down=v_w_down, g_ffn_post=v_g_ffn_post)
    n_layers = w_in.shape[0]
    d = x.shape[-1]

    gathered = _all_gather([w[n].astype(BF16) for n in BIG] + [conv_w], [BIG_AXIS[n] for n in BIG] + [2])
    full, pos = {}, 0
    for n in BIG + ['conv_w']:
        full[n] = gathered[pos:pos + n_layers]
        pos += n_layers

    sq, dx, gbig, gsmall = _local_step(x[0], mem[0], loss_target[0], full, {n: w[n] for n in SMALL})
    loss = lax.psum(0.5 * jnp.sum(sq) / d, ("x", "y", "c"))

    recv = _scatter_grads([gbig[n] for n in BIG], [BIG_AXIS[n] for n in BIG])
    mine = [_sum_chips(r, "sum_chips") for r in recv]
    theirs = _swap_cores(mine)

    small_full = [jnp.stack(gsmall[n]).reshape(w[n].shape) for n in SMALL]
    conv_w_full = jnp.stack(gsmall['conv_w'])
    summed = _all_reduce_small(_pack(small_full + [conv_w_full], 8))
    *small_g, conv_w_g = _unpack(summed, small_full + [conv_w_full])
    shard = conv_w.shape[-1]
    chip = 2 * lax.axis_index("x") + lax.axis_index("y")
    conv_w_g = lax.dynamic_slice_in_dim(conv_w_g, chip * shard, shard, axis=2)

    out = {}
    for n, a, b in zip(BIG, mine, theirs):
        out[n] = _adamw(w[n], m[n], v[n], [a, b], "adamw_big")
    names = SMALL + ['conv_w']
    packed = [_pack([p[n] for n in names], 256) for p in (w, m, v)]
    gpack = _pack(small_g + [conv_w_g], 256)
    res = _adamw(packed[0][None], packed[1][None], packed[2][None], [gpack[None]], "adamw_small")
    like = [w[n] for n in names]
    unpacked = [_unpack(r[0], like) for r in res]
    for i, n in enumerate(names):
        out[n] = tuple(u[i] for u in unpacked)

    return (loss, dx[None], *[out[n][0] for n in WEIGHTS], *[out[n][1] for n in WEIGHTS],
            *[out[n][2] for n in WEIGHTS], *[out[n][3] for n in WEIGHTS])
```

```python
import functools
import math

import jax
import jax.numpy as jnp
from jax import lax
from jax.experimental import pallas as pl
from jax.experimental.pallas import tpu as pltpu

F32 = jnp.float32
BF16 = jnp.bfloat16
EPS = 1e-6
HEAD_SB = 128
GROUP_GM = 128
CHUNK = 64
HEAD_XA = 256
CONV_TAPS = 3
N_CHIPS = 4
N_DEV = 8
LANES = 128
MIB = 1024 * 1024
VMEM_LIMIT = 48 * MIB
SPLITS = 3

ADAM_LR = 0.001
ADAM_B1 = 0.9
ADAM_B2 = 0.999
ADAM_EPS = 1e-08
ADAM_WD = 0.01
ADAM_STEP = 10

WEIGHTS = ['g_mix_pre', 'w_in', 'g_vnorm', 'w_s', 'b_s', 'g_mem', 'w_mem_kv', 'w_gate', 'b_gate', 'w_br_sb',
           'w_br_gm', 'w_br_xa', 'w_out', 'g_mix_post', 'g_ffn_pre', 'w_up', 'conv_w', 'conv_b', 'w_down',
           'g_ffn_post']
BIG_AXIS = {'w_in': 2, 'w_mem_kv': 1, 'w_gate': 2, 'w_br_sb': 2, 'w_br_gm': 2, 'w_br_xa': 2, 'w_out': 1,
            'w_up': 2, 'w_down': 1}
BIG = list(BIG_AXIS)
SMALL = ['g_mix_pre', 'g_vnorm', 'w_s', 'b_s', 'g_mem', 'b_gate', 'g_mix_post', 'g_ffn_pre', 'conv_b', 'g_ffn_post']
MESH = pl.DeviceIdType.MESH


def _pcall(body, **kw):
    return pl.pallas_call(body, **kw)


def _params(sem=None, vmem=VMEM_LIMIT):
    return pltpu.CompilerParams(dimension_semantics=sem, vmem_limit_bytes=vmem)


def _tile(n, cands):
    for c in cands:
        if n % c == 0:
            return c
    return n


_GELU_C = math.sqrt(2.0 / math.pi)
_GELU_A = 0.044715


def _gelu(x):
    return 0.5 * x * (1.0 + jnp.tanh(_GELU_C * (x + _GELU_A * (x * x * x))))


def _gelu_and_grad(x):
    x2 = x * x
    t = jnp.tanh(_GELU_C * (x + _GELU_A * (x2 * x)))
    val = 0.5 * x * (1.0 + t)
    grad = 0.5 * (1.0 + t) + 0.5 * x * (1.0 - t * t) * (_GELU_C * (1.0 + 3.0 * _GELU_A * x2))
    return val, grad


def _softplus(z):
    return jnp.maximum(z, 0.0) + jnp.log1p(jnp.exp(-jnp.abs(z)))


def _dot(a, b):
    return jnp.dot(a, b, preferred_element_type=F32)


def _dot_nt(a, b):
    return lax.dot_general(a, b, (((1,), (1,)), ((), ())), preferred_element_type=F32)


def _dot_tn(a, b):
    return lax.dot_general(a, b, (((0,), (0,)), ((), ())), preferred_element_type=F32)


def _split_dot(a, m):
    out = None
    rest = a
    for _ in range(SPLITS):
        piece = rest.astype(BF16)
        rest = rest - piece.astype(F32)
        term = _dot(piece, m)
        out = term if out is None else out + term
    return out


def _mm(a, b, mode, out_dtype, name, tm=None, tn=None, tk=None):
    if mode == 'nn':
        (m, kc), (kc2, n) = a.shape, b.shape
    elif mode == 'nt':
        (m, kc), (n, kc2) = a.shape, b.shape
    else:
        (kc, m), (kc2, n) = a.shape, b.shape
    assert kc == kc2, (a.shape, b.shape, mode)
    tm = tm or _tile(m, (1024, 512, 256, 128))
    tn = tn or _tile(n, (1024, 512, 256, 128))
    tk = tk or (kc if kc <= 2048 else _tile(kc, (2048, 1536, 1408, 1024, 512)))
    nk = kc // tk
    dot = {'nn': _dot, 'nt': _dot_nt, 'tn': _dot_tn}[mode]
    a_spec = pl.BlockSpec((tk, tm), lambda i, j, k: (k, i)) if mode == 'tn' else pl.BlockSpec((tm, tk), lambda i, j, k: (i, k))
    b_spec = pl.BlockSpec((tn, tk), lambda i, j, k: (j, k)) if mode == 'nt' else pl.BlockSpec((tk, tn), lambda i, j, k: (k, j))

    if nk == 1:
        def body(a_ref, b_ref, o_ref):
            o_ref[...] = dot(a_ref[...].astype(BF16), b_ref[...].astype(BF16)).astype(o_ref.dtype)
        scratch = []
    else:
        def body(a_ref, b_ref, o_ref, acc_ref):
            k = pl.program_id(2)
            part = dot(a_ref[...].astype(BF16), b_ref[...].astype(BF16))

            @pl.when(k == 0)
            def _():
                acc_ref[...] = part

            @pl.when(k > 0)
            def _():
                acc_ref[...] += part

            @pl.when(k == nk - 1)
            def _():
                o_ref[...] = acc_ref[...].astype(o_ref.dtype)
        scratch = [pltpu.VMEM((tm, tn), F32)]

    return _pcall(
        body, grid=(m // tm, n // tn, nk), in_specs=[a_spec, b_spec],
        out_specs=pl.BlockSpec((tm, tn), lambda i, j, k: (i, j)),
        out_shape=jax.ShapeDtypeStruct((m, n), out_dtype), scratch_shapes=scratch, name=name,
        compiler_params=_params(("parallel", "parallel", "arbitrary")))(a, b)


def _norm_fwd(x, g, res, out_dtype, name):
    s, d = x.shape
    tr = _tile(s, (256, 128))
    has_res = res is not None

    def body(*refs):
        x_ref, g_ref = refs[0], refs[1]
        o_ref = refs[-1]
        xv = x_ref[...]
        y = xv * lax.rsqrt(jnp.mean(xv * xv, axis=-1, keepdims=True) + EPS) * g_ref[...]
        if has_res:
            y = y + refs[2][...]
        o_ref[...] = y.astype(o_ref.dtype)

    row = pl.BlockSpec((tr, d), lambda i: (i, 0))
    ins = [x, g] + ([res] if has_res else [])
    return _pcall(
        body, grid=(s // tr,), in_specs=[row, pl.BlockSpec((1, d), lambda i: (0, 0))] + ([row] if has_res else []),
        out_specs=row, out_shape=jax.ShapeDtypeStruct((s, d), out_dtype), name=name,
        compiler_params=_params(("parallel",)))(*ins)


def _norm_bwd(x, g, douts, dres, out_dtype, name):
    s, d = x.shape
    tr = _tile(s, (256, 128))
    nd = len(douts)
    has_res = dres is not None

    def body(*refs):
        x_ref, g_ref = refs[0], refs[1]
        dx_ref, dg_ref = refs[-2], refs[-1]
        dout = refs[2][...].astype(F32)
        for r in refs[3:2 + nd]:
            dout = dout + r[...].astype(F32)
        xv = x_ref[...]
        r = lax.rsqrt(jnp.mean(xv * xv, axis=-1, keepdims=True) + EPS)
        n = xv * r
        dn = dout * g_ref[...]
        dx = r * (dn - n * jnp.mean(dn * n, axis=-1, keepdims=True))
        if has_res:
            dx = dx + refs[2 + nd][...]
        dx_ref[...] = dx.astype(dx_ref.dtype)

        @pl.when(pl.program_id(0) == 0)
        def _():
            dg_ref[...] = jnp.zeros_like(dg_ref)

        dg_ref[...] += jnp.sum(dout * n, axis=0, keepdims=True)

    row = pl.BlockSpec((tr, d), lambda i: (i, 0))
    vec = pl.BlockSpec((1, d), lambda i: (0, 0))
    ins = [x, g] + list(douts) + ([dres] if has_res else [])
    return _pcall(
        body, grid=(s // tr,), in_specs=[row, vec] + [row] * (nd + int(has_res)), out_specs=[row, vec],
        out_shape=[jax.ShapeDtypeStruct((s, d), out_dtype), jax.ShapeDtypeStruct((1, d), F32)], name=name,
        compiler_params=_params(("arbitrary",)))(*ins)


def _loss_head(y, target):
    s, d = y.shape
    tr = _tile(s, (256, 128))

    def body(y_ref, t_ref, sq_ref, dy_ref):
        e = y_ref[...] - t_ref[...]
        dy_ref[...] = e * (1.0 / d)

        @pl.when(pl.program_id(0) == 0)
        def _():
            sq_ref[...] = jnp.zeros_like(sq_ref)

        sq_ref[...] += jnp.sum(e * e, axis=0, keepdims=True)

    row = pl.BlockSpec((tr, d), lambda i: (i, 0))
    return _pcall(
        body, grid=(s // tr,), in_specs=[row, row], out_specs=[pl.BlockSpec((1, d), lambda i: (0, 0)), row],
        out_shape=[jax.ShapeDtypeStruct((1, d), F32), jax.ShapeDtypeStruct((s, d), F32)], name="loss_head",
        compiler_params=_params(("arbitrary",)))(y, target)


def _sb_masks():
    row = lax.broadcasted_iota(jnp.int32, (HEAD_SB, HEAD_SB), 0)
    col = lax.broadcasted_iota(jnp.int32, (HEAD_SB, HEAD_SB), 1)
    return row, col


def _sb_fwd(proj, n_heads):
    s = proj.shape[0]
    nq = s // HEAD_SB
    scale = HEAD_SB ** -0.5

    def body(q_ref, k_ref, v_ref, o_ref):
        i = pl.program_id(1)
        q = q_ref[...].astype(BF16)
        row, col = _sb_masks()
        m_strict = (row > col).astype(BF16)

        def step(jj, carry):
            c, acc = carry
            off = pl.multiple_of((i - jj) * HEAD_SB, HEAD_SB)
            kb = k_ref[pl.ds(off, HEAD_SB), :].astype(BF16)
            vb = v_ref[pl.ds(off, HEAD_SB), :].astype(BF16)
            z = _dot_nt(q, kb) * scale
            mask = (off + col) < (i * HEAD_SB + row)
            sp = _softplus(z)
            lk = jnp.where(mask, -sp, 0.0)
            suffix = _split_dot(lk, m_strict) + c
            a = jnp.where(mask, jnp.exp(z - sp + suffix), 0.0)
            acc = acc + _dot(a.astype(BF16), vb)
            return c + jnp.sum(lk, axis=1, keepdims=True), acc

        _, acc = lax.fori_loop(0, i + 1, step, (jnp.zeros((HEAD_SB, 1), F32), jnp.zeros((HEAD_SB, HEAD_SB), F32)))
        o_ref[...] = acc.astype(BF16)

    h = n_heads
    blk = pl.BlockSpec((HEAD_SB, HEAD_SB), lambda hh, i: (i, hh))
    return _pcall(
        body, grid=(h, nq),
        in_specs=[blk, pl.BlockSpec((s, HEAD_SB), lambda hh, i: (0, h + hh)),
                  pl.BlockSpec((s, HEAD_SB), lambda hh, i: (0, 2 * h + hh))],
        out_specs=blk, out_shape=jax.ShapeDtypeStruct((s, h * HEAD_SB), BF16),
        name="sb_fwd", compiler_params=_params(("parallel", "arbitrary")))(proj, proj, proj)


def _sb_bwd(proj, do, n_heads):
    s = proj.shape[0]
    nq = s // HEAD_SB
    scale = HEAD_SB ** -0.5

    def body(q_ref, k_ref, v_ref, do_ref, dq_ref, dk_ref, dv_ref, g_ref, beta_ref):
        i = pl.program_id(1)

        @pl.when(i == 0)
        def _():
            dk_ref[...] = jnp.zeros_like(dk_ref)
            dv_ref[...] = jnp.zeros_like(dv_ref)

        q = q_ref[...].astype(BF16)
        dob = do_ref[...].astype(BF16)
        row, col = _sb_masks()
        m_strict = (row > col).astype(BF16)
        m_before = (row < col).astype(BF16)

        def sweep_left(jj, c):
            j = i - jj
            off = pl.multiple_of(j * HEAD_SB, HEAD_SB)
            kb = k_ref[pl.ds(off, HEAD_SB), :].astype(BF16)
            vb = v_ref[pl.ds(off, HEAD_SB), :].astype(BF16)
            z = _dot_nt(q, kb) * scale
            mask = (off + col) < (i * HEAD_SB + row)
            sp = _softplus(z)
            lk = jnp.where(mask, -sp, 0.0)
            suffix = _split_dot(lk, m_strict) + c
            beta = jnp.exp(z - sp)
            a = jnp.where(mask, beta * jnp.exp(suffix), 0.0)
            g_ref[j] = a * _dot_nt(dob, vb)
            beta_ref[j] = beta
            dv_ref[pl.ds(off, HEAD_SB), :] += _dot_tn(a.astype(BF16), dob)
            return c + jnp.sum(lk, axis=1, keepdims=True)

        lax.fori_loop(0, i + 1, sweep_left, jnp.zeros((HEAD_SB, 1), F32))

        def sweep_right(j, carry):
            p, dq = carry
            off = pl.multiple_of(j * HEAD_SB, HEAD_SB)
            kb = k_ref[pl.ds(off, HEAD_SB), :].astype(BF16)
            g = g_ref[j]
            beta = beta_ref[j]
            mask = (off + col) < (i * HEAD_SB + row)
            before = _split_dot(g, m_before) + p
            dz = jnp.where(mask, g * (1.0 - beta) - beta * before, 0.0) * scale
            dzb = dz.astype(BF16)
            dk_ref[pl.ds(off, HEAD_SB), :] += _dot_tn(dzb, q)
            return p + jnp.sum(g, axis=1, keepdims=True), dq + _dot(dzb, kb)

        _, dq = lax.fori_loop(0, i + 1, sweep_right, (jnp.zeros((HEAD_SB, 1), F32), jnp.zeros((HEAD_SB, HEAD_SB), F32)))
        dq_ref[...] = dq

    h = n_heads
    blk = pl.BlockSpec((HEAD_SB, HEAD_SB), lambda hh, i: (i, hh))
    col_blk = pl.BlockSpec((s, HEAD_SB), lambda hh, i: (0, hh))
    shape = jax.ShapeDtypeStruct((s, h * HEAD_SB), F32)
    return _pcall(
        body, grid=(h, nq),
        in_specs=[blk, pl.BlockSpec((s, HEAD_SB), lambda hh, i: (0, h + hh)),
                  pl.BlockSpec((s, HEAD_SB), lambda hh, i: (0, 2 * h + hh)), blk],
        out_specs=[blk, col_blk, col_blk], out_shape=[shape, shape, shape],
        scratch_shapes=[pltpu.VMEM((nq, HEAD_SB, HEAD_SB), F32), pltpu.VMEM((nq, HEAD_SB, HEAD_SB), F32)],
        name="sb_bwd", compiler_params=_params(("parallel", "arbitrary")))(proj, proj, proj, do)


def _gm_mask():
    t = lax.broadcasted_iota(jnp.int32, (GROUP_GM, GROUP_GM), 0)
    s = lax.broadcasted_iota(jnp.int32, (GROUP_GM, GROUP_GM), 1)
    shift = CHUNK.bit_length() - 1
    return (s >> shift) <= (t >> shift)


def _gm_fwd(proj, g_vnorm, w_s, b_st, u_blk):
    s = proj.shape[0]
    groups = w_s.shape[0]
    w = groups * GROUP_GM

    def body(u_ref, v_ref, gv_ref, ws_ref, bst_ref, o_ref):
        ug = _gelu(u_ref[...])
        vg = _gelu(v_ref[...])
        vn = vg * lax.rsqrt(jnp.mean(vg * vg, axis=-1, keepdims=True) + EPS) * gv_ref[...]
        vnb = vn.astype(BF16)
        mask = _gm_mask()
        for g in range(groups):
            sl = slice(g * GROUP_GM, (g + 1) * GROUP_GM)
            wm = jnp.where(mask, ws_ref[g], 0.0).astype(BF16)
            mixed = _dot(wm, vnb[:, sl]) + bst_ref[:, g:g + 1]
            o_ref[:, sl] = (ug[:, sl] * mixed).astype(o_ref.dtype)

    return _pcall(
        body, grid=(s // GROUP_GM,),
        in_specs=[pl.BlockSpec((GROUP_GM, w), lambda c: (c, u_blk)), pl.BlockSpec((GROUP_GM, w), lambda c: (c, u_blk + 1)),
                  pl.BlockSpec((1, w), lambda c: (0, 0)), pl.BlockSpec((groups, GROUP_GM, GROUP_GM), lambda c: (0, 0, 0)),
                  pl.BlockSpec((GROUP_GM, groups), lambda c: (0, 0))],
        out_specs=pl.BlockSpec((GROUP_GM, w), lambda c: (c, 0)),
        out_shape=jax.ShapeDtypeStruct((s, w), BF16), name="gm_fwd",
        compiler_params=_params(("parallel",)))(proj, proj, g_vnorm, w_s, b_st)


def _gm_bwd(proj, g_vnorm, w_s, b_st, do, u_blk):
    s = proj.shape[0]
    groups = w_s.shape[0]
    w = groups * GROUP_GM

    def body(u_ref, v_ref, gv_ref, ws_ref, bst_ref, do_ref, du_ref, dv_ref, dgv_ref, dws_ref, dbst_ref, dvn_ref):
        @pl.when(pl.program_id(0) == 0)
        def _():
            dgv_ref[...] = jnp.zeros_like(dgv_ref)
            dws_ref[...] = jnp.zeros_like(dws_ref)
            dbst_ref[...] = jnp.zeros_like(dbst_ref)

        ug, ugrad = _gelu_and_grad(u_ref[...])
        vg, vgrad = _gelu_and_grad(v_ref[...])
        r = lax.rsqrt(jnp.mean(vg * vg, axis=-1, keepdims=True) + EPS)
        n = vg * r
        gv = gv_ref[...]
        vnb = (n * gv).astype(BF16)
        dout = do_ref[...]
        mask = _gm_mask()
        for g in range(groups):
            sl = slice(g * GROUP_GM, (g + 1) * GROUP_GM)
            wm = jnp.where(mask, ws_ref[g], 0.0).astype(BF16)
            mixed = _dot(wm, vnb[:, sl]) + bst_ref[:, g:g + 1]
            dmixed = dout[:, sl] * ug[:, sl]
            du_ref[:, sl] = dout[:, sl] * mixed * ugrad[:, sl]
            dbst_ref[:, g:g + 1] += jnp.sum(dmixed, axis=1, keepdims=True)
            dmb = dmixed.astype(BF16)
            dws_ref[g] += jnp.where(mask, _dot_nt(dmb, vnb[:, sl]), 0.0)
            dvn_ref[:, sl] = _dot_tn(wm, dmb)
        dvn = dvn_ref[...]
        dgv_ref[...] += jnp.sum(dvn * n, axis=0, keepdims=True)
        dn = dvn * gv
        dvg = r * (dn - n * jnp.mean(dn * n, axis=-1, keepdims=True))
        dv_ref[...] = dvg * vgrad

    rowb = pl.BlockSpec((GROUP_GM, w), lambda c: (c, 0))
    vec = pl.BlockSpec((1, w), lambda c: (0, 0))
    wsb = pl.BlockSpec((groups, GROUP_GM, GROUP_GM), lambda c: (0, 0, 0))
    bsb = pl.BlockSpec((GROUP_GM, groups), lambda c: (0, 0))
    return _pcall(
        body, grid=(s // GROUP_GM,),
        in_specs=[pl.BlockSpec((GROUP_GM, w), lambda c: (c, u_blk)), pl.BlockSpec((GROUP_GM, w), lambda c: (c, u_blk + 1)),
                  vec, wsb, bsb, rowb],
        out_specs=[rowb, rowb, vec, wsb, bsb],
        out_shape=[jax.ShapeDtypeStruct((s, w), F32), jax.ShapeDtypeStruct((s, w), F32), jax.ShapeDtypeStruct((1, w), F32),
                   jax.ShapeDtypeStruct((groups, GROUP_GM, GROUP_GM), F32), jax.ShapeDtypeStruct((GROUP_GM, groups), F32)],
        scratch_shapes=[pltpu.VMEM((GROUP_GM, w), F32)], name="gm_bwd",
        compiler_params=_params(("arbitrary",)))(proj, proj, g_vnorm, w_s, b_st, do)


def _xa_fwd(proj, mem_kv, q_blk, n_heads):
    s = proj.shape[0]
    nm = mem_kv.shape[0]
    tq = _tile(s, (512, 256, 128))
    scale = HEAD_XA ** -0.5

    def body(q_ref, k_ref, v_ref, o_ref):
        z = _dot_nt(q_ref[...].astype(BF16), k_ref[...].astype(BF16)) * scale
        z = z - jnp.max(z, axis=-1, keepdims=True)
        e = jnp.exp(z)
        p = e / jnp.sum(e, axis=-1, keepdims=True)
        o_ref[...] = _dot(p.astype(BF16), v_ref[...].astype(BF16)).astype(o_ref.dtype)

    h = n_heads
    return _pcall(
        body, grid=(h, s // tq),
        in_specs=[pl.BlockSpec((tq, HEAD_XA), lambda hh, i: (i, q_blk + hh)),
                  pl.BlockSpec((nm, HEAD_XA), lambda hh, i: (0, hh)), pl.BlockSpec((nm, HEAD_XA), lambda hh, i: (0, h + hh))],
        out_specs=pl.BlockSpec((tq, HEAD_XA), lambda hh, i: (i, hh)),
        out_shape=jax.ShapeDtypeStruct((s, h * HEAD_XA), BF16), name="xa_fwd",
        compiler_params=_params(("parallel", "parallel")))(proj, mem_kv, mem_kv)


def _xa_bwd(proj, mem_kv, do, q_blk, n_heads):
    s = proj.shape[0]
    nm = mem_kv.shape[0]
    tq = _tile(s, (512, 256, 128))
    scale = HEAD_XA ** -0.5
    h = n_heads

    def body(q_ref, k_ref, v_ref, do_ref, dq_ref, dk_ref, dv_ref):
        @pl.when(pl.program_id(1) == 0)
        def _():
            dk_ref[...] = jnp.zeros_like(dk_ref)
            dv_ref[...] = jnp.zeros_like(dv_ref)

        qb = q_ref[...].astype(BF16)
        kb = k_ref[...].astype(BF16)
        vb = v_ref[...].astype(BF16)
        dob = do_ref[...].astype(BF16)
        z = _dot_nt(qb, kb) * scale
        z = z - jnp.max(z, axis=-1, keepdims=True)
        e = jnp.exp(z)
        p = e / jnp.sum(e, axis=-1, keepdims=True)
        dp = _dot_nt(dob, vb)
        dz = (p * (dp - jnp.sum(dp * p, axis=-1, keepdims=True)) * scale).astype(BF16)
        dq_ref[...] = _dot(dz, kb)
        dk_ref[...] += _dot_tn(dz, qb)
        dv_ref[...] += _dot_tn(p.astype(BF16), dob)

    qspec = pl.BlockSpec((tq, HEAD_XA), lambda hh, i: (i, hh))
    dk, dv = None, None
    dq, dk, dv = _pcall(
        body, grid=(h, s // tq),
        in_specs=[pl.BlockSpec((tq, HEAD_XA), lambda hh, i: (i, q_blk + hh)),
                  pl.BlockSpec((nm, HEAD_XA), lambda hh, i: (0, hh)), pl.BlockSpec((nm, HEAD_XA), lambda hh, i: (0, h + hh)),
                  qspec],
        out_specs=[qspec, pl.BlockSpec((nm, HEAD_XA), lambda hh, i: (0, hh)), pl.BlockSpec((nm, HEAD_XA), lambda hh, i: (0, hh))],
        out_shape=[jax.ShapeDtypeStruct((s, h * HEAD_XA), F32), jax.ShapeDtypeStruct((nm, h * HEAD_XA), F32),
                   jax.ShapeDtypeStruct((nm, h * HEAD_XA), F32)],
        name="xa_bwd", compiler_params=_params(("parallel", "arbitrary")))(proj, mem_kv, mem_kv, do)
    return dq, dk, dv


def _merge_fwd(zg, b_gate, branches):
    s, d = branches[0].shape
    tr = _tile(s, (128,))

    def body(z0, z1, z2, g0, g1, g2, b0, b1, b2, o_ref):
        acc = None
        for z, g, b in ((z0, g0, b0), (z1, g1, b1), (z2, g2, b2)):
            term = jax.nn.sigmoid(z[...] + g[...]) * b[...]
            acc = term if acc is None else acc + term
        o_ref[...] = acc.astype(o_ref.dtype)

    zs = [pl.BlockSpec((tr, d), functools.partial(lambda i, k: (i, k), k=k)) for k in range(3)]
    gs = [pl.BlockSpec((1, d), functools.partial(lambda i, k: (0, k), k=k)) for k in range(3)]
    row = pl.BlockSpec((tr, d), lambda i: (i, 0))
    return _pcall(
        body, grid=(s // tr,), in_specs=zs + gs + [row] * 3, out_specs=row,
        out_shape=jax.ShapeDtypeStruct((s, d), BF16), name="merge_fwd",
        compiler_params=_params(("parallel",)))(zg, zg, zg, b_gate, b_gate, b_gate, *branches)


def _merge_bwd(zg, b_gate, branches, dmerged):
    s, d = branches[0].shape
    tr = _tile(s, (128,))

    def body(z0, z1, z2, g0, g1, g2, b0, b1, b2, dm_ref, dz_ref, d0, d1, d2, dbg_ref):
        @pl.when(pl.program_id(0) == 0)
        def _():
            dbg_ref[...] = jnp.zeros_like(dbg_ref)

        dm = dm_ref[...]
        for k, (z, g, b, dbr) in enumerate(((z0, g0, b0, d0), (z1, g1, b1, d1), (z2, g2, b2, d2))):
            sg = jax.nn.sigmoid(z[...] + g[...])
            dbr[...] = (dm * sg).astype(dbr.dtype)
            dz = dm * b[...] * sg * (1.0 - sg)
            dz_ref[:, k * d:(k + 1) * d] = dz.astype(dz_ref.dtype)
            dbg_ref[:, k * d:(k + 1) * d] += jnp.sum(dz, axis=0, keepdims=True)

    zs = [pl.BlockSpec((tr, d), functools.partial(lambda i, k: (i, k), k=k)) for k in range(3)]
    gs = [pl.BlockSpec((1, d), functools.partial(lambda i, k: (0, k), k=k)) for k in range(3)]
    row = pl.BlockSpec((tr, d), lambda i: (i, 0))
    outs = _pcall(
        body, grid=(s // tr,), in_specs=zs + gs + [row] * 4,
        out_specs=[pl.BlockSpec((tr, 3 * d), lambda i: (i, 0)), row, row, row, pl.BlockSpec((1, 3 * d), lambda i: (0, 0))],
        out_shape=[jax.ShapeDtypeStruct((s, 3 * d), BF16)] + [jax.ShapeDtypeStruct((s, d), BF16)] * 3
        + [jax.ShapeDtypeStruct((1, 3 * d), F32)],
        name="merge_bwd", compiler_params=_params(("arbitrary",)))(zg, zg, zg, b_gate, b_gate, b_gate, *branches, dmerged)
    return outs[0], list(outs[1:4]), outs[4]


def _shift_down(x, k, row):
    return jnp.where(row >= k, pltpu.roll(x, k, 0), 0.0)


def _shift_up(x, k, row, s):
    return jnp.where(row < s - k, pltpu.roll(x, s - k, 0), 0.0)


def _conv_pre(gate, cw_ref, cb_ref, row):
    conv = cb_ref[...] + cw_ref[CONV_TAPS - 1:CONV_TAPS, :] * gate
    for k in range(1, CONV_TAPS):
        conv = conv + cw_ref[CONV_TAPS - 1 - k:CONV_TAPS - k, :] * _shift_down(gate, k, row)
    return conv


def _cg_fwd(up, conv_w, conv_b):
    s = up.shape[0]
    f = conv_w.shape[1]
    tc = _tile(f, (256, 128))
    nb = f // tc

    def body(g_ref, v_ref, cw_ref, cb_ref, o_ref):
        row = lax.broadcasted_iota(jnp.int32, (s, tc), 0)
        conv = _conv_pre(g_ref[...], cw_ref, cb_ref, row)
        o_ref[...] = (_gelu(conv) * v_ref[...]).astype(o_ref.dtype)

    return _pcall(
        body, grid=(nb,),
        in_specs=[pl.BlockSpec((s, tc), lambda j: (0, j)), pl.BlockSpec((s, tc), lambda j: (0, nb + j)),
                  pl.BlockSpec((CONV_TAPS, tc), lambda j: (0, j)), pl.BlockSpec((1, tc), lambda j: (0, j))],
        out_specs=pl.BlockSpec((s, tc), lambda j: (0, j)),
        out_shape=jax.ShapeDtypeStruct((s, f), BF16), name="cg_fwd",
        compiler_params=_params(("parallel",)))(up, up, conv_w, conv_b)


def _cg_bwd(up, conv_w, conv_b, dact):
    s = up.shape[0]
    f = conv_w.shape[1]
    tc = _tile(f, (256, 128))
    nb = f // tc

    def body(g_ref, v_ref, cw_ref, cb_ref, da_ref, dg_ref, dv_ref, dcw_ref, dcb_ref):
        row = lax.broadcasted_iota(jnp.int32, (s, tc), 0)
        gate = g_ref[...]
        conv = _conv_pre(gate, cw_ref, cb_ref, row)
        gel, ggrad = _gelu_and_grad(conv)
        da = da_ref[...]
        dv_ref[...] = (da * gel).astype(dv_ref.dtype)
        dconv = da * v_ref[...] * ggrad
        dgate = cw_ref[CONV_TAPS - 1:CONV_TAPS, :] * dconv
        dcw_ref[CONV_TAPS - 1:CONV_TAPS, :] = jnp.sum(dconv * gate, axis=0, keepdims=True)
        for k in range(1, CONV_TAPS):
            dgate = dgate + cw_ref[CONV_TAPS - 1 - k:CONV_TAPS - k, :] * _shift_up(dconv, k, row, s)
            dcw_ref[CONV_TAPS - 1 - k:CONV_TAPS - k, :] = jnp.sum(dconv * _shift_down(gate, k, row), axis=0, keepdims=True)
        dg_ref[...] = dgate.astype(dg_ref.dtype)
        dcb_ref[...] = jnp.sum(dconv, axis=0, keepdims=True)

    colb = pl.BlockSpec((s, tc), lambda j: (0, j))
    return _pcall(
        body, grid=(nb,),
        in_specs=[colb, pl.BlockSpec((s, tc), lambda j: (0, nb + j)), pl.BlockSpec((CONV_TAPS, tc), lambda j: (0, j)),
                  pl.BlockSpec((1, tc), lambda j: (0, j)), colb],
        out_specs=[colb, colb, pl.BlockSpec((CONV_TAPS, tc), lambda j: (0, j)), pl.BlockSpec((1, tc), lambda j: (0, j))],
        out_shape=[jax.ShapeDtypeStruct((s, f), BF16), jax.ShapeDtypeStruct((s, f), BF16),
                   jax.ShapeDtypeStruct((CONV_TAPS, f), F32), jax.ShapeDtypeStruct((1, f), F32)],
        name="cg_bwd", compiler_params=_params(("parallel",)))(up, up, conv_w, conv_b, dact)


def _row_tile(rows, cols):
    want = max(16, (256 * 1024) // cols)
    for c in (512, 256, 128, 64, 32, 16):
        if c <= want and rows % c == 0:
            return c
    return rows


def _sum_halves(dwv, recv, core, name):
    nj, _, a, c = dwv.shape
    tr = _row_tile(a, c)

    def body(core_ref, d_ref, r_ref, o_ref):
        o_ref[0] = (d_ref[0, 0].astype(F32) + r_ref[0].astype(F32)).astype(o_ref.dtype)

    grid_spec = pltpu.PrefetchScalarGridSpec(
        num_scalar_prefetch=1, grid=(nj, a // tr),
        in_specs=[pl.BlockSpec((1, 1, tr, c), lambda j, i, cr: (j, cr[0], i, 0)),
                  pl.BlockSpec((1, tr, c), lambda j, i, cr: (j, i, 0))],
        out_specs=pl.BlockSpec((1, tr, c), lambda j, i, cr: (j, i, 0)))
    return _pcall(body, grid_spec=grid_spec, out_shape=jax.ShapeDtypeStruct((nj, a, c), BF16), name=name,
                  compiler_params=_params(("parallel", "parallel")))(core, dwv, recv)


def _sum_chips(recv, name):
    _, a, b = recv.shape
    tr = _row_tile(a, b)

    def body(r_ref, o_ref):
        acc = r_ref[0].astype(F32)
        for k in range(1, N_CHIPS):
            acc = acc + r_ref[k].astype(F32)
        o_ref[...] = acc

    return _pcall(
        body, grid=(a // tr,), in_specs=[pl.BlockSpec((N_CHIPS, tr, b), lambda i: (0, i, 0))],
        out_specs=pl.BlockSpec((tr, b), lambda i: (i, 0)),
        out_shape=jax.ShapeDtypeStruct((a, b), F32), name=name, compiler_params=_params(("parallel",)))(recv)


def _adamw(w, m, v, g, name):
    nl, r, c = w.shape
    tr = _row_tile(r, c)
    c1 = 1.0 - ADAM_B1 ** ADAM_STEP
    c2 = 1.0 - ADAM_B2 ** ADAM_STEP

    def body(w_ref, m_ref, v_ref, gin_ref, g_ref, d_ref, nm_ref, nv_ref):
        g = gin_ref[...]
        mm = ADAM_B1 * m_ref[...] + (1.0 - ADAM_B1) * g
        vv = ADAM_B2 * v_ref[...] + (1.0 - ADAM_B2) * (g * g)
        g_ref[...] = g
        nm_ref[...] = mm
        nv_ref[...] = vv
        d_ref[...] = -ADAM_LR * ((mm / c1) / (jnp.sqrt(vv / c2) + ADAM_EPS) + ADAM_WD * w_ref[...])

    blk = pl.BlockSpec((1, tr, c), lambda l, i: (l, i, 0))
    shape = jax.ShapeDtypeStruct((nl, r, c), F32)
    return _pcall(
        body, grid=(nl, r // tr), in_specs=[blk] * 4, out_specs=[blk] * 4, out_shape=[shape] * 4, name=name,
        compiler_params=_params(("parallel", "parallel")))(w, m, v, g)


HBM_SPEC = pl.BlockSpec(memory_space=pltpu.HBM)
COMM = pltpu.CompilerParams(has_side_effects=True)


def _position():
    x, y, c = lax.axis_index("x"), lax.axis_index("y"), lax.axis_index("c")
    chips = [(1 - x, y), (x, 1 - y), (1 - x, 1 - y)]
    return x, y, c, chips


def _remote(src, dst, send_sem, recv_sem, dev):
    return pltpu.make_async_remote_copy(src_ref=src, dst_ref=dst, send_sem=send_sem, recv_sem=recv_sem,
                                        device_id=dev, device_id_type=MESH)


def _full_view_shape(shard_shape, ax):
    _, r, c = shard_shape
    return (2, r // 2, c * N_CHIPS) if ax == 2 else (N_CHIPS, 2, r // 2, c)


def _piece(ref, ax, j, h, cs):
    if ax == 2:
        return ref.at[h, :, pl.ds(pl.multiple_of(j * cs, cs), cs)]
    return ref.at[j, h]


def _chip_block(ref, ax, j, cs):
    if ax == 2:
        return ref.at[:, :, pl.ds(pl.multiple_of(j * cs, cs), cs)]
    return ref.at[j]


def _gather_ici(shards, axes, layer):
    n = len(shards)
    out_shapes = [jax.ShapeDtypeStruct(_full_view_shape((sh.shape[0], sh.shape[1] * sh.shape[2], sh.shape[3]), ax), sh.dtype)
                  for sh, ax in zip(shards, axes)]

    def body(*refs):
        ins, outs = refs[:n], refs[n:2 * n]
        send_sems, recv_sems, local_sems = refs[2 * n:]
        x, y, c, chips = _position()
        me = 2 * x + y
        local, remote = [], []
        for t, ax in enumerate(axes):
            cs = ins[t].shape[3]
            local.append(pltpu.make_async_copy(ins[t].at[layer], _chip_block(outs[t], ax, me, cs), local_sems.at[t]))
            local[-1].start()
            for p, (px, py) in enumerate(chips):
                k = t * 3 + p
                remote.append(_remote(ins[t].at[layer, c], _piece(outs[t], ax, me, c, cs), send_sems.at[k], recv_sems.at[k],
                                      (px, py, c)))
                remote[-1].start()
        for t, ax in enumerate(axes):
            cs = ins[t].shape[3]
            for p, (px, py) in enumerate(chips):
                k = t * 3 + p
                _remote(ins[t].at[layer, c], _piece(outs[t], ax, 2 * px + py, c, cs), send_sems.at[k], recv_sems.at[k],
                        (px, py, c)).wait_recv()
        for cp in remote:
            cp.wait_send()
        for cp in local:
            cp.wait()

    return _pcall(
        body, in_specs=[HBM_SPEC] * n, out_specs=[HBM_SPEC] * n, out_shape=out_shapes,
        scratch_shapes=[pltpu.SemaphoreType.DMA((3 * n,)), pltpu.SemaphoreType.DMA((3 * n,)), pltpu.SemaphoreType.DMA((n,))],
        name="gather_ici", compiler_params=COMM)(*shards)


def _gather_d2d(views, axes, shard_cols):
    n = len(views)

    def body(*refs):
        outs = refs[n:2 * n]
        send_sems, recv_sems = refs[2 * n:]
        x, y, c, chips = _position()
        remote = []
        for t, ax in enumerate(axes):
            for p, (px, py) in enumerate(chips):
                k = t * 3 + p
                blk = _piece(outs[t], ax, 2 * px + py, c, shard_cols[t])
                remote.append(_remote(blk, blk, send_sems.at[k], recv_sems.at[k], (x, y, 1 - c)))
                remote[-1].start()
        for t, ax in enumerate(axes):
            for p, (px, py) in enumerate(chips):
                k = t * 3 + p
                blk = _piece(outs[t], ax, 2 * px + py, 1 - c, shard_cols[t])
                _remote(blk, blk, send_sems.at[k], recv_sems.at[k], (x, y, 1 - c)).wait_recv()
        for cp in remote:
            cp.wait_send()

    return _pcall(
        body, in_specs=[HBM_SPEC] * n, out_specs=[HBM_SPEC] * n,
        out_shape=[jax.ShapeDtypeStruct(v.shape, v.dtype) for v in views],
        input_output_aliases={t: t for t in range(n)},
        scratch_shapes=[pltpu.SemaphoreType.DMA((3 * n,)), pltpu.SemaphoreType.DMA((3 * n,))],
        name="gather_d2d", compiler_params=COMM)(*views)


def _grads_d2d(dwvs):
    n = len(dwvs)

    def body(*refs):
        ins, outs = refs[:n], refs[n:2 * n]
        send_sems, recv_sems = refs[2 * n:]
        x, y, c, _ = _position()
        remote = [_remote(ins[t].at[:, 1 - c], outs[t], send_sems.at[t], recv_sems.at[t], (x, y, 1 - c)) for t in range(n)]
        for cp in remote:
            cp.start()
        for cp in remote:
            cp.wait()

    return _pcall(
        body, in_specs=[HBM_SPEC] * n, out_specs=[HBM_SPEC] * n,
        out_shape=[jax.ShapeDtypeStruct((d.shape[0],) + d.shape[2:], d.dtype) for d in dwvs],
        scratch_shapes=[pltpu.SemaphoreType.DMA((n,)), pltpu.SemaphoreType.DMA((n,))],
        name="grads_d2d", compiler_params=COMM)(*dwvs)


def _grads_ici(sums, axes):
    n = len(sums)
    out_shapes = []
    for sm, ax in zip(sums, axes):
        _, a, c = sm.shape
        out_shapes.append(jax.ShapeDtypeStruct((N_CHIPS, a, c // N_CHIPS if ax == 2 else c), sm.dtype))

    def body(*refs):
        ins, outs = refs[:n], refs[n:2 * n]
        send_sems, recv_sems, local_sems = refs[2 * n:]
        x, y, c, chips = _position()
        me = 2 * x + y

        def block(t, j):
            if axes[t] == 2:
                cs = outs[t].shape[2]
                return ins[t].at[0, :, pl.ds(pl.multiple_of(j * cs, cs), cs)]
            return ins[t].at[j]

        local, remote = [], []
        for t in range(n):
            local.append(pltpu.make_async_copy(block(t, me), outs[t].at[me], local_sems.at[t]))
            local[-1].start()
            for p, (px, py) in enumerate(chips):
                k = t * 3 + p
                remote.append(_remote(block(t, 2 * px + py), outs[t].at[me], send_sems.at[k], recv_sems.at[k], (px, py, c)))
                remote[-1].start()
        for t in range(n):
            for p, (px, py) in enumerate(chips):
                k = t * 3 + p
                peer = 2 * px + py
                _remote(block(t, peer), outs[t].at[peer], send_sems.at[k], recv_sems.at[k], (px, py, c)).wait_recv()
        for cp in remote:
            cp.wait_send()
        for cp in local:
            cp.wait()

    return _pcall(
        body, in_specs=[HBM_SPEC] * n, out_specs=[HBM_SPEC] * n, out_shape=out_shapes,
        scratch_shapes=[pltpu.SemaphoreType.DMA((3 * n,)), pltpu.SemaphoreType.DMA((3 * n,)), pltpu.SemaphoreType.DMA((n,))],
        name="grads_ici", compiler_params=COMM)(*sums)


def _join_halves(halves):
    n = len(halves)
    nl = len(halves[0])
    flat = [h for hs in halves for h in hs]

    def body(*refs):
        ins, outs = refs[:n * nl], refs[n * nl:n * nl + n]
        send_sems, recv_sems, local_sems = refs[n * nl + n:]
        x, y, c, _ = _position()
        local, remote = [], []
        for t in range(n):
            for l in range(nl):
                k = t * nl + l
                local.append(pltpu.make_async_copy(ins[k], outs[t].at[l, c], local_sems.at[k]))
                local[-1].start()
                remote.append(_remote(ins[k], outs[t].at[l, c], send_sems.at[k], recv_sems.at[k], (x, y, 1 - c)))
                remote[-1].start()
        for t in range(n):
            for l in range(nl):
                k = t * nl + l
                _remote(ins[k], outs[t].at[l, 1 - c], send_sems.at[k], recv_sems.at[k], (x, y, 1 - c)).wait_recv()
        for cp in remote:
            cp.wait_send()
        for cp in local:
            cp.wait()

    return _pcall(
        body, in_specs=[HBM_SPEC] * (n * nl), out_specs=[HBM_SPEC] * n,
        out_shape=[jax.ShapeDtypeStruct((nl, 2) + hs[0].shape, hs[0].dtype) for hs in halves],
        scratch_shapes=[pltpu.SemaphoreType.DMA((n * nl,)), pltpu.SemaphoreType.DMA((n * nl,)), pltpu.SemaphoreType.DMA((n * nl,))],
        name="join_halves", compiler_params=COMM)(*flat)


def _gather_small(shard):
    nl, r, cs = shard.shape

    def body(in_ref, out_ref, send_sems, recv_sems, local_sem):
        x, y, c, chips = _position()

        def cols(j):
            return out_ref.at[:, :, pl.ds(pl.multiple_of(j * cs, cs), cs)]

        me = 2 * x + y
        loc = pltpu.make_async_copy(in_ref, cols(me), local_sem)
        loc.start()
        remote = [_remote(in_ref, cols(me), send_sems.at[p], recv_sems.at[p], (px, py, c)) for p, (px, py) in enumerate(chips)]
        for cp in remote:
            cp.start()
        for p, (px, py) in enumerate(chips):
            _remote(in_ref, cols(2 * px + py), send_sems.at[p], recv_sems.at[p], (px, py, c)).wait_recv()
        for cp in remote:
            cp.wait_send()
        loc.wait()

    return _pcall(
        body, in_specs=[HBM_SPEC], out_specs=HBM_SPEC, out_shape=jax.ShapeDtypeStruct((nl, r, cs * N_CHIPS), shard.dtype),
        scratch_shapes=[pltpu.SemaphoreType.DMA((3,)), pltpu.SemaphoreType.DMA((3,)), pltpu.SemaphoreType.DMA(())],
        name="gather_small", compiler_params=COMM)(shard)


def _all_reduce_small(pack):
    r, c = pack.shape

    def body(in_ref, out_ref, slots, send_sems, recv_sems, local_sem):
        x, y, cc, _ = _position()
        me = 4 * x + 2 * y + cc
        peers = []
        for k in range(1, N_DEV):
            fx, fy, fc = (k >> 2) & 1, (k >> 1) & 1, k & 1
            px, py, pc = x ^ fx, y ^ fy, cc ^ fc
            peers.append((k - 1, (px, py, pc), 4 * px + 2 * py + pc))
        loc = pltpu.make_async_copy(in_ref, slots.at[me], local_sem)
        loc.start()
        copies = [_remote(in_ref, slots.at[me], send_sems.at[k], recv_sems.at[k], dev) for k, dev, _ in peers]
        for cp in copies:
            cp.start()
        for k, dev, idx in peers:
            _remote(in_ref, slots.at[idx], send_sems.at[k], recv_sems.at[k], dev).wait_recv()
        for cp in copies:
            cp.wait_send()
        loc.wait()
        acc = slots[0]
        for k in range(1, N_DEV):
            acc = acc + slots[k]
        out_ref[...] = acc

    vm = pl.BlockSpec(memory_space=pltpu.VMEM)
    return _pcall(
        body, in_specs=[vm], out_specs=vm, out_shape=jax.ShapeDtypeStruct((r, c), F32),
        scratch_shapes=[pltpu.VMEM((N_DEV, r, c), F32), pltpu.SemaphoreType.DMA((N_DEV - 1,)),
                        pltpu.SemaphoreType.DMA((N_DEV - 1,)), pltpu.SemaphoreType.DMA(())],
        name="all_reduce_small", compiler_params=pltpu.CompilerParams(has_side_effects=True, vmem_limit_bytes=VMEM_LIMIT))(pack)


def _dims(d):
    half = d // 2
    return half // HEAD_SB, half // HEAD_XA, 3, (5 * half) // HEAD_XA


def _layer_fwd(x, mem, full, small, l):
    h_sb, h_xa, u_blk, q_blk = _dims(x.shape[1])

    def vec(name):
        return small[name][l].reshape(1, -1)

    h1 = _norm_fwd(x, vec('g_mix_pre'), None, BF16, "norm_mix_pre")
    proj = _mm(h1, full['w_in'], 'nn', F32, "mm_proj")
    o_sb = _sb_fwd(proj, h_sb)
    b_st = small['b_s'][l].T
    o_gm = _gm_fwd(proj, vec('g_vnorm'), small['w_s'][l], b_st, u_blk)
    memn = _norm_fwd(mem, vec('g_mem'), None, BF16, "norm_mem")
    mem_kv = _mm(memn, full['w_mem_kv'], 'nn', F32, "mm_mem_kv")
    o_xa = _xa_fwd(proj, mem_kv, q_blk, h_xa)
    zg = _mm(h1, full['w_gate'], 'nn', F32, "mm_gate")
    branches = [_mm(o, full[wn], 'nn', F32, "mm_branch")
                for o, wn in ((o_sb, 'w_br_sb'), (o_gm, 'w_br_gm'), (o_xa, 'w_br_xa'))]
    merged = _merge_fwd(zg, vec('b_gate'), branches)
    y1 = _mm(merged, full['w_out'], 'nn', F32, "mm_out")
    x1 = _norm_fwd(y1, vec('g_mix_post'), x, F32, "norm_mix_post")
    h2 = _norm_fwd(x1, vec('g_ffn_pre'), None, BF16, "norm_ffn_pre")
    up = _mm(h2, full['w_up'], 'nn', F32, "mm_up")
    act = _cg_fwd(up, full['conv_w'], vec('conv_b'))
    y2 = _mm(act, full['w_down'], 'nn', F32, "mm_down")
    x2 = _norm_fwd(y2, vec('g_ffn_post'), x1, F32, "norm_ffn_post")
    saved = dict(x0=x, h1=h1, proj=proj, o_sb=o_sb, o_gm=o_gm, o_xa=o_xa, memn=memn, mem_kv=mem_kv, zg=zg,
                 branches=branches, merged=merged, y1=y1, x1=x1, h2=h2, up=up, act=act, y2=y2, b_st=b_st)
    return x2, saved


def _layer_bwd(dx, mem, sv, full, small, l):
    h_sb, h_xa, u_blk, q_blk = _dims(dx.shape[1])

    def vec(name):
        return small[name][l].reshape(1, -1)

    gb, gs = {}, {}
    dy2, gs['g_ffn_post'] = _norm_bwd(sv['y2'], vec('g_ffn_post'), [dx], None, BF16, "norm_ffn_post_bwd")
    gb['w_down'] = _mm(sv['act'], dy2, 'tn', BF16, "mm_down_dw")
    dact = _mm(dy2, full['w_down'], 'nt', F32, "mm_down_dx")
    dgate, dval, gs['conv_w'], gs['conv_b'] = _cg_bwd(sv['up'], full['conv_w'], vec('conv_b'), dact)
    dup = jnp.concatenate([dgate, dval], axis=1)
    gb['w_up'] = _mm(sv['h2'], dup, 'tn', BF16, "mm_up_dw")
    dh2 = _mm(dup, full['w_up'], 'nt', F32, "mm_up_dx")
    dx1, gs['g_ffn_pre'] = _norm_bwd(sv['x1'], vec('g_ffn_pre'), [dh2], dx, F32, "norm_ffn_pre_bwd")
    dy1, gs['g_mix_post'] = _norm_bwd(sv['y1'], vec('g_mix_post'), [dx1], None, BF16, "norm_mix_post_bwd")
    gb['w_out'] = _mm(sv['merged'], dy1, 'tn', BF16, "mm_out_dw")
    dmerged = _mm(dy1, full['w_out'], 'nt', F32, "mm_out_dx")
    dzg, dbr, gs['b_gate'] = _merge_bwd(sv['zg'], vec('b_gate'), sv['branches'], dmerged)
    douts = []
    for o, db, wn in ((sv['o_sb'], dbr[0], 'w_br_sb'), (sv['o_gm'], dbr[1], 'w_br_gm'), (sv['o_xa'], dbr[2], 'w_br_xa')):
        gb[wn] = _mm(o, db, 'tn', BF16, "mm_branch_dw")
        douts.append(_mm(db, full[wn], 'nt', F32, "mm_branch_dx"))
    gb['w_gate'] = _mm(sv['h1'], dzg, 'tn', BF16, "mm_gate_dw")
    dh1_gate = _mm(dzg, full['w_gate'], 'nt', F32, "mm_gate_dx")
    dq_xa, dk_xa, dv_xa = _xa_bwd(sv['proj'], sv['mem_kv'], douts[2], q_blk, h_xa)
    dmem_kv = jnp.concatenate([dk_xa, dv_xa], axis=1).astype(BF16)
    gb['w_mem_kv'] = _mm(sv['memn'], dmem_kv, 'tn', BF16, "mm_mem_kv_dw")
    dmemn = _mm(dmem_kv, full['w_mem_kv'], 'nt', F32, "mm_mem_kv_dx")
    _, gs['g_mem'] = _norm_bwd(mem, vec('g_mem'), [dmemn], None, BF16, "norm_mem_bwd")
    du, dv, gs['g_vnorm'], gs['w_s'], db_st = _gm_bwd(sv['proj'], vec('g_vnorm'), small['w_s'][l], sv['b_st'], douts[1], u_blk)
    gs['b_s'] = db_st.T
    dq, dk, dvv = _sb_bwd(sv['proj'], douts[0], h_sb)
    dproj = jnp.concatenate([dq, dk, dvv, du, dv, dq_xa], axis=1).astype(BF16)
    gb['w_in'] = _mm(sv['h1'], dproj, 'tn', BF16, "mm_proj_dw")
    dh1_proj = _mm(dproj, full['w_in'], 'nt', F32, "mm_proj_dx")
    dx0, gs['g_mix_pre'] = _norm_bwd(sv['x0'], vec('g_mix_pre'), [dh1_gate, dh1_proj], dx1, F32, "norm_mix_pre_bwd")
    return dx0, gb, gs


def _local_step(x, mem, target, full, small):
    n_layers = len(full['w_in'])
    saved = []
    for l in range(n_layers):
        x, sv = _layer_fwd(x, mem, {n: full[n][l] for n in full}, small, l)
        saved.append(sv)
    sq, dx = _loss_head(x, target)
    gbig = {n: [None] * n_layers for n in BIG}
    gsmall = {n: [None] * n_layers for n in SMALL + ['conv_w']}
    for l in reversed(range(n_layers)):
        dx, gb, gs = _layer_bwd(dx, mem, saved[l], {n: full[n][l] for n in full}, small, l)
        for n in gb:
            gbig[n][l] = gb[n]
        for n in gs:
            gsmall[n][l] = gs[n]
    return sq, dx, gbig, gsmall


def _pack(arrays, rows_multiple):
    flat = jnp.concatenate([a.reshape(-1).astype(F32) for a in arrays])
    rows = -(-flat.shape[0] // LANES)
    rows = -(-rows // rows_multiple) * rows_multiple
    return jnp.pad(flat, (0, rows * LANES - flat.shape[0])).reshape(rows, LANES)


def _unpack(pack, like):
    flat = pack.reshape(-1)
    out, off = [], 0
    for a in like:
        out.append(flat[off:off + a.size].reshape(a.shape))
        off += a.size
    return out


def _halved(a):
    nl, r, c = a.shape
    return a.reshape(nl, 2, r // 2, c)


def _grad_view(g, ax):
    r, c = g.shape
    return g.reshape(1, 2, r // 2, c) if ax == 2 else g.reshape(N_CHIPS, 2, r // (2 * N_CHIPS), c)


def kernel(x, mem, g_mix_pre, w_in, g_vnorm, w_s, b_s, g_mem, w_mem_kv, w_gate, b_gate, w_br_sb, w_br_gm, w_br_xa, w_out, g_mix_post, g_ffn_pre, w_up, conv_w, conv_b, w_down, g_ffn_post, loss_target, m_g_mix_pre, m_w_in, m_g_vnorm, m_w_s, m_b_s, m_g_mem, m_w_mem_kv, m_w_gate, m_b_gate, m_w_br_sb, m_w_br_gm, m_w_br_xa, m_w_out, m_g_mix_post, m_g_ffn_pre, m_w_up, m_conv_w, m_conv_b, m_w_down, m_g_ffn_post, v_g_mix_pre, v_w_in, v_g_vnorm, v_w_s, v_b_s, v_g_mem, v_w_mem_kv, v_w_gate, v_b_gate, v_w_br_sb, v_w_br_gm, v_w_br_xa, v_w_out, v_g_mix_post, v_g_ffn_pre, v_w_up, v_conv_w, v_conv_b, v_w_down, v_g_ffn_post):
    w = dict(g_mix_pre=g_mix_pre, w_in=w_in, g_vnorm=g_vnorm, w_s=w_s, b_s=b_s, g_mem=g_mem, w_mem_kv=w_mem_kv,
             w_gate=w_gate, b_gate=b_gate, w_br_sb=w_br_sb, w_br_gm=w_br_gm, w_br_xa=w_br_xa, w_out=w_out,
             g_mix_post=g_mix_post, g_ffn_pre=g_ffn_pre, w_up=w_up, conv_w=conv_w, conv_b=conv_b, w_down=w_down,
             g_ffn_post=g_ffn_post)
    m = dict(g_mix_pre=m_g_mix_pre, w_in=m_w_in, g_vnorm=m_g_vnorm, w_s=m_w_s, b_s=m_b_s, g_mem=m_g_mem,
             w_mem_kv=m_w_mem_kv, w_gate=m_w_gate, b_gate=m_b_gate, w_br_sb=m_w_br_sb, w_br_gm=m_w_br_gm,
             w_br_xa=m_w_br_xa, w_out=m_w_out, g_mix_post=m_g_mix_post, g_ffn_pre=m_g_ffn_pre, w_up=m_w_up,
             conv_w=m_conv_w, conv_b=m_conv_b, w_down=m_w_down, g_ffn_post=m_g_ffn_post)
    v = dict(g_mix_pre=v_g_mix_pre, w_in=v_w_in, g_vnorm=v_g_vnorm, w_s=v_w_s, b_s=v_b_s, g_mem=v_g_mem,
             w_mem_kv=v_w_mem_kv, w_gate=v_w_gate, b_gate=v_b_gate, w_br_sb=v_w_br_sb, w_br_gm=v_w_br_gm,
             w_br_xa=v_w_br_xa, w_out=v_w_out, g_mix_post=v_g_mix_post, g_ffn_pre=v_g_ffn_pre, w_up=v_w_up,
             conv_w=v_conv_w, conv_b=v_conv_b, w_down=v_w_down, g_ffn_post=v_g_ffn_post)
    n_layers = w_in.shape[0]
    d = x.shape[-1]
    axes = [BIG_AXIS[n] for n in BIG]
    core = lax.axis_index("c").astype(jnp.int32).reshape(1)
    small = {n: w[n] for n in SMALL}
    xs, mems, target = x[0], mem[0], loss_target[0]

    shards = [_halved(w[n].astype(BF16)) for n in BIG]
    shard_cols = [sh.shape[3] for sh in shards]
    conv_w_full = _gather_small(conv_w)

    def gather(l):
        views = _gather_d2d(_gather_ici(shards, axes, l), axes, shard_cols)
        full = {n: vw.reshape(-1, vw.shape[-1]) if ax == 1 else vw.reshape(vw.shape[0] * vw.shape[1], vw.shape[2])
                for n, vw, ax in zip(BIG, views, axes)}
        full['conv_w'] = conv_w_full[l]
        return full

    def scatter(gb):
        dwvs = [_grad_view(gb[n], ax) for n, ax in zip(BIG, axes)]
        theirs = _grads_d2d(dwvs)
        sums = [_sum_halves(dv, th, core, "sum_halves") for dv, th in zip(dwvs, theirs)]
        return [_sum_chips(r, "sum_chips") for r in _grads_ici(sums, axes)]

    fulls, saved = [], []
    for l in range(n_layers):
        fulls.append(gather(l))
        xs, sv = _layer_fwd(xs, mems, fulls[l], small, l)
        saved.append(sv)
    sq, dx = _loss_head(xs, target)
    loss = lax.psum(0.5 * jnp.sum(sq) / d, ("x", "y", "c"))

    halves = [None] * n_layers
    gsmall = {n: [None] * n_layers for n in SMALL + ['conv_w']}
    for l in reversed(range(n_layers)):
        dx, gb, gs = _layer_bwd(dx, mems, saved[l], fulls[l], small, l)
        halves[l] = scatter(gb)
        for n in gs:
            gsmall[n][l] = gs[n]
    joined = _join_halves([[halves[l][t] for l in range(n_layers)] for t in range(len(BIG))])

    small_full = [jnp.stack(gsmall[n]).reshape(w[n].shape) for n in SMALL]
    conv_w_grad = jnp.stack(gsmall['conv_w'])
    summed = _all_reduce_small(_pack(small_full + [conv_w_grad], 8))
    *small_g, conv_w_g = _unpack(summed, small_full + [conv_w_grad])
    shard = conv_w.shape[-1]
    chip = 2 * lax.axis_index("x") + lax.axis_index("y")
    conv_w_g = lax.dynamic_slice_in_dim(conv_w_g, chip * shard, shard, axis=2)

    out = {}
    for n, g in zip(BIG, joined):
        out[n] = _adamw(w[n], m[n], v[n], g.reshape(w[n].shape), "adamw_big")
    names = SMALL + ['conv_w']
    packed = [_pack([p[n] for n in names], 256) for p in (w, m, v)]
    gpack = _pack(small_g + [conv_w_g], 256)
    res = _adamw(packed[0][None], packed[1][None], packed[2][None], gpack[None], "adamw_small")
    like = [w[n] for n in names]
    unpacked = [_unpack(r[0], like) for r in res]
    for i, n in enumerate(names):
        out[n] = tuple(u[i] for u in unpacked)

    return (loss, dx[None], *[out[n][0] for n in WEIGHTS], *[out[n][1] for n in WEIGHTS],
            *[out[n][2] for n in WEIGHTS], *[out[n][3] for n in WEIGHTS])
```

```python
import functools
import math

import jax
import jax.numpy as jnp
from jax import lax
from jax.experimental import pallas as pl
from jax.experimental.pallas import tpu as pltpu

F32 = jnp.float32
BF16 = jnp.bfloat16
EPS = 1e-6
HEAD_SB = 128
GROUP_GM = 128
CHUNK = 64
HEAD_XA = 256
CONV_TAPS = 3
N_CHIPS = 4
N_DEV = 8
LANES = 128
MIB = 1024 * 1024
VMEM_LIMIT = 48 * MIB
SPLITS = 3

ADAM_LR = 0.001
ADAM_B1 = 0.9
ADAM_B2 = 0.999
ADAM_EPS = 1e-08
ADAM_WD = 0.01
ADAM_STEP = 10

WEIGHTS = ['g_mix_pre', 'w_in', 'g_vnorm', 'w_s', 'b_s', 'g_mem', 'w_mem_kv', 'w_gate', 'b_gate', 'w_br_sb',
           'w_br_gm', 'w_br_xa', 'w_out', 'g_mix_post', 'g_ffn_pre', 'w_up', 'conv_w', 'conv_b', 'w_down',
           'g_ffn_post']
BIG_AXIS = {'w_in': 2, 'w_mem_kv': 1, 'w_gate': 2, 'w_br_sb': 2, 'w_br_gm': 2, 'w_br_xa': 2, 'w_out': 1,
            'w_up': 2, 'w_down': 1}
BIG = list(BIG_AXIS)
SMALL = ['g_mix_pre', 'g_vnorm', 'w_s', 'b_s', 'g_mem', 'b_gate', 'g_mix_post', 'g_ffn_pre', 'conv_b', 'g_ffn_post']
MESH = pl.DeviceIdType.MESH


def _pcall(body, **kw):
    return pl.pallas_call(body, **kw)


def _params(sem=None, vmem=VMEM_LIMIT):
    return pltpu.CompilerParams(dimension_semantics=sem, vmem_limit_bytes=vmem)


def _tile(n, cands):
    for c in cands:
        if n % c == 0:
            return c
    return n


_GELU_C = math.sqrt(2.0 / math.pi)
_GELU_A = 0.044715


def _gelu(x):
    return 0.5 * x * (1.0 + jnp.tanh(_GELU_C * (x + _GELU_A * (x * x * x))))


def _gelu_and_grad(x):
    x2 = x * x
    t = jnp.tanh(_GELU_C * (x + _GELU_A * (x2 * x)))
    val = 0.5 * x * (1.0 + t)
    grad = 0.5 * (1.0 + t) + 0.5 * x * (1.0 - t * t) * (_GELU_C * (1.0 + 3.0 * _GELU_A * x2))
    return val, grad


def _softplus(z):
    return jnp.maximum(z, 0.0) + jnp.log1p(jnp.exp(-jnp.abs(z)))


def _dot(a, b):
    return jnp.dot(a, b, preferred_element_type=F32)


def _dot_nt(a, b):
    return lax.dot_general(a, b, (((1,), (1,)), ((), ())), preferred_element_type=F32)


def _dot_tn(a, b):
    return lax.dot_general(a, b, (((0,), (0,)), ((), ())), preferred_element_type=F32)


def _split_dot(a, m):
    out = None
    rest = a
    for _ in range(SPLITS):
        piece = rest.astype(BF16)
        rest = rest - piece.astype(F32)
        term = _dot(piece, m)
        out = term if out is None else out + term
    return out


def _mm(a, b, mode, out_dtype, name, tm=None, tn=None, tk=None):
    if mode == 'nn':
        (m, kc), (kc2, n) = a.shape, b.shape
    elif mode == 'nt':
        (m, kc), (n, kc2) = a.shape, b.shape
    else:
        (kc, m), (kc2, n) = a.shape, b.shape
    assert kc == kc2, (a.shape, b.shape, mode)
    tm = tm or _tile(m, (1024, 512, 256, 128))
    tn = tn or _tile(n, (1024, 512, 256, 128))
    tk = tk or (kc if kc <= 2048 else _tile(kc, (2048, 1536, 1408, 1024, 512)))
    nk = kc // tk
    dot = {'nn': _dot, 'nt': _dot_nt, 'tn': _dot_tn}[mode]
    a_spec = pl.BlockSpec((tk, tm), lambda i, j, k: (k, i)) if mode == 'tn' else pl.BlockSpec((tm, tk), lambda i, j, k: (i, k))
    b_spec = pl.BlockSpec((tn, tk), lambda i, j, k: (j, k)) if mode == 'nt' else pl.BlockSpec((tk, tn), lambda i, j, k: (k, j))

    if nk == 1:
        def body(a_ref, b_ref, o_ref):
            o_ref[...] = dot(a_ref[...].astype(BF16), b_ref[...].astype(BF16)).astype(o_ref.dtype)
        scratch = []
    else:
        def body(a_ref, b_ref, o_ref, acc_ref):
            k = pl.program_id(2)
            part = dot(a_ref[...].astype(BF16), b_ref[...].astype(BF16))

            @pl.when(k == 0)
            def _():
                acc_ref[...] = part

            @pl.when(k > 0)
            def _():
                acc_ref[...] += part

            @pl.when(k == nk - 1)
            def _():
                o_ref[...] = acc_ref[...].astype(o_ref.dtype)
        scratch = [pltpu.VMEM((tm, tn), F32)]

    return _pcall(
        body, grid=(m // tm, n // tn, nk), in_specs=[a_spec, b_spec],
        out_specs=pl.BlockSpec((tm, tn), lambda i, j, k: (i, j)),
        out_shape=jax.ShapeDtypeStruct((m, n), out_dtype), scratch_shapes=scratch, name=name,
        compiler_params=_params(("parallel", "parallel", "arbitrary")))(a, b)


def _norm_fwd(x, g, res, out_dtype, name):
    s, d = x.shape
    tr = _tile(s, (256, 128))
    has_res = res is not None

    def body(*refs):
        x_ref, g_ref = refs[0], refs[1]
        o_ref = refs[-1]
        xv = x_ref[...]
        y = xv * lax.rsqrt(jnp.mean(xv * xv, axis=-1, keepdims=True) + EPS) * g_ref[...]
        if has_res:
            y = y + refs[2][...]
        o_ref[...] = y.astype(o_ref.dtype)

    row = pl.BlockSpec((tr, d), lambda i: (i, 0))
    ins = [x, g] + ([res] if has_res else [])
    return _pcall(
        body, grid=(s // tr,), in_specs=[row, pl.BlockSpec((1, d), lambda i: (0, 0))] + ([row] if has_res else []),
        out_specs=row, out_shape=jax.ShapeDtypeStruct((s, d), out_dtype), name=name,
        compiler_params=_params(("parallel",)))(*ins)


def _norm_bwd(x, g, douts, dres, out_dtype, name):
    s, d = x.shape
    tr = _tile(s, (256, 128))
    nd = len(douts)
    has_res = dres is not None

    def body(*refs):
        x_ref, g_ref = refs[0], refs[1]
        dx_ref, dg_ref = refs[-2], refs[-1]
        dout = refs[2][...].astype(F32)
        for r in refs[3:2 + nd]:
            dout = dout + r[...].astype(F32)
        xv = x_ref[...]
        r = lax.rsqrt(jnp.mean(xv * xv, axis=-1, keepdims=True) + EPS)
        n = xv * r
        dn = dout * g_ref[...]
        dx = r * (dn - n * jnp.mean(dn * n, axis=-1, keepdims=True))
        if has_res:
            dx = dx + refs[2 + nd][...]
        dx_ref[...] = dx.astype(dx_ref.dtype)

        @pl.when(pl.program_id(0) == 0)
        def _():
            dg_ref[...] = jnp.zeros_like(dg_ref)

        dg_ref[...] += jnp.sum(dout * n, axis=0, keepdims=True)

    row = pl.BlockSpec((tr, d), lambda i: (i, 0))
    vec = pl.BlockSpec((1, d), lambda i: (0, 0))
    ins = [x, g] + list(douts) + ([dres] if has_res else [])
    return _pcall(
        body, grid=(s // tr,), in_specs=[row, vec] + [row] * (nd + int(has_res)), out_specs=[row, vec],
        out_shape=[jax.ShapeDtypeStruct((s, d), out_dtype), jax.ShapeDtypeStruct((1, d), F32)], name=name,
        compiler_params=_params(("arbitrary",)))(*ins)


def _loss_head(y, target):
    s, d = y.shape
    tr = _tile(s, (256, 128))

    def body(y_ref, t_ref, sq_ref, dy_ref):
        e = y_ref[...] - t_ref[...]
        dy_ref[...] = e * (1.0 / d)

        @pl.when(pl.program_id(0) == 0)
        def _():
            sq_ref[...] = jnp.zeros_like(sq_ref)

        sq_ref[...] += jnp.sum(e * e, axis=0, keepdims=True)

    row = pl.BlockSpec((tr, d), lambda i: (i, 0))
    return _pcall(
        body, grid=(s // tr,), in_specs=[row, row], out_specs=[pl.BlockSpec((1, d), lambda i: (0, 0)), row],
        out_shape=[jax.ShapeDtypeStruct((1, d), F32), jax.ShapeDtypeStruct((s, d), F32)], name="loss_head",
        compiler_params=_params(("arbitrary",)))(y, target)


def _sb_masks():
    row = lax.broadcasted_iota(jnp.int32, (HEAD_SB, HEAD_SB), 0)
    col = lax.broadcasted_iota(jnp.int32, (HEAD_SB, HEAD_SB), 1)
    return row, col


def _sb_fwd(proj, n_heads):
    s = proj.shape[0]
    nq = s // HEAD_SB
    scale = HEAD_SB ** -0.5

    def body(q_ref, k_ref, v_ref, o_ref):
        i = pl.program_id(1)
        q = q_ref[...].astype(BF16)
        row, col = _sb_masks()
        m_strict = (row > col).astype(BF16)

        def step(jj, carry):
            c, acc = carry
            off = pl.multiple_of((i - jj) * HEAD_SB, HEAD_SB)
            kb = k_ref[pl.ds(off, HEAD_SB), :].astype(BF16)
            vb = v_ref[pl.ds(off, HEAD_SB), :].astype(BF16)
            z = _dot_nt(q, kb) * scale
            mask = (off + col) < (i * HEAD_SB + row)
            sp = _softplus(z)
            lk = jnp.where(mask, -sp, 0.0)
            suffix = _split_dot(lk, m_strict) + c
            a = jnp.where(mask, jnp.exp(z - sp + suffix), 0.0)
            acc = acc + _dot(a.astype(BF16), vb)
            return c + jnp.sum(lk, axis=1, keepdims=True), acc

        _, acc = lax.fori_loop(0, i + 1, step, (jnp.zeros((HEAD_SB, 1), F32), jnp.zeros((HEAD_SB, HEAD_SB), F32)))
        o_ref[...] = acc.astype(BF16)

    h = n_heads
    blk = pl.BlockSpec((HEAD_SB, HEAD_SB), lambda hh, i: (i, hh))
    return _pcall(
        body, grid=(h, nq),
        in_specs=[blk, pl.BlockSpec((s, HEAD_SB), lambda hh, i: (0, h + hh)),
                  pl.BlockSpec((s, HEAD_SB), lambda hh, i: (0, 2 * h + hh))],
        out_specs=blk, out_shape=jax.ShapeDtypeStruct((s, h * HEAD_SB), BF16),
        name="sb_fwd", compiler_params=_params(("parallel", "arbitrary")))(proj, proj, proj)


def _sb_bwd(proj, do, n_heads):
    s = proj.shape[0]
    nq = s // HEAD_SB
    scale = HEAD_SB ** -0.5

    def body(q_ref, k_ref, v_ref, do_ref, dq_ref, dk_ref, dv_ref, g_ref, beta_ref):
        i = pl.program_id(1)

        @pl.when(i == 0)
        def _():
            dk_ref[...] = jnp.zeros_like(dk_ref)
            dv_ref[...] = jnp.zeros_like(dv_ref)

        q = q_ref[...].astype(BF16)
        dob = do_ref[...].astype(BF16)
        row, col = _sb_masks()
        m_strict = (row > col).astype(BF16)
        m_before = (row < col).astype(BF16)

        def sweep_left(jj, c):
            j = i - jj
            off = pl.multiple_of(j * HEAD_SB, HEAD_SB)
            kb = k_ref[pl.ds(off, HEAD_SB), :].astype(BF16)
            vb = v_ref[pl.ds(off, HEAD_SB), :].astype(BF16)
            z = _dot_nt(q, kb) * scale
            mask = (off + col) < (i * HEAD_SB + row)
            sp = _softplus(z)
            lk = jnp.where(mask, -sp, 0.0)
            suffix = _split_dot(lk, m_strict) + c
            beta = jnp.exp(z - sp)
            a = jnp.where(mask, beta * jnp.exp(suffix), 0.0)
            g_ref[j] = a * _dot_nt(dob, vb)
            beta_ref[j] = beta
            dv_ref[pl.ds(off, HEAD_SB), :] += _dot_tn(a.astype(BF16), dob)
            return c + jnp.sum(lk, axis=1, keepdims=True)

        lax.fori_loop(0, i + 1, sweep_left, jnp.zeros((HEAD_SB, 1), F32))

        def sweep_right(j, carry):
            p, dq = carry
            off = pl.multiple_of(j * HEAD_SB, HEAD_SB)
            kb = k_ref[pl.ds(off, HEAD_SB), :].astype(BF16)
            g = g_ref[j]
            beta = beta_ref[j]
            mask = (off + col) < (i * HEAD_SB + row)
            before = _split_dot(g, m_before) + p
            dz = jnp.where(mask, g * (1.0 - beta) - beta * before, 0.0) * scale
            dzb = dz.astype(BF16)
            dk_ref[pl.ds(off, HEAD_SB), :] += _dot_tn(dzb, q)
            return p + jnp.sum(g, axis=1, keepdims=True), dq + _dot(dzb, kb)

        _, dq = lax.fori_loop(0, i + 1, sweep_right, (jnp.zeros((HEAD_SB, 1), F32), jnp.zeros((HEAD_SB, HEAD_SB), F32)))
        dq_ref[...] = dq

    h = n_heads
    blk = pl.BlockSpec((HEAD_SB, HEAD_SB), lambda hh, i: (i, hh))
    col_blk = pl.BlockSpec((s, HEAD_SB), lambda hh, i: (0, hh))
    shape = jax.ShapeDtypeStruct((s, h * HEAD_SB), F32)
    return _pcall(
        body, grid=(h, nq),
        in_specs=[blk, pl.BlockSpec((s, HEAD_SB), lambda hh, i: (0, h + hh)),
                  pl.BlockSpec((s, HEAD_SB), lambda hh, i: (0, 2 * h + hh)), blk],
        out_specs=[blk, col_blk, col_blk], out_shape=[shape, shape, shape],
        scratch_shapes=[pltpu.VMEM((nq, HEAD_SB, HEAD_SB), F32), pltpu.VMEM((nq, HEAD_SB, HEAD_SB), F32)],
        name="sb_bwd", compiler_params=_params(("parallel", "arbitrary")))(proj, proj, proj, do)


def _gm_mask():
    t = lax.broadcasted_iota(jnp.int32, (GROUP_GM, GROUP_GM), 0)
    s = lax.broadcasted_iota(jnp.int32, (GROUP_GM, GROUP_GM), 1)
    shift = CHUNK.bit_length() - 1
    return (s >> shift) <= (t >> shift)


def _gm_fwd(proj, g_vnorm, w_s, b_st, u_blk):
    s = proj.shape[0]
    groups = w_s.shape[0]
    w = groups * GROUP_GM

    def body(u_ref, v_ref, gv_ref, ws_ref, bst_ref, o_ref):
        ug = _gelu(u_ref[...])
        vg = _gelu(v_ref[...])
        vn = vg * lax.rsqrt(jnp.mean(vg * vg, axis=-1, keepdims=True) + EPS) * gv_ref[...]
        vnb = vn.astype(BF16)
        mask = _gm_mask()
        for g in range(groups):
            sl = slice(g * GROUP_GM, (g + 1) * GROUP_GM)
            wm = jnp.where(mask, ws_ref[g], 0.0).astype(BF16)
            mixed = _dot(wm, vnb[:, sl]) + bst_ref[:, g:g + 1]
            o_ref[:, sl] = (ug[:, sl] * mixed).astype(o_ref.dtype)

    return _pcall(
        body, grid=(s // GROUP_GM,),
        in_specs=[pl.BlockSpec((GROUP_GM, w), lambda c: (c, u_blk)), pl.BlockSpec((GROUP_GM, w), lambda c: (c, u_blk + 1)),
                  pl.BlockSpec((1, w), lambda c: (0, 0)), pl.BlockSpec((groups, GROUP_GM, GROUP_GM), lambda c: (0, 0, 0)),
                  pl.BlockSpec((GROUP_GM, groups), lambda c: (0, 0))],
        out_specs=pl.BlockSpec((GROUP_GM, w), lambda c: (c, 0)),
        out_shape=jax.ShapeDtypeStruct((s, w), BF16), name="gm_fwd",
        compiler_params=_params(("parallel",)))(proj, proj, g_vnorm, w_s, b_st)


def _gm_bwd(proj, g_vnorm, w_s, b_st, do, u_blk):
    s = proj.shape[0]
    groups = w_s.shape[0]
    w = groups * GROUP_GM

    def body(u_ref, v_ref, gv_ref, ws_ref, bst_ref, do_ref, du_ref, dv_ref, dgv_ref, dws_ref, dbst_ref, dvn_ref):
        @pl.when(pl.program_id(0) == 0)
        def _():
            dgv_ref[...] = jnp.zeros_like(dgv_ref)
            dws_ref[...] = jnp.zeros_like(dws_ref)
            dbst_ref[...] = jnp.zeros_like(dbst_ref)

        ug, ugrad = _gelu_and_grad(u_ref[...])
        vg, vgrad = _gelu_and_grad(v_ref[...])
        r = lax.rsqrt(jnp.mean(vg * vg, axis=-1, keepdims=True) + EPS)
        n = vg * r
        gv = gv_ref[...]
        vnb = (n * gv).astype(BF16)
        dout = do_ref[...]
        mask = _gm_mask()
        for g in range(groups):
            sl = slice(g * GROUP_GM, (g + 1) * GROUP_GM)
            wm = jnp.where(mask, ws_ref[g], 0.0).astype(BF16)
            mixed = _dot(wm, vnb[:, sl]) + bst_ref[:, g:g + 1]
            dmixed = dout[:, sl] * ug[:, sl]
            du_ref[:, sl] = dout[:, sl] * mixed * ugrad[:, sl]
            dbst_ref[:, g:g + 1] += jnp.sum(dmixed, axis=1, keepdims=True)
            dmb = dmixed.astype(BF16)
            dws_ref[g] += jnp.where(mask, _dot_nt(dmb, vnb[:, sl]), 0.0)
            dvn_ref[:, sl] = _dot_tn(wm, dmb)
        dvn = dvn_ref[...]
        dgv_ref[...] += jnp.sum(dvn * n, axis=0, keepdims=True)
        dn = dvn * gv
        dvg = r * (dn - n * jnp.mean(dn * n, axis=-1, keepdims=True))
        dv_ref[...] = dvg * vgrad

    rowb = pl.BlockSpec((GROUP_GM, w), lambda c: (c, 0))
    vec = pl.BlockSpec((1, w), lambda c: (0, 0))
    wsb = pl.BlockSpec((groups, GROUP_GM, GROUP_GM), lambda c: (0, 0, 0))
    bsb = pl.BlockSpec((GROUP_GM, groups), lambda c: (0, 0))
    return _pcall(
        body, grid=(s // GROUP_GM,),
        in_specs=[pl.BlockSpec((GROUP_GM, w), lambda c: (c, u_blk)), pl.BlockSpec((GROUP_GM, w), lambda c: (c, u_blk + 1)),
                  vec, wsb, bsb, rowb],
        out_specs=[rowb, rowb, vec, wsb, bsb],
        out_shape=[jax.ShapeDtypeStruct((s, w), F32), jax.ShapeDtypeStruct((s, w), F32), jax.ShapeDtypeStruct((1, w), F32),
                   jax.ShapeDtypeStruct((groups, GROUP_GM, GROUP_GM), F32), jax.ShapeDtypeStruct((GROUP_GM, groups), F32)],
        scratch_shapes=[pltpu.VMEM((GROUP_GM, w), F32)], name="gm_bwd",
        compiler_params=_params(("arbitrary",)))(proj, proj, g_vnorm, w_s, b_st, do)


def _xa_fwd(proj, mem_kv, q_blk, n_heads):
    s = proj.shape[0]
    nm = mem_kv.shape[0]
    tq = _tile(s, (512, 256, 128))
    scale = HEAD_XA ** -0.5

    def body(q_ref, k_ref, v_ref, o_ref):
        z = _dot_nt(q_ref[...].astype(BF16), k_ref[...].astype(BF16)) * scale
        z = z - jnp.max(z, axis=-1, keepdims=True)
        e = jnp.exp(z)
        p = e / jnp.sum(e, axis=-1, keepdims=True)
        o_ref[...] = _dot(p.astype(BF16), v_ref[...].astype(BF16)).astype(o_ref.dtype)

    h = n_heads
    return _pcall(
        body, grid=(h, s // tq),
        in_specs=[pl.BlockSpec((tq, HEAD_XA), lambda hh, i: (i, q_blk + hh)),
                  pl.BlockSpec((nm, HEAD_XA), lambda hh, i: (0, hh)), pl.BlockSpec((nm, HEAD_XA), lambda hh, i: (0, h + hh))],
        out_specs=pl.BlockSpec((tq, HEAD_XA), lambda hh, i: (i, hh)),
        out_shape=jax.ShapeDtypeStruct((s, h * HEAD_XA), BF16), name="xa_fwd",
        compiler_params=_params(("parallel", "parallel")))(proj, mem_kv, mem_kv)


def _xa_bwd(proj, mem_kv, do, q_blk, n_heads):
    s = proj.shape[0]
    nm = mem_kv.shape[0]
    tq = _tile(s, (512, 256, 128))
    scale = HEAD_XA ** -0.5
    h = n_heads

    def body(q_ref, k_ref, v_ref, do_ref, dq_ref, dk_ref, dv_ref):
        @pl.when(pl.program_id(1) == 0)
        def _():
            dk_ref[...] = jnp.zeros_like(dk_ref)
            dv_ref[...] = jnp.zeros_like(dv_ref)

        qb = q_ref[...].astype(BF16)
        kb = k_ref[...].astype(BF16)
        vb = v_ref[...].astype(BF16)
        dob = do_ref[...].astype(BF16)
        z = _dot_nt(qb, kb) * scale
        z = z - jnp.max(z, axis=-1, keepdims=True)
        e = jnp.exp(z)
        p = e / jnp.sum(e, axis=-1, keepdims=True)
        dp = _dot_nt(dob, vb)
        dz = (p * (dp - jnp.sum(dp * p, axis=-1, keepdims=True)) * scale).astype(BF16)
        dq_ref[...] = _dot(dz, kb)
        dk_ref[...] += _dot_tn(dz, qb)
        dv_ref[...] += _dot_tn(p.astype(BF16), dob)

    qspec = pl.BlockSpec((tq, HEAD_XA), lambda hh, i: (i, hh))
    dk, dv = None, None
    dq, dk, dv = _pcall(
        body, grid=(h, s // tq),
        in_specs=[pl.BlockSpec((tq, HEAD_XA), lambda hh, i: (i, q_blk + hh)),
                  pl.BlockSpec((nm, HEAD_XA), lambda hh, i: (0, hh)), pl.BlockSpec((nm, HEAD_XA), lambda hh, i: (0, h + hh)),
                  qspec],
        out_specs=[qspec, pl.BlockSpec((nm, HEAD_XA), lambda hh, i: (0, hh)), pl.BlockSpec((nm, HEAD_XA), lambda hh, i: (0, hh))],
        out_shape=[jax.ShapeDtypeStruct((s, h * HEAD_XA), F32), jax.ShapeDtypeStruct((nm, h * HEAD_XA), F32),
                   jax.ShapeDtypeStruct((nm, h * HEAD_XA), F32)],
        name="xa_bwd", compiler_params=_params(("parallel", "arbitrary")))(proj, mem_kv, mem_kv, do)
    return dq, dk, dv


def _merge_fwd(zg, b_gate, branches):
    s, d = branches[0].shape
    tr = _tile(s, (128,))

    def body(z0, z1, z2, g0, g1, g2, b0, b1, b2, o_ref):
        acc = None
        for z, g, b in ((z0, g0, b0), (z1, g1, b1), (z2, g2, b2)):
            term = jax.nn.sigmoid(z[...] + g[...]) * b[...]
            acc = term if acc is None else acc + term
        o_ref[...] = acc.astype(o_ref.dtype)

    zs = [pl.BlockSpec((tr, d), functools.partial(lambda i, k: (i, k), k=k)) for k in range(3)]
    gs = [pl.BlockSpec((1, d), functools.partial(lambda i, k: (0, k), k=k)) for k in range(3)]
    row = pl.BlockSpec((tr, d), lambda i: (i, 0))
    return _pcall(
        body, grid=(s // tr,), in_specs=zs + gs + [row] * 3, out_specs=row,
        out_shape=jax.ShapeDtypeStruct((s, d), BF16), name="merge_fwd",
        compiler_params=_params(("parallel",)))(zg, zg, zg, b_gate, b_gate, b_gate, *branches)


def _merge_bwd(zg, b_gate, branches, dmerged):
    s, d = branches[0].shape
    tr = _tile(s, (128,))

    def body(z0, z1, z2, g0, g1, g2, b0, b1, b2, dm_ref, dz_ref, d0, d1, d2, dbg_ref):
        @pl.when(pl.program_id(0) == 0)
        def _():
            dbg_ref[...] = jnp.zeros_like(dbg_ref)

        dm = dm_ref[...]
        for k, (z, g, b, dbr) in enumerate(((z0, g0, b0, d0), (z1, g1, b1, d1), (z2, g2, b2, d2))):
            sg = jax.nn.sigmoid(z[...] + g[...])
            dbr[...] = (dm * sg).astype(dbr.dtype)
            dz = dm * b[...] * sg * (1.0 - sg)
            dz_ref[:, k * d:(k + 1) * d] = dz.astype(dz_ref.dtype)
            dbg_ref[:, k * d:(k + 1) * d] += jnp.sum(dz, axis=0, keepdims=True)

    zs = [pl.BlockSpec((tr, d), functools.partial(lambda i, k: (i, k), k=k)) for k in range(3)]
    gs = [pl.BlockSpec((1, d), functools.partial(lambda i, k: (0, k), k=k)) for k in range(3)]
    row = pl.BlockSpec((tr, d), lambda i: (i, 0))
    outs = _pcall(
        body, grid=(s // tr,), in_specs=zs + gs + [row] * 4,
        out_specs=[pl.BlockSpec((tr, 3 * d), lambda i: (i, 0)), row, row, row, pl.BlockSpec((1, 3 * d), lambda i: (0, 0))],
        out_shape=[jax.ShapeDtypeStruct((s, 3 * d), BF16)] + [jax.ShapeDtypeStruct((s, d), BF16)] * 3
        + [jax.ShapeDtypeStruct((1, 3 * d), F32)],
        name="merge_bwd", compiler_params=_params(("arbitrary",)))(zg, zg, zg, b_gate, b_gate, b_gate, *branches, dmerged)
    return outs[0], list(outs[1:4]), outs[4]


def _shift_down(x, k, row):
    return jnp.where(row >= k, pltpu.roll(x, k, 0), 0.0)


def _shift_up(x, k, row, s):
    return jnp.where(row < s - k, pltpu.roll(x, s - k, 0), 0.0)


def _conv_pre(gate, cw_ref, cb_ref, row):
    conv = cb_ref[...] + cw_ref[CONV_TAPS - 1:CONV_TAPS, :] * gate
    for k in range(1, CONV_TAPS):
        conv = conv + cw_ref[CONV_TAPS - 1 - k:CONV_TAPS - k, :] * _shift_down(gate, k, row)
    return conv


def _cg_fwd(up, conv_w, conv_b):
    s = up.shape[0]
    f = conv_w.shape[1]
    tc = _tile(f, (256, 128))
    nb = f // tc

    def body(g_ref, v_ref, cw_ref, cb_ref, o_ref):
        row = lax.broadcasted_iota(jnp.int32, (s, tc), 0)
        conv = _conv_pre(g_ref[...], cw_ref, cb_ref, row)
        o_ref[...] = (_gelu(conv) * v_ref[...]).astype(o_ref.dtype)

    return _pcall(
        body, grid=(nb,),
        in_specs=[pl.BlockSpec((s, tc), lambda j: (0, j)), pl.BlockSpec((s, tc), lambda j: (0, nb + j)),
                  pl.BlockSpec((CONV_TAPS, tc), lambda j: (0, j)), pl.BlockSpec((1, tc), lambda j: (0, j))],
        out_specs=pl.BlockSpec((s, tc), lambda j: (0, j)),
        out_shape=jax.ShapeDtypeStruct((s, f), BF16), name="cg_fwd",
        compiler_params=_params(("parallel",)))(up, up, conv_w, conv_b)


def _cg_bwd(up, conv_w, conv_b, dact):
    s = up.shape[0]
    f = conv_w.shape[1]
    tc = _tile(f, (256, 128))
    nb = f // tc

    def body(g_ref, v_ref, cw_ref, cb_ref, da_ref, dg_ref, dv_ref, dcw_ref, dcb_ref):
        row = lax.broadcasted_iota(jnp.int32, (s, tc), 0)
        gate = g_ref[...]
        conv = _conv_pre(gate, cw_ref, cb_ref, row)
        gel, ggrad = _gelu_and_grad(conv)
        da = da_ref[...]
        dv_ref[...] = (da * gel).astype(dv_ref.dtype)
        dconv = da * v_ref[...] * ggrad
        dgate = cw_ref[CONV_TAPS - 1:CONV_TAPS, :] * dconv
        dcw_ref[CONV_TAPS - 1:CONV_TAPS, :] = jnp.sum(dconv * gate, axis=0, keepdims=True)
        for k in range(1, CONV_TAPS):
            dgate = dgate + cw_ref[CONV_TAPS - 1 - k:CONV_TAPS - k, :] * _shift_up(dconv, k, row, s)
            dcw_ref[CONV_TAPS - 1 - k:CONV_TAPS - k, :] = jnp.sum(dconv * _shift_down(gate, k, row), axis=0, keepdims=True)
        dg_ref[...] = dgate.astype(dg_ref.dtype)
        dcb_ref[...] = jnp.sum(dconv, axis=0, keepdims=True)

    colb = pl.BlockSpec((s, tc), lambda j: (0, j))
    return _pcall(
        body, grid=(nb,),
        in_specs=[colb, pl.BlockSpec((s, tc), lambda j: (0, nb + j)), pl.BlockSpec((CONV_TAPS, tc), lambda j: (0, j)),
                  pl.BlockSpec((1, tc), lambda j: (0, j)), colb],
        out_specs=[colb, colb, pl.BlockSpec((CONV_TAPS, tc), lambda j: (0, j)), pl.BlockSpec((1, tc), lambda j: (0, j))],
        out_shape=[jax.ShapeDtypeStruct((s, f), BF16), jax.ShapeDtypeStruct((s, f), BF16),
                   jax.ShapeDtypeStruct((CONV_TAPS, f), F32), jax.ShapeDtypeStruct((1, f), F32)],
        name="cg_bwd", compiler_params=_params(("parallel",)))(up, up, conv_w, conv_b, dact)


def _row_tile(rows, cols):
    want = max(16, (256 * 1024) // cols)
    for c in (512, 256, 128, 64, 32, 16):
        if c <= want and rows % c == 0:
            return c
    return rows


def _sum_halves(dwv, recv, core, name):
    nj, _, a, c = dwv.shape
    tr = _row_tile(a, c)

    def body(core_ref, d_ref, r_ref, o_ref):
        o_ref[0] = (d_ref[0, 0].astype(F32) + r_ref[0].astype(F32)).astype(o_ref.dtype)

    grid_spec = pltpu.PrefetchScalarGridSpec(
        num_scalar_prefetch=1, grid=(nj, a // tr),
        in_specs=[pl.BlockSpec((1, 1, tr, c), lambda j, i, cr: (j, cr[0], i, 0)),
                  pl.BlockSpec((1, tr, c), lambda j, i, cr: (j, i, 0))],
        out_specs=pl.BlockSpec((1, tr, c), lambda j, i, cr: (j, i, 0)))
    return _pcall(body, grid_spec=grid_spec, out_shape=jax.ShapeDtypeStruct((nj, a, c), BF16), name=name,
                  compiler_params=_params(("parallel", "parallel")))(core, dwv, recv)


def _sum_chips(recv, own, joined, layer, n_layers, chip, core, ax):
    _, a, b = recv.shape
    tr = _row_tile(a, b)

    def body(chip_ref, core_ref, r_ref, own_ref, *rest):
        o_ref = rest[-1]
        me = chip_ref[0]
        mine = own_ref[0].astype(F32)
        acc = None
        for k in range(N_CHIPS):
            term = jnp.where(me == k, mine, r_ref[k].astype(F32))
            acc = term if acc is None else acc + term
        o_ref[0, 0] = acc

    own_spec = (pl.BlockSpec((1, tr, b), lambda i, ch, co: (0, i, ch[0])) if ax == 2
                else pl.BlockSpec((1, tr, b), lambda i, ch, co: (ch[0], i, 0)))
    in_specs = [pl.BlockSpec((N_CHIPS, tr, b), lambda i, ch, co: (0, i, 0)), own_spec]
    args = [chip, core, recv, own]
    aliases = {}
    if joined is not None:
        in_specs.append(pl.BlockSpec(memory_space=pl.ANY))
        args.append(joined)
        aliases = {4: 0}
    grid_spec = pltpu.PrefetchScalarGridSpec(
        num_scalar_prefetch=2, grid=(a // tr,), in_specs=in_specs,
        out_specs=pl.BlockSpec((1, 1, tr, b), lambda i, ch, co: (layer, co[0], i, 0)))
    return _pcall(body, grid_spec=grid_spec, out_shape=jax.ShapeDtypeStruct((n_layers, 2, a, b), F32),
                  input_output_aliases=aliases, name="sum_chips", compiler_params=_params(("parallel",)))(*args)


def _place_own(wt, layer, chip, ax):
    nl, r, c = wt.shape
    half = r // 2
    tr = _row_tile(half, c)
    nb = half // tr

    def body(chip_ref, w_ref, o_ref):
        o_ref[...] = w_ref[...].astype(BF16).reshape(o_ref.shape)

    if ax == 2:
        out_spec = pl.BlockSpec((1, tr, c), lambda h, i, ch: (h, i, ch[0]))
    else:
        out_spec = pl.BlockSpec((1, 1, tr, c), lambda h, i, ch: (ch[0], h, i, 0))
    grid_spec = pltpu.PrefetchScalarGridSpec(
        num_scalar_prefetch=1, grid=(2, nb),
        in_specs=[pl.BlockSpec((1, tr, c), lambda h, i, ch: (layer, h * nb + i, 0))], out_specs=out_spec)
    return _pcall(body, grid_spec=grid_spec, out_shape=jax.ShapeDtypeStruct(_full_view_shape(wt.shape, ax), BF16),
                  name="place_own", compiler_params=_params(("parallel", "parallel")))(chip, wt)


def _adamw(w, m, v, g, name):
    nl, r, c = w.shape
    tr = _row_tile(r, c)
    c1 = 1.0 - ADAM_B1 ** ADAM_STEP
    c2 = 1.0 - ADAM_B2 ** ADAM_STEP

    def body(w_ref, m_ref, v_ref, gin_ref, g_ref, d_ref, nm_ref, nv_ref):
        g = gin_ref[...]
        mm = ADAM_B1 * m_ref[...] + (1.0 - ADAM_B1) * g
        vv = ADAM_B2 * v_ref[...] + (1.0 - ADAM_B2) * (g * g)
        g_ref[...] = g
        nm_ref[...] = mm
        nv_ref[...] = vv
        d_ref[...] = -ADAM_LR * ((mm / c1) / (jnp.sqrt(vv / c2) + ADAM_EPS) + ADAM_WD * w_ref[...])

    blk = pl.BlockSpec((1, tr, c), lambda l, i: (l, i, 0))
    shape = jax.ShapeDtypeStruct((nl, r, c), F32)
    return _pcall(
        body, grid=(nl, r // tr), in_specs=[blk] * 4, out_specs=[blk] * 4, out_shape=[shape] * 4, name=name,
        compiler_params=_params(("parallel", "parallel")))(w, m, v, g)


HBM_SPEC = pl.BlockSpec(memory_space=pltpu.HBM)
COMM = pltpu.CompilerParams(has_side_effects=True)


def _position():
    x, y, c = lax.axis_index("x"), lax.axis_index("y"), lax.axis_index("c")
    chips = [(1 - x, y), (x, 1 - y), (1 - x, 1 - y)]
    return x, y, c, chips


def _remote(src, dst, send_sem, recv_sem, dev):
    return pltpu.make_async_remote_copy(src_ref=src, dst_ref=dst, send_sem=send_sem, recv_sem=recv_sem,
                                        device_id=dev, device_id_type=MESH)


def _full_view_shape(shard_shape, ax):
    _, r, c = shard_shape
    return (2, r // 2, c * N_CHIPS) if ax == 2 else (N_CHIPS, 2, r // 2, c)


def _piece(ref, ax, j, h, cs):
    if ax == 2:
        return ref.at[h, :, pl.ds(pl.multiple_of(j * cs, cs), cs)]
    return ref.at[j, h]


def _chip_block(ref, ax, j, cs):
    if ax == 2:
        return ref.at[:, :, pl.ds(pl.multiple_of(j * cs, cs), cs)]
    return ref.at[j]


def _gather_ici(views, axes, shard_cols):
    n = len(views)

    def body(*refs):
        outs = refs[n:2 * n]
        send_sems, recv_sems = refs[2 * n:]
        x, y, c, chips = _position()
        me = 2 * x + y
        remote = []
        for t, ax in enumerate(axes):
            mine = _piece(outs[t], ax, me, c, shard_cols[t])
            for p, (px, py) in enumerate(chips):
                k = t * 3 + p
                remote.append(_remote(mine, mine, send_sems.at[k], recv_sems.at[k], (px, py, c)))
                remote[-1].start()
        for t, ax in enumerate(axes):
            for p, (px, py) in enumerate(chips):
                k = t * 3 + p
                theirs = _piece(outs[t], ax, 2 * px + py, c, shard_cols[t])
                _remote(theirs, theirs, send_sems.at[k], recv_sems.at[k], (px, py, c)).wait_recv()
        for cp in remote:
            cp.wait_send()

    return _pcall(
        body, in_specs=[HBM_SPEC] * n, out_specs=[HBM_SPEC] * n,
        out_shape=[jax.ShapeDtypeStruct(v.shape, v.dtype) for v in views],
        input_output_aliases={t: t for t in range(n)},
        scratch_shapes=[pltpu.SemaphoreType.DMA((3 * n,)), pltpu.SemaphoreType.DMA((3 * n,))],
        name="gather_ici", compiler_params=COMM)(*views)


def _gather_d2d(views, axes, shard_cols):
    n = len(views)

    def body(*refs):
        outs = refs[n:2 * n]
        send_sems, recv_sems = refs[2 * n:]
        x, y, c, chips = _position()
        remote = []
        for t, ax in enumerate(axes):
            for p, (px, py) in enumerate(chips):
                k = t * 3 + p
                blk = _piece(outs[t], ax, 2 * px + py, c, shard_cols[t])
                remote.append(_remote(blk, blk, send_sems.at[k], recv_sems.at[k], (x, y, 1 - c)))
                remote[-1].start()
        for t, ax in enumerate(axes):
            for p, (px, py) in enumerate(chips):
                k = t * 3 + p
                blk = _piece(outs[t], ax, 2 * px + py, 1 - c, shard_cols[t])
                _remote(blk, blk, send_sems.at[k], recv_sems.at[k], (x, y, 1 - c)).wait_recv()
        for cp in remote:
            cp.wait_send()

    return _pcall(
        body, in_specs=[HBM_SPEC] * n, out_specs=[HBM_SPEC] * n,
        out_shape=[jax.ShapeDtypeStruct(v.shape, v.dtype) for v in views],
        input_output_aliases={t: t for t in range(n)},
        scratch_shapes=[pltpu.SemaphoreType.DMA((3 * n,)), pltpu.SemaphoreType.DMA((3 * n,))],
        name="gather_d2d", compiler_params=COMM)(*views)


def _grads_d2d(dwvs):
    n = len(dwvs)

    def body(*refs):
        ins, outs = refs[:n], refs[n:2 * n]
        send_sems, recv_sems = refs[2 * n:]
        x, y, c, _ = _position()
        remote = [_remote(ins[t].at[:, 1 - c], outs[t], send_sems.at[t], recv_sems.at[t], (x, y, 1 - c)) for t in range(n)]
        for cp in remote:
            cp.start()
        for cp in remote:
            cp.wait()

    return _pcall(
        body, in_specs=[HBM_SPEC] * n, out_specs=[HBM_SPEC] * n,
        out_shape=[jax.ShapeDtypeStruct((d.shape[0],) + d.shape[2:], d.dtype) for d in dwvs],
        scratch_shapes=[pltpu.SemaphoreType.DMA((n,)), pltpu.SemaphoreType.DMA((n,))],
        name="grads_d2d", compiler_params=COMM)(*dwvs)


def _grads_ici(sums, axes):
    n = len(sums)
    out_shapes = []
    for sm, ax in zip(sums, axes):
        _, a, c = sm.shape
        out_shapes.append(jax.ShapeDtypeStruct((N_CHIPS, a, c // N_CHIPS if ax == 2 else c), sm.dtype))

    def body(*refs):
        ins, outs = refs[:n], refs[n:2 * n]
        send_sems, recv_sems = refs[2 * n:]
        x, y, c, chips = _position()
        me = 2 * x + y

        def block(t, j):
            if axes[t] == 2:
                cs = outs[t].shape[2]
                return ins[t].at[0, :, pl.ds(pl.multiple_of(j * cs, cs), cs)]
            return ins[t].at[j]

        remote = []
        for t in range(n):
            for p, (px, py) in enumerate(chips):
                k = t * 3 + p
                remote.append(_remote(block(t, 2 * px + py), outs[t].at[me], send_sems.at[k], recv_sems.at[k], (px, py, c)))
                remote[-1].start()
        for t in range(n):
            for p, (px, py) in enumerate(chips):
                k = t * 3 + p
                peer = 2 * px + py
                _remote(block(t, peer), outs[t].at[peer], send_sems.at[k], recv_sems.at[k], (px, py, c)).wait_recv()
        for cp in remote:
            cp.wait_send()

    return _pcall(
        body, in_specs=[HBM_SPEC] * n, out_specs=[HBM_SPEC] * n, out_shape=out_shapes,
        scratch_shapes=[pltpu.SemaphoreType.DMA((3 * n,)), pltpu.SemaphoreType.DMA((3 * n,))],
        name="grads_ici", compiler_params=COMM)(*sums)


def _join_halves(joined):
    n = len(joined)

    def body(*refs):
        outs = refs[n:2 * n]
        send_sems, recv_sems = refs[2 * n:]
        x, y, c, _ = _position()
        remote = [_remote(outs[t].at[:, c], outs[t].at[:, c], send_sems.at[t], recv_sems.at[t], (x, y, 1 - c)) for t in range(n)]
        for cp in remote:
            cp.start()
        for t in range(n):
            _remote(outs[t].at[:, 1 - c], outs[t].at[:, 1 - c], send_sems.at[t], recv_sems.at[t], (x, y, 1 - c)).wait_recv()
        for cp in remote:
            cp.wait_send()

    return _pcall(
        body, in_specs=[HBM_SPEC] * n, out_specs=[HBM_SPEC] * n,
        out_shape=[jax.ShapeDtypeStruct(j.shape, j.dtype) for j in joined],
        input_output_aliases={t: t for t in range(n)},
        scratch_shapes=[pltpu.SemaphoreType.DMA((n,)), pltpu.SemaphoreType.DMA((n,))],
        name="join_halves", compiler_params=COMM)(*joined)


def _gather_small(shard):
    nl, r, cs = shard.shape

    def body(in_ref, out_ref, send_sems, recv_sems, local_sem):
        x, y, c, chips = _position()

        def cols(j):
            return out_ref.at[:, :, pl.ds(pl.multiple_of(j * cs, cs), cs)]

        me = 2 * x + y
        loc = pltpu.make_async_copy(in_ref, cols(me), local_sem)
        loc.start()
        remote = [_remote(in_ref, cols(me), send_sems.at[p], recv_sems.at[p], (px, py, c)) for p, (px, py) in enumerate(chips)]
        for cp in remote:
            cp.start()
        for p, (px, py) in enumerate(chips):
            _remote(in_ref, cols(2 * px + py), send_sems.at[p], recv_sems.at[p], (px, py, c)).wait_recv()
        for cp in remote:
            cp.wait_send()
        loc.wait()

    return _pcall(
        body, in_specs=[HBM_SPEC], out_specs=HBM_SPEC, out_shape=jax.ShapeDtypeStruct((nl, r, cs * N_CHIPS), shard.dtype),
        scratch_shapes=[pltpu.SemaphoreType.DMA((3,)), pltpu.SemaphoreType.DMA((3,)), pltpu.SemaphoreType.DMA(())],
        name="gather_small", compiler_params=COMM)(shard)


def _all_reduce_small(pack):
    r, c = pack.shape

    def body(in_ref, out_ref, slots, send_sems, recv_sems, local_sem):
        x, y, cc, _ = _position()
        me = 4 * x + 2 * y + cc
        peers = []
        for k in range(1, N_DEV):
            fx, fy, fc = (k >> 2) & 1, (k >> 1) & 1, k & 1
            px, py, pc = x ^ fx, y ^ fy, cc ^ fc
            peers.append((k - 1, (px, py, pc), 4 * px + 2 * py + pc))
        loc = pltpu.make_async_copy(in_ref, slots.at[me], local_sem)
        loc.start()
        copies = [_remote(in_ref, slots.at[me], send_sems.at[k], recv_sems.at[k], dev) for k, dev, _ in peers]
        for cp in copies:
            cp.start()
        for k, dev, idx in peers:
            _remote(in_ref, slots.at[idx], send_sems.at[k], recv_sems.at[k], dev).wait_recv()
        for cp in copies:
            cp.wait_send()
        loc.wait()
        acc = slots[0]
        for k in range(1, N_DEV):
            acc = acc + slots[k]
        out_ref[...] = acc

    vm = pl.BlockSpec(memory_space=pltpu.VMEM)
    return _pcall(
        body, in_specs=[vm], out_specs=vm, out_shape=jax.ShapeDtypeStruct((r, c), F32),
        scratch_shapes=[pltpu.VMEM((N_DEV, r, c), F32), pltpu.SemaphoreType.DMA((N_DEV - 1,)),
                        pltpu.SemaphoreType.DMA((N_DEV - 1,)), pltpu.SemaphoreType.DMA(())],
        name="all_reduce_small", compiler_params=pltpu.CompilerParams(has_side_effects=True, vmem_limit_bytes=VMEM_LIMIT))(pack)


def _dims(d):
    half = d // 2
    return half // HEAD_SB, half // HEAD_XA, 3, (5 * half) // HEAD_XA


def _layer_fwd(x, mem, full, small, l):
    h_sb, h_xa, u_blk, q_blk = _dims(x.shape[1])

    def vec(name):
        return small[name][l].reshape(1, -1)

    h1 = _norm_fwd(x, vec('g_mix_pre'), None, BF16, "norm_mix_pre")
    proj = _mm(h1, full['w_in'], 'nn', F32, "mm_proj")
    o_sb = _sb_fwd(proj, h_sb)
    b_st = small['b_s'][l].T
    o_gm = _gm_fwd(proj, vec('g_vnorm'), small['w_s'][l], b_st, u_blk)
    memn = _norm_fwd(mem, vec('g_mem'), None, BF16, "norm_mem")
    mem_kv = _mm(memn, full['w_mem_kv'], 'nn', F32, "mm_mem_kv")
    o_xa = _xa_fwd(proj, mem_kv, q_blk, h_xa)
    zg = _mm(h1, full['w_gate'], 'nn', F32, "mm_gate")
    branches = [_mm(o, full[wn], 'nn', F32, "mm_branch")
                for o, wn in ((o_sb, 'w_br_sb'), (o_gm, 'w_br_gm'), (o_xa, 'w_br_xa'))]
    merged = _merge_fwd(zg, vec('b_gate'), branches)
    y1 = _mm(merged, full['w_out'], 'nn', F32, "mm_out")
    x1 = _norm_fwd(y1, vec('g_mix_post'), x, F32, "norm_mix_post")
    h2 = _norm_fwd(x1, vec('g_ffn_pre'), None, BF16, "norm_ffn_pre")
    up = _mm(h2, full['w_up'], 'nn', F32, "mm_up")
    act = _cg_fwd(up, full['conv_w'], vec('conv_b'))
    y2 = _mm(act, full['w_down'], 'nn', F32, "mm_down")
    x2 = _norm_fwd(y2, vec('g_ffn_post'), x1, F32, "norm_ffn_post")
    saved = dict(x0=x, h1=h1, proj=proj, o_sb=o_sb, o_gm=o_gm, o_xa=o_xa, memn=memn, mem_kv=mem_kv, zg=zg,
                 branches=branches, merged=merged, y1=y1, x1=x1, h2=h2, up=up, act=act, y2=y2, b_st=b_st)
    return x2, saved


def _layer_bwd(dx, mem, sv, full, small, l):
    h_sb, h_xa, u_blk, q_blk = _dims(dx.shape[1])

    def vec(name):
        return small[name][l].reshape(1, -1)

    gb, gs = {}, {}
    dy2, gs['g_ffn_post'] = _norm_bwd(sv['y2'], vec('g_ffn_post'), [dx], None, BF16, "norm_ffn_post_bwd")
    gb['w_down'] = _mm(sv['act'], dy2, 'tn', BF16, "mm_down_dw")
    dact = _mm(dy2, full['w_down'], 'nt', F32, "mm_down_dx")
    dgate, dval, gs['conv_w'], gs['conv_b'] = _cg_bwd(sv['up'], full['conv_w'], vec('conv_b'), dact)
    dup = jnp.concatenate([dgate, dval], axis=1)
    gb['w_up'] = _mm(sv['h2'], dup, 'tn', BF16, "mm_up_dw")
    dh2 = _mm(dup, full['w_up'], 'nt', F32, "mm_up_dx")
    dx1, gs['g_ffn_pre'] = _norm_bwd(sv['x1'], vec('g_ffn_pre'), [dh2], dx, F32, "norm_ffn_pre_bwd")
    dy1, gs['g_mix_post'] = _norm_bwd(sv['y1'], vec('g_mix_post'), [dx1], None, BF16, "norm_mix_post_bwd")
    gb['w_out'] = _mm(sv['merged'], dy1, 'tn', BF16, "mm_out_dw")
    dmerged = _mm(dy1, full['w_out'], 'nt', F32, "mm_out_dx")
    dzg, dbr, gs['b_gate'] = _merge_bwd(sv['zg'], vec('b_gate'), sv['branches'], dmerged)
    douts = []
    for o, db, wn in ((sv['o_sb'], dbr[0], 'w_br_sb'), (sv['o_gm'], dbr[1], 'w_br_gm'), (sv['o_xa'], dbr[2], 'w_br_xa')):
        gb[wn] = _mm(o, db, 'tn', BF16, "mm_branch_dw")
        douts.append(_mm(db, full[wn], 'nt', F32, "mm_branch_dx"))
    gb['w_gate'] = _mm(sv['h1'], dzg, 'tn', BF16, "mm_gate_dw")
    dh1_gate = _mm(dzg, full['w_gate'], 'nt', F32, "mm_gate_dx")
    dq_xa, dk_xa, dv_xa = _xa_bwd(sv['proj'], sv['mem_kv'], douts[2], q_blk, h_xa)
    dmem_kv = jnp.concatenate([dk_xa, dv_xa], axis=1).astype(BF16)
    gb['w_mem_kv'] = _mm(sv['memn'], dmem_kv, 'tn', BF16, "mm_mem_kv_dw")
    dmemn = _mm(dmem_kv, full['w_mem_kv'], 'nt', F32, "mm_mem_kv_dx")
    _, gs['g_mem'] = _norm_bwd(mem, vec('g_mem'), [dmemn], None, BF16, "norm_mem_bwd")
    du, dv, gs['g_vnorm'], gs['w_s'], db_st = _gm_bwd(sv['proj'], vec('g_vnorm'), small['w_s'][l], sv['b_st'], douts[1], u_blk)
    gs['b_s'] = db_st.T
    dq, dk, dvv = _sb_bwd(sv['proj'], douts[0], h_sb)
    dproj = jnp.concatenate([dq, dk, dvv, du, dv, dq_xa], axis=1).astype(BF16)
    gb['w_in'] = _mm(sv['h1'], dproj, 'tn', BF16, "mm_proj_dw")
    dh1_proj = _mm(dproj, full['w_in'], 'nt', F32, "mm_proj_dx")
    dx0, gs['g_mix_pre'] = _norm_bwd(sv['x0'], vec('g_mix_pre'), [dh1_gate, dh1_proj], dx1, F32, "norm_mix_pre_bwd")
    return dx0, gb, gs


def _local_step(x, mem, target, full, small):
    n_layers = len(full['w_in'])
    saved = []
    for l in range(n_layers):
        x, sv = _layer_fwd(x, mem, {n: full[n][l] for n in full}, small, l)
        saved.append(sv)
    sq, dx = _loss_head(x, target)
    gbig = {n: [None] * n_layers for n in BIG}
    gsmall = {n: [None] * n_layers for n in SMALL + ['conv_w']}
    for l in reversed(range(n_layers)):
        dx, gb, gs = _layer_bwd(dx, mem, saved[l], {n: full[n][l] for n in full}, small, l)
        for n in gb:
            gbig[n][l] = gb[n]
        for n in gs:
            gsmall[n][l] = gs[n]
    return sq, dx, gbig, gsmall


def _pack(arrays, rows_multiple):
    flat = jnp.concatenate([a.reshape(-1).astype(F32) for a in arrays])
    rows = -(-flat.shape[0] // LANES)
    rows = -(-rows // rows_multiple) * rows_multiple
    return jnp.pad(flat, (0, rows * LANES - flat.shape[0])).reshape(rows, LANES)


def _unpack(pack, like):
    flat = pack.reshape(-1)
    out, off = [], 0
    for a in like:
        out.append(flat[off:off + a.size].reshape(a.shape))
        off += a.size
    return out


def _grad_view(g, ax):
    r, c = g.shape
    return g.reshape(1, 2, r // 2, c) if ax == 2 else g.reshape(N_CHIPS, 2, r // (2 * N_CHIPS), c)


def kernel(x, mem, g_mix_pre, w_in, g_vnorm, w_s, b_s, g_mem, w_mem_kv, w_gate, b_gate, w_br_sb, w_br_gm, w_br_xa, w_out, g_mix_post, g_ffn_pre, w_up, conv_w, conv_b, w_down, g_ffn_post, loss_target, m_g_mix_pre, m_w_in, m_g_vnorm, m_w_s, m_b_s, m_g_mem, m_w_mem_kv, m_w_gate, m_b_gate, m_w_br_sb, m_w_br_gm, m_w_br_xa, m_w_out, m_g_mix_post, m_g_ffn_pre, m_w_up, m_conv_w, m_conv_b, m_w_down, m_g_ffn_post, v_g_mix_pre, v_w_in, v_g_vnorm, v_w_s, v_b_s, v_g_mem, v_w_mem_kv, v_w_gate, v_b_gate, v_w_br_sb, v_w_br_gm, v_w_br_xa, v_w_out, v_g_mix_post, v_g_ffn_pre, v_w_up, v_conv_w, v_conv_b, v_w_down, v_g_ffn_post):
    w = dict(g_mix_pre=g_mix_pre, w_in=w_in, g_vnorm=g_vnorm, w_s=w_s, b_s=b_s, g_mem=g_mem, w_mem_kv=w_mem_kv,
             w_gate=w_gate, b_gate=b_gate, w_br_sb=w_br_sb, w_br_gm=w_br_gm, w_br_xa=w_br_xa, w_out=w_out,
             g_mix_post=g_mix_post, g_ffn_pre=g_ffn_pre, w_up=w_up, conv_w=conv_w, conv_b=conv_b, w_down=w_down,
             g_ffn_post=g_ffn_post)
    m = dict(g_mix_pre=m_g_mix_pre, w_in=m_w_in, g_vnorm=m_g_vnorm, w_s=m_w_s, b_s=m_b_s, g_mem=m_g_mem,
             w_mem_kv=m_w_mem_kv, w_gate=m_w_gate, b_gate=m_b_gate, w_br_sb=m_w_br_sb, w_br_gm=m_w_br_gm,
             w_br_xa=m_w_br_xa, w_out=m_w_out, g_mix_post=m_g_mix_post, g_ffn_pre=m_g_ffn_pre, w_up=m_w_up,
             conv_w=m_conv_w, conv_b=m_conv_b, w_down=m_w_down, g_ffn_post=m_g_ffn_post)
    v = dict(g_mix_pre=v_g_mix_pre, w_in=v_w_in, g_vnorm=v_g_vnorm, w_s=v_w_s, b_s=v_b_s, g_mem=v_g_mem,
             w_mem_kv=v_w_mem_kv, w_gate=v_w_gate, b_gate=v_b_gate, w_br_sb=v_w_br_sb, w_br_gm=v_w_br_gm,
             w_br_xa=v_w_br_xa, w_out=v_w_out, g_mix_post=v_g_mix_post, g_ffn_pre=v_g_ffn_pre, w_up=v_w_up,
             conv_w=v_conv_w, conv_b=v_conv_b, w_down=v_w_down, g_ffn_post=v_g_ffn_post)
    n_layers = w_in.shape[0]
    d = x.shape[-1]
    axes = [BIG_AXIS[n] for n in BIG]
    core = lax.axis_index("c").astype(jnp.int32).reshape(1)
    chip = (2 * lax.axis_index("x") + lax.axis_index("y")).astype(jnp.int32).reshape(1)
    small = {n: w[n] for n in SMALL}
    xs, mems, target = x[0], mem[0], loss_target[0]

    shard_cols = [w[n].shape[2] for n in BIG]
    conv_w_full = _gather_small(conv_w)

    def gather(l):
        views = [_place_own(w[n], l, chip, ax) for n, ax in zip(BIG, axes)]
        views = _gather_d2d(_gather_ici(views, axes, shard_cols), axes, shard_cols)
        full = {n: vw.reshape(-1, vw.shape[-1]) if ax == 1 else vw.reshape(vw.shape[0] * vw.shape[1], vw.shape[2])
                for n, vw, ax in zip(BIG, views, axes)}
        full['conv_w'] = conv_w_full[l]
        return full

    def scatter(gb, l, joined):
        dwvs = [_grad_view(gb[n], ax) for n, ax in zip(BIG, axes)]
        theirs = _grads_d2d(dwvs)
        sums = [_sum_halves(dv, th, core, "sum_halves") for dv, th in zip(dwvs, theirs)]
        recv = _grads_ici(sums, axes)
        return [_sum_chips(r, sm, None if joined is None else joined[t], l, n_layers, chip, core, ax)
                for t, (r, sm, ax) in enumerate(zip(recv, sums, axes))]

    fulls, saved = [], []
    for l in range(n_layers):
        fulls.append(gather(l))
        xs, sv = _layer_fwd(xs, mems, fulls[l], small, l)
        saved.append(sv)
    sq, dx = _loss_head(xs, target)
    loss = lax.psum(0.5 * jnp.sum(sq) / d, ("x", "y", "c"))

    joined = None
    gsmall = {n: [None] * n_layers for n in SMALL + ['conv_w']}
    for l in reversed(range(n_layers)):
        dx, gb, gs = _layer_bwd(dx, mems, saved[l], fulls[l], small, l)
        joined = scatter(gb, l, joined)
        for n in gs:
            gsmall[n][l] = gs[n]
    joined = _join_halves(joined)

    small_full = [jnp.stack(gsmall[n]).reshape(w[n].shape) for n in SMALL]
    conv_w_grad = jnp.stack(gsmall['conv_w'])
    summed = _all_reduce_small(_pack(small_full + [conv_w_grad], 8))
    *small_g, conv_w_g = _unpack(summed, small_full + [conv_w_grad])
    shard = conv_w.shape[-1]
    conv_w_g = lax.dynamic_slice_in_dim(conv_w_g, chip[0] * shard, shard, axis=2)

    out = {}
    for n, g in zip(BIG, joined):
        out[n] = _adamw(w[n], m[n], v[n], g.reshape(w[n].shape), "adamw_big")
    names = SMALL + ['conv_w']
    packed = [_pack([p[n] for n in names], 256) for p in (w, m, v)]
    gpack = _pack(small_g + [conv_w_g], 256)
    res = _adamw(packed[0][None], packed[1][None], packed[2][None], gpack[None], "adamw_small")
    like = [w[n] for n in names]
    unpacked = [_unpack(r[0], like) for r in res]
    for i, n in enumerate(names):
        out[n] = tuple(u[i] for u in unpacked)

    return (loss, dx[None], *[out[n][0] for n in WEIGHTS], *[out[n][1] for n in WEIGHTS],
            *[out[n][2] for n in WEIGHTS], *[out[n][3] for n in WEIGHTS])
```

```python
import functools
import math

import jax
import jax.numpy as jnp
from jax import lax
from jax.experimental import pallas as pl
from jax.experimental.pallas import tpu as pltpu

F32 = jnp.float32
BF16 = jnp.bfloat16
EPS = 1e-6
HEAD_SB = 128
GROUP_GM = 128
CHUNK = 64
HEAD_XA = 256
CONV_TAPS = 3
N_CHIPS = 4
N_DEV = 8
LANES = 128
MIB = 1024 * 1024
VMEM_LIMIT = 48 * MIB
SPLITS = 2

ADAM_LR = 0.001
ADAM_B1 = 0.9
ADAM_B2 = 0.999
ADAM_EPS = 1e-08
ADAM_WD = 0.01
ADAM_STEP = 10

WEIGHTS = ['g_mix_pre', 'w_in', 'g_vnorm', 'w_s', 'b_s', 'g_mem', 'w_mem_kv', 'w_gate', 'b_gate', 'w_br_sb',
           'w_br_gm', 'w_br_xa', 'w_out', 'g_mix_post', 'g_ffn_pre', 'w_up', 'conv_w', 'conv_b', 'w_down',
           'g_ffn_post']
BIG_AXIS = {'w_in': 2, 'w_mem_kv': 1, 'w_gate': 2, 'w_br_sb': 2, 'w_br_gm': 2, 'w_br_xa': 2, 'w_out': 1,
            'w_up': 2, 'w_down': 1}
BIG = list(BIG_AXIS)
SMALL = ['g_mix_pre', 'g_vnorm', 'w_s', 'b_s', 'g_mem', 'b_gate', 'g_mix_post', 'g_ffn_pre', 'conv_b', 'g_ffn_post']
MESH = pl.DeviceIdType.MESH


def _pcall(body, **kw):
    return pl.pallas_call(body, **kw)


def _params(sem=None, vmem=VMEM_LIMIT):
    return pltpu.CompilerParams(dimension_semantics=sem, vmem_limit_bytes=vmem)


def _tile(n, cands):
    for c in cands:
        if n % c == 0:
            return c
    return n


_GELU_C = math.sqrt(2.0 / math.pi)
_GELU_A = 0.044715


def _gelu(x):
    return 0.5 * x * (1.0 + jnp.tanh(_GELU_C * (x + _GELU_A * (x * x * x))))


def _gelu_and_grad(x):
    x2 = x * x
    t = jnp.tanh(_GELU_C * (x + _GELU_A * (x2 * x)))
    val = 0.5 * x * (1.0 + t)
    grad = 0.5 * (1.0 + t) + 0.5 * x * (1.0 - t * t) * (_GELU_C * (1.0 + 3.0 * _GELU_A * x2))
    return val, grad


def _softplus(z):
    return jnp.maximum(z, 0.0) + jnp.log1p(jnp.exp(-jnp.abs(z)))


def _dot(a, b):
    return jnp.dot(a, b, preferred_element_type=F32)


def _dot_nt(a, b):
    return lax.dot_general(a, b, (((1,), (1,)), ((), ())), preferred_element_type=F32)


def _dot_tn(a, b):
    return lax.dot_general(a, b, (((0,), (0,)), ((), ())), preferred_element_type=F32)


def _split_dot(a, m):
    out = None
    rest = a
    for _ in range(SPLITS):
        piece = rest.astype(BF16)
        rest = rest - piece.astype(F32)
        term = _dot(piece, m)
        out = term if out is None else out + term
    return out


def _mm(a, b, mode, out_dtype, name, tm=None, tn=None, tk=None):
    if mode == 'nn':
        (m, kc), (kc2, n) = a.shape, b.shape
    elif mode == 'nt':
        (m, kc), (n, kc2) = a.shape, b.shape
    else:
        (kc, m), (kc2, n) = a.shape, b.shape
    assert kc == kc2, (a.shape, b.shape, mode)
    tm = tm or _tile(m, (1024, 512, 256, 128))
    tn = tn or _tile(n, (1024, 512, 256, 128))
    tk = tk or (kc if kc <= 2048 else _tile(kc, (2048, 1536, 1408, 1024, 512)))
    nk = kc // tk
    dot = {'nn': _dot, 'nt': _dot_nt, 'tn': _dot_tn}[mode]
    a_spec = pl.BlockSpec((tk, tm), lambda i, j, k: (k, i)) if mode == 'tn' else pl.BlockSpec((tm, tk), lambda i, j, k: (i, k))
    b_spec = pl.BlockSpec((tn, tk), lambda i, j, k: (j, k)) if mode == 'nt' else pl.BlockSpec((tk, tn), lambda i, j, k: (k, j))

    if nk == 1:
        def body(a_ref, b_ref, o_ref):
            o_ref[...] = dot(a_ref[...].astype(BF16), b_ref[...].astype(BF16)).astype(o_ref.dtype)
        scratch = []
    else:
        def body(a_ref, b_ref, o_ref, acc_ref):
            k = pl.program_id(2)
            part = dot(a_ref[...].astype(BF16), b_ref[...].astype(BF16))

            @pl.when(k == 0)
            def _():
                acc_ref[...] = part

            @pl.when(k > 0)
            def _():
                acc_ref[...] += part

            @pl.when(k == nk - 1)
            def _():
                o_ref[...] = acc_ref[...].astype(o_ref.dtype)
        scratch = [pltpu.VMEM((tm, tn), F32)]

    return _pcall(
        body, grid=(m // tm, n // tn, nk), in_specs=[a_spec, b_spec],
        out_specs=pl.BlockSpec((tm, tn), lambda i, j, k: (i, j)),
        out_shape=jax.ShapeDtypeStruct((m, n), out_dtype), scratch_shapes=scratch, name=name,
        compiler_params=_params(("parallel", "parallel", "arbitrary")))(a, b)


def _norm_fwd(x, g, res, out_dtype, name):
    s, d = x.shape
    tr = _tile(s, (256, 128))
    has_res = res is not None

    def body(*refs):
        x_ref, g_ref = refs[0], refs[1]
        o_ref = refs[-1]
        xv = x_ref[...]
        y = xv * lax.rsqrt(jnp.mean(xv * xv, axis=-1, keepdims=True) + EPS) * g_ref[...]
        if has_res:
            y = y + refs[2][...]
        o_ref[...] = y.astype(o_ref.dtype)

    row = pl.BlockSpec((tr, d), lambda i: (i, 0))
    ins = [x, g] + ([res] if has_res else [])
    return _pcall(
        body, grid=(s // tr,), in_specs=[row, pl.BlockSpec((1, d), lambda i: (0, 0))] + ([row] if has_res else []),
        out_specs=row, out_shape=jax.ShapeDtypeStruct((s, d), out_dtype), name=name,
        compiler_params=_params(("parallel",)))(*ins)


def _norm_bwd(x, g, douts, dres, out_dtype, name):
    s, d = x.shape
    tr = _tile(s, (256, 128))
    nd = len(douts)
    has_res = dres is not None

    def body(*refs):
        x_ref, g_ref = refs[0], refs[1]
        dx_ref, dg_ref = refs[-2], refs[-1]
        dout = refs[2][...].astype(F32)
        for r in refs[3:2 + nd]:
            dout = dout + r[...].astype(F32)
        xv = x_ref[...]
        r = lax.rsqrt(jnp.mean(xv * xv, axis=-1, keepdims=True) + EPS)
        n = xv * r
        dn = dout * g_ref[...]
        dx = r * (dn - n * jnp.mean(dn * n, axis=-1, keepdims=True))
        if has_res:
            dx = dx + refs[2 + nd][...]
        dx_ref[...] = dx.astype(dx_ref.dtype)

        @pl.when(pl.program_id(0) == 0)
        def _():
            dg_ref[...] = jnp.zeros_like(dg_ref)

        dg_ref[...] += jnp.sum(dout * n, axis=0, keepdims=True)

    row = pl.BlockSpec((tr, d), lambda i: (i, 0))
    vec = pl.BlockSpec((1, d), lambda i: (0, 0))
    ins = [x, g] + list(douts) + ([dres] if has_res else [])
    return _pcall(
        body, grid=(s // tr,), in_specs=[row, vec] + [row] * (nd + int(has_res)), out_specs=[row, vec],
        out_shape=[jax.ShapeDtypeStruct((s, d), out_dtype), jax.ShapeDtypeStruct((1, d), F32)], name=name,
        compiler_params=_params(("arbitrary",)))(*ins)


def _loss_head(y, target):
    s, d = y.shape
    tr = _tile(s, (256, 128))

    def body(y_ref, t_ref, sq_ref, dy_ref):
        e = y_ref[...] - t_ref[...]
        dy_ref[...] = e * (1.0 / d)

        @pl.when(pl.program_id(0) == 0)
        def _():
            sq_ref[...] = jnp.zeros_like(sq_ref)

        sq_ref[...] += jnp.sum(e * e, axis=0, keepdims=True)

    row = pl.BlockSpec((tr, d), lambda i: (i, 0))
    return _pcall(
        body, grid=(s // tr,), in_specs=[row, row], out_specs=[pl.BlockSpec((1, d), lambda i: (0, 0)), row],
        out_shape=[jax.ShapeDtypeStruct((1, d), F32), jax.ShapeDtypeStruct((s, d), F32)], name="loss_head",
        compiler_params=_params(("arbitrary",)))(y, target)


SB_QUERIES = 512


def _sb_sum_matrix(later):
    r = lax.broadcasted_iota(jnp.int32, (HEAD_SB, 2 * HEAD_SB), 0)
    c = lax.broadcasted_iota(jnp.int32, (HEAD_SB, 2 * HEAD_SB), 1)
    tri = jnp.where((r > c) if later else (r < c), 1.0, 0.0)
    return jnp.where(c < HEAD_SB, tri, 1.0).astype(BF16)


def _sb_mask(tq, q0, k0):
    row = lax.broadcasted_iota(jnp.int32, (tq, HEAD_SB), 0)
    col = lax.broadcasted_iota(jnp.int32, (tq, HEAD_SB), 1)
    return (k0 + col) < (q0 + row)


def _sb_fwd(proj, n_heads):
    s = proj.shape[0]
    tq = min(SB_QUERIES, s)
    per = tq // HEAD_SB
    scale = HEAD_SB ** -0.5

    def body(q_ref, k_ref, v_ref, o_ref, acc_ref, c_ref):
        i = pl.program_id(1)
        q = q_ref[...].astype(BF16)
        sums = _sb_sum_matrix(True)
        acc_ref[...] = jnp.zeros_like(acc_ref)
        c_ref[...] = jnp.zeros_like(c_ref)
        last = (i + 1) * per - 1

        def step(jj, masked):
            off = pl.multiple_of((last - jj) * HEAD_SB, HEAD_SB)
            kb = k_ref[pl.ds(off, HEAD_SB), :].astype(BF16)
            vb = v_ref[pl.ds(off, HEAD_SB), :].astype(BF16)
            z = _dot_nt(q, kb) * scale
            sp = _softplus(z)
            if masked:
                mask = _sb_mask(tq, i * tq, off)
                sp = jnp.where(mask, sp, 0.0)
            both = _split_dot(sp, sums)
            c = c_ref[...]
            a = jnp.exp(z - sp - both[:, :HEAD_SB] - c)
            if masked:
                a = jnp.where(mask, a, 0.0)
            acc_ref[...] += _dot(a.astype(BF16), vb)
            c_ref[...] = c + both[:, HEAD_SB:]

        def diagonal(jj, carry):
            step(jj, True)
            return carry

        def below(jj, carry):
            step(jj, False)
            return carry

        lax.fori_loop(0, per, diagonal, 0)
        lax.fori_loop(per, last + 1, below, 0)
        o_ref[...] = acc_ref[...].astype(o_ref.dtype)

    h = n_heads
    blk = pl.BlockSpec((tq, HEAD_SB), lambda hh, i: (i, hh))
    return _pcall(
        body, grid=(h, s // tq),
        in_specs=[blk, pl.BlockSpec((s, HEAD_SB), lambda hh, i: (0, h + hh)),
                  pl.BlockSpec((s, HEAD_SB), lambda hh, i: (0, 2 * h + hh))],
        out_specs=blk, out_shape=jax.ShapeDtypeStruct((s, h * HEAD_SB), BF16),
        scratch_shapes=[pltpu.VMEM((tq, HEAD_SB), F32), pltpu.VMEM((tq, HEAD_SB), F32)],
        name="sb_fwd", compiler_params=_params(("parallel", "arbitrary")))(proj, proj, proj)


def _sb_bwd(proj, do, n_heads):
    s = proj.shape[0]
    tq = min(SB_QUERIES, s)
    per = tq // HEAD_SB
    scale = HEAD_SB ** -0.5

    def body(q_ref, k_ref, v_ref, do_ref, dq_ref, dk_ref, dv_ref, g_ref, beta_ref, run_ref, acc_ref):
        i = pl.program_id(1)

        @pl.when(i == 0)
        def _():
            dk_ref[...] = jnp.zeros_like(dk_ref)
            dv_ref[...] = jnp.zeros_like(dv_ref)

        q = q_ref[...].astype(BF16)
        dob = do_ref[...].astype(BF16)
        last = (i + 1) * per - 1

        run_ref[...] = jnp.zeros_like(run_ref)
        later = _sb_sum_matrix(True)

        def left(jj, masked):
            j = last - jj
            off = pl.multiple_of(j * HEAD_SB, HEAD_SB)
            kb = k_ref[pl.ds(off, HEAD_SB), :].astype(BF16)
            vb = v_ref[pl.ds(off, HEAD_SB), :].astype(BF16)
            z = _dot_nt(q, kb) * scale
            sp = _softplus(z)
            beta = jnp.exp(z - sp)
            if masked:
                mask = _sb_mask(tq, i * tq, off)
                sp = jnp.where(mask, sp, 0.0)
            both = _split_dot(sp, later)
            c = run_ref[...]
            a = beta * jnp.exp(-(both[:, :HEAD_SB] + c))
            if masked:
                a = jnp.where(mask, a, 0.0)
            g_ref[j] = a * _dot_nt(dob, vb)
            beta_ref[j] = beta
            dv_ref[pl.ds(off, HEAD_SB), :] += _dot_tn(a.astype(BF16), dob)
            run_ref[...] = c + both[:, HEAD_SB:]

        def left_diagonal(jj, carry):
            left(jj, True)
            return carry

        def left_below(jj, carry):
            left(jj, False)
            return carry

        lax.fori_loop(0, per, left_diagonal, 0)
        lax.fori_loop(per, last + 1, left_below, 0)

        run_ref[...] = jnp.zeros_like(run_ref)
        acc_ref[...] = jnp.zeros_like(acc_ref)
        earlier = _sb_sum_matrix(False)

        def right(j, masked):
            off = pl.multiple_of(j * HEAD_SB, HEAD_SB)
            kb = k_ref[pl.ds(off, HEAD_SB), :].astype(BF16)
            g = g_ref[j]
            beta = beta_ref[j]
            both = _split_dot(g, earlier)
            p = run_ref[...]
            dz = (g * (1.0 - beta) - beta * (both[:, :HEAD_SB] + p)) * scale
            if masked:
                dz = jnp.where(_sb_mask(tq, i * tq, off), dz, 0.0)
            dzb = dz.astype(BF16)
            dk_ref[pl.ds(off, HEAD_SB), :] += _dot_tn(dzb, q)
            acc_ref[...] += _dot(dzb, kb)
            run_ref[...] = p + both[:, HEAD_SB:]

        def right_below(j, carry):
            right(j, False)
            return carry

        def right_diagonal(j, carry):
            right(j, True)
            return carry

        lax.fori_loop(0, last + 1 - per, right_below, 0)
        lax.fori_loop(last + 1 - per, last + 1, right_diagonal, 0)
        dq_ref[...] = acc_ref[...]

    h = n_heads
    blk = pl.BlockSpec((tq, HEAD_SB), lambda hh, i: (i, hh))
    col_blk = pl.BlockSpec((s, HEAD_SB), lambda hh, i: (0, hh))
    shape = jax.ShapeDtypeStruct((s, h * HEAD_SB), F32)
    nk = s // HEAD_SB
    return _pcall(
        body, grid=(h, s // tq),
        in_specs=[blk, pl.BlockSpec((s, HEAD_SB), lambda hh, i: (0, h + hh)),
                  pl.BlockSpec((s, HEAD_SB), lambda hh, i: (0, 2 * h + hh)), blk],
        out_specs=[blk, col_blk, col_blk], out_shape=[shape, shape, shape],
        scratch_shapes=[pltpu.VMEM((nk, tq, HEAD_SB), F32), pltpu.VMEM((nk, tq, HEAD_SB), F32),
                        pltpu.VMEM((tq, HEAD_SB), F32), pltpu.VMEM((tq, HEAD_SB), F32)],
        name="sb_bwd", compiler_params=_params(("parallel", "arbitrary")))(proj, proj, proj, do)


def _gm_mask():
    t = lax.broadcasted_iota(jnp.int32, (GROUP_GM, GROUP_GM), 0)
    s = lax.broadcasted_iota(jnp.int32, (GROUP_GM, GROUP_GM), 1)
    shift = CHUNK.bit_length() - 1
    return (s >> shift) <= (t >> shift)


def _gm_fwd(proj, g_vnorm, w_s, b_st, u_blk):
    s = proj.shape[0]
    groups = w_s.shape[0]
    w = groups * GROUP_GM

    def body(u_ref, v_ref, gv_ref, ws_ref, bst_ref, o_ref):
        ug = _gelu(u_ref[...])
        vg = _gelu(v_ref[...])
        vn = vg * lax.rsqrt(jnp.mean(vg * vg, axis=-1, keepdims=True) + EPS) * gv_ref[...]
        vnb = vn.astype(BF16)
        mask = _gm_mask()
        for g in range(groups):
            sl = slice(g * GROUP_GM, (g + 1) * GROUP_GM)
            wm = jnp.where(mask, ws_ref[g], 0.0).astype(BF16)
            mixed = _dot(wm, vnb[:, sl]) + bst_ref[:, g:g + 1]
            o_ref[:, sl] = (ug[:, sl] * mixed).astype(o_ref.dtype)

    return _pcall(
        body, grid=(s // GROUP_GM,),
        in_specs=[pl.BlockSpec((GROUP_GM, w), lambda c: (c, u_blk)), pl.BlockSpec((GROUP_GM, w), lambda c: (c, u_blk + 1)),
                  pl.BlockSpec((1, w), lambda c: (0, 0)), pl.BlockSpec((groups, GROUP_GM, GROUP_GM), lambda c: (0, 0, 0)),
                  pl.BlockSpec((GROUP_GM, groups), lambda c: (0, 0))],
        out_specs=pl.BlockSpec((GROUP_GM, w), lambda c: (c, 0)),
        out_shape=jax.ShapeDtypeStruct((s, w), BF16), name="gm_fwd",
        compiler_params=_params(("parallel",)))(proj, proj, g_vnorm, w_s, b_st)


def _gm_bwd(proj, g_vnorm, w_s, b_st, do, u_blk):
    s = proj.shape[0]
    groups = w_s.shape[0]
    w = groups * GROUP_GM

    def body(u_ref, v_ref, gv_ref, ws_ref, bst_ref, do_ref, du_ref, dv_ref, dgv_ref, dws_ref, dbst_ref, dvn_ref):
        @pl.when(pl.program_id(0) == 0)
        def _():
            dgv_ref[...] = jnp.zeros_like(dgv_ref)
            dws_ref[...] = jnp.zeros_like(dws_ref)
            dbst_ref[...] = jnp.zeros_like(dbst_ref)

        ug, ugrad = _gelu_and_grad(u_ref[...])
        vg, vgrad = _gelu_and_grad(v_ref[...])
        r = lax.rsqrt(jnp.mean(vg * vg, axis=-1, keepdims=True) + EPS)
        n = vg * r
        gv = gv_ref[...]
        vnb = (n * gv).astype(BF16)
        dout = do_ref[...]
        mask = _gm_mask()
        for g in range(groups):
            sl = slice(g * GROUP_GM, (g + 1) * GROUP_GM)
            wm = jnp.where(mask, ws_ref[g], 0.0).astype(BF16)
            mixed = _dot(wm, vnb[:, sl]) + bst_ref[:, g:g + 1]
            dmixed = dout[:, sl] * ug[:, sl]
            du_ref[:, sl] = dout[:, sl] * mixed * ugrad[:, sl]
            dbst_ref[:, g:g + 1] += jnp.sum(dmixed, axis=1, keepdims=True)
            dmb = dmixed.astype(BF16)
            dws_ref[g] += jnp.where(mask, _dot_nt(dmb, vnb[:, sl]), 0.0)
            dvn_ref[:, sl] = _dot_tn(wm, dmb)
        dvn = dvn_ref[...]
        dgv_ref[...] += jnp.sum(dvn * n, axis=0, keepdims=True)
        dn = dvn * gv
        dvg = r * (dn - n * jnp.mean(dn * n, axis=-1, keepdims=True))
        dv_ref[...] = dvg * vgrad

    rowb = pl.BlockSpec((GROUP_GM, w), lambda c: (c, 0))
    vec = pl.BlockSpec((1, w), lambda c: (0, 0))
    wsb = pl.BlockSpec((groups, GROUP_GM, GROUP_GM), lambda c: (0, 0, 0))
    bsb = pl.BlockSpec((GROUP_GM, groups), lambda c: (0, 0))
    return _pcall(
        body, grid=(s // GROUP_GM,),
        in_specs=[pl.BlockSpec((GROUP_GM, w), lambda c: (c, u_blk)), pl.BlockSpec((GROUP_GM, w), lambda c: (c, u_blk + 1)),
                  vec, wsb, bsb, rowb],
        out_specs=[rowb, rowb, vec, wsb, bsb],
        out_shape=[jax.ShapeDtypeStruct((s, w), F32), jax.ShapeDtypeStruct((s, w), F32), jax.ShapeDtypeStruct((1, w), F32),
                   jax.ShapeDtypeStruct((groups, GROUP_GM, GROUP_GM), F32), jax.ShapeDtypeStruct((GROUP_GM, groups), F32)],
        scratch_shapes=[pltpu.VMEM((GROUP_GM, w), F32)], name="gm_bwd",
        compiler_params=_params(("arbitrary",)))(proj, proj, g_vnorm, w_s, b_st, do)


def _xa_fwd(proj, mem_kv, q_blk, n_heads):
    s = proj.shape[0]
    nm = mem_kv.shape[0]
    tq = _tile(s, (512, 256, 128))
    scale = HEAD_XA ** -0.5

    def body(q_ref, k_ref, v_ref, o_ref):
        z = _dot_nt(q_ref[...].astype(BF16), k_ref[...].astype(BF16)) * scale
        z = z - jnp.max(z, axis=-1, keepdims=True)
        e = jnp.exp(z)
        p = e / jnp.sum(e, axis=-1, keepdims=True)
        o_ref[...] = _dot(p.astype(BF16), v_ref[...].astype(BF16)).astype(o_ref.dtype)

    h = n_heads
    return _pcall(
        body, grid=(h, s // tq),
        in_specs=[pl.BlockSpec((tq, HEAD_XA), lambda hh, i: (i, q_blk + hh)),
                  pl.BlockSpec((nm, HEAD_XA), lambda hh, i: (0, hh)), pl.BlockSpec((nm, HEAD_XA), lambda hh, i: (0, h + hh))],
        out_specs=pl.BlockSpec((tq, HEAD_XA), lambda hh, i: (i, hh)),
        out_shape=jax.ShapeDtypeStruct((s, h * HEAD_XA), BF16), name="xa_fwd",
        compiler_params=_params(("parallel", "parallel")))(proj, mem_kv, mem_kv)


def _xa_bwd(proj, mem_kv, do, q_blk, n_heads):
    s = proj.shape[0]
    nm = mem_kv.shape[0]
    tq = _tile(s, (512, 256, 128))
    scale = HEAD_XA ** -0.5
    h = n_heads

    def body(q_ref, k_ref, v_ref, do_ref, dq_ref, dk_ref, dv_ref):
        @pl.when(pl.program_id(1) == 0)
        def _():
            dk_ref[...] = jnp.zeros_like(dk_ref)
            dv_ref[...] = jnp.zeros_like(dv_ref)

        qb = q_ref[...].astype(BF16)
        kb = k_ref[...].astype(BF16)
        vb = v_ref[...].astype(BF16)
        dob = do_ref[...].astype(BF16)
        z = _dot_nt(qb, kb) * scale
        z = z - jnp.max(z, axis=-1, keepdims=True)
        e = jnp.exp(z)
        p = e / jnp.sum(e, axis=-1, keepdims=True)
        dp = _dot_nt(dob, vb)
        dz = (p * (dp - jnp.sum(dp * p, axis=-1, keepdims=True)) * scale).astype(BF16)
        dq_ref[...] = _dot(dz, kb)
        dk_ref[...] += _dot_tn(dz, qb)
        dv_ref[...] += _dot_tn(p.astype(BF16), dob)

    qspec = pl.BlockSpec((tq, HEAD_XA), lambda hh, i: (i, hh))
    dk, dv = None, None
    dq, dk, dv = _pcall(
        body, grid=(h, s // tq),
        in_specs=[pl.BlockSpec((tq, HEAD_XA), lambda hh, i: (i, q_blk + hh)),
                  pl.BlockSpec((nm, HEAD_XA), lambda hh, i: (0, hh)), pl.BlockSpec((nm, HEAD_XA), lambda hh, i: (0, h + hh)),
                  qspec],
        out_specs=[qspec, pl.BlockSpec((nm, HEAD_XA), lambda hh, i: (0, hh)), pl.BlockSpec((nm, HEAD_XA), lambda hh, i: (0, hh))],
        out_shape=[jax.ShapeDtypeStruct((s, h * HEAD_XA), F32), jax.ShapeDtypeStruct((nm, h * HEAD_XA), F32),
                   jax.ShapeDtypeStruct((nm, h * HEAD_XA), F32)],
        name="xa_bwd", compiler_params=_params(("parallel", "arbitrary")))(proj, mem_kv, mem_kv, do)
    return dq, dk, dv


def _merge_fwd(zg, b_gate, branches):
    s, d = branches[0].shape
    tr = _tile(s, (128,))

    def body(z0, z1, z2, g0, g1, g2, b0, b1, b2, o_ref):
        acc = None
        for z, g, b in ((z0, g0, b0), (z1, g1, b1), (z2, g2, b2)):
            term = jax.nn.sigmoid(z[...] + g[...]) * b[...]
            acc = term if acc is None else acc + term
        o_ref[...] = acc.astype(o_ref.dtype)

    zs = [pl.BlockSpec((tr, d), functools.partial(lambda i, k: (i, k), k=k)) for k in range(3)]
    gs = [pl.BlockSpec((1, d), functools.partial(lambda i, k: (0, k), k=k)) for k in range(3)]
    row = pl.BlockSpec((tr, d), lambda i: (i, 0))
    return _pcall(
        body, grid=(s // tr,), in_specs=zs + gs + [row] * 3, out_specs=row,
        out_shape=jax.ShapeDtypeStruct((s, d), BF16), name="merge_fwd",
        compiler_params=_params(("parallel",)))(zg, zg, zg, b_gate, b_gate, b_gate, *branches)


def _merge_bwd(zg, b_gate, branches, dmerged):
    s, d = branches[0].shape
    tr = _tile(s, (128,))

    def body(z0, z1, z2, g0, g1, g2, b0, b1, b2, dm_ref, dz_ref, d0, d1, d2, dbg_ref):
        @pl.when(pl.program_id(0) == 0)
        def _():
            dbg_ref[...] = jnp.zeros_like(dbg_ref)

        dm = dm_ref[...]
        for k, (z, g, b, dbr) in enumerate(((z0, g0, b0, d0), (z1, g1, b1, d1), (z2, g2, b2, d2))):
            sg = jax.nn.sigmoid(z[...] + g[...])
            dbr[...] = (dm * sg).astype(dbr.dtype)
            dz = dm * b[...] * sg * (1.0 - sg)
            dz_ref[:, k * d:(k + 1) * d] = dz.astype(dz_ref.dtype)
            dbg_ref[:, k * d:(k + 1) * d] += jnp.sum(dz, axis=0, keepdims=True)

    zs = [pl.BlockSpec((tr, d), functools.partial(lambda i, k: (i, k), k=k)) for k in range(3)]
    gs = [pl.BlockSpec((1, d), functools.partial(lambda i, k: (0, k), k=k)) for k in range(3)]
    row = pl.BlockSpec((tr, d), lambda i: (i, 0))
    outs = _pcall(
        body, grid=(s // tr,), in_specs=zs + gs + [row] * 4,
        out_specs=[pl.BlockSpec((tr, 3 * d), lambda i: (i, 0)), row, row, row, pl.BlockSpec((1, 3 * d), lambda i: (0, 0))],
        out_shape=[jax.ShapeDtypeStruct((s, 3 * d), BF16)] + [jax.ShapeDtypeStruct((s, d), BF16)] * 3
        + [jax.ShapeDtypeStruct((1, 3 * d), F32)],
        name="merge_bwd", compiler_params=_params(("arbitrary",)))(zg, zg, zg, b_gate, b_gate, b_gate, *branches, dmerged)
    return outs[0], list(outs[1:4]), outs[4]


def _shift_down(x, k, row):
    return jnp.where(row >= k, pltpu.roll(x, k, 0), 0.0)


def _shift_up(x, k, row, s):
    return jnp.where(row < s - k, pltpu.roll(x, s - k, 0), 0.0)


def _conv_pre(gate, cw_ref, cb_ref, row):
    conv = cb_ref[...] + cw_ref[CONV_TAPS - 1:CONV_TAPS, :] * gate
    for k in range(1, CONV_TAPS):
        conv = conv + cw_ref[CONV_TAPS - 1 - k:CONV_TAPS - k, :] * _shift_down(gate, k, row)
    return conv


def _cg_fwd(up, conv_w, conv_b):
    s = up.shape[0]
    f = conv_w.shape[1]
    tc = _tile(f, (256, 128))
    nb = f // tc

    def body(g_ref, v_ref, cw_ref, cb_ref, o_ref):
        row = lax.broadcasted_iota(jnp.int32, (s, tc), 0)
        conv = _conv_pre(g_ref[...], cw_ref, cb_ref, row)
        o_ref[...] = (_gelu(conv) * v_ref[...]).astype(o_ref.dtype)

    return _pcall(
        body, grid=(nb,),
        in_specs=[pl.BlockSpec((s, tc), lambda j: (0, j)), pl.BlockSpec((s, tc), lambda j: (0, nb + j)),
                  pl.BlockSpec((CONV_TAPS, tc), lambda j: (0, j)), pl.BlockSpec((1, tc), lambda j: (0, j))],
        out_specs=pl.BlockSpec((s, tc), lambda j: (0, j)),
        out_shape=jax.ShapeDtypeStruct((s, f), BF16), name="cg_fwd",
        compiler_params=_params(("parallel",)))(up, up, conv_w, conv_b)


def _cg_bwd(up, conv_w, conv_b, dact):
    s = up.shape[0]
    f = conv_w.shape[1]
    tc = _tile(f, (256, 128))
    nb = f // tc

    def body(g_ref, v_ref, cw_ref, cb_ref, da_ref, dg_ref, dv_ref, dcw_ref, dcb_ref):
        row = lax.broadcasted_iota(jnp.int32, (s, tc), 0)
        gate = g_ref[...]
        conv = _conv_pre(gate, cw_ref, cb_ref, row)
        gel, ggrad = _gelu_and_grad(conv)
        da = da_ref[...]
        dv_ref[...] = (da * gel).astype(dv_ref.dtype)
        dconv = da * v_ref[...] * ggrad
        dgate = cw_ref[CONV_TAPS - 1:CONV_TAPS, :] * dconv
        dcw_ref[CONV_TAPS - 1:CONV_TAPS, :] = jnp.sum(dconv * gate, axis=0, keepdims=True)
        for k in range(1, CONV_TAPS):
            dgate = dgate + cw_ref[CONV_TAPS - 1 - k:CONV_TAPS - k, :] * _shift_up(dconv, k, row, s)
            dcw_ref[CONV_TAPS - 1 - k:CONV_TAPS - k, :] = jnp.sum(dconv * _shift_down(gate, k, row), axis=0, keepdims=True)
        dg_ref[...] = dgate.astype(dg_ref.dtype)
        dcb_ref[...] = jnp.sum(dconv, axis=0, keepdims=True)

    colb = pl.BlockSpec((s, tc), lambda j: (0, j))
    return _pcall(
        body, grid=(nb,),
        in_specs=[colb, pl.BlockSpec((s, tc), lambda j: (0, nb + j)), pl.BlockSpec((CONV_TAPS, tc), lambda j: (0, j)),
                  pl.BlockSpec((1, tc), lambda j: (0, j)), colb],
        out_specs=[colb, colb, pl.BlockSpec((CONV_TAPS, tc), lambda j: (0, j)), pl.BlockSpec((1, tc), lambda j: (0, j))],
        out_shape=[jax.ShapeDtypeStruct((s, f), BF16), jax.ShapeDtypeStruct((s, f), BF16),
                   jax.ShapeDtypeStruct((CONV_TAPS, f), F32), jax.ShapeDtypeStruct((1, f), F32)],
        name="cg_bwd", compiler_params=_params(("parallel",)))(up, up, conv_w, conv_b, dact)


def _row_tile(rows, cols):
    want = max(16, (256 * 1024) // cols)
    for c in (512, 256, 128, 64, 32, 16):
        if c <= want and rows % c == 0:
            return c
    return rows


def _sum_halves(dwv, recv, core, name):
    nj, _, a, c = dwv.shape
    tr = _row_tile(a, c)

    def body(core_ref, d_ref, r_ref, o_ref):
        o_ref[0] = (d_ref[0, 0].astype(F32) + r_ref[0].astype(F32)).astype(o_ref.dtype)

    grid_spec = pltpu.PrefetchScalarGridSpec(
        num_scalar_prefetch=1, grid=(nj, a // tr),
        in_specs=[pl.BlockSpec((1, 1, tr, c), lambda j, i, cr: (j, cr[0], i, 0)),
                  pl.BlockSpec((1, tr, c), lambda j, i, cr: (j, i, 0))],
        out_specs=pl.BlockSpec((1, tr, c), lambda j, i, cr: (j, i, 0)))
    return _pcall(body, grid_spec=grid_spec, out_shape=jax.ShapeDtypeStruct((nj, a, c), BF16), name=name,
                  compiler_params=_params(("parallel", "parallel")))(core, dwv, recv)


def _sum_chips(recv, own, joined, layer, n_layers, chip, core, ax):
    _, a, b = recv.shape
    tr = _row_tile(a, b)

    def body(chip_ref, core_ref, r_ref, own_ref, *rest):
        o_ref = rest[-1]
        me = chip_ref[0]
        mine = own_ref[0].astype(F32)
        acc = None
        for k in range(N_CHIPS):
            term = jnp.where(me == k, mine, r_ref[k].astype(F32))
            acc = term if acc is None else acc + term
        o_ref[0, 0] = acc

    own_spec = (pl.BlockSpec((1, tr, b), lambda i, ch, co: (0, i, ch[0])) if ax == 2
                else pl.BlockSpec((1, tr, b), lambda i, ch, co: (ch[0], i, 0)))
    in_specs = [pl.BlockSpec((N_CHIPS, tr, b), lambda i, ch, co: (0, i, 0)), own_spec]
    args = [chip, core, recv, own]
    aliases = {}
    if joined is not None:
        in_specs.append(pl.BlockSpec(memory_space=pl.ANY))
        args.append(joined)
        aliases = {4: 0}
    grid_spec = pltpu.PrefetchScalarGridSpec(
        num_scalar_prefetch=2, grid=(a // tr,), in_specs=in_specs,
        out_specs=pl.BlockSpec((1, 1, tr, b), lambda i, ch, co: (layer, co[0], i, 0)))
    return _pcall(body, grid_spec=grid_spec, out_shape=jax.ShapeDtypeStruct((n_layers, 2, a, b), F32),
                  input_output_aliases=aliases, name="sum_chips", compiler_params=_params(("parallel",)))(*args)


def _place_own(wt, layer, chip, ax):
    nl, r, c = wt.shape
    half = r // 2
    tr = _row_tile(half, c)
    nb = half // tr

    def body(chip_ref, w_ref, o_ref):
        o_ref[...] = w_ref[...].astype(BF16).reshape(o_ref.shape)

    if ax == 2:
        out_spec = pl.BlockSpec((1, tr, c), lambda h, i, ch: (h, i, ch[0]))
    else:
        out_spec = pl.BlockSpec((1, 1, tr, c), lambda h, i, ch: (ch[0], h, i, 0))
    grid_spec = pltpu.PrefetchScalarGridSpec(
        num_scalar_prefetch=1, grid=(2, nb),
        in_specs=[pl.BlockSpec((1, tr, c), lambda h, i, ch: (layer, h * nb + i, 0))], out_specs=out_spec)
    return _pcall(body, grid_spec=grid_spec, out_shape=jax.ShapeDtypeStruct(_full_view_shape(wt.shape, ax), BF16),
                  name="place_own", compiler_params=_params(("parallel", "parallel")))(chip, wt)


def _adamw(w, m, v, g, name):
    nl, r, c = w.shape
    tr = _row_tile(r, c)
    c1 = 1.0 - ADAM_B1 ** ADAM_STEP
    c2 = 1.0 - ADAM_B2 ** ADAM_STEP

    def body(w_ref, m_ref, v_ref, gin_ref, g_ref, d_ref, nm_ref, nv_ref):
        g = gin_ref[...]
        mm = ADAM_B1 * m_ref[...] + (1.0 - ADAM_B1) * g
        vv = ADAM_B2 * v_ref[...] + (1.0 - ADAM_B2) * (g * g)
        g_ref[...] = g
        nm_ref[...] = mm
        nv_ref[...] = vv
        d_ref[...] = -ADAM_LR * ((mm / c1) / (jnp.sqrt(vv / c2) + ADAM_EPS) + ADAM_WD * w_ref[...])

    blk = pl.BlockSpec((1, tr, c), lambda l, i: (l, i, 0))
    shape = jax.ShapeDtypeStruct((nl, r, c), F32)
    return _pcall(
        body, grid=(nl, r // tr), in_specs=[blk] * 4, out_specs=[blk] * 4, out_shape=[shape] * 4, name=name,
        compiler_params=_params(("parallel", "parallel")))(w, m, v, g)


HBM_SPEC = pl.BlockSpec(memory_space=pltpu.HBM)
COMM = pltpu.CompilerParams(has_side_effects=True)


def _position():
    x, y, c = lax.axis_index("x"), lax.axis_index("y"), lax.axis_index("c")
    chips = [(1 - x, y), (x, 1 - y), (1 - x, 1 - y)]
    return x, y, c, chips


def _remote(src, dst, send_sem, recv_sem, dev):
    return pltpu.make_async_remote_copy(src_ref=src, dst_ref=dst, send_sem=send_sem, recv_sem=recv_sem,
                                        device_id=dev, device_id_type=MESH)


def _full_view_shape(shard_shape, ax):
    _, r, c = shard_shape
    return (2, r // 2, c * N_CHIPS) if ax == 2 else (N_CHIPS, 2, r // 2, c)


def _piece(ref, ax, j, h, cs):
    if ax == 2:
        return ref.at[h, :, pl.ds(pl.multiple_of(j * cs, cs), cs)]
    return ref.at[j, h]


def _chip_block(ref, ax, j, cs):
    if ax == 2:
        return ref.at[:, :, pl.ds(pl.multiple_of(j * cs, cs), cs)]
    return ref.at[j]


def _gather_ici(views, axes, shard_cols):
    n = len(views)

    def body(*refs):
        outs = refs[n:2 * n]
        send_sems, recv_sems = refs[2 * n:]
        x, y, c, chips = _position()
        me = 2 * x + y
        remote = []
        for t, ax in enumerate(axes):
            mine = _piece(outs[t], ax, me, c, shard_cols[t])
            for p, (px, py) in enumerate(chips):
                k = t * 3 + p
                remote.append(_remote(mine, mine, send_sems.at[k], recv_sems.at[k], (px, py, c)))
                remote[-1].start()
        for t, ax in enumerate(axes):
            for p, (px, py) in enumerate(chips):
                k = t * 3 + p
                theirs = _piece(outs[t], ax, 2 * px + py, c, shard_cols[t])
                _remote(theirs, theirs, send_sems.at[k], recv_sems.at[k], (px, py, c)).wait_recv()
        for cp in remote:
            cp.wait_send()

    return _pcall(
        body, in_specs=[HBM_SPEC] * n, out_specs=[HBM_SPEC] * n,
        out_shape=[jax.ShapeDtypeStruct(v.shape, v.dtype) for v in views],
        input_output_aliases={t: t for t in range(n)},
        scratch_shapes=[pltpu.SemaphoreType.DMA((3 * n,)), pltpu.SemaphoreType.DMA((3 * n,))],
        name="gather_ici", compiler_params=COMM)(*views)


def _gather_d2d(views, axes, shard_cols):
    n = len(views)

    def body(*refs):
        outs = refs[n:2 * n]
        send_sems, recv_sems = refs[2 * n:]
        x, y, c, chips = _position()
        remote = []
        for t, ax in enumerate(axes):
            for p, (px, py) in enumerate(chips):
                k = t * 3 + p
                blk = _piece(outs[t], ax, 2 * px + py, c, shard_cols[t])
                remote.append(_remote(blk, blk, send_sems.at[k], recv_sems.at[k], (x, y, 1 - c)))
                remote[-1].start()
        for t, ax in enumerate(axes):
            for p, (px, py) in enumerate(chips):
                k = t * 3 + p
                blk = _piece(outs[t], ax, 2 * px + py, 1 - c, shard_cols[t])
                _remote(blk, blk, send_sems.at[k], recv_sems.at[k], (x, y, 1 - c)).wait_recv()
        for cp in remote:
            cp.wait_send()

    return _pcall(
        body, in_specs=[HBM_SPEC] * n, out_specs=[HBM_SPEC] * n,
        out_shape=[jax.ShapeDtypeStruct(v.shape, v.dtype) for v in views],
        input_output_aliases={t: t for t in range(n)},
        scratch_shapes=[pltpu.SemaphoreType.DMA((3 * n,)), pltpu.SemaphoreType.DMA((3 * n,))],
        name="gather_d2d", compiler_params=COMM)(*views)


def _grads_d2d(dwvs):
    n = len(dwvs)

    def body(*refs):
        ins, outs = refs[:n], refs[n:2 * n]
        send_sems, recv_sems = refs[2 * n:]
        x, y, c, _ = _position()
        remote = [_remote(ins[t].at[:, 1 - c], outs[t], send_sems.at[t], recv_sems.at[t], (x, y, 1 - c)) for t in range(n)]
        for cp in remote:
            cp.start()
        for cp in remote:
            cp.wait()

    return _pcall(
        body, in_specs=[HBM_SPEC] * n, out_specs=[HBM_SPEC] * n,
        out_shape=[jax.ShapeDtypeStruct((d.shape[0],) + d.shape[2:], d.dtype) for d in dwvs],
        scratch_shapes=[pltpu.SemaphoreType.DMA((n,)), pltpu.SemaphoreType.DMA((n,))],
        name="grads_d2d", compiler_params=COMM)(*dwvs)


def _grads_ici(sums, axes):
    n = len(sums)
    out_shapes = []
    for sm, ax in zip(sums, axes):
        _, a, c = sm.shape
        out_shapes.append(jax.ShapeDtypeStruct((N_CHIPS, a, c // N_CHIPS if ax == 2 else c), sm.dtype))

    def body(*refs):
        ins, outs = refs[:n], refs[n:2 * n]
        send_sems, recv_sems = refs[2 * n:]
        x, y, c, chips = _position()
        me = 2 * x + y

        def block(t, j):
            if axes[t] == 2:
                cs = outs[t].shape[2]
                return ins[t].at[0, :, pl.ds(pl.multiple_of(j * cs, cs), cs)]
            return ins[t].at[j]

        remote = []
        for t in range(n):
            for p, (px, py) in enumerate(chips):
                k = t * 3 + p
                remote.append(_remote(block(t, 2 * px + py), outs[t].at[me], send_sems.at[k], recv_sems.at[k], (px, py, c)))
                remote[-1].start()
        for t in range(n):
            for p, (px, py) in enumerate(chips):
                k = t * 3 + p
                peer = 2 * px + py
                _remote(block(t, peer), outs[t].at[peer], send_sems.at[k], recv_sems.at[k], (px, py, c)).wait_recv()
        for cp in remote:
            cp.wait_send()

    return _pcall(
        body, in_specs=[HBM_SPEC] * n, out_specs=[HBM_SPEC] * n, out_shape=out_shapes,
        scratch_shapes=[pltpu.SemaphoreType.DMA((3 * n,)), pltpu.SemaphoreType.DMA((3 * n,))],
        name="grads_ici", compiler_params=COMM)(*sums)


def _join_halves(joined):
    n = len(joined)

    def body(*refs):
        outs = refs[n:2 * n]
        send_sems, recv_sems = refs[2 * n:]
        x, y, c, _ = _position()
        remote = [_remote(outs[t].at[:, c], outs[t].at[:, c], send_sems.at[t], recv_sems.at[t], (x, y, 1 - c)) for t in range(n)]
        for cp in remote:
            cp.start()
        for t in range(n):
            _remote(outs[t].at[:, 1 - c], outs[t].at[:, 1 - c], send_sems.at[t], recv_sems.at[t], (x, y, 1 - c)).wait_recv()
        for cp in remote:
            cp.wait_send()

    return _pcall(
        body, in_specs=[HBM_SPEC] * n, out_specs=[HBM_SPEC] * n,
        out_shape=[jax.ShapeDtypeStruct(j.shape, j.dtype) for j in joined],
        input_output_aliases={t: t for t in range(n)},
        scratch_shapes=[pltpu.SemaphoreType.DMA((n,)), pltpu.SemaphoreType.DMA((n,))],
        name="join_halves", compiler_params=COMM)(*joined)


def _gather_small(shard):
    nl, r, cs = shard.shape

    def body(in_ref, out_ref, send_sems, recv_sems, local_sem):
        x, y, c, chips = _position()

        def cols(j):
            return out_ref.at[:, :, pl.ds(pl.multiple_of(j * cs, cs), cs)]

        me = 2 * x + y
        loc = pltpu.make_async_copy(in_ref, cols(me), local_sem)
        loc.start()
        remote = [_remote(in_ref, cols(me), send_sems.at[p], recv_sems.at[p], (px, py, c)) for p, (px, py) in enumerate(chips)]
        for cp in remote:
            cp.start()
        for p, (px, py) in enumerate(chips):
            _remote(in_ref, cols(2 * px + py), send_sems.at[p], recv_sems.at[p], (px, py, c)).wait_recv()
        for cp in remote:
            cp.wait_send()
        loc.wait()

    return _pcall(
        body, in_specs=[HBM_SPEC], out_specs=HBM_SPEC, out_shape=jax.ShapeDtypeStruct((nl, r, cs * N_CHIPS), shard.dtype),
        scratch_shapes=[pltpu.SemaphoreType.DMA((3,)), pltpu.SemaphoreType.DMA((3,)), pltpu.SemaphoreType.DMA(())],
        name="gather_small", compiler_params=COMM)(shard)


def _all_reduce_small(pack):
    r, c = pack.shape

    def body(in_ref, out_ref, slots, send_sems, recv_sems, local_sem):
        x, y, cc, _ = _position()
        me = 4 * x + 2 * y + cc
        peers = []
        for k in range(1, N_DEV):
            fx, fy, fc = (k >> 2) & 1, (k >> 1) & 1, k & 1
            px, py, pc = x ^ fx, y ^ fy, cc ^ fc
            peers.append((k - 1, (px, py, pc), 4 * px + 2 * py + pc))
        loc = pltpu.make_async_copy(in_ref, slots.at[me], local_sem)
        loc.start()
        copies = [_remote(in_ref, slots.at[me], send_sems.at[k], recv_sems.at[k], dev) for k, dev, _ in peers]
        for cp in copies:
            cp.start()
        for k, dev, idx in peers:
            _remote(in_ref, slots.at[idx], send_sems.at[k], recv_sems.at[k], dev).wait_recv()
        for cp in copies:
            cp.wait_send()
        loc.wait()
        acc = slots[0]
        for k in range(1, N_DEV):
            acc = acc + slots[k]
        out_ref[...] = acc

    vm = pl.BlockSpec(memory_space=pltpu.VMEM)
    return _pcall(
        body, in_specs=[vm], out_specs=vm, out_shape=jax.ShapeDtypeStruct((r, c), F32),
        scratch_shapes=[pltpu.VMEM((N_DEV, r, c), F32), pltpu.SemaphoreType.DMA((N_DEV - 1,)),
                        pltpu.SemaphoreType.DMA((N_DEV - 1,)), pltpu.SemaphoreType.DMA(())],
        name="all_reduce_small", compiler_params=pltpu.CompilerParams(has_side_effects=True, vmem_limit_bytes=VMEM_LIMIT))(pack)


def _dims(d):
    half = d // 2
    return half // HEAD_SB, half // HEAD_XA, 3, (5 * half) // HEAD_XA


def _layer_fwd(x, mem, full, small, l):
    h_sb, h_xa, u_blk, q_blk = _dims(x.shape[1])

    def vec(name):
        return small[name][l].reshape(1, -1)

    h1 = _norm_fwd(x, vec('g_mix_pre'), None, BF16, "norm_mix_pre")
    proj = _mm(h1, full['w_in'], 'nn', F32, "mm_proj")
    o_sb = _sb_fwd(proj, h_sb)
    b_st = small['b_s'][l].T
    o_gm = _gm_fwd(proj, vec('g_vnorm'), small['w_s'][l], b_st, u_blk)
    memn = _norm_fwd(mem, vec('g_mem'), None, BF16, "norm_mem")
    mem_kv = _mm(memn, full['w_mem_kv'], 'nn', F32, "mm_mem_kv")
    o_xa = _xa_fwd(proj, mem_kv, q_blk, h_xa)
    zg = _mm(h1, full['w_gate'], 'nn', F32, "mm_gate")
    branches = [_mm(o, full[wn], 'nn', F32, "mm_branch")
                for o, wn in ((o_sb, 'w_br_sb'), (o_gm, 'w_br_gm'), (o_xa, 'w_br_xa'))]
    merged = _merge_fwd(zg, vec('b_gate'), branches)
    y1 = _mm(merged, full['w_out'], 'nn', F32, "mm_out")
    x1 = _norm_fwd(y1, vec('g_mix_post'), x, F32, "norm_mix_post")
    h2 = _norm_fwd(x1, vec('g_ffn_pre'), None, BF16, "norm_ffn_pre")
    up = _mm(h2, full['w_up'], 'nn', F32, "mm_up")
    act = _cg_fwd(up, full['conv_w'], vec('conv_b'))
    y2 = _mm(act, full['w_down'], 'nn', F32, "mm_down")
    x2 = _norm_fwd(y2, vec('g_ffn_post'), x1, F32, "norm_ffn_post")
    saved = dict(x0=x, h1=h1, proj=proj, o_sb=o_sb, o_gm=o_gm, o_xa=o_xa, memn=memn, mem_kv=mem_kv, zg=zg,
                 branches=branches, merged=merged, y1=y1, x1=x1, h2=h2, up=up, act=act, y2=y2, b_st=b_st)
    return x2, saved


def _layer_bwd(dx, mem, sv, full, small, l):
    h_sb, h_xa, u_blk, q_blk = _dims(dx.shape[1])

    def vec(name):
        return small[name][l].reshape(1, -1)

    gb, gs = {}, {}
    dy2, gs['g_ffn_post'] = _norm_bwd(sv['y2'], vec('g_ffn_post'), [dx], None, BF16, "norm_ffn_post_bwd")
    gb['w_down'] = _mm(sv['act'], dy2, 'tn', BF16, "mm_down_dw")
    dact = _mm(dy2, full['w_down'], 'nt', F32, "mm_down_dx")
    dgate, dval, gs['conv_w'], gs['conv_b'] = _cg_bwd(sv['up'], full['conv_w'], vec('conv_b'), dact)
    dup = jnp.concatenate([dgate, dval], axis=1)
    gb['w_up'] = _mm(sv['h2'], dup, 'tn', BF16, "mm_up_dw")
    dh2 = _mm(dup, full['w_up'], 'nt', F32, "mm_up_dx")
    dx1, gs['g_ffn_pre'] = _norm_bwd(sv['x1'], vec('g_ffn_pre'), [dh2], dx, F32, "norm_ffn_pre_bwd")
    dy1, gs['g_mix_post'] = _norm_bwd(sv['y1'], vec('g_mix_post'), [dx1], None, BF16, "norm_mix_post_bwd")
    gb['w_out'] = _mm(sv['merged'], dy1, 'tn', BF16, "mm_out_dw")
    dmerged = _mm(dy1, full['w_out'], 'nt', F32, "mm_out_dx")
    dzg, dbr, gs['b_gate'] = _merge_bwd(sv['zg'], vec('b_gate'), sv['branches'], dmerged)
    douts = []
    for o, db, wn in ((sv['o_sb'], dbr[0], 'w_br_sb'), (sv['o_gm'], dbr[1], 'w_br_gm'), (sv['o_xa'], dbr[2], 'w_br_xa')):
        gb[wn] = _mm(o, db, 'tn', BF16, "mm_branch_dw")
        douts.append(_mm(db, full[wn], 'nt', F32, "mm_branch_dx"))
    gb['w_gate'] = _mm(sv['h1'], dzg, 'tn', BF16, "mm_gate_dw")
    dh1_gate = _mm(dzg, full['w_gate'], 'nt', F32, "mm_gate_dx")
    dq_xa, dk_xa, dv_xa = _xa_bwd(sv['proj'], sv['mem_kv'], douts[2], q_blk, h_xa)
    dmem_kv = jnp.concatenate([dk_xa, dv_xa], axis=1).astype(BF16)
    gb['w_mem_kv'] = _mm(sv['memn'], dmem_kv, 'tn', BF16, "mm_mem_kv_dw")
    dmemn = _mm(dmem_kv, full['w_mem_kv'], 'nt', F32, "mm_mem_kv_dx")
    _, gs['g_mem'] = _norm_bwd(mem, vec('g_mem'), [dmemn], None, BF16, "norm_mem_bwd")
    du, dv, gs['g_vnorm'], gs['w_s'], db_st = _gm_bwd(sv['proj'], vec('g_vnorm'), small['w_s'][l], sv['b_st'], douts[1], u_blk)
    gs['b_s'] = db_st.T
    dq, dk, dvv = _sb_bwd(sv['proj'], douts[0], h_sb)
    dproj = jnp.concatenate([dq, dk, dvv, du, dv, dq_xa], axis=1).astype(BF16)
    gb['w_in'] = _mm(sv['h1'], dproj, 'tn', BF16, "mm_proj_dw")
    dh1_proj = _mm(dproj, full['w_in'], 'nt', F32, "mm_proj_dx")
    dx0, gs['g_mix_pre'] = _norm_bwd(sv['x0'], vec('g_mix_pre'), [dh1_gate, dh1_proj], dx1, F32, "norm_mix_pre_bwd")
    return dx0, gb, gs


def _local_step(x, mem, target, full, small):
    n_layers = len(full['w_in'])
    saved = []
    for l in range(n_layers):
        x, sv = _layer_fwd(x, mem, {n: full[n][l] for n in full}, small, l)
        saved.append(sv)
    sq, dx = _loss_head(x, target)
    gbig = {n: [None] * n_layers for n in BIG}
    gsmall = {n: [None] * n_layers for n in SMALL + ['conv_w']}
    for l in reversed(range(n_layers)):
        dx, gb, gs = _layer_bwd(dx, mem, saved[l], {n: full[n][l] for n in full}, small, l)
        for n in gb:
            gbig[n][l] = gb[n]
        for n in gs:
            gsmall[n][l] = gs[n]
    return sq, dx, gbig, gsmall


def _pack(arrays, rows_multiple):
    flat = jnp.concatenate([a.reshape(-1).astype(F32) for a in arrays])
    rows = -(-flat.shape[0] // LANES)
    rows = -(-rows // rows_multiple) * rows_multiple
    return jnp.pad(flat, (0, rows * LANES - flat.shape[0])).reshape(rows, LANES)


def _unpack(pack, like):
    flat = pack.reshape(-1)
    out, off = [], 0
    for a in like:
        out.append(flat[off:off + a.size].reshape(a.shape))
        off += a.size
    return out


def _grad_view(g, ax):
    r, c = g.shape
    return g.reshape(1, 2, r // 2, c) if ax == 2 else g.reshape(N_CHIPS, 2, r // (2 * N_CHIPS), c)


def kernel(x, mem, g_mix_pre, w_in, g_vnorm, w_s, b_s, g_mem, w_mem_kv, w_gate, b_gate, w_br_sb, w_br_gm, w_br_xa, w_out, g_mix_post, g_ffn_pre, w_up, conv_w, conv_b, w_down, g_ffn_post, loss_target, m_g_mix_pre, m_w_in, m_g_vnorm, m_w_s, m_b_s, m_g_mem, m_w_mem_kv, m_w_gate, m_b_gate, m_w_br_sb, m_w_br_gm, m_w_br_xa, m_w_out, m_g_mix_post, m_g_ffn_pre, m_w_up, m_conv_w, m_conv_b, m_w_down, m_g_ffn_post, v_g_mix_pre, v_w_in, v_g_vnorm, v_w_s, v_b_s, v_g_mem, v_w_mem_kv, v_w_gate, v_b_gate, v_w_br_sb, v_w_br_gm, v_w_br_xa, v_w_out, v_g_mix_post, v_g_ffn_pre, v_w_up, v_conv_w, v_conv_b, v_w_down, v_g_ffn_post):
    w = dict(g_mix_pre=g_mix_pre, w_in=w_in, g_vnorm=g_vnorm, w_s=w_s, b_s=b_s, g_mem=g_mem, w_mem_kv=w_mem_kv,
             w_gate=w_gate, b_gate=b_gate, w_br_sb=w_br_sb, w_br_gm=w_br_gm, w_br_xa=w_br_xa, w_out=w_out,
             g_mix_post=g_mix_post, g_ffn_pre=g_ffn_pre, w_up=w_up, conv_w=conv_w, conv_b=conv_b, w_down=w_down,
             g_ffn_post=g_ffn_post)
    m = dict(g_mix_pre=m_g_mix_pre, w_in=m_w_in, g_vnorm=m_g_vnorm, w_s=m_w_s, b_s=m_b_s, g_mem=m_g_mem,
             w_mem_kv=m_w_mem_kv, w_gate=m_w_gate, b_gate=m_b_gate, w_br_sb=m_w_br_sb, w_br_gm=m_w_br_gm,
             w_br_xa=m_w_br_xa, w_out=m_w_out, g_mix_post=m_g_mix_post, g_ffn_pre=m_g_ffn_pre, w_up=m_w_up,
             conv_w=m_conv_w, conv_b=m_conv_b, w_down=m_w_down, g_ffn_post=m_g_ffn_post)
    v = dict(g_mix_pre=v_g_mix_pre, w_in=v_w_in, g_vnorm=v_g_vnorm, w_s=v_w_s, b_s=v_b_s, g_mem=v_g_mem,
             w_mem_kv=v_w_mem_kv, w_gate=v_w_gate, b_gate=v_b_gate, w_br_sb=v_w_br_sb, w_br_gm=v_w_br_gm,
             w_br_xa=v_w_br_xa, w_out=v_w_out, g_mix_post=v_g_mix_post, g_ffn_pre=v_g_ffn_pre, w_up=v_w_up,
             conv_w=v_conv_w, conv_b=v_conv_b, w_down=v_w_down, g_ffn_post=v_g_ffn_post)
    n_layers = w_in.shape[0]
    d = x.shape[-1]
    axes = [BIG_AXIS[n] for n in BIG]
    core = lax.axis_index("c").astype(jnp.int32).reshape(1)
    chip = (2 * lax.axis_index("x") + lax.axis_index("y")).astype(jnp.int32).reshape(1)
    small = {n: w[n] for n in SMALL}
    xs, mems, target = x[0], mem[0], loss_target[0]

    shard_cols = [w[n].shape[2] for n in BIG]
    conv_w_full = _gather_small(conv_w)

    def gather(l):
        views = [_place_own(w[n], l, chip, ax) for n, ax in zip(BIG, axes)]
        views = _gather_d2d(_gather_ici(views, axes, shard_cols), axes, shard_cols)
        full = {n: vw.reshape(-1, vw.shape[-1]) if ax == 1 else vw.reshape(vw.shape[0] * vw.shape[1], vw.shape[2])
                for n, vw, ax in zip(BIG, views, axes)}
        full['conv_w'] = conv_w_full[l]
        return full

    def scatter(gb, l, joined):
        dwvs = [_grad_view(gb[n], ax) for n, ax in zip(BIG, axes)]
        theirs = _grads_d2d(dwvs)
        sums = [_sum_halves(dv, th, core, "sum_halves") for dv, th in zip(dwvs, theirs)]
        recv = _grads_ici(sums, axes)
        return [_sum_chips(r, sm, None if joined is None else joined[t], l, n_layers, chip, core, ax)
                for t, (r, sm, ax) in enumerate(zip(recv, sums, axes))]

    fulls, saved = [], []
    for l in range(n_layers):
        fulls.append(gather(l))
        xs, sv = _layer_fwd(xs, mems, fulls[l], small, l)
        saved.append(sv)
    sq, dx = _loss_head(xs, target)
    loss = lax.psum(0.5 * jnp.sum(sq) / d, ("x", "y", "c"))

    joined = None
    gsmall = {n: [None] * n_layers for n in SMALL + ['conv_w']}
    for l in reversed(range(n_layers)):
        dx, gb, gs = _layer_bwd(dx, mems, saved[l], fulls[l], small, l)
        joined = scatter(gb, l, joined)
        for n in gs:
            gsmall[n][l] = gs[n]
    joined = _join_halves(joined)

    small_full = [jnp.stack(gsmall[n]).reshape(w[n].shape) for n in SMALL]
    conv_w_grad = jnp.stack(gsmall['conv_w'])
    summed = _all_reduce_small(_pack(small_full + [conv_w_grad], 8))
    *small_g, conv_w_g = _unpack(summed, small_full + [conv_w_grad])
    shard = conv_w.shape[-1]
    conv_w_g = lax.dynamic_slice_in_dim(conv_w_g, chip[0] * shard, shard, axis=2)

    out = {}
    for n, g in zip(BIG, joined):
        out[n] = _adamw(w[n], m[n], v[n], g.reshape(w[n].shape), "adamw_big")
    names = SMALL + ['conv_w']
    packed = [_pack([p[n] for n in names], 256) for p in (w, m, v)]
    gpack = _pack(small_g + [conv_w_g], 256)
    res = _adamw(packed[0][None], packed[1][None], packed[2][None], gpack[None], "adamw_small")
    like = [w[n] for n in names]
    unpacked = [_unpack(r[0], like) for r in res]
    for i, n in enumerate(names):
        out[n] = tuple(u[i] for u in unpacked)

    return (loss, dx[None], *[out[n][0] for n in WEIGHTS], *[out[n][1] for n in WEIGHTS],
            *[out[n][2] for n in WEIGHTS], *[out[n][3] for n in WEIGHTS])
```

```python
import functools
import math

import jax
import jax.numpy as jnp
from jax import lax
from jax.experimental import pallas as pl
from jax.experimental.pallas import tpu as pltpu

F32 = jnp.float32
BF16 = jnp.bfloat16
EPS = 1e-6
HEAD_SB = 128
GROUP_GM = 128
CHUNK = 64
HEAD_XA = 256
CONV_TAPS = 3
N_CHIPS = 4
N_DEV = 8
LANES = 128
MIB = 1024 * 1024
VMEM_LIMIT = 48 * MIB
SPLITS = 2

ADAM_LR = 0.001
ADAM_B1 = 0.9
ADAM_B2 = 0.999
ADAM_EPS = 1e-08
ADAM_WD = 0.01
ADAM_STEP = 10

WEIGHTS = ['g_mix_pre', 'w_in', 'g_vnorm', 'w_s', 'b_s', 'g_mem', 'w_mem_kv', 'w_gate', 'b_gate', 'w_br_sb',
           'w_br_gm', 'w_br_xa', 'w_out', 'g_mix_post', 'g_ffn_pre', 'w_up', 'conv_w', 'conv_b', 'w_down',
           'g_ffn_post']
BIG_AXIS = {'w_in': 2, 'w_mem_kv': 1, 'w_gate': 2, 'w_br_sb': 2, 'w_br_gm': 2, 'w_br_xa': 2, 'w_out': 1,
            'w_up': 2, 'w_down': 1}
BIG = list(BIG_AXIS)
SMALL = ['g_mix_pre', 'g_vnorm', 'w_s', 'b_s', 'g_mem', 'b_gate', 'g_mix_post', 'g_ffn_pre', 'conv_b', 'g_ffn_post']
MESH = pl.DeviceIdType.MESH


def _pcall(body, **kw):
    return pl.pallas_call(body, **kw)


def _params(sem=None, vmem=VMEM_LIMIT):
    return pltpu.CompilerParams(dimension_semantics=sem, vmem_limit_bytes=vmem)


def _tile(n, cands):
    for c in cands:
        if n % c == 0:
            return c
    return n


_GELU_C = math.sqrt(2.0 / math.pi)
_GELU_A = 0.044715


def _gelu(x):
    return 0.5 * x * (1.0 + jnp.tanh(_GELU_C * (x + _GELU_A * (x * x * x))))


def _gelu_and_grad(x):
    x2 = x * x
    t = jnp.tanh(_GELU_C * (x + _GELU_A * (x2 * x)))
    val = 0.5 * x * (1.0 + t)
    grad = 0.5 * (1.0 + t) + 0.5 * x * (1.0 - t * t) * (_GELU_C * (1.0 + 3.0 * _GELU_A * x2))
    return val, grad


def _softplus(z):
    return jnp.maximum(z, 0.0) + jnp.log1p(jnp.exp(-jnp.abs(z)))


def _dot(a, b):
    return jnp.dot(a, b, preferred_element_type=F32)


def _dot_nt(a, b):
    return lax.dot_general(a, b, (((1,), (1,)), ((), ())), preferred_element_type=F32)


def _dot_tn(a, b):
    return lax.dot_general(a, b, (((0,), (0,)), ((), ())), preferred_element_type=F32)


def _split_dot(a, m):
    out = None
    rest = a
    for _ in range(SPLITS):
        piece = rest.astype(BF16)
        rest = rest - piece.astype(F32)
        term = _dot(piece, m)
        out = term if out is None else out + term
    return out


def _mm(a, b, mode, out_dtype, name, tm=None, tn=None, tk=None):
    if mode == 'nn':
        (m, kc), (kc2, n) = a.shape, b.shape
    elif mode == 'nt':
        (m, kc), (n, kc2) = a.shape, b.shape
    else:
        (kc, m), (kc2, n) = a.shape, b.shape
    assert kc == kc2, (a.shape, b.shape, mode)
    tm = tm or _tile(m, (1024, 512, 256, 128))
    tn = tn or _tile(n, (1024, 512, 256, 128))
    tk = tk or (kc if kc <= 2048 else _tile(kc, (2048, 1536, 1408, 1024, 512)))
    nk = kc // tk
    dot = {'nn': _dot, 'nt': _dot_nt, 'tn': _dot_tn}[mode]
    a_spec = pl.BlockSpec((tk, tm), lambda i, j, k: (k, i)) if mode == 'tn' else pl.BlockSpec((tm, tk), lambda i, j, k: (i, k))
    b_spec = pl.BlockSpec((tn, tk), lambda i, j, k: (j, k)) if mode == 'nt' else pl.BlockSpec((tk, tn), lambda i, j, k: (k, j))

    if nk == 1:
        def body(a_ref, b_ref, o_ref):
            o_ref[...] = dot(a_ref[...].astype(BF16), b_ref[...].astype(BF16)).astype(o_ref.dtype)
        scratch = []
    else:
        def body(a_ref, b_ref, o_ref, acc_ref):
            k = pl.program_id(2)
            part = dot(a_ref[...].astype(BF16), b_ref[...].astype(BF16))

            @pl.when(k == 0)
            def _():
                acc_ref[...] = part

            @pl.when(k > 0)
            def _():
                acc_ref[...] += part

            @pl.when(k == nk - 1)
            def _():
                o_ref[...] = acc_ref[...].astype(o_ref.dtype)
        scratch = [pltpu.VMEM((tm, tn), F32)]

    return _pcall(
        body, grid=(m // tm, n // tn, nk), in_specs=[a_spec, b_spec],
        out_specs=pl.BlockSpec((tm, tn), lambda i, j, k: (i, j)),
        out_shape=jax.ShapeDtypeStruct((m, n), out_dtype), scratch_shapes=scratch, name=name,
        compiler_params=_params(("parallel", "parallel", "arbitrary")))(a, b)


def _norm_fwd(x, g, res, out_dtype, name, after=None):
    s, d = x.shape
    tr = _tile(s, (256, 128))
    has_res = res is not None
    has_after = after is not None

    def body(*refs):
        x_ref, g_ref = refs[0], refs[1]
        o_ref = refs[-1]
        xv = x_ref[...]
        y = xv * lax.rsqrt(jnp.mean(xv * xv, axis=-1, keepdims=True) + EPS) * g_ref[...]
        if has_res:
            y = y + refs[2][...]
        o_ref[...] = y.astype(o_ref.dtype)

    row = pl.BlockSpec((tr, d), lambda i: (i, 0))
    ins = [x, g] + ([res] if has_res else []) + ([after] if has_after else [])
    return _pcall(
        body, grid=(s // tr,),
        in_specs=[row, pl.BlockSpec((1, d), lambda i: (0, 0))] + ([row] if has_res else [])
        + ([pl.BlockSpec(memory_space=pl.ANY)] if has_after else []),
        out_specs=row, out_shape=jax.ShapeDtypeStruct((s, d), out_dtype), name=name,
        compiler_params=_params(("parallel",)))(*ins)


def _norm_bwd(x, g, douts, dres, out_dtype, name, after=None):
    s, d = x.shape
    tr = _tile(s, (256, 128))
    nd = len(douts)
    has_res = dres is not None
    has_after = after is not None

    def body(*refs):
        x_ref, g_ref = refs[0], refs[1]
        dx_ref, dg_ref = refs[-2], refs[-1]
        dout = refs[2][...].astype(F32)
        for r in refs[3:2 + nd]:
            dout = dout + r[...].astype(F32)
        xv = x_ref[...]
        r = lax.rsqrt(jnp.mean(xv * xv, axis=-1, keepdims=True) + EPS)
        n = xv * r
        dn = dout * g_ref[...]
        dx = r * (dn - n * jnp.mean(dn * n, axis=-1, keepdims=True))
        if has_res:
            dx = dx + refs[2 + nd][...]
        dx_ref[...] = dx.astype(dx_ref.dtype)

        @pl.when(pl.program_id(0) == 0)
        def _():
            dg_ref[...] = jnp.zeros_like(dg_ref)

        dg_ref[...] += jnp.sum(dout * n, axis=0, keepdims=True)

    row = pl.BlockSpec((tr, d), lambda i: (i, 0))
    vec = pl.BlockSpec((1, d), lambda i: (0, 0))
    ins = [x, g] + list(douts) + ([dres] if has_res else []) + ([after] if has_after else [])
    return _pcall(
        body, grid=(s // tr,),
        in_specs=[row, vec] + [row] * (nd + int(has_res)) + ([pl.BlockSpec(memory_space=pl.ANY)] if has_after else []),
        out_specs=[row, vec],
        out_shape=[jax.ShapeDtypeStruct((s, d), out_dtype), jax.ShapeDtypeStruct((1, d), F32)], name=name,
        compiler_params=_params(("arbitrary",)))(*ins)


def _loss_head(y, target):
    s, d = y.shape
    tr = _tile(s, (256, 128))

    def body(y_ref, t_ref, sq_ref, dy_ref):
        e = y_ref[...] - t_ref[...]
        dy_ref[...] = e * (1.0 / d)

        @pl.when(pl.program_id(0) == 0)
        def _():
            sq_ref[...] = jnp.zeros_like(sq_ref)

        sq_ref[...] += jnp.sum(e * e, axis=0, keepdims=True)

    row = pl.BlockSpec((tr, d), lambda i: (i, 0))
    return _pcall(
        body, grid=(s // tr,), in_specs=[row, row], out_specs=[pl.BlockSpec((1, d), lambda i: (0, 0)), row],
        out_shape=[jax.ShapeDtypeStruct((1, d), F32), jax.ShapeDtypeStruct((s, d), F32)], name="loss_head",
        compiler_params=_params(("arbitrary",)))(y, target)


SB_QUERIES = 512


def _sb_sum_matrix(later):
    r = lax.broadcasted_iota(jnp.int32, (HEAD_SB, 2 * HEAD_SB), 0)
    c = lax.broadcasted_iota(jnp.int32, (HEAD_SB, 2 * HEAD_SB), 1)
    tri = jnp.where((r > c) if later else (r < c), 1.0, 0.0)
    return jnp.where(c < HEAD_SB, tri, 1.0).astype(BF16)


def _sb_mask(tq, q0, k0):
    row = lax.broadcasted_iota(jnp.int32, (tq, HEAD_SB), 0)
    col = lax.broadcasted_iota(jnp.int32, (tq, HEAD_SB), 1)
    return (k0 + col) < (q0 + row)


def _sb_fwd(proj, n_heads):
    s = proj.shape[0]
    tq = min(SB_QUERIES, s)
    per = tq // HEAD_SB
    scale = HEAD_SB ** -0.5

    def body(q_ref, k_ref, v_ref, o_ref, acc_ref, c_ref):
        i = pl.program_id(1)
        q = q_ref[...].astype(BF16)
        sums = _sb_sum_matrix(True)
        acc_ref[...] = jnp.zeros_like(acc_ref)
        c_ref[...] = jnp.zeros_like(c_ref)
        last = (i + 1) * per - 1

        def step(jj, masked):
            off = pl.multiple_of((last - jj) * HEAD_SB, HEAD_SB)
            kb = k_ref[pl.ds(off, HEAD_SB), :].astype(BF16)
            vb = v_ref[pl.ds(off, HEAD_SB), :].astype(BF16)
            z = _dot_nt(q, kb) * scale
            sp = _softplus(z)
            if masked:
                mask = _sb_mask(tq, i * tq, off)
                sp = jnp.where(mask, sp, 0.0)
            both = _split_dot(sp, sums)
            c = c_ref[...]
            a = jnp.exp(z - sp - both[:, :HEAD_SB] - c)
            if masked:
                a = jnp.where(mask, a, 0.0)
            acc_ref[...] += _dot(a.astype(BF16), vb)
            c_ref[...] = c + both[:, HEAD_SB:]

        def diagonal(jj, carry):
            step(jj, True)
            return carry

        def below(jj, carry):
            step(jj, False)
            return carry

        lax.fori_loop(0, per, diagonal, 0)
        lax.fori_loop(per, last + 1, below, 0)
        o_ref[...] = acc_ref[...].astype(o_ref.dtype)

    h = n_heads
    blk = pl.BlockSpec((tq, HEAD_SB), lambda hh, i: (i, hh))
    return _pcall(
        body, grid=(h, s // tq),
        in_specs=[blk, pl.BlockSpec((s, HEAD_SB), lambda hh, i: (0, h + hh)),
                  pl.BlockSpec((s, HEAD_SB), lambda hh, i: (0, 2 * h + hh))],
        out_specs=blk, out_shape=jax.ShapeDtypeStruct((s, h * HEAD_SB), BF16),
        scratch_shapes=[pltpu.VMEM((tq, HEAD_SB), F32), pltpu.VMEM((tq, HEAD_SB), F32)],
        name="sb_fwd", compiler_params=_params(("parallel", "arbitrary")))(proj, proj, proj)


def _sb_bwd(proj, do, n_heads):
    s = proj.shape[0]
    tq = min(SB_QUERIES, s)
    per = tq // HEAD_SB
    scale = HEAD_SB ** -0.5

    def body(q_ref, k_ref, v_ref, do_ref, dq_ref, dk_ref, dv_ref, g_ref, beta_ref, run_ref, acc_ref):
        i = pl.program_id(1)

        @pl.when(i == 0)
        def _():
            dk_ref[...] = jnp.zeros_like(dk_ref)
            dv_ref[...] = jnp.zeros_like(dv_ref)

        q = q_ref[...].astype(BF16)
        dob = do_ref[...].astype(BF16)
        last = (i + 1) * per - 1

        run_ref[...] = jnp.zeros_like(run_ref)
        later = _sb_sum_matrix(True)

        def left(jj, masked):
            j = last - jj
            off = pl.multiple_of(j * HEAD_SB, HEAD_SB)
            kb = k_ref[pl.ds(off, HEAD_SB), :].astype(BF16)
            vb = v_ref[pl.ds(off, HEAD_SB), :].astype(BF16)
            z = _dot_nt(q, kb) * scale
            sp = _softplus(z)
            beta = jnp.exp(z - sp)
            if masked:
                mask = _sb_mask(tq, i * tq, off)
                sp = jnp.where(mask, sp, 0.0)
            both = _split_dot(sp, later)
            c = run_ref[...]
            a = beta * jnp.exp(-(both[:, :HEAD_SB] + c))
            if masked:
                a = jnp.where(mask, a, 0.0)
            g_ref[j] = a * _dot_nt(dob, vb)
            beta_ref[j] = beta
            dv_ref[pl.ds(off, HEAD_SB), :] += _dot_tn(a.astype(BF16), dob)
            run_ref[...] = c + both[:, HEAD_SB:]

        def left_diagonal(jj, carry):
            left(jj, True)
            return carry

        def left_below(jj, carry):
            left(jj, False)
            return carry

        lax.fori_loop(0, per, left_diagonal, 0)
        lax.fori_loop(per, last + 1, left_below, 0)

        run_ref[...] = jnp.zeros_like(run_ref)
        acc_ref[...] = jnp.zeros_like(acc_ref)
        earlier = _sb_sum_matrix(False)

        def right(j, masked):
            off = pl.multiple_of(j * HEAD_SB, HEAD_SB)
            kb = k_ref[pl.ds(off, HEAD_SB), :].astype(BF16)
            g = g_ref[j]
            beta = beta_ref[j]
            both = _split_dot(g, earlier)
            p = run_ref[...]
            dz = (g * (1.0 - beta) - beta * (both[:, :HEAD_SB] + p)) * scale
            if masked:
                dz = jnp.where(_sb_mask(tq, i * tq, off), dz, 0.0)
            dzb = dz.astype(BF16)
            dk_ref[pl.ds(off, HEAD_SB), :] += _dot_tn(dzb, q)
            acc_ref[...] += _dot(dzb, kb)
            run_ref[...] = p + both[:, HEAD_SB:]

        def right_below(j, carry):
            right(j, False)
            return carry

        def right_diagonal(j, carry):
            right(j, True)
            return carry

        lax.fori_loop(0, last + 1 - per, right_below, 0)
        lax.fori_loop(last + 1 - per, last + 1, right_diagonal, 0)
        dq_ref[...] = acc_ref[...]

    h = n_heads
    blk = pl.BlockSpec((tq, HEAD_SB), lambda hh, i: (i, hh))
    col_blk = pl.BlockSpec((s, HEAD_SB), lambda hh, i: (0, hh))
    shape = jax.ShapeDtypeStruct((s, h * HEAD_SB), F32)
    nk = s // HEAD_SB
    return _pcall(
        body, grid=(h, s // tq),
        in_specs=[blk, pl.BlockSpec((s, HEAD_SB), lambda hh, i: (0, h + hh)),
                  pl.BlockSpec((s, HEAD_SB), lambda hh, i: (0, 2 * h + hh)), blk],
        out_specs=[blk, col_blk, col_blk], out_shape=[shape, shape, shape],
        scratch_shapes=[pltpu.VMEM((nk, tq, HEAD_SB), F32), pltpu.VMEM((nk, tq, HEAD_SB), F32),
                        pltpu.VMEM((tq, HEAD_SB), F32), pltpu.VMEM((tq, HEAD_SB), F32)],
        name="sb_bwd", compiler_params=_params(("parallel", "arbitrary")))(proj, proj, proj, do)


def _gm_mask():
    t = lax.broadcasted_iota(jnp.int32, (GROUP_GM, GROUP_GM), 0)
    s = lax.broadcasted_iota(jnp.int32, (GROUP_GM, GROUP_GM), 1)
    shift = CHUNK.bit_length() - 1
    return (s >> shift) <= (t >> shift)


def _gm_fwd(proj, g_vnorm, w_s, b_st, u_blk):
    s = proj.shape[0]
    groups = w_s.shape[0]
    w = groups * GROUP_GM

    def body(u_ref, v_ref, gv_ref, ws_ref, bst_ref, o_ref):
        ug = _gelu(u_ref[...])
        vg = _gelu(v_ref[...])
        vn = vg * lax.rsqrt(jnp.mean(vg * vg, axis=-1, keepdims=True) + EPS) * gv_ref[...]
        vnb = vn.astype(BF16)
        mask = _gm_mask()
        for g in range(groups):
            sl = slice(g * GROUP_GM, (g + 1) * GROUP_GM)
            wm = jnp.where(mask, ws_ref[g], 0.0).astype(BF16)
            mixed = _dot(wm, vnb[:, sl]) + bst_ref[:, g:g + 1]
            o_ref[:, sl] = (ug[:, sl] * mixed).astype(o_ref.dtype)

    return _pcall(
        body, grid=(s // GROUP_GM,),
        in_specs=[pl.BlockSpec((GROUP_GM, w), lambda c: (c, u_blk)), pl.BlockSpec((GROUP_GM, w), lambda c: (c, u_blk + 1)),
                  pl.BlockSpec((1, w), lambda c: (0, 0)), pl.BlockSpec((groups, GROUP_GM, GROUP_GM), lambda c: (0, 0, 0)),
                  pl.BlockSpec((GROUP_GM, groups), lambda c: (0, 0))],
        out_specs=pl.BlockSpec((GROUP_GM, w), lambda c: (c, 0)),
        out_shape=jax.ShapeDtypeStruct((s, w), BF16), name="gm_fwd",
        compiler_params=_params(("parallel",)))(proj, proj, g_vnorm, w_s, b_st)


def _gm_bwd(proj, g_vnorm, w_s, b_st, do, u_blk):
    s = proj.shape[0]
    groups = w_s.shape[0]
    w = groups * GROUP_GM

    def body(u_ref, v_ref, gv_ref, ws_ref, bst_ref, do_ref, du_ref, dv_ref, dgv_ref, dws_ref, dbst_ref, dvn_ref):
        @pl.when(pl.program_id(0) == 0)
        def _():
            dgv_ref[...] = jnp.zeros_like(dgv_ref)
            dws_ref[...] = jnp.zeros_like(dws_ref)
            dbst_ref[...] = jnp.zeros_like(dbst_ref)

        ug, ugrad = _gelu_and_grad(u_ref[...])
        vg, vgrad = _gelu_and_grad(v_ref[...])
        r = lax.rsqrt(jnp.mean(vg * vg, axis=-1, keepdims=True) + EPS)
        n = vg * r
        gv = gv_ref[...]
        vnb = (n * gv).astype(BF16)
        dout = do_ref[...]
        mask = _gm_mask()
        for g in range(groups):
            sl = slice(g * GROUP_GM, (g + 1) * GROUP_GM)
            wm = jnp.where(mask, ws_ref[g], 0.0).astype(BF16)
            mixed = _dot(wm, vnb[:, sl]) + bst_ref[:, g:g + 1]
            dmixed = dout[:, sl] * ug[:, sl]
            du_ref[:, sl] = dout[:, sl] * mixed * ugrad[:, sl]
            dbst_ref[:, g:g + 1] += jnp.sum(dmixed, axis=1, keepdims=True)
            dmb = dmixed.astype(BF16)
            dws_ref[g] += jnp.where(mask, _dot_nt(dmb, vnb[:, sl]), 0.0)
            dvn_ref[:, sl] = _dot_tn(wm, dmb)
        dvn = dvn_ref[...]
        dgv_ref[...] += jnp.sum(dvn * n, axis=0, keepdims=True)
        dn = dvn * gv
        dvg = r * (dn - n * jnp.mean(dn * n, axis=-1, keepdims=True))
        dv_ref[...] = dvg * vgrad

    rowb = pl.BlockSpec((GROUP_GM, w), lambda c: (c, 0))
    vec = pl.BlockSpec((1, w), lambda c: (0, 0))
    wsb = pl.BlockSpec((groups, GROUP_GM, GROUP_GM), lambda c: (0, 0, 0))
    bsb = pl.BlockSpec((GROUP_GM, groups), lambda c: (0, 0))
    return _pcall(
        body, grid=(s // GROUP_GM,),
        in_specs=[pl.BlockSpec((GROUP_GM, w), lambda c: (c, u_blk)), pl.BlockSpec((GROUP_GM, w), lambda c: (c, u_blk + 1)),
                  vec, wsb, bsb, rowb],
        out_specs=[rowb, rowb, vec, wsb, bsb],
        out_shape=[jax.ShapeDtypeStruct((s, w), F32), jax.ShapeDtypeStruct((s, w), F32), jax.ShapeDtypeStruct((1, w), F32),
                   jax.ShapeDtypeStruct((groups, GROUP_GM, GROUP_GM), F32), jax.ShapeDtypeStruct((GROUP_GM, groups), F32)],
        scratch_shapes=[pltpu.VMEM((GROUP_GM, w), F32)], name="gm_bwd",
        compiler_params=_params(("arbitrary",)))(proj, proj, g_vnorm, w_s, b_st, do)


def _xa_fwd(proj, mem_kv, q_blk, n_heads):
    s = proj.shape[0]
    nm = mem_kv.shape[0]
    tq = _tile(s, (512, 256, 128))
    scale = HEAD_XA ** -0.5

    def body(q_ref, k_ref, v_ref, o_ref):
        z = _dot_nt(q_ref[...].astype(BF16), k_ref[...].astype(BF16)) * scale
        z = z - jnp.max(z, axis=-1, keepdims=True)
        e = jnp.exp(z)
        p = e / jnp.sum(e, axis=-1, keepdims=True)
        o_ref[...] = _dot(p.astype(BF16), v_ref[...].astype(BF16)).astype(o_ref.dtype)

    h = n_heads
    return _pcall(
        body, grid=(h, s // tq),
        in_specs=[pl.BlockSpec((tq, HEAD_XA), lambda hh, i: (i, q_blk + hh)),
                  pl.BlockSpec((nm, HEAD_XA), lambda hh, i: (0, hh)), pl.BlockSpec((nm, HEAD_XA), lambda hh, i: (0, h + hh))],
        out_specs=pl.BlockSpec((tq, HEAD_XA), lambda hh, i: (i, hh)),
        out_shape=jax.ShapeDtypeStruct((s, h * HEAD_XA), BF16), name="xa_fwd",
        compiler_params=_params(("parallel", "parallel")))(proj, mem_kv, mem_kv)


def _xa_bwd(proj, mem_kv, do, q_blk, n_heads):
    s = proj.shape[0]
    nm = mem_kv.shape[0]
    tq = _tile(s, (512, 256, 128))
    scale = HEAD_XA ** -0.5
    h = n_heads

    def body(q_ref, k_ref, v_ref, do_ref, dq_ref, dk_ref, dv_ref):
        @pl.when(pl.program_id(1) == 0)
        def _():
            dk_ref[...] = jnp.zeros_like(dk_ref)
            dv_ref[...] = jnp.zeros_like(dv_ref)

        qb = q_ref[...].astype(BF16)
        kb = k_ref[...].astype(BF16)
        vb = v_ref[...].astype(BF16)
        dob = do_ref[...].astype(BF16)
        z = _dot_nt(qb, kb) * scale
        z = z - jnp.max(z, axis=-1, keepdims=True)
        e = jnp.exp(z)
        p = e / jnp.sum(e, axis=-1, keepdims=True)
        dp = _dot_nt(dob, vb)
        dz = (p * (dp - jnp.sum(dp * p, axis=-1, keepdims=True)) * scale).astype(BF16)
        dq_ref[...] = _dot(dz, kb)
        dk_ref[...] += _dot_tn(dz, qb)
        dv_ref[...] += _dot_tn(p.astype(BF16), dob)

    qspec = pl.BlockSpec((tq, HEAD_XA), lambda hh, i: (i, hh))
    dk, dv = None, None
    dq, dk, dv = _pcall(
        body, grid=(h, s // tq),
        in_specs=[pl.BlockSpec((tq, HEAD_XA), lambda hh, i: (i, q_blk + hh)),
                  pl.BlockSpec((nm, HEAD_XA), lambda hh, i: (0, hh)), pl.BlockSpec((nm, HEAD_XA), lambda hh, i: (0, h + hh)),
                  qspec],
        out_specs=[qspec, pl.BlockSpec((nm, HEAD_XA), lambda hh, i: (0, hh)), pl.BlockSpec((nm, HEAD_XA), lambda hh, i: (0, hh))],
        out_shape=[jax.ShapeDtypeStruct((s, h * HEAD_XA), F32), jax.ShapeDtypeStruct((nm, h * HEAD_XA), F32),
                   jax.ShapeDtypeStruct((nm, h * HEAD_XA), F32)],
        name="xa_bwd", compiler_params=_params(("parallel", "arbitrary")))(proj, mem_kv, mem_kv, do)
    return dq, dk, dv


def _merge_fwd(zg, b_gate, branches):
    s, d = branches[0].shape
    tr = _tile(s, (128,))

    def body(z0, z1, z2, g0, g1, g2, b0, b1, b2, o_ref):
        acc = None
        for z, g, b in ((z0, g0, b0), (z1, g1, b1), (z2, g2, b2)):
            term = jax.nn.sigmoid(z[...] + g[...]) * b[...]
            acc = term if acc is None else acc + term
        o_ref[...] = acc.astype(o_ref.dtype)

    zs = [pl.BlockSpec((tr, d), functools.partial(lambda i, k: (i, k), k=k)) for k in range(3)]
    gs = [pl.BlockSpec((1, d), functools.partial(lambda i, k: (0, k), k=k)) for k in range(3)]
    row = pl.BlockSpec((tr, d), lambda i: (i, 0))
    return _pcall(
        body, grid=(s // tr,), in_specs=zs + gs + [row] * 3, out_specs=row,
        out_shape=jax.ShapeDtypeStruct((s, d), BF16), name="merge_fwd",
        compiler_params=_params(("parallel",)))(zg, zg, zg, b_gate, b_gate, b_gate, *branches)


def _merge_bwd(zg, b_gate, branches, dmerged):
    s, d = branches[0].shape
    tr = _tile(s, (128,))

    def body(z0, z1, z2, g0, g1, g2, b0, b1, b2, dm_ref, dz_ref, d0, d1, d2, dbg_ref):
        @pl.when(pl.program_id(0) == 0)
        def _():
            dbg_ref[...] = jnp.zeros_like(dbg_ref)

        dm = dm_ref[...]
        for k, (z, g, b, dbr) in enumerate(((z0, g0, b0, d0), (z1, g1, b1, d1), (z2, g2, b2, d2))):
            sg = jax.nn.sigmoid(z[...] + g[...])
            dbr[...] = (dm * sg).astype(dbr.dtype)
            dz = dm * b[...] * sg * (1.0 - sg)
            dz_ref[:, k * d:(k + 1) * d] = dz.astype(dz_ref.dtype)
            dbg_ref[:, k * d:(k + 1) * d] += jnp.sum(dz, axis=0, keepdims=True)

    zs = [pl.BlockSpec((tr, d), functools.partial(lambda i, k: (i, k), k=k)) for k in range(3)]
    gs = [pl.BlockSpec((1, d), functools.partial(lambda i, k: (0, k), k=k)) for k in range(3)]
    row = pl.BlockSpec((tr, d), lambda i: (i, 0))
    outs = _pcall(
        body, grid=(s // tr,), in_specs=zs + gs + [row] * 4,
        out_specs=[pl.BlockSpec((tr, 3 * d), lambda i: (i, 0)), row, row, row, pl.BlockSpec((1, 3 * d), lambda i: (0, 0))],
        out_shape=[jax.ShapeDtypeStruct((s, 3 * d), BF16)] + [jax.ShapeDtypeStruct((s, d), BF16)] * 3
        + [jax.ShapeDtypeStruct((1, 3 * d), F32)],
        name="merge_bwd", compiler_params=_params(("arbitrary",)))(zg, zg, zg, b_gate, b_gate, b_gate, *branches, dmerged)
    return outs[0], list(outs[1:4]), outs[4]


def _shift_down(x, k, row):
    return jnp.where(row >= k, pltpu.roll(x, k, 0), 0.0)


def _shift_up(x, k, row, s):
    return jnp.where(row < s - k, pltpu.roll(x, s - k, 0), 0.0)


def _conv_pre(gate, cw_ref, cb_ref, row):
    conv = cb_ref[...] + cw_ref[CONV_TAPS - 1:CONV_TAPS, :] * gate
    for k in range(1, CONV_TAPS):
        conv = conv + cw_ref[CONV_TAPS - 1 - k:CONV_TAPS - k, :] * _shift_down(gate, k, row)
    return conv


def _cg_fwd(up, conv_w, conv_b):
    s = up.shape[0]
    f = conv_w.shape[1]
    tc = _tile(f, (256, 128))
    nb = f // tc

    def body(g_ref, v_ref, cw_ref, cb_ref, o_ref):
        row = lax.broadcasted_iota(jnp.int32, (s, tc), 0)
        conv = _conv_pre(g_ref[...], cw_ref, cb_ref, row)
        o_ref[...] = (_gelu(conv) * v_ref[...]).astype(o_ref.dtype)

    return _pcall(
        body, grid=(nb,),
        in_specs=[pl.BlockSpec((s, tc), lambda j: (0, j)), pl.BlockSpec((s, tc), lambda j: (0, nb + j)),
                  pl.BlockSpec((CONV_TAPS, tc), lambda j: (0, j)), pl.BlockSpec((1, tc), lambda j: (0, j))],
        out_specs=pl.BlockSpec((s, tc), lambda j: (0, j)),
        out_shape=jax.ShapeDtypeStruct((s, f), BF16), name="cg_fwd",
        compiler_params=_params(("parallel",)))(up, up, conv_w, conv_b)


def _cg_bwd(up, conv_w, conv_b, dact):
    s = up.shape[0]
    f = conv_w.shape[1]
    tc = _tile(f, (256, 128))
    nb = f // tc

    def body(g_ref, v_ref, cw_ref, cb_ref, da_ref, dg_ref, dv_ref, dcw_ref, dcb_ref):
        row = lax.broadcasted_iota(jnp.int32, (s, tc), 0)
        gate = g_ref[...]
        conv = _conv_pre(gate, cw_ref, cb_ref, row)
        gel, ggrad = _gelu_and_grad(conv)
        da = da_ref[...]
        dv_ref[...] = (da * gel).astype(dv_ref.dtype)
        dconv = da * v_ref[...] * ggrad
        dgate = cw_ref[CONV_TAPS - 1:CONV_TAPS, :] * dconv
        dcw_ref[CONV_TAPS - 1:CONV_TAPS, :] = jnp.sum(dconv * gate, axis=0, keepdims=True)
        for k in range(1, CONV_TAPS):
            dgate = dgate + cw_ref[CONV_TAPS - 1 - k:CONV_TAPS - k, :] * _shift_up(dconv, k, row, s)
            dcw_ref[CONV_TAPS - 1 - k:CONV_TAPS - k, :] = jnp.sum(dconv * _shift_down(gate, k, row), axis=0, keepdims=True)
        dg_ref[...] = dgate.astype(dg_ref.dtype)
        dcb_ref[...] = jnp.sum(dconv, axis=0, keepdims=True)

    colb = pl.BlockSpec((s, tc), lambda j: (0, j))
    return _pcall(
        body, grid=(nb,),
        in_specs=[colb, pl.BlockSpec((s, tc), lambda j: (0, nb + j)), pl.BlockSpec((CONV_TAPS, tc), lambda j: (0, j)),
                  pl.BlockSpec((1, tc), lambda j: (0, j)), colb],
        out_specs=[colb, colb, pl.BlockSpec((CONV_TAPS, tc), lambda j: (0, j)), pl.BlockSpec((1, tc), lambda j: (0, j))],
        out_shape=[jax.ShapeDtypeStruct((s, f), BF16), jax.ShapeDtypeStruct((s, f), BF16),
                   jax.ShapeDtypeStruct((CONV_TAPS, f), F32), jax.ShapeDtypeStruct((1, f), F32)],
        name="cg_bwd", compiler_params=_params(("parallel",)))(up, up, conv_w, conv_b, dact)


def _row_tile(rows, cols):
    want = max(16, (256 * 1024) // cols)
    for c in (512, 256, 128, 64, 32, 16):
        if c <= want and rows % c == 0:
            return c
    return rows


def _sum_halves(dwv, recv, core, name):
    nj, _, a, c = dwv.shape
    tr = _row_tile(a, c)

    def body(core_ref, d_ref, r_ref, o_ref):
        o_ref[0] = (d_ref[0, 0].astype(F32) + r_ref[0].astype(F32)).astype(o_ref.dtype)

    grid_spec = pltpu.PrefetchScalarGridSpec(
        num_scalar_prefetch=1, grid=(nj, a // tr),
        in_specs=[pl.BlockSpec((1, 1, tr, c), lambda j, i, cr: (j, cr[0], i, 0)),
                  pl.BlockSpec((1, tr, c), lambda j, i, cr: (j, i, 0))],
        out_specs=pl.BlockSpec((1, tr, c), lambda j, i, cr: (j, i, 0)))
    return _pcall(body, grid_spec=grid_spec, out_shape=jax.ShapeDtypeStruct((nj, a, c), BF16), name=name,
                  compiler_params=_params(("parallel", "parallel")))(core, dwv, recv)


def _sum_chips(recv, own, joined, layer, n_layers, chip, core, ax):
    _, a, b = recv.shape
    tr = _row_tile(a, b)

    def body(chip_ref, core_ref, r_ref, own_ref, *rest):
        o_ref = rest[-1]
        me = chip_ref[0]
        mine = own_ref[0].astype(F32)
        acc = None
        for k in range(N_CHIPS):
            term = jnp.where(me == k, mine, r_ref[k].astype(F32))
            acc = term if acc is None else acc + term
        o_ref[0, 0] = acc

    own_spec = (pl.BlockSpec((1, tr, b), lambda i, ch, co: (0, i, ch[0])) if ax == 2
                else pl.BlockSpec((1, tr, b), lambda i, ch, co: (ch[0], i, 0)))
    in_specs = [pl.BlockSpec((N_CHIPS, tr, b), lambda i, ch, co: (0, i, 0)), own_spec]
    args = [chip, core, recv, own]
    aliases = {}
    if joined is not None:
        in_specs.append(pl.BlockSpec(memory_space=pl.ANY))
        args.append(joined)
        aliases = {4: 0}
    grid_spec = pltpu.PrefetchScalarGridSpec(
        num_scalar_prefetch=2, grid=(a // tr,), in_specs=in_specs,
        out_specs=pl.BlockSpec((1, 1, tr, b), lambda i, ch, co: (layer, co[0], i, 0)))
    return _pcall(body, grid_spec=grid_spec, out_shape=jax.ShapeDtypeStruct((n_layers, 2, a, b), F32),
                  input_output_aliases=aliases, name="sum_chips", compiler_params=_params(("parallel",)))(*args)


def _place_own(wt, layer, chip, ax):
    nl, r, c = wt.shape
    half = r // 2
    tr = _row_tile(half, c)
    nb = half // tr

    def body(chip_ref, w_ref, o_ref):
        o_ref[...] = w_ref[...].astype(BF16).reshape(o_ref.shape)

    if ax == 2:
        out_spec = pl.BlockSpec((1, tr, c), lambda h, i, ch: (h, i, ch[0]))
    else:
        out_spec = pl.BlockSpec((1, 1, tr, c), lambda h, i, ch: (ch[0], h, i, 0))
    grid_spec = pltpu.PrefetchScalarGridSpec(
        num_scalar_prefetch=1, grid=(2, nb),
        in_specs=[pl.BlockSpec((1, tr, c), lambda h, i, ch: (layer, h * nb + i, 0))], out_specs=out_spec)
    return _pcall(body, grid_spec=grid_spec, out_shape=jax.ShapeDtypeStruct(_full_view_shape(wt.shape, ax), BF16),
                  name="place_own", compiler_params=_params(("parallel", "parallel")))(chip, wt)


def _adamw(w, m, v, g, name):
    nl, r, c = w.shape
    tr = _row_tile(r, c)
    c1 = 1.0 - ADAM_B1 ** ADAM_STEP
    c2 = 1.0 - ADAM_B2 ** ADAM_STEP

    def body(w_ref, m_ref, v_ref, gin_ref, g_ref, d_ref, nm_ref, nv_ref):
        g = gin_ref[...]
        mm = ADAM_B1 * m_ref[...] + (1.0 - ADAM_B1) * g
        vv = ADAM_B2 * v_ref[...] + (1.0 - ADAM_B2) * (g * g)
        g_ref[...] = g
        nm_ref[...] = mm
        nv_ref[...] = vv
        d_ref[...] = -ADAM_LR * ((mm / c1) / (jnp.sqrt(vv / c2) + ADAM_EPS) + ADAM_WD * w_ref[...])

    blk = pl.BlockSpec((1, tr, c), lambda l, i: (l, i, 0))
    shape = jax.ShapeDtypeStruct((nl, r, c), F32)
    return _pcall(
        body, grid=(nl, r // tr), in_specs=[blk] * 4, out_specs=[blk] * 4, out_shape=[shape] * 4, name=name,
        compiler_params=_params(("parallel", "parallel")))(w, m, v, g)


HBM_SPEC = pl.BlockSpec(memory_space=pltpu.HBM)
COMM = pltpu.CompilerParams(has_side_effects=True)


def _position():
    x, y, c = lax.axis_index("x"), lax.axis_index("y"), lax.axis_index("c")
    chips = [(1 - x, y), (x, 1 - y), (1 - x, 1 - y)]
    return x, y, c, chips


def _remote(src, dst, send_sem, recv_sem, dev):
    return pltpu.make_async_remote_copy(src_ref=src, dst_ref=dst, send_sem=send_sem, recv_sem=recv_sem,
                                        device_id=dev, device_id_type=MESH)


def _full_view_shape(shard_shape, ax):
    _, r, c = shard_shape
    return (2, r // 2, c * N_CHIPS) if ax == 2 else (N_CHIPS, 2, r // 2, c)


def _piece(ref, ax, j, h, cs):
    if ax == 2:
        return ref.at[h, :, pl.ds(pl.multiple_of(j * cs, cs), cs)]
    return ref.at[j, h]


def _chip_block(ref, ax, j, cs):
    if ax == 2:
        return ref.at[:, :, pl.ds(pl.multiple_of(j * cs, cs), cs)]
    return ref.at[j]


SEM_SPEC = pl.BlockSpec(memory_space=pltpu.SEMAPHORE)
ANY_SPEC = pl.BlockSpec(memory_space=pl.ANY)
SPLIT = pltpu.CompilerParams(has_side_effects=pltpu.SideEffectType.DATAFLOW_SIDE_EFFECTING)


def _gather_copies(bufs, axes, shard_cols, send_sems, recv_sems):
    x, y, c, chips = _position()
    me = 2 * x + y
    mine, theirs = [], []
    for t, ax in enumerate(axes):
        own = _piece(bufs[t], ax, me, c, shard_cols[t])
        for p, (px, py) in enumerate(chips):
            k = t * 3 + p
            got = _piece(bufs[t], ax, 2 * px + py, c, shard_cols[t])
            mine.append(_remote(own, own, send_sems.at[k], recv_sems.at[k], (px, py, c)))
            theirs.append(_remote(got, got, send_sems.at[k], recv_sems.at[k], (px, py, c)))
    return mine, theirs


def _gather_ici(views, axes, shard_cols):
    n = len(views)

    def body(*refs):
        mine, theirs = _gather_copies(refs[n:2 * n], axes, shard_cols, *refs[2 * n:])
        for cp in mine:
            cp.start()
        for cp in theirs:
            cp.wait_recv()
        for cp in mine:
            cp.wait_send()

    return _pcall(
        body, in_specs=[HBM_SPEC] * n, out_specs=[HBM_SPEC] * n,
        out_shape=[jax.ShapeDtypeStruct(v.shape, v.dtype) for v in views],
        input_output_aliases={t: t for t in range(n)},
        scratch_shapes=[pltpu.SemaphoreType.DMA((3 * n,)), pltpu.SemaphoreType.DMA((3 * n,))],
        name="gather_ici", compiler_params=COMM)(*views)


def _gather_ici_start(views, axes, shard_cols, after):
    n = len(views)

    def body(*refs):
        send_sems, recv_sems = refs[n + 1], refs[n + 2]
        mine, _ = _gather_copies(refs[n + 3:2 * n + 3], axes, shard_cols, send_sems, recv_sems)
        for cp in mine:
            cp.start()
        refs[2 * n + 3][...] = jnp.zeros_like(refs[2 * n + 3])

    outs = _pcall(
        body, in_specs=[HBM_SPEC] * n + [ANY_SPEC],
        out_specs=[SEM_SPEC, SEM_SPEC] + [HBM_SPEC] * n + [pl.BlockSpec(memory_space=pltpu.VMEM)],
        out_shape=[pltpu.SemaphoreType.DMA((3 * n,)), pltpu.SemaphoreType.DMA((3 * n,))]
        + [pltpu.HBM(v.shape, v.dtype) for v in views] + [jax.ShapeDtypeStruct((8, LANES), F32)],
        input_output_aliases={t: 2 + t for t in range(n)},
        name="gather_ici_start", compiler_params=SPLIT)(*[pltpu.with_memory_space_constraint(v, pltpu.HBM) for v in views], after)
    return outs[0], outs[1], list(outs[2:2 + n]), outs[2 + n]


def _gather_ici_wait(views, send_sems, recv_sems, axes, shard_cols, after):
    n = len(views)

    def body(*refs):
        mine, theirs = _gather_copies(refs[:n], axes, shard_cols, refs[n], refs[n + 1])
        for cp in mine:
            cp.wait_send()
        for cp in theirs:
            cp.wait_recv()

    return _pcall(
        body, in_specs=[HBM_SPEC] * n + [SEM_SPEC, SEM_SPEC, ANY_SPEC], out_specs=[HBM_SPEC] * n,
        out_shape=[pltpu.HBM(v.shape, v.dtype) for v in views],
        input_output_aliases={t: t for t in range(n)},
        name="gather_ici_wait", compiler_params=SPLIT)(*views, send_sems, recv_sems, after)


def _gather_d2d(views, axes, shard_cols):
    n = len(views)

    def body(*refs):
        outs = refs[n:2 * n]
        send_sems, recv_sems = refs[2 * n:]
        x, y, c, chips = _position()
        remote = []
        for t, ax in enumerate(axes):
            for p, (px, py) in enumerate(chips):
                k = t * 3 + p
                blk = _piece(outs[t], ax, 2 * px + py, c, shard_cols[t])
                remote.append(_remote(blk, blk, send_sems.at[k], recv_sems.at[k], (x, y, 1 - c)))
                remote[-1].start()
        for t, ax in enumerate(axes):
            for p, (px, py) in enumerate(chips):
                k = t * 3 + p
                blk = _piece(outs[t], ax, 2 * px + py, 1 - c, shard_cols[t])
                _remote(blk, blk, send_sems.at[k], recv_sems.at[k], (x, y, 1 - c)).wait_recv()
        for cp in remote:
            cp.wait_send()

    return _pcall(
        body, in_specs=[HBM_SPEC] * n, out_specs=[HBM_SPEC] * n,
        out_shape=[jax.ShapeDtypeStruct(v.shape, v.dtype) for v in views],
        input_output_aliases={t: t for t in range(n)},
        scratch_shapes=[pltpu.SemaphoreType.DMA((3 * n,)), pltpu.SemaphoreType.DMA((3 * n,))],
        name="gather_d2d", compiler_params=COMM)(*views)


def _grads_d2d(dwvs):
    n = len(dwvs)

    def body(*refs):
        ins, outs = refs[:n], refs[n:2 * n]
        send_sems, recv_sems = refs[2 * n:]
        x, y, c, _ = _position()
        remote = [_remote(ins[t].at[:, 1 - c], outs[t], send_sems.at[t], recv_sems.at[t], (x, y, 1 - c)) for t in range(n)]
        for cp in remote:
            cp.start()
        for cp in remote:
            cp.wait()

    return _pcall(
        body, in_specs=[HBM_SPEC] * n, out_specs=[HBM_SPEC] * n,
        out_shape=[jax.ShapeDtypeStruct((d.shape[0],) + d.shape[2:], d.dtype) for d in dwvs],
        scratch_shapes=[pltpu.SemaphoreType.DMA((n,)), pltpu.SemaphoreType.DMA((n,))],
        name="grads_d2d", compiler_params=COMM)(*dwvs)


def _grads_recv_shapes(sums, axes):
    out = []
    for sm, ax in zip(sums, axes):
        _, a, c = sm.shape
        out.append(((N_CHIPS, a, c // N_CHIPS if ax == 2 else c), sm.dtype))
    return out


def _grads_copies(ins, outs, axes, send_sems, recv_sems):
    x, y, c, chips = _position()
    me = 2 * x + y

    def block(t, j):
        if axes[t] == 2:
            cs = outs[t].shape[2]
            return ins[t].at[0, :, pl.ds(pl.multiple_of(j * cs, cs), cs)]
        return ins[t].at[j]

    mine, theirs = [], []
    for t in range(len(ins)):
        for p, (px, py) in enumerate(chips):
            k = t * 3 + p
            peer = 2 * px + py
            mine.append(_remote(block(t, peer), outs[t].at[me], send_sems.at[k], recv_sems.at[k], (px, py, c)))
            theirs.append(_remote(block(t, peer), outs[t].at[peer], send_sems.at[k], recv_sems.at[k], (px, py, c)))
    return mine, theirs


def _grads_ici(sums, axes):
    n = len(sums)

    def body(*refs):
        mine, theirs = _grads_copies(refs[:n], refs[n:2 * n], axes, *refs[2 * n:])
        for cp in mine:
            cp.start()
        for cp in theirs:
            cp.wait_recv()
        for cp in mine:
            cp.wait_send()

    return _pcall(
        body, in_specs=[HBM_SPEC] * n, out_specs=[HBM_SPEC] * n,
        out_shape=[jax.ShapeDtypeStruct(sh, dt) for sh, dt in _grads_recv_shapes(sums, axes)],
        scratch_shapes=[pltpu.SemaphoreType.DMA((3 * n,)), pltpu.SemaphoreType.DMA((3 * n,))],
        name="grads_ici", compiler_params=COMM)(*sums)


def _grads_ici_start(sums, axes):
    n = len(sums)
    shapes = _grads_recv_shapes(sums, axes)

    def body(*refs):
        send_sems, recv_sems = refs[2 * n], refs[2 * n + 1]
        mine, _ = _grads_copies(refs[2 * n + 2:3 * n + 2], refs[3 * n + 2:4 * n + 2], axes, send_sems, recv_sems)
        for cp in mine:
            cp.start()
        refs[4 * n + 2][...] = jnp.zeros_like(refs[4 * n + 2])

    lands = [pltpu.with_memory_space_constraint(lax.empty(sh, dt), pltpu.HBM) for sh, dt in shapes]
    outs = _pcall(
        body, in_specs=[HBM_SPEC] * (2 * n),
        out_specs=[SEM_SPEC, SEM_SPEC] + [HBM_SPEC] * (2 * n) + [pl.BlockSpec(memory_space=pltpu.VMEM)],
        out_shape=[pltpu.SemaphoreType.DMA((3 * n,)), pltpu.SemaphoreType.DMA((3 * n,))]
        + [pltpu.HBM(sm.shape, sm.dtype) for sm in sums] + [pltpu.HBM(sh, dt) for sh, dt in shapes]
        + [jax.ShapeDtypeStruct((8, LANES), F32)],
        input_output_aliases={t: 2 + t for t in range(2 * n)},
        name="grads_ici_start", compiler_params=SPLIT)(*[pltpu.with_memory_space_constraint(sm, pltpu.HBM) for sm in sums], *lands)
    return outs[0], outs[1], list(outs[2:2 + n]), list(outs[2 + n:2 + 2 * n]), outs[2 + 2 * n]


def _grads_ici_wait(sums, lands, send_sems, recv_sems, axes, after):
    n = len(sums)

    def body(*refs):
        mine, theirs = _grads_copies(refs[:n], refs[n:2 * n], axes, refs[2 * n], refs[2 * n + 1])
        for cp in mine:
            cp.wait_send()
        for cp in theirs:
            cp.wait_recv()

    outs = _pcall(
        body, in_specs=[HBM_SPEC] * (2 * n) + [SEM_SPEC, SEM_SPEC, ANY_SPEC], out_specs=[HBM_SPEC] * (2 * n),
        out_shape=[pltpu.HBM(a.shape, a.dtype) for a in list(sums) + list(lands)],
        input_output_aliases={t: t for t in range(2 * n)},
        name="grads_ici_wait", compiler_params=SPLIT)(*sums, *lands, send_sems, recv_sems, after)
    return list(outs[:n]), list(outs[n:])


def _join_halves(joined):
    n = len(joined)

    def body(*refs):
        outs = refs[n:2 * n]
        send_sems, recv_sems = refs[2 * n:]
        x, y, c, _ = _position()
        remote = [_remote(outs[t].at[:, c], outs[t].at[:, c], send_sems.at[t], recv_sems.at[t], (x, y, 1 - c)) for t in range(n)]
        for cp in remote:
            cp.start()
        for t in range(n):
            _remote(outs[t].at[:, 1 - c], outs[t].at[:, 1 - c], send_sems.at[t], recv_sems.at[t], (x, y, 1 - c)).wait_recv()
        for cp in remote:
            cp.wait_send()

    return _pcall(
        body, in_specs=[HBM_SPEC] * n, out_specs=[HBM_SPEC] * n,
        out_shape=[jax.ShapeDtypeStruct(j.shape, j.dtype) for j in joined],
        input_output_aliases={t: t for t in range(n)},
        scratch_shapes=[pltpu.SemaphoreType.DMA((n,)), pltpu.SemaphoreType.DMA((n,))],
        name="join_halves", compiler_params=COMM)(*joined)


def _gather_small(shard):
    nl, r, cs = shard.shape

    def body(in_ref, out_ref, send_sems, recv_sems, local_sem):
        x, y, c, chips = _position()

        def cols(j):
            return out_ref.at[:, :, pl.ds(pl.multiple_of(j * cs, cs), cs)]

        me = 2 * x + y
        loc = pltpu.make_async_copy(in_ref, cols(me), local_sem)
        loc.start()
        remote = [_remote(in_ref, cols(me), send_sems.at[p], recv_sems.at[p], (px, py, c)) for p, (px, py) in enumerate(chips)]
        for cp in remote:
            cp.start()
        for p, (px, py) in enumerate(chips):
            _remote(in_ref, cols(2 * px + py), send_sems.at[p], recv_sems.at[p], (px, py, c)).wait_recv()
        for cp in remote:
            cp.wait_send()
        loc.wait()

    return _pcall(
        body, in_specs=[HBM_SPEC], out_specs=HBM_SPEC, out_shape=jax.ShapeDtypeStruct((nl, r, cs * N_CHIPS), shard.dtype),
        scratch_shapes=[pltpu.SemaphoreType.DMA((3,)), pltpu.SemaphoreType.DMA((3,)), pltpu.SemaphoreType.DMA(())],
        name="gather_small", compiler_params=COMM)(shard)


def _all_reduce_small(pack):
    r, c = pack.shape

    def body(in_ref, out_ref, slots, send_sems, recv_sems, local_sem):
        x, y, cc, _ = _position()
        me = 4 * x + 2 * y + cc
        peers = []
        for k in range(1, N_DEV):
            fx, fy, fc = (k >> 2) & 1, (k >> 1) & 1, k & 1
            px, py, pc = x ^ fx, y ^ fy, cc ^ fc
            peers.append((k - 1, (px, py, pc), 4 * px + 2 * py + pc))
        loc = pltpu.make_async_copy(in_ref, slots.at[me], local_sem)
        loc.start()
        copies = [_remote(in_ref, slots.at[me], send_sems.at[k], recv_sems.at[k], dev) for k, dev, _ in peers]
        for cp in copies:
            cp.start()
        for k, dev, idx in peers:
            _remote(in_ref, slots.at[idx], send_sems.at[k], recv_sems.at[k], dev).wait_recv()
        for cp in copies:
            cp.wait_send()
        loc.wait()
        acc = slots[0]
        for k in range(1, N_DEV):
            acc = acc + slots[k]
        out_ref[...] = acc

    vm = pl.BlockSpec(memory_space=pltpu.VMEM)
    return _pcall(
        body, in_specs=[vm], out_specs=vm, out_shape=jax.ShapeDtypeStruct((r, c), F32),
        scratch_shapes=[pltpu.VMEM((N_DEV, r, c), F32), pltpu.SemaphoreType.DMA((N_DEV - 1,)),
                        pltpu.SemaphoreType.DMA((N_DEV - 1,)), pltpu.SemaphoreType.DMA(())],
        name="all_reduce_small", compiler_params=pltpu.CompilerParams(has_side_effects=True, vmem_limit_bytes=VMEM_LIMIT))(pack)


def _dims(d):
    half = d // 2
    return half // HEAD_SB, half // HEAD_XA, 3, (5 * half) // HEAD_XA


def _layer_fwd(x, mem, full, small, l, after=None):
    h_sb, h_xa, u_blk, q_blk = _dims(x.shape[1])

    def vec(name):
        return small[name][l].reshape(1, -1)

    h1 = _norm_fwd(x, vec('g_mix_pre'), None, BF16, "norm_mix_pre", after)
    proj = _mm(h1, full['w_in'], 'nn', F32, "mm_proj")
    o_sb = _sb_fwd(proj, h_sb)
    b_st = small['b_s'][l].T
    o_gm = _gm_fwd(proj, vec('g_vnorm'), small['w_s'][l], b_st, u_blk)
    memn = _norm_fwd(mem, vec('g_mem'), None, BF16, "norm_mem")
    mem_kv = _mm(memn, full['w_mem_kv'], 'nn', F32, "mm_mem_kv")
    o_xa = _xa_fwd(proj, mem_kv, q_blk, h_xa)
    zg = _mm(h1, full['w_gate'], 'nn', F32, "mm_gate")
    branches = [_mm(o, full[wn], 'nn', F32, "mm_branch")
                for o, wn in ((o_sb, 'w_br_sb'), (o_gm, 'w_br_gm'), (o_xa, 'w_br_xa'))]
    merged = _merge_fwd(zg, vec('b_gate'), branches)
    y1 = _mm(merged, full['w_out'], 'nn', F32, "mm_out")
    x1 = _norm_fwd(y1, vec('g_mix_post'), x, F32, "norm_mix_post")
    h2 = _norm_fwd(x1, vec('g_ffn_pre'), None, BF16, "norm_ffn_pre")
    up = _mm(h2, full['w_up'], 'nn', F32, "mm_up")
    act = _cg_fwd(up, full['conv_w'], vec('conv_b'))
    y2 = _mm(act, full['w_down'], 'nn', F32, "mm_down")
    x2 = _norm_fwd(y2, vec('g_ffn_post'), x1, F32, "norm_ffn_post")
    saved = dict(x0=x, h1=h1, proj=proj, o_sb=o_sb, o_gm=o_gm, o_xa=o_xa, memn=memn, mem_kv=mem_kv, zg=zg,
                 branches=branches, merged=merged, y1=y1, x1=x1, h2=h2, up=up, act=act, y2=y2, b_st=b_st)
    return x2, saved


def _layer_bwd(dx, mem, sv, full, small, l, after=None):
    h_sb, h_xa, u_blk, q_blk = _dims(dx.shape[1])

    def vec(name):
        return small[name][l].reshape(1, -1)

    gb, gs = {}, {}
    dy2, gs['g_ffn_post'] = _norm_bwd(sv['y2'], vec('g_ffn_post'), [dx], None, BF16, "norm_ffn_post_bwd", after)
    gb['w_down'] = _mm(sv['act'], dy2, 'tn', BF16, "mm_down_dw")
    dact = _mm(dy2, full['w_down'], 'nt', F32, "mm_down_dx")
    dgate, dval, gs['conv_w'], gs['conv_b'] = _cg_bwd(sv['up'], full['conv_w'], vec('conv_b'), dact)
    dup = jnp.concatenate([dgate, dval], axis=1)
    gb['w_up'] = _mm(sv['h2'], dup, 'tn', BF16, "mm_up_dw")
    dh2 = _mm(dup, full['w_up'], 'nt', F32, "mm_up_dx")
    dx1, gs['g_ffn_pre'] = _norm_bwd(sv['x1'], vec('g_ffn_pre'), [dh2], dx, F32, "norm_ffn_pre_bwd")
    dy1, gs['g_mix_post'] = _norm_bwd(sv['y1'], vec('g_mix_post'), [dx1], None, BF16, "norm_mix_post_bwd")
    gb['w_out'] = _mm(sv['merged'], dy1, 'tn', BF16, "mm_out_dw")
    dmerged = _mm(dy1, full['w_out'], 'nt', F32, "mm_out_dx")
    dzg, dbr, gs['b_gate'] = _merge_bwd(sv['zg'], vec('b_gate'), sv['branches'], dmerged)
    douts = []
    for o, db, wn in ((sv['o_sb'], dbr[0], 'w_br_sb'), (sv['o_gm'], dbr[1], 'w_br_gm'), (sv['o_xa'], dbr[2], 'w_br_xa')):
        gb[wn] = _mm(o, db, 'tn', BF16, "mm_branch_dw")
        douts.append(_mm(db, full[wn], 'nt', F32, "mm_branch_dx"))
    gb['w_gate'] = _mm(sv['h1'], dzg, 'tn', BF16, "mm_gate_dw")
    dh1_gate = _mm(dzg, full['w_gate'], 'nt', F32, "mm_gate_dx")
    dq_xa, dk_xa, dv_xa = _xa_bwd(sv['proj'], sv['mem_kv'], douts[2], q_blk, h_xa)
    dmem_kv = jnp.concatenate([dk_xa, dv_xa], axis=1).astype(BF16)
    gb['w_mem_kv'] = _mm(sv['memn'], dmem_kv, 'tn', BF16, "mm_mem_kv_dw")
    dmemn = _mm(dmem_kv, full['w_mem_kv'], 'nt', F32, "mm_mem_kv_dx")
    _, gs['g_mem'] = _norm_bwd(mem, vec('g_mem'), [dmemn], None, BF16, "norm_mem_bwd")
    du, dv, gs['g_vnorm'], gs['w_s'], db_st = _gm_bwd(sv['proj'], vec('g_vnorm'), small['w_s'][l], sv['b_st'], douts[1], u_blk)
    gs['b_s'] = db_st.T
    dq, dk, dvv = _sb_bwd(sv['proj'], douts[0], h_sb)
    dproj = jnp.concatenate([dq, dk, dvv, du, dv, dq_xa], axis=1).astype(BF16)
    gb['w_in'] = _mm(sv['h1'], dproj, 'tn', BF16, "mm_proj_dw")
    dh1_proj = _mm(dproj, full['w_in'], 'nt', F32, "mm_proj_dx")
    dx0, gs['g_mix_pre'] = _norm_bwd(sv['x0'], vec('g_mix_pre'), [dh1_gate, dh1_proj], dx1, F32, "norm_mix_pre_bwd")
    return dx0, gb, gs


def _local_step(x, mem, target, full, small):
    n_layers = len(full['w_in'])
    saved = []
    for l in range(n_layers):
        x, sv = _layer_fwd(x, mem, {n: full[n][l] for n in full}, small, l)
        saved.append(sv)
    sq, dx = _loss_head(x, target)
    gbig = {n: [None] * n_layers for n in BIG}
    gsmall = {n: [None] * n_layers for n in SMALL + ['conv_w']}
    for l in reversed(range(n_layers)):
        dx, gb, gs = _layer_bwd(dx, mem, saved[l], {n: full[n][l] for n in full}, small, l)
        for n in gb:
            gbig[n][l] = gb[n]
        for n in gs:
            gsmall[n][l] = gs[n]
    return sq, dx, gbig, gsmall


def _pack(arrays, rows_multiple):
    flat = jnp.concatenate([a.reshape(-1).astype(F32) for a in arrays])
    rows = -(-flat.shape[0] // LANES)
    rows = -(-rows // rows_multiple) * rows_multiple
    return jnp.pad(flat, (0, rows * LANES - flat.shape[0])).reshape(rows, LANES)


def _unpack(pack, like):
    flat = pack.reshape(-1)
    out, off = [], 0
    for a in like:
        out.append(flat[off:off + a.size].reshape(a.shape))
        off += a.size
    return out


def _grad_view(g, ax):
    r, c = g.shape
    return g.reshape(1, 2, r // 2, c) if ax == 2 else g.reshape(N_CHIPS, 2, r // (2 * N_CHIPS), c)


def kernel(x, mem, g_mix_pre, w_in, g_vnorm, w_s, b_s, g_mem, w_mem_kv, w_gate, b_gate, w_br_sb, w_br_gm, w_br_xa, w_out, g_mix_post, g_ffn_pre, w_up, conv_w, conv_b, w_down, g_ffn_post, loss_target, m_g_mix_pre, m_w_in, m_g_vnorm, m_w_s, m_b_s, m_g_mem, m_w_mem_kv, m_w_gate, m_b_gate, m_w_br_sb, m_w_br_gm, m_w_br_xa, m_w_out, m_g_mix_post, m_g_ffn_pre, m_w_up, m_conv_w, m_conv_b, m_w_down, m_g_ffn_post, v_g_mix_pre, v_w_in, v_g_vnorm, v_w_s, v_b_s, v_g_mem, v_w_mem_kv, v_w_gate, v_b_gate, v_w_br_sb, v_w_br_gm, v_w_br_xa, v_w_out, v_g_mix_post, v_g_ffn_pre, v_w_up, v_conv_w, v_conv_b, v_w_down, v_g_ffn_post):
    w = dict(g_mix_pre=g_mix_pre, w_in=w_in, g_vnorm=g_vnorm, w_s=w_s, b_s=b_s, g_mem=g_mem, w_mem_kv=w_mem_kv,
             w_gate=w_gate, b_gate=b_gate, w_br_sb=w_br_sb, w_br_gm=w_br_gm, w_br_xa=w_br_xa, w_out=w_out,
             g_mix_post=g_mix_post, g_ffn_pre=g_ffn_pre, w_up=w_up, conv_w=conv_w, conv_b=conv_b, w_down=w_down,
             g_ffn_post=g_ffn_post)
    m = dict(g_mix_pre=m_g_mix_pre, w_in=m_w_in, g_vnorm=m_g_vnorm, w_s=m_w_s, b_s=m_b_s, g_mem=m_g_mem,
             w_mem_kv=m_w_mem_kv, w_gate=m_w_gate, b_gate=m_b_gate, w_br_sb=m_w_br_sb, w_br_gm=m_w_br_gm,
             w_br_xa=m_w_br_xa, w_out=m_w_out, g_mix_post=m_g_mix_post, g_ffn_pre=m_g_ffn_pre, w_up=m_w_up,
             conv_w=m_conv_w, conv_b=m_conv_b, w_down=m_w_down, g_ffn_post=m_g_ffn_post)
    v = dict(g_mix_pre=v_g_mix_pre, w_in=v_w_in, g_vnorm=v_g_vnorm, w_s=v_w_s, b_s=v_b_s, g_mem=v_g_mem,
             w_mem_kv=v_w_mem_kv, w_gate=v_w_gate, b_gate=v_b_gate, w_br_sb=v_w_br_sb, w_br_gm=v_w_br_gm,
             w_br_xa=v_w_br_xa, w_out=v_w_out, g_mix_post=v_g_mix_post, g_ffn_pre=v_g_ffn_pre, w_up=v_w_up,
             conv_w=v_conv_w, conv_b=v_conv_b, w_down=v_w_down, g_ffn_post=v_g_ffn_post)
    n_layers = w_in.shape[0]
    d = x.shape[-1]
    axes = [BIG_AXIS[n] for n in BIG]
    core = lax.axis_index("c").astype(jnp.int32).reshape(1)
    chip = (2 * lax.axis_index("x") + lax.axis_index("y")).astype(jnp.int32).reshape(1)
    small = {n: w[n] for n in SMALL}
    xs, mems, target = x[0], mem[0], loss_target[0]

    shard_cols = [w[n].shape[2] for n in BIG]
    conv_w_full = _gather_small(conv_w)

    def place(l):
        return [_place_own(w[n], l, chip, ax) for n, ax in zip(BIG, axes)]

    def as_full(views, l):
        full = {n: vw.reshape(-1, vw.shape[-1]) if ax == 1 else vw.reshape(vw.shape[0] * vw.shape[1], vw.shape[2])
                for n, vw, ax in zip(BIG, views, axes)}
        full['conv_w'] = conv_w_full[l]
        return full

    fulls, saved = [], []
    views = _gather_d2d(_gather_ici(place(0), axes, shard_cols), axes, shard_cols)
    for l in range(n_layers):
        fulls.append(as_full(views, l))
        token = None
        if l + 1 < n_layers:
            send_sems, recv_sems, nxt, token = _gather_ici_start(place(l + 1), axes, shard_cols, views[0])
        xs, sv = _layer_fwd(xs, mems, fulls[l], small, l, token)
        saved.append(sv)
        if l + 1 < n_layers:
            views = _gather_d2d(_gather_ici_wait(nxt, send_sems, recv_sems, axes, shard_cols, xs), axes, shard_cols)
    sq, dx = _loss_head(xs, target)
    loss = lax.psum(0.5 * jnp.sum(sq) / d, ("x", "y", "c"))

    def chip_sums(recv, sums, l, joined):
        return [_sum_chips(r, sm, None if joined is None else joined[t], l, n_layers, chip, core, ax)
                for t, (r, sm, ax) in enumerate(zip(recv, sums, axes))]

    joined, pending, token = None, None, None
    gsmall = {n: [None] * n_layers for n in SMALL + ['conv_w']}
    for l in reversed(range(n_layers)):
        dx, gb, gs = _layer_bwd(dx, mems, saved[l], fulls[l], small, l, token)
        for n in gs:
            gsmall[n][l] = gs[n]
        if pending is not None:
            sums, lands, send_sems, recv_sems = pending
            sums, recv = _grads_ici_wait(sums, lands, send_sems, recv_sems, axes, dx)
            joined = chip_sums(recv, sums, l + 1, joined)
        dwvs = [_grad_view(gb[n], ax) for n, ax in zip(BIG, axes)]
        theirs = _grads_d2d(dwvs)
        sums = [_sum_halves(dv, th, core, "sum_halves") for dv, th in zip(dwvs, theirs)]
        if l > 0:
            send_sems, recv_sems, sums, lands, token = _grads_ici_start(sums, axes)
            pending = (sums, lands, send_sems, recv_sems)
        else:
            joined = chip_sums(_grads_ici(sums, axes), sums, l, joined)
    joined = _join_halves(joined)

    small_full = [jnp.stack(gsmall[n]).reshape(w[n].shape) for n in SMALL]
    conv_w_grad = jnp.stack(gsmall['conv_w'])
    summed = _all_reduce_small(_pack(small_full + [conv_w_grad], 8))
    *small_g, conv_w_g = _unpack(summed, small_full + [conv_w_grad])
    shard = conv_w.shape[-1]
    conv_w_g = lax.dynamic_slice_in_dim(conv_w_g, chip[0] * shard, shard, axis=2)

    out = {}
    for n, g in zip(BIG, joined):
        out[n] = _adamw(w[n], m[n], v[n], g.reshape(w[n].shape), "adamw_big")
    names = SMALL + ['conv_w']
    packed = [_pack([p[n] for n in names], 256) for p in (w, m, v)]
    gpack = _pack(small_g + [conv_w_g], 256)
    res = _adamw(packed[0][None], packed[1][None], packed[2][None], gpack[None], "adamw_small")
    like = [w[n] for n in names]
    unpacked = [_unpack(r[0], like) for r in res]
    for i, n in enumerate(names):
        out[n] = tuple(u[i] for u in unpacked)

    return (loss, dx[None], *[out[n][0] for n in WEIGHTS], *[out[n][1] for n in WEIGHTS],
            *[out[n][2] for n in WEIGHTS], *[out[n][3] for n in WEIGHTS])
```

```python
import functools
import math

import jax
import jax.numpy as jnp
from jax import lax
from jax.experimental import pallas as pl
from jax.experimental.pallas import tpu as pltpu

F32 = jnp.float32
BF16 = jnp.bfloat16
EPS = 1e-6
HEAD_SB = 128
GROUP_GM = 128
CHUNK = 64
HEAD_XA = 256
CONV_TAPS = 3
N_CHIPS = 4
N_DEV = 8
LANES = 128
MIB = 1024 * 1024
VMEM_LIMIT = 48 * MIB
SPLITS = 2

ADAM_LR = 0.001
ADAM_B1 = 0.9
ADAM_B2 = 0.999
ADAM_EPS = 1e-08
ADAM_WD = 0.01
ADAM_STEP = 10

WEIGHTS = ['g_mix_pre', 'w_in', 'g_vnorm', 'w_s', 'b_s', 'g_mem', 'w_mem_kv', 'w_gate', 'b_gate', 'w_br_sb',
           'w_br_gm', 'w_br_xa', 'w_out', 'g_mix_post', 'g_ffn_pre', 'w_up', 'conv_w', 'conv_b', 'w_down',
           'g_ffn_post']
BIG_AXIS = {'w_in': 2, 'w_mem_kv': 1, 'w_gate': 2, 'w_br_sb': 2, 'w_br_gm': 2, 'w_br_xa': 2, 'w_out': 1,
            'w_up': 2, 'w_down': 1}
BIG = list(BIG_AXIS)
FWD_GROUPS = [['w_in'], ['w_mem_kv', 'w_gate'], ['w_br_sb', 'w_br_gm', 'w_br_xa', 'w_out'], ['w_up'], ['w_down']]
BWD_GROUPS = [['w_down', 'w_up'], ['w_out', 'w_br_sb', 'w_br_gm', 'w_br_xa', 'w_gate'], ['w_mem_kv', 'w_in']]
SMALL = ['g_mix_pre', 'g_vnorm', 'w_s', 'b_s', 'g_mem', 'b_gate', 'g_mix_post', 'g_ffn_pre', 'conv_b', 'g_ffn_post']
MESH = pl.DeviceIdType.MESH


def _pcall(body, **kw):
    return pl.pallas_call(body, **kw)


def _params(sem=None, vmem=VMEM_LIMIT):
    return pltpu.CompilerParams(dimension_semantics=sem, vmem_limit_bytes=vmem)


def _tile(n, cands):
    for c in cands:
        if n % c == 0:
            return c
    return n


_GELU_C = math.sqrt(2.0 / math.pi)
_GELU_A = 0.044715


def _gelu(x):
    return 0.5 * x * (1.0 + jnp.tanh(_GELU_C * (x + _GELU_A * (x * x * x))))


def _gelu_and_grad(x):
    x2 = x * x
    t = jnp.tanh(_GELU_C * (x + _GELU_A * (x2 * x)))
    val = 0.5 * x * (1.0 + t)
    grad = 0.5 * (1.0 + t) + 0.5 * x * (1.0 - t * t) * (_GELU_C * (1.0 + 3.0 * _GELU_A * x2))
    return val, grad


def _softplus(z):
    return jnp.maximum(z, 0.0) + jnp.log1p(jnp.exp(-jnp.abs(z)))


def _dot(a, b):
    return jnp.dot(a, b, preferred_element_type=F32)


def _dot_nt(a, b):
    return lax.dot_general(a, b, (((1,), (1,)), ((), ())), preferred_element_type=F32)


def _dot_tn(a, b):
    return lax.dot_general(a, b, (((0,), (0,)), ((), ())), preferred_element_type=F32)


def _split_dot(a, m):
    out = None
    rest = a
    for _ in range(SPLITS):
        piece = rest.astype(BF16)
        rest = rest - piece.astype(F32)
        term = _dot(piece, m)
        out = term if out is None else out + term
    return out


def _mm(a, b, mode, out_dtype, name, tm=None, tn=None, tk=None, after=None):
    if mode == 'nn':
        (m, kc), (kc2, n) = a.shape, b.shape
    elif mode == 'nt':
        (m, kc), (n, kc2) = a.shape, b.shape
    else:
        (kc, m), (kc2, n) = a.shape, b.shape
    assert kc == kc2, (a.shape, b.shape, mode)
    tm = tm or _tile(m, (1024, 512, 256, 128))
    tn = tn or _tile(n, (1024, 512, 256, 128))
    tk = tk or (kc if kc <= 2048 else _tile(kc, (2048, 1536, 1408, 1024, 512)))
    nk = kc // tk
    dot = {'nn': _dot, 'nt': _dot_nt, 'tn': _dot_tn}[mode]
    a_spec = pl.BlockSpec((tk, tm), lambda i, j, k: (k, i)) if mode == 'tn' else pl.BlockSpec((tm, tk), lambda i, j, k: (i, k))
    b_spec = pl.BlockSpec((tn, tk), lambda i, j, k: (j, k)) if mode == 'nt' else pl.BlockSpec((tk, tn), lambda i, j, k: (k, j))

    extra = [] if after is None else [after]
    extra_specs = [pl.BlockSpec(memory_space=pl.ANY)] * len(extra)

    if nk == 1:
        def body(a_ref, b_ref, *rest):
            o_ref = rest[-1]
            o_ref[...] = dot(a_ref[...].astype(BF16), b_ref[...].astype(BF16)).astype(o_ref.dtype)
        scratch = []
    else:
        def body(a_ref, b_ref, *rest):
            o_ref, acc_ref = rest[-2], rest[-1]
            k = pl.program_id(2)
            part = dot(a_ref[...].astype(BF16), b_ref[...].astype(BF16))

            @pl.when(k == 0)
            def _():
                acc_ref[...] = part

            @pl.when(k > 0)
            def _():
                acc_ref[...] += part

            @pl.when(k == nk - 1)
            def _():
                o_ref[...] = acc_ref[...].astype(o_ref.dtype)
        scratch = [pltpu.VMEM((tm, tn), F32)]

    return _pcall(
        body, grid=(m // tm, n // tn, nk), in_specs=[a_spec, b_spec] + extra_specs,
        out_specs=pl.BlockSpec((tm, tn), lambda i, j, k: (i, j)),
        out_shape=jax.ShapeDtypeStruct((m, n), out_dtype), scratch_shapes=scratch, name=name,
        compiler_params=_params(("parallel", "parallel", "arbitrary")))(a, b, *extra)


def _norm_fwd(x, g, res, out_dtype, name, after=None):
    s, d = x.shape
    tr = _tile(s, (256, 128))
    has_res = res is not None
    has_after = after is not None

    def body(*refs):
        x_ref, g_ref = refs[0], refs[1]
        o_ref = refs[-1]
        xv = x_ref[...]
        y = xv * lax.rsqrt(jnp.mean(xv * xv, axis=-1, keepdims=True) + EPS) * g_ref[...]
        if has_res:
            y = y + refs[2][...]
        o_ref[...] = y.astype(o_ref.dtype)

    row = pl.BlockSpec((tr, d), lambda i: (i, 0))
    ins = [x, g] + ([res] if has_res else []) + ([after] if has_after else [])
    return _pcall(
        body, grid=(s // tr,),
        in_specs=[row, pl.BlockSpec((1, d), lambda i: (0, 0))] + ([row] if has_res else [])
        + ([pl.BlockSpec(memory_space=pl.ANY)] if has_after else []),
        out_specs=row, out_shape=jax.ShapeDtypeStruct((s, d), out_dtype), name=name,
        compiler_params=_params(("parallel",)))(*ins)


def _norm_bwd(x, g, douts, dres, out_dtype, name, after=None):
    s, d = x.shape
    tr = _tile(s, (256, 128))
    nd = len(douts)
    has_res = dres is not None
    has_after = after is not None

    def body(*refs):
        x_ref, g_ref = refs[0], refs[1]
        dx_ref, dg_ref = refs[-2], refs[-1]
        dout = refs[2][...].astype(F32)
        for r in refs[3:2 + nd]:
            dout = dout + r[...].astype(F32)
        xv = x_ref[...]
        r = lax.rsqrt(jnp.mean(xv * xv, axis=-1, keepdims=True) + EPS)
        n = xv * r
        dn = dout * g_ref[...]
        dx = r * (dn - n * jnp.mean(dn * n, axis=-1, keepdims=True))
        if has_res:
            dx = dx + refs[2 + nd][...]
        dx_ref[...] = dx.astype(dx_ref.dtype)

        @pl.when(pl.program_id(0) == 0)
        def _():
            dg_ref[...] = jnp.zeros_like(dg_ref)

        dg_ref[...] += jnp.sum(dout * n, axis=0, keepdims=True)

    row = pl.BlockSpec((tr, d), lambda i: (i, 0))
    vec = pl.BlockSpec((1, d), lambda i: (0, 0))
    ins = [x, g] + list(douts) + ([dres] if has_res else []) + ([after] if has_after else [])
    return _pcall(
        body, grid=(s // tr,),
        in_specs=[row, vec] + [row] * (nd + int(has_res)) + ([pl.BlockSpec(memory_space=pl.ANY)] if has_after else []),
        out_specs=[row, vec],
        out_shape=[jax.ShapeDtypeStruct((s, d), out_dtype), jax.ShapeDtypeStruct((1, d), F32)], name=name,
        compiler_params=_params(("arbitrary",)))(*ins)


def _loss_head(y, target):
    s, d = y.shape
    tr = _tile(s, (256, 128))

    def body(y_ref, t_ref, sq_ref, dy_ref):
        e = y_ref[...] - t_ref[...]
        dy_ref[...] = e * (1.0 / d)

        @pl.when(pl.program_id(0) == 0)
        def _():
            sq_ref[...] = jnp.zeros_like(sq_ref)

        sq_ref[...] += jnp.sum(e * e, axis=0, keepdims=True)

    row = pl.BlockSpec((tr, d), lambda i: (i, 0))
    return _pcall(
        body, grid=(s // tr,), in_specs=[row, row], out_specs=[pl.BlockSpec((1, d), lambda i: (0, 0)), row],
        out_shape=[jax.ShapeDtypeStruct((1, d), F32), jax.ShapeDtypeStruct((s, d), F32)], name="loss_head",
        compiler_params=_params(("arbitrary",)))(y, target)


SB_QUERIES = 512


def _sb_sum_matrix(later):
    r = lax.broadcasted_iota(jnp.int32, (HEAD_SB, 2 * HEAD_SB), 0)
    c = lax.broadcasted_iota(jnp.int32, (HEAD_SB, 2 * HEAD_SB), 1)
    tri = jnp.where((r > c) if later else (r < c), 1.0, 0.0)
    return jnp.where(c < HEAD_SB, tri, 1.0).astype(BF16)


def _sb_mask(tq, q0, k0):
    row = lax.broadcasted_iota(jnp.int32, (tq, HEAD_SB), 0)
    col = lax.broadcasted_iota(jnp.int32, (tq, HEAD_SB), 1)
    return (k0 + col) < (q0 + row)


def _sb_fwd(proj, n_heads):
    s = proj.shape[0]
    tq = min(SB_QUERIES, s)
    per = tq // HEAD_SB
    scale = HEAD_SB ** -0.5

    def body(q_ref, k_ref, v_ref, o_ref, acc_ref, c_ref):
        i = pl.program_id(1)
        q = q_ref[...].astype(BF16)
        sums = _sb_sum_matrix(True)
        acc_ref[...] = jnp.zeros_like(acc_ref)
        c_ref[...] = jnp.zeros_like(c_ref)
        last = (i + 1) * per - 1

        def step(jj, masked):
            off = pl.multiple_of((last - jj) * HEAD_SB, HEAD_SB)
            kb = k_ref[pl.ds(off, HEAD_SB), :].astype(BF16)
            vb = v_ref[pl.ds(off, HEAD_SB), :].astype(BF16)
            z = _dot_nt(q, kb) * scale
            sp = _softplus(z)
            if masked:
                mask = _sb_mask(tq, i * tq, off)
                sp = jnp.where(mask, sp, 0.0)
            both = _split_dot(sp, sums)
            c = c_ref[...]
            a = jnp.exp(z - sp - both[:, :HEAD_SB] - c)
            if masked:
                a = jnp.where(mask, a, 0.0)
            acc_ref[...] += _dot(a.astype(BF16), vb)
            c_ref[...] = c + both[:, HEAD_SB:]

        def diagonal(jj, carry):
            step(jj, True)
            return carry

        def below(jj, carry):
            step(jj, False)
            return carry

        lax.fori_loop(0, per, diagonal, 0)
        lax.fori_loop(per, last + 1, below, 0)
        o_ref[...] = acc_ref[...].astype(o_ref.dtype)

    h = n_heads
    blk = pl.BlockSpec((tq, HEAD_SB), lambda hh, i: (i, hh))
    return _pcall(
        body, grid=(h, s // tq),
        in_specs=[blk, pl.BlockSpec((s, HEAD_SB), lambda hh, i: (0, h + hh)),
                  pl.BlockSpec((s, HEAD_SB), lambda hh, i: (0, 2 * h + hh))],
        out_specs=blk, out_shape=jax.ShapeDtypeStruct((s, h * HEAD_SB), BF16),
        scratch_shapes=[pltpu.VMEM((tq, HEAD_SB), F32), pltpu.VMEM((tq, HEAD_SB), F32)],
        name="sb_fwd", compiler_params=_params(("parallel", "arbitrary")))(proj, proj, proj)


def _sb_bwd(proj, do, n_heads):
    s = proj.shape[0]
    tq = min(SB_QUERIES, s)
    per = tq // HEAD_SB
    scale = HEAD_SB ** -0.5

    def body(q_ref, k_ref, v_ref, do_ref, dq_ref, dk_ref, dv_ref, g_ref, beta_ref, run_ref, acc_ref):
        i = pl.program_id(1)

        @pl.when(i == 0)
        def _():
            dk_ref[...] = jnp.zeros_like(dk_ref)
            dv_ref[...] = jnp.zeros_like(dv_ref)

        q = q_ref[...].astype(BF16)
        dob = do_ref[...].astype(BF16)
        last = (i + 1) * per - 1

        run_ref[...] = jnp.zeros_like(run_ref)
        later = _sb_sum_matrix(True)

        def left(jj, masked):
            j = last - jj
            off = pl.multiple_of(j * HEAD_SB, HEAD_SB)
            kb = k_ref[pl.ds(off, HEAD_SB), :].astype(BF16)
            vb = v_ref[pl.ds(off, HEAD_SB), :].astype(BF16)
            z = _dot_nt(q, kb) * scale
            sp = _softplus(z)
            beta = jnp.exp(z - sp)
            if masked:
                mask = _sb_mask(tq, i * tq, off)
                sp = jnp.where(mask, sp, 0.0)
            both = _split_dot(sp, later)
            c = run_ref[...]
            a = beta * jnp.exp(-(both[:, :HEAD_SB] + c))
            if masked:
                a = jnp.where(mask, a, 0.0)
            g_ref[j] = a * _dot_nt(dob, vb)
            beta_ref[j] = beta
            dv_ref[pl.ds(off, HEAD_SB), :] += _dot_tn(a.astype(BF16), dob)
            run_ref[...] = c + both[:, HEAD_SB:]

        def left_diagonal(jj, carry):
            left(jj, True)
            return carry

        def left_below(jj, carry):
            left(jj, False)
            return carry

        lax.fori_loop(0, per, left_diagonal, 0)
        lax.fori_loop(per, last + 1, left_below, 0)

        run_ref[...] = jnp.zeros_like(run_ref)
        acc_ref[...] = jnp.zeros_like(acc_ref)
        earlier = _sb_sum_matrix(False)

        def right(j, masked):
            off = pl.multiple_of(j * HEAD_SB, HEAD_SB)
            kb = k_ref[pl.ds(off, HEAD_SB), :].astype(BF16)
            g = g_ref[j]
            beta = beta_ref[j]
            both = _split_dot(g, earlier)
            p = run_ref[...]
            dz = (g * (1.0 - beta) - beta * (both[:, :HEAD_SB] + p)) * scale
            if masked:
                dz = jnp.where(_sb_mask(tq, i * tq, off), dz, 0.0)
            dzb = dz.astype(BF16)
            dk_ref[pl.ds(off, HEAD_SB), :] += _dot_tn(dzb, q)
            acc_ref[...] += _dot(dzb, kb)
            run_ref[...] = p + both[:, HEAD_SB:]

        def right_below(j, carry):
            right(j, False)
            return carry

        def right_diagonal(j, carry):
            right(j, True)
            return carry

        lax.fori_loop(0, last + 1 - per, right_below, 0)
        lax.fori_loop(last + 1 - per, last + 1, right_diagonal, 0)
        dq_ref[...] = acc_ref[...]

    h = n_heads
    blk = pl.BlockSpec((tq, HEAD_SB), lambda hh, i: (i, hh))
    col_blk = pl.BlockSpec((s, HEAD_SB), lambda hh, i: (0, hh))
    shape = jax.ShapeDtypeStruct((s, h * HEAD_SB), F32)
    nk = s // HEAD_SB
    return _pcall(
        body, grid=(h, s // tq),
        in_specs=[blk, pl.BlockSpec((s, HEAD_SB), lambda hh, i: (0, h + hh)),
                  pl.BlockSpec((s, HEAD_SB), lambda hh, i: (0, 2 * h + hh)), blk],
        out_specs=[blk, col_blk, col_blk], out_shape=[shape, shape, shape],
        scratch_shapes=[pltpu.VMEM((nk, tq, HEAD_SB), F32), pltpu.VMEM((nk, tq, HEAD_SB), F32),
                        pltpu.VMEM((tq, HEAD_SB), F32), pltpu.VMEM((tq, HEAD_SB), F32)],
        name="sb_bwd", compiler_params=_params(("parallel", "arbitrary")))(proj, proj, proj, do)


def _gm_mask():
    t = lax.broadcasted_iota(jnp.int32, (GROUP_GM, GROUP_GM), 0)
    s = lax.broadcasted_iota(jnp.int32, (GROUP_GM, GROUP_GM), 1)
    shift = CHUNK.bit_length() - 1
    return (s >> shift) <= (t >> shift)


def _gm_fwd(proj, g_vnorm, w_s, b_st, u_blk):
    s = proj.shape[0]
    groups = w_s.shape[0]
    w = groups * GROUP_GM

    def body(u_ref, v_ref, gv_ref, ws_ref, bst_ref, o_ref):
        ug = _gelu(u_ref[...])
        vg = _gelu(v_ref[...])
        vn = vg * lax.rsqrt(jnp.mean(vg * vg, axis=-1, keepdims=True) + EPS) * gv_ref[...]
        vnb = vn.astype(BF16)
        mask = _gm_mask()
        for g in range(groups):
            sl = slice(g * GROUP_GM, (g + 1) * GROUP_GM)
            wm = jnp.where(mask, ws_ref[g], 0.0).astype(BF16)
            mixed = _dot(wm, vnb[:, sl]) + bst_ref[:, g:g + 1]
            o_ref[:, sl] = (ug[:, sl] * mixed).astype(o_ref.dtype)

    return _pcall(
        body, grid=(s // GROUP_GM,),
        in_specs=[pl.BlockSpec((GROUP_GM, w), lambda c: (c, u_blk)), pl.BlockSpec((GROUP_GM, w), lambda c: (c, u_blk + 1)),
                  pl.BlockSpec((1, w), lambda c: (0, 0)), pl.BlockSpec((groups, GROUP_GM, GROUP_GM), lambda c: (0, 0, 0)),
                  pl.BlockSpec((GROUP_GM, groups), lambda c: (0, 0))],
        out_specs=pl.BlockSpec((GROUP_GM, w), lambda c: (c, 0)),
        out_shape=jax.ShapeDtypeStruct((s, w), BF16), name="gm_fwd",
        compiler_params=_params(("parallel",)))(proj, proj, g_vnorm, w_s, b_st)


def _gm_bwd(proj, g_vnorm, w_s, b_st, do, u_blk):
    s = proj.shape[0]
    groups = w_s.shape[0]
    w = groups * GROUP_GM

    def body(u_ref, v_ref, gv_ref, ws_ref, bst_ref, do_ref, du_ref, dv_ref, dgv_ref, dws_ref, dbst_ref, dvn_ref):
        @pl.when(pl.program_id(0) == 0)
        def _():
            dgv_ref[...] = jnp.zeros_like(dgv_ref)
            dws_ref[...] = jnp.zeros_like(dws_ref)
            dbst_ref[...] = jnp.zeros_like(dbst_ref)

        ug, ugrad = _gelu_and_grad(u_ref[...])
        vg, vgrad = _gelu_and_grad(v_ref[...])
        r = lax.rsqrt(jnp.mean(vg * vg, axis=-1, keepdims=True) + EPS)
        n = vg * r
        gv = gv_ref[...]
        vnb = (n * gv).astype(BF16)
        dout = do_ref[...]
        mask = _gm_mask()
        for g in range(groups):
            sl = slice(g * GROUP_GM, (g + 1) * GROUP_GM)
            wm = jnp.where(mask, ws_ref[g], 0.0).astype(BF16)
            mixed = _dot(wm, vnb[:, sl]) + bst_ref[:, g:g + 1]
            dmixed = dout[:, sl] * ug[:, sl]
            du_ref[:, sl] = dout[:, sl] * mixed * ugrad[:, sl]
            dbst_ref[:, g:g + 1] += jnp.sum(dmixed, axis=1, keepdims=True)
            dmb = dmixed.astype(BF16)
            dws_ref[g] += jnp.where(mask, _dot_nt(dmb, vnb[:, sl]), 0.0)
            dvn_ref[:, sl] = _dot_tn(wm, dmb)
        dvn = dvn_ref[...]
        dgv_ref[...] += jnp.sum(dvn * n, axis=0, keepdims=True)
        dn = dvn * gv
        dvg = r * (dn - n * jnp.mean(dn * n, axis=-1, keepdims=True))
        dv_ref[...] = dvg * vgrad

    rowb = pl.BlockSpec((GROUP_GM, w), lambda c: (c, 0))
    vec = pl.BlockSpec((1, w), lambda c: (0, 0))
    wsb = pl.BlockSpec((groups, GROUP_GM, GROUP_GM), lambda c: (0, 0, 0))
    bsb = pl.BlockSpec((GROUP_GM, groups), lambda c: (0, 0))
    return _pcall(
        body, grid=(s // GROUP_GM,),
        in_specs=[pl.BlockSpec((GROUP_GM, w), lambda c: (c, u_blk)), pl.BlockSpec((GROUP_GM, w), lambda c: (c, u_blk + 1)),
                  vec, wsb, bsb, rowb],
        out_specs=[rowb, rowb, vec, wsb, bsb],
        out_shape=[jax.ShapeDtypeStruct((s, w), F32), jax.ShapeDtypeStruct((s, w), F32), jax.ShapeDtypeStruct((1, w), F32),
                   jax.ShapeDtypeStruct((groups, GROUP_GM, GROUP_GM), F32), jax.ShapeDtypeStruct((GROUP_GM, groups), F32)],
        scratch_shapes=[pltpu.VMEM((GROUP_GM, w), F32)], name="gm_bwd",
        compiler_params=_params(("arbitrary",)))(proj, proj, g_vnorm, w_s, b_st, do)


def _xa_fwd(proj, mem_kv, q_blk, n_heads):
    s = proj.shape[0]
    nm = mem_kv.shape[0]
    tq = _tile(s, (512, 256, 128))
    scale = HEAD_XA ** -0.5

    def body(q_ref, k_ref, v_ref, o_ref):
        z = _dot_nt(q_ref[...].astype(BF16), k_ref[...].astype(BF16)) * scale
        z = z - jnp.max(z, axis=-1, keepdims=True)
        e = jnp.exp(z)
        p = e / jnp.sum(e, axis=-1, keepdims=True)
        o_ref[...] = _dot(p.astype(BF16), v_ref[...].astype(BF16)).astype(o_ref.dtype)

    h = n_heads
    return _pcall(
        body, grid=(h, s // tq),
        in_specs=[pl.BlockSpec((tq, HEAD_XA), lambda hh, i: (i, q_blk + hh)),
                  pl.BlockSpec((nm, HEAD_XA), lambda hh, i: (0, hh)), pl.BlockSpec((nm, HEAD_XA), lambda hh, i: (0, h + hh))],
        out_specs=pl.BlockSpec((tq, HEAD_XA), lambda hh, i: (i, hh)),
        out_shape=jax.ShapeDtypeStruct((s, h * HEAD_XA), BF16), name="xa_fwd",
        compiler_params=_params(("parallel", "parallel")))(proj, mem_kv, mem_kv)


def _xa_bwd(proj, mem_kv, do, q_blk, n_heads):
    s = proj.shape[0]
    nm = mem_kv.shape[0]
    tq = _tile(s, (512, 256, 128))
    scale = HEAD_XA ** -0.5
    h = n_heads

    def body(q_ref, k_ref, v_ref, do_ref, dq_ref, dk_ref, dv_ref):
        @pl.when(pl.program_id(1) == 0)
        def _():
            dk_ref[...] = jnp.zeros_like(dk_ref)
            dv_ref[...] = jnp.zeros_like(dv_ref)

        qb = q_ref[...].astype(BF16)
        kb = k_ref[...].astype(BF16)
        vb = v_ref[...].astype(BF16)
        dob = do_ref[...].astype(BF16)
        z = _dot_nt(qb, kb) * scale
        z = z - jnp.max(z, axis=-1, keepdims=True)
        e = jnp.exp(z)
        p = e / jnp.sum(e, axis=-1, keepdims=True)
        dp = _dot_nt(dob, vb)
        dz = (p * (dp - jnp.sum(dp * p, axis=-1, keepdims=True)) * scale).astype(BF16)
        dq_ref[...] = _dot(dz, kb)
        dk_ref[...] += _dot_tn(dz, qb)
        dv_ref[...] += _dot_tn(p.astype(BF16), dob)

    qspec = pl.BlockSpec((tq, HEAD_XA), lambda hh, i: (i, hh))
    dk, dv = None, None
    dq, dk, dv = _pcall(
        body, grid=(h, s // tq),
        in_specs=[pl.BlockSpec((tq, HEAD_XA), lambda hh, i: (i, q_blk + hh)),
                  pl.BlockSpec((nm, HEAD_XA), lambda hh, i: (0, hh)), pl.BlockSpec((nm, HEAD_XA), lambda hh, i: (0, h + hh)),
                  qspec],
        out_specs=[qspec, pl.BlockSpec((nm, HEAD_XA), lambda hh, i: (0, hh)), pl.BlockSpec((nm, HEAD_XA), lambda hh, i: (0, hh))],
        out_shape=[jax.ShapeDtypeStruct((s, h * HEAD_XA), F32), jax.ShapeDtypeStruct((nm, h * HEAD_XA), F32),
                   jax.ShapeDtypeStruct((nm, h * HEAD_XA), F32)],
        name="xa_bwd", compiler_params=_params(("parallel", "arbitrary")))(proj, mem_kv, mem_kv, do)
    return dq, dk, dv


def _merge_fwd(zg, b_gate, branches):
    s, d = branches[0].shape
    tr = _tile(s, (128,))

    def body(z0, z1, z2, g0, g1, g2, b0, b1, b2, o_ref):
        acc = None
        for z, g, b in ((z0, g0, b0), (z1, g1, b1), (z2, g2, b2)):
            term = jax.nn.sigmoid(z[...] + g[...]) * b[...]
            acc = term if acc is None else acc + term
        o_ref[...] = acc.astype(o_ref.dtype)

    zs = [pl.BlockSpec((tr, d), functools.partial(lambda i, k: (i, k), k=k)) for k in range(3)]
    gs = [pl.BlockSpec((1, d), functools.partial(lambda i, k: (0, k), k=k)) for k in range(3)]
    row = pl.BlockSpec((tr, d), lambda i: (i, 0))
    return _pcall(
        body, grid=(s // tr,), in_specs=zs + gs + [row] * 3, out_specs=row,
        out_shape=jax.ShapeDtypeStruct((s, d), BF16), name="merge_fwd",
        compiler_params=_params(("parallel",)))(zg, zg, zg, b_gate, b_gate, b_gate, *branches)


def _merge_bwd(zg, b_gate, branches, dmerged):
    s, d = branches[0].shape
    tr = _tile(s, (128,))

    def body(z0, z1, z2, g0, g1, g2, b0, b1, b2, dm_ref, dz_ref, d0, d1, d2, dbg_ref):
        @pl.when(pl.program_id(0) == 0)
        def _():
            dbg_ref[...] = jnp.zeros_like(dbg_ref)

        dm = dm_ref[...]
        for k, (z, g, b, dbr) in enumerate(((z0, g0, b0, d0), (z1, g1, b1, d1), (z2, g2, b2, d2))):
            sg = jax.nn.sigmoid(z[...] + g[...])
            dbr[...] = (dm * sg).astype(dbr.dtype)
            dz = dm * b[...] * sg * (1.0 - sg)
            dz_ref[:, k * d:(k + 1) * d] = dz.astype(dz_ref.dtype)
            dbg_ref[:, k * d:(k + 1) * d] += jnp.sum(dz, axis=0, keepdims=True)

    zs = [pl.BlockSpec((tr, d), functools.partial(lambda i, k: (i, k), k=k)) for k in range(3)]
    gs = [pl.BlockSpec((1, d), functools.partial(lambda i, k: (0, k), k=k)) for k in range(3)]
    row = pl.BlockSpec((tr, d), lambda i: (i, 0))
    outs = _pcall(
        body, grid=(s // tr,), in_specs=zs + gs + [row] * 4,
        out_specs=[pl.BlockSpec((tr, 3 * d), lambda i: (i, 0)), row, row, row, pl.BlockSpec((1, 3 * d), lambda i: (0, 0))],
        out_shape=[jax.ShapeDtypeStruct((s, 3 * d), BF16)] + [jax.ShapeDtypeStruct((s, d), BF16)] * 3
        + [jax.ShapeDtypeStruct((1, 3 * d), F32)],
        name="merge_bwd", compiler_params=_params(("arbitrary",)))(zg, zg, zg, b_gate, b_gate, b_gate, *branches, dmerged)
    return outs[0], list(outs[1:4]), outs[4]


def _shift_down(x, k, row):
    return jnp.where(row >= k, pltpu.roll(x, k, 0), 0.0)


def _shift_up(x, k, row, s):
    return jnp.where(row < s - k, pltpu.roll(x, s - k, 0), 0.0)


def _conv_pre(gate, cw_ref, cb_ref, row):
    conv = cb_ref[...] + cw_ref[CONV_TAPS - 1:CONV_TAPS, :] * gate
    for k in range(1, CONV_TAPS):
        conv = conv + cw_ref[CONV_TAPS - 1 - k:CONV_TAPS - k, :] * _shift_down(gate, k, row)
    return conv


def _cg_fwd(up, conv_w, conv_b):
    s = up.shape[0]
    f = conv_w.shape[1]
    tc = _tile(f, (256, 128))
    nb = f // tc

    def body(g_ref, v_ref, cw_ref, cb_ref, o_ref):
        row = lax.broadcasted_iota(jnp.int32, (s, tc), 0)
        conv = _conv_pre(g_ref[...], cw_ref, cb_ref, row)
        o_ref[...] = (_gelu(conv) * v_ref[...]).astype(o_ref.dtype)

    return _pcall(
        body, grid=(nb,),
        in_specs=[pl.BlockSpec((s, tc), lambda j: (0, j)), pl.BlockSpec((s, tc), lambda j: (0, nb + j)),
                  pl.BlockSpec((CONV_TAPS, tc), lambda j: (0, j)), pl.BlockSpec((1, tc), lambda j: (0, j))],
        out_specs=pl.BlockSpec((s, tc), lambda j: (0, j)),
        out_shape=jax.ShapeDtypeStruct((s, f), BF16), name="cg_fwd",
        compiler_params=_params(("parallel",)))(up, up, conv_w, conv_b)


def _cg_bwd(up, conv_w, conv_b, dact):
    s = up.shape[0]
    f = conv_w.shape[1]
    tc = _tile(f, (256, 128))
    nb = f // tc

    def body(g_ref, v_ref, cw_ref, cb_ref, da_ref, dg_ref, dv_ref, dcw_ref, dcb_ref):
        row = lax.broadcasted_iota(jnp.int32, (s, tc), 0)
        gate = g_ref[...]
        conv = _conv_pre(gate, cw_ref, cb_ref, row)
        gel, ggrad = _gelu_and_grad(conv)
        da = da_ref[...]
        dv_ref[...] = (da * gel).astype(dv_ref.dtype)
        dconv = da * v_ref[...] * ggrad
        dgate = cw_ref[CONV_TAPS - 1:CONV_TAPS, :] * dconv
        dcw_ref[CONV_TAPS - 1:CONV_TAPS, :] = jnp.sum(dconv * gate, axis=0, keepdims=True)
        for k in range(1, CONV_TAPS):
            dgate = dgate + cw_ref[CONV_TAPS - 1 - k:CONV_TAPS - k, :] * _shift_up(dconv, k, row, s)
            dcw_ref[CONV_TAPS - 1 - k:CONV_TAPS - k, :] = jnp.sum(dconv * _shift_down(gate, k, row), axis=0, keepdims=True)
        dg_ref[...] = dgate.astype(dg_ref.dtype)
        dcb_ref[...] = jnp.sum(dconv, axis=0, keepdims=True)

    colb = pl.BlockSpec((s, tc), lambda j: (0, j))
    return _pcall(
        body, grid=(nb,),
        in_specs=[colb, pl.BlockSpec((s, tc), lambda j: (0, nb + j)), pl.BlockSpec((CONV_TAPS, tc), lambda j: (0, j)),
                  pl.BlockSpec((1, tc), lambda j: (0, j)), colb],
        out_specs=[colb, colb, pl.BlockSpec((CONV_TAPS, tc), lambda j: (0, j)), pl.BlockSpec((1, tc), lambda j: (0, j))],
        out_shape=[jax.ShapeDtypeStruct((s, f), BF16), jax.ShapeDtypeStruct((s, f), BF16),
                   jax.ShapeDtypeStruct((CONV_TAPS, f), F32), jax.ShapeDtypeStruct((1, f), F32)],
        name="cg_bwd", compiler_params=_params(("parallel",)))(up, up, conv_w, conv_b, dact)


def _row_tile(rows, cols):
    want = max(16, (256 * 1024) // cols)
    for c in (512, 256, 128, 64, 32, 16):
        if c <= want and rows % c == 0:
            return c
    return rows


def _sum_halves(dwv, recv, core, name):
    nj, _, a, c = dwv.shape
    tr = _row_tile(a, c)

    def body(core_ref, d_ref, r_ref, o_ref):
        o_ref[0] = (d_ref[0, 0].astype(F32) + r_ref[0].astype(F32)).astype(o_ref.dtype)

    grid_spec = pltpu.PrefetchScalarGridSpec(
        num_scalar_prefetch=1, grid=(nj, a // tr),
        in_specs=[pl.BlockSpec((1, 1, tr, c), lambda j, i, cr: (j, cr[0], i, 0)),
                  pl.BlockSpec((1, tr, c), lambda j, i, cr: (j, i, 0))],
        out_specs=pl.BlockSpec((1, tr, c), lambda j, i, cr: (j, i, 0)))
    return _pcall(body, grid_spec=grid_spec, out_shape=jax.ShapeDtypeStruct((nj, a, c), BF16), name=name,
                  compiler_params=_params(("parallel", "parallel")))(core, dwv, recv)


def _sum_chips(recv, own, joined, layer, n_layers, chip, core, ax):
    _, a, b = recv.shape
    tr = _row_tile(a, b)

    def body(chip_ref, core_ref, r_ref, own_ref, *rest):
        o_ref = rest[-1]
        me = chip_ref[0]
        mine = own_ref[0].astype(F32)
        acc = None
        for k in range(N_CHIPS):
            term = jnp.where(me == k, mine, r_ref[k].astype(F32))
            acc = term if acc is None else acc + term
        o_ref[0, 0] = acc

    own_spec = (pl.BlockSpec((1, tr, b), lambda i, ch, co: (0, i, ch[0])) if ax == 2
                else pl.BlockSpec((1, tr, b), lambda i, ch, co: (ch[0], i, 0)))
    in_specs = [pl.BlockSpec((N_CHIPS, tr, b), lambda i, ch, co: (0, i, 0)), own_spec]
    args = [chip, core, recv, own]
    aliases = {}
    if joined is not None:
        in_specs.append(pl.BlockSpec(memory_space=pl.ANY))
        args.append(joined)
        aliases = {4: 0}
    grid_spec = pltpu.PrefetchScalarGridSpec(
        num_scalar_prefetch=2, grid=(a // tr,), in_specs=in_specs,
        out_specs=pl.BlockSpec((1, 1, tr, b), lambda i, ch, co: (layer, co[0], i, 0)))
    return _pcall(body, grid_spec=grid_spec, out_shape=jax.ShapeDtypeStruct((n_layers, 2, a, b), F32),
                  input_output_aliases=aliases, name="sum_chips", compiler_params=_params(("parallel",)))(*args)


def _place_own(wt, layer, chip, ax):
    nl, r, c = wt.shape
    half = r // 2
    tr = _row_tile(half, c)
    nb = half // tr

    def body(chip_ref, w_ref, o_ref):
        o_ref[...] = w_ref[...].astype(BF16).reshape(o_ref.shape)

    if ax == 2:
        out_spec = pl.BlockSpec((1, tr, c), lambda h, i, ch: (h, i, ch[0]))
    else:
        out_spec = pl.BlockSpec((1, 1, tr, c), lambda h, i, ch: (ch[0], h, i, 0))
    grid_spec = pltpu.PrefetchScalarGridSpec(
        num_scalar_prefetch=1, grid=(2, nb),
        in_specs=[pl.BlockSpec((1, tr, c), lambda h, i, ch: (layer, h * nb + i, 0))], out_specs=out_spec)
    return _pcall(body, grid_spec=grid_spec, out_shape=jax.ShapeDtypeStruct(_full_view_shape(wt.shape, ax), BF16),
                  name="place_own", compiler_params=_params(("parallel", "parallel")))(chip, wt)


def _adamw(w, m, v, g, name):
    nl, r, c = w.shape
    tr = _row_tile(r, c)
    c1 = 1.0 - ADAM_B1 ** ADAM_STEP
    c2 = 1.0 - ADAM_B2 ** ADAM_STEP

    def body(w_ref, m_ref, v_ref, gin_ref, g_ref, d_ref, nm_ref, nv_ref):
        g = gin_ref[...]
        mm = ADAM_B1 * m_ref[...] + (1.0 - ADAM_B1) * g
        vv = ADAM_B2 * v_ref[...] + (1.0 - ADAM_B2) * (g * g)
        g_ref[...] = g
        nm_ref[...] = mm
        nv_ref[...] = vv
        d_ref[...] = -ADAM_LR * ((mm / c1) / (jnp.sqrt(vv / c2) + ADAM_EPS) + ADAM_WD * w_ref[...])

    blk = pl.BlockSpec((1, tr, c), lambda l, i: (l, i, 0))
    shape = jax.ShapeDtypeStruct((nl, r, c), F32)
    return _pcall(
        body, grid=(nl, r // tr), in_specs=[blk] * 4, out_specs=[blk] * 4, out_shape=[shape] * 4, name=name,
        compiler_params=_params(("parallel", "parallel")))(w, m, v, g)


HBM_SPEC = pl.BlockSpec(memory_space=pltpu.HBM)
COMM = pltpu.CompilerParams(has_side_effects=True)


def _position():
    x, y, c = lax.axis_index("x"), lax.axis_index("y"), lax.axis_index("c")
    chips = [(1 - x, y), (x, 1 - y), (1 - x, 1 - y)]
    return x, y, c, chips


def _remote(src, dst, send_sem, recv_sem, dev):
    return pltpu.make_async_remote_copy(src_ref=src, dst_ref=dst, send_sem=send_sem, recv_sem=recv_sem,
                                        device_id=dev, device_id_type=MESH)


def _full_view_shape(shard_shape, ax):
    _, r, c = shard_shape
    return (2, r // 2, c * N_CHIPS) if ax == 2 else (N_CHIPS, 2, r // 2, c)


def _piece(ref, ax, j, h, cs):
    if ax == 2:
        return ref.at[h, :, pl.ds(pl.multiple_of(j * cs, cs), cs)]
    return ref.at[j, h]


def _chip_block(ref, ax, j, cs):
    if ax == 2:
        return ref.at[:, :, pl.ds(pl.multiple_of(j * cs, cs), cs)]
    return ref.at[j]


SEM_SPEC = pl.BlockSpec(memory_space=pltpu.SEMAPHORE)
ANY_SPEC = pl.BlockSpec(memory_space=pl.ANY)
SPLIT = pltpu.CompilerParams(has_side_effects=pltpu.SideEffectType.DATAFLOW_SIDE_EFFECTING)


def _gather_copies(bufs, axes, shard_cols, send_sems, recv_sems):
    x, y, c, chips = _position()
    me = 2 * x + y
    mine, theirs = [], []
    for t, ax in enumerate(axes):
        own = _piece(bufs[t], ax, me, c, shard_cols[t])
        for p, (px, py) in enumerate(chips):
            k = t * 3 + p
            got = _piece(bufs[t], ax, 2 * px + py, c, shard_cols[t])
            mine.append(_remote(own, own, send_sems.at[k], recv_sems.at[k], (px, py, c)))
            theirs.append(_remote(got, got, send_sems.at[k], recv_sems.at[k], (px, py, c)))
    return mine, theirs


def _gather_ici(views, axes, shard_cols):
    n = len(views)

    def body(*refs):
        mine, theirs = _gather_copies(refs[n:2 * n], axes, shard_cols, *refs[2 * n:])
        for cp in mine:
            cp.start()
        for cp in theirs:
            cp.wait_recv()
        for cp in mine:
            cp.wait_send()

    return _pcall(
        body, in_specs=[HBM_SPEC] * n, out_specs=[HBM_SPEC] * n,
        out_shape=[jax.ShapeDtypeStruct(v.shape, v.dtype) for v in views],
        input_output_aliases={t: t for t in range(n)},
        scratch_shapes=[pltpu.SemaphoreType.DMA((3 * n,)), pltpu.SemaphoreType.DMA((3 * n,))],
        name="gather_ici", compiler_params=COMM)(*views)


def _gather_ici_start(views, axes, shard_cols, after, name):
    n = len(views)

    def body(*refs):
        send_sems, recv_sems = refs[n + 1], refs[n + 2]
        mine, _ = _gather_copies(refs[n + 3:2 * n + 3], axes, shard_cols, send_sems, recv_sems)
        for cp in mine:
            cp.start()
        refs[2 * n + 3][...] = jnp.zeros_like(refs[2 * n + 3])

    outs = _pcall(
        body, in_specs=[HBM_SPEC] * n + [ANY_SPEC],
        out_specs=[SEM_SPEC, SEM_SPEC] + [HBM_SPEC] * n + [pl.BlockSpec(memory_space=pltpu.VMEM)],
        out_shape=[pltpu.SemaphoreType.DMA((3 * n,)), pltpu.SemaphoreType.DMA((3 * n,))]
        + [pltpu.HBM(v.shape, v.dtype) for v in views] + [jax.ShapeDtypeStruct((8, LANES), F32)],
        input_output_aliases={t: 2 + t for t in range(n)},
        name=name, compiler_params=SPLIT)(*[pltpu.with_memory_space_constraint(v, pltpu.HBM) for v in views], after)
    return outs[0], outs[1], list(outs[2:2 + n]), outs[2 + n]


def _gather_ici_wait(views, send_sems, recv_sems, axes, shard_cols, after, name):
    n = len(views)

    def body(*refs):
        mine, theirs = _gather_copies(refs[:n], axes, shard_cols, refs[n], refs[n + 1])
        for cp in mine:
            cp.wait_send()
        for cp in theirs:
            cp.wait_recv()

    return _pcall(
        body, in_specs=[HBM_SPEC] * n + [SEM_SPEC, SEM_SPEC, ANY_SPEC], out_specs=[HBM_SPEC] * n,
        out_shape=[pltpu.HBM(v.shape, v.dtype) for v in views],
        input_output_aliases={t: t for t in range(n)},
        name=name, compiler_params=SPLIT)(*views, send_sems, recv_sems, after)


def _gather_d2d(views, axes, shard_cols):
    n = len(views)

    def body(*refs):
        outs = refs[n:2 * n]
        send_sems, recv_sems = refs[2 * n:]
        x, y, c, chips = _position()
        remote = []
        for t, ax in enumerate(axes):
            for p, (px, py) in enumerate(chips):
                k = t * 3 + p
                blk = _piece(outs[t], ax, 2 * px + py, c, shard_cols[t])
                remote.append(_remote(blk, blk, send_sems.at[k], recv_sems.at[k], (x, y, 1 - c)))
                remote[-1].start()
        for t, ax in enumerate(axes):
            for p, (px, py) in enumerate(chips):
                k = t * 3 + p
                blk = _piece(outs[t], ax, 2 * px + py, 1 - c, shard_cols[t])
                _remote(blk, blk, send_sems.at[k], recv_sems.at[k], (x, y, 1 - c)).wait_recv()
        for cp in remote:
            cp.wait_send()

    return _pcall(
        body, in_specs=[HBM_SPEC] * n, out_specs=[HBM_SPEC] * n,
        out_shape=[jax.ShapeDtypeStruct(v.shape, v.dtype) for v in views],
        input_output_aliases={t: t for t in range(n)},
        scratch_shapes=[pltpu.SemaphoreType.DMA((3 * n,)), pltpu.SemaphoreType.DMA((3 * n,))],
        name="gather_d2d", compiler_params=COMM)(*views)


def _grads_d2d(dwvs):
    n = len(dwvs)

    def body(*refs):
        ins, outs = refs[:n], refs[n:2 * n]
        send_sems, recv_sems = refs[2 * n:]
        x, y, c, _ = _position()
        remote = [_remote(ins[t].at[:, 1 - c], outs[t], send_sems.at[t], recv_sems.at[t], (x, y, 1 - c)) for t in range(n)]
        for cp in remote:
            cp.start()
        for cp in remote:
            cp.wait()

    return _pcall(
        body, in_specs=[HBM_SPEC] * n, out_specs=[HBM_SPEC] * n,
        out_shape=[jax.ShapeDtypeStruct((d.shape[0],) + d.shape[2:], d.dtype) for d in dwvs],
        scratch_shapes=[pltpu.SemaphoreType.DMA((n,)), pltpu.SemaphoreType.DMA((n,))],
        name="grads_d2d", compiler_params=COMM)(*dwvs)


def _grads_recv_shapes(sums, axes):
    out = []
    for sm, ax in zip(sums, axes):
        _, a, c = sm.shape
        out.append(((N_CHIPS, a, c // N_CHIPS if ax == 2 else c), sm.dtype))
    return out


def _grads_copies(ins, outs, axes, send_sems, recv_sems):
    x, y, c, chips = _position()
    me = 2 * x + y

    def block(t, j):
        if axes[t] == 2:
            cs = outs[t].shape[2]
            return ins[t].at[0, :, pl.ds(pl.multiple_of(j * cs, cs), cs)]
        return ins[t].at[j]

    mine, theirs = [], []
    for t in range(len(ins)):
        for p, (px, py) in enumerate(chips):
            k = t * 3 + p
            peer = 2 * px + py
            mine.append(_remote(block(t, peer), outs[t].at[me], send_sems.at[k], recv_sems.at[k], (px, py, c)))
            theirs.append(_remote(block(t, peer), outs[t].at[peer], send_sems.at[k], recv_sems.at[k], (px, py, c)))
    return mine, theirs


def _grads_ici(sums, axes):
    n = len(sums)

    def body(*refs):
        mine, theirs = _grads_copies(refs[:n], refs[n:2 * n], axes, *refs[2 * n:])
        for cp in mine:
            cp.start()
        for cp in theirs:
            cp.wait_recv()
        for cp in mine:
            cp.wait_send()

    return _pcall(
        body, in_specs=[HBM_SPEC] * n, out_specs=[HBM_SPEC] * n,
        out_shape=[jax.ShapeDtypeStruct(sh, dt) for sh, dt in _grads_recv_shapes(sums, axes)],
        scratch_shapes=[pltpu.SemaphoreType.DMA((3 * n,)), pltpu.SemaphoreType.DMA((3 * n,))],
        name="grads_ici", compiler_params=COMM)(*sums)


def _grads_ici_start(sums, axes, name):
    n = len(sums)
    shapes = _grads_recv_shapes(sums, axes)

    def body(*refs):
        send_sems, recv_sems = refs[2 * n], refs[2 * n + 1]
        mine, _ = _grads_copies(refs[2 * n + 2:3 * n + 2], refs[3 * n + 2:4 * n + 2], axes, send_sems, recv_sems)
        for cp in mine:
            cp.start()
        refs[4 * n + 2][...] = jnp.zeros_like(refs[4 * n + 2])

    lands = [pltpu.with_memory_space_constraint(lax.empty(sh, dt), pltpu.HBM) for sh, dt in shapes]
    outs = _pcall(
        body, in_specs=[HBM_SPEC] * (2 * n),
        out_specs=[SEM_SPEC, SEM_SPEC] + [HBM_SPEC] * (2 * n) + [pl.BlockSpec(memory_space=pltpu.VMEM)],
        out_shape=[pltpu.SemaphoreType.DMA((3 * n,)), pltpu.SemaphoreType.DMA((3 * n,))]
        + [pltpu.HBM(sm.shape, sm.dtype) for sm in sums] + [pltpu.HBM(sh, dt) for sh, dt in shapes]
        + [jax.ShapeDtypeStruct((8, LANES), F32)],
        input_output_aliases={t: 2 + t for t in range(2 * n)},
        name=name, compiler_params=SPLIT)(*[pltpu.with_memory_space_constraint(sm, pltpu.HBM) for sm in sums], *lands)
    return outs[0], outs[1], list(outs[2:2 + n]), list(outs[2 + n:2 + 2 * n]), outs[2 + 2 * n]


def _grads_ici_wait(sums, lands, send_sems, recv_sems, axes, after, name):
    n = len(sums)

    def body(*refs):
        mine, theirs = _grads_copies(refs[:n], refs[n:2 * n], axes, refs[2 * n], refs[2 * n + 1])
        for cp in mine:
            cp.wait_send()
        for cp in theirs:
            cp.wait_recv()

    outs = _pcall(
        body, in_specs=[HBM_SPEC] * (2 * n) + [SEM_SPEC, SEM_SPEC, ANY_SPEC], out_specs=[HBM_SPEC] * (2 * n),
        out_shape=[pltpu.HBM(a.shape, a.dtype) for a in list(sums) + list(lands)],
        input_output_aliases={t: t for t in range(2 * n)},
        name=name, compiler_params=SPLIT)(*sums, *lands, send_sems, recv_sems, after)
    return list(outs[:n]), list(outs[n:])


def _join_halves(joined):
    n = len(joined)

    def body(*refs):
        outs = refs[n:2 * n]
        send_sems, recv_sems = refs[2 * n:]
        x, y, c, _ = _position()
        remote = [_remote(outs[t].at[:, c], outs[t].at[:, c], send_sems.at[t], recv_sems.at[t], (x, y, 1 - c)) for t in range(n)]
        for cp in remote:
            cp.start()
        for t in range(n):
            _remote(outs[t].at[:, 1 - c], outs[t].at[:, 1 - c], send_sems.at[t], recv_sems.at[t], (x, y, 1 - c)).wait_recv()
        for cp in remote:
            cp.wait_send()

    return _pcall(
        body, in_specs=[HBM_SPEC] * n, out_specs=[HBM_SPEC] * n,
        out_shape=[jax.ShapeDtypeStruct(j.shape, j.dtype) for j in joined],
        input_output_aliases={t: t for t in range(n)},
        scratch_shapes=[pltpu.SemaphoreType.DMA((n,)), pltpu.SemaphoreType.DMA((n,))],
        name="join_halves", compiler_params=COMM)(*joined)


def _gather_small(shard):
    nl, r, cs = shard.shape

    def body(in_ref, out_ref, send_sems, recv_sems, local_sem):
        x, y, c, chips = _position()

        def cols(j):
            return out_ref.at[:, :, pl.ds(pl.multiple_of(j * cs, cs), cs)]

        me = 2 * x + y
        loc = pltpu.make_async_copy(in_ref, cols(me), local_sem)
        loc.start()
        remote = [_remote(in_ref, cols(me), send_sems.at[p], recv_sems.at[p], (px, py, c)) for p, (px, py) in enumerate(chips)]
        for cp in remote:
            cp.start()
        for p, (px, py) in enumerate(chips):
            _remote(in_ref, cols(2 * px + py), send_sems.at[p], recv_sems.at[p], (px, py, c)).wait_recv()
        for cp in remote:
            cp.wait_send()
        loc.wait()

    return _pcall(
        body, in_specs=[HBM_SPEC], out_specs=HBM_SPEC, out_shape=jax.ShapeDtypeStruct((nl, r, cs * N_CHIPS), shard.dtype),
        scratch_shapes=[pltpu.SemaphoreType.DMA((3,)), pltpu.SemaphoreType.DMA((3,)), pltpu.SemaphoreType.DMA(())],
        name="gather_small", compiler_params=COMM)(shard)


def _all_reduce_small(pack):
    r, c = pack.shape

    def body(in_ref, out_ref, slots, send_sems, recv_sems, local_sem):
        x, y, cc, _ = _position()
        me = 4 * x + 2 * y + cc
        peers = []
        for k in range(1, N_DEV):
            fx, fy, fc = (k >> 2) & 1, (k >> 1) & 1, k & 1
            px, py, pc = x ^ fx, y ^ fy, cc ^ fc
            peers.append((k - 1, (px, py, pc), 4 * px + 2 * py + pc))
        loc = pltpu.make_async_copy(in_ref, slots.at[me], local_sem)
        loc.start()
        copies = [_remote(in_ref, slots.at[me], send_sems.at[k], recv_sems.at[k], dev) for k, dev, _ in peers]
        for cp in copies:
            cp.start()
        for k, dev, idx in peers:
            _remote(in_ref, slots.at[idx], send_sems.at[k], recv_sems.at[k], dev).wait_recv()
        for cp in copies:
            cp.wait_send()
        loc.wait()
        acc = slots[0]
        for k in range(1, N_DEV):
            acc = acc + slots[k]
        out_ref[...] = acc

    vm = pl.BlockSpec(memory_space=pltpu.VMEM)
    return _pcall(
        body, in_specs=[vm], out_specs=vm, out_shape=jax.ShapeDtypeStruct((r, c), F32),
        scratch_shapes=[pltpu.VMEM((N_DEV, r, c), F32), pltpu.SemaphoreType.DMA((N_DEV - 1,)),
                        pltpu.SemaphoreType.DMA((N_DEV - 1,)), pltpu.SemaphoreType.DMA(())],
        name="all_reduce_small", compiler_params=pltpu.CompilerParams(has_side_effects=True, vmem_limit_bytes=VMEM_LIMIT))(pack)


def _dims(d):
    half = d // 2
    return half // HEAD_SB, half // HEAD_XA, 3, (5 * half) // HEAD_XA


def _layer_fwd(x, mem, weight, small, l, after=None):
    h_sb, h_xa, u_blk, q_blk = _dims(x.shape[1])

    def vec(name):
        return small[name][l].reshape(1, -1)

    h1 = _norm_fwd(x, vec('g_mix_pre'), None, BF16, "norm_mix_pre", after)
    proj = _mm(h1, weight('w_in', h1), 'nn', F32, "mm_proj")
    o_sb = _sb_fwd(proj, h_sb)
    b_st = small['b_s'][l].T
    o_gm = _gm_fwd(proj, vec('g_vnorm'), small['w_s'][l], b_st, u_blk)
    memn = _norm_fwd(mem, vec('g_mem'), None, BF16, "norm_mem")
    mem_kv = _mm(memn, weight('w_mem_kv', o_gm), 'nn', F32, "mm_mem_kv")
    o_xa = _xa_fwd(proj, mem_kv, q_blk, h_xa)
    zg = _mm(h1, weight('w_gate', o_gm), 'nn', F32, "mm_gate")
    branches = [_mm(o, weight(wn, zg), 'nn', F32, "mm_branch")
                for o, wn in ((o_sb, 'w_br_sb'), (o_gm, 'w_br_gm'), (o_xa, 'w_br_xa'))]
    merged = _merge_fwd(zg, vec('b_gate'), branches)
    y1 = _mm(merged, weight('w_out', zg), 'nn', F32, "mm_out")
    x1 = _norm_fwd(y1, vec('g_mix_post'), x, F32, "norm_mix_post")
    h2 = _norm_fwd(x1, vec('g_ffn_pre'), None, BF16, "norm_ffn_pre")
    up = _mm(h2, weight('w_up', h2), 'nn', F32, "mm_up")
    act = _cg_fwd(up, weight('conv_w', up), vec('conv_b'))
    y2 = _mm(act, weight('w_down', act), 'nn', F32, "mm_down")
    x2 = _norm_fwd(y2, vec('g_ffn_post'), x1, F32, "norm_ffn_post")
    saved = dict(x0=x, h1=h1, proj=proj, o_sb=o_sb, o_gm=o_gm, o_xa=o_xa, memn=memn, mem_kv=mem_kv, zg=zg,
                 branches=branches, merged=merged, y1=y1, x1=x1, h2=h2, up=up, act=act, y2=y2, b_st=b_st)
    return x2, saved


def _layer_bwd(dx, mem, sv, full, small, l, after, emit):
    h_sb, h_xa, u_blk, q_blk = _dims(dx.shape[1])

    def vec(name):
        return small[name][l].reshape(1, -1)

    gb, gs = {}, {}
    dy2, gs['g_ffn_post'] = _norm_bwd(sv['y2'], vec('g_ffn_post'), [dx], None, BF16, "norm_ffn_post_bwd", after)
    gb['w_down'] = _mm(sv['act'], dy2, 'tn', BF16, "mm_down_dw")
    dact = _mm(dy2, full['w_down'], 'nt', F32, "mm_down_dx")
    dgate, dval, gs['conv_w'], gs['conv_b'] = _cg_bwd(sv['up'], full['conv_w'], vec('conv_b'), dact)
    dup = jnp.concatenate([dgate, dval], axis=1)
    gb['w_up'] = _mm(sv['h2'], dup, 'tn', BF16, "mm_up_dw")
    token = emit(0, gb)
    dh2 = _mm(dup, full['w_up'], 'nt', F32, "mm_up_dx")
    dx1, gs['g_ffn_pre'] = _norm_bwd(sv['x1'], vec('g_ffn_pre'), [dh2], dx, F32, "norm_ffn_pre_bwd", token)
    dy1, gs['g_mix_post'] = _norm_bwd(sv['y1'], vec('g_mix_post'), [dx1], None, BF16, "norm_mix_post_bwd")
    gb['w_out'] = _mm(sv['merged'], dy1, 'tn', BF16, "mm_out_dw")
    dmerged = _mm(dy1, full['w_out'], 'nt', F32, "mm_out_dx")
    dzg, dbr, gs['b_gate'] = _merge_bwd(sv['zg'], vec('b_gate'), sv['branches'], dmerged)
    douts = []
    for o, db, wn in ((sv['o_sb'], dbr[0], 'w_br_sb'), (sv['o_gm'], dbr[1], 'w_br_gm'), (sv['o_xa'], dbr[2], 'w_br_xa')):
        gb[wn] = _mm(o, db, 'tn', BF16, "mm_branch_dw")
        douts.append(_mm(db, full[wn], 'nt', F32, "mm_branch_dx"))
    gb['w_gate'] = _mm(sv['h1'], dzg, 'tn', BF16, "mm_gate_dw")
    token = emit(1, gb)
    dh1_gate = _mm(dzg, full['w_gate'], 'nt', F32, "mm_gate_dx", after=token)
    dq_xa, dk_xa, dv_xa = _xa_bwd(sv['proj'], sv['mem_kv'], douts[2], q_blk, h_xa)
    dmem_kv = jnp.concatenate([dk_xa, dv_xa], axis=1).astype(BF16)
    gb['w_mem_kv'] = _mm(sv['memn'], dmem_kv, 'tn', BF16, "mm_mem_kv_dw")
    dmemn = _mm(dmem_kv, full['w_mem_kv'], 'nt', F32, "mm_mem_kv_dx")
    _, gs['g_mem'] = _norm_bwd(mem, vec('g_mem'), [dmemn], None, BF16, "norm_mem_bwd")
    du, dv, gs['g_vnorm'], gs['w_s'], db_st = _gm_bwd(sv['proj'], vec('g_vnorm'), small['w_s'][l], sv['b_st'], douts[1], u_blk)
    gs['b_s'] = db_st.T
    dq, dk, dvv = _sb_bwd(sv['proj'], douts[0], h_sb)
    dproj = jnp.concatenate([dq, dk, dvv, du, dv, dq_xa], axis=1).astype(BF16)
    gb['w_in'] = _mm(sv['h1'], dproj, 'tn', BF16, "mm_proj_dw")
    token = emit(2, gb)
    dh1_proj = _mm(dproj, full['w_in'], 'nt', F32, "mm_proj_dx")
    dx0, gs['g_mix_pre'] = _norm_bwd(sv['x0'], vec('g_mix_pre'), [dh1_gate, dh1_proj], dx1, F32, "norm_mix_pre_bwd", token)
    return dx0, gs


def _local_step(x, mem, target, full, small):
    n_layers = len(full['w_in'])
    saved = []
    for l in range(n_layers):
        x, sv = _layer_fwd(x, mem, lambda name, follows, l=l: full[name][l], small, l)
        saved.append(sv)
    sq, dx = _loss_head(x, target)
    gbig = {n: [None] * n_layers for n in BIG}
    gsmall = {n: [None] * n_layers for n in SMALL + ['conv_w']}
    for l in reversed(range(n_layers)):
        def emit(g, gb, l=l):
            for n in BWD_GROUPS[g]:
                gbig[n][l] = gb[n]

        dx, gs = _layer_bwd(dx, mem, saved[l], {n: full[n][l] for n in full}, small, l, None, emit)
        for n in gs:
            gsmall[n][l] = gs[n]
    return sq, dx, gbig, gsmall


def _pack(arrays, rows_multiple):
    flat = jnp.concatenate([a.reshape(-1).astype(F32) for a in arrays])
    rows = -(-flat.shape[0] // LANES)
    rows = -(-rows // rows_multiple) * rows_multiple
    return jnp.pad(flat, (0, rows * LANES - flat.shape[0])).reshape(rows, LANES)


def _unpack(pack, like):
    flat = pack.reshape(-1)
    out, off = [], 0
    for a in like:
        out.append(flat[off:off + a.size].reshape(a.shape))
        off += a.size
    return out


def _grad_view(g, ax):
    r, c = g.shape
    return g.reshape(1, 2, r // 2, c) if ax == 2 else g.reshape(N_CHIPS, 2, r // (2 * N_CHIPS), c)


def kernel(x, mem, g_mix_pre, w_in, g_vnorm, w_s, b_s, g_mem, w_mem_kv, w_gate, b_gate, w_br_sb, w_br_gm, w_br_xa, w_out, g_mix_post, g_ffn_pre, w_up, conv_w, conv_b, w_down, g_ffn_post, loss_target, m_g_mix_pre, m_w_in, m_g_vnorm, m_w_s, m_b_s, m_g_mem, m_w_mem_kv, m_w_gate, m_b_gate, m_w_br_sb, m_w_br_gm, m_w_br_xa, m_w_out, m_g_mix_post, m_g_ffn_pre, m_w_up, m_conv_w, m_conv_b, m_w_down, m_g_ffn_post, v_g_mix_pre, v_w_in, v_g_vnorm, v_w_s, v_b_s, v_g_mem, v_w_mem_kv, v_w_gate, v_b_gate, v_w_br_sb, v_w_br_gm, v_w_br_xa, v_w_out, v_g_mix_post, v_g_ffn_pre, v_w_up, v_conv_w, v_conv_b, v_w_down, v_g_ffn_post):
    w = dict(g_mix_pre=g_mix_pre, w_in=w_in, g_vnorm=g_vnorm, w_s=w_s, b_s=b_s, g_mem=g_mem, w_mem_kv=w_mem_kv,
             w_gate=w_gate, b_gate=b_gate, w_br_sb=w_br_sb, w_br_gm=w_br_gm, w_br_xa=w_br_xa, w_out=w_out,
             g_mix_post=g_mix_post, g_ffn_pre=g_ffn_pre, w_up=w_up, conv_w=conv_w, conv_b=conv_b, w_down=w_down,
             g_ffn_post=g_ffn_post)
    m = dict(g_mix_pre=m_g_mix_pre, w_in=m_w_in, g_vnorm=m_g_vnorm, w_s=m_w_s, b_s=m_b_s, g_mem=m_g_mem,
             w_mem_kv=m_w_mem_kv, w_gate=m_w_gate, b_gate=m_b_gate, w_br_sb=m_w_br_sb, w_br_gm=m_w_br_gm,
             w_br_xa=m_w_br_xa, w_out=m_w_out, g_mix_post=m_g_mix_post, g_ffn_pre=m_g_ffn_pre, w_up=m_w_up,
             conv_w=m_conv_w, conv_b=m_conv_b, w_down=m_w_down, g_ffn_post=m_g_ffn_post)
    v = dict(g_mix_pre=v_g_mix_pre, w_in=v_w_in, g_vnorm=v_g_vnorm, w_s=v_w_s, b_s=v_b_s, g_mem=v_g_mem,
             w_mem_kv=v_w_mem_kv, w_gate=v_w_gate, b_gate=v_b_gate, w_br_sb=v_w_br_sb, w_br_gm=v_w_br_gm,
             w_br_xa=v_w_br_xa, w_out=v_w_out, g_mix_post=v_g_mix_post, g_ffn_pre=v_g_ffn_pre, w_up=v_w_up,
             conv_w=v_conv_w, conv_b=v_conv_b, w_down=v_w_down, g_ffn_post=v_g_ffn_post)
    n_layers = w_in.shape[0]
    d = x.shape[-1]
    core = lax.axis_index("c").astype(jnp.int32).reshape(1)
    chip = (2 * lax.axis_index("x") + lax.axis_index("y")).astype(jnp.int32).reshape(1)
    small = {n: w[n] for n in SMALL}
    xs, mems, target = x[0], mem[0], loss_target[0]

    conv_w_full = _gather_small(conv_w)

    def as_full(vw, ax):
        return vw.reshape(-1, vw.shape[-1]) if ax == 1 else vw.reshape(vw.shape[0] * vw.shape[1], vw.shape[2])

    pending, token = [], core
    for l in range(n_layers):
        for g, names in enumerate(FWD_GROUPS):
            ax_g = [BIG_AXIS[n] for n in names]
            cols_g = [w[n].shape[2] for n in names]
            views = [_place_own(w[n], l, chip, ax) for n, ax in zip(names, ax_g)]
            send_sems, recv_sems, views, token = _gather_ici_start(views, ax_g, cols_g, token, f"gather_ici_start_{l}_{g}")
            pending.append((l, g, names, ax_g, cols_g, views, send_sems, recv_sems))

    fulls, saved = [], []
    for l in range(n_layers):
        ready = {'conv_w': conv_w_full[l]}

        def weight(name, follows, l=l, ready=ready):
            if name not in ready:
                (_, g, names, ax_g, cols_g, views, send_sems, recv_sems), = [p for p in pending if p[0] == l and name in p[2]]
                views = _gather_ici_wait(views, send_sems, recv_sems, ax_g, cols_g, follows, f"gather_ici_wait_{l}_{g}")
                views = _gather_d2d(views, ax_g, cols_g)
                for n, vw, ax in zip(names, views, ax_g):
                    ready[n] = as_full(vw, ax)
            return ready[name]

        xs, sv = _layer_fwd(xs, mems, weight, small, l, token if l == 0 else None)
        fulls.append(ready)
        saved.append(sv)
    sq, dx = _loss_head(xs, target)
    loss = lax.psum(0.5 * jnp.sum(sq) / d, ("x", "y", "c"))

    sent = []
    gsmall = {n: [None] * n_layers for n in SMALL + ['conv_w']}
    for l in reversed(range(n_layers)):
        def emit(g, gb, l=l):
            names = BWD_GROUPS[g]
            ax_g = [BIG_AXIS[n] for n in names]
            dwvs = [_grad_view(gb[n], ax) for n, ax in zip(names, ax_g)]
            theirs = _grads_d2d(dwvs)
            sums = [_sum_halves(dv, th, core, "sum_halves") for dv, th in zip(dwvs, theirs)]
            send_sems, recv_sems, sums, lands, token = _grads_ici_start(sums, ax_g, f"grads_ici_start_{l}_{g}")
            sent.append((l, g, names, ax_g, sums, lands, send_sems, recv_sems))
            return token

        dx, gs = _layer_bwd(dx, mems, saved[l], fulls[l], small, l, None, emit)
        for n in gs:
            gsmall[n][l] = gs[n]
    joined, follows = {}, dx
    for l, g, names, ax_g, sums, lands, send_sems, recv_sems in sent:
        sums, recv = _grads_ici_wait(sums, lands, send_sems, recv_sems, ax_g, follows, f"grads_ici_wait_{l}_{g}")
        for n, r, sm, ax in zip(names, recv, sums, ax_g):
            joined[n] = follows = _sum_chips(r, sm, joined.get(n), l, n_layers, chip, core, ax)
    joined = _join_halves([joined[n] for n in BIG])

    small_full = [jnp.stack(gsmall[n]).reshape(w[n].shape) for n in SMALL]
    conv_w_grad = jnp.stack(gsmall['conv_w'])
    summed = _all_reduce_small(_pack(small_full + [conv_w_grad], 8))
    *small_g, conv_w_g = _unpack(summed, small_full + [conv_w_grad])
    shard = conv_w.shape[-1]
    conv_w_g = lax.dynamic_slice_in_dim(conv_w_g, chip[0] * shard, shard, axis=2)

    out = {}
    for n, g in zip(BIG, joined):
        out[n] = _adamw(w[n], m[n], v[n], g.reshape(w[n].shape), "adamw_big")
    names = SMALL + ['conv_w']
    packed = [_pack([p[n] for n in names], 256) for p in (w, m, v)]
    gpack = _pack(small_g + [conv_w_g], 256)
    res = _adamw(packed[0][None], packed[1][None], packed[2][None], gpack[None], "adamw_small")
    like = [w[n] for n in names]
    unpacked = [_unpack(r[0], like) for r in res]
    for i, n in enumerate(names):
        out[n] = tuple(u[i] for u in unpacked)

    return (loss, dx[None], *[out[n][0] for n in WEIGHTS], *[out[n][1] for n in WEIGHTS],
            *[out[n][2] for n in WEIGHTS], *[out[n][3] for n in WEIGHTS])
```

```python
import functools
import math

import jax
import jax.numpy as jnp
from jax import lax
from jax.experimental import pallas as pl
from jax.experimental.pallas import tpu as pltpu

F32 = jnp.float32
BF16 = jnp.bfloat16
EPS = 1e-6
HEAD_SB = 128
GROUP_GM = 128
CHUNK = 64
HEAD_XA = 256
CONV_TAPS = 3
N_CHIPS = 4
N_DEV = 8
LANES = 128
MIB = 1024 * 1024
VMEM_LIMIT = 48 * MIB
SPLITS = 2

ADAM_LR = 0.001
ADAM_B1 = 0.9
ADAM_B2 = 0.999
ADAM_EPS = 1e-08
ADAM_WD = 0.01
ADAM_STEP = 10

WEIGHTS = ['g_mix_pre', 'w_in', 'g_vnorm', 'w_s', 'b_s', 'g_mem', 'w_mem_kv', 'w_gate', 'b_gate', 'w_br_sb',
           'w_br_gm', 'w_br_xa', 'w_out', 'g_mix_post', 'g_ffn_pre', 'w_up', 'conv_w', 'conv_b', 'w_down',
           'g_ffn_post']
BIG_AXIS = {'w_in': 2, 'w_mem_kv': 1, 'w_gate': 2, 'w_br_sb': 2, 'w_br_gm': 2, 'w_br_xa': 2, 'w_out': 1,
            'w_up': 2, 'w_down': 1}
BIG = list(BIG_AXIS)
FWD_GROUPS = [['w_in'], ['w_mem_kv', 'w_gate'], ['w_br_sb', 'w_br_gm', 'w_br_xa', 'w_out'], ['w_up'], ['w_down']]
BWD_GROUPS = [['w_down', 'w_up'], ['w_out', 'w_br_sb', 'w_br_gm', 'w_br_xa', 'w_gate', 'w_mem_kv'], ['w_in']]
SMALL = ['g_mix_pre', 'g_vnorm', 'w_s', 'b_s', 'g_mem', 'b_gate', 'g_mix_post', 'g_ffn_pre', 'conv_b', 'g_ffn_post']
MESH = pl.DeviceIdType.MESH


def _pcall(body, **kw):
    return pl.pallas_call(body, **kw)


def _params(sem=None, vmem=VMEM_LIMIT):
    return pltpu.CompilerParams(dimension_semantics=sem, vmem_limit_bytes=vmem)


def _tile(n, cands):
    for c in cands:
        if n % c == 0:
            return c
    return n


_GELU_C = math.sqrt(2.0 / math.pi)
_GELU_A = 0.044715


def _gelu(x):
    return 0.5 * x * (1.0 + jnp.tanh(_GELU_C * (x + _GELU_A * (x * x * x))))


def _gelu_and_grad(x):
    x2 = x * x
    t = jnp.tanh(_GELU_C * (x + _GELU_A * (x2 * x)))
    val = 0.5 * x * (1.0 + t)
    grad = 0.5 * (1.0 + t) + 0.5 * x * (1.0 - t * t) * (_GELU_C * (1.0 + 3.0 * _GELU_A * x2))
    return val, grad


def _softplus(z):
    return jnp.maximum(z, 0.0) + jnp.log1p(jnp.exp(-jnp.abs(z)))


def _dot(a, b):
    return jnp.dot(a, b, preferred_element_type=F32)


def _dot_nt(a, b):
    return lax.dot_general(a, b, (((1,), (1,)), ((), ())), preferred_element_type=F32)


def _dot_tn(a, b):
    return lax.dot_general(a, b, (((0,), (0,)), ((), ())), preferred_element_type=F32)


def _split_dot(a, m):
    out = None
    rest = a
    for _ in range(SPLITS):
        piece = rest.astype(BF16)
        rest = rest - piece.astype(F32)
        term = _dot(piece, m)
        out = term if out is None else out + term
    return out


def _mm(a, b, mode, out_dtype, name, tm=None, tn=None, tk=None, after=None):
    if mode == 'nn':
        (m, kc), (kc2, n) = a.shape, b.shape
    elif mode == 'nt':
        (m, kc), (n, kc2) = a.shape, b.shape
    else:
        (kc, m), (kc2, n) = a.shape, b.shape
    assert kc == kc2, (a.shape, b.shape, mode)
    tm = tm or _tile(m, (1024, 512, 256, 128))
    tn = tn or _tile(n, (1024, 512, 256, 128))
    tk = tk or (kc if kc <= 2048 else _tile(kc, (2048, 1536, 1408, 1024, 512)))
    nk = kc // tk
    dot = {'nn': _dot, 'nt': _dot_nt, 'tn': _dot_tn}[mode]
    a_spec = pl.BlockSpec((tk, tm), lambda i, j, k: (k, i)) if mode == 'tn' else pl.BlockSpec((tm, tk), lambda i, j, k: (i, k))
    b_spec = pl.BlockSpec((tn, tk), lambda i, j, k: (j, k)) if mode == 'nt' else pl.BlockSpec((tk, tn), lambda i, j, k: (k, j))

    extra = [] if after is None else [after]
    extra_specs = [pl.BlockSpec(memory_space=pl.ANY)] * len(extra)

    if nk == 1:
        def body(a_ref, b_ref, *rest):
            o_ref = rest[-1]
            o_ref[...] = dot(a_ref[...].astype(BF16), b_ref[...].astype(BF16)).astype(o_ref.dtype)
        scratch = []
    else:
        def body(a_ref, b_ref, *rest):
            o_ref, acc_ref = rest[-2], rest[-1]
            k = pl.program_id(2)
            part = dot(a_ref[...].astype(BF16), b_ref[...].astype(BF16))

            @pl.when(k == 0)
            def _():
                acc_ref[...] = part

            @pl.when(k > 0)
            def _():
                acc_ref[...] += part

            @pl.when(k == nk - 1)
            def _():
                o_ref[...] = acc_ref[...].astype(o_ref.dtype)
        scratch = [pltpu.VMEM((tm, tn), F32)]

    return _pcall(
        body, grid=(m // tm, n // tn, nk), in_specs=[a_spec, b_spec] + extra_specs,
        out_specs=pl.BlockSpec((tm, tn), lambda i, j, k: (i, j)),
        out_shape=jax.ShapeDtypeStruct((m, n), out_dtype), scratch_shapes=scratch, name=name,
        compiler_params=_params(("parallel", "parallel", "arbitrary")))(a, b, *extra)


def _norm_fwd(x, g, res, out_dtype, name, after=None):
    s, d = x.shape
    tr = _tile(s, (256, 128))
    has_res = res is not None
    has_after = after is not None

    def body(*refs):
        x_ref, g_ref = refs[0], refs[1]
        o_ref = refs[-1]
        xv = x_ref[...]
        y = xv * lax.rsqrt(jnp.mean(xv * xv, axis=-1, keepdims=True) + EPS) * g_ref[...]
        if has_res:
            y = y + refs[2][...]
        o_ref[...] = y.astype(o_ref.dtype)

    row = pl.BlockSpec((tr, d), lambda i: (i, 0))
    ins = [x, g] + ([res] if has_res else []) + ([after] if has_after else [])
    return _pcall(
        body, grid=(s // tr,),
        in_specs=[row, pl.BlockSpec((1, d), lambda i: (0, 0))] + ([row] if has_res else [])
        + ([pl.BlockSpec(memory_space=pl.ANY)] if has_after else []),
        out_specs=row, out_shape=jax.ShapeDtypeStruct((s, d), out_dtype), name=name,
        compiler_params=_params(("parallel",)))(*ins)


def _norm_bwd(x, g, douts, dres, out_dtype, name, after=None):
    s, d = x.shape
    tr = _tile(s, (256, 128))
    nd = len(douts)
    has_res = dres is not None
    has_after = after is not None

    def body(*refs):
        x_ref, g_ref = refs[0], refs[1]
        dx_ref, dg_ref = refs[-2], refs[-1]
        dout = refs[2][...].astype(F32)
        for r in refs[3:2 + nd]:
            dout = dout + r[...].astype(F32)
        xv = x_ref[...]
        r = lax.rsqrt(jnp.mean(xv * xv, axis=-1, keepdims=True) + EPS)
        n = xv * r
        dn = dout * g_ref[...]
        dx = r * (dn - n * jnp.mean(dn * n, axis=-1, keepdims=True))
        if has_res:
            dx = dx + refs[2 + nd][...]
        dx_ref[...] = dx.astype(dx_ref.dtype)

        @pl.when(pl.program_id(0) == 0)
        def _():
            dg_ref[...] = jnp.zeros_like(dg_ref)

        dg_ref[...] += jnp.sum(dout * n, axis=0, keepdims=True)

    row = pl.BlockSpec((tr, d), lambda i: (i, 0))
    vec = pl.BlockSpec((1, d), lambda i: (0, 0))
    ins = [x, g] + list(douts) + ([dres] if has_res else []) + ([after] if has_after else [])
    return _pcall(
        body, grid=(s // tr,),
        in_specs=[row, vec] + [row] * (nd + int(has_res)) + ([pl.BlockSpec(memory_space=pl.ANY)] if has_after else []),
        out_specs=[row, vec],
        out_shape=[jax.ShapeDtypeStruct((s, d), out_dtype), jax.ShapeDtypeStruct((1, d), F32)], name=name,
        compiler_params=_params(("arbitrary",)))(*ins)


def _loss_head(y, target):
    s, d = y.shape
    tr = _tile(s, (256, 128))

    def body(y_ref, t_ref, sq_ref, dy_ref):
        e = y_ref[...] - t_ref[...]
        dy_ref[...] = e * (1.0 / d)

        @pl.when(pl.program_id(0) == 0)
        def _():
            sq_ref[...] = jnp.zeros_like(sq_ref)

        sq_ref[...] += jnp.sum(e * e, axis=0, keepdims=True)

    row = pl.BlockSpec((tr, d), lambda i: (i, 0))
    return _pcall(
        body, grid=(s // tr,), in_specs=[row, row], out_specs=[pl.BlockSpec((1, d), lambda i: (0, 0)), row],
        out_shape=[jax.ShapeDtypeStruct((1, d), F32), jax.ShapeDtypeStruct((s, d), F32)], name="loss_head",
        compiler_params=_params(("arbitrary",)))(y, target)


SB_QUERIES = 512


def _sb_sum_matrix(later):
    r = lax.broadcasted_iota(jnp.int32, (HEAD_SB, 2 * HEAD_SB), 0)
    c = lax.broadcasted_iota(jnp.int32, (HEAD_SB, 2 * HEAD_SB), 1)
    tri = jnp.where((r > c) if later else (r < c), 1.0, 0.0)
    return jnp.where(c < HEAD_SB, tri, 1.0).astype(BF16)


def _sb_mask(tq, q0, k0):
    row = lax.broadcasted_iota(jnp.int32, (tq, HEAD_SB), 0)
    col = lax.broadcasted_iota(jnp.int32, (tq, HEAD_SB), 1)
    return (k0 + col) < (q0 + row)


def _sb_fwd(proj, n_heads):
    s = proj.shape[0]
    tq = min(SB_QUERIES, s)
    per = tq // HEAD_SB
    scale = HEAD_SB ** -0.5

    def body(q_ref, k_ref, v_ref, o_ref, acc_ref, c_ref):
        i = pl.program_id(1)
        q = q_ref[...].astype(BF16)
        sums = _sb_sum_matrix(True)
        acc_ref[...] = jnp.zeros_like(acc_ref)
        c_ref[...] = jnp.zeros_like(c_ref)
        last = (i + 1) * per - 1

        def step(jj, masked):
            off = pl.multiple_of((last - jj) * HEAD_SB, HEAD_SB)
            kb = k_ref[pl.ds(off, HEAD_SB), :].astype(BF16)
            vb = v_ref[pl.ds(off, HEAD_SB), :].astype(BF16)
            z = _dot_nt(q, kb) * scale
            sp = _softplus(z)
            if masked:
                mask = _sb_mask(tq, i * tq, off)
                sp = jnp.where(mask, sp, 0.0)
            both = _split_dot(sp, sums)
            c = c_ref[...]
            a = jnp.exp(z - sp - both[:, :HEAD_SB] - c)
            if masked:
                a = jnp.where(mask, a, 0.0)
            acc_ref[...] += _dot(a.astype(BF16), vb)
            c_ref[...] = c + both[:, HEAD_SB:]

        def diagonal(jj, carry):
            step(jj, True)
            return carry

        def below(jj, carry):
            step(jj, False)
            return carry

        lax.fori_loop(0, per, diagonal, 0)
        lax.fori_loop(per, last + 1, below, 0)
        o_ref[...] = acc_ref[...].astype(o_ref.dtype)

    h = n_heads
    blk = pl.BlockSpec((tq, HEAD_SB), lambda hh, i: (i, hh))
    return _pcall(
        body, grid=(h, s // tq),
        in_specs=[blk, pl.BlockSpec((s, HEAD_SB), lambda hh, i: (0, h + hh)),
                  pl.BlockSpec((s, HEAD_SB), lambda hh, i: (0, 2 * h + hh))],
        out_specs=blk, out_shape=jax.ShapeDtypeStruct((s, h * HEAD_SB), BF16),
        scratch_shapes=[pltpu.VMEM((tq, HEAD_SB), F32), pltpu.VMEM((tq, HEAD_SB), F32)],
        name="sb_fwd", compiler_params=_params(("parallel", "arbitrary")))(proj, proj, proj)


def _sb_bwd(proj, do, n_heads):
    s = proj.shape[0]
    tq = min(SB_QUERIES, s)
    per = tq // HEAD_SB
    scale = HEAD_SB ** -0.5

    def body(q_ref, k_ref, v_ref, do_ref, dq_ref, dk_ref, dv_ref, g_ref, beta_ref, run_ref, acc_ref):
        i = pl.program_id(1)

        @pl.when(i == 0)
        def _():
            dk_ref[...] = jnp.zeros_like(dk_ref)
            dv_ref[...] = jnp.zeros_like(dv_ref)

        q = q_ref[...].astype(BF16)
        dob = do_ref[...].astype(BF16)
        last = (i + 1) * per - 1

        run_ref[...] = jnp.zeros_like(run_ref)
        later = _sb_sum_matrix(True)

        def left(jj, masked):
            j = last - jj
            off = pl.multiple_of(j * HEAD_SB, HEAD_SB)
            kb = k_ref[pl.ds(off, HEAD_SB), :].astype(BF16)
            vb = v_ref[pl.ds(off, HEAD_SB), :].astype(BF16)
            z = _dot_nt(q, kb) * scale
            sp = _softplus(z)
            beta = jnp.exp(z - sp)
            if masked:
                mask = _sb_mask(tq, i * tq, off)
                sp = jnp.where(mask, sp, 0.0)
            both = _split_dot(sp, later)
            c = run_ref[...]
            a = beta * jnp.exp(-(both[:, :HEAD_SB] + c))
            if masked:
                a = jnp.where(mask, a, 0.0)
            g_ref[j] = a * _dot_nt(dob, vb)
            beta_ref[j] = beta
            dv_ref[pl.ds(off, HEAD_SB), :] += _dot_tn(a.astype(BF16), dob)
            run_ref[...] = c + both[:, HEAD_SB:]

        def left_diagonal(jj, carry):
            left(jj, True)
            return carry

        def left_below(jj, carry):
            left(jj, False)
            return carry

        lax.fori_loop(0, per, left_diagonal, 0)
        lax.fori_loop(per, last + 1, left_below, 0)

        run_ref[...] = jnp.zeros_like(run_ref)
        acc_ref[...] = jnp.zeros_like(acc_ref)
        earlier = _sb_sum_matrix(False)

        def right(j, masked):
            off = pl.multiple_of(j * HEAD_SB, HEAD_SB)
            kb = k_ref[pl.ds(off, HEAD_SB), :].astype(BF16)
            g = g_ref[j]
            beta = beta_ref[j]
            both = _split_dot(g, earlier)
            p = run_ref[...]
            dz = (g * (1.0 - beta) - beta * (both[:, :HEAD_SB] + p)) * scale
            if masked:
                dz = jnp.where(_sb_mask(tq, i * tq, off), dz, 0.0)
            dzb = dz.astype(BF16)
            dk_ref[pl.ds(off, HEAD_SB), :] += _dot_tn(dzb, q)
            acc_ref[...] += _dot(dzb, kb)
            run_ref[...] = p + both[:, HEAD_SB:]

        def right_below(j, carry):
            right(j, False)
            return carry

        def right_diagonal(j, carry):
            right(j, True)
            return carry

        lax.fori_loop(0, last + 1 - per, right_below, 0)
        lax.fori_loop(last + 1 - per, last + 1, right_diagonal, 0)
        dq_ref[...] = acc_ref[...]

    h = n_heads
    blk = pl.BlockSpec((tq, HEAD_SB), lambda hh, i: (i, hh))
    col_blk = pl.BlockSpec((s, HEAD_SB), lambda hh, i: (0, hh))
    shape = jax.ShapeDtypeStruct((s, h * HEAD_SB), F32)
    nk = s // HEAD_SB
    return _pcall(
        body, grid=(h, s // tq),
        in_specs=[blk, pl.BlockSpec((s, HEAD_SB), lambda hh, i: (0, h + hh)),
                  pl.BlockSpec((s, HEAD_SB), lambda hh, i: (0, 2 * h + hh)), blk],
        out_specs=[blk, col_blk, col_blk], out_shape=[shape, shape, shape],
        scratch_shapes=[pltpu.VMEM((nk, tq, HEAD_SB), F32), pltpu.VMEM((nk, tq, HEAD_SB), F32),
                        pltpu.VMEM((tq, HEAD_SB), F32), pltpu.VMEM((tq, HEAD_SB), F32)],
        name="sb_bwd", compiler_params=_params(("parallel", "arbitrary")))(proj, proj, proj, do)


def _gm_mask():
    t = lax.broadcasted_iota(jnp.int32, (GROUP_GM, GROUP_GM), 0)
    s = lax.broadcasted_iota(jnp.int32, (GROUP_GM, GROUP_GM), 1)
    shift = CHUNK.bit_length() - 1
    return (s >> shift) <= (t >> shift)


def _gm_fwd(proj, g_vnorm, w_s, b_st, u_blk):
    s = proj.shape[0]
    groups = w_s.shape[0]
    w = groups * GROUP_GM

    def body(u_ref, v_ref, gv_ref, ws_ref, bst_ref, o_ref):
        ug = _gelu(u_ref[...])
        vg = _gelu(v_ref[...])
        vn = vg * lax.rsqrt(jnp.mean(vg * vg, axis=-1, keepdims=True) + EPS) * gv_ref[...]
        vnb = vn.astype(BF16)
        mask = _gm_mask()
        for g in range(groups):
            sl = slice(g * GROUP_GM, (g + 1) * GROUP_GM)
            wm = jnp.where(mask, ws_ref[g], 0.0).astype(BF16)
            mixed = _dot(wm, vnb[:, sl]) + bst_ref[:, g:g + 1]
            o_ref[:, sl] = (ug[:, sl] * mixed).astype(o_ref.dtype)

    return _pcall(
        body, grid=(s // GROUP_GM,),
        in_specs=[pl.BlockSpec((GROUP_GM, w), lambda c: (c, u_blk)), pl.BlockSpec((GROUP_GM, w), lambda c: (c, u_blk + 1)),
                  pl.BlockSpec((1, w), lambda c: (0, 0)), pl.BlockSpec((groups, GROUP_GM, GROUP_GM), lambda c: (0, 0, 0)),
                  pl.BlockSpec((GROUP_GM, groups), lambda c: (0, 0))],
        out_specs=pl.BlockSpec((GROUP_GM, w), lambda c: (c, 0)),
        out_shape=jax.ShapeDtypeStruct((s, w), BF16), name="gm_fwd",
        compiler_params=_params(("parallel",)))(proj, proj, g_vnorm, w_s, b_st)


def _gm_bwd(proj, g_vnorm, w_s, b_st, do, u_blk):
    s = proj.shape[0]
    groups = w_s.shape[0]
    w = groups * GROUP_GM

    def body(u_ref, v_ref, gv_ref, ws_ref, bst_ref, do_ref, du_ref, dv_ref, dgv_ref, dws_ref, dbst_ref, dvn_ref):
        @pl.when(pl.program_id(0) == 0)
        def _():
            dgv_ref[...] = jnp.zeros_like(dgv_ref)
            dws_ref[...] = jnp.zeros_like(dws_ref)
            dbst_ref[...] = jnp.zeros_like(dbst_ref)

        ug, ugrad = _gelu_and_grad(u_ref[...])
        vg, vgrad = _gelu_and_grad(v_ref[...])
        r = lax.rsqrt(jnp.mean(vg * vg, axis=-1, keepdims=True) + EPS)
        n = vg * r
        gv = gv_ref[...]
        vnb = (n * gv).astype(BF16)
        dout = do_ref[...]
        mask = _gm_mask()
        for g in range(groups):
            sl = slice(g * GROUP_GM, (g + 1) * GROUP_GM)
            wm = jnp.where(mask, ws_ref[g], 0.0).astype(BF16)
            mixed = _dot(wm, vnb[:, sl]) + bst_ref[:, g:g + 1]
            dmixed = dout[:, sl] * ug[:, sl]
            du_ref[:, sl] = dout[:, sl] * mixed * ugrad[:, sl]
            dbst_ref[:, g:g + 1] += jnp.sum(dmixed, axis=1, keepdims=True)
            dmb = dmixed.astype(BF16)
            dws_ref[g] += jnp.where(mask, _dot_nt(dmb, vnb[:, sl]), 0.0)
            dvn_ref[:, sl] = _dot_tn(wm, dmb)
        dvn = dvn_ref[...]
        dgv_ref[...] += jnp.sum(dvn * n, axis=0, keepdims=True)
        dn = dvn * gv
        dvg = r * (dn - n * jnp.mean(dn * n, axis=-1, keepdims=True))
        dv_ref[...] = dvg * vgrad

    rowb = pl.BlockSpec((GROUP_GM, w), lambda c: (c, 0))
    vec = pl.BlockSpec((1, w), lambda c: (0, 0))
    wsb = pl.BlockSpec((groups, GROUP_GM, GROUP_GM), lambda c: (0, 0, 0))
    bsb = pl.BlockSpec((GROUP_GM, groups), lambda c: (0, 0))
    return _pcall(
        body, grid=(s // GROUP_GM,),
        in_specs=[pl.BlockSpec((GROUP_GM, w), lambda c: (c, u_blk)), pl.BlockSpec((GROUP_GM, w), lambda c: (c, u_blk + 1)),
                  vec, wsb, bsb, rowb],
        out_specs=[rowb, rowb, vec, wsb, bsb],
        out_shape=[jax.ShapeDtypeStruct((s, w), F32), jax.ShapeDtypeStruct((s, w), F32), jax.ShapeDtypeStruct((1, w), F32),
                   jax.ShapeDtypeStruct((groups, GROUP_GM, GROUP_GM), F32), jax.ShapeDtypeStruct((GROUP_GM, groups), F32)],
        scratch_shapes=[pltpu.VMEM((GROUP_GM, w), F32)], name="gm_bwd",
        compiler_params=_params(("arbitrary",)))(proj, proj, g_vnorm, w_s, b_st, do)


def _xa_fwd(proj, mem_kv, q_blk, n_heads):
    s = proj.shape[0]
    nm = mem_kv.shape[0]
    tq = _tile(s, (512, 256, 128))
    scale = HEAD_XA ** -0.5

    def body(q_ref, k_ref, v_ref, o_ref):
        z = _dot_nt(q_ref[...].astype(BF16), k_ref[...].astype(BF16)) * scale
        z = z - jnp.max(z, axis=-1, keepdims=True)
        e = jnp.exp(z)
        p = e / jnp.sum(e, axis=-1, keepdims=True)
        o_ref[...] = _dot(p.astype(BF16), v_ref[...].astype(BF16)).astype(o_ref.dtype)

    h = n_heads
    return _pcall(
        body, grid=(h, s // tq),
        in_specs=[pl.BlockSpec((tq, HEAD_XA), lambda hh, i: (i, q_blk + hh)),
                  pl.BlockSpec((nm, HEAD_XA), lambda hh, i: (0, hh)), pl.BlockSpec((nm, HEAD_XA), lambda hh, i: (0, h + hh))],
        out_specs=pl.BlockSpec((tq, HEAD_XA), lambda hh, i: (i, hh)),
        out_shape=jax.ShapeDtypeStruct((s, h * HEAD_XA), BF16), name="xa_fwd",
        compiler_params=_params(("parallel", "parallel")))(proj, mem_kv, mem_kv)


def _xa_bwd(proj, mem_kv, do, q_blk, n_heads):
    s = proj.shape[0]
    nm = mem_kv.shape[0]
    tq = _tile(s, (512, 256, 128))
    scale = HEAD_XA ** -0.5
    h = n_heads

    def body(q_ref, k_ref, v_ref, do_ref, dq_ref, dk_ref, dv_ref):
        @pl.when(pl.program_id(1) == 0)
        def _():
            dk_ref[...] = jnp.zeros_like(dk_ref)
            dv_ref[...] = jnp.zeros_like(dv_ref)

        qb = q_ref[...].astype(BF16)
        kb = k_ref[...].astype(BF16)
        vb = v_ref[...].astype(BF16)
        dob = do_ref[...].astype(BF16)
        z = _dot_nt(qb, kb) * scale
        z = z - jnp.max(z, axis=-1, keepdims=True)
        e = jnp.exp(z)
        p = e / jnp.sum(e, axis=-1, keepdims=True)
        dp = _dot_nt(dob, vb)
        dz = (p * (dp - jnp.sum(dp * p, axis=-1, keepdims=True)) * scale).astype(BF16)
        dq_ref[...] = _dot(dz, kb)
        dk_ref[...] += _dot_tn(dz, qb)
        dv_ref[...] += _dot_tn(p.astype(BF16), dob)

    qspec = pl.BlockSpec((tq, HEAD_XA), lambda hh, i: (i, hh))
    dk, dv = None, None
    dq, dk, dv = _pcall(
        body, grid=(h, s // tq),
        in_specs=[pl.BlockSpec((tq, HEAD_XA), lambda hh, i: (i, q_blk + hh)),
                  pl.BlockSpec((nm, HEAD_XA), lambda hh, i: (0, hh)), pl.BlockSpec((nm, HEAD_XA), lambda hh, i: (0, h + hh)),
                  qspec],
        out_specs=[qspec, pl.BlockSpec((nm, HEAD_XA), lambda hh, i: (0, hh)), pl.BlockSpec((nm, HEAD_XA), lambda hh, i: (0, hh))],
        out_shape=[jax.ShapeDtypeStruct((s, h * HEAD_XA), F32), jax.ShapeDtypeStruct((nm, h * HEAD_XA), F32),
                   jax.ShapeDtypeStruct((nm, h * HEAD_XA), F32)],
        name="xa_bwd", compiler_params=_params(("parallel", "arbitrary")))(proj, mem_kv, mem_kv, do)
    return dq, dk, dv


def _merge_fwd(zg, b_gate, branches):
    s, d = branches[0].shape
    tr = _tile(s, (128,))

    def body(z0, z1, z2, g0, g1, g2, b0, b1, b2, o_ref):
        acc = None
        for z, g, b in ((z0, g0, b0), (z1, g1, b1), (z2, g2, b2)):
            term = jax.nn.sigmoid(z[...] + g[...]) * b[...]
            acc = term if acc is None else acc + term
        o_ref[...] = acc.astype(o_ref.dtype)

    zs = [pl.BlockSpec((tr, d), functools.partial(lambda i, k: (i, k), k=k)) for k in range(3)]
    gs = [pl.BlockSpec((1, d), functools.partial(lambda i, k: (0, k), k=k)) for k in range(3)]
    row = pl.BlockSpec((tr, d), lambda i: (i, 0))
    return _pcall(
        body, grid=(s // tr,), in_specs=zs + gs + [row] * 3, out_specs=row,
        out_shape=jax.ShapeDtypeStruct((s, d), BF16), name="merge_fwd",
        compiler_params=_params(("parallel",)))(zg, zg, zg, b_gate, b_gate, b_gate, *branches)


def _merge_bwd(zg, b_gate, branches, dmerged):
    s, d = branches[0].shape
    tr = _tile(s, (128,))

    def body(z0, z1, z2, g0, g1, g2, b0, b1, b2, dm_ref, dz_ref, d0, d1, d2, dbg_ref):
        @pl.when(pl.program_id(0) == 0)
        def _():
            dbg_ref[...] = jnp.zeros_like(dbg_ref)

        dm = dm_ref[...]
        for k, (z, g, b, dbr) in enumerate(((z0, g0, b0, d0), (z1, g1, b1, d1), (z2, g2, b2, d2))):
            sg = jax.nn.sigmoid(z[...] + g[...])
            dbr[...] = (dm * sg).astype(dbr.dtype)
            dz = dm * b[...] * sg * (1.0 - sg)
            dz_ref[:, k * d:(k + 1) * d] = dz.astype(dz_ref.dtype)
            dbg_ref[:, k * d:(k + 1) * d] += jnp.sum(dz, axis=0, keepdims=True)

    zs = [pl.BlockSpec((tr, d), functools.partial(lambda i, k: (i, k), k=k)) for k in range(3)]
    gs = [pl.BlockSpec((1, d), functools.partial(lambda i, k: (0, k), k=k)) for k in range(3)]
    row = pl.BlockSpec((tr, d), lambda i: (i, 0))
    outs = _pcall(
        body, grid=(s // tr,), in_specs=zs + gs + [row] * 4,
        out_specs=[pl.BlockSpec((tr, 3 * d), lambda i: (i, 0)), row, row, row, pl.BlockSpec((1, 3 * d), lambda i: (0, 0))],
        out_shape=[jax.ShapeDtypeStruct((s, 3 * d), BF16)] + [jax.ShapeDtypeStruct((s, d), BF16)] * 3
        + [jax.ShapeDtypeStruct((1, 3 * d), F32)],
        name="merge_bwd", compiler_params=_params(("arbitrary",)))(zg, zg, zg, b_gate, b_gate, b_gate, *branches, dmerged)
    return outs[0], list(outs[1:4]), outs[4]


def _shift_down(x, k, row):
    return jnp.where(row >= k, pltpu.roll(x, k, 0), 0.0)


def _shift_up(x, k, row, s):
    return jnp.where(row < s - k, pltpu.roll(x, s - k, 0), 0.0)


def _conv_pre(gate, cw_ref, cb_ref, row):
    conv = cb_ref[...] + cw_ref[CONV_TAPS - 1:CONV_TAPS, :] * gate
    for k in range(1, CONV_TAPS):
        conv = conv + cw_ref[CONV_TAPS - 1 - k:CONV_TAPS - k, :] * _shift_down(gate, k, row)
    return conv


def _cg_fwd(up, conv_w, conv_b):
    s = up.shape[0]
    f = conv_w.shape[1]
    tc = _tile(f, (256, 128))
    nb = f // tc

    def body(g_ref, v_ref, cw_ref, cb_ref, o_ref):
        row = lax.broadcasted_iota(jnp.int32, (s, tc), 0)
        conv = _conv_pre(g_ref[...], cw_ref, cb_ref, row)
        o_ref[...] = (_gelu(conv) * v_ref[...]).astype(o_ref.dtype)

    return _pcall(
        body, grid=(nb,),
        in_specs=[pl.BlockSpec((s, tc), lambda j: (0, j)), pl.BlockSpec((s, tc), lambda j: (0, nb + j)),
                  pl.BlockSpec((CONV_TAPS, tc), lambda j: (0, j)), pl.BlockSpec((1, tc), lambda j: (0, j))],
        out_specs=pl.BlockSpec((s, tc), lambda j: (0, j)),
        out_shape=jax.ShapeDtypeStruct((s, f), BF16), name="cg_fwd",
        compiler_params=_params(("parallel",)))(up, up, conv_w, conv_b)


def _cg_bwd(up, conv_w, conv_b, dact):
    s = up.shape[0]
    f = conv_w.shape[1]
    tc = _tile(f, (256, 128))
    nb = f // tc

    def body(g_ref, v_ref, cw_ref, cb_ref, da_ref, dg_ref, dv_ref, dcw_ref, dcb_ref):
        row = lax.broadcasted_iota(jnp.int32, (s, tc), 0)
        gate = g_ref[...]
        conv = _conv_pre(gate, cw_ref, cb_ref, row)
        gel, ggrad = _gelu_and_grad(conv)
        da = da_ref[...]
        dv_ref[...] = (da * gel).astype(dv_ref.dtype)
        dconv = da * v_ref[...] * ggrad
        dgate = cw_ref[CONV_TAPS - 1:CONV_TAPS, :] * dconv
        dcw_ref[CONV_TAPS - 1:CONV_TAPS, :] = jnp.sum(dconv * gate, axis=0, keepdims=True)
        for k in range(1, CONV_TAPS):
            dgate = dgate + cw_ref[CONV_TAPS - 1 - k:CONV_TAPS - k, :] * _shift_up(dconv, k, row, s)
            dcw_ref[CONV_TAPS - 1 - k:CONV_TAPS - k, :] = jnp.sum(dconv * _shift_down(gate, k, row), axis=0, keepdims=True)
        dg_ref[...] = dgate.astype(dg_ref.dtype)
        dcb_ref[...] = jnp.sum(dconv, axis=0, keepdims=True)

    colb = pl.BlockSpec((s, tc), lambda j: (0, j))
    return _pcall(
        body, grid=(nb,),
        in_specs=[colb, pl.BlockSpec((s, tc), lambda j: (0, nb + j)), pl.BlockSpec((CONV_TAPS, tc), lambda j: (0, j)),
                  pl.BlockSpec((1, tc), lambda j: (0, j)), colb],
        out_specs=[colb, colb, pl.BlockSpec((CONV_TAPS, tc), lambda j: (0, j)), pl.BlockSpec((1, tc), lambda j: (0, j))],
        out_shape=[jax.ShapeDtypeStruct((s, f), BF16), jax.ShapeDtypeStruct((s, f), BF16),
                   jax.ShapeDtypeStruct((CONV_TAPS, f), F32), jax.ShapeDtypeStruct((1, f), F32)],
        name="cg_bwd", compiler_params=_params(("parallel",)))(up, up, conv_w, conv_b, dact)


def _row_tile(rows, cols, elems=256 * 1024):
    want = max(16, elems // cols)
    for c in (512, 256, 128, 64, 32, 16):
        if c <= want and rows % c == 0:
            return c
    return rows


def _sum_halves(dwv, recv, core, name):
    nj, _, a, c = dwv.shape
    tr = _row_tile(a, c, 1024 * 1024)

    def body(core_ref, d_ref, r_ref, o_ref):
        o_ref[0] = (d_ref[0, 0].astype(F32) + r_ref[0].astype(F32)).astype(o_ref.dtype)

    grid_spec = pltpu.PrefetchScalarGridSpec(
        num_scalar_prefetch=1, grid=(nj, a // tr),
        in_specs=[pl.BlockSpec((1, 1, tr, c), lambda j, i, cr: (j, cr[0], i, 0)),
                  pl.BlockSpec((1, tr, c), lambda j, i, cr: (j, i, 0))],
        out_specs=pl.BlockSpec((1, tr, c), lambda j, i, cr: (j, i, 0)))
    return _pcall(body, grid_spec=grid_spec, out_shape=jax.ShapeDtypeStruct((nj, a, c), BF16), name=name,
                  compiler_params=_params(("parallel", "parallel")))(core, dwv, recv)


def _sum_chips(recv, own, joined, layer, n_layers, chip, core, ax):
    _, a, b = recv.shape
    tr = _row_tile(a, b)

    def body(chip_ref, core_ref, r_ref, own_ref, *rest):
        o_ref = rest[-1]
        me = chip_ref[0]
        mine = own_ref[0].astype(F32)
        acc = None
        for k in range(N_CHIPS):
            term = jnp.where(me == k, mine, r_ref[k].astype(F32))
            acc = term if acc is None else acc + term
        o_ref[0, 0] = acc

    own_spec = (pl.BlockSpec((1, tr, b), lambda i, ch, co: (0, i, ch[0])) if ax == 2
                else pl.BlockSpec((1, tr, b), lambda i, ch, co: (ch[0], i, 0)))
    in_specs = [pl.BlockSpec((N_CHIPS, tr, b), lambda i, ch, co: (0, i, 0)), own_spec]
    args = [chip, core, recv, own]
    aliases = {}
    if joined is not None:
        in_specs.append(pl.BlockSpec(memory_space=pl.ANY))
        args.append(joined)
        aliases = {4: 0}
    grid_spec = pltpu.PrefetchScalarGridSpec(
        num_scalar_prefetch=2, grid=(a // tr,), in_specs=in_specs,
        out_specs=pl.BlockSpec((1, 1, tr, b), lambda i, ch, co: (layer, co[0], i, 0)))
    return _pcall(body, grid_spec=grid_spec, out_shape=jax.ShapeDtypeStruct((n_layers, 2, a, b), F32),
                  input_output_aliases=aliases, name="sum_chips", compiler_params=_params(("parallel",)))(*args)


def _place_own(wt, layer, chip, ax, after):
    nl, r, c = wt.shape
    half = r // 2
    tr = _row_tile(half, c)
    nb = half // tr

    def body(chip_ref, w_ref, after_ref, o_ref):
        o_ref[...] = w_ref[...].astype(BF16).reshape(o_ref.shape)

    if ax == 2:
        out_spec = pl.BlockSpec((1, tr, c), lambda h, i, ch: (h, i, ch[0]))
    else:
        out_spec = pl.BlockSpec((1, 1, tr, c), lambda h, i, ch: (ch[0], h, i, 0))
    grid_spec = pltpu.PrefetchScalarGridSpec(
        num_scalar_prefetch=1, grid=(2, nb),
        in_specs=[pl.BlockSpec((1, tr, c), lambda h, i, ch: (layer, h * nb + i, 0)), pl.BlockSpec(memory_space=pl.ANY)],
        out_specs=out_spec)
    return _pcall(body, grid_spec=grid_spec, out_shape=jax.ShapeDtypeStruct(_full_view_shape(wt.shape, ax), BF16),
                  name="place_own", compiler_params=_params(("parallel", "parallel")))(chip, wt, after)


def _adamw(w, m, v, g, name):
    nl, r, c = w.shape
    tr = _row_tile(r, c)
    c1 = 1.0 - ADAM_B1 ** ADAM_STEP
    c2 = 1.0 - ADAM_B2 ** ADAM_STEP

    def body(w_ref, m_ref, v_ref, gin_ref, g_ref, d_ref, nm_ref, nv_ref):
        g = gin_ref[...]
        mm = ADAM_B1 * m_ref[...] + (1.0 - ADAM_B1) * g
        vv = ADAM_B2 * v_ref[...] + (1.0 - ADAM_B2) * (g * g)
        g_ref[...] = g
        nm_ref[...] = mm
        nv_ref[...] = vv
        d_ref[...] = -ADAM_LR * ((mm / c1) / (jnp.sqrt(vv / c2) + ADAM_EPS) + ADAM_WD * w_ref[...])

    blk = pl.BlockSpec((1, tr, c), lambda l, i: (l, i, 0))
    shape = jax.ShapeDtypeStruct((nl, r, c), F32)
    return _pcall(
        body, grid=(nl, r // tr), in_specs=[blk] * 4, out_specs=[blk] * 4, out_shape=[shape] * 4, name=name,
        compiler_params=_params(("parallel", "parallel")))(w, m, v, g)


HBM_SPEC = pl.BlockSpec(memory_space=pltpu.HBM)
COMM = pltpu.CompilerParams(has_side_effects=True)


def _position():
    x, y, c = lax.axis_index("x"), lax.axis_index("y"), lax.axis_index("c")
    chips = [(1 - x, y), (x, 1 - y), (1 - x, 1 - y)]
    return x, y, c, chips


def _remote(src, dst, send_sem, recv_sem, dev):
    return pltpu.make_async_remote_copy(src_ref=src, dst_ref=dst, send_sem=send_sem, recv_sem=recv_sem,
                                        device_id=dev, device_id_type=MESH)


def _full_view_shape(shard_shape, ax):
    _, r, c = shard_shape
    return (2, r // 2, c * N_CHIPS) if ax == 2 else (N_CHIPS, 2, r // 2, c)


def _piece(ref, ax, j, h, cs):
    if ax == 2:
        return ref.at[h, :, pl.ds(pl.multiple_of(j * cs, cs), cs)]
    return ref.at[j, h]


def _chip_block(ref, ax, j, cs):
    if ax == 2:
        return ref.at[:, :, pl.ds(pl.multiple_of(j * cs, cs), cs)]
    return ref.at[j]


SEM_SPEC = pl.BlockSpec(memory_space=pltpu.SEMAPHORE)
ANY_SPEC = pl.BlockSpec(memory_space=pl.ANY)
SPLIT = pltpu.CompilerParams(has_side_effects=pltpu.SideEffectType.DATAFLOW_SIDE_EFFECTING)


def _gather_copies(bufs, axes, shard_cols, send_sems, recv_sems):
    x, y, c, chips = _position()
    me = 2 * x + y
    mine, theirs = [], []
    for t, ax in enumerate(axes):
        own = _piece(bufs[t], ax, me, c, shard_cols[t])
        for p, (px, py) in enumerate(chips):
            k = t * 3 + p
            got = _piece(bufs[t], ax, 2 * px + py, c, shard_cols[t])
            mine.append(_remote(own, own, send_sems.at[k], recv_sems.at[k], (px, py, c)))
            theirs.append(_remote(got, got, send_sems.at[k], recv_sems.at[k], (px, py, c)))
    return mine, theirs


def _gather_ici(views, axes, shard_cols):
    n = len(views)

    def body(*refs):
        mine, theirs = _gather_copies(refs[n:2 * n], axes, shard_cols, *refs[2 * n:])
        for cp in mine:
            cp.start()
        for cp in theirs:
            cp.wait_recv()
        for cp in mine:
            cp.wait_send()

    return _pcall(
        body, in_specs=[HBM_SPEC] * n, out_specs=[HBM_SPEC] * n,
        out_shape=[jax.ShapeDtypeStruct(v.shape, v.dtype) for v in views],
        input_output_aliases={t: t for t in range(n)},
        scratch_shapes=[pltpu.SemaphoreType.DMA((3 * n,)), pltpu.SemaphoreType.DMA((3 * n,))],
        name="gather_ici", compiler_params=COMM)(*views)


def _gather_ici_start(views, axes, shard_cols, after, name):
    n = len(views)

    def body(*refs):
        send_sems, recv_sems = refs[n + 1], refs[n + 2]
        mine, _ = _gather_copies(refs[n + 3:2 * n + 3], axes, shard_cols, send_sems, recv_sems)
        for cp in mine:
            cp.start()
        refs[2 * n + 3][...] = jnp.zeros_like(refs[2 * n + 3])

    outs = _pcall(
        body, in_specs=[HBM_SPEC] * n + [ANY_SPEC],
        out_specs=[SEM_SPEC, SEM_SPEC] + [HBM_SPEC] * n + [pl.BlockSpec(memory_space=pltpu.VMEM)],
        out_shape=[pltpu.SemaphoreType.DMA((3 * n,)), pltpu.SemaphoreType.DMA((3 * n,))]
        + [pltpu.HBM(v.shape, v.dtype) for v in views] + [jax.ShapeDtypeStruct((8, LANES), F32)],
        input_output_aliases={t: 2 + t for t in range(n)},
        name=name, compiler_params=SPLIT)(*[pltpu.with_memory_space_constraint(v, pltpu.HBM) for v in views], after)
    return outs[0], outs[1], list(outs[2:2 + n]), outs[2 + n]


def _gather_ici_wait(views, send_sems, recv_sems, axes, shard_cols, after, name):
    n = len(views)

    def body(*refs):
        mine, theirs = _gather_copies(refs[:n], axes, shard_cols, refs[n], refs[n + 1])
        for cp in mine:
            cp.wait_send()
        for cp in theirs:
            cp.wait_recv()

    return _pcall(
        body, in_specs=[HBM_SPEC] * n + [SEM_SPEC, SEM_SPEC, ANY_SPEC], out_specs=[HBM_SPEC] * n,
        out_shape=[pltpu.HBM(v.shape, v.dtype) for v in views],
        input_output_aliases={t: t for t in range(n)},
        name=name, compiler_params=SPLIT)(*views, send_sems, recv_sems, after)


def _gather_d2d(views, axes, shard_cols):
    n = len(views)

    def body(*refs):
        outs = refs[n:2 * n]
        send_sems, recv_sems = refs[2 * n:]
        x, y, c, chips = _position()
        remote = []
        for t, ax in enumerate(axes):
            for p, (px, py) in enumerate(chips):
                k = t * 3 + p
                blk = _piece(outs[t], ax, 2 * px + py, c, shard_cols[t])
                remote.append(_remote(blk, blk, send_sems.at[k], recv_sems.at[k], (x, y, 1 - c)))
                remote[-1].start()
        for t, ax in enumerate(axes):
            for p, (px, py) in enumerate(chips):
                k = t * 3 + p
                blk = _piece(outs[t], ax, 2 * px + py, 1 - c, shard_cols[t])
                _remote(blk, blk, send_sems.at[k], recv_sems.at[k], (x, y, 1 - c)).wait_recv()
        for cp in remote:
            cp.wait_send()

    return _pcall(
        body, in_specs=[HBM_SPEC] * n, out_specs=[HBM_SPEC] * n,
        out_shape=[jax.ShapeDtypeStruct(v.shape, v.dtype) for v in views],
        input_output_aliases={t: t for t in range(n)},
        scratch_shapes=[pltpu.SemaphoreType.DMA((3 * n,)), pltpu.SemaphoreType.DMA((3 * n,))],
        name="gather_d2d", compiler_params=COMM)(*views)


def _grads_d2d(dwvs):
    n = len(dwvs)

    def body(*refs):
        ins, outs = refs[:n], refs[n:2 * n]
        send_sems, recv_sems = refs[2 * n:]
        x, y, c, _ = _position()
        remote = [_remote(ins[t].at[:, 1 - c], outs[t], send_sems.at[t], recv_sems.at[t], (x, y, 1 - c)) for t in range(n)]
        for cp in remote:
            cp.start()
        for cp in remote:
            cp.wait()

    return _pcall(
        body, in_specs=[HBM_SPEC] * n, out_specs=[HBM_SPEC] * n,
        out_shape=[jax.ShapeDtypeStruct((d.shape[0],) + d.shape[2:], d.dtype) for d in dwvs],
        scratch_shapes=[pltpu.SemaphoreType.DMA((n,)), pltpu.SemaphoreType.DMA((n,))],
        name="grads_d2d", compiler_params=COMM)(*dwvs)


def _grads_recv_shapes(sums, axes):
    out = []
    for sm, ax in zip(sums, axes):
        _, a, c = sm.shape
        out.append(((N_CHIPS, a, c // N_CHIPS if ax == 2 else c), sm.dtype))
    return out


def _grads_copies(ins, outs, axes, send_sems, recv_sems):
    x, y, c, chips = _position()
    me = 2 * x + y

    def block(t, j):
        if axes[t] == 2:
            cs = outs[t].shape[2]
            return ins[t].at[0, :, pl.ds(pl.multiple_of(j * cs, cs), cs)]
        return ins[t].at[j]

    mine, theirs = [], []
    for t in range(len(ins)):
        for p, (px, py) in enumerate(chips):
            k = t * 3 + p
            peer = 2 * px + py
            mine.append(_remote(block(t, peer), outs[t].at[me], send_sems.at[k], recv_sems.at[k], (px, py, c)))
            theirs.append(_remote(block(t, peer), outs[t].at[peer], send_sems.at[k], recv_sems.at[k], (px, py, c)))
    return mine, theirs


def _grads_ici(sums, axes):
    n = len(sums)

    def body(*refs):
        mine, theirs = _grads_copies(refs[:n], refs[n:2 * n], axes, *refs[2 * n:])
        for cp in mine:
            cp.start()
        for cp in theirs:
            cp.wait_recv()
        for cp in mine:
            cp.wait_send()

    return _pcall(
        body, in_specs=[HBM_SPEC] * n, out_specs=[HBM_SPEC] * n,
        out_shape=[jax.ShapeDtypeStruct(sh, dt) for sh, dt in _grads_recv_shapes(sums, axes)],
        scratch_shapes=[pltpu.SemaphoreType.DMA((3 * n,)), pltpu.SemaphoreType.DMA((3 * n,))],
        name="grads_ici", compiler_params=COMM)(*sums)


def _grads_ici_start(sums, axes, name):
    n = len(sums)
    shapes = _grads_recv_shapes(sums, axes)

    def body(*refs):
        send_sems, recv_sems = refs[2 * n], refs[2 * n + 1]
        mine, _ = _grads_copies(refs[2 * n + 2:3 * n + 2], refs[3 * n + 2:4 * n + 2], axes, send_sems, recv_sems)
        for cp in mine:
            cp.start()
        refs[4 * n + 2][...] = jnp.zeros_like(refs[4 * n + 2])

    lands = [pltpu.with_memory_space_constraint(lax.empty(sh, dt), pltpu.HBM) for sh, dt in shapes]
    outs = _pcall(
        body, in_specs=[HBM_SPEC] * (2 * n),
        out_specs=[SEM_SPEC, SEM_SPEC] + [HBM_SPEC] * (2 * n) + [pl.BlockSpec(memory_space=pltpu.VMEM)],
        out_shape=[pltpu.SemaphoreType.DMA((3 * n,)), pltpu.SemaphoreType.DMA((3 * n,))]
        + [pltpu.HBM(sm.shape, sm.dtype) for sm in sums] + [pltpu.HBM(sh, dt) for sh, dt in shapes]
        + [jax.ShapeDtypeStruct((8, LANES), F32)],
        input_output_aliases={t: 2 + t for t in range(2 * n)},
        name=name, compiler_params=SPLIT)(*[pltpu.with_memory_space_constraint(sm, pltpu.HBM) for sm in sums], *lands)
    return outs[0], outs[1], list(outs[2:2 + n]), list(outs[2 + n:2 + 2 * n]), outs[2 + 2 * n]


def _grads_ici_wait(sums, lands, send_sems, recv_sems, axes, after, name):
    n = len(sums)

    def body(*refs):
        mine, theirs = _grads_copies(refs[:n], refs[n:2 * n], axes, refs[2 * n], refs[2 * n + 1])
        for cp in mine:
            cp.wait_send()
        for cp in theirs:
            cp.wait_recv()

    outs = _pcall(
        body, in_specs=[HBM_SPEC] * (2 * n) + [SEM_SPEC, SEM_SPEC, ANY_SPEC], out_specs=[HBM_SPEC] * (2 * n),
        out_shape=[pltpu.HBM(a.shape, a.dtype) for a in list(sums) + list(lands)],
        input_output_aliases={t: t for t in range(2 * n)},
        name=name, compiler_params=SPLIT)(*sums, *lands, send_sems, recv_sems, after)
    return list(outs[:n]), list(outs[n:])


def _join_halves(joined):
    n = len(joined)

    def body(*refs):
        outs = refs[n:2 * n]
        send_sems, recv_sems = refs[2 * n:]
        x, y, c, _ = _position()
        remote = [_remote(outs[t].at[:, c], outs[t].at[:, c], send_sems.at[t], recv_sems.at[t], (x, y, 1 - c)) for t in range(n)]
        for cp in remote:
            cp.start()
        for t in range(n):
            _remote(outs[t].at[:, 1 - c], outs[t].at[:, 1 - c], send_sems.at[t], recv_sems.at[t], (x, y, 1 - c)).wait_recv()
        for cp in remote:
            cp.wait_send()

    return _pcall(
        body, in_specs=[HBM_SPEC] * n, out_specs=[HBM_SPEC] * n,
        out_shape=[jax.ShapeDtypeStruct(j.shape, j.dtype) for j in joined],
        input_output_aliases={t: t for t in range(n)},
        scratch_shapes=[pltpu.SemaphoreType.DMA((n,)), pltpu.SemaphoreType.DMA((n,))],
        name="join_halves", compiler_params=COMM)(*joined)


def _gather_small(shard):
    nl, r, cs = shard.shape

    def body(in_ref, out_ref, send_sems, recv_sems, local_sem):
        x, y, c, chips = _position()

        def cols(j):
            return out_ref.at[:, :, pl.ds(pl.multiple_of(j * cs, cs), cs)]

        me = 2 * x + y
        loc = pltpu.make_async_copy(in_ref, cols(me), local_sem)
        loc.start()
        remote = [_remote(in_ref, cols(me), send_sems.at[p], recv_sems.at[p], (px, py, c)) for p, (px, py) in enumerate(chips)]
        for cp in remote:
            cp.start()
        for p, (px, py) in enumerate(chips):
            _remote(in_ref, cols(2 * px + py), send_sems.at[p], recv_sems.at[p], (px, py, c)).wait_recv()
        for cp in remote:
            cp.wait_send()
        loc.wait()

    return _pcall(
        body, in_specs=[HBM_SPEC], out_specs=HBM_SPEC, out_shape=jax.ShapeDtypeStruct((nl, r, cs * N_CHIPS), shard.dtype),
        scratch_shapes=[pltpu.SemaphoreType.DMA((3,)), pltpu.SemaphoreType.DMA((3,)), pltpu.SemaphoreType.DMA(())],
        name="gather_small", compiler_params=COMM)(shard)


def _all_reduce_small(pack):
    r, c = pack.shape

    def body(in_ref, out_ref, slots, send_sems, recv_sems, local_sem):
        x, y, cc, _ = _position()
        me = 4 * x + 2 * y + cc
        peers = []
        for k in range(1, N_DEV):
            fx, fy, fc = (k >> 2) & 1, (k >> 1) & 1, k & 1
            px, py, pc = x ^ fx, y ^ fy, cc ^ fc
            peers.append((k - 1, (px, py, pc), 4 * px + 2 * py + pc))
        loc = pltpu.make_async_copy(in_ref, slots.at[me], local_sem)
        loc.start()
        copies = [_remote(in_ref, slots.at[me], send_sems.at[k], recv_sems.at[k], dev) for k, dev, _ in peers]
        for cp in copies:
            cp.start()
        for k, dev, idx in peers:
            _remote(in_ref, slots.at[idx], send_sems.at[k], recv_sems.at[k], dev).wait_recv()
        for cp in copies:
            cp.wait_send()
        loc.wait()
        acc = slots[0]
        for k in range(1, N_DEV):
            acc = acc + slots[k]
        out_ref[...] = acc

    vm = pl.BlockSpec(memory_space=pltpu.VMEM)
    return _pcall(
        body, in_specs=[vm], out_specs=vm, out_shape=jax.ShapeDtypeStruct((r, c), F32),
        scratch_shapes=[pltpu.VMEM((N_DEV, r, c), F32), pltpu.SemaphoreType.DMA((N_DEV - 1,)),
                        pltpu.SemaphoreType.DMA((N_DEV - 1,)), pltpu.SemaphoreType.DMA(())],
        name="all_reduce_small", compiler_params=pltpu.CompilerParams(has_side_effects=True, vmem_limit_bytes=VMEM_LIMIT))(pack)


def _dims(d):
    half = d // 2
    return half // HEAD_SB, half // HEAD_XA, 3, (5 * half) // HEAD_XA


def _layer_fwd(x, mem, weight, small, l, after=None):
    h_sb, h_xa, u_blk, q_blk = _dims(x.shape[1])

    def vec(name):
        return small[name][l].reshape(1, -1)

    h1 = _norm_fwd(x, vec('g_mix_pre'), None, BF16, "norm_mix_pre", after)
    proj = _mm(h1, weight('w_in', h1), 'nn', F32, "mm_proj")
    o_sb = _sb_fwd(proj, h_sb)
    b_st = small['b_s'][l].T
    o_gm = _gm_fwd(proj, vec('g_vnorm'), small['w_s'][l], b_st, u_blk)
    memn = _norm_fwd(mem, vec('g_mem'), None, BF16, "norm_mem")
    mem_kv = _mm(memn, weight('w_mem_kv', o_sb), 'nn', F32, "mm_mem_kv")
    o_xa = _xa_fwd(proj, mem_kv, q_blk, h_xa)
    zg = _mm(h1, weight('w_gate', o_sb), 'nn', F32, "mm_gate")
    branches = [_mm(o, weight(wn, zg), 'nn', F32, "mm_branch")
                for o, wn in ((o_sb, 'w_br_sb'), (o_gm, 'w_br_gm'), (o_xa, 'w_br_xa'))]
    merged = _merge_fwd(zg, vec('b_gate'), branches)
    y1 = _mm(merged, weight('w_out', zg), 'nn', F32, "mm_out")
    x1 = _norm_fwd(y1, vec('g_mix_post'), x, F32, "norm_mix_post")
    h2 = _norm_fwd(x1, vec('g_ffn_pre'), None, BF16, "norm_ffn_pre")
    up = _mm(h2, weight('w_up', h2), 'nn', F32, "mm_up")
    act = _cg_fwd(up, weight('conv_w', up), vec('conv_b'))
    y2 = _mm(act, weight('w_down', act), 'nn', F32, "mm_down")
    x2 = _norm_fwd(y2, vec('g_ffn_post'), x1, F32, "norm_ffn_post")
    saved = dict(x0=x, h1=h1, proj=proj, o_sb=o_sb, o_gm=o_gm, o_xa=o_xa, memn=memn, mem_kv=mem_kv, zg=zg,
                 branches=branches, merged=merged, y1=y1, x1=x1, h2=h2, up=up, act=act, y2=y2, b_st=b_st)
    return x2, saved


def _layer_bwd(dx, mem, sv, full, small, l, after, emit):
    h_sb, h_xa, u_blk, q_blk = _dims(dx.shape[1])

    def vec(name):
        return small[name][l].reshape(1, -1)

    gb, gs = {}, {}
    dy2, gs['g_ffn_post'] = _norm_bwd(sv['y2'], vec('g_ffn_post'), [dx], None, BF16, "norm_ffn_post_bwd", after)
    gb['w_down'] = _mm(sv['act'], dy2, 'tn', BF16, "mm_down_dw")
    dact = _mm(dy2, full['w_down'], 'nt', F32, "mm_down_dx")
    dgate, dval, gs['conv_w'], gs['conv_b'] = _cg_bwd(sv['up'], full['conv_w'], vec('conv_b'), dact)
    dup = jnp.concatenate([dgate, dval], axis=1)
    gb['w_up'] = _mm(sv['h2'], dup, 'tn', BF16, "mm_up_dw")
    token = emit(0, gb)
    dh2 = _mm(dup, full['w_up'], 'nt', F32, "mm_up_dx")
    dx1, gs['g_ffn_pre'] = _norm_bwd(sv['x1'], vec('g_ffn_pre'), [dh2], dx, F32, "norm_ffn_pre_bwd", token)
    dy1, gs['g_mix_post'] = _norm_bwd(sv['y1'], vec('g_mix_post'), [dx1], None, BF16, "norm_mix_post_bwd")
    gb['w_out'] = _mm(sv['merged'], dy1, 'tn', BF16, "mm_out_dw")
    dmerged = _mm(dy1, full['w_out'], 'nt', F32, "mm_out_dx")
    dzg, dbr, gs['b_gate'] = _merge_bwd(sv['zg'], vec('b_gate'), sv['branches'], dmerged)
    douts = []
    for o, db, wn in ((sv['o_sb'], dbr[0], 'w_br_sb'), (sv['o_gm'], dbr[1], 'w_br_gm'), (sv['o_xa'], dbr[2], 'w_br_xa')):
        gb[wn] = _mm(o, db, 'tn', BF16, "mm_branch_dw")
        douts.append(_mm(db, full[wn], 'nt', F32, "mm_branch_dx"))
    gb['w_gate'] = _mm(sv['h1'], dzg, 'tn', BF16, "mm_gate_dw")
    dq_xa, dk_xa, dv_xa = _xa_bwd(sv['proj'], sv['mem_kv'], douts[2], q_blk, h_xa)
    dmem_kv = jnp.concatenate([dk_xa, dv_xa], axis=1).astype(BF16)
    gb['w_mem_kv'] = _mm(sv['memn'], dmem_kv, 'tn', BF16, "mm_mem_kv_dw")
    token = emit(1, gb)
    dh1_gate = _mm(dzg, full['w_gate'], 'nt', F32, "mm_gate_dx", after=token)
    dmemn = _mm(dmem_kv, full['w_mem_kv'], 'nt', F32, "mm_mem_kv_dx")
    _, gs['g_mem'] = _norm_bwd(mem, vec('g_mem'), [dmemn], None, BF16, "norm_mem_bwd")
    du, dv, gs['g_vnorm'], gs['w_s'], db_st = _gm_bwd(sv['proj'], vec('g_vnorm'), small['w_s'][l], sv['b_st'], douts[1], u_blk)
    gs['b_s'] = db_st.T
    dq, dk, dvv = _sb_bwd(sv['proj'], douts[0], h_sb)
    dproj = jnp.concatenate([dq, dk, dvv, du, dv, dq_xa], axis=1).astype(BF16)
    gb['w_in'] = _mm(sv['h1'], dproj, 'tn', BF16, "mm_proj_dw")
    token = emit(2, gb)
    dh1_proj = _mm(dproj, full['w_in'], 'nt', F32, "mm_proj_dx")
    dx0, gs['g_mix_pre'] = _norm_bwd(sv['x0'], vec('g_mix_pre'), [dh1_gate, dh1_proj], dx1, F32, "norm_mix_pre_bwd", token)
    return dx0, gs


def _local_step(x, mem, target, full, small):
    n_layers = len(full['w_in'])
    saved = []
    for l in range(n_layers):
        x, sv = _layer_fwd(x, mem, lambda name, follows, l=l: full[name][l], small, l)
        saved.append(sv)
    sq, dx = _loss_head(x, target)
    gbig = {n: [None] * n_layers for n in BIG}
    gsmall = {n: [None] * n_layers for n in SMALL + ['conv_w']}
    for l in reversed(range(n_layers)):
        def emit(g, gb, l=l):
            for n in BWD_GROUPS[g]:
                gbig[n][l] = gb[n]

        dx, gs = _layer_bwd(dx, mem, saved[l], {n: full[n][l] for n in full}, small, l, None, emit)
        for n in gs:
            gsmall[n][l] = gs[n]
    return sq, dx, gbig, gsmall


def _pack(arrays, rows_multiple):
    flat = jnp.concatenate([a.reshape(-1).astype(F32) for a in arrays])
    rows = -(-flat.shape[0] // LANES)
    rows = -(-rows // rows_multiple) * rows_multiple
    return jnp.pad(flat, (0, rows * LANES - flat.shape[0])).reshape(rows, LANES)


def _unpack(pack, like):
    flat = pack.reshape(-1)
    out, off = [], 0
    for a in like:
        out.append(flat[off:off + a.size].reshape(a.shape))
        off += a.size
    return out


def _grad_view(g, ax):
    r, c = g.shape
    return g.reshape(1, 2, r // 2, c) if ax == 2 else g.reshape(N_CHIPS, 2, r // (2 * N_CHIPS), c)


def kernel(x, mem, g_mix_pre, w_in, g_vnorm, w_s, b_s, g_mem, w_mem_kv, w_gate, b_gate, w_br_sb, w_br_gm, w_br_xa, w_out, g_mix_post, g_ffn_pre, w_up, conv_w, conv_b, w_down, g_ffn_post, loss_target, m_g_mix_pre, m_w_in, m_g_vnorm, m_w_s, m_b_s, m_g_mem, m_w_mem_kv, m_w_gate, m_b_gate, m_w_br_sb, m_w_br_gm, m_w_br_xa, m_w_out, m_g_mix_post, m_g_ffn_pre, m_w_up, m_conv_w, m_conv_b, m_w_down, m_g_ffn_post, v_g_mix_pre, v_w_in, v_g_vnorm, v_w_s, v_b_s, v_g_mem, v_w_mem_kv, v_w_gate, v_b_gate, v_w_br_sb, v_w_br_gm, v_w_br_xa, v_w_out, v_g_mix_post, v_g_ffn_pre, v_w_up, v_conv_w, v_conv_b, v_w_down, v_g_ffn_post):
    w = dict(g_mix_pre=g_mix_pre, w_in=w_in, g_vnorm=g_vnorm, w_s=w_s, b_s=b_s, g_mem=g_mem, w_mem_kv=w_mem_kv,
             w_gate=w_gate, b_gate=b_gate, w_br_sb=w_br_sb, w_br_gm=w_br_gm, w_br_xa=w_br_xa, w_out=w_out,
             g_mix_post=g_mix_post, g_ffn_pre=g_ffn_pre, w_up=w_up, conv_w=conv_w, conv_b=conv_b, w_down=w_down,
             g_ffn_post=g_ffn_post)
    m = dict(g_mix_pre=m_g_mix_pre, w_in=m_w_in, g_vnorm=m_g_vnorm, w_s=m_w_s, b_s=m_b_s, g_mem=m_g_mem,
             w_mem_kv=m_w_mem_kv, w_gate=m_w_gate, b_gate=m_b_gate, w_br_sb=m_w_br_sb, w_br_gm=m_w_br_gm,
             w_br_xa=m_w_br_xa, w_out=m_w_out, g_mix_post=m_g_mix_post, g_ffn_pre=m_g_ffn_pre, w_up=m_w_up,
             conv_w=m_conv_w, conv_b=m_conv_b, w_down=m_w_down, g_ffn_post=m_g_ffn_post)
    v = dict(g_mix_pre=v_g_mix_pre, w_in=v_w_in, g_vnorm=v_g_vnorm, w_s=v_w_s, b_s=v_b_s, g_mem=v_g_mem,
             w_mem_kv=v_w_mem_kv, w_gate=v_w_gate, b_gate=v_b_gate, w_br_sb=v_w_br_sb, w_br_gm=v_w_br_gm,
             w_br_xa=v_w_br_xa, w_out=v_w_out, g_mix_post=v_g_mix_post, g_ffn_pre=v_g_ffn_pre, w_up=v_w_up,
             conv_w=v_conv_w, conv_b=v_conv_b, w_down=v_w_down, g_ffn_post=v_g_ffn_post)
    n_layers = w_in.shape[0]
    d = x.shape[-1]
    core = lax.axis_index("c").astype(jnp.int32).reshape(1)
    chip = (2 * lax.axis_index("x") + lax.axis_index("y")).astype(jnp.int32).reshape(1)
    small = {n: w[n] for n in SMALL}
    xs, mems, target = x[0], mem[0], loss_target[0]

    conv_w_full = _gather_small(conv_w)

    def as_full(vw, ax):
        return vw.reshape(-1, vw.shape[-1]) if ax == 1 else vw.reshape(vw.shape[0] * vw.shape[1], vw.shape[2])

    pending, token = [], conv_w_full
    for l in range(n_layers):
        for g, names in enumerate(FWD_GROUPS):
            ax_g = [BIG_AXIS[n] for n in names]
            cols_g = [w[n].shape[2] for n in names]
            views = [_place_own(w[n], l, chip, ax, token) for n, ax in zip(names, ax_g)]
            send_sems, recv_sems, views, token = _gather_ici_start(views, ax_g, cols_g, token, f"gather_ici_start_{l}_{g}")
            pending.append((l, g, names, ax_g, cols_g, views, send_sems, recv_sems))

    fulls, saved = [], []
    for l in range(n_layers):
        ready = {'conv_w': conv_w_full[l]}

        def weight(name, follows, l=l, ready=ready):
            if name not in ready:
                (_, g, names, ax_g, cols_g, views, send_sems, recv_sems), = [p for p in pending if p[0] == l and name in p[2]]
                views = _gather_ici_wait(views, send_sems, recv_sems, ax_g, cols_g, follows, f"gather_ici_wait_{l}_{g}")
                views = _gather_d2d(views, ax_g, cols_g)
                for n, vw, ax in zip(names, views, ax_g):
                    ready[n] = as_full(vw, ax)
            return ready[name]

        xs, sv = _layer_fwd(xs, mems, weight, small, l, token if l == 0 else None)
        fulls.append(ready)
        saved.append(sv)
    sq, dx = _loss_head(xs, target)
    loss = lax.psum(0.5 * jnp.sum(sq) / d, ("x", "y", "c"))

    sent = []
    gsmall = {n: [None] * n_layers for n in SMALL + ['conv_w']}
    for l in reversed(range(n_layers)):
        def emit(g, gb, l=l):
            names = BWD_GROUPS[g]
            ax_g = [BIG_AXIS[n] for n in names]
            dwvs = [_grad_view(gb[n], ax) for n, ax in zip(names, ax_g)]
            theirs = _grads_d2d(dwvs)
            sums = [_sum_halves(dv, th, core, "sum_halves") for dv, th in zip(dwvs, theirs)]
            send_sems, recv_sems, sums, lands, token = _grads_ici_start(sums, ax_g, f"grads_ici_start_{l}_{g}")
            sent.append((l, g, names, ax_g, sums, lands, send_sems, recv_sems))
            return token

        dx, gs = _layer_bwd(dx, mems, saved[l], fulls[l], small, l, None, emit)
        for n in gs:
            gsmall[n][l] = gs[n]
    joined, follows = {}, dx
    for l, g, names, ax_g, sums, lands, send_sems, recv_sems in sent:
        sums, recv = _grads_ici_wait(sums, lands, send_sems, recv_sems, ax_g, follows, f"grads_ici_wait_{l}_{g}")
        for n, r, sm, ax in zip(names, recv, sums, ax_g):
            joined[n] = follows = _sum_chips(r, sm, joined.get(n), l, n_layers, chip, core, ax)
    joined = _join_halves([joined[n] for n in BIG])

    small_full = [jnp.stack(gsmall[n]).reshape(w[n].shape) for n in SMALL]
    conv_w_grad = jnp.stack(gsmall['conv_w'])
    summed = _all_reduce_small(_pack(small_full + [conv_w_grad], 8))
    *small_g, conv_w_g = _unpack(summed, small_full + [conv_w_grad])
    shard = conv_w.shape[-1]
    conv_w_g = lax.dynamic_slice_in_dim(conv_w_g, chip[0] * shard, shard, axis=2)

    out = {}
    for n, g in zip(BIG, joined):
        out[n] = _adamw(w[n], m[n], v[n], g.reshape(w[n].shape), "adamw_big")
    names = SMALL + ['conv_w']
    packed = [_pack([p[n] for n in names], 256) for p in (w, m, v)]
    gpack = _pack(small_g + [conv_w_g], 256)
    res = _adamw(packed[0][None], packed[1][None], packed[2][None], gpack[None], "adamw_small")
    like = [w[n] for n in names]
    unpacked = [_unpack(r[0], like) for r in res]
    for i, n in enumerate(names):
        out[n] = tuple(u[i] for u in unpacked)

    return (loss, dx[None], *[out[n][0] for n in WEIGHTS], *[out[n][1] for n in WEIGHTS],
            *[out[n][2] for n in WEIGHTS], *[out[n][3] for n in WEIGHTS])
```

```python
import functools
import math

import jax
import jax.numpy as jnp
from jax import lax
from jax.experimental import pallas as pl
from jax.experimental.pallas import tpu as pltpu

F32 = jnp.float32
BF16 = jnp.bfloat16
EPS = 1e-6
HEAD_SB = 128
GROUP_GM = 128
CHUNK = 64
HEAD_XA = 256
CONV_TAPS = 3
N_CHIPS = 4
N_DEV = 8
LANES = 128
MIB = 1024 * 1024
VMEM_LIMIT = 48 * MIB
SPLITS = 2

ADAM_LR = 0.001
ADAM_B1 = 0.9
ADAM_B2 = 0.999
ADAM_EPS = 1e-08
ADAM_WD = 0.01
ADAM_STEP = 10

WEIGHTS = ['g_mix_pre', 'w_in', 'g_vnorm', 'w_s', 'b_s', 'g_mem', 'w_mem_kv', 'w_gate', 'b_gate', 'w_br_sb',
           'w_br_gm', 'w_br_xa', 'w_out', 'g_mix_post', 'g_ffn_pre', 'w_up', 'conv_w', 'conv_b', 'w_down',
           'g_ffn_post']
BIG_AXIS = {'w_in': 2, 'w_mem_kv': 1, 'w_gate': 2, 'w_br_sb': 2, 'w_br_gm': 2, 'w_br_xa': 2, 'w_out': 1,
            'w_up': 2, 'w_down': 1}
BIG = list(BIG_AXIS)
FWD_GROUPS = [['w_in'], ['w_mem_kv', 'w_gate'], ['w_br_sb', 'w_br_gm', 'w_br_xa', 'w_out'], ['w_up'], ['w_down']]
BWD_GROUPS = [['w_down', 'w_up'], ['w_out', 'w_br_sb', 'w_br_gm', 'w_br_xa', 'w_gate', 'w_mem_kv'], ['w_in']]
SMALL = ['g_mix_pre', 'g_vnorm', 'w_s', 'b_s', 'g_mem', 'b_gate', 'g_mix_post', 'g_ffn_pre', 'conv_b', 'g_ffn_post']
MESH = pl.DeviceIdType.MESH


def _pcall(body, **kw):
    return pl.pallas_call(body, **kw)


def _params(sem=None, vmem=VMEM_LIMIT):
    return pltpu.CompilerParams(dimension_semantics=sem, vmem_limit_bytes=vmem)


def _tile(n, cands):
    for c in cands:
        if n % c == 0:
            return c
    return n


_GELU_C = math.sqrt(2.0 / math.pi)
_GELU_A = 0.044715


def _gelu(x):
    return 0.5 * x * (1.0 + jnp.tanh(_GELU_C * (x + _GELU_A * (x * x * x))))


def _gelu_and_grad(x):
    x2 = x * x
    t = jnp.tanh(_GELU_C * (x + _GELU_A * (x2 * x)))
    val = 0.5 * x * (1.0 + t)
    grad = 0.5 * (1.0 + t) + 0.5 * x * (1.0 - t * t) * (_GELU_C * (1.0 + 3.0 * _GELU_A * x2))
    return val, grad


def _softplus(z):
    return jnp.maximum(z, 0.0) + jnp.log(1.0 + jnp.exp(-jnp.abs(z)))


def _dot(a, b):
    return jnp.dot(a, b, preferred_element_type=F32)


def _dot_nt(a, b):
    return lax.dot_general(a, b, (((1,), (1,)), ((), ())), preferred_element_type=F32)


def _dot_tn(a, b):
    return lax.dot_general(a, b, (((0,), (0,)), ((), ())), preferred_element_type=F32)


def _split_dot(a, m):
    out = None
    rest = a
    for _ in range(SPLITS):
        piece = rest.astype(BF16)
        rest = rest - piece.astype(F32)
        term = _dot(piece, m)
        out = term if out is None else out + term
    return out


def _mm(a, b, mode, out_dtype, name, tm=None, tn=None, tk=None, after=None):
    if mode == 'nn':
        (m, kc), (kc2, n) = a.shape, b.shape
    elif mode == 'nt':
        (m, kc), (n, kc2) = a.shape, b.shape
    else:
        (kc, m), (kc2, n) = a.shape, b.shape
    assert kc == kc2, (a.shape, b.shape, mode)
    tm = tm or _tile(m, (1024, 512, 256, 128))
    tn = tn or _tile(n, (1024, 512, 256, 128))
    tk = tk or (kc if kc <= 2048 else _tile(kc, (2048, 1536, 1408, 1024, 512)))
    nk = kc // tk
    dot = {'nn': _dot, 'nt': _dot_nt, 'tn': _dot_tn}[mode]
    a_spec = pl.BlockSpec((tk, tm), lambda i, j, k: (k, i)) if mode == 'tn' else pl.BlockSpec((tm, tk), lambda i, j, k: (i, k))
    b_spec = pl.BlockSpec((tn, tk), lambda i, j, k: (j, k)) if mode == 'nt' else pl.BlockSpec((tk, tn), lambda i, j, k: (k, j))

    extra = [] if after is None else [after]
    extra_specs = [pl.BlockSpec(memory_space=pl.ANY)] * len(extra)

    if nk == 1:
        def body(a_ref, b_ref, *rest):
            o_ref = rest[-1]
            o_ref[...] = dot(a_ref[...].astype(BF16), b_ref[...].astype(BF16)).astype(o_ref.dtype)
        scratch = []
    else:
        def body(a_ref, b_ref, *rest):
            o_ref, acc_ref = rest[-2], rest[-1]
            k = pl.program_id(2)
            part = dot(a_ref[...].astype(BF16), b_ref[...].astype(BF16))

            @pl.when(k == 0)
            def _():
                acc_ref[...] = part

            @pl.when(k > 0)
            def _():
                acc_ref[...] += part

            @pl.when(k == nk - 1)
            def _():
                o_ref[...] = acc_ref[...].astype(o_ref.dtype)
        scratch = [pltpu.VMEM((tm, tn), F32)]

    return _pcall(
        body, grid=(m // tm, n // tn, nk), in_specs=[a_spec, b_spec] + extra_specs,
        out_specs=pl.BlockSpec((tm, tn), lambda i, j, k: (i, j)),
        out_shape=jax.ShapeDtypeStruct((m, n), out_dtype), scratch_shapes=scratch, name=name,
        compiler_params=_params(("parallel", "parallel", "arbitrary")))(a, b, *extra)


def _norm_fwd(x, g, res, out_dtype, name, after=None):
    s, d = x.shape
    tr = _tile(s, (256, 128))
    has_res = res is not None
    has_after = after is not None

    def body(*refs):
        x_ref, g_ref = refs[0], refs[1]
        o_ref = refs[-1]
        xv = x_ref[...]
        y = xv * lax.rsqrt(jnp.mean(xv * xv, axis=-1, keepdims=True) + EPS) * g_ref[...]
        if has_res:
            y = y + refs[2][...]
        o_ref[...] = y.astype(o_ref.dtype)

    row = pl.BlockSpec((tr, d), lambda i: (i, 0))
    ins = [x, g] + ([res] if has_res else []) + ([after] if has_after else [])
    return _pcall(
        body, grid=(s // tr,),
        in_specs=[row, pl.BlockSpec((1, d), lambda i: (0, 0))] + ([row] if has_res else [])
        + ([pl.BlockSpec(memory_space=pl.ANY)] if has_after else []),
        out_specs=row, out_shape=jax.ShapeDtypeStruct((s, d), out_dtype), name=name,
        compiler_params=_params(("parallel",)))(*ins)


def _norm_bwd(x, g, douts, dres, out_dtype, name, after=None):
    s, d = x.shape
    tr = _tile(s, (256, 128))
    nd = len(douts)
    has_res = dres is not None
    has_after = after is not None

    def body(*refs):
        x_ref, g_ref = refs[0], refs[1]
        dx_ref, dg_ref = refs[-2], refs[-1]
        dout = refs[2][...].astype(F32)
        for r in refs[3:2 + nd]:
            dout = dout + r[...].astype(F32)
        xv = x_ref[...]
        r = lax.rsqrt(jnp.mean(xv * xv, axis=-1, keepdims=True) + EPS)
        n = xv * r
        dn = dout * g_ref[...]
        dx = r * (dn - n * jnp.mean(dn * n, axis=-1, keepdims=True))
        if has_res:
            dx = dx + refs[2 + nd][...]
        dx_ref[...] = dx.astype(dx_ref.dtype)

        @pl.when(pl.program_id(0) == 0)
        def _():
            dg_ref[...] = jnp.zeros_like(dg_ref)

        dg_ref[...] += jnp.sum(dout * n, axis=0, keepdims=True)

    row = pl.BlockSpec((tr, d), lambda i: (i, 0))
    vec = pl.BlockSpec((1, d), lambda i: (0, 0))
    ins = [x, g] + list(douts) + ([dres] if has_res else []) + ([after] if has_after else [])
    return _pcall(
        body, grid=(s // tr,),
        in_specs=[row, vec] + [row] * (nd + int(has_res)) + ([pl.BlockSpec(memory_space=pl.ANY)] if has_after else []),
        out_specs=[row, vec],
        out_shape=[jax.ShapeDtypeStruct((s, d), out_dtype), jax.ShapeDtypeStruct((1, d), F32)], name=name,
        compiler_params=_params(("arbitrary",)))(*ins)


def _loss_head(y, target):
    s, d = y.shape
    tr = _tile(s, (256, 128))

    def body(y_ref, t_ref, sq_ref, dy_ref):
        e = y_ref[...] - t_ref[...]
        dy_ref[...] = e * (1.0 / d)

        @pl.when(pl.program_id(0) == 0)
        def _():
            sq_ref[...] = jnp.zeros_like(sq_ref)

        sq_ref[...] += jnp.sum(e * e, axis=0, keepdims=True)

    row = pl.BlockSpec((tr, d), lambda i: (i, 0))
    return _pcall(
        body, grid=(s // tr,), in_specs=[row, row], out_specs=[pl.BlockSpec((1, d), lambda i: (0, 0)), row],
        out_shape=[jax.ShapeDtypeStruct((1, d), F32), jax.ShapeDtypeStruct((s, d), F32)], name="loss_head",
        compiler_params=_params(("arbitrary",)))(y, target)


NEVER = -1e30
SB_QUERIES = 512


def _sb_sum_matrix(later):
    r = lax.broadcasted_iota(jnp.int32, (HEAD_SB, 2 * HEAD_SB), 0)
    c = lax.broadcasted_iota(jnp.int32, (HEAD_SB, 2 * HEAD_SB), 1)
    tri = jnp.where((r > c) if later else (r < c), 1.0, 0.0)
    return jnp.where(c < HEAD_SB, tri, 1.0).astype(BF16)


def _sb_mask(tq, q0, k0):
    row = lax.broadcasted_iota(jnp.int32, (tq, HEAD_SB), 0)
    col = lax.broadcasted_iota(jnp.int32, (tq, HEAD_SB), 1)
    return (k0 + col) < (q0 + row)


def _sb_fwd(proj, n_heads):
    s = proj.shape[0]
    tq = min(SB_QUERIES, s)
    per = tq // HEAD_SB
    scale = HEAD_SB ** -0.5

    def body(q_ref, k_ref, v_ref, o_ref, acc_ref, c_ref):
        i = pl.program_id(1)
        q = q_ref[...].astype(BF16)
        sums = _sb_sum_matrix(True)
        acc_ref[...] = jnp.zeros_like(acc_ref)
        c_ref[...] = jnp.zeros_like(c_ref)
        last = (i + 1) * per - 1

        def scores(j, masked):
            off = pl.multiple_of(j * HEAD_SB, HEAD_SB)
            z = _dot_nt(q, k_ref[pl.ds(off, HEAD_SB), :].astype(BF16)) * scale
            sp = _softplus(z)
            logb = z - sp
            if masked:
                mask = _sb_mask(tq, i * tq, off)
                logb = jnp.where(mask, logb, NEVER)
                sp = jnp.where(mask, sp, 0.0)
            return logb, _split_dot(sp, sums)

        def values(j, logb, both):
            off = pl.multiple_of(j * HEAD_SB, HEAD_SB)
            c = c_ref[...]
            a = jnp.exp(logb - both[:, :HEAD_SB] - c)
            acc_ref[...] += _dot(a.astype(BF16), v_ref[pl.ds(off, HEAD_SB), :].astype(BF16))
            c_ref[...] = c + both[:, HEAD_SB:]

        def step(jj, carry, masked):
            j = last - jj
            nxt = scores(j, masked)
            values(jnp.minimum(j + 1, last), *carry)
            return nxt

        idle = (jnp.full((tq, HEAD_SB), NEVER, F32), jnp.zeros((tq, 2 * HEAD_SB), F32))
        carry = lax.fori_loop(0, per, functools.partial(step, masked=True), idle)
        carry = lax.fori_loop(per, last + 1, functools.partial(step, masked=False), carry)
        values(0, *carry)
        o_ref[...] = acc_ref[...].astype(o_ref.dtype)

    h = n_heads
    blk = pl.BlockSpec((tq, HEAD_SB), lambda hh, i: (i, hh))
    return _pcall(
        body, grid=(h, s // tq),
        in_specs=[blk, pl.BlockSpec((s, HEAD_SB), lambda hh, i: (0, h + hh)),
                  pl.BlockSpec((s, HEAD_SB), lambda hh, i: (0, 2 * h + hh))],
        out_specs=blk, out_shape=jax.ShapeDtypeStruct((s, h * HEAD_SB), BF16),
        scratch_shapes=[pltpu.VMEM((tq, HEAD_SB), F32), pltpu.VMEM((tq, HEAD_SB), F32)],
        name="sb_fwd", compiler_params=_params(("parallel", "arbitrary")))(proj, proj, proj)


def _sb_bwd(proj, do, n_heads):
    s = proj.shape[0]
    tq = min(SB_QUERIES, s)
    per = tq // HEAD_SB
    scale = HEAD_SB ** -0.5

    def body(q_ref, k_ref, v_ref, do_ref, dq_ref, dk_ref, dv_ref, g_ref, beta_ref, run_ref, acc_ref):
        i = pl.program_id(1)

        @pl.when(i == 0)
        def _():
            dk_ref[...] = jnp.zeros_like(dk_ref)
            dv_ref[...] = jnp.zeros_like(dv_ref)

        q = q_ref[...].astype(BF16)
        dob = do_ref[...].astype(BF16)
        last = (i + 1) * per - 1

        run_ref[...] = jnp.zeros_like(run_ref)
        later = _sb_sum_matrix(True)

        def left(jj, masked):
            j = last - jj
            off = pl.multiple_of(j * HEAD_SB, HEAD_SB)
            kb = k_ref[pl.ds(off, HEAD_SB), :].astype(BF16)
            vb = v_ref[pl.ds(off, HEAD_SB), :].astype(BF16)
            z = _dot_nt(q, kb) * scale
            sp = _softplus(z)
            beta = jnp.exp(z - sp)
            if masked:
                mask = _sb_mask(tq, i * tq, off)
                sp = jnp.where(mask, sp, 0.0)
            both = _split_dot(sp, later)
            c = run_ref[...]
            a = beta * jnp.exp(-(both[:, :HEAD_SB] + c))
            if masked:
                a = jnp.where(mask, a, 0.0)
            g_ref[j] = a * _dot_nt(dob, vb)
            beta_ref[j] = beta
            dv_ref[pl.ds(off, HEAD_SB), :] += _dot_tn(a.astype(BF16), dob)
            run_ref[...] = c + both[:, HEAD_SB:]

        def left_diagonal(jj, carry):
            left(jj, True)
            return carry

        def left_below(jj, carry):
            left(jj, False)
            return carry

        lax.fori_loop(0, per, left_diagonal, 0)
        lax.fori_loop(per, last + 1, left_below, 0)

        run_ref[...] = jnp.zeros_like(run_ref)
        acc_ref[...] = jnp.zeros_like(acc_ref)
        earlier = _sb_sum_matrix(False)

        def right(j, masked):
            off = pl.multiple_of(j * HEAD_SB, HEAD_SB)
            kb = k_ref[pl.ds(off, HEAD_SB), :].astype(BF16)
            g = g_ref[j]
            beta = beta_ref[j]
            both = _split_dot(g, earlier)
            p = run_ref[...]
            dz = (g * (1.0 - beta) - beta * (both[:, :HEAD_SB] + p)) * scale
            if masked:
                dz = jnp.where(_sb_mask(tq, i * tq, off), dz, 0.0)
            dzb = dz.astype(BF16)
            dk_ref[pl.ds(off, HEAD_SB), :] += _dot_tn(dzb, q)
            acc_ref[...] += _dot(dzb, kb)
            run_ref[...] = p + both[:, HEAD_SB:]

        def right_below(j, carry):
            right(j, False)
            return carry

        def right_diagonal(j, carry):
            right(j, True)
            return carry

        lax.fori_loop(0, last + 1 - per, right_below, 0)
        lax.fori_loop(last + 1 - per, last + 1, right_diagonal, 0)
        dq_ref[...] = acc_ref[...]

    h = n_heads
    blk = pl.BlockSpec((tq, HEAD_SB), lambda hh, i: (i, hh))
    col_blk = pl.BlockSpec((s, HEAD_SB), lambda hh, i: (0, hh))
    shape = jax.ShapeDtypeStruct((s, h * HEAD_SB), F32)
    nk = s // HEAD_SB
    return _pcall(
        body, grid=(h, s // tq),
        in_specs=[blk, pl.BlockSpec((s, HEAD_SB), lambda hh, i: (0, h + hh)),
                  pl.BlockSpec((s, HEAD_SB), lambda hh, i: (0, 2 * h + hh)), blk],
        out_specs=[blk, col_blk, col_blk], out_shape=[shape, shape, shape],
        scratch_shapes=[pltpu.VMEM((nk, tq, HEAD_SB), F32), pltpu.VMEM((nk, tq, HEAD_SB), F32),
                        pltpu.VMEM((tq, HEAD_SB), F32), pltpu.VMEM((tq, HEAD_SB), F32)],
        name="sb_bwd", compiler_params=_params(("parallel", "arbitrary")))(proj, proj, proj, do)


def _gm_mask():
    t = lax.broadcasted_iota(jnp.int32, (GROUP_GM, GROUP_GM), 0)
    s = lax.broadcasted_iota(jnp.int32, (GROUP_GM, GROUP_GM), 1)
    shift = CHUNK.bit_length() - 1
    return (s >> shift) <= (t >> shift)


def _gm_fwd(proj, g_vnorm, w_s, b_st, u_blk):
    s = proj.shape[0]
    groups = w_s.shape[0]
    w = groups * GROUP_GM

    def body(u_ref, v_ref, gv_ref, ws_ref, bst_ref, o_ref):
        ug = _gelu(u_ref[...])
        vg = _gelu(v_ref[...])
        vn = vg * lax.rsqrt(jnp.mean(vg * vg, axis=-1, keepdims=True) + EPS) * gv_ref[...]
        vnb = vn.astype(BF16)
        mask = _gm_mask()
        for g in range(groups):
            sl = slice(g * GROUP_GM, (g + 1) * GROUP_GM)
            wm = jnp.where(mask, ws_ref[g], 0.0).astype(BF16)
            mixed = _dot(wm, vnb[:, sl]) + bst_ref[:, g:g + 1]
            o_ref[:, sl] = (ug[:, sl] * mixed).astype(o_ref.dtype)

    return _pcall(
        body, grid=(s // GROUP_GM,),
        in_specs=[pl.BlockSpec((GROUP_GM, w), lambda c: (c, u_blk)), pl.BlockSpec((GROUP_GM, w), lambda c: (c, u_blk + 1)),
                  pl.BlockSpec((1, w), lambda c: (0, 0)), pl.BlockSpec((groups, GROUP_GM, GROUP_GM), lambda c: (0, 0, 0)),
                  pl.BlockSpec((GROUP_GM, groups), lambda c: (0, 0))],
        out_specs=pl.BlockSpec((GROUP_GM, w), lambda c: (c, 0)),
        out_shape=jax.ShapeDtypeStruct((s, w), BF16), name="gm_fwd",
        compiler_params=_params(("parallel",)))(proj, proj, g_vnorm, w_s, b_st)


def _gm_bwd(proj, g_vnorm, w_s, b_st, do, u_blk):
    s = proj.shape[0]
    groups = w_s.shape[0]
    w = groups * GROUP_GM

    def body(u_ref, v_ref, gv_ref, ws_ref, bst_ref, do_ref, du_ref, dv_ref, dgv_ref, dws_ref, dbst_ref, dvn_ref):
        @pl.when(pl.program_id(0) == 0)
        def _():
            dgv_ref[...] = jnp.zeros_like(dgv_ref)
            dws_ref[...] = jnp.zeros_like(dws_ref)
            dbst_ref[...] = jnp.zeros_like(dbst_ref)

        ug, ugrad = _gelu_and_grad(u_ref[...])
        vg, vgrad = _gelu_and_grad(v_ref[...])
        r = lax.rsqrt(jnp.mean(vg * vg, axis=-1, keepdims=True) + EPS)
        n = vg * r
        gv = gv_ref[...]
        vnb = (n * gv).astype(BF16)
        dout = do_ref[...]
        mask = _gm_mask()
        for g in range(groups):
            sl = slice(g * GROUP_GM, (g + 1) * GROUP_GM)
            wm = jnp.where(mask, ws_ref[g], 0.0).astype(BF16)
            mixed = _dot(wm, vnb[:, sl]) + bst_ref[:, g:g + 1]
            dmixed = dout[:, sl] * ug[:, sl]
            du_ref[:, sl] = dout[:, sl] * mixed * ugrad[:, sl]
            dbst_ref[:, g:g + 1] += jnp.sum(dmixed, axis=1, keepdims=True)
            dmb = dmixed.astype(BF16)
            dws_ref[g] += jnp.where(mask, _dot_nt(dmb, vnb[:, sl]), 0.0)
            dvn_ref[:, sl] = _dot_tn(wm, dmb)
        dvn = dvn_ref[...]
        dgv_ref[...] += jnp.sum(dvn * n, axis=0, keepdims=True)
        dn = dvn * gv
        dvg = r * (dn - n * jnp.mean(dn * n, axis=-1, keepdims=True))
        dv_ref[...] = dvg * vgrad

    rowb = pl.BlockSpec((GROUP_GM, w), lambda c: (c, 0))
    vec = pl.BlockSpec((1, w), lambda c: (0, 0))
    wsb = pl.BlockSpec((groups, GROUP_GM, GROUP_GM), lambda c: (0, 0, 0))
    bsb = pl.BlockSpec((GROUP_GM, groups), lambda c: (0, 0))
    return _pcall(
        body, grid=(s // GROUP_GM,),
        in_specs=[pl.BlockSpec((GROUP_GM, w), lambda c: (c, u_blk)), pl.BlockSpec((GROUP_GM, w), lambda c: (c, u_blk + 1)),
                  vec, wsb, bsb, rowb],
        out_specs=[rowb, rowb, vec, wsb, bsb],
        out_shape=[jax.ShapeDtypeStruct((s, w), F32), jax.ShapeDtypeStruct((s, w), F32), jax.ShapeDtypeStruct((1, w), F32),
                   jax.ShapeDtypeStruct((groups, GROUP_GM, GROUP_GM), F32), jax.ShapeDtypeStruct((GROUP_GM, groups), F32)],
        scratch_shapes=[pltpu.VMEM((GROUP_GM, w), F32)], name="gm_bwd",
        compiler_params=_params(("arbitrary",)))(proj, proj, g_vnorm, w_s, b_st, do)


def _xa_fwd(proj, mem_kv, q_blk, n_heads):
    s = proj.shape[0]
    nm = mem_kv.shape[0]
    tq = _tile(s, (512, 256, 128))
    scale = HEAD_XA ** -0.5

    def body(q_ref, k_ref, v_ref, o_ref):
        z = _dot_nt(q_ref[...].astype(BF16), k_ref[...].astype(BF16)) * scale
        z = z - jnp.max(z, axis=-1, keepdims=True)
        e = jnp.exp(z)
        p = e / jnp.sum(e, axis=-1, keepdims=True)
        o_ref[...] = _dot(p.astype(BF16), v_ref[...].astype(BF16)).astype(o_ref.dtype)

    h = n_heads
    return _pcall(
        body, grid=(h, s // tq),
        in_specs=[pl.BlockSpec((tq, HEAD_XA), lambda hh, i: (i, q_blk + hh)),
                  pl.BlockSpec((nm, HEAD_XA), lambda hh, i: (0, hh)), pl.BlockSpec((nm, HEAD_XA), lambda hh, i: (0, h + hh))],
        out_specs=pl.BlockSpec((tq, HEAD_XA), lambda hh, i: (i, hh)),
        out_shape=jax.ShapeDtypeStruct((s, h * HEAD_XA), BF16), name="xa_fwd",
        compiler_params=_params(("parallel", "parallel")))(proj, mem_kv, mem_kv)


def _xa_bwd(proj, mem_kv, do, q_blk, n_heads):
    s = proj.shape[0]
    nm = mem_kv.shape[0]
    tq = _tile(s, (512, 256, 128))
    scale = HEAD_XA ** -0.5
    h = n_heads

    def body(q_ref, k_ref, v_ref, do_ref, dq_ref, dk_ref, dv_ref):
        @pl.when(pl.program_id(1) == 0)
        def _():
            dk_ref[...] = jnp.zeros_like(dk_ref)
            dv_ref[...] = jnp.zeros_like(dv_ref)

        qb = q_ref[...].astype(BF16)
        kb = k_ref[...].astype(BF16)
        vb = v_ref[...].astype(BF16)
        dob = do_ref[...].astype(BF16)
        z = _dot_nt(qb, kb) * scale
        z = z - jnp.max(z, axis=-1, keepdims=True)
        e = jnp.exp(z)
        p = e / jnp.sum(e, axis=-1, keepdims=True)
        dp = _dot_nt(dob, vb)
        dz = (p * (dp - jnp.sum(dp * p, axis=-1, keepdims=True)) * scale).astype(BF16)
        dq_ref[...] = _dot(dz, kb)
        dk_ref[...] += _dot_tn(dz, qb)
        dv_ref[...] += _dot_tn(p.astype(BF16), dob)

    qspec = pl.BlockSpec((tq, HEAD_XA), lambda hh, i: (i, hh))
    dk, dv = None, None
    dq, dk, dv = _pcall(
        body, grid=(h, s // tq),
        in_specs=[pl.BlockSpec((tq, HEAD_XA), lambda hh, i: (i, q_blk + hh)),
                  pl.BlockSpec((nm, HEAD_XA), lambda hh, i: (0, hh)), pl.BlockSpec((nm, HEAD_XA), lambda hh, i: (0, h + hh)),
                  qspec],
        out_specs=[qspec, pl.BlockSpec((nm, HEAD_XA), lambda hh, i: (0, hh)), pl.BlockSpec((nm, HEAD_XA), lambda hh, i: (0, hh))],
        out_shape=[jax.ShapeDtypeStruct((s, h * HEAD_XA), F32), jax.ShapeDtypeStruct((nm, h * HEAD_XA), F32),
                   jax.ShapeDtypeStruct((nm, h * HEAD_XA), F32)],
        name="xa_bwd", compiler_params=_params(("parallel", "arbitrary")))(proj, mem_kv, mem_kv, do)
    return dq, dk, dv


def _merge_fwd(zg, b_gate, branches):
    s, d = branches[0].shape
    tr = _tile(s, (128,))

    def body(z0, z1, z2, g0, g1, g2, b0, b1, b2, o_ref):
        acc = None
        for z, g, b in ((z0, g0, b0), (z1, g1, b1), (z2, g2, b2)):
            term = jax.nn.sigmoid(z[...] + g[...]) * b[...]
            acc = term if acc is None else acc + term
        o_ref[...] = acc.astype(o_ref.dtype)

    zs = [pl.BlockSpec((tr, d), functools.partial(lambda i, k: (i, k), k=k)) for k in range(3)]
    gs = [pl.BlockSpec((1, d), functools.partial(lambda i, k: (0, k), k=k)) for k in range(3)]
    row = pl.BlockSpec((tr, d), lambda i: (i, 0))
    return _pcall(
        body, grid=(s // tr,), in_specs=zs + gs + [row] * 3, out_specs=row,
        out_shape=jax.ShapeDtypeStruct((s, d), BF16), name="merge_fwd",
        compiler_params=_params(("parallel",)))(zg, zg, zg, b_gate, b_gate, b_gate, *branches)


def _merge_bwd(zg, b_gate, branches, dmerged):
    s, d = branches[0].shape
    tr = _tile(s, (128,))

    def body(z0, z1, z2, g0, g1, g2, b0, b1, b2, dm_ref, dz_ref, d0, d1, d2, dbg_ref):
        @pl.when(pl.program_id(0) == 0)
        def _():
            dbg_ref[...] = jnp.zeros_like(dbg_ref)

        dm = dm_ref[...]
        for k, (z, g, b, dbr) in enumerate(((z0, g0, b0, d0), (z1, g1, b1, d1), (z2, g2, b2, d2))):
            sg = jax.nn.sigmoid(z[...] + g[...])
            dbr[...] = (dm * sg).astype(dbr.dtype)
            dz = dm * b[...] * sg * (1.0 - sg)
            dz_ref[:, k * d:(k + 1) * d] = dz.astype(dz_ref.dtype)
            dbg_ref[:, k * d:(k + 1) * d] += jnp.sum(dz, axis=0, keepdims=True)

    zs = [pl.BlockSpec((tr, d), functools.partial(lambda i, k: (i, k), k=k)) for k in range(3)]
    gs = [pl.BlockSpec((1, d), functools.partial(lambda i, k: (0, k), k=k)) for k in range(3)]
    row = pl.BlockSpec((tr, d), lambda i: (i, 0))
    outs = _pcall(
        body, grid=(s // tr,), in_specs=zs + gs + [row] * 4,
        out_specs=[pl.BlockSpec((tr, 3 * d), lambda i: (i, 0)), row, row, row, pl.BlockSpec((1, 3 * d), lambda i: (0, 0))],
        out_shape=[jax.ShapeDtypeStruct((s, 3 * d), BF16)] + [jax.ShapeDtypeStruct((s, d), BF16)] * 3
        + [jax.ShapeDtypeStruct((1, 3 * d), F32)],
        name="merge_bwd", compiler_params=_params(("arbitrary",)))(zg, zg, zg, b_gate, b_gate, b_gate, *branches, dmerged)
    return outs[0], list(outs[1:4]), outs[4]


def _shift_down(x, k, row):
    return jnp.where(row >= k, pltpu.roll(x, k, 0), 0.0)


def _shift_up(x, k, row, s):
    return jnp.where(row < s - k, pltpu.roll(x, s - k, 0), 0.0)


def _conv_pre(gate, cw_ref, cb_ref, row):
    conv = cb_ref[...] + cw_ref[CONV_TAPS - 1:CONV_TAPS, :] * gate
    for k in range(1, CONV_TAPS):
        conv = conv + cw_ref[CONV_TAPS - 1 - k:CONV_TAPS - k, :] * _shift_down(gate, k, row)
    return conv


def _cg_fwd(up, conv_w, conv_b):
    s = up.shape[0]
    f = conv_w.shape[1]
    tc = _tile(f, (256, 128))
    nb = f // tc

    def body(g_ref, v_ref, cw_ref, cb_ref, o_ref):
        row = lax.broadcasted_iota(jnp.int32, (s, tc), 0)
        conv = _conv_pre(g_ref[...], cw_ref, cb_ref, row)
        o_ref[...] = (_gelu(conv) * v_ref[...]).astype(o_ref.dtype)

    return _pcall(
        body, grid=(nb,),
        in_specs=[pl.BlockSpec((s, tc), lambda j: (0, j)), pl.BlockSpec((s, tc), lambda j: (0, nb + j)),
                  pl.BlockSpec((CONV_TAPS, tc), lambda j: (0, j)), pl.BlockSpec((1, tc), lambda j: (0, j))],
        out_specs=pl.BlockSpec((s, tc), lambda j: (0, j)),
        out_shape=jax.ShapeDtypeStruct((s, f), BF16), name="cg_fwd",
        compiler_params=_params(("parallel",)))(up, up, conv_w, conv_b)


def _cg_bwd(up, conv_w, conv_b, dact):
    s = up.shape[0]
    f = conv_w.shape[1]
    tc = _tile(f, (256, 128))
    nb = f // tc

    def body(g_ref, v_ref, cw_ref, cb_ref, da_ref, dg_ref, dv_ref, dcw_ref, dcb_ref):
        row = lax.broadcasted_iota(jnp.int32, (s, tc), 0)
        gate = g_ref[...]
        conv = _conv_pre(gate, cw_ref, cb_ref, row)
        gel, ggrad = _gelu_and_grad(conv)
        da = da_ref[...]
        dv_ref[...] = (da * gel).astype(dv_ref.dtype)
        dconv = da * v_ref[...] * ggrad
        dgate = cw_ref[CONV_TAPS - 1:CONV_TAPS, :] * dconv
        dcw_ref[CONV_TAPS - 1:CONV_TAPS, :] = jnp.sum(dconv * gate, axis=0, keepdims=True)
        for k in range(1, CONV_TAPS):
            dgate = dgate + cw_ref[CONV_TAPS - 1 - k:CONV_TAPS - k, :] * _shift_up(dconv, k, row, s)
            dcw_ref[CONV_TAPS - 1 - k:CONV_TAPS - k, :] = jnp.sum(dconv * _shift_down(gate, k, row), axis=0, keepdims=True)
        dg_ref[...] = dgate.astype(dg_ref.dtype)
        dcb_ref[...] = jnp.sum(dconv, axis=0, keepdims=True)

    colb = pl.BlockSpec((s, tc), lambda j: (0, j))
    return _pcall(
        body, grid=(nb,),
        in_specs=[colb, pl.BlockSpec((s, tc), lambda j: (0, nb + j)), pl.BlockSpec((CONV_TAPS, tc), lambda j: (0, j)),
                  pl.BlockSpec((1, tc), lambda j: (0, j)), colb],
        out_specs=[colb, colb, pl.BlockSpec((CONV_TAPS, tc), lambda j: (0, j)), pl.BlockSpec((1, tc), lambda j: (0, j))],
        out_shape=[jax.ShapeDtypeStruct((s, f), BF16), jax.ShapeDtypeStruct((s, f), BF16),
                   jax.ShapeDtypeStruct((CONV_TAPS, f), F32), jax.ShapeDtypeStruct((1, f), F32)],
        name="cg_bwd", compiler_params=_params(("parallel",)))(up, up, conv_w, conv_b, dact)


def _row_tile(rows, cols, elems=256 * 1024):
    want = max(16, elems // cols)
    for c in (512, 256, 128, 64, 32, 16):
        if c <= want and rows % c == 0:
            return c
    return rows


def _sum_halves(dwv, recv, core, name):
    nj, _, a, c = dwv.shape
    tr = _row_tile(a, c, 1024 * 1024)

    def body(core_ref, d_ref, r_ref, o_ref):
        o_ref[0] = (d_ref[0, 0].astype(F32) + r_ref[0].astype(F32)).astype(o_ref.dtype)

    grid_spec = pltpu.PrefetchScalarGridSpec(
        num_scalar_prefetch=1, grid=(nj, a // tr),
        in_specs=[pl.BlockSpec((1, 1, tr, c), lambda j, i, cr: (j, cr[0], i, 0)),
                  pl.BlockSpec((1, tr, c), lambda j, i, cr: (j, i, 0))],
        out_specs=pl.BlockSpec((1, tr, c), lambda j, i, cr: (j, i, 0)))
    return _pcall(body, grid_spec=grid_spec, out_shape=jax.ShapeDtypeStruct((nj, a, c), BF16), name=name,
                  compiler_params=_params(("parallel", "parallel")))(core, dwv, recv)


def _sum_chips(recv, own, chip, core, ax):
    _, a, b = recv.shape
    tr = _row_tile(a, b)

    def body(chip_ref, core_ref, r_ref, own_ref, o_ref):
        me = chip_ref[0]
        mine = own_ref[0].astype(F32)
        acc = None
        for k in range(N_CHIPS):
            term = jnp.where(me == k, mine, r_ref[k].astype(F32))
            acc = term if acc is None else acc + term
        o_ref[0] = acc

    own_spec = (pl.BlockSpec((1, tr, b), lambda i, ch, co: (0, i, ch[0])) if ax == 2
                else pl.BlockSpec((1, tr, b), lambda i, ch, co: (ch[0], i, 0)))
    grid_spec = pltpu.PrefetchScalarGridSpec(
        num_scalar_prefetch=2, grid=(a // tr,),
        in_specs=[pl.BlockSpec((N_CHIPS, tr, b), lambda i, ch, co: (0, i, 0)), own_spec],
        out_specs=pl.BlockSpec((1, tr, b), lambda i, ch, co: (co[0], i, 0)))
    return _pcall(body, grid_spec=grid_spec, out_shape=jax.ShapeDtypeStruct((2, a, b), F32),
                  name="sum_chips", compiler_params=_params(("parallel",)))(chip, core, recv, own)


def _place_own(wt, layer, chip, ax, after):
    nl, r, c = wt.shape
    half = r // 2
    tr = _row_tile(half, c)
    nb = half // tr

    def body(chip_ref, w_ref, after_ref, o_ref):
        o_ref[...] = w_ref[...].astype(BF16).reshape(o_ref.shape)

    if ax == 2:
        out_spec = pl.BlockSpec((1, tr, c), lambda h, i, ch: (h, i, ch[0]))
    else:
        out_spec = pl.BlockSpec((1, 1, tr, c), lambda h, i, ch: (ch[0], h, i, 0))
    grid_spec = pltpu.PrefetchScalarGridSpec(
        num_scalar_prefetch=1, grid=(2, nb),
        in_specs=[pl.BlockSpec((1, tr, c), lambda h, i, ch: (layer, h * nb + i, 0)), pl.BlockSpec(memory_space=pl.ANY)],
        out_specs=out_spec)
    return _pcall(body, grid_spec=grid_spec, out_shape=jax.ShapeDtypeStruct(_full_view_shape(wt.shape, ax), BF16),
                  name="place_own", compiler_params=_params(("parallel", "parallel")))(chip, wt, after)


def _adamw(w, m, v, g, layer, prev, name):
    nl, r, c = w.shape
    tr = _row_tile(r, c)
    c1 = 1.0 - ADAM_B1 ** ADAM_STEP
    c2 = 1.0 - ADAM_B2 ** ADAM_STEP

    def body(w_ref, m_ref, v_ref, gin_ref, *rest):
        g_ref, d_ref, nm_ref, nv_ref = rest[-4:]
        g = gin_ref[...]
        mm = ADAM_B1 * m_ref[0] + (1.0 - ADAM_B1) * g
        vv = ADAM_B2 * v_ref[0] + (1.0 - ADAM_B2) * (g * g)
        g_ref[0] = g
        nm_ref[0] = mm
        nv_ref[0] = vv
        d_ref[0] = -ADAM_LR * ((mm / c1) / (jnp.sqrt(vv / c2) + ADAM_EPS) + ADAM_WD * w_ref[0])

    blk = pl.BlockSpec((1, tr, c), lambda i: (layer, i, 0))
    shape = jax.ShapeDtypeStruct((nl, r, c), F32)
    extra = [] if prev is None else list(prev)
    return _pcall(
        body, grid=(r // tr,), in_specs=[blk] * 3 + [pl.BlockSpec((tr, c), lambda i: (i, 0))] + [pl.BlockSpec(memory_space=pl.ANY)] * len(extra),
        out_specs=[blk] * 4, out_shape=[shape] * 4, input_output_aliases={4 + k: k for k in range(len(extra))}, name=name,
        compiler_params=_params(("parallel",)))(w, m, v, g, *extra)


HBM_SPEC = pl.BlockSpec(memory_space=pltpu.HBM)
COMM = pltpu.CompilerParams(has_side_effects=True)


def _position():
    x, y, c = lax.axis_index("x"), lax.axis_index("y"), lax.axis_index("c")
    chips = [(1 - x, y), (x, 1 - y), (1 - x, 1 - y)]
    return x, y, c, chips


def _remote(src, dst, send_sem, recv_sem, dev):
    return pltpu.make_async_remote_copy(src_ref=src, dst_ref=dst, send_sem=send_sem, recv_sem=recv_sem,
                                        device_id=dev, device_id_type=MESH)


def _full_view_shape(shard_shape, ax):
    _, r, c = shard_shape
    return (2, r // 2, c * N_CHIPS) if ax == 2 else (N_CHIPS, 2, r // 2, c)


def _piece(ref, ax, j, h, cs):
    if ax == 2:
        return ref.at[h, :, pl.ds(pl.multiple_of(j * cs, cs), cs)]
    return ref.at[j, h]


def _chip_block(ref, ax, j, cs):
    if ax == 2:
        return ref.at[:, :, pl.ds(pl.multiple_of(j * cs, cs), cs)]
    return ref.at[j]


SEM_SPEC = pl.BlockSpec(memory_space=pltpu.SEMAPHORE)
ANY_SPEC = pl.BlockSpec(memory_space=pl.ANY)
SPLIT = pltpu.CompilerParams(has_side_effects=pltpu.SideEffectType.DATAFLOW_SIDE_EFFECTING)


def _gather_copies(bufs, axes, shard_cols, send_sems, recv_sems):
    x, y, c, chips = _position()
    me = 2 * x + y
    mine, theirs = [], []
    for t, ax in enumerate(axes):
        own = _piece(bufs[t], ax, me, c, shard_cols[t])
        for p, (px, py) in enumerate(chips):
            k = t * 3 + p
            got = _piece(bufs[t], ax, 2 * px + py, c, shard_cols[t])
            mine.append(_remote(own, own, send_sems.at[k], recv_sems.at[k], (px, py, c)))
            theirs.append(_remote(got, got, send_sems.at[k], recv_sems.at[k], (px, py, c)))
    return mine, theirs


def _gather_ici(views, axes, shard_cols):
    n = len(views)

    def body(*refs):
        mine, theirs = _gather_copies(refs[n:2 * n], axes, shard_cols, *refs[2 * n:])
        for cp in mine:
            cp.start()
        for cp in theirs:
            cp.wait_recv()
        for cp in mine:
            cp.wait_send()

    return _pcall(
        body, in_specs=[HBM_SPEC] * n, out_specs=[HBM_SPEC] * n,
        out_shape=[jax.ShapeDtypeStruct(v.shape, v.dtype) for v in views],
        input_output_aliases={t: t for t in range(n)},
        scratch_shapes=[pltpu.SemaphoreType.DMA((3 * n,)), pltpu.SemaphoreType.DMA((3 * n,))],
        name="gather_ici", compiler_params=COMM)(*views)


def _gather_ici_start(views, axes, shard_cols, after, name):
    n = len(views)

    def body(*refs):
        send_sems, recv_sems = refs[n + 1], refs[n + 2]
        mine, _ = _gather_copies(refs[n + 3:2 * n + 3], axes, shard_cols, send_sems, recv_sems)
        for cp in mine:
            cp.start()
        refs[2 * n + 3][...] = jnp.zeros_like(refs[2 * n + 3])

    outs = _pcall(
        body, in_specs=[HBM_SPEC] * n + [ANY_SPEC],
        out_specs=[SEM_SPEC, SEM_SPEC] + [HBM_SPEC] * n + [pl.BlockSpec(memory_space=pltpu.VMEM)],
        out_shape=[pltpu.SemaphoreType.DMA((3 * n,)), pltpu.SemaphoreType.DMA((3 * n,))]
        + [pltpu.HBM(v.shape, v.dtype) for v in views] + [jax.ShapeDtypeStruct((8, LANES), F32)],
        input_output_aliases={t: 2 + t for t in range(n)},
        name=name, compiler_params=SPLIT)(*[pltpu.with_memory_space_constraint(v, pltpu.HBM) for v in views], after)
    return outs[0], outs[1], list(outs[2:2 + n]), outs[2 + n]


def _gather_ici_wait(views, send_sems, recv_sems, axes, shard_cols, after, name):
    n = len(views)

    def body(*refs):
        mine, theirs = _gather_copies(refs[:n], axes, shard_cols, refs[n], refs[n + 1])
        for cp in mine:
            cp.wait_send()
        for cp in theirs:
            cp.wait_recv()

    return _pcall(
        body, in_specs=[HBM_SPEC] * n + [SEM_SPEC, SEM_SPEC, ANY_SPEC], out_specs=[HBM_SPEC] * n,
        out_shape=[pltpu.HBM(v.shape, v.dtype) for v in views],
        input_output_aliases={t: t for t in range(n)},
        name=name, compiler_params=SPLIT)(*views, send_sems, recv_sems, after)


def _gather_d2d(views, axes, shard_cols):
    n = len(views)

    def body(*refs):
        outs = refs[n:2 * n]
        send_sems, recv_sems = refs[2 * n:]
        x, y, c, chips = _position()
        remote = []
        for t, ax in enumerate(axes):
            for p, (px, py) in enumerate(chips):
                k = t * 3 + p
                blk = _piece(outs[t], ax, 2 * px + py, c, shard_cols[t])
                remote.append(_remote(blk, blk, send_sems.at[k], recv_sems.at[k], (x, y, 1 - c)))
                remote[-1].start()
        for t, ax in enumerate(axes):
            for p, (px, py) in enumerate(chips):
                k = t * 3 + p
                blk = _piece(outs[t], ax, 2 * px + py, 1 - c, shard_cols[t])
                _remote(blk, blk, send_sems.at[k], recv_sems.at[k], (x, y, 1 - c)).wait_recv()
        for cp in remote:
            cp.wait_send()

    return _pcall(
        body, in_specs=[HBM_SPEC] * n, out_specs=[HBM_SPEC] * n,
        out_shape=[jax.ShapeDtypeStruct(v.shape, v.dtype) for v in views],
        input_output_aliases={t: t for t in range(n)},
        scratch_shapes=[pltpu.SemaphoreType.DMA((3 * n,)), pltpu.SemaphoreType.DMA((3 * n,))],
        name="gather_d2d", compiler_params=COMM)(*views)


def _grads_d2d(dwvs):
    n = len(dwvs)

    def body(*refs):
        ins, outs = refs[:n], refs[n:2 * n]
        send_sems, recv_sems = refs[2 * n:]
        x, y, c, _ = _position()
        remote = [_remote(ins[t].at[:, 1 - c], outs[t], send_sems.at[t], recv_sems.at[t], (x, y, 1 - c)) for t in range(n)]
        for cp in remote:
            cp.start()
        for cp in remote:
            cp.wait()

    return _pcall(
        body, in_specs=[HBM_SPEC] * n, out_specs=[HBM_SPEC] * n,
        out_shape=[jax.ShapeDtypeStruct((d.shape[0],) + d.shape[2:], d.dtype) for d in dwvs],
        scratch_shapes=[pltpu.SemaphoreType.DMA((n,)), pltpu.SemaphoreType.DMA((n,))],
        name="grads_d2d", compiler_params=COMM)(*dwvs)


def _grads_recv_shapes(sums, axes):
    out = []
    for sm, ax in zip(sums, axes):
        _, a, c = sm.shape
        out.append(((N_CHIPS, a, c // N_CHIPS if ax == 2 else c), sm.dtype))
    return out


def _grads_copies(ins, outs, axes, send_sems, recv_sems):
    x, y, c, chips = _position()
    me = 2 * x + y

    def block(t, j):
        if axes[t] == 2:
            cs = outs[t].shape[2]
            return ins[t].at[0, :, pl.ds(pl.multiple_of(j * cs, cs), cs)]
        return ins[t].at[j]

    mine, theirs = [], []
    for t in range(len(ins)):
        for p, (px, py) in enumerate(chips):
            k = t * 3 + p
            peer = 2 * px + py
            mine.append(_remote(block(t, peer), outs[t].at[me], send_sems.at[k], recv_sems.at[k], (px, py, c)))
            theirs.append(_remote(block(t, peer), outs[t].at[peer], send_sems.at[k], recv_sems.at[k], (px, py, c)))
    return mine, theirs


def _grads_ici(sums, axes):
    n = len(sums)

    def body(*refs):
        mine, theirs = _grads_copies(refs[:n], refs[n:2 * n], axes, *refs[2 * n:])
        for cp in mine:
            cp.start()
        for cp in theirs:
            cp.wait_recv()
        for cp in mine:
            cp.wait_send()

    return _pcall(
        body, in_specs=[HBM_SPEC] * n, out_specs=[HBM_SPEC] * n,
        out_shape=[jax.ShapeDtypeStruct(sh, dt) for sh, dt in _grads_recv_shapes(sums, axes)],
        scratch_shapes=[pltpu.SemaphoreType.DMA((3 * n,)), pltpu.SemaphoreType.DMA((3 * n,))],
        name="grads_ici", compiler_params=COMM)(*sums)


def _grads_ici_start(sums, axes, name):
    n = len(sums)
    shapes = _grads_recv_shapes(sums, axes)

    def body(*refs):
        send_sems, recv_sems = refs[2 * n], refs[2 * n + 1]
        mine, _ = _grads_copies(refs[2 * n + 2:3 * n + 2], refs[3 * n + 2:4 * n + 2], axes, send_sems, recv_sems)
        for cp in mine:
            cp.start()
        refs[4 * n + 2][...] = jnp.zeros_like(refs[4 * n + 2])

    lands = [pltpu.with_memory_space_constraint(lax.empty(sh, dt), pltpu.HBM) for sh, dt in shapes]
    outs = _pcall(
        body, in_specs=[HBM_SPEC] * (2 * n),
        out_specs=[SEM_SPEC, SEM_SPEC] + [HBM_SPEC] * (2 * n) + [pl.BlockSpec(memory_space=pltpu.VMEM)],
        out_shape=[pltpu.SemaphoreType.DMA((3 * n,)), pltpu.SemaphoreType.DMA((3 * n,))]
        + [pltpu.HBM(sm.shape, sm.dtype) for sm in sums] + [pltpu.HBM(sh, dt) for sh, dt in shapes]
        + [jax.ShapeDtypeStruct((8, LANES), F32)],
        input_output_aliases={t: 2 + t for t in range(2 * n)},
        name=name, compiler_params=SPLIT)(*[pltpu.with_memory_space_constraint(sm, pltpu.HBM) for sm in sums], *lands)
    return outs[0], outs[1], list(outs[2:2 + n]), list(outs[2 + n:2 + 2 * n]), outs[2 + 2 * n]


def _grads_ici_wait(sums, lands, send_sems, recv_sems, axes, after, name):
    n = len(sums)

    def body(*refs):
        mine, theirs = _grads_copies(refs[:n], refs[n:2 * n], axes, refs[2 * n], refs[2 * n + 1])
        for cp in mine:
            cp.wait_send()
        for cp in theirs:
            cp.wait_recv()

    outs = _pcall(
        body, in_specs=[HBM_SPEC] * (2 * n) + [SEM_SPEC, SEM_SPEC, ANY_SPEC], out_specs=[HBM_SPEC] * (2 * n),
        out_shape=[pltpu.HBM(a.shape, a.dtype) for a in list(sums) + list(lands)],
        input_output_aliases={t: t for t in range(2 * n)},
        name=name, compiler_params=SPLIT)(*sums, *lands, send_sems, recv_sems, after)
    return list(outs[:n]), list(outs[n:])


def _join_halves(joined):
    n = len(joined)

    def body(*refs):
        outs = refs[n:2 * n]
        send_sems, recv_sems = refs[2 * n:]
        x, y, c, _ = _position()
        remote = [_remote(outs[t].at[c], outs[t].at[c], send_sems.at[t], recv_sems.at[t], (x, y, 1 - c)) for t in range(n)]
        for cp in remote:
            cp.start()
        for t in range(n):
            _remote(outs[t].at[1 - c], outs[t].at[1 - c], send_sems.at[t], recv_sems.at[t], (x, y, 1 - c)).wait_recv()
        for cp in remote:
            cp.wait_send()

    return _pcall(
        body, in_specs=[HBM_SPEC] * n, out_specs=[HBM_SPEC] * n,
        out_shape=[jax.ShapeDtypeStruct(j.shape, j.dtype) for j in joined],
        input_output_aliases={t: t for t in range(n)},
        scratch_shapes=[pltpu.SemaphoreType.DMA((n,)), pltpu.SemaphoreType.DMA((n,))],
        name="join_halves", compiler_params=COMM)(*joined)


def _gather_small(shard):
    nl, r, cs = shard.shape

    def body(in_ref, out_ref, send_sems, recv_sems, local_sem):
        x, y, c, chips = _position()

        def cols(j):
            return out_ref.at[:, :, pl.ds(pl.multiple_of(j * cs, cs), cs)]

        me = 2 * x + y
        loc = pltpu.make_async_copy(in_ref, cols(me), local_sem)
        loc.start()
        remote = [_remote(in_ref, cols(me), send_sems.at[p], recv_sems.at[p], (px, py, c)) for p, (px, py) in enumerate(chips)]
        for cp in remote:
            cp.start()
        for p, (px, py) in enumerate(chips):
            _remote(in_ref, cols(2 * px + py), send_sems.at[p], recv_sems.at[p], (px, py, c)).wait_recv()
        for cp in remote:
            cp.wait_send()
        loc.wait()

    return _pcall(
        body, in_specs=[HBM_SPEC], out_specs=HBM_SPEC, out_shape=jax.ShapeDtypeStruct((nl, r, cs * N_CHIPS), shard.dtype),
        scratch_shapes=[pltpu.SemaphoreType.DMA((3,)), pltpu.SemaphoreType.DMA((3,)), pltpu.SemaphoreType.DMA(())],
        name="gather_small", compiler_params=COMM)(shard)


def _all_reduce_small(pack):
    r, c = pack.shape

    def body(in_ref, out_ref, slots, send_sems, recv_sems, local_sem):
        x, y, cc, _ = _position()
        me = 4 * x + 2 * y + cc
        peers = []
        for k in range(1, N_DEV):
            fx, fy, fc = (k >> 2) & 1, (k >> 1) & 1, k & 1
            px, py, pc = x ^ fx, y ^ fy, cc ^ fc
            peers.append((k - 1, (px, py, pc), 4 * px + 2 * py + pc))
        loc = pltpu.make_async_copy(in_ref, slots.at[me], local_sem)
        loc.start()
        copies = [_remote(in_ref, slots.at[me], send_sems.at[k], recv_sems.at[k], dev) for k, dev, _ in peers]
        for cp in copies:
            cp.start()
        for k, dev, idx in peers:
            _remote(in_ref, slots.at[idx], send_sems.at[k], recv_sems.at[k], dev).wait_recv()
        for cp in copies:
            cp.wait_send()
        loc.wait()
        acc = slots[0]
        for k in range(1, N_DEV):
            acc = acc + slots[k]
        out_ref[...] = acc

    vm = pl.BlockSpec(memory_space=pltpu.VMEM)
    return _pcall(
        body, in_specs=[vm], out_specs=vm, out_shape=jax.ShapeDtypeStruct((r, c), F32),
        scratch_shapes=[pltpu.VMEM((N_DEV, r, c), F32), pltpu.SemaphoreType.DMA((N_DEV - 1,)),
                        pltpu.SemaphoreType.DMA((N_DEV - 1,)), pltpu.SemaphoreType.DMA(())],
        name="all_reduce_small", compiler_params=pltpu.CompilerParams(has_side_effects=True, vmem_limit_bytes=VMEM_LIMIT))(pack)


def _dims(d):
    half = d // 2
    return half // HEAD_SB, half // HEAD_XA, 3, (5 * half) // HEAD_XA


def _layer_fwd(x, mem, weight, small, l, after=None):
    h_sb, h_xa, u_blk, q_blk = _dims(x.shape[1])

    def vec(name):
        return small[name][l].reshape(1, -1)

    h1 = _norm_fwd(x, vec('g_mix_pre'), None, BF16, "norm_mix_pre", after)
    proj = _mm(h1, weight('w_in', h1), 'nn', F32, "mm_proj")
    o_sb = _sb_fwd(proj, h_sb)
    b_st = small['b_s'][l].T
    o_gm = _gm_fwd(proj, vec('g_vnorm'), small['w_s'][l], b_st, u_blk)
    memn = _norm_fwd(mem, vec('g_mem'), None, BF16, "norm_mem")
    mem_kv = _mm(memn, weight('w_mem_kv', o_sb), 'nn', F32, "mm_mem_kv")
    o_xa = _xa_fwd(proj, mem_kv, q_blk, h_xa)
    zg = _mm(h1, weight('w_gate', o_sb), 'nn', F32, "mm_gate")
    branches = [_mm(o, weight(wn, zg), 'nn', F32, "mm_branch")
                for o, wn in ((o_sb, 'w_br_sb'), (o_gm, 'w_br_gm'), (o_xa, 'w_br_xa'))]
    merged = _merge_fwd(zg, vec('b_gate'), branches)
    y1 = _mm(merged, weight('w_out', zg), 'nn', F32, "mm_out")
    x1 = _norm_fwd(y1, vec('g_mix_post'), x, F32, "norm_mix_post")
    h2 = _norm_fwd(x1, vec('g_ffn_pre'), None, BF16, "norm_ffn_pre")
    up = _mm(h2, weight('w_up', h2), 'nn', F32, "mm_up")
    act = _cg_fwd(up, weight('conv_w', up), vec('conv_b'))
    y2 = _mm(act, weight('w_down', act), 'nn', F32, "mm_down")
    x2 = _norm_fwd(y2, vec('g_ffn_post'), x1, F32, "norm_ffn_post")
    saved = dict(x0=x, h1=h1, proj=proj, o_sb=o_sb, o_gm=o_gm, o_xa=o_xa, memn=memn, mem_kv=mem_kv, zg=zg,
                 branches=branches, merged=merged, y1=y1, x1=x1, h2=h2, up=up, act=act, y2=y2, b_st=b_st)
    return x2, saved


def _layer_bwd(dx, mem, sv, full, small, l, after, emit):
    h_sb, h_xa, u_blk, q_blk = _dims(dx.shape[1])

    def vec(name):
        return small[name][l].reshape(1, -1)

    gb, gs = {}, {}
    dy2, gs['g_ffn_post'] = _norm_bwd(sv['y2'], vec('g_ffn_post'), [dx], None, BF16, "norm_ffn_post_bwd", after)
    gb['w_down'] = _mm(sv['act'], dy2, 'tn', BF16, "mm_down_dw")
    dact = _mm(dy2, full['w_down'], 'nt', F32, "mm_down_dx")
    dgate, dval, gs['conv_w'], gs['conv_b'] = _cg_bwd(sv['up'], full['conv_w'], vec('conv_b'), dact)
    dup = jnp.concatenate([dgate, dval], axis=1)
    gb['w_up'] = _mm(sv['h2'], dup, 'tn', BF16, "mm_up_dw")
    token = emit(0, gb)
    dh2 = _mm(dup, full['w_up'], 'nt', F32, "mm_up_dx")
    dx1, gs['g_ffn_pre'] = _norm_bwd(sv['x1'], vec('g_ffn_pre'), [dh2], dx, F32, "norm_ffn_pre_bwd", token)
    dy1, gs['g_mix_post'] = _norm_bwd(sv['y1'], vec('g_mix_post'), [dx1], None, BF16, "norm_mix_post_bwd")
    gb['w_out'] = _mm(sv['merged'], dy1, 'tn', BF16, "mm_out_dw")
    dmerged = _mm(dy1, full['w_out'], 'nt', F32, "mm_out_dx")
    dzg, dbr, gs['b_gate'] = _merge_bwd(sv['zg'], vec('b_gate'), sv['branches'], dmerged)
    douts = []
    for o, db, wn in ((sv['o_sb'], dbr[0], 'w_br_sb'), (sv['o_gm'], dbr[1], 'w_br_gm'), (sv['o_xa'], dbr[2], 'w_br_xa')):
        gb[wn] = _mm(o, db, 'tn', BF16, "mm_branch_dw")
        douts.append(_mm(db, full[wn], 'nt', F32, "mm_branch_dx"))
    gb['w_gate'] = _mm(sv['h1'], dzg, 'tn', BF16, "mm_gate_dw")
    dq_xa, dk_xa, dv_xa = _xa_bwd(sv['proj'], sv['mem_kv'], douts[2], q_blk, h_xa)
    dmem_kv = jnp.concatenate([dk_xa, dv_xa], axis=1).astype(BF16)
    gb['w_mem_kv'] = _mm(sv['memn'], dmem_kv, 'tn', BF16, "mm_mem_kv_dw")
    token = emit(1, gb)
    dh1_gate = _mm(dzg, full['w_gate'], 'nt', F32, "mm_gate_dx", after=token)
    dmemn = _mm(dmem_kv, full['w_mem_kv'], 'nt', F32, "mm_mem_kv_dx")
    _, gs['g_mem'] = _norm_bwd(mem, vec('g_mem'), [dmemn], None, BF16, "norm_mem_bwd")
    du, dv, gs['g_vnorm'], gs['w_s'], db_st = _gm_bwd(sv['proj'], vec('g_vnorm'), small['w_s'][l], sv['b_st'], douts[1], u_blk)
    gs['b_s'] = db_st.T
    dq, dk, dvv = _sb_bwd(sv['proj'], douts[0], h_sb)
    dproj = jnp.concatenate([dq, dk, dvv, du, dv, dq_xa], axis=1).astype(BF16)
    gb['w_in'] = _mm(sv['h1'], dproj, 'tn', BF16, "mm_proj_dw")
    token = emit(2, gb)
    dh1_proj = _mm(dproj, full['w_in'], 'nt', F32, "mm_proj_dx")
    dx0, gs['g_mix_pre'] = _norm_bwd(sv['x0'], vec('g_mix_pre'), [dh1_gate, dh1_proj], dx1, F32, "norm_mix_pre_bwd", token)
    return dx0, gs


def _local_step(x, mem, target, full, small):
    n_layers = len(full['w_in'])
    saved = []
    for l in range(n_layers):
        x, sv = _layer_fwd(x, mem, lambda name, follows, l=l: full[name][l], small, l)
        saved.append(sv)
    sq, dx = _loss_head(x, target)
    gbig = {n: [None] * n_layers for n in BIG}
    gsmall = {n: [None] * n_layers for n in SMALL + ['conv_w']}
    for l in reversed(range(n_layers)):
        def emit(g, gb, l=l):
            for n in BWD_GROUPS[g]:
                gbig[n][l] = gb[n]

        dx, gs = _layer_bwd(dx, mem, saved[l], {n: full[n][l] for n in full}, small, l, None, emit)
        for n in gs:
            gsmall[n][l] = gs[n]
    return sq, dx, gbig, gsmall


def _pack(arrays, rows_multiple):
    flat = jnp.concatenate([a.reshape(-1).astype(F32) for a in arrays])
    rows = -(-flat.shape[0] // LANES)
    rows = -(-rows // rows_multiple) * rows_multiple
    return jnp.pad(flat, (0, rows * LANES - flat.shape[0])).reshape(rows, LANES)


def _unpack(pack, like):
    flat = pack.reshape(-1)
    out, off = [], 0
    for a in like:
        out.append(flat[off:off + a.size].reshape(a.shape))
        off += a.size
    return out


def _grad_view(g, ax):
    r, c = g.shape
    return g.reshape(1, 2, r // 2, c) if ax == 2 else g.reshape(N_CHIPS, 2, r // (2 * N_CHIPS), c)


def kernel(x, mem, g_mix_pre, w_in, g_vnorm, w_s, b_s, g_mem, w_mem_kv, w_gate, b_gate, w_br_sb, w_br_gm, w_br_xa, w_out, g_mix_post, g_ffn_pre, w_up, conv_w, conv_b, w_down, g_ffn_post, loss_target, m_g_mix_pre, m_w_in, m_g_vnorm, m_w_s, m_b_s, m_g_mem, m_w_mem_kv, m_w_gate, m_b_gate, m_w_br_sb, m_w_br_gm, m_w_br_xa, m_w_out, m_g_mix_post, m_g_ffn_pre, m_w_up, m_conv_w, m_conv_b, m_w_down, m_g_ffn_post, v_g_mix_pre, v_w_in, v_g_vnorm, v_w_s, v_b_s, v_g_mem, v_w_mem_kv, v_w_gate, v_b_gate, v_w_br_sb, v_w_br_gm, v_w_br_xa, v_w_out, v_g_mix_post, v_g_ffn_pre, v_w_up, v_conv_w, v_conv_b, v_w_down, v_g_ffn_post):
    w = dict(g_mix_pre=g_mix_pre, w_in=w_in, g_vnorm=g_vnorm, w_s=w_s, b_s=b_s, g_mem=g_mem, w_mem_kv=w_mem_kv,
             w_gate=w_gate, b_gate=b_gate, w_br_sb=w_br_sb, w_br_gm=w_br_gm, w_br_xa=w_br_xa, w_out=w_out,
             g_mix_post=g_mix_post, g_ffn_pre=g_ffn_pre, w_up=w_up, conv_w=conv_w, conv_b=conv_b, w_down=w_down,
             g_ffn_post=g_ffn_post)
    m = dict(g_mix_pre=m_g_mix_pre, w_in=m_w_in, g_vnorm=m_g_vnorm, w_s=m_w_s, b_s=m_b_s, g_mem=m_g_mem,
             w_mem_kv=m_w_mem_kv, w_gate=m_w_gate, b_gate=m_b_gate, w_br_sb=m_w_br_sb, w_br_gm=m_w_br_gm,
             w_br_xa=m_w_br_xa, w_out=m_w_out, g_mix_post=m_g_mix_post, g_ffn_pre=m_g_ffn_pre, w_up=m_w_up,
             conv_w=m_conv_w, conv_b=m_conv_b, w_down=m_w_down, g_ffn_post=m_g_ffn_post)
    v = dict(g_mix_pre=v_g_mix_pre, w_in=v_w_in, g_vnorm=v_g_vnorm, w_s=v_w_s, b_s=v_b_s, g_mem=v_g_mem,
             w_mem_kv=v_w_mem_kv, w_gate=v_w_gate, b_gate=v_b_gate, w_br_sb=v_w_br_sb, w_br_gm=v_w_br_gm,
             w_br_xa=v_w_br_xa, w_out=v_w_out, g_mix_post=v_g_mix_post, g_ffn_pre=v_g_ffn_pre, w_up=v_w_up,
             conv_w=v_conv_w, conv_b=v_conv_b, w_down=v_w_down, g_ffn_post=v_g_ffn_post)
    n_layers = w_in.shape[0]
    d = x.shape[-1]
    core = lax.axis_index("c").astype(jnp.int32).reshape(1)
    chip = (2 * lax.axis_index("x") + lax.axis_index("y")).astype(jnp.int32).reshape(1)
    small = {n: w[n] for n in SMALL}
    xs, mems, target = x[0], mem[0], loss_target[0]

    conv_w_full = _gather_small(conv_w)

    def as_full(vw, ax):
        return vw.reshape(-1, vw.shape[-1]) if ax == 1 else vw.reshape(vw.shape[0] * vw.shape[1], vw.shape[2])

    pending, token = [], conv_w_full
    for l in range(n_layers):
        for g, names in enumerate(FWD_GROUPS):
            ax_g = [BIG_AXIS[n] for n in names]
            cols_g = [w[n].shape[2] for n in names]
            views = [_place_own(w[n], l, chip, ax, token) for n, ax in zip(names, ax_g)]
            send_sems, recv_sems, views, token = _gather_ici_start(views, ax_g, cols_g, token, f"gather_ici_start_{l}_{g}")
            pending.append((l, g, names, ax_g, cols_g, views, send_sems, recv_sems))

    fulls, saved = [], []
    for l in range(n_layers):
        ready = {'conv_w': conv_w_full[l]}

        def weight(name, follows, l=l, ready=ready):
            if name not in ready:
                (_, g, names, ax_g, cols_g, views, send_sems, recv_sems), = [p for p in pending if p[0] == l and name in p[2]]
                views = _gather_ici_wait(views, send_sems, recv_sems, ax_g, cols_g, follows, f"gather_ici_wait_{l}_{g}")
                views = _gather_d2d(views, ax_g, cols_g)
                for n, vw, ax in zip(names, views, ax_g):
                    ready[n] = as_full(vw, ax)
            return ready[name]

        xs, sv = _layer_fwd(xs, mems, weight, small, l, token if l == 0 else None)
        fulls.append(ready)
        saved.append(sv)
    sq, dx = _loss_head(xs, target)
    loss = lax.psum(0.5 * jnp.sum(sq) / d, ("x", "y", "c"))

    sent = []
    gsmall = {n: [None] * n_layers for n in SMALL + ['conv_w']}
    for l in reversed(range(n_layers)):
        def emit(g, gb, l=l):
            names = BWD_GROUPS[g]
            ax_g = [BIG_AXIS[n] for n in names]
            dwvs = [_grad_view(gb[n], ax) for n, ax in zip(names, ax_g)]
            theirs = _grads_d2d(dwvs)
            sums = [_sum_halves(dv, th, core, "sum_halves") for dv, th in zip(dwvs, theirs)]
            send_sems, recv_sems, sums, lands, token = _grads_ici_start(sums, ax_g, f"grads_ici_start_{l}_{g}")
            sent.append((l, g, names, ax_g, sums, lands, send_sems, recv_sems))
            return token

        dx, gs = _layer_bwd(dx, mems, saved[l], fulls[l], small, l, None, emit)
        for n in gs:
            gsmall[n][l] = gs[n]
    def small_update():
        small_full = [jnp.stack(gsmall[n]).reshape(w[n].shape) for n in SMALL]
        conv_w_grad = jnp.stack(gsmall['conv_w'])
        summed = _all_reduce_small(_pack(small_full + [conv_w_grad], 8))
        *small_g, conv_w_g = _unpack(summed, small_full + [conv_w_grad])
        shard = conv_w.shape[-1]
        conv_w_g = lax.dynamic_slice_in_dim(conv_w_g, chip[0] * shard, shard, axis=2)
        names = SMALL + ['conv_w']
        packed = [_pack([p[n] for n in names], 256) for p in (w, m, v)]
        gpack = _pack(small_g + [conv_w_g], 256)
        res = _adamw(packed[0][None], packed[1][None], packed[2][None], gpack, 0, None, "adamw_small")
        like = [w[n] for n in names]
        unpacked = [_unpack(r[0], like) for r in res]
        return {n: tuple(u[i] for u in unpacked) for i, n in enumerate(names)}

    out, follows = {}, dx
    for k, (l, g, names, ax_g, sums, lands, send_sems, recv_sems) in enumerate(sent):
        if k == len(sent) - 1:
            out.update(small_update())
        sums, recv = _grads_ici_wait(sums, lands, send_sems, recv_sems, ax_g, follows, f"grads_ici_wait_{l}_{g}")
        halves = _join_halves([_sum_chips(r, sm, chip, core, ax) for r, sm, ax in zip(recv, sums, ax_g)])
        for n, hv in zip(names, halves):
            out[n] = follows = _adamw(w[n], m[n], v[n], hv.reshape(w[n].shape[1:]), l, out.get(n), "adamw_big")
            follows = follows[0]

    return (loss, dx[None], *[out[n][0] for n in WEIGHTS], *[out[n][1] for n in WEIGHTS],
            *[out[n][2] for n in WEIGHTS], *[out[n][3] for n in WEIGHTS])
```

```python
import functools
import math

import jax
import jax.numpy as jnp
from jax import lax
from jax.experimental import pallas as pl
from jax.experimental.pallas import tpu as pltpu

F32 = jnp.float32
BF16 = jnp.bfloat16
EPS = 1e-6
HEAD_SB = 128
GROUP_GM = 128
CHUNK = 64
HEAD_XA = 256
CONV_TAPS = 3
N_CHIPS = 4
N_DEV = 8
LANES = 128
MIB = 1024 * 1024
VMEM_LIMIT = 48 * MIB
SPLITS = 2

ADAM_LR = 0.001
ADAM_B1 = 0.9
ADAM_B2 = 0.999
ADAM_EPS = 1e-08
ADAM_WD = 0.01
ADAM_STEP = 10

WEIGHTS = ['g_mix_pre', 'w_in', 'g_vnorm', 'w_s', 'b_s', 'g_mem', 'w_mem_kv', 'w_gate', 'b_gate', 'w_br_sb',
           'w_br_gm', 'w_br_xa', 'w_out', 'g_mix_post', 'g_ffn_pre', 'w_up', 'conv_w', 'conv_b', 'w_down',
           'g_ffn_post']
BIG_AXIS = {'w_in': 2, 'w_mem_kv': 1, 'w_gate': 2, 'w_br_sb': 2, 'w_br_gm': 2, 'w_br_xa': 2, 'w_out': 1,
            'w_up': 2, 'w_down': 1}
BIG = list(BIG_AXIS)
FWD_GROUPS = [['w_in'], ['w_mem_kv', 'w_gate'], ['w_br_sb', 'w_br_gm', 'w_br_xa', 'w_out'], ['w_up'], ['w_down']]
BWD_GROUPS = [['w_down', 'w_up'], ['w_out', 'w_br_sb', 'w_br_gm', 'w_br_xa', 'w_gate', 'w_mem_kv'], ['w_in']]
SMALL = ['g_mix_pre', 'g_vnorm', 'w_s', 'b_s', 'g_mem', 'b_gate', 'g_mix_post', 'g_ffn_pre', 'conv_b', 'g_ffn_post']
MESH = pl.DeviceIdType.MESH


def _pcall(body, **kw):
    return pl.pallas_call(body, **kw)


def _params(sem=None, vmem=VMEM_LIMIT):
    return pltpu.CompilerParams(dimension_semantics=sem, vmem_limit_bytes=vmem)


def _tile(n, cands):
    for c in cands:
        if n % c == 0:
            return c
    return n


_GELU_C = math.sqrt(2.0 / math.pi)
_GELU_A = 0.044715


def _gelu(x):
    return 0.5 * x * (1.0 + jnp.tanh(_GELU_C * (x + _GELU_A * (x * x * x))))


def _gelu_and_grad(x):
    x2 = x * x
    t = jnp.tanh(_GELU_C * (x + _GELU_A * (x2 * x)))
    val = 0.5 * x * (1.0 + t)
    grad = 0.5 * (1.0 + t) + 0.5 * x * (1.0 - t * t) * (_GELU_C * (1.0 + 3.0 * _GELU_A * x2))
    return val, grad


def _softplus(z):
    return jnp.maximum(z, 0.0) + jnp.log(1.0 + jnp.exp(-jnp.abs(z)))


def _dot(a, b):
    return jnp.dot(a, b, preferred_element_type=F32)


def _dot_nt(a, b):
    return lax.dot_general(a, b, (((1,), (1,)), ((), ())), preferred_element_type=F32)


def _dot_tn(a, b):
    return lax.dot_general(a, b, (((0,), (0,)), ((), ())), preferred_element_type=F32)


def _split_dot(a, m):
    out = None
    rest = a
    for _ in range(SPLITS):
        piece = rest.astype(BF16)
        rest = rest - piece.astype(F32)
        term = _dot(piece, m)
        out = term if out is None else out + term
    return out


def _mm(a, b, mode, out_dtype, name, tm=None, tn=None, tk=None, after=None):
    if mode == 'nn':
        (m, kc), (kc2, n) = a.shape, b.shape
    elif mode == 'nt':
        (m, kc), (n, kc2) = a.shape, b.shape
    else:
        (kc, m), (kc2, n) = a.shape, b.shape
    assert kc == kc2, (a.shape, b.shape, mode)
    tm = tm or _tile(m, (1024, 512, 256, 128))
    tn = tn or _tile(n, (1024, 512, 256, 128))
    tk = tk or (kc if kc <= 2048 else _tile(kc, (2048, 1536, 1408, 1024, 512)))
    nk = kc // tk
    dot = {'nn': _dot, 'nt': _dot_nt, 'tn': _dot_tn}[mode]
    a_spec = pl.BlockSpec((tk, tm), lambda i, j, k: (k, i)) if mode == 'tn' else pl.BlockSpec((tm, tk), lambda i, j, k: (i, k))
    b_spec = pl.BlockSpec((tn, tk), lambda i, j, k: (j, k)) if mode == 'nt' else pl.BlockSpec((tk, tn), lambda i, j, k: (k, j))

    extra = [] if after is None else [after]
    extra_specs = [pl.BlockSpec(memory_space=pl.ANY)] * len(extra)

    if nk == 1:
        def body(a_ref, b_ref, *rest):
            o_ref = rest[-1]
            o_ref[...] = dot(a_ref[...].astype(BF16), b_ref[...].astype(BF16)).astype(o_ref.dtype)
        scratch = []
    else:
        def body(a_ref, b_ref, *rest):
            o_ref, acc_ref = rest[-2], rest[-1]
            k = pl.program_id(2)
            part = dot(a_ref[...].astype(BF16), b_ref[...].astype(BF16))

            @pl.when(k == 0)
            def _():
                acc_ref[...] = part

            @pl.when(k > 0)
            def _():
                acc_ref[...] += part

            @pl.when(k == nk - 1)
            def _():
                o_ref[...] = acc_ref[...].astype(o_ref.dtype)
        scratch = [pltpu.VMEM((tm, tn), F32)]

    return _pcall(
        body, grid=(m // tm, n // tn, nk), in_specs=[a_spec, b_spec] + extra_specs,
        out_specs=pl.BlockSpec((tm, tn), lambda i, j, k: (i, j)),
        out_shape=jax.ShapeDtypeStruct((m, n), out_dtype), scratch_shapes=scratch, name=name,
        compiler_params=_params(("parallel", "parallel", "arbitrary")))(a, b, *extra)


def _norm_fwd(x, g, res, out_dtype, name, after=None):
    s, d = x.shape
    tr = _tile(s, (256, 128))
    has_res = res is not None
    has_after = after is not None

    def body(*refs):
        x_ref, g_ref = refs[0], refs[1]
        o_ref = refs[-1]
        xv = x_ref[...]
        y = xv * lax.rsqrt(jnp.mean(xv * xv, axis=-1, keepdims=True) + EPS) * g_ref[...]
        if has_res:
            y = y + refs[2][...]
        o_ref[...] = y.astype(o_ref.dtype)

    row = pl.BlockSpec((tr, d), lambda i: (i, 0))
    ins = [x, g] + ([res] if has_res else []) + ([after] if has_after else [])
    return _pcall(
        body, grid=(s // tr,),
        in_specs=[row, pl.BlockSpec((1, d), lambda i: (0, 0))] + ([row] if has_res else [])
        + ([pl.BlockSpec(memory_space=pl.ANY)] if has_after else []),
        out_specs=row, out_shape=jax.ShapeDtypeStruct((s, d), out_dtype), name=name,
        compiler_params=_params(("parallel",)))(*ins)


def _norm_bwd(x, g, douts, dres, out_dtype, name, after=None):
    s, d = x.shape
    tr = _tile(s, (256, 128))
    nd = len(douts)
    has_res = dres is not None
    has_after = after is not None

    def body(*refs):
        x_ref, g_ref = refs[0], refs[1]
        dx_ref, dg_ref = refs[-2], refs[-1]
        dout = refs[2][...].astype(F32)
        for r in refs[3:2 + nd]:
            dout = dout + r[...].astype(F32)
        xv = x_ref[...]
        r = lax.rsqrt(jnp.mean(xv * xv, axis=-1, keepdims=True) + EPS)
        n = xv * r
        dn = dout * g_ref[...]
        dx = r * (dn - n * jnp.mean(dn * n, axis=-1, keepdims=True))
        if has_res:
            dx = dx + refs[2 + nd][...]
        dx_ref[...] = dx.astype(dx_ref.dtype)

        @pl.when(pl.program_id(0) == 0)
        def _():
            dg_ref[...] = jnp.zeros_like(dg_ref)

        dg_ref[...] += jnp.sum(dout * n, axis=0, keepdims=True)

    row = pl.BlockSpec((tr, d), lambda i: (i, 0))
    vec = pl.BlockSpec((1, d), lambda i: (0, 0))
    ins = [x, g] + list(douts) + ([dres] if has_res else []) + ([after] if has_after else [])
    return _pcall(
        body, grid=(s // tr,),
        in_specs=[row, vec] + [row] * (nd + int(has_res)) + ([pl.BlockSpec(memory_space=pl.ANY)] if has_after else []),
        out_specs=[row, vec],
        out_shape=[jax.ShapeDtypeStruct((s, d), out_dtype), jax.ShapeDtypeStruct((1, d), F32)], name=name,
        compiler_params=_params(("arbitrary",)))(*ins)


def _loss_head(y, target):
    s, d = y.shape
    tr = _tile(s, (256, 128))

    def body(y_ref, t_ref, sq_ref, dy_ref):
        e = y_ref[...] - t_ref[...]
        dy_ref[...] = e * (1.0 / d)

        @pl.when(pl.program_id(0) == 0)
        def _():
            sq_ref[...] = jnp.zeros_like(sq_ref)

        sq_ref[...] += jnp.sum(e * e, axis=0, keepdims=True)

    row = pl.BlockSpec((tr, d), lambda i: (i, 0))
    return _pcall(
        body, grid=(s // tr,), in_specs=[row, row], out_specs=[pl.BlockSpec((1, d), lambda i: (0, 0)), row],
        out_shape=[jax.ShapeDtypeStruct((1, d), F32), jax.ShapeDtypeStruct((s, d), F32)], name="loss_head",
        compiler_params=_params(("arbitrary",)))(y, target)


NEVER = -1e30
SB_QUERIES = 512


def _sb_sum_matrix(later):
    r = lax.broadcasted_iota(jnp.int32, (HEAD_SB, 2 * HEAD_SB), 0)
    c = lax.broadcasted_iota(jnp.int32, (HEAD_SB, 2 * HEAD_SB), 1)
    tri = jnp.where((r > c) if later else (r < c), 1.0, 0.0)
    return jnp.where(c < HEAD_SB, tri, 1.0).astype(BF16)


def _sb_mask(tq, q0, k0):
    row = lax.broadcasted_iota(jnp.int32, (tq, HEAD_SB), 0)
    col = lax.broadcasted_iota(jnp.int32, (tq, HEAD_SB), 1)
    return (k0 + col) < (q0 + row)


def _sb_fwd(proj, n_heads):
    s = proj.shape[0]
    tq = min(SB_QUERIES, s)
    per = tq // HEAD_SB
    scale = HEAD_SB ** -0.5

    def body(q_ref, k_ref, v_ref, o_ref, acc_ref, c_ref):
        i = pl.program_id(1)
        q = q_ref[...].astype(BF16)
        sums = _sb_sum_matrix(True)
        acc_ref[...] = jnp.zeros_like(acc_ref)
        c_ref[...] = jnp.zeros_like(c_ref)
        last = (i + 1) * per - 1

        def scores(j, masked):
            off = pl.multiple_of(j * HEAD_SB, HEAD_SB)
            z = _dot_nt(q, k_ref[pl.ds(off, HEAD_SB), :].astype(BF16)) * scale
            sp = _softplus(z)
            logb = z - sp
            if masked:
                mask = _sb_mask(tq, i * tq, off)
                logb = jnp.where(mask, logb, NEVER)
                sp = jnp.where(mask, sp, 0.0)
            return logb, _split_dot(sp, sums)

        def values(j, logb, both):
            off = pl.multiple_of(j * HEAD_SB, HEAD_SB)
            c = c_ref[...]
            a = jnp.exp(logb - both[:, :HEAD_SB] - c)
            acc_ref[...] += _dot(a.astype(BF16), v_ref[pl.ds(off, HEAD_SB), :].astype(BF16))
            c_ref[...] = c + both[:, HEAD_SB:]

        def step(jj, carry, masked):
            j = last - jj
            nxt = scores(j, masked)
            values(jnp.minimum(j + 1, last), *carry)
            return nxt

        idle = (jnp.full((tq, HEAD_SB), NEVER, F32), jnp.zeros((tq, 2 * HEAD_SB), F32))
        carry = lax.fori_loop(0, per, functools.partial(step, masked=True), idle)
        carry = lax.fori_loop(per, last + 1, functools.partial(step, masked=False), carry)
        values(0, *carry)
        o_ref[...] = acc_ref[...].astype(o_ref.dtype)

    h = n_heads
    blk = pl.BlockSpec((tq, HEAD_SB), lambda hh, i: (i, hh))
    return _pcall(
        body, grid=(h, s // tq),
        in_specs=[blk, pl.BlockSpec((s, HEAD_SB), lambda hh, i: (0, h + hh)),
                  pl.BlockSpec((s, HEAD_SB), lambda hh, i: (0, 2 * h + hh))],
        out_specs=blk, out_shape=jax.ShapeDtypeStruct((s, h * HEAD_SB), BF16),
        scratch_shapes=[pltpu.VMEM((tq, HEAD_SB), F32), pltpu.VMEM((tq, HEAD_SB), F32)],
        name="sb_fwd", compiler_params=_params(("parallel", "arbitrary")))(proj, proj, proj)


def _sb_bwd(proj, do, n_heads, after=None):
    s = proj.shape[0]
    tq = min(SB_QUERIES, s)
    per = tq // HEAD_SB
    scale = HEAD_SB ** -0.5

    follow = [] if after is None else [after]

    def body(q_ref, k_ref, v_ref, do_ref, *rest):
        dq_ref, dk_ref, dv_ref, g_ref, beta_ref, run_ref, acc_ref = rest[len(follow):]
        i = pl.program_id(1)

        @pl.when(i == 0)
        def _():
            dk_ref[...] = jnp.zeros_like(dk_ref)
            dv_ref[...] = jnp.zeros_like(dv_ref)

        q = q_ref[...].astype(BF16)
        dob = do_ref[...].astype(BF16)
        last = (i + 1) * per - 1

        run_ref[...] = jnp.zeros_like(run_ref)
        later = _sb_sum_matrix(True)

        def left(jj, masked):
            j = last - jj
            off = pl.multiple_of(j * HEAD_SB, HEAD_SB)
            kb = k_ref[pl.ds(off, HEAD_SB), :].astype(BF16)
            vb = v_ref[pl.ds(off, HEAD_SB), :].astype(BF16)
            z = _dot_nt(q, kb) * scale
            sp = _softplus(z)
            beta = jnp.exp(z - sp)
            if masked:
                mask = _sb_mask(tq, i * tq, off)
                sp = jnp.where(mask, sp, 0.0)
            both = _split_dot(sp, later)
            c = run_ref[...]
            a = beta * jnp.exp(-(both[:, :HEAD_SB] + c))
            if masked:
                a = jnp.where(mask, a, 0.0)
            g_ref[j] = a * _dot_nt(dob, vb)
            beta_ref[j] = beta
            dv_ref[pl.ds(off, HEAD_SB), :] += _dot_tn(a.astype(BF16), dob)
            run_ref[...] = c + both[:, HEAD_SB:]

        def left_diagonal(jj, carry):
            left(jj, True)
            return carry

        def left_below(jj, carry):
            left(jj, False)
            return carry

        lax.fori_loop(0, per, left_diagonal, 0)
        lax.fori_loop(per, last + 1, left_below, 0)

        run_ref[...] = jnp.zeros_like(run_ref)
        acc_ref[...] = jnp.zeros_like(acc_ref)
        earlier = _sb_sum_matrix(False)

        def right(j, masked):
            off = pl.multiple_of(j * HEAD_SB, HEAD_SB)
            kb = k_ref[pl.ds(off, HEAD_SB), :].astype(BF16)
            g = g_ref[j]
            beta = beta_ref[j]
            both = _split_dot(g, earlier)
            p = run_ref[...]
            dz = (g * (1.0 - beta) - beta * (both[:, :HEAD_SB] + p)) * scale
            if masked:
                dz = jnp.where(_sb_mask(tq, i * tq, off), dz, 0.0)
            dzb = dz.astype(BF16)
            dk_ref[pl.ds(off, HEAD_SB), :] += _dot_tn(dzb, q)
            acc_ref[...] += _dot(dzb, kb)
            run_ref[...] = p + both[:, HEAD_SB:]

        def right_below(j, carry):
            right(j, False)
            return carry

        def right_diagonal(j, carry):
            right(j, True)
            return carry

        lax.fori_loop(0, last + 1 - per, right_below, 0)
        lax.fori_loop(last + 1 - per, last + 1, right_diagonal, 0)
        dq_ref[...] = acc_ref[...]

    h = n_heads
    blk = pl.BlockSpec((tq, HEAD_SB), lambda hh, i: (i, hh))
    col_blk = pl.BlockSpec((s, HEAD_SB), lambda hh, i: (0, hh))
    shape = jax.ShapeDtypeStruct((s, h * HEAD_SB), F32)
    nk = s // HEAD_SB
    return _pcall(
        body, grid=(h, s // tq),
        in_specs=[blk, pl.BlockSpec((s, HEAD_SB), lambda hh, i: (0, h + hh)),
                  pl.BlockSpec((s, HEAD_SB), lambda hh, i: (0, 2 * h + hh)), blk] + [pl.BlockSpec(memory_space=pl.ANY)] * len(follow),
        out_specs=[blk, col_blk, col_blk], out_shape=[shape, shape, shape],
        scratch_shapes=[pltpu.VMEM((nk, tq, HEAD_SB), F32), pltpu.VMEM((nk, tq, HEAD_SB), F32),
                        pltpu.VMEM((tq, HEAD_SB), F32), pltpu.VMEM((tq, HEAD_SB), F32)],
        name="sb_bwd", compiler_params=_params(("parallel", "arbitrary")))(proj, proj, proj, do, *follow)


def _gm_mask():
    t = lax.broadcasted_iota(jnp.int32, (GROUP_GM, GROUP_GM), 0)
    s = lax.broadcasted_iota(jnp.int32, (GROUP_GM, GROUP_GM), 1)
    shift = CHUNK.bit_length() - 1
    return (s >> shift) <= (t >> shift)


def _gm_fwd(proj, g_vnorm, w_s, b_st, u_blk):
    s = proj.shape[0]
    groups = w_s.shape[0]
    w = groups * GROUP_GM

    def body(u_ref, v_ref, gv_ref, ws_ref, bst_ref, o_ref):
        ug = _gelu(u_ref[...])
        vg = _gelu(v_ref[...])
        vn = vg * lax.rsqrt(jnp.mean(vg * vg, axis=-1, keepdims=True) + EPS) * gv_ref[...]
        vnb = vn.astype(BF16)
        mask = _gm_mask()
        for g in range(groups):
            sl = slice(g * GROUP_GM, (g + 1) * GROUP_GM)
            wm = jnp.where(mask, ws_ref[g], 0.0).astype(BF16)
            mixed = _dot(wm, vnb[:, sl]) + bst_ref[:, g:g + 1]
            o_ref[:, sl] = (ug[:, sl] * mixed).astype(o_ref.dtype)

    return _pcall(
        body, grid=(s // GROUP_GM,),
        in_specs=[pl.BlockSpec((GROUP_GM, w), lambda c: (c, u_blk)), pl.BlockSpec((GROUP_GM, w), lambda c: (c, u_blk + 1)),
                  pl.BlockSpec((1, w), lambda c: (0, 0)), pl.BlockSpec((groups, GROUP_GM, GROUP_GM), lambda c: (0, 0, 0)),
                  pl.BlockSpec((GROUP_GM, groups), lambda c: (0, 0))],
        out_specs=pl.BlockSpec((GROUP_GM, w), lambda c: (c, 0)),
        out_shape=jax.ShapeDtypeStruct((s, w), BF16), name="gm_fwd",
        compiler_params=_params(("parallel",)))(proj, proj, g_vnorm, w_s, b_st)


def _gm_bwd(proj, g_vnorm, w_s, b_st, do, u_blk):
    s = proj.shape[0]
    groups = w_s.shape[0]
    w = groups * GROUP_GM

    def body(u_ref, v_ref, gv_ref, ws_ref, bst_ref, do_ref, du_ref, dv_ref, dgv_ref, dws_ref, dbst_ref, dvn_ref):
        @pl.when(pl.program_id(0) == 0)
        def _():
            dgv_ref[...] = jnp.zeros_like(dgv_ref)
            dws_ref[...] = jnp.zeros_like(dws_ref)
            dbst_ref[...] = jnp.zeros_like(dbst_ref)

        ug, ugrad = _gelu_and_grad(u_ref[...])
        vg, vgrad = _gelu_and_grad(v_ref[...])
        r = lax.rsqrt(jnp.mean(vg * vg, axis=-1, keepdims=True) + EPS)
        n = vg * r
        gv = gv_ref[...]
        vnb = (n * gv).astype(BF16)
        dout = do_ref[...]
        mask = _gm_mask()
        for g in range(groups):
            sl = slice(g * GROUP_GM, (g + 1) * GROUP_GM)
            wm = jnp.where(mask, ws_ref[g], 0.0).astype(BF16)
            mixed = _dot(wm, vnb[:, sl]) + bst_ref[:, g:g + 1]
            dmixed = dout[:, sl] * ug[:, sl]
            du_ref[:, sl] = dout[:, sl] * mixed * ugrad[:, sl]
            dbst_ref[:, g:g + 1] += jnp.sum(dmixed, axis=1, keepdims=True)
            dmb = dmixed.astype(BF16)
            dws_ref[g] += jnp.where(mask, _dot_nt(dmb, vnb[:, sl]), 0.0)
            dvn_ref[:, sl] = _dot_tn(wm, dmb)
        dvn = dvn_ref[...]
        dgv_ref[...] += jnp.sum(dvn * n, axis=0, keepdims=True)
        dn = dvn * gv
        dvg = r * (dn - n * jnp.mean(dn * n, axis=-1, keepdims=True))
        dv_ref[...] = dvg * vgrad

    rowb = pl.BlockSpec((GROUP_GM, w), lambda c: (c, 0))
    vec = pl.BlockSpec((1, w), lambda c: (0, 0))
    wsb = pl.BlockSpec((groups, GROUP_GM, GROUP_GM), lambda c: (0, 0, 0))
    bsb = pl.BlockSpec((GROUP_GM, groups), lambda c: (0, 0))
    return _pcall(
        body, grid=(s // GROUP_GM,),
        in_specs=[pl.BlockSpec((GROUP_GM, w), lambda c: (c, u_blk)), pl.BlockSpec((GROUP_GM, w), lambda c: (c, u_blk + 1)),
                  vec, wsb, bsb, rowb],
        out_specs=[rowb, rowb, vec, wsb, bsb],
        out_shape=[jax.ShapeDtypeStruct((s, w), F32), jax.ShapeDtypeStruct((s, w), F32), jax.ShapeDtypeStruct((1, w), F32),
                   jax.ShapeDtypeStruct((groups, GROUP_GM, GROUP_GM), F32), jax.ShapeDtypeStruct((GROUP_GM, groups), F32)],
        scratch_shapes=[pltpu.VMEM((GROUP_GM, w), F32)], name="gm_bwd",
        compiler_params=_params(("arbitrary",)))(proj, proj, g_vnorm, w_s, b_st, do)


def _xa_fwd(proj, mem_kv, q_blk, n_heads):
    s = proj.shape[0]
    nm = mem_kv.shape[0]
    tq = _tile(s, (512, 256, 128))
    scale = HEAD_XA ** -0.5

    def body(q_ref, k_ref, v_ref, o_ref):
        z = _dot_nt(q_ref[...].astype(BF16), k_ref[...].astype(BF16)) * scale
        z = z - jnp.max(z, axis=-1, keepdims=True)
        e = jnp.exp(z)
        p = e / jnp.sum(e, axis=-1, keepdims=True)
        o_ref[...] = _dot(p.astype(BF16), v_ref[...].astype(BF16)).astype(o_ref.dtype)

    h = n_heads
    return _pcall(
        body, grid=(h, s // tq),
        in_specs=[pl.BlockSpec((tq, HEAD_XA), lambda hh, i: (i, q_blk + hh)),
                  pl.BlockSpec((nm, HEAD_XA), lambda hh, i: (0, hh)), pl.BlockSpec((nm, HEAD_XA), lambda hh, i: (0, h + hh))],
        out_specs=pl.BlockSpec((tq, HEAD_XA), lambda hh, i: (i, hh)),
        out_shape=jax.ShapeDtypeStruct((s, h * HEAD_XA), BF16), name="xa_fwd",
        compiler_params=_params(("parallel", "parallel")))(proj, mem_kv, mem_kv)


def _xa_bwd(proj, mem_kv, do, q_blk, n_heads):
    s = proj.shape[0]
    nm = mem_kv.shape[0]
    tq = _tile(s, (512, 256, 128))
    scale = HEAD_XA ** -0.5
    h = n_heads

    def body(q_ref, k_ref, v_ref, do_ref, dq_ref, dk_ref, dv_ref):
        @pl.when(pl.program_id(1) == 0)
        def _():
            dk_ref[...] = jnp.zeros_like(dk_ref)
            dv_ref[...] = jnp.zeros_like(dv_ref)

        qb = q_ref[...].astype(BF16)
        kb = k_ref[...].astype(BF16)
        vb = v_ref[...].astype(BF16)
        dob = do_ref[...].astype(BF16)
        z = _dot_nt(qb, kb) * scale
        z = z - jnp.max(z, axis=-1, keepdims=True)
        e = jnp.exp(z)
        p = e / jnp.sum(e, axis=-1, keepdims=True)
        dp = _dot_nt(dob, vb)
        dz = (p * (dp - jnp.sum(dp * p, axis=-1, keepdims=True)) * scale).astype(BF16)
        dq_ref[...] = _dot(dz, kb)
        dk_ref[...] += _dot_tn(dz, qb)
        dv_ref[...] += _dot_tn(p.astype(BF16), dob)

    qspec = pl.BlockSpec((tq, HEAD_XA), lambda hh, i: (i, hh))
    dk, dv = None, None
    dq, dk, dv = _pcall(
        body, grid=(h, s // tq),
        in_specs=[pl.BlockSpec((tq, HEAD_XA), lambda hh, i: (i, q_blk + hh)),
                  pl.BlockSpec((nm, HEAD_XA), lambda hh, i: (0, hh)), pl.BlockSpec((nm, HEAD_XA), lambda hh, i: (0, h + hh)),
                  qspec],
        out_specs=[qspec, pl.BlockSpec((nm, HEAD_XA), lambda hh, i: (0, hh)), pl.BlockSpec((nm, HEAD_XA), lambda hh, i: (0, hh))],
        out_shape=[jax.ShapeDtypeStruct((s, h * HEAD_XA), F32), jax.ShapeDtypeStruct((nm, h * HEAD_XA), F32),
                   jax.ShapeDtypeStruct((nm, h * HEAD_XA), F32)],
        name="xa_bwd", compiler_params=_params(("parallel", "arbitrary")))(proj, mem_kv, mem_kv, do)
    return dq, dk, dv


def _merge_fwd(zg, b_gate, branches):
    s, d = branches[0].shape
    tr = _tile(s, (128,))

    def body(z0, z1, z2, g0, g1, g2, b0, b1, b2, o_ref):
        acc = None
        for z, g, b in ((z0, g0, b0), (z1, g1, b1), (z2, g2, b2)):
            term = jax.nn.sigmoid(z[...] + g[...]) * b[...]
            acc = term if acc is None else acc + term
        o_ref[...] = acc.astype(o_ref.dtype)

    zs = [pl.BlockSpec((tr, d), functools.partial(lambda i, k: (i, k), k=k)) for k in range(3)]
    gs = [pl.BlockSpec((1, d), functools.partial(lambda i, k: (0, k), k=k)) for k in range(3)]
    row = pl.BlockSpec((tr, d), lambda i: (i, 0))
    return _pcall(
        body, grid=(s // tr,), in_specs=zs + gs + [row] * 3, out_specs=row,
        out_shape=jax.ShapeDtypeStruct((s, d), BF16), name="merge_fwd",
        compiler_params=_params(("parallel",)))(zg, zg, zg, b_gate, b_gate, b_gate, *branches)


def _merge_bwd(zg, b_gate, branches, dmerged):
    s, d = branches[0].shape
    tr = _tile(s, (128,))

    def body(z0, z1, z2, g0, g1, g2, b0, b1, b2, dm_ref, dz_ref, d0, d1, d2, dbg_ref):
        @pl.when(pl.program_id(0) == 0)
        def _():
            dbg_ref[...] = jnp.zeros_like(dbg_ref)

        dm = dm_ref[...]
        for k, (z, g, b, dbr) in enumerate(((z0, g0, b0, d0), (z1, g1, b1, d1), (z2, g2, b2, d2))):
            sg = jax.nn.sigmoid(z[...] + g[...])
            dbr[...] = (dm * sg).astype(dbr.dtype)
            dz = dm * b[...] * sg * (1.0 - sg)
            dz_ref[:, k * d:(k + 1) * d] = dz.astype(dz_ref.dtype)
            dbg_ref[:, k * d:(k + 1) * d] += jnp.sum(dz, axis=0, keepdims=True)

    zs = [pl.BlockSpec((tr, d), functools.partial(lambda i, k: (i, k), k=k)) for k in range(3)]
    gs = [pl.BlockSpec((1, d), functools.partial(lambda i, k: (0, k), k=k)) for k in range(3)]
    row = pl.BlockSpec((tr, d), lambda i: (i, 0))
    outs = _pcall(
        body, grid=(s // tr,), in_specs=zs + gs + [row] * 4,
        out_specs=[pl.BlockSpec((tr, 3 * d), lambda i: (i, 0)), row, row, row, pl.BlockSpec((1, 3 * d), lambda i: (0, 0))],
        out_shape=[jax.ShapeDtypeStruct((s, 3 * d), BF16)] + [jax.ShapeDtypeStruct((s, d), BF16)] * 3
        + [jax.ShapeDtypeStruct((1, 3 * d), F32)],
        name="merge_bwd", compiler_params=_params(("arbitrary",)))(zg, zg, zg, b_gate, b_gate, b_gate, *branches, dmerged)
    return outs[0], list(outs[1:4]), outs[4]


def _shift_down(x, k, row):
    return jnp.where(row >= k, pltpu.roll(x, k, 0), 0.0)


def _shift_up(x, k, row, s):
    return jnp.where(row < s - k, pltpu.roll(x, s - k, 0), 0.0)


def _conv_pre(gate, cw_ref, cb_ref, row):
    conv = cb_ref[...] + cw_ref[CONV_TAPS - 1:CONV_TAPS, :] * gate
    for k in range(1, CONV_TAPS):
        conv = conv + cw_ref[CONV_TAPS - 1 - k:CONV_TAPS - k, :] * _shift_down(gate, k, row)
    return conv


def _cg_fwd(up, conv_w, conv_b):
    s = up.shape[0]
    f = conv_w.shape[1]
    tc = _tile(f, (256, 128))
    nb = f // tc

    def body(g_ref, v_ref, cw_ref, cb_ref, o_ref):
        row = lax.broadcasted_iota(jnp.int32, (s, tc), 0)
        conv = _conv_pre(g_ref[...], cw_ref, cb_ref, row)
        o_ref[...] = (_gelu(conv) * v_ref[...]).astype(o_ref.dtype)

    return _pcall(
        body, grid=(nb,),
        in_specs=[pl.BlockSpec((s, tc), lambda j: (0, j)), pl.BlockSpec((s, tc), lambda j: (0, nb + j)),
                  pl.BlockSpec((CONV_TAPS, tc), lambda j: (0, j)), pl.BlockSpec((1, tc), lambda j: (0, j))],
        out_specs=pl.BlockSpec((s, tc), lambda j: (0, j)),
        out_shape=jax.ShapeDtypeStruct((s, f), BF16), name="cg_fwd",
        compiler_params=_params(("parallel",)))(up, up, conv_w, conv_b)


def _cg_bwd(up, conv_w, conv_b, dact):
    s = up.shape[0]
    f = conv_w.shape[1]
    tc = _tile(f, (256, 128))
    nb = f // tc

    def body(g_ref, v_ref, cw_ref, cb_ref, da_ref, dg_ref, dv_ref, dcw_ref, dcb_ref):
        row = lax.broadcasted_iota(jnp.int32, (s, tc), 0)
        gate = g_ref[...]
        conv = _conv_pre(gate, cw_ref, cb_ref, row)
        gel, ggrad = _gelu_and_grad(conv)
        da = da_ref[...]
        dv_ref[...] = (da * gel).astype(dv_ref.dtype)
        dconv = da * v_ref[...] * ggrad
        dgate = cw_ref[CONV_TAPS - 1:CONV_TAPS, :] * dconv
        dcw_ref[CONV_TAPS - 1:CONV_TAPS, :] = jnp.sum(dconv * gate, axis=0, keepdims=True)
        for k in range(1, CONV_TAPS):
            dgate = dgate + cw_ref[CONV_TAPS - 1 - k:CONV_TAPS - k, :] * _shift_up(dconv, k, row, s)
            dcw_ref[CONV_TAPS - 1 - k:CONV_TAPS - k, :] = jnp.sum(dconv * _shift_down(gate, k, row), axis=0, keepdims=True)
        dg_ref[...] = dgate.astype(dg_ref.dtype)
        dcb_ref[...] = jnp.sum(dconv, axis=0, keepdims=True)

    colb = pl.BlockSpec((s, tc), lambda j: (0, j))
    return _pcall(
        body, grid=(nb,),
        in_specs=[colb, pl.BlockSpec((s, tc), lambda j: (0, nb + j)), pl.BlockSpec((CONV_TAPS, tc), lambda j: (0, j)),
                  pl.BlockSpec((1, tc), lambda j: (0, j)), colb],
        out_specs=[colb, colb, pl.BlockSpec((CONV_TAPS, tc), lambda j: (0, j)), pl.BlockSpec((1, tc), lambda j: (0, j))],
        out_shape=[jax.ShapeDtypeStruct((s, f), BF16), jax.ShapeDtypeStruct((s, f), BF16),
                   jax.ShapeDtypeStruct((CONV_TAPS, f), F32), jax.ShapeDtypeStruct((1, f), F32)],
        name="cg_bwd", compiler_params=_params(("parallel",)))(up, up, conv_w, conv_b, dact)


def _row_tile(rows, cols, elems=256 * 1024):
    want = max(16, elems // cols)
    for c in (512, 256, 128, 64, 32, 16):
        if c <= want and rows % c == 0:
            return c
    return rows


def _sum_halves(dwv, recv, core, name):
    nj, _, a, c = dwv.shape
    tr = _row_tile(a, c, 1024 * 1024)

    def body(core_ref, d_ref, r_ref, o_ref):
        o_ref[0] = (d_ref[0, 0].astype(F32) + r_ref[0].astype(F32)).astype(o_ref.dtype)

    grid_spec = pltpu.PrefetchScalarGridSpec(
        num_scalar_prefetch=1, grid=(nj, a // tr),
        in_specs=[pl.BlockSpec((1, 1, tr, c), lambda j, i, cr: (j, cr[0], i, 0)),
                  pl.BlockSpec((1, tr, c), lambda j, i, cr: (j, i, 0))],
        out_specs=pl.BlockSpec((1, tr, c), lambda j, i, cr: (j, i, 0)))
    return _pcall(body, grid_spec=grid_spec, out_shape=jax.ShapeDtypeStruct((nj, a, c), BF16), name=name,
                  compiler_params=_params(("parallel", "parallel")))(core, dwv, recv)


def _sum_chips(recv, own, chip, core, ax):
    _, a, b = recv.shape
    tr = _row_tile(a, b)

    def body(chip_ref, core_ref, r_ref, own_ref, o_ref):
        me = chip_ref[0]
        mine = own_ref[0].astype(F32)
        acc = None
        for k in range(N_CHIPS):
            term = jnp.where(me == k, mine, r_ref[k].astype(F32))
            acc = term if acc is None else acc + term
        o_ref[0] = acc

    own_spec = (pl.BlockSpec((1, tr, b), lambda i, ch, co: (0, i, ch[0])) if ax == 2
                else pl.BlockSpec((1, tr, b), lambda i, ch, co: (ch[0], i, 0)))
    grid_spec = pltpu.PrefetchScalarGridSpec(
        num_scalar_prefetch=2, grid=(a // tr,),
        in_specs=[pl.BlockSpec((N_CHIPS, tr, b), lambda i, ch, co: (0, i, 0)), own_spec],
        out_specs=pl.BlockSpec((1, tr, b), lambda i, ch, co: (co[0], i, 0)))
    return _pcall(body, grid_spec=grid_spec, out_shape=jax.ShapeDtypeStruct((2, a, b), F32),
                  name="sum_chips", compiler_params=_params(("parallel",)))(chip, core, recv, own)


def _place_own(wt, layer, chip, ax, after):
    nl, r, c = wt.shape
    half = r // 2
    tr = _row_tile(half, c)
    nb = half // tr

    def body(chip_ref, w_ref, after_ref, o_ref):
        o_ref[...] = w_ref[...].astype(BF16).reshape(o_ref.shape)

    if ax == 2:
        out_spec = pl.BlockSpec((1, tr, c), lambda h, i, ch: (h, i, ch[0]))
    else:
        out_spec = pl.BlockSpec((1, 1, tr, c), lambda h, i, ch: (ch[0], h, i, 0))
    grid_spec = pltpu.PrefetchScalarGridSpec(
        num_scalar_prefetch=1, grid=(2, nb),
        in_specs=[pl.BlockSpec((1, tr, c), lambda h, i, ch: (layer, h * nb + i, 0)), pl.BlockSpec(memory_space=pl.ANY)],
        out_specs=out_spec)
    return _pcall(body, grid_spec=grid_spec, out_shape=jax.ShapeDtypeStruct(_full_view_shape(wt.shape, ax), BF16),
                  name="place_own", compiler_params=_params(("parallel", "parallel")))(chip, wt, after)


def _adamw(w, m, v, g, layer, prev, name, after=None):
    nl, r, c = w.shape
    tr = _row_tile(r, c)
    c1 = 1.0 - ADAM_B1 ** ADAM_STEP
    c2 = 1.0 - ADAM_B2 ** ADAM_STEP

    follow = [] if after is None else [after]

    def body(w_ref, m_ref, v_ref, gin_ref, *rest):
        g_ref, d_ref, nm_ref, nv_ref = rest[-4:]
        g = gin_ref[...]
        mm = ADAM_B1 * m_ref[0] + (1.0 - ADAM_B1) * g
        vv = ADAM_B2 * v_ref[0] + (1.0 - ADAM_B2) * (g * g)
        g_ref[0] = g
        nm_ref[0] = mm
        nv_ref[0] = vv
        d_ref[0] = -ADAM_LR * ((mm / c1) / (jnp.sqrt(vv / c2) + ADAM_EPS) + ADAM_WD * w_ref[0])

    blk = pl.BlockSpec((1, tr, c), lambda i: (layer, i, 0))
    shape = jax.ShapeDtypeStruct((nl, r, c), F32)
    extra = [] if prev is None else list(prev)
    return _pcall(
        body, grid=(r // tr,),
        in_specs=[blk] * 3 + [pl.BlockSpec((tr, c), lambda i: (i, 0))] + [pl.BlockSpec(memory_space=pl.ANY)] * (len(extra) + len(follow)),
        out_specs=[blk] * 4, out_shape=[shape] * 4, input_output_aliases={4 + k: k for k in range(len(extra))}, name=name,
        compiler_params=_params(("parallel",)))(w, m, v, g, *extra, *follow)


HBM_SPEC = pl.BlockSpec(memory_space=pltpu.HBM)
COMM = pltpu.CompilerParams(has_side_effects=True)


def _position():
    x, y, c = lax.axis_index("x"), lax.axis_index("y"), lax.axis_index("c")
    chips = [(1 - x, y), (x, 1 - y), (1 - x, 1 - y)]
    return x, y, c, chips


def _remote(src, dst, send_sem, recv_sem, dev):
    return pltpu.make_async_remote_copy(src_ref=src, dst_ref=dst, send_sem=send_sem, recv_sem=recv_sem,
                                        device_id=dev, device_id_type=MESH)


def _full_view_shape(shard_shape, ax):
    _, r, c = shard_shape
    return (2, r // 2, c * N_CHIPS) if ax == 2 else (N_CHIPS, 2, r // 2, c)


def _piece(ref, ax, j, h, cs):
    if ax == 2:
        return ref.at[h, :, pl.ds(pl.multiple_of(j * cs, cs), cs)]
    return ref.at[j, h]


def _chip_block(ref, ax, j, cs):
    if ax == 2:
        return ref.at[:, :, pl.ds(pl.multiple_of(j * cs, cs), cs)]
    return ref.at[j]


SEM_SPEC = pl.BlockSpec(memory_space=pltpu.SEMAPHORE)
ANY_SPEC = pl.BlockSpec(memory_space=pl.ANY)
SPLIT = pltpu.CompilerParams(has_side_effects=pltpu.SideEffectType.DATAFLOW_SIDE_EFFECTING)


def _exchange(kind, name, bufs, build, n_sems, after=None, sems=None):
    n = len(bufs)
    if kind == 'sync':
        def body(*refs):
            mine, theirs = build(refs[n:2 * n], refs[2 * n], refs[2 * n + 1])
            for cp in mine:
                cp.start()
            for cp in theirs:
                cp.wait_recv()
            for cp in mine:
                cp.wait_send()

        return list(_pcall(
            body, in_specs=[HBM_SPEC] * n, out_specs=[HBM_SPEC] * n,
            out_shape=[jax.ShapeDtypeStruct(v.shape, v.dtype) for v in bufs], input_output_aliases={t: t for t in range(n)},
            scratch_shapes=[pltpu.SemaphoreType.DMA((n_sems,)), pltpu.SemaphoreType.DMA((n_sems,))],
            name=name, compiler_params=COMM)(*bufs))
    if kind == 'start':
        def body(*refs):
            mine, _ = build(refs[n + 3:2 * n + 3], refs[n + 1], refs[n + 2])
            for cp in mine:
                cp.start()
            refs[2 * n + 3][...] = jnp.zeros_like(refs[2 * n + 3])

        outs = _pcall(
            body, in_specs=[HBM_SPEC] * n + [ANY_SPEC],
            out_specs=[SEM_SPEC, SEM_SPEC] + [HBM_SPEC] * n + [pl.BlockSpec(memory_space=pltpu.VMEM)],
            out_shape=[pltpu.SemaphoreType.DMA((n_sems,)), pltpu.SemaphoreType.DMA((n_sems,))]
            + [pltpu.HBM(v.shape, v.dtype) for v in bufs] + [jax.ShapeDtypeStruct((8, LANES), F32)],
            input_output_aliases={t: 2 + t for t in range(n)}, name=name,
            compiler_params=SPLIT)(*[pltpu.with_memory_space_constraint(v, pltpu.HBM) for v in bufs], after)
        return (outs[0], outs[1]), list(outs[2:2 + n]), outs[2 + n]

    def body(*refs):
        mine, theirs = build(refs[:n], refs[n], refs[n + 1])
        for cp in mine:
            cp.wait_send()
        for cp in theirs:
            cp.wait_recv()

    return list(_pcall(
        body, in_specs=[HBM_SPEC] * n + [SEM_SPEC, SEM_SPEC, ANY_SPEC], out_specs=[HBM_SPEC] * n,
        out_shape=[pltpu.HBM(v.shape, v.dtype) for v in bufs], input_output_aliases={t: t for t in range(n)},
        name=name, compiler_params=SPLIT)(*bufs, sems[0], sems[1], after))


def _gather_ici_copies(axes, shard_cols):
    def build(bufs, send_sems, recv_sems):
        x, y, c, chips = _position()
        me = 2 * x + y
        mine, theirs = [], []
        for t, ax in enumerate(axes):
            own = _piece(bufs[t], ax, me, c, shard_cols[t])
            for p, (px, py) in enumerate(chips):
                k = t * 3 + p
                got = _piece(bufs[t], ax, 2 * px + py, c, shard_cols[t])
                mine.append(_remote(own, own, send_sems.at[k], recv_sems.at[k], (px, py, c)))
                theirs.append(_remote(got, got, send_sems.at[k], recv_sems.at[k], (px, py, c)))
        return mine, theirs
    return build


def _gather_d2d_copies(axes, shard_cols):
    def build(bufs, send_sems, recv_sems):
        x, y, c, chips = _position()
        mine, theirs = [], []
        for t, ax in enumerate(axes):
            for p, (px, py) in enumerate(chips):
                k = t * 3 + p
                had = _piece(bufs[t], ax, 2 * px + py, c, shard_cols[t])
                got = _piece(bufs[t], ax, 2 * px + py, 1 - c, shard_cols[t])
                mine.append(_remote(had, had, send_sems.at[k], recv_sems.at[k], (x, y, 1 - c)))
                theirs.append(_remote(got, got, send_sems.at[k], recv_sems.at[k], (x, y, 1 - c)))
        return mine, theirs
    return build


def _grads_d2d_copies(n):
    def build(bufs, send_sems, recv_sems):
        x, y, c, _ = _position()
        mine = [_remote(bufs[t].at[:, 1 - c], bufs[n + t], send_sems.at[t], recv_sems.at[t], (x, y, 1 - c)) for t in range(n)]
        return mine, mine
    return build


def _grads_ici_copies(axes):
    n = len(axes)

    def build(bufs, send_sems, recv_sems):
        x, y, c, chips = _position()
        me = 2 * x + y

        def block(t, j):
            if axes[t] == 2:
                cs = bufs[n + t].shape[2]
                return bufs[t].at[0, :, pl.ds(pl.multiple_of(j * cs, cs), cs)]
            return bufs[t].at[j]

        mine, theirs = [], []
        for t in range(n):
            for p, (px, py) in enumerate(chips):
                k = t * 3 + p
                peer = 2 * px + py
                mine.append(_remote(block(t, peer), bufs[n + t].at[me], send_sems.at[k], recv_sems.at[k], (px, py, c)))
                theirs.append(_remote(block(t, peer), bufs[n + t].at[peer], send_sems.at[k], recv_sems.at[k], (px, py, c)))
        return mine, theirs
    return build


def _join_copies(bufs, send_sems, recv_sems):
    x, y, c, _ = _position()
    mine = [_remote(b.at[c], b.at[c], send_sems.at[t], recv_sems.at[t], (x, y, 1 - c)) for t, b in enumerate(bufs)]
    theirs = [_remote(b.at[1 - c], b.at[1 - c], send_sems.at[t], recv_sems.at[t], (x, y, 1 - c)) for t, b in enumerate(bufs)]
    return mine, theirs


def _grads_recv_shape(sm, ax):
    _, a, c = sm.shape
    return (N_CHIPS, a, c // N_CHIPS if ax == 2 else c)


def _gather_small(shard):
    nl, r, cs = shard.shape

    def body(in_ref, out_ref, send_sems, recv_sems, local_sem):
        x, y, c, chips = _position()

        def cols(j):
            return out_ref.at[:, :, pl.ds(pl.multiple_of(j * cs, cs), cs)]

        me = 2 * x + y
        loc = pltpu.make_async_copy(in_ref, cols(me), local_sem)
        loc.start()
        remote = [_remote(in_ref, cols(me), send_sems.at[p], recv_sems.at[p], (px, py, c)) for p, (px, py) in enumerate(chips)]
        for cp in remote:
            cp.start()
        for p, (px, py) in enumerate(chips):
            _remote(in_ref, cols(2 * px + py), send_sems.at[p], recv_sems.at[p], (px, py, c)).wait_recv()
        for cp in remote:
            cp.wait_send()
        loc.wait()

    return _pcall(
        body, in_specs=[HBM_SPEC], out_specs=HBM_SPEC, out_shape=jax.ShapeDtypeStruct((nl, r, cs * N_CHIPS), shard.dtype),
        scratch_shapes=[pltpu.SemaphoreType.DMA((3,)), pltpu.SemaphoreType.DMA((3,)), pltpu.SemaphoreType.DMA(())],
        name="gather_small", compiler_params=COMM)(shard)


def _all_reduce_small(pack, after):
    r, c = pack.shape

    def body(in_ref, after_ref, out_ref, slots, send_sems, recv_sems, local_sem):
        x, y, cc, _ = _position()
        me = 4 * x + 2 * y + cc
        peers = []
        for k in range(1, N_DEV):
            fx, fy, fc = (k >> 2) & 1, (k >> 1) & 1, k & 1
            px, py, pc = x ^ fx, y ^ fy, cc ^ fc
            peers.append((k - 1, (px, py, pc), 4 * px + 2 * py + pc))
        loc = pltpu.make_async_copy(in_ref, slots.at[me], local_sem)
        loc.start()
        copies = [_remote(in_ref, slots.at[me], send_sems.at[k], recv_sems.at[k], dev) for k, dev, _ in peers]
        for cp in copies:
            cp.start()
        for k, dev, idx in peers:
            _remote(in_ref, slots.at[idx], send_sems.at[k], recv_sems.at[k], dev).wait_recv()
        for cp in copies:
            cp.wait_send()
        loc.wait()
        acc = slots[0]
        for k in range(1, N_DEV):
            acc = acc + slots[k]
        out_ref[...] = acc

    vm = pl.BlockSpec(memory_space=pltpu.VMEM)
    return _pcall(
        body, in_specs=[vm, ANY_SPEC], out_specs=vm, out_shape=jax.ShapeDtypeStruct((r, c), F32),
        scratch_shapes=[pltpu.VMEM((N_DEV, r, c), F32), pltpu.SemaphoreType.DMA((N_DEV - 1,)),
                        pltpu.SemaphoreType.DMA((N_DEV - 1,)), pltpu.SemaphoreType.DMA(())],
        name="all_reduce_small", compiler_params=pltpu.CompilerParams(has_side_effects=True, vmem_limit_bytes=VMEM_LIMIT))(pack, after)


def _dims(d):
    half = d // 2
    return half // HEAD_SB, half // HEAD_XA, 3, (5 * half) // HEAD_XA


def _layer_fwd(x, mem, weight, small, l, after=None):
    h_sb, h_xa, u_blk, q_blk = _dims(x.shape[1])

    def vec(name):
        return small[name][l].reshape(1, -1)

    def use(a, name, follows, mm_name):
        wt = weight(name, follows)
        return _mm(a, wt, 'nn', F32, mm_name, after=weight.token())

    h1 = _norm_fwd(x, vec('g_mix_pre'), None, BF16, "norm_mix_pre", after)
    proj = use(h1, 'w_in', h1, "mm_proj")
    o_sb = _sb_fwd(proj, h_sb)
    b_st = small['b_s'][l].T
    o_gm = _gm_fwd(proj, vec('g_vnorm'), small['w_s'][l], b_st, u_blk)
    memn = _norm_fwd(mem, vec('g_mem'), None, BF16, "norm_mem")
    mem_kv = use(memn, 'w_mem_kv', o_sb, "mm_mem_kv")
    o_xa = _xa_fwd(proj, mem_kv, q_blk, h_xa)
    zg = use(h1, 'w_gate', o_sb, "mm_gate")
    branches = [use(o, wn, zg, "mm_branch") for o, wn in ((o_sb, 'w_br_sb'), (o_gm, 'w_br_gm'), (o_xa, 'w_br_xa'))]
    merged = _merge_fwd(zg, vec('b_gate'), branches)
    y1 = use(merged, 'w_out', zg, "mm_out")
    x1 = _norm_fwd(y1, vec('g_mix_post'), x, F32, "norm_mix_post")
    h2 = _norm_fwd(x1, vec('g_ffn_pre'), None, BF16, "norm_ffn_pre")
    up = use(h2, 'w_up', h2, "mm_up")
    act = _cg_fwd(up, weight('conv_w', up), vec('conv_b'))
    y2 = use(act, 'w_down', act, "mm_down")
    x2 = _norm_fwd(y2, vec('g_ffn_post'), x1, F32, "norm_ffn_post")
    saved = dict(x0=x, h1=h1, proj=proj, o_sb=o_sb, o_gm=o_gm, o_xa=o_xa, memn=memn, mem_kv=mem_kv, zg=zg,
                 branches=branches, merged=merged, y1=y1, x1=x1, h2=h2, up=up, act=act, y2=y2, b_st=b_st)
    return x2, saved


def _layer_bwd(dx, mem, sv, full, small, l, after, emit):
    h_sb, h_xa, u_blk, q_blk = _dims(dx.shape[1])

    def vec(name):
        return small[name][l].reshape(1, -1)

    gb, gs = {}, {}
    dy2, gs['g_ffn_post'] = _norm_bwd(sv['y2'], vec('g_ffn_post'), [dx], None, BF16, "norm_ffn_post_bwd", after)
    gb['w_down'] = _mm(sv['act'], dy2, 'tn', BF16, "mm_down_dw")
    dact = _mm(dy2, full['w_down'], 'nt', F32, "mm_down_dx")
    dgate, dval, gs['conv_w'], gs['conv_b'] = _cg_bwd(sv['up'], full['conv_w'], vec('conv_b'), dact)
    dup = jnp.concatenate([dgate, dval], axis=1)
    gb['w_up'] = _mm(sv['h2'], dup, 'tn', BF16, "mm_up_dw")
    token = emit(0, gb)
    dh2 = _mm(dup, full['w_up'], 'nt', F32, "mm_up_dx", after=token)
    token = emit.flush(dh2)
    dx1, gs['g_ffn_pre'] = _norm_bwd(sv['x1'], vec('g_ffn_pre'), [dh2], dx, F32, "norm_ffn_pre_bwd", token)
    dy1, gs['g_mix_post'] = _norm_bwd(sv['y1'], vec('g_mix_post'), [dx1], None, BF16, "norm_mix_post_bwd")
    gb['w_out'] = _mm(sv['merged'], dy1, 'tn', BF16, "mm_out_dw")
    dmerged = _mm(dy1, full['w_out'], 'nt', F32, "mm_out_dx")
    dzg, dbr, gs['b_gate'] = _merge_bwd(sv['zg'], vec('b_gate'), sv['branches'], dmerged)
    douts = []
    for o, db, wn in ((sv['o_sb'], dbr[0], 'w_br_sb'), (sv['o_gm'], dbr[1], 'w_br_gm'), (sv['o_xa'], dbr[2], 'w_br_xa')):
        gb[wn] = _mm(o, db, 'tn', BF16, "mm_branch_dw")
        douts.append(_mm(db, full[wn], 'nt', F32, "mm_branch_dx"))
    gb['w_gate'] = _mm(sv['h1'], dzg, 'tn', BF16, "mm_gate_dw")
    dq_xa, dk_xa, dv_xa = _xa_bwd(sv['proj'], sv['mem_kv'], douts[2], q_blk, h_xa)
    dmem_kv = jnp.concatenate([dk_xa, dv_xa], axis=1).astype(BF16)
    gb['w_mem_kv'] = _mm(sv['memn'], dmem_kv, 'tn', BF16, "mm_mem_kv_dw")
    token = emit(1, gb)
    dh1_gate = _mm(dzg, full['w_gate'], 'nt', F32, "mm_gate_dx", after=token)
    token = emit.flush(dh1_gate)
    dmemn = _mm(dmem_kv, full['w_mem_kv'], 'nt', F32, "mm_mem_kv_dx")
    _, gs['g_mem'] = _norm_bwd(mem, vec('g_mem'), [dmemn], None, BF16, "norm_mem_bwd")
    du, dv, gs['g_vnorm'], gs['w_s'], db_st = _gm_bwd(sv['proj'], vec('g_vnorm'), small['w_s'][l], sv['b_st'], douts[1], u_blk)
    gs['b_s'] = db_st.T
    dq, dk, dvv = _sb_bwd(sv['proj'], douts[0], h_sb, token)
    dproj = jnp.concatenate([dq, dk, dvv, du, dv, dq_xa], axis=1).astype(BF16)
    gb['w_in'] = _mm(sv['h1'], dproj, 'tn', BF16, "mm_proj_dw")
    token = emit(2, gb)
    dh1_proj = _mm(dproj, full['w_in'], 'nt', F32, "mm_proj_dx")
    dx0, gs['g_mix_pre'] = _norm_bwd(sv['x0'], vec('g_mix_pre'), [dh1_gate, dh1_proj], dx1, F32, "norm_mix_pre_bwd", token)
    return dx0, gs


class _Given:
    def __init__(self, full):
        self.full = full

    def __call__(self, name, follows):
        return self.full[name]

    def token(self):
        return None


def _local_step(x, mem, target, full, small):
    n_layers = len(full['w_in'])
    saved = []
    for l in range(n_layers):
        x, sv = _layer_fwd(x, mem, _Given({n: full[n][l] for n in full}), small, l)
        saved.append(sv)
    sq, dx = _loss_head(x, target)
    gbig = {n: [None] * n_layers for n in BIG}
    gsmall = {n: [None] * n_layers for n in SMALL + ['conv_w']}
    class Collect:
        def __init__(self, l):
            self.l = l

        def __call__(self, g, gb):
            for n in BWD_GROUPS[g]:
                gbig[n][self.l] = gb[n]

        def flush(self, follows):
            return None

    for l in reversed(range(n_layers)):
        dx, gs = _layer_bwd(dx, mem, saved[l], {n: full[n][l] for n in full}, small, l, None, Collect(l))
        for n in gs:
            gsmall[n][l] = gs[n]
    return sq, dx, gbig, gsmall


def _pack(arrays, rows_multiple):
    flat = jnp.concatenate([a.reshape(-1).astype(F32) for a in arrays])
    rows = -(-flat.shape[0] // LANES)
    rows = -(-rows // rows_multiple) * rows_multiple
    return jnp.pad(flat, (0, rows * LANES - flat.shape[0])).reshape(rows, LANES)


def _unpack(pack, like):
    flat = pack.reshape(-1)
    out, off = [], 0
    for a in like:
        out.append(flat[off:off + a.size].reshape(a.shape))
        off += a.size
    return out


def _grad_view(g, ax):
    r, c = g.shape
    return g.reshape(1, 2, r // 2, c) if ax == 2 else g.reshape(N_CHIPS, 2, r // (2 * N_CHIPS), c)


def kernel(x, mem, g_mix_pre, w_in, g_vnorm, w_s, b_s, g_mem, w_mem_kv, w_gate, b_gate, w_br_sb, w_br_gm, w_br_xa, w_out, g_mix_post, g_ffn_pre, w_up, conv_w, conv_b, w_down, g_ffn_post, loss_target, m_g_mix_pre, m_w_in, m_g_vnorm, m_w_s, m_b_s, m_g_mem, m_w_mem_kv, m_w_gate, m_b_gate, m_w_br_sb, m_w_br_gm, m_w_br_xa, m_w_out, m_g_mix_post, m_g_ffn_pre, m_w_up, m_conv_w, m_conv_b, m_w_down, m_g_ffn_post, v_g_mix_pre, v_w_in, v_g_vnorm, v_w_s, v_b_s, v_g_mem, v_w_mem_kv, v_w_gate, v_b_gate, v_w_br_sb, v_w_br_gm, v_w_br_xa, v_w_out, v_g_mix_post, v_g_ffn_pre, v_w_up, v_conv_w, v_conv_b, v_w_down, v_g_ffn_post):
    w = dict(g_mix_pre=g_mix_pre, w_in=w_in, g_vnorm=g_vnorm, w_s=w_s, b_s=b_s, g_mem=g_mem, w_mem_kv=w_mem_kv,
             w_gate=w_gate, b_gate=b_gate, w_br_sb=w_br_sb, w_br_gm=w_br_gm, w_br_xa=w_br_xa, w_out=w_out,
             g_mix_post=g_mix_post, g_ffn_pre=g_ffn_pre, w_up=w_up, conv_w=conv_w, conv_b=conv_b, w_down=w_down,
             g_ffn_post=g_ffn_post)
    m = dict(g_mix_pre=m_g_mix_pre, w_in=m_w_in, g_vnorm=m_g_vnorm, w_s=m_w_s, b_s=m_b_s, g_mem=m_g_mem,
             w_mem_kv=m_w_mem_kv, w_gate=m_w_gate, b_gate=m_b_gate, w_br_sb=m_w_br_sb, w_br_gm=m_w_br_gm,
             w_br_xa=m_w_br_xa, w_out=m_w_out, g_mix_post=m_g_mix_post, g_ffn_pre=m_g_ffn_pre, w_up=m_w_up,
             conv_w=m_conv_w, conv_b=m_conv_b, w_down=m_w_down, g_ffn_post=m_g_ffn_post)
    v = dict(g_mix_pre=v_g_mix_pre, w_in=v_w_in, g_vnorm=v_g_vnorm, w_s=v_w_s, b_s=v_b_s, g_mem=v_g_mem,
             w_mem_kv=v_w_mem_kv, w_gate=v_w_gate, b_gate=v_b_gate, w_br_sb=v_w_br_sb, w_br_gm=v_w_br_gm,
             w_br_xa=v_w_br_xa, w_out=v_w_out, g_mix_post=v_g_mix_post, g_ffn_pre=v_g_ffn_pre, w_up=v_w_up,
             conv_w=v_conv_w, conv_b=v_conv_b, w_down=v_w_down, g_ffn_post=v_g_ffn_post)
    n_layers = w_in.shape[0]
    d = x.shape[-1]
    core = lax.axis_index("c").astype(jnp.int32).reshape(1)
    chip = (2 * lax.axis_index("x") + lax.axis_index("y")).astype(jnp.int32).reshape(1)
    small = {n: w[n] for n in SMALL}
    xs, mems, target = x[0], mem[0], loss_target[0]

    conv_w_full = _gather_small(conv_w)

    def as_full(vw, ax):
        return vw.reshape(-1, vw.shape[-1]) if ax == 1 else vw.reshape(vw.shape[0] * vw.shape[1], vw.shape[2])

    stages, token = {}, conv_w_full
    keys = [(l, g) for l in range(n_layers) for g in range(len(FWD_GROUPS))]
    for l, g in keys:
        names = FWD_GROUPS[g]
        ax_g = [BIG_AXIS[n] for n in names]
        cols_g = [w[n].shape[2] for n in names]
        views = [_place_own(w[n], l, chip, ax, token) for n, ax in zip(names, ax_g)]
        sems, views, token = _exchange('start', f"gather_ici_start_{l}_{g}", views, _gather_ici_copies(ax_g, cols_g),
                                       3 * len(names), after=token)
        stages[l, g] = dict(names=names, ax=ax_g, cols=cols_g, views=views, ici=sems, d2d=None, full=None)

    def cross_cores(key, follows):
        st, (l, g) = stages[key], key
        n3 = 3 * len(st['names'])
        views = _exchange('wait', f"gather_ici_wait_{l}_{g}", st['views'], _gather_ici_copies(st['ax'], st['cols']), n3,
                          after=follows, sems=st['ici'])
        st['d2d'], st['views'], tok = _exchange('start', f"gather_d2d_start_{l}_{g}", views,
                                                _gather_d2d_copies(st['ax'], st['cols']), n3, after=views[0])
        return tok

    class Weights:
        def __init__(self, l):
            self.l, self.tok = l, None

        def __call__(self, name, follows):
            if name == 'conv_w':
                return conv_w_full[self.l]
            key = (self.l, [g for g, names in enumerate(FWD_GROUPS) if name in names][0])
            st = stages[key]
            if st['full'] is None:
                if st['d2d'] is None:
                    cross_cores(key, follows)
                views = _exchange('wait', f"gather_d2d_wait_{key[0]}_{key[1]}", st['views'],
                                  _gather_d2d_copies(st['ax'], st['cols']), 3 * len(st['names']), after=follows, sems=st['d2d'])
                st['full'] = {n: as_full(vw, ax) for n, vw, ax in zip(st['names'], views, st['ax'])}
                nxt = keys.index(key) + 1
                if nxt < len(keys) and key != keys[0]:
                    self.tok = cross_cores(keys[nxt], views[0])
            return st['full'][name]

        def token(self):
            return self.tok

    fulls, saved = [], []
    for l in range(n_layers):
        xs, sv = _layer_fwd(xs, mems, Weights(l), small, l, token if l == 0 else None)
        fulls.append({n: stages[l, g]['full'][n] for g, names in enumerate(FWD_GROUPS) for n in names} | {'conv_w': conv_w_full[l]})
        saved.append(sv)
    sq, dx = _loss_head(xs, target)
    loss = lax.psum(0.5 * jnp.sum(sq) / d, ("x", "y", "c"))

    sent = []

    def to_chips(l, g, names, ax_g, bufs, after):
        n = len(names)
        sums = [_sum_halves(dv, th, core, "sum_halves") for dv, th in zip(bufs[:n], bufs[n:])]
        lands = [lax.empty(_grads_recv_shape(sm, ax), sm.dtype) for sm, ax in zip(sums, ax_g)]
        sems, bufs, tok = _exchange('start', f"grads_ici_start_{l}_{g}", sums + lands, _grads_ici_copies(ax_g), 3 * n,
                                    after=sums[0] if after is None else after)
        sent.append((l, g, names, ax_g, bufs, sems))
        return tok

    class Grads:
        def __init__(self, l):
            self.l, self.crossing = l, None

        def __call__(self, g, gb):
            names = BWD_GROUPS[g]
            ax_g = [BIG_AXIS[n] for n in names]
            n = len(names)
            dwvs = [_grad_view(gb[nm], ax) for nm, ax in zip(names, ax_g)]
            lands = [lax.empty((dv.shape[0],) + dv.shape[2:], dv.dtype) for dv in dwvs]
            if g + 1 < len(BWD_GROUPS):
                sems, bufs, tok = _exchange('start', f"grads_d2d_start_{self.l}_{g}", dwvs + lands, _grads_d2d_copies(n), n,
                                            after=gb[names[-1]])
                self.crossing = (g, names, ax_g, bufs, sems)
                return tok
            return to_chips(self.l, g, names, ax_g, _exchange('sync', "grads_d2d", dwvs + lands, _grads_d2d_copies(n), n), None)

        def flush(self, follows):
            if self.crossing is None:
                return None
            (g, names, ax_g, bufs, sems), self.crossing = self.crossing, None
            bufs = _exchange('wait', f"grads_d2d_wait_{self.l}_{g}", bufs, _grads_d2d_copies(len(names)), len(names),
                             after=follows, sems=sems)
            return to_chips(self.l, g, names, ax_g, bufs, None)

    gsmall = {n: [None] * n_layers for n in SMALL + ['conv_w']}
    for l in reversed(range(n_layers)):
        dx, gs = _layer_bwd(dx, mems, saved[l], fulls[l], small, l, None, Grads(l))
        for n in gs:
            gsmall[n][l] = gs[n]

    def small_update(follows):
        small_full = [jnp.stack(gsmall[n]).reshape(w[n].shape) for n in SMALL]
        conv_w_grad = jnp.stack(gsmall['conv_w'])
        summed = _all_reduce_small(_pack(small_full + [conv_w_grad], 8), follows)
        *small_g, conv_w_g = _unpack(summed, small_full + [conv_w_grad])
        shard = conv_w.shape[-1]
        conv_w_g = lax.dynamic_slice_in_dim(conv_w_g, chip[0] * shard, shard, axis=2)
        names = SMALL + ['conv_w']
        packed = [_pack([p[n] for n in names], 256) for p in (w, m, v)]
        gpack = _pack(small_g + [conv_w_g], 256)
        res = _adamw(packed[0][None], packed[1][None], packed[2][None], gpack, 0, None, "adamw_small")
        like = [w[n] for n in names]
        unpacked = [_unpack(r[0], like) for r in res]
        return {n: tuple(u[i] for u in unpacked) for i, n in enumerate(names)}

    out = {}

    def update(joining, follows):
        l, g, names, halves, sems = joining
        halves = _exchange('wait', f"join_wait_{l}_{g}", halves, _join_copies, len(names), after=follows, sems=sems)
        last = None
        for n, hv in zip(names, halves):
            out[n] = _adamw(w[n], m[n], v[n], hv.reshape(w[n].shape[1:]), l, out.get(n), "adamw_big", last)
            last = out[n][0]
        return last

    joining, follows = None, dx
    for k, (l, g, names, ax_g, bufs, sems) in enumerate(sent):
        if k == len(sent) - 1:
            out.update(small_update(follows))
        n = len(names)
        bufs = _exchange('wait', f"grads_ici_wait_{l}_{g}", bufs, _grads_ici_copies(ax_g), 3 * n, after=follows, sems=sems)
        halves = [_sum_chips(r, sm, chip, core, ax) for sm, r, ax in zip(bufs[:n], bufs[n:], ax_g)]
        jsems, halves, tok = _exchange('start', f"join_start_{l}_{g}", halves, _join_copies, n, after=halves[-1])
        if joining is not None:
            follows = update(joining, tok)
        joining = (l, g, names, halves, jsems)
    update(joining, follows)

    return (loss, dx[None], *[out[n][0] for n in WEIGHTS], *[out[n][1] for n in WEIGHTS],
            *[out[n][2] for n in WEIGHTS], *[out[n][3] for n in WEIGHTS])
```

```python
import functools
import math

import jax
import jax.numpy as jnp
from jax import lax
from jax.experimental import pallas as pl
from jax.experimental.pallas import tpu as pltpu

F32 = jnp.float32
BF16 = jnp.bfloat16
EPS = 1e-6
HEAD_SB = 128
GROUP_GM = 128
CHUNK = 64
HEAD_XA = 256
CONV_TAPS = 3
N_CHIPS = 4
N_DEV = 8
LANES = 128
MIB = 1024 * 1024
VMEM_LIMIT = 48 * MIB
SPLITS = 2

ADAM_LR = 0.001
ADAM_B1 = 0.9
ADAM_B2 = 0.999
ADAM_EPS = 1e-08
ADAM_WD = 0.01
ADAM_STEP = 10

WEIGHTS = ['g_mix_pre', 'w_in', 'g_vnorm', 'w_s', 'b_s', 'g_mem', 'w_mem_kv', 'w_gate', 'b_gate', 'w_br_sb',
           'w_br_gm', 'w_br_xa', 'w_out', 'g_mix_post', 'g_ffn_pre', 'w_up', 'conv_w', 'conv_b', 'w_down',
           'g_ffn_post']
BIG_AXIS = {'w_in': 2, 'w_mem_kv': 1, 'w_gate': 2, 'w_br_sb': 2, 'w_br_gm': 2, 'w_br_xa': 2, 'w_out': 1,
            'w_up': 2, 'w_down': 1}
BIG = list(BIG_AXIS)
FWD_GROUPS = [['w_in'], ['w_mem_kv', 'w_gate'], ['w_br_sb', 'w_br_gm', 'w_br_xa', 'w_out'], ['w_up'], ['w_down']]
BWD_GROUPS = [['w_down', 'w_up'], ['w_out', 'w_br_sb', 'w_br_gm', 'w_br_xa', 'w_gate', 'w_mem_kv'], ['w_in']]
SMALL = ['g_mix_pre', 'g_vnorm', 'w_s', 'b_s', 'g_mem', 'b_gate', 'g_mix_post', 'g_ffn_pre', 'conv_b', 'g_ffn_post']
MESH = pl.DeviceIdType.MESH


def _pcall(body, **kw):
    return pl.pallas_call(body, **kw)


def _params(sem=None, vmem=VMEM_LIMIT):
    return pltpu.CompilerParams(dimension_semantics=sem, vmem_limit_bytes=vmem)


def _tile(n, cands):
    for c in cands:
        if n % c == 0:
            return c
    return n


_GELU_C = math.sqrt(2.0 / math.pi)
_GELU_A = 0.044715


def _gelu(x):
    return 0.5 * x * (1.0 + jnp.tanh(_GELU_C * (x + _GELU_A * (x * x * x))))


def _gelu_and_grad(x):
    x2 = x * x
    t = jnp.tanh(_GELU_C * (x + _GELU_A * (x2 * x)))
    val = 0.5 * x * (1.0 + t)
    grad = 0.5 * (1.0 + t) + 0.5 * x * (1.0 - t * t) * (_GELU_C * (1.0 + 3.0 * _GELU_A * x2))
    return val, grad


def _softplus(z):
    return jnp.maximum(z, 0.0) + jnp.log(1.0 + jnp.exp(-jnp.abs(z)))


def _dot(a, b):
    return jnp.dot(a, b, preferred_element_type=F32)


def _dot_nt(a, b):
    return lax.dot_general(a, b, (((1,), (1,)), ((), ())), preferred_element_type=F32)


def _dot_tn(a, b):
    return lax.dot_general(a, b, (((0,), (0,)), ((), ())), preferred_element_type=F32)


def _split_dot(a, m):
    out = None
    rest = a
    for _ in range(SPLITS):
        piece = rest.astype(BF16)
        rest = rest - piece.astype(F32)
        term = _dot(piece, m)
        out = term if out is None else out + term
    return out


def _mm(a, b, mode, out_dtype, name, tm=None, tn=None, tk=None, after=None):
    if mode == 'nn':
        (m, kc), (kc2, n) = a.shape, b.shape
    elif mode == 'nt':
        (m, kc), (n, kc2) = a.shape, b.shape
    else:
        (kc, m), (kc2, n) = a.shape, b.shape
    assert kc == kc2, (a.shape, b.shape, mode)
    tm = tm or _tile(m, (1024, 512, 256, 128))
    tn = tn or _tile(n, (1024, 512, 256, 128))
    tk = tk or (kc if kc <= 3072 else _tile(kc, (3072, 2816, 2048, 1536, 1408, 1024, 512)))
    nk = kc // tk
    dot = {'nn': _dot, 'nt': _dot_nt, 'tn': _dot_tn}[mode]
    a_spec = pl.BlockSpec((tk, tm), lambda i, j, k: (k, i)) if mode == 'tn' else pl.BlockSpec((tm, tk), lambda i, j, k: (i, k))
    b_spec = pl.BlockSpec((tn, tk), lambda i, j, k: (j, k)) if mode == 'nt' else pl.BlockSpec((tk, tn), lambda i, j, k: (k, j))

    extra = [] if after is None else [after]
    extra_specs = [pl.BlockSpec(memory_space=pl.ANY)] * len(extra)

    if nk == 1:
        def body(a_ref, b_ref, *rest):
            o_ref = rest[-1]
            o_ref[...] = dot(a_ref[...].astype(BF16), b_ref[...].astype(BF16)).astype(o_ref.dtype)
        scratch = []
    else:
        def body(a_ref, b_ref, *rest):
            o_ref, acc_ref = rest[-2], rest[-1]
            k = pl.program_id(2)
            part = dot(a_ref[...].astype(BF16), b_ref[...].astype(BF16))

            @pl.when(k == 0)
            def _():
                acc_ref[...] = part

            @pl.when(k > 0)
            def _():
                acc_ref[...] += part

            @pl.when(k == nk - 1)
            def _():
                o_ref[...] = acc_ref[...].astype(o_ref.dtype)
        scratch = [pltpu.VMEM((tm, tn), F32)]

    return _pcall(
        body, grid=(m // tm, n // tn, nk), in_specs=[a_spec, b_spec] + extra_specs,
        out_specs=pl.BlockSpec((tm, tn), lambda i, j, k: (i, j)),
        out_shape=jax.ShapeDtypeStruct((m, n), out_dtype), scratch_shapes=scratch, name=name,
        compiler_params=_params(("parallel", "parallel", "arbitrary")))(a, b, *extra)


def _norm_fwd(x, g, res, out_dtype, name, after=None):
    s, d = x.shape
    tr = _tile(s, (256, 128))
    has_res = res is not None
    has_after = after is not None

    def body(*refs):
        x_ref, g_ref = refs[0], refs[1]
        o_ref = refs[-1]
        xv = x_ref[...]
        y = xv * lax.rsqrt(jnp.mean(xv * xv, axis=-1, keepdims=True) + EPS) * g_ref[...]
        if has_res:
            y = y + refs[2][...]
        o_ref[...] = y.astype(o_ref.dtype)

    row = pl.BlockSpec((tr, d), lambda i: (i, 0))
    ins = [x, g] + ([res] if has_res else []) + ([after] if has_after else [])
    return _pcall(
        body, grid=(s // tr,),
        in_specs=[row, pl.BlockSpec((1, d), lambda i: (0, 0))] + ([row] if has_res else [])
        + ([pl.BlockSpec(memory_space=pl.ANY)] if has_after else []),
        out_specs=row, out_shape=jax.ShapeDtypeStruct((s, d), out_dtype), name=name,
        compiler_params=_params(("parallel",)))(*ins)


def _norm_bwd(x, g, douts, dres, out_dtype, name, after=None):
    s, d = x.shape
    tr = _tile(s, (256, 128))
    nd = len(douts)
    has_res = dres is not None
    has_after = after is not None

    def body(*refs):
        x_ref, g_ref = refs[0], refs[1]
        dx_ref, dg_ref = refs[-2], refs[-1]
        dout = refs[2][...].astype(F32)
        for r in refs[3:2 + nd]:
            dout = dout + r[...].astype(F32)
        xv = x_ref[...]
        r = lax.rsqrt(jnp.mean(xv * xv, axis=-1, keepdims=True) + EPS)
        n = xv * r
        dn = dout * g_ref[...]
        dx = r * (dn - n * jnp.mean(dn * n, axis=-1, keepdims=True))
        if has_res:
            dx = dx + refs[2 + nd][...]
        dx_ref[...] = dx.astype(dx_ref.dtype)

        @pl.when(pl.program_id(0) == 0)
        def _():
            dg_ref[...] = jnp.zeros_like(dg_ref)

        dg_ref[...] += jnp.sum(dout * n, axis=0, keepdims=True)

    row = pl.BlockSpec((tr, d), lambda i: (i, 0))
    vec = pl.BlockSpec((1, d), lambda i: (0, 0))
    ins = [x, g] + list(douts) + ([dres] if has_res else []) + ([after] if has_after else [])
    return _pcall(
        body, grid=(s // tr,),
        in_specs=[row, vec] + [row] * (nd + int(has_res)) + ([pl.BlockSpec(memory_space=pl.ANY)] if has_after else []),
        out_specs=[row, vec],
        out_shape=[jax.ShapeDtypeStruct((s, d), out_dtype), jax.ShapeDtypeStruct((1, d), F32)], name=name,
        compiler_params=_params(("arbitrary",)))(*ins)


def _loss_head(y, target):
    s, d = y.shape
    tr = _tile(s, (256, 128))

    def body(y_ref, t_ref, sq_ref, dy_ref):
        e = y_ref[...] - t_ref[...]
        dy_ref[...] = e * (1.0 / d)

        @pl.when(pl.program_id(0) == 0)
        def _():
            sq_ref[...] = jnp.zeros_like(sq_ref)

        sq_ref[...] += jnp.sum(e * e, axis=0, keepdims=True)

    row = pl.BlockSpec((tr, d), lambda i: (i, 0))
    return _pcall(
        body, grid=(s // tr,), in_specs=[row, row], out_specs=[pl.BlockSpec((1, d), lambda i: (0, 0)), row],
        out_shape=[jax.ShapeDtypeStruct((1, d), F32), jax.ShapeDtypeStruct((s, d), F32)], name="loss_head",
        compiler_params=_params(("arbitrary",)))(y, target)


NEVER = -1e30
SB_QUERIES = 512


def _sb_sum_matrix(later):
    r = lax.broadcasted_iota(jnp.int32, (HEAD_SB, 2 * HEAD_SB), 0)
    c = lax.broadcasted_iota(jnp.int32, (HEAD_SB, 2 * HEAD_SB), 1)
    tri = jnp.where((r > c) if later else (r < c), 1.0, 0.0)
    return jnp.where(c < HEAD_SB, tri, 1.0).astype(BF16)


def _sb_mask(tq, q0, k0):
    row = lax.broadcasted_iota(jnp.int32, (tq, HEAD_SB), 0)
    col = lax.broadcasted_iota(jnp.int32, (tq, HEAD_SB), 1)
    return (k0 + col) < (q0 + row)


def _sb_fwd(proj, n_heads):
    s = proj.shape[0]
    tq = min(SB_QUERIES, s)
    per = tq // HEAD_SB
    scale = HEAD_SB ** -0.5

    def body(q_ref, k_ref, v_ref, o_ref, a_ref, acc_ref, c_ref):
        i = pl.program_id(1)
        q = q_ref[...].astype(BF16)
        sums = _sb_sum_matrix(True)
        acc_ref[...] = jnp.zeros_like(acc_ref)
        c_ref[...] = jnp.zeros_like(c_ref)
        last = (i + 1) * per - 1

        def scores(j, masked):
            off = pl.multiple_of(j * HEAD_SB, HEAD_SB)
            z = _dot_nt(q, k_ref[pl.ds(off, HEAD_SB), :].astype(BF16)) * scale
            sp = _softplus(z)
            logb = z - sp
            if masked:
                mask = _sb_mask(tq, i * tq, off)
                logb = jnp.where(mask, logb, NEVER)
                sp = jnp.where(mask, sp, 0.0)
            return logb, _split_dot(sp, sums)

        def values(j, logb, both):
            off = pl.multiple_of(j * HEAD_SB, HEAD_SB)
            c = c_ref[...]
            a = jnp.exp(logb - both[:, :HEAD_SB] - c).astype(BF16)
            a_ref[0, 0, j] = a
            acc_ref[...] += _dot(a, v_ref[pl.ds(off, HEAD_SB), :].astype(BF16))
            c_ref[...] = c + both[:, HEAD_SB:]

        def step(jj, carry, masked):
            j = last - jj
            nxt = scores(j, masked)
            values(jnp.minimum(j + 1, last), *carry)
            return nxt

        idle = (jnp.full((tq, HEAD_SB), NEVER, F32), jnp.zeros((tq, 2 * HEAD_SB), F32))
        carry = lax.fori_loop(0, per, functools.partial(step, masked=True), idle)
        carry = lax.fori_loop(per, last + 1, functools.partial(step, masked=False), carry)
        values(0, *carry)
        o_ref[...] = acc_ref[...].astype(o_ref.dtype)

    h = n_heads
    blk = pl.BlockSpec((tq, HEAD_SB), lambda hh, i: (i, hh))
    return _pcall(
        body, grid=(h, s // tq),
        in_specs=[blk, pl.BlockSpec((s, HEAD_SB), lambda hh, i: (0, h + hh)),
                  pl.BlockSpec((s, HEAD_SB), lambda hh, i: (0, 2 * h + hh))],
        out_specs=[blk, pl.BlockSpec((1, 1, s // HEAD_SB, tq, HEAD_SB), lambda hh, i: (hh, i, 0, 0, 0))],
        out_shape=[jax.ShapeDtypeStruct((s, h * HEAD_SB), BF16),
                   jax.ShapeDtypeStruct((h, s // tq, s // HEAD_SB, tq, HEAD_SB), BF16)],
        scratch_shapes=[pltpu.VMEM((tq, HEAD_SB), F32), pltpu.VMEM((tq, HEAD_SB), F32)],
        name="sb_fwd", compiler_params=_params(("parallel", "arbitrary")))(proj, proj, proj)


def _sb_bwd(proj, a_saved, do, n_heads, after=None):
    s = proj.shape[0]
    tq = min(SB_QUERIES, s)
    per = tq // HEAD_SB
    scale = HEAD_SB ** -0.5
    follow = [] if after is None else [after]

    def body(q_ref, k_ref, v_ref, do_ref, a_ref, *rest):
        dq_ref, dk_ref, dv_ref, run_ref, acc_ref = rest[len(follow):]
        i = pl.program_id(1)

        @pl.when(i == 0)
        def _():
            dk_ref[...] = jnp.zeros_like(dk_ref)
            dv_ref[...] = jnp.zeros_like(dv_ref)

        q = q_ref[...].astype(BF16)
        dob = do_ref[...].astype(BF16)
        run_ref[...] = jnp.zeros_like(run_ref)
        acc_ref[...] = jnp.zeros_like(acc_ref)
        earlier = _sb_sum_matrix(False)
        first_diagonal = i * per

        def step(j, masked):
            off = pl.multiple_of(j * HEAD_SB, HEAD_SB)
            kb = k_ref[pl.ds(off, HEAD_SB), :].astype(BF16)
            vb = v_ref[pl.ds(off, HEAD_SB), :].astype(BF16)
            a = a_ref[0, 0, j]
            g = a.astype(F32) * _dot_nt(dob, vb)
            dv_ref[pl.ds(off, HEAD_SB), :] += _dot_tn(a, dob)
            z = _dot_nt(q, kb) * scale
            beta = 1.0 / (1.0 + jnp.exp(-z))
            both = _split_dot(g, earlier)
            p = run_ref[...]
            dz = (g * (1.0 - beta) - beta * (both[:, :HEAD_SB] + p)) * scale
            if masked:
                dz = jnp.where(_sb_mask(tq, i * tq, off), dz, 0.0)
            dzb = dz.astype(BF16)
            dk_ref[pl.ds(off, HEAD_SB), :] += _dot_tn(dzb, q)
            acc_ref[...] += _dot(dzb, kb)
            run_ref[...] = p + both[:, HEAD_SB:]

        def below(j, carry):
            step(j, False)
            return carry

        def diagonal(j, carry):
            step(j, True)
            return carry

        lax.fori_loop(0, first_diagonal, below, 0)
        lax.fori_loop(first_diagonal, first_diagonal + per, diagonal, 0)
        dq_ref[...] = acc_ref[...]

    h = n_heads
    blk = pl.BlockSpec((tq, HEAD_SB), lambda hh, i: (i, hh))
    col_blk = pl.BlockSpec((s, HEAD_SB), lambda hh, i: (0, hh))
    shape = jax.ShapeDtypeStruct((s, h * HEAD_SB), F32)
    return _pcall(
        body, grid=(h, s // tq),
        in_specs=[blk, pl.BlockSpec((s, HEAD_SB), lambda hh, i: (0, h + hh)),
                  pl.BlockSpec((s, HEAD_SB), lambda hh, i: (0, 2 * h + hh)), blk,
                  pl.BlockSpec((1, 1, s // HEAD_SB, tq, HEAD_SB), lambda hh, i: (hh, i, 0, 0, 0))]
        + [pl.BlockSpec(memory_space=pl.ANY)] * len(follow),
        out_specs=[blk, col_blk, col_blk], out_shape=[shape, shape, shape],
        scratch_shapes=[pltpu.VMEM((tq, HEAD_SB), F32), pltpu.VMEM((tq, HEAD_SB), F32)],
        name="sb_bwd", compiler_params=_params(("parallel", "arbitrary")))(proj, proj, proj, do, a_saved, *follow)


def _gm_mask():
    t = lax.broadcasted_iota(jnp.int32, (GROUP_GM, GROUP_GM), 0)
    s = lax.broadcasted_iota(jnp.int32, (GROUP_GM, GROUP_GM), 1)
    shift = CHUNK.bit_length() - 1
    return (s >> shift) <= (t >> shift)


def _gm_fwd(proj, g_vnorm, w_s, b_st, u_blk):
    s = proj.shape[0]
    groups = w_s.shape[0]
    w = groups * GROUP_GM

    def body(u_ref, v_ref, gv_ref, ws_ref, bst_ref, o_ref):
        ug = _gelu(u_ref[...])
        vg = _gelu(v_ref[...])
        vn = vg * lax.rsqrt(jnp.mean(vg * vg, axis=-1, keepdims=True) + EPS) * gv_ref[...]
        vnb = vn.astype(BF16)
        mask = _gm_mask()
        for g in range(groups):
            sl = slice(g * GROUP_GM, (g + 1) * GROUP_GM)
            wm = jnp.where(mask, ws_ref[g], 0.0).astype(BF16)
            mixed = _dot(wm, vnb[:, sl]) + bst_ref[:, g:g + 1]
            o_ref[:, sl] = (ug[:, sl] * mixed).astype(o_ref.dtype)

    return _pcall(
        body, grid=(s // GROUP_GM,),
        in_specs=[pl.BlockSpec((GROUP_GM, w), lambda c: (c, u_blk)), pl.BlockSpec((GROUP_GM, w), lambda c: (c, u_blk + 1)),
                  pl.BlockSpec((1, w), lambda c: (0, 0)), pl.BlockSpec((groups, GROUP_GM, GROUP_GM), lambda c: (0, 0, 0)),
                  pl.BlockSpec((GROUP_GM, groups), lambda c: (0, 0))],
        out_specs=pl.BlockSpec((GROUP_GM, w), lambda c: (c, 0)),
        out_shape=jax.ShapeDtypeStruct((s, w), BF16), name="gm_fwd",
        compiler_params=_params(("parallel",)))(proj, proj, g_vnorm, w_s, b_st)


def _gm_bwd(proj, g_vnorm, w_s, b_st, do, u_blk):
    s = proj.shape[0]
    groups = w_s.shape[0]
    w = groups * GROUP_GM

    def body(u_ref, v_ref, gv_ref, ws_ref, bst_ref, do_ref, du_ref, dv_ref, dgv_ref, dws_ref, dbst_ref, dvn_ref):
        @pl.when(pl.program_id(0) == 0)
        def _():
            dgv_ref[...] = jnp.zeros_like(dgv_ref)
            dws_ref[...] = jnp.zeros_like(dws_ref)
            dbst_ref[...] = jnp.zeros_like(dbst_ref)

        ug, ugrad = _gelu_and_grad(u_ref[...])
        vg, vgrad = _gelu_and_grad(v_ref[...])
        r = lax.rsqrt(jnp.mean(vg * vg, axis=-1, keepdims=True) + EPS)
        n = vg * r
        gv = gv_ref[...]
        vnb = (n * gv).astype(BF16)
        dout = do_ref[...]
        mask = _gm_mask()
        for g in range(groups):
            sl = slice(g * GROUP_GM, (g + 1) * GROUP_GM)
            wm = jnp.where(mask, ws_ref[g], 0.0).astype(BF16)
            mixed = _dot(wm, vnb[:, sl]) + bst_ref[:, g:g + 1]
            dmixed = dout[:, sl] * ug[:, sl]
            du_ref[:, sl] = dout[:, sl] * mixed * ugrad[:, sl]
            dbst_ref[:, g:g + 1] += jnp.sum(dmixed, axis=1, keepdims=True)
            dmb = dmixed.astype(BF16)
            dws_ref[g] += jnp.where(mask, _dot_nt(dmb, vnb[:, sl]), 0.0)
            dvn_ref[:, sl] = _dot_tn(wm, dmb)
        dvn = dvn_ref[...]
        dgv_ref[...] += jnp.sum(dvn * n, axis=0, keepdims=True)
        dn = dvn * gv
        dvg = r * (dn - n * jnp.mean(dn * n, axis=-1, keepdims=True))
        dv_ref[...] = dvg * vgrad

    rowb = pl.BlockSpec((GROUP_GM, w), lambda c: (c, 0))
    vec = pl.BlockSpec((1, w), lambda c: (0, 0))
    wsb = pl.BlockSpec((groups, GROUP_GM, GROUP_GM), lambda c: (0, 0, 0))
    bsb = pl.BlockSpec((GROUP_GM, groups), lambda c: (0, 0))
    return _pcall(
        body, grid=(s // GROUP_GM,),
        in_specs=[pl.BlockSpec((GROUP_GM, w), lambda c: (c, u_blk)), pl.BlockSpec((GROUP_GM, w), lambda c: (c, u_blk + 1)),
                  vec, wsb, bsb, rowb],
        out_specs=[rowb, rowb, vec, wsb, bsb],
        out_shape=[jax.ShapeDtypeStruct((s, w), F32), jax.ShapeDtypeStruct((s, w), F32), jax.ShapeDtypeStruct((1, w), F32),
                   jax.ShapeDtypeStruct((groups, GROUP_GM, GROUP_GM), F32), jax.ShapeDtypeStruct((GROUP_GM, groups), F32)],
        scratch_shapes=[pltpu.VMEM((GROUP_GM, w), F32)], name="gm_bwd",
        compiler_params=_params(("arbitrary",)))(proj, proj, g_vnorm, w_s, b_st, do)


def _xa_fwd(proj, mem_kv, q_blk, n_heads):
    s = proj.shape[0]
    nm = mem_kv.shape[0]
    tq = _tile(s, (512, 256, 128))
    scale = HEAD_XA ** -0.5

    def body(q_ref, k_ref, v_ref, o_ref):
        z = _dot_nt(q_ref[...].astype(BF16), k_ref[...].astype(BF16)) * scale
        z = z - jnp.max(z, axis=-1, keepdims=True)
        e = jnp.exp(z)
        p = e / jnp.sum(e, axis=-1, keepdims=True)
        o_ref[...] = _dot(p.astype(BF16), v_ref[...].astype(BF16)).astype(o_ref.dtype)

    h = n_heads
    return _pcall(
        body, grid=(h, s // tq),
        in_specs=[pl.BlockSpec((tq, HEAD_XA), lambda hh, i: (i, q_blk + hh)),
                  pl.BlockSpec((nm, HEAD_XA), lambda hh, i: (0, hh)), pl.BlockSpec((nm, HEAD_XA), lambda hh, i: (0, h + hh))],
        out_specs=pl.BlockSpec((tq, HEAD_XA), lambda hh, i: (i, hh)),
        out_shape=jax.ShapeDtypeStruct((s, h * HEAD_XA), BF16), name="xa_fwd",
        compiler_params=_params(("parallel", "parallel")))(proj, mem_kv, mem_kv)


def _xa_bwd(proj, mem_kv, do, q_blk, n_heads):
    s = proj.shape[0]
    nm = mem_kv.shape[0]
    tq = _tile(s, (512, 256, 128))
    scale = HEAD_XA ** -0.5
    h = n_heads

    def body(q_ref, k_ref, v_ref, do_ref, dq_ref, dk_ref, dv_ref):
        @pl.when(pl.program_id(1) == 0)
        def _():
            dk_ref[...] = jnp.zeros_like(dk_ref)
            dv_ref[...] = jnp.zeros_like(dv_ref)

        qb = q_ref[...].astype(BF16)
        kb = k_ref[...].astype(BF16)
        vb = v_ref[...].astype(BF16)
        dob = do_ref[...].astype(BF16)
        z = _dot_nt(qb, kb) * scale
        z = z - jnp.max(z, axis=-1, keepdims=True)
        e = jnp.exp(z)
        p = e / jnp.sum(e, axis=-1, keepdims=True)
        dp = _dot_nt(dob, vb)
        dz = (p * (dp - jnp.sum(dp * p, axis=-1, keepdims=True)) * scale).astype(BF16)
        dq_ref[...] = _dot(dz, kb)
        dk_ref[...] += _dot_tn(dz, qb)
        dv_ref[...] += _dot_tn(p.astype(BF16), dob)

    qspec = pl.BlockSpec((tq, HEAD_XA), lambda hh, i: (i, hh))
    dk, dv = None, None
    dq, dk, dv = _pcall(
        body, grid=(h, s // tq),
        in_specs=[pl.BlockSpec((tq, HEAD_XA), lambda hh, i: (i, q_blk + hh)),
                  pl.BlockSpec((nm, HEAD_XA), lambda hh, i: (0, hh)), pl.BlockSpec((nm, HEAD_XA), lambda hh, i: (0, h + hh)),
                  qspec],
        out_specs=[qspec, pl.BlockSpec((nm, HEAD_XA), lambda hh, i: (0, hh)), pl.BlockSpec((nm, HEAD_XA), lambda hh, i: (0, hh))],
        out_shape=[jax.ShapeDtypeStruct((s, h * HEAD_XA), F32), jax.ShapeDtypeStruct((nm, h * HEAD_XA), F32),
                   jax.ShapeDtypeStruct((nm, h * HEAD_XA), F32)],
        name="xa_bwd", compiler_params=_params(("parallel", "arbitrary")))(proj, mem_kv, mem_kv, do)
    return dq, dk, dv


def _merge_fwd(zg, b_gate, branches):
    s, d = branches[0].shape
    tr = _tile(s, (128,))

    def body(z0, z1, z2, g0, g1, g2, b0, b1, b2, o_ref):
        acc = None
        for z, g, b in ((z0, g0, b0), (z1, g1, b1), (z2, g2, b2)):
            term = jax.nn.sigmoid(z[...] + g[...]) * b[...]
            acc = term if acc is None else acc + term
        o_ref[...] = acc.astype(o_ref.dtype)

    zs = [pl.BlockSpec((tr, d), functools.partial(lambda i, k: (i, k), k=k)) for k in range(3)]
    gs = [pl.BlockSpec((1, d), functools.partial(lambda i, k: (0, k), k=k)) for k in range(3)]
    row = pl.BlockSpec((tr, d), lambda i: (i, 0))
    return _pcall(
        body, grid=(s // tr,), in_specs=zs + gs + [row] * 3, out_specs=row,
        out_shape=jax.ShapeDtypeStruct((s, d), BF16), name="merge_fwd",
        compiler_params=_params(("parallel",)))(zg, zg, zg, b_gate, b_gate, b_gate, *branches)


def _merge_bwd(zg, b_gate, branches, dmerged):
    s, d = branches[0].shape
    tr = _tile(s, (128,))

    def body(z0, z1, z2, g0, g1, g2, b0, b1, b2, dm_ref, dz_ref, d0, d1, d2, dbg_ref):
        @pl.when(pl.program_id(0) == 0)
        def _():
            dbg_ref[...] = jnp.zeros_like(dbg_ref)

        dm = dm_ref[...]
        for k, (z, g, b, dbr) in enumerate(((z0, g0, b0, d0), (z1, g1, b1, d1), (z2, g2, b2, d2))):
            sg = jax.nn.sigmoid(z[...] + g[...])
            dbr[...] = (dm * sg).astype(dbr.dtype)
            dz = dm * b[...] * sg * (1.0 - sg)
            dz_ref[:, k * d:(k + 1) * d] = dz.astype(dz_ref.dtype)
            dbg_ref[:, k * d:(k + 1) * d] += jnp.sum(dz, axis=0, keepdims=True)

    zs = [pl.BlockSpec((tr, d), functools.partial(lambda i, k: (i, k), k=k)) for k in range(3)]
    gs = [pl.BlockSpec((1, d), functools.partial(lambda i, k: (0, k), k=k)) for k in range(3)]
    row = pl.BlockSpec((tr, d), lambda i: (i, 0))
    outs = _pcall(
        body, grid=(s // tr,), in_specs=zs + gs + [row] * 4,
        out_specs=[pl.BlockSpec((tr, 3 * d), lambda i: (i, 0)), row, row, row, pl.BlockSpec((1, 3 * d), lambda i: (0, 0))],
        out_shape=[jax.ShapeDtypeStruct((s, 3 * d), BF16)] + [jax.ShapeDtypeStruct((s, d), BF16)] * 3
        + [jax.ShapeDtypeStruct((1, 3 * d), F32)],
        name="merge_bwd", compiler_params=_params(("arbitrary",)))(zg, zg, zg, b_gate, b_gate, b_gate, *branches, dmerged)
    return outs[0], list(outs[1:4]), outs[4]


def _shift_down(x, k, row):
    return jnp.where(row >= k, pltpu.roll(x, k, 0), 0.0)


def _shift_up(x, k, row, s):
    return jnp.where(row < s - k, pltpu.roll(x, s - k, 0), 0.0)


def _conv_pre(gate, cw_ref, cb_ref, row):
    conv = cb_ref[...] + cw_ref[CONV_TAPS - 1:CONV_TAPS, :] * gate
    for k in range(1, CONV_TAPS):
        conv = conv + cw_ref[CONV_TAPS - 1 - k:CONV_TAPS - k, :] * _shift_down(gate, k, row)
    return conv


def _cg_fwd(up, conv_w, conv_b):
    s = up.shape[0]
    f = conv_w.shape[1]
    tc = _tile(f, (256, 128))
    nb = f // tc

    def body(g_ref, v_ref, cw_ref, cb_ref, o_ref):
        row = lax.broadcasted_iota(jnp.int32, (s, tc), 0)
        conv = _conv_pre(g_ref[...], cw_ref, cb_ref, row)
        o_ref[...] = (_gelu(conv) * v_ref[...]).astype(o_ref.dtype)

    return _pcall(
        body, grid=(nb,),
        in_specs=[pl.BlockSpec((s, tc), lambda j: (0, j)), pl.BlockSpec((s, tc), lambda j: (0, nb + j)),
                  pl.BlockSpec((CONV_TAPS, tc), lambda j: (0, j)), pl.BlockSpec((1, tc), lambda j: (0, j))],
        out_specs=pl.BlockSpec((s, tc), lambda j: (0, j)),
        out_shape=jax.ShapeDtypeStruct((s, f), BF16), name="cg_fwd",
        compiler_params=_params(("parallel",)))(up, up, conv_w, conv_b)


def _cg_bwd(up, conv_w, conv_b, dact):
    s = up.shape[0]
    f = conv_w.shape[1]
    tc = _tile(f, (256, 128))
    nb = f // tc

    def body(g_ref, v_ref, cw_ref, cb_ref, da_ref, dg_ref, dv_ref, dcw_ref, dcb_ref):
        row = lax.broadcasted_iota(jnp.int32, (s, tc), 0)
        gate = g_ref[...]
        conv = _conv_pre(gate, cw_ref, cb_ref, row)
        gel, ggrad = _gelu_and_grad(conv)
        da = da_ref[...]
        dv_ref[...] = (da * gel).astype(dv_ref.dtype)
        dconv = da * v_ref[...] * ggrad
        dgate = cw_ref[CONV_TAPS - 1:CONV_TAPS, :] * dconv
        dcw_ref[CONV_TAPS - 1:CONV_TAPS, :] = jnp.sum(dconv * gate, axis=0, keepdims=True)
        for k in range(1, CONV_TAPS):
            dgate = dgate + cw_ref[CONV_TAPS - 1 - k:CONV_TAPS - k, :] * _shift_up(dconv, k, row, s)
            dcw_ref[CONV_TAPS - 1 - k:CONV_TAPS - k, :] = jnp.sum(dconv * _shift_down(gate, k, row), axis=0, keepdims=True)
        dg_ref[...] = dgate.astype(dg_ref.dtype)
        dcb_ref[...] = jnp.sum(dconv, axis=0, keepdims=True)

    colb = pl.BlockSpec((s, tc), lambda j: (0, j))
    return _pcall(
        body, grid=(nb,),
        in_specs=[colb, pl.BlockSpec((s, tc), lambda j: (0, nb + j)), pl.BlockSpec((CONV_TAPS, tc), lambda j: (0, j)),
                  pl.BlockSpec((1, tc), lambda j: (0, j)), colb],
        out_specs=[colb, colb, pl.BlockSpec((CONV_TAPS, tc), lambda j: (0, j)), pl.BlockSpec((1, tc), lambda j: (0, j))],
        out_shape=[jax.ShapeDtypeStruct((s, f), BF16), jax.ShapeDtypeStruct((s, f), BF16),
                   jax.ShapeDtypeStruct((CONV_TAPS, f), F32), jax.ShapeDtypeStruct((1, f), F32)],
        name="cg_bwd", compiler_params=_params(("parallel",)))(up, up, conv_w, conv_b, dact)


def _row_tile(rows, cols, elems=256 * 1024):
    want = max(16, elems // cols)
    for c in (512, 256, 128, 64, 32, 16):
        if c <= want and rows % c == 0:
            return c
    return rows


def _sum_halves(dwv, recv, core, name):
    nj, _, a, c = dwv.shape
    tr = _row_tile(a, c, 1024 * 1024)

    def body(core_ref, d_ref, r_ref, o_ref):
        o_ref[0] = (d_ref[0, 0].astype(F32) + r_ref[0].astype(F32)).astype(o_ref.dtype)

    grid_spec = pltpu.PrefetchScalarGridSpec(
        num_scalar_prefetch=1, grid=(nj, a // tr),
        in_specs=[pl.BlockSpec((1, 1, tr, c), lambda j, i, cr: (j, cr[0], i, 0)),
                  pl.BlockSpec((1, tr, c), lambda j, i, cr: (j, i, 0))],
        out_specs=pl.BlockSpec((1, tr, c), lambda j, i, cr: (j, i, 0)))
    return _pcall(body, grid_spec=grid_spec, out_shape=jax.ShapeDtypeStruct((nj, a, c), BF16), name=name,
                  compiler_params=_params(("parallel", "parallel")))(core, dwv, recv)


def _sum_chips(recv, own, chip, core, ax):
    _, a, b = recv.shape
    tr = _row_tile(a, b, 512 * 1024)

    def body(chip_ref, core_ref, r_ref, own_ref, o_ref):
        me = chip_ref[0]
        mine = own_ref[0].astype(F32)
        acc = None
        for k in range(N_CHIPS):
            term = jnp.where(me == k, mine, r_ref[k].astype(F32))
            acc = term if acc is None else acc + term
        o_ref[0] = acc

    own_spec = (pl.BlockSpec((1, tr, b), lambda i, ch, co: (0, i, ch[0])) if ax == 2
                else pl.BlockSpec((1, tr, b), lambda i, ch, co: (ch[0], i, 0)))
    grid_spec = pltpu.PrefetchScalarGridSpec(
        num_scalar_prefetch=2, grid=(a // tr,),
        in_specs=[pl.BlockSpec((N_CHIPS, tr, b), lambda i, ch, co: (0, i, 0)), own_spec],
        out_specs=pl.BlockSpec((1, tr, b), lambda i, ch, co: (co[0], i, 0)))
    return _pcall(body, grid_spec=grid_spec, out_shape=jax.ShapeDtypeStruct((2, a, b), F32),
                  name="sum_chips", compiler_params=_params(("parallel",)))(chip, core, recv, own)


def _place_own(wt, layer, chip, ax, after):
    nl, r, c = wt.shape
    half = r // 2
    tr = _row_tile(half, c, 512 * 1024)
    nb = half // tr

    def body(chip_ref, w_ref, after_ref, o_ref):
        o_ref[...] = w_ref[...].astype(BF16).reshape(o_ref.shape)

    if ax == 2:
        out_spec = pl.BlockSpec((1, tr, c), lambda h, i, ch: (h, i, ch[0]))
    else:
        out_spec = pl.BlockSpec((1, 1, tr, c), lambda h, i, ch: (ch[0], h, i, 0))
    grid_spec = pltpu.PrefetchScalarGridSpec(
        num_scalar_prefetch=1, grid=(2, nb),
        in_specs=[pl.BlockSpec((1, tr, c), lambda h, i, ch: (layer, h * nb + i, 0)), pl.BlockSpec(memory_space=pl.ANY)],
        out_specs=out_spec)
    return _pcall(body, grid_spec=grid_spec, out_shape=jax.ShapeDtypeStruct(_full_view_shape(wt.shape, ax), BF16),
                  name="place_own", compiler_params=_params(("parallel", "parallel")))(chip, wt, after)


def _adamw(w, m, v, g, layer, prev, name, after=None):
    nl, r, c = w.shape
    tr = _row_tile(r, c)
    c1 = 1.0 - ADAM_B1 ** ADAM_STEP
    c2 = 1.0 - ADAM_B2 ** ADAM_STEP

    follow = [] if after is None else [after]

    def body(w_ref, m_ref, v_ref, gin_ref, *rest):
        g_ref, d_ref, nm_ref, nv_ref = rest[-4:]
        g = gin_ref[...]
        mm = ADAM_B1 * m_ref[0] + (1.0 - ADAM_B1) * g
        vv = ADAM_B2 * v_ref[0] + (1.0 - ADAM_B2) * (g * g)
        g_ref[0] = g
        nm_ref[0] = mm
        nv_ref[0] = vv
        d_ref[0] = -ADAM_LR * ((mm / c1) / (jnp.sqrt(vv / c2) + ADAM_EPS) + ADAM_WD * w_ref[0])

    blk = pl.BlockSpec((1, tr, c), lambda i: (layer, i, 0))
    shape = jax.ShapeDtypeStruct((nl, r, c), F32)
    extra = [] if prev is None else list(prev)
    return _pcall(
        body, grid=(r // tr,),
        in_specs=[blk] * 3 + [pl.BlockSpec((tr, c), lambda i: (i, 0))] + [pl.BlockSpec(memory_space=pl.ANY)] * (len(extra) + len(follow)),
        out_specs=[blk] * 4, out_shape=[shape] * 4, input_output_aliases={4 + k: k for k in range(len(extra))}, name=name,
        compiler_params=_params(("parallel",)))(w, m, v, g, *extra, *follow)


HBM_SPEC = pl.BlockSpec(memory_space=pltpu.HBM)
COMM = pltpu.CompilerParams(has_side_effects=True)


def _position():
    x, y, c = lax.axis_index("x"), lax.axis_index("y"), lax.axis_index("c")
    chips = [(1 - x, y), (x, 1 - y), (1 - x, 1 - y)]
    return x, y, c, chips


def _remote(src, dst, send_sem, recv_sem, dev):
    return pltpu.make_async_remote_copy(src_ref=src, dst_ref=dst, send_sem=send_sem, recv_sem=recv_sem,
                                        device_id=dev, device_id_type=MESH)


def _full_view_shape(shard_shape, ax):
    _, r, c = shard_shape
    return (2, r // 2, c * N_CHIPS) if ax == 2 else (N_CHIPS, 2, r // 2, c)


def _piece(ref, ax, j, h, cs):
    if ax == 2:
        return ref.at[h, :, pl.ds(pl.multiple_of(j * cs, cs), cs)]
    return ref.at[j, h]


def _chip_block(ref, ax, j, cs):
    if ax == 2:
        return ref.at[:, :, pl.ds(pl.multiple_of(j * cs, cs), cs)]
    return ref.at[j]


SEM_SPEC = pl.BlockSpec(memory_space=pltpu.SEMAPHORE)
ANY_SPEC = pl.BlockSpec(memory_space=pl.ANY)
SPLIT = pltpu.CompilerParams(has_side_effects=pltpu.SideEffectType.DATAFLOW_SIDE_EFFECTING)


def _exchange(kind, name, bufs, build, n_sems, after=None, sems=None):
    n = len(bufs)
    if kind == 'sync':
        def body(*refs):
            mine, theirs = build(refs[n:2 * n], refs[2 * n], refs[2 * n + 1])
            for cp in mine:
                cp.start()
            for cp in theirs:
                cp.wait_recv()
            for cp in mine:
                cp.wait_send()

        return list(_pcall(
            body, in_specs=[HBM_SPEC] * n, out_specs=[HBM_SPEC] * n,
            out_shape=[jax.ShapeDtypeStruct(v.shape, v.dtype) for v in bufs], input_output_aliases={t: t for t in range(n)},
            scratch_shapes=[pltpu.SemaphoreType.DMA((n_sems,)), pltpu.SemaphoreType.DMA((n_sems,))],
            name=name, compiler_params=COMM)(*bufs))
    if kind == 'start':
        def body(*refs):
            mine, _ = build(refs[n + 3:2 * n + 3], refs[n + 1], refs[n + 2])
            for cp in mine:
                cp.start()
            refs[2 * n + 3][...] = jnp.zeros_like(refs[2 * n + 3])

        outs = _pcall(
            body, in_specs=[HBM_SPEC] * n + [ANY_SPEC],
            out_specs=[SEM_SPEC, SEM_SPEC] + [HBM_SPEC] * n + [pl.BlockSpec(memory_space=pltpu.VMEM)],
            out_shape=[pltpu.SemaphoreType.DMA((n_sems,)), pltpu.SemaphoreType.DMA((n_sems,))]
            + [pltpu.HBM(v.shape, v.dtype) for v in bufs] + [jax.ShapeDtypeStruct((8, LANES), F32)],
            input_output_aliases={t: 2 + t for t in range(n)}, name=name,
            compiler_params=SPLIT)(*[pltpu.with_memory_space_constraint(v, pltpu.HBM) for v in bufs], after)
        return (outs[0], outs[1]), list(outs[2:2 + n]), outs[2 + n]

    def body(*refs):
        mine, theirs = build(refs[:n], refs[n], refs[n + 1])
        for cp in mine:
            cp.wait_send()
        for cp in theirs:
            cp.wait_recv()

    return list(_pcall(
        body, in_specs=[HBM_SPEC] * n + [SEM_SPEC, SEM_SPEC, ANY_SPEC], out_specs=[HBM_SPEC] * n,
        out_shape=[pltpu.HBM(v.shape, v.dtype) for v in bufs], input_output_aliases={t: t for t in range(n)},
        name=name, compiler_params=SPLIT)(*bufs, sems[0], sems[1], after))


def _gather_ici_copies(axes, shard_cols):
    def build(bufs, send_sems, recv_sems):
        x, y, c, chips = _position()
        me = 2 * x + y
        mine, theirs = [], []
        for t, ax in enumerate(axes):
            own = _piece(bufs[t], ax, me, c, shard_cols[t])
            for p, (px, py) in enumerate(chips):
                k = t * 3 + p
                got = _piece(bufs[t], ax, 2 * px + py, c, shard_cols[t])
                mine.append(_remote(own, own, send_sems.at[k], recv_sems.at[k], (px, py, c)))
                theirs.append(_remote(got, got, send_sems.at[k], recv_sems.at[k], (px, py, c)))
        return mine, theirs
    return build


def _gather_d2d_copies(axes, shard_cols):
    def build(bufs, send_sems, recv_sems):
        x, y, c, chips = _position()
        mine, theirs = [], []
        for t, ax in enumerate(axes):
            for p, (px, py) in enumerate(chips):
                k = t * 3 + p
                had = _piece(bufs[t], ax, 2 * px + py, c, shard_cols[t])
                got = _piece(bufs[t], ax, 2 * px + py, 1 - c, shard_cols[t])
                mine.append(_remote(had, had, send_sems.at[k], recv_sems.at[k], (x, y, 1 - c)))
                theirs.append(_remote(got, got, send_sems.at[k], recv_sems.at[k], (x, y, 1 - c)))
        return mine, theirs
    return build


def _grads_d2d_copies(n):
    def build(bufs, send_sems, recv_sems):
        x, y, c, _ = _position()
        mine = [_remote(bufs[t].at[:, 1 - c], bufs[n + t], send_sems.at[t], recv_sems.at[t], (x, y, 1 - c)) for t in range(n)]
        return mine, mine
    return build


def _grads_ici_copies(axes):
    n = len(axes)

    def build(bufs, send_sems, recv_sems):
        x, y, c, chips = _position()
        me = 2 * x + y

        def block(t, j):
            if axes[t] == 2:
                cs = bufs[n + t].shape[2]
                return bufs[t].at[0, :, pl.ds(pl.multiple_of(j * cs, cs), cs)]
            return bufs[t].at[j]

        mine, theirs = [], []
        for t in range(n):
            for p, (px, py) in enumerate(chips):
                k = t * 3 + p
                peer = 2 * px + py
                mine.append(_remote(block(t, peer), bufs[n + t].at[me], send_sems.at[k], recv_sems.at[k], (px, py, c)))
                theirs.append(_remote(block(t, peer), bufs[n + t].at[peer], send_sems.at[k], recv_sems.at[k], (px, py, c)))
        return mine, theirs
    return build


def _join_copies(bufs, send_sems, recv_sems):
    x, y, c, _ = _position()
    mine = [_remote(b.at[c], b.at[c], send_sems.at[t], recv_sems.at[t], (x, y, 1 - c)) for t, b in enumerate(bufs)]
    theirs = [_remote(b.at[1 - c], b.at[1 - c], send_sems.at[t], recv_sems.at[t], (x, y, 1 - c)) for t, b in enumerate(bufs)]
    return mine, theirs


def _grads_recv_shape(sm, ax):
    _, a, c = sm.shape
    return (N_CHIPS, a, c // N_CHIPS if ax == 2 else c)


def _gather_small(shard):
    nl, r, cs = shard.shape

    def body(in_ref, out_ref, send_sems, recv_sems, local_sem):
        x, y, c, chips = _position()

        def cols(j):
            return out_ref.at[:, :, pl.ds(pl.multiple_of(j * cs, cs), cs)]

        me = 2 * x + y
        loc = pltpu.make_async_copy(in_ref, cols(me), local_sem)
        loc.start()
        remote = [_remote(in_ref, cols(me), send_sems.at[p], recv_sems.at[p], (px, py, c)) for p, (px, py) in enumerate(chips)]
        for cp in remote:
            cp.start()
        for p, (px, py) in enumerate(chips):
            _remote(in_ref, cols(2 * px + py), send_sems.at[p], recv_sems.at[p], (px, py, c)).wait_recv()
        for cp in remote:
            cp.wait_send()
        loc.wait()

    return _pcall(
        body, in_specs=[HBM_SPEC], out_specs=HBM_SPEC, out_shape=jax.ShapeDtypeStruct((nl, r, cs * N_CHIPS), shard.dtype),
        scratch_shapes=[pltpu.SemaphoreType.DMA((3,)), pltpu.SemaphoreType.DMA((3,)), pltpu.SemaphoreType.DMA(())],
        name="gather_small", compiler_params=COMM)(shard)


def _all_reduce_small(pack, after):
    r, c = pack.shape

    def body(in_ref, after_ref, out_ref, slots, send_sems, recv_sems, local_sem):
        x, y, cc, _ = _position()
        me = 4 * x + 2 * y + cc
        peers = []
        for k in range(1, N_DEV):
            fx, fy, fc = (k >> 2) & 1, (k >> 1) & 1, k & 1
            px, py, pc = x ^ fx, y ^ fy, cc ^ fc
            peers.append((k - 1, (px, py, pc), 4 * px + 2 * py + pc))
        loc = pltpu.make_async_copy(in_ref, slots.at[me], local_sem)
        loc.start()
        copies = [_remote(in_ref, slots.at[me], send_sems.at[k], recv_sems.at[k], dev) for k, dev, _ in peers]
        for cp in copies:
            cp.start()
        for k, dev, idx in peers:
            _remote(in_ref, slots.at[idx], send_sems.at[k], recv_sems.at[k], dev).wait_recv()
        for cp in copies:
            cp.wait_send()
        loc.wait()
        acc = slots[0]
        for k in range(1, N_DEV):
            acc = acc + slots[k]
        out_ref[...] = acc

    vm = pl.BlockSpec(memory_space=pltpu.VMEM)
    return _pcall(
        body, in_specs=[vm, ANY_SPEC], out_specs=vm, out_shape=jax.ShapeDtypeStruct((r, c), F32),
        scratch_shapes=[pltpu.VMEM((N_DEV, r, c), F32), pltpu.SemaphoreType.DMA((N_DEV - 1,)),
                        pltpu.SemaphoreType.DMA((N_DEV - 1,)), pltpu.SemaphoreType.DMA(())],
        name="all_reduce_small", compiler_params=pltpu.CompilerParams(has_side_effects=True, vmem_limit_bytes=VMEM_LIMIT))(pack, after)


def _dims(d):
    half = d // 2
    return half // HEAD_SB, half // HEAD_XA, 3, (5 * half) // HEAD_XA


def _layer_fwd(x, mem, weight, small, l, after=None):
    h_sb, h_xa, u_blk, q_blk = _dims(x.shape[1])

    def vec(name):
        return small[name][l].reshape(1, -1)

    def use(a, name, follows, mm_name):
        wt = weight(name, follows)
        return _mm(a, wt, 'nn', F32, mm_name, after=weight.token())

    h1 = _norm_fwd(x, vec('g_mix_pre'), None, BF16, "norm_mix_pre", after)
    proj = use(h1, 'w_in', h1, "mm_proj")
    o_sb, a_sb = _sb_fwd(proj, h_sb)
    b_st = small['b_s'][l].T
    o_gm = _gm_fwd(proj, vec('g_vnorm'), small['w_s'][l], b_st, u_blk)
    memn = _norm_fwd(mem, vec('g_mem'), None, BF16, "norm_mem")
    mem_kv = use(memn, 'w_mem_kv', o_sb, "mm_mem_kv")
    o_xa = _xa_fwd(proj, mem_kv, q_blk, h_xa)
    zg = use(h1, 'w_gate', o_sb, "mm_gate")
    branches = [use(o, wn, zg, "mm_branch") for o, wn in ((o_sb, 'w_br_sb'), (o_gm, 'w_br_gm'), (o_xa, 'w_br_xa'))]
    merged = _merge_fwd(zg, vec('b_gate'), branches)
    y1 = use(merged, 'w_out', zg, "mm_out")
    x1 = _norm_fwd(y1, vec('g_mix_post'), x, F32, "norm_mix_post")
    h2 = _norm_fwd(x1, vec('g_ffn_pre'), None, BF16, "norm_ffn_pre")
    up = use(h2, 'w_up', h2, "mm_up")
    act = _cg_fwd(up, weight('conv_w', up), vec('conv_b'))
    y2 = use(act, 'w_down', act, "mm_down")
    x2 = _norm_fwd(y2, vec('g_ffn_post'), x1, F32, "norm_ffn_post")
    saved = dict(x0=x, h1=h1, proj=proj, o_sb=o_sb, a_sb=a_sb, o_gm=o_gm, o_xa=o_xa, memn=memn, mem_kv=mem_kv, zg=zg,
                 branches=branches, merged=merged, y1=y1, x1=x1, h2=h2, up=up, act=act, y2=y2, b_st=b_st)
    return x2, saved


def _layer_bwd(dx, mem, sv, full, small, l, after, emit):
    h_sb, h_xa, u_blk, q_blk = _dims(dx.shape[1])

    def vec(name):
        return small[name][l].reshape(1, -1)

    gb, gs = {}, {}
    dy2, gs['g_ffn_post'] = _norm_bwd(sv['y2'], vec('g_ffn_post'), [dx], None, BF16, "norm_ffn_post_bwd", after)
    gb['w_down'] = _mm(sv['act'], dy2, 'tn', BF16, "mm_down_dw")
    dact = _mm(dy2, full['w_down'], 'nt', F32, "mm_down_dx")
    dgate, dval, gs['conv_w'], gs['conv_b'] = _cg_bwd(sv['up'], full['conv_w'], vec('conv_b'), dact)
    dup = jnp.concatenate([dgate, dval], axis=1)
    gb['w_up'] = _mm(sv['h2'], dup, 'tn', BF16, "mm_up_dw")
    token = emit(0, gb)
    dh2 = _mm(dup, full['w_up'], 'nt', F32, "mm_up_dx", after=token)
    token = emit.flush(dh2)
    dx1, gs['g_ffn_pre'] = _norm_bwd(sv['x1'], vec('g_ffn_pre'), [dh2], dx, F32, "norm_ffn_pre_bwd", token)
    dy1, gs['g_mix_post'] = _norm_bwd(sv['y1'], vec('g_mix_post'), [dx1], None, BF16, "norm_mix_post_bwd")
    gb['w_out'] = _mm(sv['merged'], dy1, 'tn', BF16, "mm_out_dw")
    dmerged = _mm(dy1, full['w_out'], 'nt', F32, "mm_out_dx")
    dzg, dbr, gs['b_gate'] = _merge_bwd(sv['zg'], vec('b_gate'), sv['branches'], dmerged)
    douts = []
    for o, db, wn in ((sv['o_sb'], dbr[0], 'w_br_sb'), (sv['o_gm'], dbr[1], 'w_br_gm'), (sv['o_xa'], dbr[2], 'w_br_xa')):
        gb[wn] = _mm(o, db, 'tn', BF16, "mm_branch_dw")
        douts.append(_mm(db, full[wn], 'nt', F32, "mm_branch_dx"))
    gb['w_gate'] = _mm(sv['h1'], dzg, 'tn', BF16, "mm_gate_dw")
    dq_xa, dk_xa, dv_xa = _xa_bwd(sv['proj'], sv['mem_kv'], douts[2], q_blk, h_xa)
    dmem_kv = jnp.concatenate([dk_xa, dv_xa], axis=1).astype(BF16)
    gb['w_mem_kv'] = _mm(sv['memn'], dmem_kv, 'tn', BF16, "mm_mem_kv_dw")
    token = emit(1, gb)
    dh1_gate = _mm(dzg, full['w_gate'], 'nt', F32, "mm_gate_dx", after=token)
    token = emit.flush(dh1_gate)
    dmemn = _mm(dmem_kv, full['w_mem_kv'], 'nt', F32, "mm_mem_kv_dx")
    _, gs['g_mem'] = _norm_bwd(mem, vec('g_mem'), [dmemn], None, BF16, "norm_mem_bwd")
    du, dv, gs['g_vnorm'], gs['w_s'], db_st = _gm_bwd(sv['proj'], vec('g_vnorm'), small['w_s'][l], sv['b_st'], douts[1], u_blk)
    gs['b_s'] = db_st.T
    dq, dk, dvv = _sb_bwd(sv['proj'], sv['a_sb'], douts[0], h_sb, token)
    dproj = jnp.concatenate([dq, dk, dvv, du, dv, dq_xa], axis=1).astype(BF16)
    gb['w_in'] = _mm(sv['h1'], dproj, 'tn', BF16, "mm_proj_dw")
    token = emit(2, gb)
    dh1_proj = _mm(dproj, full['w_in'], 'nt', F32, "mm_proj_dx")
    dx0, gs['g_mix_pre'] = _norm_bwd(sv['x0'], vec('g_mix_pre'), [dh1_gate, dh1_proj], dx1, F32, "norm_mix_pre_bwd", token)
    return dx0, gs


class _Given:
    def __init__(self, full):
        self.full = full

    def __call__(self, name, follows):
        return self.full[name]

    def token(self):
        return None


def _local_step(x, mem, target, full, small):
    n_layers = len(full['w_in'])
    saved = []
    for l in range(n_layers):
        x, sv = _layer_fwd(x, mem, _Given({n: full[n][l] for n in full}), small, l)
        saved.append(sv)
    sq, dx = _loss_head(x, target)
    gbig = {n: [None] * n_layers for n in BIG}
    gsmall = {n: [None] * n_layers for n in SMALL + ['conv_w']}
    class Collect:
        def __init__(self, l):
            self.l = l

        def __call__(self, g, gb):
            for n in BWD_GROUPS[g]:
                gbig[n][self.l] = gb[n]

        def flush(self, follows):
            return None

    for l in reversed(range(n_layers)):
        dx, gs = _layer_bwd(dx, mem, saved[l], {n: full[n][l] for n in full}, small, l, None, Collect(l))
        for n in gs:
            gsmall[n][l] = gs[n]
    return sq, dx, gbig, gsmall


def _pack(arrays, rows_multiple):
    flat = jnp.concatenate([a.reshape(-1).astype(F32) for a in arrays])
    rows = -(-flat.shape[0] // LANES)
    rows = -(-rows // rows_multiple) * rows_multiple
    return jnp.pad(flat, (0, rows * LANES - flat.shape[0])).reshape(rows, LANES)


def _unpack(pack, like):
    flat = pack.reshape(-1)
    out, off = [], 0
    for a in like:
        out.append(flat[off:off + a.size].reshape(a.shape))
        off += a.size
    return out


def _grad_view(g, ax):
    r, c = g.shape
    return g.reshape(1, 2, r // 2, c) if ax == 2 else g.reshape(N_CHIPS, 2, r // (2 * N_CHIPS), c)


def kernel(x, mem, g_mix_pre, w_in, g_vnorm, w_s, b_s, g_mem, w_mem_kv, w_gate, b_gate, w_br_sb, w_br_gm, w_br_xa, w_out, g_mix_post, g_ffn_pre, w_up, conv_w, conv_b, w_down, g_ffn_post, loss_target, m_g_mix_pre, m_w_in, m_g_vnorm, m_w_s, m_b_s, m_g_mem, m_w_mem_kv, m_w_gate, m_b_gate, m_w_br_sb, m_w_br_gm, m_w_br_xa, m_w_out, m_g_mix_post, m_g_ffn_pre, m_w_up, m_conv_w, m_conv_b, m_w_down, m_g_ffn_post, v_g_mix_pre, v_w_in, v_g_vnorm, v_w_s, v_b_s, v_g_mem, v_w_mem_kv, v_w_gate, v_b_gate, v_w_br_sb, v_w_br_gm, v_w_br_xa, v_w_out, v_g_mix_post, v_g_ffn_pre, v_w_up, v_conv_w, v_conv_b, v_w_down, v_g_ffn_post):
    w = dict(g_mix_pre=g_mix_pre, w_in=w_in, g_vnorm=g_vnorm, w_s=w_s, b_s=b_s, g_mem=g_mem, w_mem_kv=w_mem_kv,
             w_gate=w_gate, b_gate=b_gate, w_br_sb=w_br_sb, w_br_gm=w_br_gm, w_br_xa=w_br_xa, w_out=w_out,
             g_mix_post=g_mix_post, g_ffn_pre=g_ffn_pre, w_up=w_up, conv_w=conv_w, conv_b=conv_b, w_down=w_down,
             g_ffn_post=g_ffn_post)
    m = dict(g_mix_pre=m_g_mix_pre, w_in=m_w_in, g_vnorm=m_g_vnorm, w_s=m_w_s, b_s=m_b_s, g_mem=m_g_mem,
             w_mem_kv=m_w_mem_kv, w_gate=m_w_gate, b_gate=m_b_gate, w_br_sb=m_w_br_sb, w_br_gm=m_w_br_gm,
             w_br_xa=m_w_br_xa, w_out=m_w_out, g_mix_post=m_g_mix_post, g_ffn_pre=m_g_ffn_pre, w_up=m_w_up,
             conv_w=m_conv_w, conv_b=m_conv_b, w_down=m_w_down, g_ffn_post=m_g_ffn_post)
    v = dict(g_mix_pre=v_g_mix_pre, w_in=v_w_in, g_vnorm=v_g_vnorm, w_s=v_w_s, b_s=v_b_s, g_mem=v_g_mem,
             w_mem_kv=v_w_mem_kv, w_gate=v_w_gate, b_gate=v_b_gate, w_br_sb=v_w_br_sb, w_br_gm=v_w_br_gm,
             w_br_xa=v_w_br_xa, w_out=v_w_out, g_mix_post=v_g_mix_post, g_ffn_pre=v_g_ffn_pre, w_up=v_w_up,
             conv_w=v_conv_w, conv_b=v_conv_b, w_down=v_w_down, g_ffn_post=v_g_ffn_post)
    n_layers = w_in.shape[0]
    d = x.shape[-1]
    core = lax.axis_index("c").astype(jnp.int32).reshape(1)
    chip = (2 * lax.axis_index("x") + lax.axis_index("y")).astype(jnp.int32).reshape(1)
    small = {n: w[n] for n in SMALL}
    xs, mems, target = x[0], mem[0], loss_target[0]

    conv_w_full = _gather_small(conv_w)

    def as_full(vw, ax):
        return vw.reshape(-1, vw.shape[-1]) if ax == 1 else vw.reshape(vw.shape[0] * vw.shape[1], vw.shape[2])

    stages, token = {}, conv_w_full
    keys = [(l, g) for l in range(n_layers) for g in range(len(FWD_GROUPS))]
    for l, g in keys:
        names = FWD_GROUPS[g]
        ax_g = [BIG_AXIS[n] for n in names]
        cols_g = [w[n].shape[2] for n in names]
        views = [_place_own(w[n], l, chip, ax, token) for n, ax in zip(names, ax_g)]
        sems, views, token = _exchange('start', f"gather_ici_start_{l}_{g}", views, _gather_ici_copies(ax_g, cols_g),
                                       3 * len(names), after=token)
        stages[l, g] = dict(names=names, ax=ax_g, cols=cols_g, views=views, ici=sems, d2d=None, full=None)

    def cross_cores(key, follows):
        st, (l, g) = stages[key], key
        n3 = 3 * len(st['names'])
        views = _exchange('wait', f"gather_ici_wait_{l}_{g}", st['views'], _gather_ici_copies(st['ax'], st['cols']), n3,
                          after=follows, sems=st['ici'])
        st['d2d'], st['views'], tok = _exchange('start', f"gather_d2d_start_{l}_{g}", views,
                                                _gather_d2d_copies(st['ax'], st['cols']), n3, after=core)
        return tok

    class Weights:
        def __init__(self, l):
            self.l, self.tok = l, None

        def __call__(self, name, follows):
            if name == 'conv_w':
                return conv_w_full[self.l]
            key = (self.l, [g for g, names in enumerate(FWD_GROUPS) if name in names][0])
            st = stages[key]
            if st['full'] is None:
                if st['d2d'] is None:
                    cross_cores(key, follows)
                views = _exchange('wait', f"gather_d2d_wait_{key[0]}_{key[1]}", st['views'],
                                  _gather_d2d_copies(st['ax'], st['cols']), 3 * len(st['names']), after=follows, sems=st['d2d'])
                st['full'] = {n: as_full(vw, ax) for n, vw, ax in zip(st['names'], views, st['ax'])}
                nxt = keys.index(key) + 1
                if nxt < len(keys) and key != keys[0]:
                    self.tok = cross_cores(keys[nxt], views[0])
            return st['full'][name]

        def token(self):
            return self.tok

    fulls, saved = [], []
    for l in range(n_layers):
        xs, sv = _layer_fwd(xs, mems, Weights(l), small, l, token if l == 0 else None)
        fulls.append({n: stages[l, g]['full'][n] for g, names in enumerate(FWD_GROUPS) for n in names} | {'conv_w': conv_w_full[l]})
        saved.append(sv)
    sq, dx = _loss_head(xs, target)
    loss = lax.psum(0.5 * jnp.sum(sq) / d, ("x", "y", "c"))

    sent = []

    def to_chips(l, g, names, ax_g, bufs, after):
        n = len(names)
        sums = [_sum_halves(dv, th, core, "sum_halves") for dv, th in zip(bufs[:n], bufs[n:])]
        lands = [lax.empty(_grads_recv_shape(sm, ax), sm.dtype) for sm, ax in zip(sums, ax_g)]
        sems, bufs, tok = _exchange('start', f"grads_ici_start_{l}_{g}", sums + lands, _grads_ici_copies(ax_g), 3 * n,
                                    after=core if after is None else after)
        sent.append((l, g, names, ax_g, bufs, sems))
        return tok

    class Grads:
        def __init__(self, l):
            self.l, self.crossing = l, None

        def __call__(self, g, gb):
            names = BWD_GROUPS[g]
            ax_g = [BIG_AXIS[n] for n in names]
            n = len(names)
            dwvs = [_grad_view(gb[nm], ax) for nm, ax in zip(names, ax_g)]
            lands = [lax.empty((dv.shape[0],) + dv.shape[2:], dv.dtype) for dv in dwvs]
            if g + 1 < len(BWD_GROUPS):
                sems, bufs, tok = _exchange('start', f"grads_d2d_start_{self.l}_{g}", dwvs + lands, _grads_d2d_copies(n), n,
                                            after=core)
                self.crossing = (g, names, ax_g, bufs, sems)
                return tok
            return to_chips(self.l, g, names, ax_g, _exchange('sync', "grads_d2d", dwvs + lands, _grads_d2d_copies(n), n), None)

        def flush(self, follows):
            if self.crossing is None:
                return None
            (g, names, ax_g, bufs, sems), self.crossing = self.crossing, None
            bufs = _exchange('wait', f"grads_d2d_wait_{self.l}_{g}", bufs, _grads_d2d_copies(len(names)), len(names),
                             after=follows, sems=sems)
            return to_chips(self.l, g, names, ax_g, bufs, None)

    gsmall = {n: [None] * n_layers for n in SMALL + ['conv_w']}
    for l in reversed(range(n_layers)):
        dx, gs = _layer_bwd(dx, mems, saved[l], fulls[l], small, l, None, Grads(l))
        for n in gs:
            gsmall[n][l] = gs[n]

    def small_update(follows):
        small_full = [jnp.stack(gsmall[n]).reshape(w[n].shape) for n in SMALL]
        conv_w_grad = jnp.stack(gsmall['conv_w'])
        summed = _all_reduce_small(_pack(small_full + [conv_w_grad], 8), follows)
        *small_g, conv_w_g = _unpack(summed, small_full + [conv_w_grad])
        shard = conv_w.shape[-1]
        conv_w_g = lax.dynamic_slice_in_dim(conv_w_g, chip[0] * shard, shard, axis=2)
        names = SMALL + ['conv_w']
        packed = [_pack([p[n] for n in names], 256) for p in (w, m, v)]
        gpack = _pack(small_g + [conv_w_g], 256)
        res = _adamw(packed[0][None], packed[1][None], packed[2][None], gpack, 0, None, "adamw_small")
        like = [w[n] for n in names]
        unpacked = [_unpack(r[0], like) for r in res]
        return {n: tuple(u[i] for u in unpacked) for i, n in enumerate(names)}

    out = {}

    def update(joining, follows):
        l, g, names, halves, sems = joining
        halves = _exchange('wait', f"join_wait_{l}_{g}", halves, _join_copies, len(names), after=follows, sems=sems)
        last = None
        for n, hv in zip(names, halves):
            out[n] = _adamw(w[n], m[n], v[n], hv.reshape(w[n].shape[1:]), l, out.get(n), "adamw_big", last)
            last = out[n][0]
        return last

    joining, follows = None, dx
    for k, (l, g, names, ax_g, bufs, sems) in enumerate(sent):
        if k == len(sent) - 1:
            out.update(small_update(follows))
        n = len(names)
        bufs = _exchange('wait', f"grads_ici_wait_{l}_{g}", bufs, _grads_ici_copies(ax_g), 3 * n, after=follows, sems=sems)
        halves = [_sum_chips(r, sm, chip, core, ax) for sm, r, ax in zip(bufs[:n], bufs[n:], ax_g)]
        jsems, halves, tok = _exchange('start', f"join_start_{l}_{g}", halves, _join_copies, n, after=core)
        if joining is not None:
            follows = update(joining, tok)
        joining = (l, g, names, halves, jsems)
    update(joining, follows)

    return (loss, dx[None], *[out[n][0] for n in WEIGHTS], *[out[n][1] for n in WEIGHTS],
            *[out[n][2] for n in WEIGHTS], *[out[n][3] for n in WEIGHTS])
```

```python
import functools
import math

import jax
import jax.numpy as jnp
from jax import lax
from jax.experimental import pallas as pl
from jax.experimental.pallas import tpu as pltpu

F32 = jnp.float32
BF16 = jnp.bfloat16
EPS = 1e-6
HEAD_SB = 128
GROUP_GM = 128
CHUNK = 64
HEAD_XA = 256
CONV_TAPS = 3
N_CHIPS = 4
N_DEV = 8
LANES = 128
MIB = 1024 * 1024
VMEM_LIMIT = 48 * MIB
SPLITS = 2

ADAM_LR = 0.001
ADAM_B1 = 0.9
ADAM_B2 = 0.999
ADAM_EPS = 1e-08
ADAM_WD = 0.01
ADAM_STEP = 10

WEIGHTS = ['g_mix_pre', 'w_in', 'g_vnorm', 'w_s', 'b_s', 'g_mem', 'w_mem_kv', 'w_gate', 'b_gate', 'w_br_sb',
           'w_br_gm', 'w_br_xa', 'w_out', 'g_mix_post', 'g_ffn_pre', 'w_up', 'conv_w', 'conv_b', 'w_down',
           'g_ffn_post']
BIG_AXIS = {'w_in': 2, 'w_mem_kv': 1, 'w_gate': 2, 'w_br_sb': 2, 'w_br_gm': 2, 'w_br_xa': 2, 'w_out': 1,
            'w_up': 2, 'w_down': 1}
BIG = list(BIG_AXIS)
FWD_GROUPS = [['w_in'], ['w_mem_kv', 'w_gate'], ['w_br_sb', 'w_br_gm', 'w_br_xa', 'w_out'], ['w_up'], ['w_down']]
BWD_GROUPS = [['w_down', 'w_up'], ['w_out', 'w_br_sb', 'w_br_gm', 'w_br_xa', 'w_gate', 'w_mem_kv'], ['w_in']]
SMALL = ['g_mix_pre', 'g_vnorm', 'w_s', 'b_s', 'g_mem', 'b_gate', 'g_mix_post', 'g_ffn_pre', 'conv_b', 'g_ffn_post']
MESH = pl.DeviceIdType.MESH


def _pcall(body, **kw):
    return pl.pallas_call(body, **kw)


def _params(sem=None, vmem=VMEM_LIMIT):
    return pltpu.CompilerParams(dimension_semantics=sem, vmem_limit_bytes=vmem)


def _tile(n, cands):
    for c in cands:
        if n % c == 0:
            return c
    return n


_GELU_C = math.sqrt(2.0 / math.pi)
_GELU_A = 0.044715


def _gelu(x):
    return 0.5 * x * (1.0 + jnp.tanh(_GELU_C * (x + _GELU_A * (x * x * x))))


def _gelu_and_grad(x):
    x2 = x * x
    t = jnp.tanh(_GELU_C * (x + _GELU_A * (x2 * x)))
    val = 0.5 * x * (1.0 + t)
    grad = 0.5 * (1.0 + t) + 0.5 * x * (1.0 - t * t) * (_GELU_C * (1.0 + 3.0 * _GELU_A * x2))
    return val, grad


def _softplus(z):
    return jnp.maximum(z, 0.0) + jnp.log(1.0 + jnp.exp(-jnp.abs(z)))


def _dot(a, b):
    return jnp.dot(a, b, preferred_element_type=F32)


def _dot_nt(a, b):
    return lax.dot_general(a, b, (((1,), (1,)), ((), ())), preferred_element_type=F32)


def _dot_tn(a, b):
    return lax.dot_general(a, b, (((0,), (0,)), ((), ())), preferred_element_type=F32)


def _split_dot(a, m):
    out = None
    rest = a
    for _ in range(SPLITS):
        piece = rest.astype(BF16)
        rest = rest - piece.astype(F32)
        term = _dot(piece, m)
        out = term if out is None else out + term
    return out


def _mm(a, b, mode, out_dtype, name, tm=None, tn=None, tk=None, after=None):
    if mode == 'nn':
        (m, kc), (kc2, n) = a.shape, b.shape
    elif mode == 'nt':
        (m, kc), (n, kc2) = a.shape, b.shape
    else:
        (kc, m), (kc2, n) = a.shape, b.shape
    assert kc == kc2, (a.shape, b.shape, mode)
    tm = tm or _tile(m, (1024, 512, 256, 128))
    tn = tn or _tile(n, (1024, 512, 256, 128))
    tk = tk or (kc if kc <= 3072 else _tile(kc, (3072, 2816, 2048, 1536, 1408, 1024, 512)))
    nk = kc // tk
    dot = {'nn': _dot, 'nt': _dot_nt, 'tn': _dot_tn}[mode]
    a_spec = pl.BlockSpec((tk, tm), lambda i, j, k: (k, i)) if mode == 'tn' else pl.BlockSpec((tm, tk), lambda i, j, k: (i, k))
    b_spec = pl.BlockSpec((tn, tk), lambda i, j, k: (j, k)) if mode == 'nt' else pl.BlockSpec((tk, tn), lambda i, j, k: (k, j))

    extra = [] if after is None else [after]
    extra_specs = [pl.BlockSpec(memory_space=pl.ANY)] * len(extra)

    if nk == 1:
        def body(a_ref, b_ref, *rest):
            o_ref = rest[-1]
            o_ref[...] = dot(a_ref[...].astype(BF16), b_ref[...].astype(BF16)).astype(o_ref.dtype)
        scratch = []
    else:
        def body(a_ref, b_ref, *rest):
            o_ref, acc_ref = rest[-2], rest[-1]
            k = pl.program_id(2)
            part = dot(a_ref[...].astype(BF16), b_ref[...].astype(BF16))

            @pl.when(k == 0)
            def _():
                acc_ref[...] = part

            @pl.when(k > 0)
            def _():
                acc_ref[...] += part

            @pl.when(k == nk - 1)
            def _():
                o_ref[...] = acc_ref[...].astype(o_ref.dtype)
        scratch = [pltpu.VMEM((tm, tn), F32)]

    return _pcall(
        body, grid=(m // tm, n // tn, nk), in_specs=[a_spec, b_spec] + extra_specs,
        out_specs=pl.BlockSpec((tm, tn), lambda i, j, k: (i, j)),
        out_shape=jax.ShapeDtypeStruct((m, n), out_dtype), scratch_shapes=scratch, name=name,
        compiler_params=_params(("parallel", "parallel", "arbitrary")))(a, b, *extra)


def _norm_fwd(x, g, res, out_dtype, name, after=None):
    s, d = x.shape
    tr = _tile(s, (256, 128))
    has_res = res is not None
    has_after = after is not None

    def body(*refs):
        x_ref, g_ref = refs[0], refs[1]
        o_ref = refs[-1]
        xv = x_ref[...]
        y = xv * lax.rsqrt(jnp.mean(xv * xv, axis=-1, keepdims=True) + EPS) * g_ref[...]
        if has_res:
            y = y + refs[2][...]
        o_ref[...] = y.astype(o_ref.dtype)

    row = pl.BlockSpec((tr, d), lambda i: (i, 0))
    ins = [x, g] + ([res] if has_res else []) + ([after] if has_after else [])
    return _pcall(
        body, grid=(s // tr,),
        in_specs=[row, pl.BlockSpec((1, d), lambda i: (0, 0))] + ([row] if has_res else [])
        + ([pl.BlockSpec(memory_space=pl.ANY)] if has_after else []),
        out_specs=row, out_shape=jax.ShapeDtypeStruct((s, d), out_dtype), name=name,
        compiler_params=_params(("parallel",)))(*ins)


def _norm_bwd(x, g, douts, dres, out_dtype, name, after=None):
    s, d = x.shape
    tr = _tile(s, (256, 128))
    nd = len(douts)
    has_res = dres is not None
    has_after = after is not None

    def body(*refs):
        x_ref, g_ref = refs[0], refs[1]
        dx_ref, dg_ref = refs[-2], refs[-1]
        dout = refs[2][...].astype(F32)
        for r in refs[3:2 + nd]:
            dout = dout + r[...].astype(F32)
        xv = x_ref[...]
        r = lax.rsqrt(jnp.mean(xv * xv, axis=-1, keepdims=True) + EPS)
        n = xv * r
        dn = dout * g_ref[...]
        dx = r * (dn - n * jnp.mean(dn * n, axis=-1, keepdims=True))
        if has_res:
            dx = dx + refs[2 + nd][...]
        dx_ref[...] = dx.astype(dx_ref.dtype)

        @pl.when(pl.program_id(0) == 0)
        def _():
            dg_ref[...] = jnp.zeros_like(dg_ref)

        dg_ref[...] += jnp.sum(dout * n, axis=0, keepdims=True)

    row = pl.BlockSpec((tr, d), lambda i: (i, 0))
    vec = pl.BlockSpec((1, d), lambda i: (0, 0))
    ins = [x, g] + list(douts) + ([dres] if has_res else []) + ([after] if has_after else [])
    return _pcall(
        body, grid=(s // tr,),
        in_specs=[row, vec] + [row] * (nd + int(has_res)) + ([pl.BlockSpec(memory_space=pl.ANY)] if has_after else []),
        out_specs=[row, vec],
        out_shape=[jax.ShapeDtypeStruct((s, d), out_dtype), jax.ShapeDtypeStruct((1, d), F32)], name=name,
        compiler_params=_params(("arbitrary",)))(*ins)


def _loss_head(y, target):
    s, d = y.shape
    tr = _tile(s, (256, 128))

    def body(y_ref, t_ref, sq_ref, dy_ref):
        e = y_ref[...] - t_ref[...]
        dy_ref[...] = e * (1.0 / d)

        @pl.when(pl.program_id(0) == 0)
        def _():
            sq_ref[...] = jnp.zeros_like(sq_ref)

        sq_ref[...] += jnp.sum(e * e, axis=0, keepdims=True)

    row = pl.BlockSpec((tr, d), lambda i: (i, 0))
    return _pcall(
        body, grid=(s // tr,), in_specs=[row, row], out_specs=[pl.BlockSpec((1, d), lambda i: (0, 0)), row],
        out_shape=[jax.ShapeDtypeStruct((1, d), F32), jax.ShapeDtypeStruct((s, d), F32)], name="loss_head",
        compiler_params=_params(("arbitrary",)))(y, target)


NEVER = -1e30
SB_QUERIES = 512


def _sb_sum_matrix(later):
    r = lax.broadcasted_iota(jnp.int32, (HEAD_SB, 2 * HEAD_SB), 0)
    c = lax.broadcasted_iota(jnp.int32, (HEAD_SB, 2 * HEAD_SB), 1)
    tri = jnp.where((r > c) if later else (r < c), 1.0, 0.0)
    return jnp.where(c < HEAD_SB, tri, 1.0).astype(BF16)


def _sb_mask(tq, q0, k0):
    row = lax.broadcasted_iota(jnp.int32, (tq, HEAD_SB), 0)
    col = lax.broadcasted_iota(jnp.int32, (tq, HEAD_SB), 1)
    return (k0 + col) < (q0 + row)


def _sb_fwd(proj, n_heads):
    s = proj.shape[0]
    tq = min(SB_QUERIES, s)
    per = tq // HEAD_SB
    scale = HEAD_SB ** -0.5

    def body(q_ref, k_ref, v_ref, o_ref, a_ref, acc_ref, c_ref):
        i = pl.program_id(1)
        q = q_ref[...].astype(BF16)
        sums = _sb_sum_matrix(True)
        acc_ref[...] = jnp.zeros_like(acc_ref)
        c_ref[...] = jnp.zeros_like(c_ref)
        last = (i + 1) * per - 1

        def scores(j, masked):
            off = pl.multiple_of(j * HEAD_SB, HEAD_SB)
            z = _dot_nt(q, k_ref[pl.ds(off, HEAD_SB), :].astype(BF16)) * scale
            sp = _softplus(z)
            logb = z - sp
            if masked:
                mask = _sb_mask(tq, i * tq, off)
                logb = jnp.where(mask, logb, NEVER)
                sp = jnp.where(mask, sp, 0.0)
            return logb, _split_dot(sp, sums)

        def values(j, logb, both):
            off = pl.multiple_of(j * HEAD_SB, HEAD_SB)
            c = c_ref[...]
            a = jnp.exp(logb - both[:, :HEAD_SB] - c).astype(BF16)
            a_ref[0, 0, j] = a
            acc_ref[...] += _dot(a, v_ref[pl.ds(off, HEAD_SB), :].astype(BF16))
            c_ref[...] = c + both[:, HEAD_SB:]

        def step(jj, carry, masked):
            j = last - jj
            nxt = scores(j, masked)
            values(jnp.minimum(j + 1, last), *carry)
            return nxt

        idle = (jnp.full((tq, HEAD_SB), NEVER, F32), jnp.zeros((tq, 2 * HEAD_SB), F32))
        carry = idle
        for jj in range(per):
            carry = step(jj, carry, True)
        def pair(jp, carry):
            return step(2 * jp + 1, step(2 * jp, carry, False), False)

        carry = lax.fori_loop(per // 2, (last + 1) // 2, pair, carry)
        values(0, *carry)
        o_ref[...] = acc_ref[...].astype(o_ref.dtype)

    h = n_heads
    blk = pl.BlockSpec((tq, HEAD_SB), lambda hh, i: (i, hh))
    return _pcall(
        body, grid=(h, s // tq),
        in_specs=[blk, pl.BlockSpec((s, HEAD_SB), lambda hh, i: (0, h + hh)),
                  pl.BlockSpec((s, HEAD_SB), lambda hh, i: (0, 2 * h + hh))],
        out_specs=[blk, pl.BlockSpec((1, 1, s // HEAD_SB, tq, HEAD_SB), lambda hh, i: (hh, i, 0, 0, 0))],
        out_shape=[jax.ShapeDtypeStruct((s, h * HEAD_SB), BF16),
                   jax.ShapeDtypeStruct((h, s // tq, s // HEAD_SB, tq, HEAD_SB), BF16)],
        scratch_shapes=[pltpu.VMEM((tq, HEAD_SB), F32), pltpu.VMEM((tq, HEAD_SB), F32)],
        name="sb_fwd", compiler_params=_params(("parallel", "arbitrary")))(proj, proj, proj)


def _sb_bwd(proj, a_saved, do, n_heads, after=None):
    s = proj.shape[0]
    tq = min(SB_QUERIES, s)
    per = tq // HEAD_SB
    scale = HEAD_SB ** -0.5
    follow = [] if after is None else [after]

    def body(q_ref, k_ref, v_ref, do_ref, a_ref, *rest):
        dq_ref, dk_ref, dv_ref, run_ref, acc_ref = rest[len(follow):]
        i = pl.program_id(1)

        @pl.when(i == 0)
        def _():
            dk_ref[...] = jnp.zeros_like(dk_ref)
            dv_ref[...] = jnp.zeros_like(dv_ref)

        q = q_ref[...].astype(BF16)
        dob = do_ref[...].astype(BF16)
        run_ref[...] = jnp.zeros_like(run_ref)
        acc_ref[...] = jnp.zeros_like(acc_ref)
        earlier = _sb_sum_matrix(False)
        first_diagonal = i * per

        def step(j, masked):
            off = pl.multiple_of(j * HEAD_SB, HEAD_SB)
            kb = k_ref[pl.ds(off, HEAD_SB), :].astype(BF16)
            vb = v_ref[pl.ds(off, HEAD_SB), :].astype(BF16)
            a = a_ref[0, 0, j]
            g = a.astype(F32) * _dot_nt(dob, vb)
            dv_ref[pl.ds(off, HEAD_SB), :] += _dot_tn(a, dob)
            z = _dot_nt(q, kb) * scale
            beta = 1.0 / (1.0 + jnp.exp(-z))
            both = _split_dot(g, earlier)
            p = run_ref[...]
            dz = (g * (1.0 - beta) - beta * (both[:, :HEAD_SB] + p)) * scale
            if masked:
                dz = jnp.where(_sb_mask(tq, i * tq, off), dz, 0.0)
            dzb = dz.astype(BF16)
            dk_ref[pl.ds(off, HEAD_SB), :] += _dot_tn(dzb, q)
            acc_ref[...] += _dot(dzb, kb)
            run_ref[...] = p + both[:, HEAD_SB:]

        def pair(jp, carry):
            step(2 * jp, False)
            step(2 * jp + 1, False)
            return carry

        lax.fori_loop(0, first_diagonal // 2, pair, 0)
        for u in range(per):
            step(first_diagonal + u, True)
        dq_ref[...] = acc_ref[...]

    h = n_heads
    blk = pl.BlockSpec((tq, HEAD_SB), lambda hh, i: (i, hh))
    col_blk = pl.BlockSpec((s, HEAD_SB), lambda hh, i: (0, hh))
    shape = jax.ShapeDtypeStruct((s, h * HEAD_SB), F32)
    return _pcall(
        body, grid=(h, s // tq),
        in_specs=[blk, pl.BlockSpec((s, HEAD_SB), lambda hh, i: (0, h + hh)),
                  pl.BlockSpec((s, HEAD_SB), lambda hh, i: (0, 2 * h + hh)), blk,
                  pl.BlockSpec((1, 1, s // HEAD_SB, tq, HEAD_SB), lambda hh, i: (hh, i, 0, 0, 0))]
        + [pl.BlockSpec(memory_space=pl.ANY)] * len(follow),
        out_specs=[blk, col_blk, col_blk], out_shape=[shape, shape, shape],
        scratch_shapes=[pltpu.VMEM((tq, HEAD_SB), F32), pltpu.VMEM((tq, HEAD_SB), F32)],
        name="sb_bwd", compiler_params=_params(("parallel", "arbitrary")))(proj, proj, proj, do, a_saved, *follow)


def _gm_mask():
    t = lax.broadcasted_iota(jnp.int32, (GROUP_GM, GROUP_GM), 0)
    s = lax.broadcasted_iota(jnp.int32, (GROUP_GM, GROUP_GM), 1)
    shift = CHUNK.bit_length() - 1
    return (s >> shift) <= (t >> shift)


def _gm_fwd(proj, g_vnorm, w_s, b_st, u_blk):
    s = proj.shape[0]
    groups = w_s.shape[0]
    w = groups * GROUP_GM

    def body(u_ref, v_ref, gv_ref, ws_ref, bst_ref, o_ref):
        ug = _gelu(u_ref[...])
        vg = _gelu(v_ref[...])
        vn = vg * lax.rsqrt(jnp.mean(vg * vg, axis=-1, keepdims=True) + EPS) * gv_ref[...]
        vnb = vn.astype(BF16)
        mask = _gm_mask()
        for g in range(groups):
            sl = slice(g * GROUP_GM, (g + 1) * GROUP_GM)
            wm = jnp.where(mask, ws_ref[g], 0.0).astype(BF16)
            mixed = _dot(wm, vnb[:, sl]) + bst_ref[:, g:g + 1]
            o_ref[:, sl] = (ug[:, sl] * mixed).astype(o_ref.dtype)

    return _pcall(
        body, grid=(s // GROUP_GM,),
        in_specs=[pl.BlockSpec((GROUP_GM, w), lambda c: (c, u_blk)), pl.BlockSpec((GROUP_GM, w), lambda c: (c, u_blk + 1)),
                  pl.BlockSpec((1, w), lambda c: (0, 0)), pl.BlockSpec((groups, GROUP_GM, GROUP_GM), lambda c: (0, 0, 0)),
                  pl.BlockSpec((GROUP_GM, groups), lambda c: (0, 0))],
        out_specs=pl.BlockSpec((GROUP_GM, w), lambda c: (c, 0)),
        out_shape=jax.ShapeDtypeStruct((s, w), BF16), name="gm_fwd",
        compiler_params=_params(("parallel",)))(proj, proj, g_vnorm, w_s, b_st)


def _gm_bwd(proj, g_vnorm, w_s, b_st, do, u_blk):
    s = proj.shape[0]
    groups = w_s.shape[0]
    w = groups * GROUP_GM

    def body(u_ref, v_ref, gv_ref, ws_ref, bst_ref, do_ref, du_ref, dv_ref, dgv_ref, dws_ref, dbst_ref, dvn_ref):
        @pl.when(pl.program_id(0) == 0)
        def _():
            dgv_ref[...] = jnp.zeros_like(dgv_ref)
            dws_ref[...] = jnp.zeros_like(dws_ref)
            dbst_ref[...] = jnp.zeros_like(dbst_ref)

        ug, ugrad = _gelu_and_grad(u_ref[...])
        vg, vgrad = _gelu_and_grad(v_ref[...])
        r = lax.rsqrt(jnp.mean(vg * vg, axis=-1, keepdims=True) + EPS)
        n = vg * r
        gv = gv_ref[...]
        vnb = (n * gv).astype(BF16)
        dout = do_ref[...]
        mask = _gm_mask()
        for g in range(groups):
            sl = slice(g * GROUP_GM, (g + 1) * GROUP_GM)
            wm = jnp.where(mask, ws_ref[g], 0.0).astype(BF16)
            mixed = _dot(wm, vnb[:, sl]) + bst_ref[:, g:g + 1]
            dmixed = dout[:, sl] * ug[:, sl]
            du_ref[:, sl] = dout[:, sl] * mixed * ugrad[:, sl]
            dbst_ref[:, g:g + 1] += jnp.sum(dmixed, axis=1, keepdims=True)
            dmb = dmixed.astype(BF16)
            dws_ref[g] += jnp.where(mask, _dot_nt(dmb, vnb[:, sl]), 0.0)
            dvn_ref[:, sl] = _dot_tn(wm, dmb)
        dvn = dvn_ref[...]
        dgv_ref[...] += jnp.sum(dvn * n, axis=0, keepdims=True)
        dn = dvn * gv
        dvg = r * (dn - n * jnp.mean(dn * n, axis=-1, keepdims=True))
        dv_ref[...] = dvg * vgrad

    rowb = pl.BlockSpec((GROUP_GM, w), lambda c: (c, 0))
    vec = pl.BlockSpec((1, w), lambda c: (0, 0))
    wsb = pl.BlockSpec((groups, GROUP_GM, GROUP_GM), lambda c: (0, 0, 0))
    bsb = pl.BlockSpec((GROUP_GM, groups), lambda c: (0, 0))
    return _pcall(
        body, grid=(s // GROUP_GM,),
        in_specs=[pl.BlockSpec((GROUP_GM, w), lambda c: (c, u_blk)), pl.BlockSpec((GROUP_GM, w), lambda c: (c, u_blk + 1)),
                  vec, wsb, bsb, rowb],
        out_specs=[rowb, rowb, vec, wsb, bsb],
        out_shape=[jax.ShapeDtypeStruct((s, w), F32), jax.ShapeDtypeStruct((s, w), F32), jax.ShapeDtypeStruct((1, w), F32),
                   jax.ShapeDtypeStruct((groups, GROUP_GM, GROUP_GM), F32), jax.ShapeDtypeStruct((GROUP_GM, groups), F32)],
        scratch_shapes=[pltpu.VMEM((GROUP_GM, w), F32)], name="gm_bwd",
        compiler_params=_params(("arbitrary",)))(proj, proj, g_vnorm, w_s, b_st, do)


def _xa_fwd(proj, mem_kv, q_blk, n_heads):
    s = proj.shape[0]
    nm = mem_kv.shape[0]
    tq = _tile(s, (512, 256, 128))
    scale = HEAD_XA ** -0.5

    def body(q_ref, k_ref, v_ref, o_ref):
        z = _dot_nt(q_ref[...].astype(BF16), k_ref[...].astype(BF16)) * scale
        z = z - jnp.max(z, axis=-1, keepdims=True)
        e = jnp.exp(z)
        p = e / jnp.sum(e, axis=-1, keepdims=True)
        o_ref[...] = _dot(p.astype(BF16), v_ref[...].astype(BF16)).astype(o_ref.dtype)

    h = n_heads
    return _pcall(
        body, grid=(h, s // tq),
        in_specs=[pl.BlockSpec((tq, HEAD_XA), lambda hh, i: (i, q_blk + hh)),
                  pl.BlockSpec((nm, HEAD_XA), lambda hh, i: (0, hh)), pl.BlockSpec((nm, HEAD_XA), lambda hh, i: (0, h + hh))],
        out_specs=pl.BlockSpec((tq, HEAD_XA), lambda hh, i: (i, hh)),
        out_shape=jax.ShapeDtypeStruct((s, h * HEAD_XA), BF16), name="xa_fwd",
        compiler_params=_params(("parallel", "parallel")))(proj, mem_kv, mem_kv)


def _xa_bwd(proj, mem_kv, do, q_blk, n_heads):
    s = proj.shape[0]
    nm = mem_kv.shape[0]
    tq = _tile(s, (512, 256, 128))
    scale = HEAD_XA ** -0.5
    h = n_heads

    def body(q_ref, k_ref, v_ref, do_ref, dq_ref, dk_ref, dv_ref):
        @pl.when(pl.program_id(1) == 0)
        def _():
            dk_ref[...] = jnp.zeros_like(dk_ref)
            dv_ref[...] = jnp.zeros_like(dv_ref)

        qb = q_ref[...].astype(BF16)
        kb = k_ref[...].astype(BF16)
        vb = v_ref[...].astype(BF16)
        dob = do_ref[...].astype(BF16)
        z = _dot_nt(qb, kb) * scale
        z = z - jnp.max(z, axis=-1, keepdims=True)
        e = jnp.exp(z)
        p = e / jnp.sum(e, axis=-1, keepdims=True)
        dp = _dot_nt(dob, vb)
        dz = (p * (dp - jnp.sum(dp * p, axis=-1, keepdims=True)) * scale).astype(BF16)
        dq_ref[...] = _dot(dz, kb)
        dk_ref[...] += _dot_tn(dz, qb)
        dv_ref[...] += _dot_tn(p.astype(BF16), dob)

    qspec = pl.BlockSpec((tq, HEAD_XA), lambda hh, i: (i, hh))
    dk, dv = None, None
    dq, dk, dv = _pcall(
        body, grid=(h, s // tq),
        in_specs=[pl.BlockSpec((tq, HEAD_XA), lambda hh, i: (i, q_blk + hh)),
                  pl.BlockSpec((nm, HEAD_XA), lambda hh, i: (0, hh)), pl.BlockSpec((nm, HEAD_XA), lambda hh, i: (0, h + hh)),
                  qspec],
        out_specs=[qspec, pl.BlockSpec((nm, HEAD_XA), lambda hh, i: (0, hh)), pl.BlockSpec((nm, HEAD_XA), lambda hh, i: (0, hh))],
        out_shape=[jax.ShapeDtypeStruct((s, h * HEAD_XA), F32), jax.ShapeDtypeStruct((nm, h * HEAD_XA), F32),
                   jax.ShapeDtypeStruct((nm, h * HEAD_XA), F32)],
        name="xa_bwd", compiler_params=_params(("parallel", "arbitrary")))(proj, mem_kv, mem_kv, do)
    return dq, dk, dv


def _merge_fwd(zg, b_gate, branches):
    s, d = branches[0].shape
    tr = _tile(s, (128,))

    def body(z0, z1, z2, g0, g1, g2, b0, b1, b2, o_ref):
        acc = None
        for z, g, b in ((z0, g0, b0), (z1, g1, b1), (z2, g2, b2)):
            term = jax.nn.sigmoid(z[...] + g[...]) * b[...]
            acc = term if acc is None else acc + term
        o_ref[...] = acc.astype(o_ref.dtype)

    zs = [pl.BlockSpec((tr, d), functools.partial(lambda i, k: (i, k), k=k)) for k in range(3)]
    gs = [pl.BlockSpec((1, d), functools.partial(lambda i, k: (0, k), k=k)) for k in range(3)]
    row = pl.BlockSpec((tr, d), lambda i: (i, 0))
    return _pcall(
        body, grid=(s // tr,), in_specs=zs + gs + [row] * 3, out_specs=row,
        out_shape=jax.ShapeDtypeStruct((s, d), BF16), name="merge_fwd",
        compiler_params=_params(("parallel",)))(zg, zg, zg, b_gate, b_gate, b_gate, *branches)


def _merge_bwd(zg, b_gate, branches, dmerged):
    s, d = branches[0].shape
    tr = _tile(s, (128,))

    def body(z0, z1, z2, g0, g1, g2, b0, b1, b2, dm_ref, dz_ref, d0, d1, d2, dbg_ref):
        @pl.when(pl.program_id(0) == 0)
        def _():
            dbg_ref[...] = jnp.zeros_like(dbg_ref)

        dm = dm_ref[...]
        for k, (z, g, b, dbr) in enumerate(((z0, g0, b0, d0), (z1, g1, b1, d1), (z2, g2, b2, d2))):
            sg = jax.nn.sigmoid(z[...] + g[...])
            dbr[...] = (dm * sg).astype(dbr.dtype)
            dz = dm * b[...] * sg * (1.0 - sg)
            dz_ref[:, k * d:(k + 1) * d] = dz.astype(dz_ref.dtype)
            dbg_ref[:, k * d:(k + 1) * d] += jnp.sum(dz, axis=0, keepdims=True)

    zs = [pl.BlockSpec((tr, d), functools.partial(lambda i, k: (i, k), k=k)) for k in range(3)]
    gs = [pl.BlockSpec((1, d), functools.partial(lambda i, k: (0, k), k=k)) for k in range(3)]
    row = pl.BlockSpec((tr, d), lambda i: (i, 0))
    outs = _pcall(
        body, grid=(s // tr,), in_specs=zs + gs + [row] * 4,
        out_specs=[pl.BlockSpec((tr, 3 * d), lambda i: (i, 0)), row, row, row, pl.BlockSpec((1, 3 * d), lambda i: (0, 0))],
        out_shape=[jax.ShapeDtypeStruct((s, 3 * d), BF16)] + [jax.ShapeDtypeStruct((s, d), BF16)] * 3
        + [jax.ShapeDtypeStruct((1, 3 * d), F32)],
        name="merge_bwd", compiler_params=_params(("arbitrary",)))(zg, zg, zg, b_gate, b_gate, b_gate, *branches, dmerged)
    return outs[0], list(outs[1:4]), outs[4]


def _shift_down(x, k, row):
    return jnp.where(row >= k, pltpu.roll(x, k, 0), 0.0)


def _shift_up(x, k, row, s):
    return jnp.where(row < s - k, pltpu.roll(x, s - k, 0), 0.0)


def _conv_pre(gate, cw_ref, cb_ref, row):
    conv = cb_ref[...] + cw_ref[CONV_TAPS - 1:CONV_TAPS, :] * gate
    for k in range(1, CONV_TAPS):
        conv = conv + cw_ref[CONV_TAPS - 1 - k:CONV_TAPS - k, :] * _shift_down(gate, k, row)
    return conv


def _cg_fwd(up, conv_w, conv_b):
    s = up.shape[0]
    f = conv_w.shape[1]
    tc = _tile(f, (256, 128))
    nb = f // tc

    def body(g_ref, v_ref, cw_ref, cb_ref, o_ref):
        row = lax.broadcasted_iota(jnp.int32, (s, tc), 0)
        conv = _conv_pre(g_ref[...], cw_ref, cb_ref, row)
        o_ref[...] = (_gelu(conv) * v_ref[...]).astype(o_ref.dtype)

    return _pcall(
        body, grid=(nb,),
        in_specs=[pl.BlockSpec((s, tc), lambda j: (0, j)), pl.BlockSpec((s, tc), lambda j: (0, nb + j)),
                  pl.BlockSpec((CONV_TAPS, tc), lambda j: (0, j)), pl.BlockSpec((1, tc), lambda j: (0, j))],
        out_specs=pl.BlockSpec((s, tc), lambda j: (0, j)),
        out_shape=jax.ShapeDtypeStruct((s, f), BF16), name="cg_fwd",
        compiler_params=_params(("parallel",)))(up, up, conv_w, conv_b)


def _cg_bwd(up, conv_w, conv_b, dact):
    s = up.shape[0]
    f = conv_w.shape[1]
    tc = _tile(f, (256, 128))
    nb = f // tc

    def body(g_ref, v_ref, cw_ref, cb_ref, da_ref, dg_ref, dv_ref, dcw_ref, dcb_ref):
        row = lax.broadcasted_iota(jnp.int32, (s, tc), 0)
        gate = g_ref[...]
        conv = _conv_pre(gate, cw_ref, cb_ref, row)
        gel, ggrad = _gelu_and_grad(conv)
        da = da_ref[...]
        dv_ref[...] = (da * gel).astype(dv_ref.dtype)
        dconv = da * v_ref[...] * ggrad
        dgate = cw_ref[CONV_TAPS - 1:CONV_TAPS, :] * dconv
        dcw_ref[CONV_TAPS - 1:CONV_TAPS, :] = jnp.sum(dconv * gate, axis=0, keepdims=True)
        for k in range(1, CONV_TAPS):
            dgate = dgate + cw_ref[CONV_TAPS - 1 - k:CONV_TAPS - k, :] * _shift_up(dconv, k, row, s)
            dcw_ref[CONV_TAPS - 1 - k:CONV_TAPS - k, :] = jnp.sum(dconv * _shift_down(gate, k, row), axis=0, keepdims=True)
        dg_ref[...] = dgate.astype(dg_ref.dtype)
        dcb_ref[...] = jnp.sum(dconv, axis=0, keepdims=True)

    colb = pl.BlockSpec((s, tc), lambda j: (0, j))
    return _pcall(
        body, grid=(nb,),
        in_specs=[colb, pl.BlockSpec((s, tc), lambda j: (0, nb + j)), pl.BlockSpec((CONV_TAPS, tc), lambda j: (0, j)),
                  pl.BlockSpec((1, tc), lambda j: (0, j)), colb],
        out_specs=[colb, colb, pl.BlockSpec((CONV_TAPS, tc), lambda j: (0, j)), pl.BlockSpec((1, tc), lambda j: (0, j))],
        out_shape=[jax.ShapeDtypeStruct((s, f), BF16), jax.ShapeDtypeStruct((s, f), BF16),
                   jax.ShapeDtypeStruct((CONV_TAPS, f), F32), jax.ShapeDtypeStruct((1, f), F32)],
        name="cg_bwd", compiler_params=_params(("parallel",)))(up, up, conv_w, conv_b, dact)


def _row_tile(rows, cols, elems=256 * 1024):
    want = max(16, elems // cols)
    for c in (512, 256, 128, 64, 32, 16):
        if c <= want and rows % c == 0:
            return c
    return rows


def _sum_halves(dwv, recv, core, name):
    nj, _, a, c = dwv.shape
    tr = _row_tile(a, c, 1024 * 1024)

    def body(core_ref, d_ref, r_ref, o_ref):
        o_ref[0] = (d_ref[0, 0].astype(F32) + r_ref[0].astype(F32)).astype(o_ref.dtype)

    grid_spec = pltpu.PrefetchScalarGridSpec(
        num_scalar_prefetch=1, grid=(nj, a // tr),
        in_specs=[pl.BlockSpec((1, 1, tr, c), lambda j, i, cr: (j, cr[0], i, 0)),
                  pl.BlockSpec((1, tr, c), lambda j, i, cr: (j, i, 0))],
        out_specs=pl.BlockSpec((1, tr, c), lambda j, i, cr: (j, i, 0)))
    return _pcall(body, grid_spec=grid_spec, out_shape=jax.ShapeDtypeStruct((nj, a, c), BF16), name=name,
                  compiler_params=_params(("parallel", "parallel")))(core, dwv, recv)


def _sum_chips(recv, own, chip, core, ax):
    _, a, b = recv.shape
    tr = _row_tile(a, b, 512 * 1024)

    def body(chip_ref, core_ref, r_ref, own_ref, o_ref):
        me = chip_ref[0]
        mine = own_ref[0].astype(F32)
        acc = None
        for k in range(N_CHIPS):
            term = jnp.where(me == k, mine, r_ref[k].astype(F32))
            acc = term if acc is None else acc + term
        o_ref[0] = acc

    own_spec = (pl.BlockSpec((1, tr, b), lambda i, ch, co: (0, i, ch[0])) if ax == 2
                else pl.BlockSpec((1, tr, b), lambda i, ch, co: (ch[0], i, 0)))
    grid_spec = pltpu.PrefetchScalarGridSpec(
        num_scalar_prefetch=2, grid=(a // tr,),
        in_specs=[pl.BlockSpec((N_CHIPS, tr, b), lambda i, ch, co: (0, i, 0)), own_spec],
        out_specs=pl.BlockSpec((1, tr, b), lambda i, ch, co: (co[0], i, 0)))
    return _pcall(body, grid_spec=grid_spec, out_shape=jax.ShapeDtypeStruct((2, a, b), F32),
                  name="sum_chips", compiler_params=_params(("parallel",)))(chip, core, recv, own)


def _place_own(wt, layer, chip, ax, after):
    nl, r, c = wt.shape
    half = r // 2
    tr = _row_tile(half, c, 512 * 1024)
    nb = half // tr

    def body(chip_ref, w_ref, after_ref, o_ref):
        o_ref[...] = w_ref[...].astype(BF16).reshape(o_ref.shape)

    if ax == 2:
        out_spec = pl.BlockSpec((1, tr, c), lambda h, i, ch: (h, i, ch[0]))
    else:
        out_spec = pl.BlockSpec((1, 1, tr, c), lambda h, i, ch: (ch[0], h, i, 0))
    grid_spec = pltpu.PrefetchScalarGridSpec(
        num_scalar_prefetch=1, grid=(2, nb),
        in_specs=[pl.BlockSpec((1, tr, c), lambda h, i, ch: (layer, h * nb + i, 0)), pl.BlockSpec(memory_space=pl.ANY)],
        out_specs=out_spec)
    return _pcall(body, grid_spec=grid_spec, out_shape=jax.ShapeDtypeStruct(_full_view_shape(wt.shape, ax), BF16),
                  name="place_own", compiler_params=_params(("parallel", "parallel")))(chip, wt, after)


def _adamw(w, m, v, g, layer, prev, name, after=None):
    nl, r, c = w.shape
    tr = _row_tile(r, c)
    c1 = 1.0 - ADAM_B1 ** ADAM_STEP
    c2 = 1.0 - ADAM_B2 ** ADAM_STEP

    follow = [] if after is None else [after]

    def body(w_ref, m_ref, v_ref, gin_ref, *rest):
        g_ref, d_ref, nm_ref, nv_ref = rest[-4:]
        g = gin_ref[...]
        mm = ADAM_B1 * m_ref[0] + (1.0 - ADAM_B1) * g
        vv = ADAM_B2 * v_ref[0] + (1.0 - ADAM_B2) * (g * g)
        g_ref[0] = g
        nm_ref[0] = mm
        nv_ref[0] = vv
        d_ref[0] = -ADAM_LR * ((mm / c1) / (jnp.sqrt(vv / c2) + ADAM_EPS) + ADAM_WD * w_ref[0])

    blk = pl.BlockSpec((1, tr, c), lambda i: (layer, i, 0))
    shape = jax.ShapeDtypeStruct((nl, r, c), F32)
    extra = [] if prev is None else list(prev)
    return _pcall(
        body, grid=(r // tr,),
        in_specs=[blk] * 3 + [pl.BlockSpec((tr, c), lambda i: (i, 0))] + [pl.BlockSpec(memory_space=pl.ANY)] * (len(extra) + len(follow)),
        out_specs=[blk] * 4, out_shape=[shape] * 4, input_output_aliases={4 + k: k for k in range(len(extra))}, name=name,
        compiler_params=_params(("parallel",)))(w, m, v, g, *extra, *follow)


HBM_SPEC = pl.BlockSpec(memory_space=pltpu.HBM)
COMM = pltpu.CompilerParams(has_side_effects=True)


def _position():
    x, y, c = lax.axis_index("x"), lax.axis_index("y"), lax.axis_index("c")
    chips = [(1 - x, y), (x, 1 - y), (1 - x, 1 - y)]
    return x, y, c, chips


def _remote(src, dst, send_sem, recv_sem, dev):
    return pltpu.make_async_remote_copy(src_ref=src, dst_ref=dst, send_sem=send_sem, recv_sem=recv_sem,
                                        device_id=dev, device_id_type=MESH)


def _full_view_shape(shard_shape, ax):
    _, r, c = shard_shape
    return (2, r // 2, c * N_CHIPS) if ax == 2 else (N_CHIPS, 2, r // 2, c)


def _piece(ref, ax, j, h, cs):
    if ax == 2:
        return ref.at[h, :, pl.ds(pl.multiple_of(j * cs, cs), cs)]
    return ref.at[j, h]


def _chip_block(ref, ax, j, cs):
    if ax == 2:
        return ref.at[:, :, pl.ds(pl.multiple_of(j * cs, cs), cs)]
    return ref.at[j]


SEM_SPEC = pl.BlockSpec(memory_space=pltpu.SEMAPHORE)
ANY_SPEC = pl.BlockSpec(memory_space=pl.ANY)
SPLIT = pltpu.CompilerParams(has_side_effects=pltpu.SideEffectType.DATAFLOW_SIDE_EFFECTING)


def _exchange(kind, name, bufs, build, n_sems, after=None, sems=None):
    n = len(bufs)
    if kind == 'sync':
        def body(*refs):
            mine, theirs = build(refs[n:2 * n], refs[2 * n], refs[2 * n + 1])
            for cp in mine:
                cp.start()
            for cp in theirs:
                cp.wait_recv()
            for cp in mine:
                cp.wait_send()

        return list(_pcall(
            body, in_specs=[HBM_SPEC] * n, out_specs=[HBM_SPEC] * n,
            out_shape=[jax.ShapeDtypeStruct(v.shape, v.dtype) for v in bufs], input_output_aliases={t: t for t in range(n)},
            scratch_shapes=[pltpu.SemaphoreType.DMA((n_sems,)), pltpu.SemaphoreType.DMA((n_sems,))],
            name=name, compiler_params=COMM)(*bufs))
    if kind == 'start':
        def body(*refs):
            mine, _ = build(refs[n + 3:2 * n + 3], refs[n + 1], refs[n + 2])
            for cp in mine:
                cp.start()
            refs[2 * n + 3][...] = jnp.zeros_like(refs[2 * n + 3])

        outs = _pcall(
            body, in_specs=[HBM_SPEC] * n + [ANY_SPEC],
            out_specs=[SEM_SPEC, SEM_SPEC] + [HBM_SPEC] * n + [pl.BlockSpec(memory_space=pltpu.VMEM)],
            out_shape=[pltpu.SemaphoreType.DMA((n_sems,)), pltpu.SemaphoreType.DMA((n_sems,))]
            + [pltpu.HBM(v.shape, v.dtype) for v in bufs] + [jax.ShapeDtypeStruct((8, LANES), F32)],
            input_output_aliases={t: 2 + t for t in range(n)}, name=name,
            compiler_params=SPLIT)(*[pltpu.with_memory_space_constraint(v, pltpu.HBM) for v in bufs], after)
        return (outs[0], outs[1]), list(outs[2:2 + n]), outs[2 + n]

    def body(*refs):
        mine, theirs = build(refs[:n], refs[n], refs[n + 1])
        for cp in mine:
            cp.wait_send()
        for cp in theirs:
            cp.wait_recv()

    return list(_pcall(
        body, in_specs=[HBM_SPEC] * n + [SEM_SPEC, SEM_SPEC, ANY_SPEC], out_specs=[HBM_SPEC] * n,
        out_shape=[pltpu.HBM(v.shape, v.dtype) for v in bufs], input_output_aliases={t: t for t in range(n)},
        name=name, compiler_params=SPLIT)(*bufs, sems[0], sems[1], after))


def _gather_ici_copies(axes, shard_cols):
    def build(bufs, send_sems, recv_sems):
        x, y, c, chips = _position()
        me = 2 * x + y
        mine, theirs = [], []
        for t, ax in enumerate(axes):
            own = _piece(bufs[t], ax, me, c, shard_cols[t])
            for p, (px, py) in enumerate(chips):
                k = t * 3 + p
                got = _piece(bufs[t], ax, 2 * px + py, c, shard_cols[t])
                mine.append(_remote(own, own, send_sems.at[k], recv_sems.at[k], (px, py, c)))
                theirs.append(_remote(got, got, send_sems.at[k], recv_sems.at[k], (px, py, c)))
        return mine, theirs
    return build


def _gather_d2d_copies(axes, shard_cols):
    def build(bufs, send_sems, recv_sems):
        x, y, c, chips = _position()
        mine, theirs = [], []
        for t, ax in enumerate(axes):
            for p, (px, py) in enumerate(chips):
                k = t * 3 + p
                had = _piece(bufs[t], ax, 2 * px + py, c, shard_cols[t])
                got = _piece(bufs[t], ax, 2 * px + py, 1 - c, shard_cols[t])
                mine.append(_remote(had, had, send_sems.at[k], recv_sems.at[k], (x, y, 1 - c)))
                theirs.append(_remote(got, got, send_sems.at[k], recv_sems.at[k], (x, y, 1 - c)))
        return mine, theirs
    return build


def _grads_d2d_copies(n):
    def build(bufs, send_sems, recv_sems):
        x, y, c, _ = _position()
        mine = [_remote(bufs[t].at[:, 1 - c], bufs[n + t], send_sems.at[t], recv_sems.at[t], (x, y, 1 - c)) for t in range(n)]
        return mine, mine
    return build


def _grads_ici_copies(axes):
    n = len(axes)

    def build(bufs, send_sems, recv_sems):
        x, y, c, chips = _position()
        me = 2 * x + y

        def block(t, j):
            if axes[t] == 2:
                cs = bufs[n + t].shape[2]
                return bufs[t].at[0, :, pl.ds(pl.multiple_of(j * cs, cs), cs)]
            return bufs[t].at[j]

        mine, theirs = [], []
        for t in range(n):
            for p, (px, py) in enumerate(chips):
                k = t * 3 + p
                peer = 2 * px + py
                mine.append(_remote(block(t, peer), bufs[n + t].at[me], send_sems.at[k], recv_sems.at[k], (px, py, c)))
                theirs.append(_remote(block(t, peer), bufs[n + t].at[peer], send_sems.at[k], recv_sems.at[k], (px, py, c)))
        return mine, theirs
    return build


def _join_copies(bufs, send_sems, recv_sems):
    x, y, c, _ = _position()
    mine = [_remote(b.at[c], b.at[c], send_sems.at[t], recv_sems.at[t], (x, y, 1 - c)) for t, b in enumerate(bufs)]
    theirs = [_remote(b.at[1 - c], b.at[1 - c], send_sems.at[t], recv_sems.at[t], (x, y, 1 - c)) for t, b in enumerate(bufs)]
    return mine, theirs


def _grads_recv_shape(sm, ax):
    _, a, c = sm.shape
    return (N_CHIPS, a, c // N_CHIPS if ax == 2 else c)


def _gather_small(shard):
    nl, r, cs = shard.shape

    def body(in_ref, out_ref, send_sems, recv_sems, local_sem):
        x, y, c, chips = _position()

        def cols(j):
            return out_ref.at[:, :, pl.ds(pl.multiple_of(j * cs, cs), cs)]

        me = 2 * x + y
        loc = pltpu.make_async_copy(in_ref, cols(me), local_sem)
        loc.start()
        remote = [_remote(in_ref, cols(me), send_sems.at[p], recv_sems.at[p], (px, py, c)) for p, (px, py) in enumerate(chips)]
        for cp in remote:
            cp.start()
        for p, (px, py) in enumerate(chips):
            _remote(in_ref, cols(2 * px + py), send_sems.at[p], recv_sems.at[p], (px, py, c)).wait_recv()
        for cp in remote:
            cp.wait_send()
        loc.wait()

    return _pcall(
        body, in_specs=[HBM_SPEC], out_specs=HBM_SPEC, out_shape=jax.ShapeDtypeStruct((nl, r, cs * N_CHIPS), shard.dtype),
        scratch_shapes=[pltpu.SemaphoreType.DMA((3,)), pltpu.SemaphoreType.DMA((3,)), pltpu.SemaphoreType.DMA(())],
        name="gather_small", compiler_params=COMM)(shard)


def _small_copies(bufs, send_sems, recv_sems):
    pack, land = bufs
    x, y, cc, _ = _position()
    me = 4 * x + 2 * y + cc
    mine, theirs = [], []
    for k in range(1, N_DEV):
        px, py, pc = x ^ ((k >> 2) & 1), y ^ ((k >> 1) & 1), cc ^ (k & 1)
        mine.append(_remote(pack, land.at[me], send_sems.at[k - 1], recv_sems.at[k - 1], (px, py, pc)))
        theirs.append(_remote(pack, land.at[4 * px + 2 * py + pc], send_sems.at[k - 1], recv_sems.at[k - 1], (px, py, pc)))
    return mine, theirs


def _sum_devices(land, pack, dev):
    _, r, c = land.shape
    tr = _tile(r, (672, 512, 256, 128, 64, 8))

    def body(dev_ref, l_ref, p_ref, o_ref):
        me = dev_ref[0]
        acc = None
        for k in range(N_DEV):
            term = jnp.where(me == k, p_ref[...], l_ref[k])
            acc = term if acc is None else acc + term
        o_ref[...] = acc

    grid_spec = pltpu.PrefetchScalarGridSpec(
        num_scalar_prefetch=1, grid=(r // tr,),
        in_specs=[pl.BlockSpec((N_DEV, tr, c), lambda i, dv: (0, i, 0)), pl.BlockSpec((tr, c), lambda i, dv: (i, 0))],
        out_specs=pl.BlockSpec((tr, c), lambda i, dv: (i, 0)))
    return _pcall(body, grid_spec=grid_spec, out_shape=jax.ShapeDtypeStruct((r, c), F32), name="sum_devices",
                  compiler_params=_params(("parallel",)))(dev, land, pack)


def _dims(d):
    half = d // 2
    return half // HEAD_SB, half // HEAD_XA, 3, (5 * half) // HEAD_XA


def _layer_fwd(x, mem, weight, small, l, after=None):
    h_sb, h_xa, u_blk, q_blk = _dims(x.shape[1])

    def vec(name):
        return small[name][l].reshape(1, -1)

    def use(a, name, follows, mm_name):
        wt = weight(name, follows)
        return _mm(a, wt, 'nn', F32, mm_name, after=weight.token())

    h1 = _norm_fwd(x, vec('g_mix_pre'), None, BF16, "norm_mix_pre", after)
    proj = use(h1, 'w_in', h1, "mm_proj")
    o_sb, a_sb = _sb_fwd(proj, h_sb)
    b_st = small['b_s'][l].T
    o_gm = _gm_fwd(proj, vec('g_vnorm'), small['w_s'][l], b_st, u_blk)
    memn = _norm_fwd(mem, vec('g_mem'), None, BF16, "norm_mem")
    mem_kv = use(memn, 'w_mem_kv', o_sb, "mm_mem_kv")
    o_xa = _xa_fwd(proj, mem_kv, q_blk, h_xa)
    zg = use(h1, 'w_gate', o_sb, "mm_gate")
    branches = [use(o, wn, zg, "mm_branch") for o, wn in ((o_sb, 'w_br_sb'), (o_gm, 'w_br_gm'), (o_xa, 'w_br_xa'))]
    merged = _merge_fwd(zg, vec('b_gate'), branches)
    y1 = use(merged, 'w_out', zg, "mm_out")
    x1 = _norm_fwd(y1, vec('g_mix_post'), x, F32, "norm_mix_post")
    h2 = _norm_fwd(x1, vec('g_ffn_pre'), None, BF16, "norm_ffn_pre")
    up = use(h2, 'w_up', h2, "mm_up")
    act = _cg_fwd(up, weight('conv_w', up), vec('conv_b'))
    y2 = use(act, 'w_down', act, "mm_down")
    x2 = _norm_fwd(y2, vec('g_ffn_post'), x1, F32, "norm_ffn_post")
    saved = dict(x0=x, h1=h1, proj=proj, o_sb=o_sb, a_sb=a_sb, o_gm=o_gm, o_xa=o_xa, memn=memn, mem_kv=mem_kv, zg=zg,
                 branches=branches, merged=merged, y1=y1, x1=x1, h2=h2, up=up, act=act, y2=y2, b_st=b_st)
    return x2, saved


def _layer_bwd(dx, mem, sv, full, small, l, after, emit):
    h_sb, h_xa, u_blk, q_blk = _dims(dx.shape[1])

    def vec(name):
        return small[name][l].reshape(1, -1)

    gb, gs = {}, {}
    dy2, gs['g_ffn_post'] = _norm_bwd(sv['y2'], vec('g_ffn_post'), [dx], None, BF16, "norm_ffn_post_bwd", after)
    gb['w_down'] = _mm(sv['act'], dy2, 'tn', BF16, "mm_down_dw")
    dact = _mm(dy2, full['w_down'], 'nt', F32, "mm_down_dx")
    dgate, dval, gs['conv_w'], gs['conv_b'] = _cg_bwd(sv['up'], full['conv_w'], vec('conv_b'), dact)
    dup = jnp.concatenate([dgate, dval], axis=1)
    gb['w_up'] = _mm(sv['h2'], dup, 'tn', BF16, "mm_up_dw")
    token = emit(0, gb)
    dh2 = _mm(dup, full['w_up'], 'nt', F32, "mm_up_dx", after=token)
    token = emit.flush(dh2)
    dx1, gs['g_ffn_pre'] = _norm_bwd(sv['x1'], vec('g_ffn_pre'), [dh2], dx, F32, "norm_ffn_pre_bwd", token)
    dy1, gs['g_mix_post'] = _norm_bwd(sv['y1'], vec('g_mix_post'), [dx1], None, BF16, "norm_mix_post_bwd")
    gb['w_out'] = _mm(sv['merged'], dy1, 'tn', BF16, "mm_out_dw")
    dmerged = _mm(dy1, full['w_out'], 'nt', F32, "mm_out_dx")
    dzg, dbr, gs['b_gate'] = _merge_bwd(sv['zg'], vec('b_gate'), sv['branches'], dmerged)
    douts = []
    for o, db, wn in ((sv['o_sb'], dbr[0], 'w_br_sb'), (sv['o_gm'], dbr[1], 'w_br_gm'), (sv['o_xa'], dbr[2], 'w_br_xa')):
        gb[wn] = _mm(o, db, 'tn', BF16, "mm_branch_dw")
        douts.append(_mm(db, full[wn], 'nt', F32, "mm_branch_dx"))
    gb['w_gate'] = _mm(sv['h1'], dzg, 'tn', BF16, "mm_gate_dw")
    dq_xa, dk_xa, dv_xa = _xa_bwd(sv['proj'], sv['mem_kv'], douts[2], q_blk, h_xa)
    dmem_kv = jnp.concatenate([dk_xa, dv_xa], axis=1).astype(BF16)
    gb['w_mem_kv'] = _mm(sv['memn'], dmem_kv, 'tn', BF16, "mm_mem_kv_dw")
    token = emit(1, gb)
    dh1_gate = _mm(dzg, full['w_gate'], 'nt', F32, "mm_gate_dx", after=token)
    token = emit.flush(dh1_gate)
    dmemn = _mm(dmem_kv, full['w_mem_kv'], 'nt', F32, "mm_mem_kv_dx")
    _, gs['g_mem'] = _norm_bwd(mem, vec('g_mem'), [dmemn], None, BF16, "norm_mem_bwd")
    du, dv, gs['g_vnorm'], gs['w_s'], db_st = _gm_bwd(sv['proj'], vec('g_vnorm'), small['w_s'][l], sv['b_st'], douts[1], u_blk)
    gs['b_s'] = db_st.T
    dq, dk, dvv = _sb_bwd(sv['proj'], sv['a_sb'], douts[0], h_sb, token)
    dproj = jnp.concatenate([dq, dk, dvv, du, dv, dq_xa], axis=1).astype(BF16)
    gb['w_in'] = _mm(sv['h1'], dproj, 'tn', BF16, "mm_proj_dw")
    token = emit(2, gb)
    dh1_proj = _mm(dproj, full['w_in'], 'nt', F32, "mm_proj_dx")
    dx0, gs['g_mix_pre'] = _norm_bwd(sv['x0'], vec('g_mix_pre'), [dh1_gate, dh1_proj], dx1, F32, "norm_mix_pre_bwd", token)
    return dx0, gs


class _Given:
    def __init__(self, full):
        self.full = full

    def __call__(self, name, follows):
        return self.full[name]

    def token(self):
        return None


def _local_step(x, mem, target, full, small):
    n_layers = len(full['w_in'])
    saved = []
    for l in range(n_layers):
        x, sv = _layer_fwd(x, mem, _Given({n: full[n][l] for n in full}), small, l)
        saved.append(sv)
    sq, dx = _loss_head(x, target)
    gbig = {n: [None] * n_layers for n in BIG}
    gsmall = {n: [None] * n_layers for n in SMALL + ['conv_w']}
    class Collect:
        def __init__(self, l):
            self.l = l

        def __call__(self, g, gb):
            for n in BWD_GROUPS[g]:
                gbig[n][self.l] = gb[n]

        def flush(self, follows):
            return None

    for l in reversed(range(n_layers)):
        dx, gs = _layer_bwd(dx, mem, saved[l], {n: full[n][l] for n in full}, small, l, None, Collect(l))
        for n in gs:
            gsmall[n][l] = gs[n]
    return sq, dx, gbig, gsmall


def _pack(arrays, rows_multiple):
    flat = jnp.concatenate([a.reshape(-1).astype(F32) for a in arrays])
    rows = -(-flat.shape[0] // LANES)
    rows = -(-rows // rows_multiple) * rows_multiple
    return jnp.pad(flat, (0, rows * LANES - flat.shape[0])).reshape(rows, LANES)


def _unpack(pack, like):
    flat = pack.reshape(-1)
    out, off = [], 0
    for a in like:
        out.append(flat[off:off + a.size].reshape(a.shape))
        off += a.size
    return out


def _grad_view(g, ax):
    r, c = g.shape
    return g.reshape(1, 2, r // 2, c) if ax == 2 else g.reshape(N_CHIPS, 2, r // (2 * N_CHIPS), c)


def kernel(x, mem, g_mix_pre, w_in, g_vnorm, w_s, b_s, g_mem, w_mem_kv, w_gate, b_gate, w_br_sb, w_br_gm, w_br_xa, w_out, g_mix_post, g_ffn_pre, w_up, conv_w, conv_b, w_down, g_ffn_post, loss_target, m_g_mix_pre, m_w_in, m_g_vnorm, m_w_s, m_b_s, m_g_mem, m_w_mem_kv, m_w_gate, m_b_gate, m_w_br_sb, m_w_br_gm, m_w_br_xa, m_w_out, m_g_mix_post, m_g_ffn_pre, m_w_up, m_conv_w, m_conv_b, m_w_down, m_g_ffn_post, v_g_mix_pre, v_w_in, v_g_vnorm, v_w_s, v_b_s, v_g_mem, v_w_mem_kv, v_w_gate, v_b_gate, v_w_br_sb, v_w_br_gm, v_w_br_xa, v_w_out, v_g_mix_post, v_g_ffn_pre, v_w_up, v_conv_w, v_conv_b, v_w_down, v_g_ffn_post):
    w = dict(g_mix_pre=g_mix_pre, w_in=w_in, g_vnorm=g_vnorm, w_s=w_s, b_s=b_s, g_mem=g_mem, w_mem_kv=w_mem_kv,
             w_gate=w_gate, b_gate=b_gate, w_br_sb=w_br_sb, w_br_gm=w_br_gm, w_br_xa=w_br_xa, w_out=w_out,
             g_mix_post=g_mix_post, g_ffn_pre=g_ffn_pre, w_up=w_up, conv_w=conv_w, conv_b=conv_b, w_down=w_down,
             g_ffn_post=g_ffn_post)
    m = dict(g_mix_pre=m_g_mix_pre, w_in=m_w_in, g_vnorm=m_g_vnorm, w_s=m_w_s, b_s=m_b_s, g_mem=m_g_mem,
             w_mem_kv=m_w_mem_kv, w_gate=m_w_gate, b_gate=m_b_gate, w_br_sb=m_w_br_sb, w_br_gm=m_w_br_gm,
             w_br_xa=m_w_br_xa, w_out=m_w_out, g_mix_post=m_g_mix_post, g_ffn_pre=m_g_ffn_pre, w_up=m_w_up,
             conv_w=m_conv_w, conv_b=m_conv_b, w_down=m_w_down, g_ffn_post=m_g_ffn_post)
    v = dict(g_mix_pre=v_g_mix_pre, w_in=v_w_in, g_vnorm=v_g_vnorm, w_s=v_w_s, b_s=v_b_s, g_mem=v_g_mem,
             w_mem_kv=v_w_mem_kv, w_gate=v_w_gate, b_gate=v_b_gate, w_br_sb=v_w_br_sb, w_br_gm=v_w_br_gm,
             w_br_xa=v_w_br_xa, w_out=v_w_out, g_mix_post=v_g_mix_post, g_ffn_pre=v_g_ffn_pre, w_up=v_w_up,
             conv_w=v_conv_w, conv_b=v_conv_b, w_down=v_w_down, g_ffn_post=v_g_ffn_post)
    n_layers = w_in.shape[0]
    d = x.shape[-1]
    core = lax.axis_index("c").astype(jnp.int32).reshape(1)
    chip = (2 * lax.axis_index("x") + lax.axis_index("y")).astype(jnp.int32).reshape(1)
    small = {n: w[n] for n in SMALL}
    xs, mems, target = x[0], mem[0], loss_target[0]

    conv_w_full = _gather_small(conv_w)

    def as_full(vw, ax):
        return vw.reshape(-1, vw.shape[-1]) if ax == 1 else vw.reshape(vw.shape[0] * vw.shape[1], vw.shape[2])

    stages, token = {}, conv_w_full
    keys = [(l, g) for l in range(n_layers) for g in range(len(FWD_GROUPS))]
    for l, g in keys:
        names = FWD_GROUPS[g]
        ax_g = [BIG_AXIS[n] for n in names]
        cols_g = [w[n].shape[2] for n in names]
        views = [_place_own(w[n], l, chip, ax, token) for n, ax in zip(names, ax_g)]
        sems, views, token = _exchange('start', f"gather_ici_start_{l}_{g}", views, _gather_ici_copies(ax_g, cols_g),
                                       3 * len(names), after=token)
        stages[l, g] = dict(names=names, ax=ax_g, cols=cols_g, views=views, ici=sems, d2d=None, full=None)

    def cross_cores(key, follows):
        st, (l, g) = stages[key], key
        n3 = 3 * len(st['names'])
        views = _exchange('wait', f"gather_ici_wait_{l}_{g}", st['views'], _gather_ici_copies(st['ax'], st['cols']), n3,
                          after=follows, sems=st['ici'])
        st['d2d'], st['views'], tok = _exchange('start', f"gather_d2d_start_{l}_{g}", views,
                                                _gather_d2d_copies(st['ax'], st['cols']), n3, after=core)
        return tok

    class Weights:
        def __init__(self, l):
            self.l, self.tok = l, None

        def __call__(self, name, follows):
            if name == 'conv_w':
                return conv_w_full[self.l]
            key = (self.l, [g for g, names in enumerate(FWD_GROUPS) if name in names][0])
            st = stages[key]
            if st['full'] is None:
                if st['d2d'] is None:
                    cross_cores(key, follows)
                views = _exchange('wait', f"gather_d2d_wait_{key[0]}_{key[1]}", st['views'],
                                  _gather_d2d_copies(st['ax'], st['cols']), 3 * len(st['names']), after=follows, sems=st['d2d'])
                st['full'] = {n: as_full(vw, ax) for n, vw, ax in zip(st['names'], views, st['ax'])}
                nxt = keys.index(key) + 1
                if nxt < len(keys) and key != keys[0]:
                    self.tok = cross_cores(keys[nxt], views[0])
            return st['full'][name]

        def token(self):
            return self.tok

    fulls, saved = [], []
    for l in range(n_layers):
        xs, sv = _layer_fwd(xs, mems, Weights(l), small, l, token if l == 0 else None)
        fulls.append({n: stages[l, g]['full'][n] for g, names in enumerate(FWD_GROUPS) for n in names} | {'conv_w': conv_w_full[l]})
        saved.append(sv)
    sq, dx = _loss_head(xs, target)
    loss = lax.psum(0.5 * jnp.sum(sq) / d, ("x", "y", "c"))

    sent = []

    def to_chips(l, g, names, ax_g, bufs, after):
        n = len(names)
        sums = [_sum_halves(dv, th, core, "sum_halves") for dv, th in zip(bufs[:n], bufs[n:])]
        lands = [lax.empty(_grads_recv_shape(sm, ax), sm.dtype) for sm, ax in zip(sums, ax_g)]
        sems, bufs, tok = _exchange('start', f"grads_ici_start_{l}_{g}", sums + lands, _grads_ici_copies(ax_g), 3 * n,
                                    after=core if after is None else after)
        sent.append((l, g, names, ax_g, bufs, sems))
        return tok

    class Grads:
        def __init__(self, l):
            self.l, self.crossing = l, None

        def __call__(self, g, gb):
            names = BWD_GROUPS[g]
            ax_g = [BIG_AXIS[n] for n in names]
            n = len(names)
            dwvs = [_grad_view(gb[nm], ax) for nm, ax in zip(names, ax_g)]
            lands = [lax.empty((dv.shape[0],) + dv.shape[2:], dv.dtype) for dv in dwvs]
            if g + 1 < len(BWD_GROUPS):
                sems, bufs, tok = _exchange('start', f"grads_d2d_start_{self.l}_{g}", dwvs + lands, _grads_d2d_copies(n), n,
                                            after=core)
                self.crossing = (g, names, ax_g, bufs, sems)
                return tok
            return to_chips(self.l, g, names, ax_g, _exchange('sync', "grads_d2d", dwvs + lands, _grads_d2d_copies(n), n), None)

        def flush(self, follows):
            if self.crossing is None:
                return None
            (g, names, ax_g, bufs, sems), self.crossing = self.crossing, None
            bufs = _exchange('wait', f"grads_d2d_wait_{self.l}_{g}", bufs, _grads_d2d_copies(len(names)), len(names),
                             after=follows, sems=sems)
            return to_chips(self.l, g, names, ax_g, bufs, None)

    gsmall = {n: [None] * n_layers for n in SMALL + ['conv_w']}
    for l in reversed(range(n_layers)):
        dx, gs = _layer_bwd(dx, mems, saved[l], fulls[l], small, l, None, Grads(l))
        for n in gs:
            gsmall[n][l] = gs[n]

    names_small = SMALL + ['conv_w']
    small_full = [jnp.stack(gsmall[n]).reshape(w[n].shape) for n in SMALL]
    conv_w_grad = jnp.stack(gsmall['conv_w'])
    pack = _pack(small_full + [conv_w_grad], 8)
    small_sems, small_bufs, _ = _exchange('start', "small_start", [pack, lax.empty((N_DEV,) + pack.shape, F32)], _small_copies,
                                          N_DEV - 1, after=dx)

    def small_update(follows):
        pk, land = _exchange('wait', "small_wait", small_bufs, _small_copies, N_DEV - 1, after=follows, sems=small_sems)
        device = (4 * lax.axis_index("x") + 2 * lax.axis_index("y") + lax.axis_index("c")).astype(jnp.int32).reshape(1)
        summed = _sum_devices(land, pk, device)
        *small_g, conv_w_g = _unpack(summed, small_full + [conv_w_grad])
        shard = conv_w.shape[-1]
        conv_w_g = lax.dynamic_slice_in_dim(conv_w_g, chip[0] * shard, shard, axis=2)
        packed = [_pack([p[n] for n in names_small], 256) for p in (w, m, v)]
        gpack = _pack(small_g + [conv_w_g], 256)
        res = _adamw(packed[0][None], packed[1][None], packed[2][None], gpack, 0, None, "adamw_small")
        like = [w[n] for n in names_small]
        unpacked = [_unpack(r[0], like) for r in res]
        return {n: tuple(u[i] for u in unpacked) for i, n in enumerate(names_small)}

    out = {}

    def update(joining, follows):
        l, g, names, halves, sems = joining
        halves = _exchange('wait', f"join_wait_{l}_{g}", halves, _join_copies, len(names), after=follows, sems=sems)
        last = None
        for n, hv in zip(names, halves):
            out[n] = _adamw(w[n], m[n], v[n], hv.reshape(w[n].shape[1:]), l, out.get(n), "adamw_big", last)
            last = out[n][0]
        return last

    joining, follows = None, dx
    for k, (l, g, names, ax_g, bufs, sems) in enumerate(sent):
        n = len(names)
        bufs = _exchange('wait', f"grads_ici_wait_{l}_{g}", bufs, _grads_ici_copies(ax_g), 3 * n, after=follows, sems=sems)
        halves = [_sum_chips(r, sm, chip, core, ax) for sm, r, ax in zip(bufs[:n], bufs[n:], ax_g)]
        jsems, halves, tok = _exchange('start', f"join_start_{l}_{g}", halves, _join_copies, n, after=core)
        if joining is not None:
            follows = update(joining, tok)
        joining = (l, g, names, halves, jsems)
    out.update(small_update(update(joining, follows)))

    return (loss, dx[None], *[out[n][0] for n in WEIGHTS], *[out[n][1] for n in WEIGHTS],
            *[out[n][2] for n in WEIGHTS], *[out[n][3] for n in WEIGHTS])
```

```python
import functools
import math

import jax
import jax.numpy as jnp
from jax import lax
from jax.experimental import pallas as pl
from jax.experimental.pallas import tpu as pltpu

F32 = jnp.float32
BF16 = jnp.bfloat16
EPS = 1e-6
HEAD_SB = 128
GROUP_GM = 128
CHUNK = 64
HEAD_XA = 256
CONV_TAPS = 3
N_CHIPS = 4
N_DEV = 8
LANES = 128
MIB = 1024 * 1024
VMEM_LIMIT = 48 * MIB
SPLITS = 2

ADAM_LR = 0.001
ADAM_B1 = 0.9
ADAM_B2 = 0.999
ADAM_EPS = 1e-08
ADAM_WD = 0.01
ADAM_STEP = 10

WEIGHTS = ['g_mix_pre', 'w_in', 'g_vnorm', 'w_s', 'b_s', 'g_mem', 'w_mem_kv', 'w_gate', 'b_gate', 'w_br_sb',
           'w_br_gm', 'w_br_xa', 'w_out', 'g_mix_post', 'g_ffn_pre', 'w_up', 'conv_w', 'conv_b', 'w_down',
           'g_ffn_post']
BIG_AXIS = {'w_in': 2, 'w_mem_kv': 1, 'w_gate': 2, 'w_br_sb': 2, 'w_br_gm': 2, 'w_br_xa': 2, 'w_out': 1,
            'w_up': 2, 'w_down': 1}
BIG = list(BIG_AXIS)
FWD_GROUPS = [['w_in'], ['w_mem_kv', 'w_gate'], ['w_br_sb', 'w_br_gm', 'w_br_xa', 'w_out'], ['w_up'], ['w_down']]
BWD_GROUPS = [['w_down', 'w_up'], ['w_out', 'w_br_sb', 'w_br_gm', 'w_br_xa', 'w_gate', 'w_mem_kv'], ['w_in']]
SMALL = ['g_mix_pre', 'g_vnorm', 'w_s', 'b_s', 'g_mem', 'b_gate', 'g_mix_post', 'g_ffn_pre', 'conv_b', 'g_ffn_post']
MESH = pl.DeviceIdType.MESH


def _pcall(body, **kw):
    return pl.pallas_call(body, **kw)


def _params(sem=None, vmem=VMEM_LIMIT):
    return pltpu.CompilerParams(dimension_semantics=sem, vmem_limit_bytes=vmem)


def _tile(n, cands):
    for c in cands:
        if n % c == 0:
            return c
    return n


_GELU_C = math.sqrt(2.0 / math.pi)
_GELU_A = 0.044715


def _gelu(x):
    return 0.5 * x * (1.0 + jnp.tanh(_GELU_C * (x + _GELU_A * (x * x * x))))


def _gelu_and_grad(x):
    x2 = x * x
    t = jnp.tanh(_GELU_C * (x + _GELU_A * (x2 * x)))
    val = 0.5 * x * (1.0 + t)
    grad = 0.5 * (1.0 + t) + 0.5 * x * (1.0 - t * t) * (_GELU_C * (1.0 + 3.0 * _GELU_A * x2))
    return val, grad


def _softplus(z):
    return jnp.maximum(z, 0.0) + jnp.log(1.0 + jnp.exp(-jnp.abs(z)))


def _dot(a, b):
    return jnp.dot(a, b, preferred_element_type=F32)


def _dot_nt(a, b):
    return lax.dot_general(a, b, (((1,), (1,)), ((), ())), preferred_element_type=F32)


def _dot_tn(a, b):
    return lax.dot_general(a, b, (((0,), (0,)), ((), ())), preferred_element_type=F32)


def _split_dot(a, m):
    out = None
    rest = a
    for _ in range(SPLITS):
        piece = rest.astype(BF16)
        rest = rest - piece.astype(F32)
        term = _dot(piece, m)
        out = term if out is None else out + term
    return out


def _mm(a, b, mode, out_dtype, name, tm=None, tn=None, tk=None, after=None):
    if mode == 'nn':
        (m, kc), (kc2, n) = a.shape, b.shape
    elif mode == 'nt':
        (m, kc), (n, kc2) = a.shape, b.shape
    else:
        (kc, m), (kc2, n) = a.shape, b.shape
    assert kc == kc2, (a.shape, b.shape, mode)
    tm = tm or _tile(m, (1024, 512, 256, 128))
    tn = tn or _tile(n, (1024, 512, 256, 128))
    tk = tk or (kc if kc <= 3072 else _tile(kc, (3072, 2816, 2048, 1536, 1408, 1024, 512)))
    nk = kc // tk
    dot = {'nn': _dot, 'nt': _dot_nt, 'tn': _dot_tn}[mode]
    a_spec = pl.BlockSpec((tk, tm), lambda i, j, k: (k, i)) if mode == 'tn' else pl.BlockSpec((tm, tk), lambda i, j, k: (i, k))
    b_spec = pl.BlockSpec((tn, tk), lambda i, j, k: (j, k)) if mode == 'nt' else pl.BlockSpec((tk, tn), lambda i, j, k: (k, j))

    extra = [] if after is None else [after]
    extra_specs = [pl.BlockSpec(memory_space=pl.ANY)] * len(extra)

    if nk == 1:
        def body(a_ref, b_ref, *rest):
            o_ref = rest[-1]
            o_ref[...] = dot(a_ref[...].astype(BF16), b_ref[...].astype(BF16)).astype(o_ref.dtype)
        scratch = []
    else:
        def body(a_ref, b_ref, *rest):
            o_ref, acc_ref = rest[-2], rest[-1]
            k = pl.program_id(2)
            part = dot(a_ref[...].astype(BF16), b_ref[...].astype(BF16))

            @pl.when(k == 0)
            def _():
                acc_ref[...] = part

            @pl.when(k > 0)
            def _():
                acc_ref[...] += part

            @pl.when(k == nk - 1)
            def _():
                o_ref[...] = acc_ref[...].astype(o_ref.dtype)
        scratch = [pltpu.VMEM((tm, tn), F32)]

    return _pcall(
        body, grid=(m // tm, n // tn, nk), in_specs=[a_spec, b_spec] + extra_specs,
        out_specs=pl.BlockSpec((tm, tn), lambda i, j, k: (i, j)),
        out_shape=jax.ShapeDtypeStruct((m, n), out_dtype), scratch_shapes=scratch, name=name,
        compiler_params=_params(("parallel", "parallel", "arbitrary")))(a, b, *extra)


def _norm_fwd(x, g, res, out_dtype, name, after=None):
    s, d = x.shape
    tr = _tile(s, (256, 128))
    has_res = res is not None
    has_after = after is not None

    def body(*refs):
        x_ref, g_ref = refs[0], refs[1]
        o_ref = refs[-1]
        xv = x_ref[...]
        y = xv * lax.rsqrt(jnp.mean(xv * xv, axis=-1, keepdims=True) + EPS) * g_ref[...]
        if has_res:
            y = y + refs[2][...]
        o_ref[...] = y.astype(o_ref.dtype)

    row = pl.BlockSpec((tr, d), lambda i: (i, 0))
    ins = [x, g] + ([res] if has_res else []) + ([after] if has_after else [])
    return _pcall(
        body, grid=(s // tr,),
        in_specs=[row, pl.BlockSpec((1, d), lambda i: (0, 0))] + ([row] if has_res else [])
        + ([pl.BlockSpec(memory_space=pl.ANY)] if has_after else []),
        out_specs=row, out_shape=jax.ShapeDtypeStruct((s, d), out_dtype), name=name,
        compiler_params=_params(("parallel",)))(*ins)


def _norm_bwd(x, g, douts, dres, out_dtype, name, after=None):
    s, d = x.shape
    tr = _tile(s, (256, 128))
    nd = len(douts)
    has_res = dres is not None
    has_after = after is not None

    def body(*refs):
        x_ref, g_ref = refs[0], refs[1]
        dx_ref, dg_ref = refs[-2], refs[-1]
        dout = refs[2][...].astype(F32)
        for r in refs[3:2 + nd]:
            dout = dout + r[...].astype(F32)
        xv = x_ref[...]
        r = lax.rsqrt(jnp.mean(xv * xv, axis=-1, keepdims=True) + EPS)
        n = xv * r
        dn = dout * g_ref[...]
        dx = r * (dn - n * jnp.mean(dn * n, axis=-1, keepdims=True))
        if has_res:
            dx = dx + refs[2 + nd][...]
        dx_ref[...] = dx.astype(dx_ref.dtype)

        @pl.when(pl.program_id(0) == 0)
        def _():
            dg_ref[...] = jnp.zeros_like(dg_ref)

        dg_ref[...] += jnp.sum(dout * n, axis=0, keepdims=True)

    row = pl.BlockSpec((tr, d), lambda i: (i, 0))
    vec = pl.BlockSpec((1, d), lambda i: (0, 0))
    ins = [x, g] + list(douts) + ([dres] if has_res else []) + ([after] if has_after else [])
    return _pcall(
        body, grid=(s // tr,),
        in_specs=[row, vec] + [row] * (nd + int(has_res)) + ([pl.BlockSpec(memory_space=pl.ANY)] if has_after else []),
        out_specs=[row, vec],
        out_shape=[jax.ShapeDtypeStruct((s, d), out_dtype), jax.ShapeDtypeStruct((1, d), F32)], name=name,
        compiler_params=_params(("arbitrary",)))(*ins)


def _loss_head(y, target):
    s, d = y.shape
    tr = _tile(s, (256, 128))

    def body(y_ref, t_ref, sq_ref, dy_ref):
        e = y_ref[...] - t_ref[...]
        dy_ref[...] = e * (1.0 / d)

        @pl.when(pl.program_id(0) == 0)
        def _():
            sq_ref[...] = jnp.zeros_like(sq_ref)

        sq_ref[...] += jnp.sum(e * e, axis=0, keepdims=True)

    row = pl.BlockSpec((tr, d), lambda i: (i, 0))
    return _pcall(
        body, grid=(s // tr,), in_specs=[row, row], out_specs=[pl.BlockSpec((1, d), lambda i: (0, 0)), row],
        out_shape=[jax.ShapeDtypeStruct((1, d), F32), jax.ShapeDtypeStruct((s, d), F32)], name="loss_head",
        compiler_params=_params(("arbitrary",)))(y, target)


NEVER = -1e30
SB_QUERIES = 512


def _sb_sum_matrix(later):
    r = lax.broadcasted_iota(jnp.int32, (HEAD_SB, 2 * HEAD_SB), 0)
    c = lax.broadcasted_iota(jnp.int32, (HEAD_SB, 2 * HEAD_SB), 1)
    tri = jnp.where((r > c) if later else (r < c), 1.0, 0.0)
    return jnp.where(c < HEAD_SB, tri, 1.0).astype(BF16)


def _sb_mask(tq, q0, k0):
    row = lax.broadcasted_iota(jnp.int32, (tq, HEAD_SB), 0)
    col = lax.broadcasted_iota(jnp.int32, (tq, HEAD_SB), 1)
    return (k0 + col) < (q0 + row)


def _sb_fwd(proj, n_heads):
    s = proj.shape[0]
    tq = min(SB_QUERIES, s)
    per = tq // HEAD_SB
    scale = HEAD_SB ** -0.5

    def body(q_ref, k_ref, v_ref, o_ref, a_ref, acc_ref, c_ref):
        i = pl.program_id(1)
        q = q_ref[...].astype(BF16)
        sums = _sb_sum_matrix(True)
        acc_ref[...] = jnp.zeros_like(acc_ref)
        c_ref[...] = jnp.zeros_like(c_ref)
        last = (i + 1) * per - 1

        def scores(j, masked):
            off = pl.multiple_of(j * HEAD_SB, HEAD_SB)
            z = _dot_nt(q, k_ref[pl.ds(off, HEAD_SB), :].astype(BF16)) * scale
            sp = _softplus(z)
            logb = z - sp
            if masked:
                mask = _sb_mask(tq, i * tq, off)
                logb = jnp.where(mask, logb, NEVER)
                sp = jnp.where(mask, sp, 0.0)
            return logb, _split_dot(sp, sums)

        def values(j, logb, both):
            off = pl.multiple_of(j * HEAD_SB, HEAD_SB)
            c = c_ref[...]
            a = jnp.exp(logb - both[:, :HEAD_SB] - c).astype(BF16)
            a_ref[0, 0, j] = a
            acc_ref[...] += _dot(a, v_ref[pl.ds(off, HEAD_SB), :].astype(BF16))
            c_ref[...] = c + both[:, HEAD_SB:]

        def step(jj, carry, masked):
            j = last - jj
            nxt = scores(j, masked)
            values(jnp.minimum(j + 1, last), *carry)
            return nxt

        idle = (jnp.full((tq, HEAD_SB), NEVER, F32), jnp.zeros((tq, 2 * HEAD_SB), F32))
        carry = idle
        for jj in range(per):
            carry = step(jj, carry, True)
        def pair(jp, carry):
            return step(2 * jp + 1, step(2 * jp, carry, False), False)

        carry = lax.fori_loop(per // 2, (last + 1) // 2, pair, carry)
        values(0, *carry)
        o_ref[...] = acc_ref[...].astype(o_ref.dtype)

    h = n_heads
    blk = pl.BlockSpec((tq, HEAD_SB), lambda hh, i: (i, hh))
    return _pcall(
        body, grid=(h, s // tq),
        in_specs=[blk, pl.BlockSpec((s, HEAD_SB), lambda hh, i: (0, h + hh)),
                  pl.BlockSpec((s, HEAD_SB), lambda hh, i: (0, 2 * h + hh))],
        out_specs=[blk, pl.BlockSpec((1, 1, s // HEAD_SB, tq, HEAD_SB), lambda hh, i: (hh, i, 0, 0, 0))],
        out_shape=[jax.ShapeDtypeStruct((s, h * HEAD_SB), BF16),
                   jax.ShapeDtypeStruct((h, s // tq, s // HEAD_SB, tq, HEAD_SB), BF16)],
        scratch_shapes=[pltpu.VMEM((tq, HEAD_SB), F32), pltpu.VMEM((tq, HEAD_SB), F32)],
        name="sb_fwd", compiler_params=_params(("parallel", "arbitrary")))(proj, proj, proj)


def _sb_bwd(proj, a_saved, do, n_heads, after=None):
    s = proj.shape[0]
    tq = min(SB_QUERIES, s)
    per = tq // HEAD_SB
    scale = HEAD_SB ** -0.5
    follow = [] if after is None else [after]

    def body(q_ref, k_ref, v_ref, do_ref, a_ref, *rest):
        dq_ref, dk_ref, dv_ref, run_ref, acc_ref = rest[len(follow):]
        i = pl.program_id(1)

        @pl.when(i == 0)
        def _():
            dk_ref[...] = jnp.zeros_like(dk_ref)
            dv_ref[...] = jnp.zeros_like(dv_ref)

        q = q_ref[...].astype(BF16)
        dob = do_ref[...].astype(BF16)
        run_ref[...] = jnp.zeros_like(run_ref)
        acc_ref[...] = jnp.zeros_like(acc_ref)
        earlier = _sb_sum_matrix(False)
        first_diagonal = i * per

        def step(j, masked):
            off = pl.multiple_of(j * HEAD_SB, HEAD_SB)
            kb = k_ref[pl.ds(off, HEAD_SB), :].astype(BF16)
            vb = v_ref[pl.ds(off, HEAD_SB), :].astype(BF16)
            a = a_ref[0, 0, j]
            g = a.astype(F32) * _dot_nt(dob, vb)
            dv_ref[pl.ds(off, HEAD_SB), :] += _dot_tn(a, dob)
            z = _dot_nt(q, kb) * scale
            beta = 1.0 / (1.0 + jnp.exp(-z))
            both = _split_dot(g, earlier)
            p = run_ref[...]
            dz = (g * (1.0 - beta) - beta * (both[:, :HEAD_SB] + p)) * scale
            if masked:
                dz = jnp.where(_sb_mask(tq, i * tq, off), dz, 0.0)
            dzb = dz.astype(BF16)
            dk_ref[pl.ds(off, HEAD_SB), :] += _dot_tn(dzb, q)
            acc_ref[...] += _dot(dzb, kb)
            run_ref[...] = p + both[:, HEAD_SB:]

        def pair(jp, carry):
            step(2 * jp, False)
            step(2 * jp + 1, False)
            return carry

        lax.fori_loop(0, first_diagonal // 2, pair, 0)
        for u in range(per):
            step(first_diagonal + u, True)
        dq_ref[...] = acc_ref[...]

    h = n_heads
    blk = pl.BlockSpec((tq, HEAD_SB), lambda hh, i: (i, hh))
    col_blk = pl.BlockSpec((s, HEAD_SB), lambda hh, i: (0, hh))
    shape = jax.ShapeDtypeStruct((s, h * HEAD_SB), F32)
    return _pcall(
        body, grid=(h, s // tq),
        in_specs=[blk, pl.BlockSpec((s, HEAD_SB), lambda hh, i: (0, h + hh)),
                  pl.BlockSpec((s, HEAD_SB), lambda hh, i: (0, 2 * h + hh)), blk,
                  pl.BlockSpec((1, 1, s // HEAD_SB, tq, HEAD_SB), lambda hh, i: (hh, i, 0, 0, 0))]
        + [pl.BlockSpec(memory_space=pl.ANY)] * len(follow),
        out_specs=[blk, col_blk, col_blk], out_shape=[shape, shape, shape],
        scratch_shapes=[pltpu.VMEM((tq, HEAD_SB), F32), pltpu.VMEM((tq, HEAD_SB), F32)],
        name="sb_bwd", compiler_params=_params(("parallel", "arbitrary")))(proj, proj, proj, do, a_saved, *follow)


def _gm_mask():
    t = lax.broadcasted_iota(jnp.int32, (GROUP_GM, GROUP_GM), 0)
    s = lax.broadcasted_iota(jnp.int32, (GROUP_GM, GROUP_GM), 1)
    shift = CHUNK.bit_length() - 1
    return (s >> shift) <= (t >> shift)


def _gm_fwd(proj, g_vnorm, w_s, b_st, u_blk):
    s = proj.shape[0]
    groups = w_s.shape[0]
    w = groups * GROUP_GM

    def body(u_ref, v_ref, gv_ref, ws_ref, bst_ref, o_ref):
        ug = _gelu(u_ref[...])
        vg = _gelu(v_ref[...])
        vn = vg * lax.rsqrt(jnp.mean(vg * vg, axis=-1, keepdims=True) + EPS) * gv_ref[...]
        vnb = vn.astype(BF16)
        mask = _gm_mask()
        for g in range(groups):
            sl = slice(g * GROUP_GM, (g + 1) * GROUP_GM)
            wm = jnp.where(mask, ws_ref[g], 0.0).astype(BF16)
            mixed = _dot(wm, vnb[:, sl]) + bst_ref[:, g:g + 1]
            o_ref[:, sl] = (ug[:, sl] * mixed).astype(o_ref.dtype)

    return _pcall(
        body, grid=(s // GROUP_GM,),
        in_specs=[pl.BlockSpec((GROUP_GM, w), lambda c: (c, u_blk)), pl.BlockSpec((GROUP_GM, w), lambda c: (c, u_blk + 1)),
                  pl.BlockSpec((1, w), lambda c: (0, 0)), pl.BlockSpec((groups, GROUP_GM, GROUP_GM), lambda c: (0, 0, 0)),
                  pl.BlockSpec((GROUP_GM, groups), lambda c: (0, 0))],
        out_specs=pl.BlockSpec((GROUP_GM, w), lambda c: (c, 0)),
        out_shape=jax.ShapeDtypeStruct((s, w), BF16), name="gm_fwd",
        compiler_params=_params(("parallel",)))(proj, proj, g_vnorm, w_s, b_st)


def _gm_bwd(proj, g_vnorm, w_s, b_st, do, u_blk):
    s = proj.shape[0]
    groups = w_s.shape[0]
    w = groups * GROUP_GM

    def body(u_ref, v_ref, gv_ref, ws_ref, bst_ref, do_ref, du_ref, dv_ref, dgv_ref, dws_ref, dbst_ref, dvn_ref):
        @pl.when(pl.program_id(0) == 0)
        def _():
            dgv_ref[...] = jnp.zeros_like(dgv_ref)
            dws_ref[...] = jnp.zeros_like(dws_ref)
            dbst_ref[...] = jnp.zeros_like(dbst_ref)

        ug, ugrad = _gelu_and_grad(u_ref[...])
        vg, vgrad = _gelu_and_grad(v_ref[...])
        r = lax.rsqrt(jnp.mean(vg * vg, axis=-1, keepdims=True) + EPS)
        n = vg * r
        gv = gv_ref[...]
        vnb = (n * gv).astype(BF16)
        dout = do_ref[...]
        mask = _gm_mask()
        for g in range(groups):
            sl = slice(g * GROUP_GM, (g + 1) * GROUP_GM)
            wm = jnp.where(mask, ws_ref[g], 0.0).astype(BF16)
            mixed = _dot(wm, vnb[:, sl]) + bst_ref[:, g:g + 1]
            dmixed = dout[:, sl] * ug[:, sl]
            du_ref[:, sl] = dout[:, sl] * mixed * ugrad[:, sl]
            dbst_ref[:, g:g + 1] += jnp.sum(dmixed, axis=1, keepdims=True)
            dmb = dmixed.astype(BF16)
            dws_ref[g] += jnp.where(mask, _dot_nt(dmb, vnb[:, sl]), 0.0)
            dvn_ref[:, sl] = _dot_tn(wm, dmb)
        dvn = dvn_ref[...]
        dgv_ref[...] += jnp.sum(dvn * n, axis=0, keepdims=True)
        dn = dvn * gv
        dvg = r * (dn - n * jnp.mean(dn * n, axis=-1, keepdims=True))
        dv_ref[...] = dvg * vgrad

    rowb = pl.BlockSpec((GROUP_GM, w), lambda c: (c, 0))
    vec = pl.BlockSpec((1, w), lambda c: (0, 0))
    wsb = pl.BlockSpec((groups, GROUP_GM, GROUP_GM), lambda c: (0, 0, 0))
    bsb = pl.BlockSpec((GROUP_GM, groups), lambda c: (0, 0))
    return _pcall(
        body, grid=(s // GROUP_GM,),
        in_specs=[pl.BlockSpec((GROUP_GM, w), lambda c: (c, u_blk)), pl.BlockSpec((GROUP_GM, w), lambda c: (c, u_blk + 1)),
                  vec, wsb, bsb, rowb],
        out_specs=[rowb, rowb, vec, wsb, bsb],
        out_shape=[jax.ShapeDtypeStruct((s, w), F32), jax.ShapeDtypeStruct((s, w), F32), jax.ShapeDtypeStruct((1, w), F32),
                   jax.ShapeDtypeStruct((groups, GROUP_GM, GROUP_GM), F32), jax.ShapeDtypeStruct((GROUP_GM, groups), F32)],
        scratch_shapes=[pltpu.VMEM((GROUP_GM, w), F32)], name="gm_bwd",
        compiler_params=_params(("arbitrary",)))(proj, proj, g_vnorm, w_s, b_st, do)


def _xa_fwd(proj, mem_kv, q_blk, n_heads):
    s = proj.shape[0]
    nm = mem_kv.shape[0]
    tq = _tile(s, (512, 256, 128))
    scale = HEAD_XA ** -0.5

    def body(q_ref, k_ref, v_ref, o_ref):
        z = _dot_nt(q_ref[...].astype(BF16), k_ref[...].astype(BF16)) * scale
        z = z - jnp.max(z, axis=-1, keepdims=True)
        e = jnp.exp(z)
        p = e / jnp.sum(e, axis=-1, keepdims=True)
        o_ref[...] = _dot(p.astype(BF16), v_ref[...].astype(BF16)).astype(o_ref.dtype)

    h = n_heads
    return _pcall(
        body, grid=(h, s // tq),
        in_specs=[pl.BlockSpec((tq, HEAD_XA), lambda hh, i: (i, q_blk + hh)),
                  pl.BlockSpec((nm, HEAD_XA), lambda hh, i: (0, hh)), pl.BlockSpec((nm, HEAD_XA), lambda hh, i: (0, h + hh))],
        out_specs=pl.BlockSpec((tq, HEAD_XA), lambda hh, i: (i, hh)),
        out_shape=jax.ShapeDtypeStruct((s, h * HEAD_XA), BF16), name="xa_fwd",
        compiler_params=_params(("parallel", "parallel")))(proj, mem_kv, mem_kv)


def _xa_bwd(proj, mem_kv, do, q_blk, n_heads):
    s = proj.shape[0]
    nm = mem_kv.shape[0]
    tq = _tile(s, (512, 256, 128))
    scale = HEAD_XA ** -0.5
    h = n_heads

    def body(q_ref, k_ref, v_ref, do_ref, dq_ref, dk_ref, dv_ref):
        @pl.when(pl.program_id(1) == 0)
        def _():
            dk_ref[...] = jnp.zeros_like(dk_ref)
            dv_ref[...] = jnp.zeros_like(dv_ref)

        qb = q_ref[...].astype(BF16)
        kb = k_ref[...].astype(BF16)
        vb = v_ref[...].astype(BF16)
        dob = do_ref[...].astype(BF16)
        z = _dot_nt(qb, kb) * scale
        z = z - jnp.max(z, axis=-1, keepdims=True)
        e = jnp.exp(z)
        p = e / jnp.sum(e, axis=-1, keepdims=True)
        dp = _dot_nt(dob, vb)
        dz = (p * (dp - jnp.sum(dp * p, axis=-1, keepdims=True)) * scale).astype(BF16)
        dq_ref[...] = _dot(dz, kb)
        dk_ref[...] += _dot_tn(dz, qb)
        dv_ref[...] += _dot_tn(p.astype(BF16), dob)

    qspec = pl.BlockSpec((tq, HEAD_XA), lambda hh, i: (i, hh))
    dk, dv = None, None
    dq, dk, dv = _pcall(
        body, grid=(h, s // tq),
        in_specs=[pl.BlockSpec((tq, HEAD_XA), lambda hh, i: (i, q_blk + hh)),
                  pl.BlockSpec((nm, HEAD_XA), lambda hh, i: (0, hh)), pl.BlockSpec((nm, HEAD_XA), lambda hh, i: (0, h + hh)),
                  qspec],
        out_specs=[qspec, pl.BlockSpec((nm, HEAD_XA), lambda hh, i: (0, hh)), pl.BlockSpec((nm, HEAD_XA), lambda hh, i: (0, hh))],
        out_shape=[jax.ShapeDtypeStruct((s, h * HEAD_XA), F32), jax.ShapeDtypeStruct((nm, h * HEAD_XA), F32),
                   jax.ShapeDtypeStruct((nm, h * HEAD_XA), F32)],
        name="xa_bwd", compiler_params=_params(("parallel", "arbitrary")))(proj, mem_kv, mem_kv, do)
    return dq, dk, dv


def _merge_fwd(zg, b_gate, branches):
    s, d = branches[0].shape
    tr = _tile(s, (128,))

    def body(z0, z1, z2, g0, g1, g2, b0, b1, b2, o_ref):
        acc = None
        for z, g, b in ((z0, g0, b0), (z1, g1, b1), (z2, g2, b2)):
            term = jax.nn.sigmoid(z[...].astype(F32) + g[...]) * b[...]
            acc = term if acc is None else acc + term
        o_ref[...] = acc.astype(o_ref.dtype)

    zs = [pl.BlockSpec((tr, d), functools.partial(lambda i, k: (i, k), k=k)) for k in range(3)]
    gs = [pl.BlockSpec((1, d), functools.partial(lambda i, k: (0, k), k=k)) for k in range(3)]
    row = pl.BlockSpec((tr, d), lambda i: (i, 0))
    return _pcall(
        body, grid=(s // tr,), in_specs=zs + gs + [row] * 3, out_specs=row,
        out_shape=jax.ShapeDtypeStruct((s, d), BF16), name="merge_fwd",
        compiler_params=_params(("parallel",)))(zg, zg, zg, b_gate, b_gate, b_gate, *branches)


def _merge_bwd(zg, b_gate, branches, dmerged):
    s, d = branches[0].shape
    tr = _tile(s, (128,))

    def body(z0, z1, z2, g0, g1, g2, b0, b1, b2, dm_ref, dz_ref, d0, d1, d2, dbg_ref):
        @pl.when(pl.program_id(0) == 0)
        def _():
            dbg_ref[...] = jnp.zeros_like(dbg_ref)

        dm = dm_ref[...]
        for k, (z, g, b, dbr) in enumerate(((z0, g0, b0, d0), (z1, g1, b1, d1), (z2, g2, b2, d2))):
            sg = jax.nn.sigmoid(z[...].astype(F32) + g[...])
            dbr[...] = (dm * sg).astype(dbr.dtype)
            dz = dm * b[...] * sg * (1.0 - sg)
            dz_ref[:, k * d:(k + 1) * d] = dz.astype(dz_ref.dtype)
            dbg_ref[:, k * d:(k + 1) * d] += jnp.sum(dz, axis=0, keepdims=True)

    zs = [pl.BlockSpec((tr, d), functools.partial(lambda i, k: (i, k), k=k)) for k in range(3)]
    gs = [pl.BlockSpec((1, d), functools.partial(lambda i, k: (0, k), k=k)) for k in range(3)]
    row = pl.BlockSpec((tr, d), lambda i: (i, 0))
    outs = _pcall(
        body, grid=(s // tr,), in_specs=zs + gs + [row] * 4,
        out_specs=[pl.BlockSpec((tr, 3 * d), lambda i: (i, 0)), row, row, row, pl.BlockSpec((1, 3 * d), lambda i: (0, 0))],
        out_shape=[jax.ShapeDtypeStruct((s, 3 * d), BF16)] + [jax.ShapeDtypeStruct((s, d), BF16)] * 3
        + [jax.ShapeDtypeStruct((1, 3 * d), F32)],
        name="merge_bwd", compiler_params=_params(("arbitrary",)))(zg, zg, zg, b_gate, b_gate, b_gate, *branches, dmerged)
    return outs[0], list(outs[1:4]), outs[4]


def _shift_down(x, k, row):
    return jnp.where(row >= k, pltpu.roll(x, k, 0), 0.0)


def _shift_up(x, k, row, s):
    return jnp.where(row < s - k, pltpu.roll(x, s - k, 0), 0.0)


def _conv_pre(gate, cw_ref, cb_ref, row):
    conv = cb_ref[...] + cw_ref[CONV_TAPS - 1:CONV_TAPS, :] * gate
    for k in range(1, CONV_TAPS):
        conv = conv + cw_ref[CONV_TAPS - 1 - k:CONV_TAPS - k, :] * _shift_down(gate, k, row)
    return conv


def _cg_fwd(up, conv_w, conv_b):
    s = up.shape[0]
    f = conv_w.shape[1]
    tc = _tile(f, (256, 128))
    nb = f // tc

    def body(g_ref, v_ref, cw_ref, cb_ref, o_ref):
        row = lax.broadcasted_iota(jnp.int32, (s, tc), 0)
        conv = _conv_pre(g_ref[...].astype(F32), cw_ref, cb_ref, row)
        o_ref[...] = (_gelu(conv) * v_ref[...].astype(F32)).astype(o_ref.dtype)

    return _pcall(
        body, grid=(nb,),
        in_specs=[pl.BlockSpec((s, tc), lambda j: (0, j)), pl.BlockSpec((s, tc), lambda j: (0, nb + j)),
                  pl.BlockSpec((CONV_TAPS, tc), lambda j: (0, j)), pl.BlockSpec((1, tc), lambda j: (0, j))],
        out_specs=pl.BlockSpec((s, tc), lambda j: (0, j)),
        out_shape=jax.ShapeDtypeStruct((s, f), BF16), name="cg_fwd",
        compiler_params=_params(("parallel",)))(up, up, conv_w, conv_b)


def _cg_bwd(up, conv_w, conv_b, dact):
    s = up.shape[0]
    f = conv_w.shape[1]
    tc = _tile(f, (256, 128))
    nb = f // tc

    def body(g_ref, v_ref, cw_ref, cb_ref, da_ref, dg_ref, dv_ref, dcw_ref, dcb_ref):
        row = lax.broadcasted_iota(jnp.int32, (s, tc), 0)
        gate = g_ref[...].astype(F32)
        conv = _conv_pre(gate, cw_ref, cb_ref, row)
        gel, ggrad = _gelu_and_grad(conv)
        da = da_ref[...]
        dv_ref[...] = (da * gel).astype(dv_ref.dtype)
        dconv = da * v_ref[...].astype(F32) * ggrad
        dgate = cw_ref[CONV_TAPS - 1:CONV_TAPS, :] * dconv
        dcw_ref[CONV_TAPS - 1:CONV_TAPS, :] = jnp.sum(dconv * gate, axis=0, keepdims=True)
        for k in range(1, CONV_TAPS):
            dgate = dgate + cw_ref[CONV_TAPS - 1 - k:CONV_TAPS - k, :] * _shift_up(dconv, k, row, s)
            dcw_ref[CONV_TAPS - 1 - k:CONV_TAPS - k, :] = jnp.sum(dconv * _shift_down(gate, k, row), axis=0, keepdims=True)
        dg_ref[...] = dgate.astype(dg_ref.dtype)
        dcb_ref[...] = jnp.sum(dconv, axis=0, keepdims=True)

    colb = pl.BlockSpec((s, tc), lambda j: (0, j))
    return _pcall(
        body, grid=(nb,),
        in_specs=[colb, pl.BlockSpec((s, tc), lambda j: (0, nb + j)), pl.BlockSpec((CONV_TAPS, tc), lambda j: (0, j)),
                  pl.BlockSpec((1, tc), lambda j: (0, j)), colb],
        out_specs=[colb, colb, pl.BlockSpec((CONV_TAPS, tc), lambda j: (0, j)), pl.BlockSpec((1, tc), lambda j: (0, j))],
        out_shape=[jax.ShapeDtypeStruct((s, f), BF16), jax.ShapeDtypeStruct((s, f), BF16),
                   jax.ShapeDtypeStruct((CONV_TAPS, f), F32), jax.ShapeDtypeStruct((1, f), F32)],
        name="cg_bwd", compiler_params=_params(("parallel",)))(up, up, conv_w, conv_b, dact)


def _row_tile(rows, cols, elems=256 * 1024):
    want = max(16, elems // cols)
    for c in (512, 256, 128, 64, 32, 16):
        if c <= want and rows % c == 0:
            return c
    return rows


def _sum_halves(dwv, recv, core, name):
    nj, _, a, c = dwv.shape
    tr = _row_tile(a, c, 1024 * 1024)

    def body(core_ref, d_ref, r_ref, o_ref):
        o_ref[0] = (d_ref[0, 0].astype(F32) + r_ref[0].astype(F32)).astype(o_ref.dtype)

    grid_spec = pltpu.PrefetchScalarGridSpec(
        num_scalar_prefetch=1, grid=(nj, a // tr),
        in_specs=[pl.BlockSpec((1, 1, tr, c), lambda j, i, cr: (j, cr[0], i, 0)),
                  pl.BlockSpec((1, tr, c), lambda j, i, cr: (j, i, 0))],
        out_specs=pl.BlockSpec((1, tr, c), lambda j, i, cr: (j, i, 0)))
    return _pcall(body, grid_spec=grid_spec, out_shape=jax.ShapeDtypeStruct((nj, a, c), BF16), name=name,
                  compiler_params=_params(("parallel", "parallel")))(core, dwv, recv)


def _sum_chips(recv, own, chip, core, ax):
    _, a, b = recv.shape
    tr = _row_tile(a, b, 512 * 1024)

    def body(chip_ref, core_ref, r_ref, own_ref, o_ref):
        me = chip_ref[0]
        mine = own_ref[0].astype(F32)
        acc = None
        for k in range(N_CHIPS):
            term = jnp.where(me == k, mine, r_ref[k].astype(F32))
            acc = term if acc is None else acc + term
        o_ref[0] = acc

    own_spec = (pl.BlockSpec((1, tr, b), lambda i, ch, co: (0, i, ch[0])) if ax == 2
                else pl.BlockSpec((1, tr, b), lambda i, ch, co: (ch[0], i, 0)))
    grid_spec = pltpu.PrefetchScalarGridSpec(
        num_scalar_prefetch=2, grid=(a // tr,),
        in_specs=[pl.BlockSpec((N_CHIPS, tr, b), lambda i, ch, co: (0, i, 0)), own_spec],
        out_specs=pl.BlockSpec((1, tr, b), lambda i, ch, co: (co[0], i, 0)))
    return _pcall(body, grid_spec=grid_spec, out_shape=jax.ShapeDtypeStruct((2, a, b), F32),
                  name="sum_chips", compiler_params=_params(("parallel",)))(chip, core, recv, own)


def _place_own(wt, layer, chip, ax, after):
    nl, r, c = wt.shape
    half = r // 2
    tr = _row_tile(half, c, 512 * 1024)
    nb = half // tr

    def body(chip_ref, w_ref, after_ref, o_ref):
        o_ref[...] = w_ref[...].astype(BF16).reshape(o_ref.shape)

    if ax == 2:
        out_spec = pl.BlockSpec((1, tr, c), lambda h, i, ch: (h, i, ch[0]))
    else:
        out_spec = pl.BlockSpec((1, 1, tr, c), lambda h, i, ch: (ch[0], h, i, 0))
    grid_spec = pltpu.PrefetchScalarGridSpec(
        num_scalar_prefetch=1, grid=(2, nb),
        in_specs=[pl.BlockSpec((1, tr, c), lambda h, i, ch: (layer, h * nb + i, 0)), pl.BlockSpec(memory_space=pl.ANY)],
        out_specs=out_spec)
    return _pcall(body, grid_spec=grid_spec, out_shape=jax.ShapeDtypeStruct(_full_view_shape(wt.shape, ax), BF16),
                  name="place_own", compiler_params=_params(("parallel", "parallel")))(chip, wt, after)


def _adamw(w, m, v, g, layer, prev, name, after=None):
    nl, r, c = w.shape
    tr = _row_tile(r, c)
    c1 = 1.0 - ADAM_B1 ** ADAM_STEP
    c2 = 1.0 - ADAM_B2 ** ADAM_STEP

    follow = [] if after is None else [after]

    def body(w_ref, m_ref, v_ref, gin_ref, *rest):
        g_ref, d_ref, nm_ref, nv_ref = rest[-4:]
        g = gin_ref[...]
        mm = ADAM_B1 * m_ref[0] + (1.0 - ADAM_B1) * g
        vv = ADAM_B2 * v_ref[0] + (1.0 - ADAM_B2) * (g * g)
        g_ref[0] = g
        nm_ref[0] = mm
        nv_ref[0] = vv
        d_ref[0] = -ADAM_LR * ((mm / c1) / (jnp.sqrt(vv / c2) + ADAM_EPS) + ADAM_WD * w_ref[0])

    blk = pl.BlockSpec((1, tr, c), lambda i: (layer, i, 0))
    shape = jax.ShapeDtypeStruct((nl, r, c), F32)
    extra = [] if prev is None else list(prev)
    return _pcall(
        body, grid=(r // tr,),
        in_specs=[blk] * 3 + [pl.BlockSpec((tr, c), lambda i: (i, 0))] + [pl.BlockSpec(memory_space=pl.ANY)] * (len(extra) + len(follow)),
        out_specs=[blk] * 4, out_shape=[shape] * 4, input_output_aliases={4 + k: k for k in range(len(extra))}, name=name,
        compiler_params=_params(("parallel",)))(w, m, v, g, *extra, *follow)


HBM_SPEC = pl.BlockSpec(memory_space=pltpu.HBM)
COMM = pltpu.CompilerParams(has_side_effects=True)


def _position():
    x, y, c = lax.axis_index("x"), lax.axis_index("y"), lax.axis_index("c")
    chips = [(1 - x, y), (x, 1 - y), (1 - x, 1 - y)]
    return x, y, c, chips


def _remote(src, dst, send_sem, recv_sem, dev):
    return pltpu.make_async_remote_copy(src_ref=src, dst_ref=dst, send_sem=send_sem, recv_sem=recv_sem,
                                        device_id=dev, device_id_type=MESH)


def _full_view_shape(shard_shape, ax):
    _, r, c = shard_shape
    return (2, r // 2, c * N_CHIPS) if ax == 2 else (N_CHIPS, 2, r // 2, c)


def _piece(ref, ax, j, h, cs):
    if ax == 2:
        return ref.at[h, :, pl.ds(pl.multiple_of(j * cs, cs), cs)]
    return ref.at[j, h]


def _chip_block(ref, ax, j, cs):
    if ax == 2:
        return ref.at[:, :, pl.ds(pl.multiple_of(j * cs, cs), cs)]
    return ref.at[j]


SEM_SPEC = pl.BlockSpec(memory_space=pltpu.SEMAPHORE)
ANY_SPEC = pl.BlockSpec(memory_space=pl.ANY)
SPLIT = pltpu.CompilerParams(has_side_effects=pltpu.SideEffectType.DATAFLOW_SIDE_EFFECTING)


def _exchange(kind, name, bufs, build, n_sems, after=None, sems=None):
    n = len(bufs)
    if kind == 'sync':
        def body(*refs):
            mine, theirs = build(refs[n:2 * n], refs[2 * n], refs[2 * n + 1])
            for cp in mine:
                cp.start()
            for cp in theirs:
                cp.wait_recv()
            for cp in mine:
                cp.wait_send()

        return list(_pcall(
            body, in_specs=[HBM_SPEC] * n, out_specs=[HBM_SPEC] * n,
            out_shape=[jax.ShapeDtypeStruct(v.shape, v.dtype) for v in bufs], input_output_aliases={t: t for t in range(n)},
            scratch_shapes=[pltpu.SemaphoreType.DMA((n_sems,)), pltpu.SemaphoreType.DMA((n_sems,))],
            name=name, compiler_params=COMM)(*bufs))
    if kind == 'start':
        def body(*refs):
            mine, _ = build(refs[n + 3:2 * n + 3], refs[n + 1], refs[n + 2])
            for cp in mine:
                cp.start()
            refs[2 * n + 3][...] = jnp.zeros_like(refs[2 * n + 3])

        outs = _pcall(
            body, in_specs=[HBM_SPEC] * n + [ANY_SPEC],
            out_specs=[SEM_SPEC, SEM_SPEC] + [HBM_SPEC] * n + [pl.BlockSpec(memory_space=pltpu.VMEM)],
            out_shape=[pltpu.SemaphoreType.DMA((n_sems,)), pltpu.SemaphoreType.DMA((n_sems,))]
            + [pltpu.HBM(v.shape, v.dtype) for v in bufs] + [jax.ShapeDtypeStruct((8, LANES), F32)],
            input_output_aliases={t: 2 + t for t in range(n)}, name=name,
            compiler_params=SPLIT)(*[pltpu.with_memory_space_constraint(v, pltpu.HBM) for v in bufs], after)
        return (outs[0], outs[1]), list(outs[2:2 + n]), outs[2 + n]

    def body(*refs):
        mine, theirs = build(refs[:n], refs[n], refs[n + 1])
        for cp in mine:
            cp.wait_send()
        for cp in theirs:
            cp.wait_recv()

    return list(_pcall(
        body, in_specs=[HBM_SPEC] * n + [SEM_SPEC, SEM_SPEC, ANY_SPEC], out_specs=[HBM_SPEC] * n,
        out_shape=[pltpu.HBM(v.shape, v.dtype) for v in bufs], input_output_aliases={t: t for t in range(n)},
        name=name, compiler_params=SPLIT)(*bufs, sems[0], sems[1], after))


def _gather_ici_copies(axes, shard_cols):
    def build(bufs, send_sems, recv_sems):
        x, y, c, chips = _position()
        me = 2 * x + y
        mine, theirs = [], []
        for t, ax in enumerate(axes):
            own = _piece(bufs[t], ax, me, c, shard_cols[t])
            for p, (px, py) in enumerate(chips):
                k = t * 3 + p
                got = _piece(bufs[t], ax, 2 * px + py, c, shard_cols[t])
                mine.append(_remote(own, own, send_sems.at[k], recv_sems.at[k], (px, py, c)))
                theirs.append(_remote(got, got, send_sems.at[k], recv_sems.at[k], (px, py, c)))
        return mine, theirs
    return build


def _gather_d2d_copies(axes, shard_cols):
    def build(bufs, send_sems, recv_sems):
        x, y, c, chips = _position()
        mine, theirs = [], []
        for t, ax in enumerate(axes):
            for p, (px, py) in enumerate(chips):
                k = t * 3 + p
                had = _piece(bufs[t], ax, 2 * px + py, c, shard_cols[t])
                got = _piece(bufs[t], ax, 2 * px + py, 1 - c, shard_cols[t])
                mine.append(_remote(had, had, send_sems.at[k], recv_sems.at[k], (x, y, 1 - c)))
                theirs.append(_remote(got, got, send_sems.at[k], recv_sems.at[k], (x, y, 1 - c)))
        return mine, theirs
    return build


def _grads_d2d_copies(n):
    def build(bufs, send_sems, recv_sems):
        x, y, c, _ = _position()
        mine = [_remote(bufs[t].at[:, 1 - c], bufs[n + t], send_sems.at[t], recv_sems.at[t], (x, y, 1 - c)) for t in range(n)]
        return mine, mine
    return build


def _grads_ici_copies(axes):
    n = len(axes)

    def build(bufs, send_sems, recv_sems):
        x, y, c, chips = _position()
        me = 2 * x + y

        def block(t, j):
            if axes[t] == 2:
                cs = bufs[n + t].shape[2]
                return bufs[t].at[0, :, pl.ds(pl.multiple_of(j * cs, cs), cs)]
            return bufs[t].at[j]

        mine, theirs = [], []
        for t in range(n):
            for p, (px, py) in enumerate(chips):
                k = t * 3 + p
                peer = 2 * px + py
                mine.append(_remote(block(t, peer), bufs[n + t].at[me], send_sems.at[k], recv_sems.at[k], (px, py, c)))
                theirs.append(_remote(block(t, peer), bufs[n + t].at[peer], send_sems.at[k], recv_sems.at[k], (px, py, c)))
        return mine, theirs
    return build


def _join_copies(bufs, send_sems, recv_sems):
    x, y, c, _ = _position()
    mine = [_remote(b.at[c], b.at[c], send_sems.at[t], recv_sems.at[t], (x, y, 1 - c)) for t, b in enumerate(bufs)]
    theirs = [_remote(b.at[1 - c], b.at[1 - c], send_sems.at[t], recv_sems.at[t], (x, y, 1 - c)) for t, b in enumerate(bufs)]
    return mine, theirs


def _grads_recv_shape(sm, ax):
    _, a, c = sm.shape
    return (N_CHIPS, a, c // N_CHIPS if ax == 2 else c)


def _gather_small(shard):
    nl, r, cs = shard.shape

    def body(in_ref, out_ref, send_sems, recv_sems, local_sem):
        x, y, c, chips = _position()

        def cols(j):
            return out_ref.at[:, :, pl.ds(pl.multiple_of(j * cs, cs), cs)]

        me = 2 * x + y
        loc = pltpu.make_async_copy(in_ref, cols(me), local_sem)
        loc.start()
        remote = [_remote(in_ref, cols(me), send_sems.at[p], recv_sems.at[p], (px, py, c)) for p, (px, py) in enumerate(chips)]
        for cp in remote:
            cp.start()
        for p, (px, py) in enumerate(chips):
            _remote(in_ref, cols(2 * px + py), send_sems.at[p], recv_sems.at[p], (px, py, c)).wait_recv()
        for cp in remote:
            cp.wait_send()
        loc.wait()

    return _pcall(
        body, in_specs=[HBM_SPEC], out_specs=HBM_SPEC, out_shape=jax.ShapeDtypeStruct((nl, r, cs * N_CHIPS), shard.dtype),
        scratch_shapes=[pltpu.SemaphoreType.DMA((3,)), pltpu.SemaphoreType.DMA((3,)), pltpu.SemaphoreType.DMA(())],
        name="gather_small", compiler_params=COMM)(shard)


def _small_copies(bufs, send_sems, recv_sems):
    pack, land = bufs
    x, y, cc, _ = _position()
    me = 4 * x + 2 * y + cc
    mine, theirs = [], []
    for k in range(1, N_DEV):
        px, py, pc = x ^ ((k >> 2) & 1), y ^ ((k >> 1) & 1), cc ^ (k & 1)
        mine.append(_remote(pack, land.at[me], send_sems.at[k - 1], recv_sems.at[k - 1], (px, py, pc)))
        theirs.append(_remote(pack, land.at[4 * px + 2 * py + pc], send_sems.at[k - 1], recv_sems.at[k - 1], (px, py, pc)))
    return mine, theirs


def _sum_devices(land, pack, dev):
    _, r, c = land.shape
    tr = _tile(r, (672, 512, 256, 128, 64, 8))

    def body(dev_ref, l_ref, p_ref, o_ref):
        me = dev_ref[0]
        acc = None
        for k in range(N_DEV):
            term = jnp.where(me == k, p_ref[...], l_ref[k])
            acc = term if acc is None else acc + term
        o_ref[...] = acc

    grid_spec = pltpu.PrefetchScalarGridSpec(
        num_scalar_prefetch=1, grid=(r // tr,),
        in_specs=[pl.BlockSpec((N_DEV, tr, c), lambda i, dv: (0, i, 0)), pl.BlockSpec((tr, c), lambda i, dv: (i, 0))],
        out_specs=pl.BlockSpec((tr, c), lambda i, dv: (i, 0)))
    return _pcall(body, grid_spec=grid_spec, out_shape=jax.ShapeDtypeStruct((r, c), F32), name="sum_devices",
                  compiler_params=_params(("parallel",)))(dev, land, pack)


def _dims(d):
    half = d // 2
    return half // HEAD_SB, half // HEAD_XA, 3, (5 * half) // HEAD_XA


def _layer_fwd(x, mem, weight, small, l, after=None):
    h_sb, h_xa, u_blk, q_blk = _dims(x.shape[1])

    def vec(name):
        return small[name][l].reshape(1, -1)

    def use(a, name, follows, mm_name, dtype=F32):
        wt = weight(name, follows)
        return _mm(a, wt, 'nn', dtype, mm_name, after=weight.token())

    h1 = _norm_fwd(x, vec('g_mix_pre'), None, BF16, "norm_mix_pre", after)
    proj = use(h1, 'w_in', h1, "mm_proj")
    o_sb, a_sb = _sb_fwd(proj, h_sb)
    b_st = small['b_s'][l].T
    o_gm = _gm_fwd(proj, vec('g_vnorm'), small['w_s'][l], b_st, u_blk)
    memn = _norm_fwd(mem, vec('g_mem'), None, BF16, "norm_mem")
    mem_kv = use(memn, 'w_mem_kv', o_sb, "mm_mem_kv")
    o_xa = _xa_fwd(proj, mem_kv, q_blk, h_xa)
    zg = use(h1, 'w_gate', o_sb, "mm_gate", BF16)
    branches = [use(o, wn, zg, "mm_branch") for o, wn in ((o_sb, 'w_br_sb'), (o_gm, 'w_br_gm'), (o_xa, 'w_br_xa'))]
    merged = _merge_fwd(zg, vec('b_gate'), branches)
    y1 = use(merged, 'w_out', zg, "mm_out")
    x1 = _norm_fwd(y1, vec('g_mix_post'), x, F32, "norm_mix_post")
    h2 = _norm_fwd(x1, vec('g_ffn_pre'), None, BF16, "norm_ffn_pre")
    up = use(h2, 'w_up', h2, "mm_up", BF16)
    act = _cg_fwd(up, weight('conv_w', up), vec('conv_b'))
    y2 = use(act, 'w_down', act, "mm_down")
    x2 = _norm_fwd(y2, vec('g_ffn_post'), x1, F32, "norm_ffn_post")
    saved = dict(x0=x, h1=h1, proj=proj, o_sb=o_sb, a_sb=a_sb, o_gm=o_gm, o_xa=o_xa, memn=memn, mem_kv=mem_kv, zg=zg,
                 branches=branches, merged=merged, y1=y1, x1=x1, h2=h2, up=up, act=act, y2=y2, b_st=b_st)
    return x2, saved


def _layer_bwd(dx, mem, sv, full, small, l, after, emit):
    h_sb, h_xa, u_blk, q_blk = _dims(dx.shape[1])

    def vec(name):
        return small[name][l].reshape(1, -1)

    gb, gs = {}, {}
    dy2, gs['g_ffn_post'] = _norm_bwd(sv['y2'], vec('g_ffn_post'), [dx], None, BF16, "norm_ffn_post_bwd", after)
    gb['w_down'] = _mm(sv['act'], dy2, 'tn', BF16, "mm_down_dw")
    dact = _mm(dy2, full['w_down'], 'nt', F32, "mm_down_dx")
    dgate, dval, gs['conv_w'], gs['conv_b'] = _cg_bwd(sv['up'], full['conv_w'], vec('conv_b'), dact)
    dup = jnp.concatenate([dgate, dval], axis=1)
    gb['w_up'] = _mm(sv['h2'], dup, 'tn', BF16, "mm_up_dw")
    token = emit(0, gb)
    dh2 = _mm(dup, full['w_up'], 'nt', F32, "mm_up_dx", after=token)
    token = emit.flush(dh2)
    dx1, gs['g_ffn_pre'] = _norm_bwd(sv['x1'], vec('g_ffn_pre'), [dh2], dx, F32, "norm_ffn_pre_bwd", token)
    dy1, gs['g_mix_post'] = _norm_bwd(sv['y1'], vec('g_mix_post'), [dx1], None, BF16, "norm_mix_post_bwd")
    gb['w_out'] = _mm(sv['merged'], dy1, 'tn', BF16, "mm_out_dw")
    dmerged = _mm(dy1, full['w_out'], 'nt', F32, "mm_out_dx")
    dzg, dbr, gs['b_gate'] = _merge_bwd(sv['zg'], vec('b_gate'), sv['branches'], dmerged)
    douts = []
    for o, db, wn in ((sv['o_sb'], dbr[0], 'w_br_sb'), (sv['o_gm'], dbr[1], 'w_br_gm'), (sv['o_xa'], dbr[2], 'w_br_xa')):
        gb[wn] = _mm(o, db, 'tn', BF16, "mm_branch_dw")
        douts.append(_mm(db, full[wn], 'nt', F32, "mm_branch_dx"))
    gb['w_gate'] = _mm(sv['h1'], dzg, 'tn', BF16, "mm_gate_dw")
    dq_xa, dk_xa, dv_xa = _xa_bwd(sv['proj'], sv['mem_kv'], douts[2], q_blk, h_xa)
    dmem_kv = jnp.concatenate([dk_xa, dv_xa], axis=1).astype(BF16)
    gb['w_mem_kv'] = _mm(sv['memn'], dmem_kv, 'tn', BF16, "mm_mem_kv_dw")
    token = emit(1, gb)
    dh1_gate = _mm(dzg, full['w_gate'], 'nt', F32, "mm_gate_dx", after=token)
    token = emit.flush(dh1_gate)
    dmemn = _mm(dmem_kv, full['w_mem_kv'], 'nt', F32, "mm_mem_kv_dx")
    _, gs['g_mem'] = _norm_bwd(mem, vec('g_mem'), [dmemn], None, BF16, "norm_mem_bwd")
    du, dv, gs['g_vnorm'], gs['w_s'], db_st = _gm_bwd(sv['proj'], vec('g_vnorm'), small['w_s'][l], sv['b_st'], douts[1], u_blk)
    gs['b_s'] = db_st.T
    dq, dk, dvv = _sb_bwd(sv['proj'], sv['a_sb'], douts[0], h_sb, token)
    dproj = jnp.concatenate([dq, dk, dvv, du, dv, dq_xa], axis=1).astype(BF16)
    gb['w_in'] = _mm(sv['h1'], dproj, 'tn', BF16, "mm_proj_dw")
    token = emit(2, gb)
    dh1_proj = _mm(dproj, full['w_in'], 'nt', F32, "mm_proj_dx")
    dx0, gs['g_mix_pre'] = _norm_bwd(sv['x0'], vec('g_mix_pre'), [dh1_gate, dh1_proj], dx1, F32, "norm_mix_pre_bwd", token)
    return dx0, gs


class _Given:
    def __init__(self, full):
        self.full = full

    def __call__(self, name, follows):
        return self.full[name]

    def token(self):
        return None


def _local_step(x, mem, target, full, small):
    n_layers = len(full['w_in'])
    saved = []
    for l in range(n_layers):
        x, sv = _layer_fwd(x, mem, _Given({n: full[n][l] for n in full}), small, l)
        saved.append(sv)
    sq, dx = _loss_head(x, target)
    gbig = {n: [None] * n_layers for n in BIG}
    gsmall = {n: [None] * n_layers for n in SMALL + ['conv_w']}
    class Collect:
        def __init__(self, l):
            self.l = l

        def __call__(self, g, gb):
            for n in BWD_GROUPS[g]:
                gbig[n][self.l] = gb[n]

        def flush(self, follows):
            return None

    for l in reversed(range(n_layers)):
        dx, gs = _layer_bwd(dx, mem, saved[l], {n: full[n][l] for n in full}, small, l, None, Collect(l))
        for n in gs:
            gsmall[n][l] = gs[n]
    return sq, dx, gbig, gsmall


def _pack(arrays, rows_multiple):
    flat = jnp.concatenate([a.reshape(-1).astype(F32) for a in arrays])
    rows = -(-flat.shape[0] // LANES)
    rows = -(-rows // rows_multiple) * rows_multiple
    return jnp.pad(flat, (0, rows * LANES - flat.shape[0])).reshape(rows, LANES)


def _unpack(pack, like):
    flat = pack.reshape(-1)
    out, off = [], 0
    for a in like:
        out.append(flat[off:off + a.size].reshape(a.shape))
        off += a.size
    return out


def _grad_view(g, ax):
    r, c = g.shape
    return g.reshape(1, 2, r // 2, c) if ax == 2 else g.reshape(N_CHIPS, 2, r // (2 * N_CHIPS), c)


def kernel(x, mem, g_mix_pre, w_in, g_vnorm, w_s, b_s, g_mem, w_mem_kv, w_gate, b_gate, w_br_sb, w_br_gm, w_br_xa, w_out, g_mix_post, g_ffn_pre, w_up, conv_w, conv_b, w_down, g_ffn_post, loss_target, m_g_mix_pre, m_w_in, m_g_vnorm, m_w_s, m_b_s, m_g_mem, m_w_mem_kv, m_w_gate, m_b_gate, m_w_br_sb, m_w_br_gm, m_w_br_xa, m_w_out, m_g_mix_post, m_g_ffn_pre, m_w_up, m_conv_w, m_conv_b, m_w_down, m_g_ffn_post, v_g_mix_pre, v_w_in, v_g_vnorm, v_w_s, v_b_s, v_g_mem, v_w_mem_kv, v_w_gate, v_b_gate, v_w_br_sb, v_w_br_gm, v_w_br_xa, v_w_out, v_g_mix_post, v_g_ffn_pre, v_w_up, v_conv_w, v_conv_b, v_w_down, v_g_ffn_post):
    w = dict(g_mix_pre=g_mix_pre, w_in=w_in, g_vnorm=g_vnorm, w_s=w_s, b_s=b_s, g_mem=g_mem, w_mem_kv=w_mem_kv,
             w_gate=w_gate, b_gate=b_gate, w_br_sb=w_br_sb, w_br_gm=w_br_gm, w_br_xa=w_br_xa, w_out=w_out,
             g_mix_post=g_mix_post, g_ffn_pre=g_ffn_pre, w_up=w_up, conv_w=conv_w, conv_b=conv_b, w_down=w_down,
             g_ffn_post=g_ffn_post)
    m = dict(g_mix_pre=m_g_mix_pre, w_in=m_w_in, g_vnorm=m_g_vnorm, w_s=m_w_s, b_s=m_b_s, g_mem=m_g_mem,
             w_mem_kv=m_w_mem_kv, w_gate=m_w_gate, b_gate=m_b_gate, w_br_sb=m_w_br_sb, w_br_gm=m_w_br_gm,
             w_br_xa=m_w_br_xa, w_out=m_w_out, g_mix_post=m_g_mix_post, g_ffn_pre=m_g_ffn_pre, w_up=m_w_up,
             conv_w=m_conv_w, conv_b=m_conv_b, w_down=m_w_down, g_ffn_post=m_g_ffn_post)
    v = dict(g_mix_pre=v_g_mix_pre, w_in=v_w_in, g_vnorm=v_g_vnorm, w_s=v_w_s, b_s=v_b_s, g_mem=v_g_mem,
             w_mem_kv=v_w_mem_kv, w_gate=v_w_gate, b_gate=v_b_gate, w_br_sb=v_w_br_sb, w_br_gm=v_w_br_gm,
             w_br_xa=v_w_br_xa, w_out=v_w_out, g_mix_post=v_g_mix_post, g_ffn_pre=v_g_ffn_pre, w_up=v_w_up,
             conv_w=v_conv_w, conv_b=v_conv_b, w_down=v_w_down, g_ffn_post=v_g_ffn_post)
    n_layers = w_in.shape[0]
    d = x.shape[-1]
    core = lax.axis_index("c").astype(jnp.int32).reshape(1)
    chip = (2 * lax.axis_index("x") + lax.axis_index("y")).astype(jnp.int32).reshape(1)
    small = {n: w[n] for n in SMALL}
    xs, mems, target = x[0], mem[0], loss_target[0]

    conv_w_full = _gather_small(conv_w)

    def as_full(vw, ax):
        return vw.reshape(-1, vw.shape[-1]) if ax == 1 else vw.reshape(vw.shape[0] * vw.shape[1], vw.shape[2])

    stages, token = {}, conv_w_full
    keys = [(l, g) for l in range(n_layers) for g in range(len(FWD_GROUPS))]
    for l, g in keys:
        names = FWD_GROUPS[g]
        ax_g = [BIG_AXIS[n] for n in names]
        cols_g = [w[n].shape[2] for n in names]
        views = [_place_own(w[n], l, chip, ax, token) for n, ax in zip(names, ax_g)]
        sems, views, token = _exchange('start', f"gather_ici_start_{l}_{g}", views, _gather_ici_copies(ax_g, cols_g),
                                       3 * len(names), after=token)
        stages[l, g] = dict(names=names, ax=ax_g, cols=cols_g, views=views, ici=sems, d2d=None, full=None)

    def cross_cores(key, follows):
        st, (l, g) = stages[key], key
        n3 = 3 * len(st['names'])
        views = _exchange('wait', f"gather_ici_wait_{l}_{g}", st['views'], _gather_ici_copies(st['ax'], st['cols']), n3,
                          after=follows, sems=st['ici'])
        st['d2d'], st['views'], tok = _exchange('start', f"gather_d2d_start_{l}_{g}", views,
                                                _gather_d2d_copies(st['ax'], st['cols']), n3, after=core)
        return tok

    class Weights:
        def __init__(self, l):
            self.l, self.tok = l, None

        def __call__(self, name, follows):
            if name == 'conv_w':
                return conv_w_full[self.l]
            key = (self.l, [g for g, names in enumerate(FWD_GROUPS) if name in names][0])
            st = stages[key]
            if st['full'] is None:
                if st['d2d'] is None:
                    cross_cores(key, follows)
                views = _exchange('wait', f"gather_d2d_wait_{key[0]}_{key[1]}", st['views'],
                                  _gather_d2d_copies(st['ax'], st['cols']), 3 * len(st['names']), after=follows, sems=st['d2d'])
                st['full'] = {n: as_full(vw, ax) for n, vw, ax in zip(st['names'], views, st['ax'])}
            nxt = keys.index(key) + 1
            if name == st['names'][-1] and nxt < len(keys) and key != keys[0] and stages[keys[nxt]]['d2d'] is None:
                self.tok = cross_cores(keys[nxt], follows)
            return st['full'][name]

        def token(self):
            return self.tok

    fulls, saved = [], []
    for l in range(n_layers):
        xs, sv = _layer_fwd(xs, mems, Weights(l), small, l, token if l == 0 else None)
        fulls.append({n: stages[l, g]['full'][n] for g, names in enumerate(FWD_GROUPS) for n in names} | {'conv_w': conv_w_full[l]})
        saved.append(sv)
    sq, dx = _loss_head(xs, target)
    loss = lax.psum(0.5 * jnp.sum(sq) / d, ("x", "y", "c"))

    sent = []

    def to_chips(l, g, names, ax_g, bufs, after):
        n = len(names)
        sums = [_sum_halves(dv, th, core, "sum_halves") for dv, th in zip(bufs[:n], bufs[n:])]
        lands = [lax.empty(_grads_recv_shape(sm, ax), sm.dtype) for sm, ax in zip(sums, ax_g)]
        sems, bufs, tok = _exchange('start', f"grads_ici_start_{l}_{g}", sums + lands, _grads_ici_copies(ax_g), 3 * n,
                                    after=core if after is None else after)
        sent.append((l, g, names, ax_g, bufs, sems))
        return tok

    class Grads:
        def __init__(self, l):
            self.l, self.crossing = l, None

        def __call__(self, g, gb):
            names = BWD_GROUPS[g]
            ax_g = [BIG_AXIS[n] for n in names]
            n = len(names)
            dwvs = [_grad_view(gb[nm], ax) for nm, ax in zip(names, ax_g)]
            lands = [lax.empty((dv.shape[0],) + dv.shape[2:], dv.dtype) for dv in dwvs]
            if g + 1 < len(BWD_GROUPS):
                sems, bufs, tok = _exchange('start', f"grads_d2d_start_{self.l}_{g}", dwvs + lands, _grads_d2d_copies(n), n,
                                            after=core)
                self.crossing = (g, names, ax_g, bufs, sems)
                return tok
            return to_chips(self.l, g, names, ax_g, _exchange('sync', "grads_d2d", dwvs + lands, _grads_d2d_copies(n), n), None)

        def flush(self, follows):
            if self.crossing is None:
                return None
            (g, names, ax_g, bufs, sems), self.crossing = self.crossing, None
            bufs = _exchange('wait', f"grads_d2d_wait_{self.l}_{g}", bufs, _grads_d2d_copies(len(names)), len(names),
                             after=follows, sems=sems)
            return to_chips(self.l, g, names, ax_g, bufs, None)

    gsmall = {n: [None] * n_layers for n in SMALL + ['conv_w']}
    for l in reversed(range(n_layers)):
        dx, gs = _layer_bwd(dx, mems, saved[l], fulls[l], small, l, None, Grads(l))
        for n in gs:
            gsmall[n][l] = gs[n]

    names_small = SMALL + ['conv_w']
    small_full = [jnp.stack(gsmall[n]).reshape(w[n].shape) for n in SMALL]
    conv_w_grad = jnp.stack(gsmall['conv_w'])
    pack = _pack(small_full + [conv_w_grad], 8)
    small_sems, small_bufs, _ = _exchange('start', "small_start", [pack, lax.empty((N_DEV,) + pack.shape, F32)], _small_copies,
                                          N_DEV - 1, after=dx)

    def small_update(follows):
        pk, land = _exchange('wait', "small_wait", small_bufs, _small_copies, N_DEV - 1, after=follows, sems=small_sems)
        device = (4 * lax.axis_index("x") + 2 * lax.axis_index("y") + lax.axis_index("c")).astype(jnp.int32).reshape(1)
        summed = _sum_devices(land, pk, device)
        *small_g, conv_w_g = _unpack(summed, small_full + [conv_w_grad])
        shard = conv_w.shape[-1]
        conv_w_g = lax.dynamic_slice_in_dim(conv_w_g, chip[0] * shard, shard, axis=2)
        packed = [_pack([p[n] for n in names_small], 256) for p in (w, m, v)]
        gpack = _pack(small_g + [conv_w_g], 256)
        res = _adamw(packed[0][None], packed[1][None], packed[2][None], gpack, 0, None, "adamw_small")
        like = [w[n] for n in names_small]
        unpacked = [_unpack(r[0], like) for r in res]
        return {n: tuple(u[i] for u in unpacked) for i, n in enumerate(names_small)}

    out = {}

    def update(joining, follows):
        l, g, names, halves, sems = joining
        halves = _exchange('wait', f"join_wait_{l}_{g}", halves, _join_copies, len(names), after=follows, sems=sems)
        last = None
        for n, hv in zip(names, halves):
            out[n] = _adamw(w[n], m[n], v[n], hv.reshape(w[n].shape[1:]), l, out.get(n), "adamw_big", last)
            last = out[n][0]
        return last

    joining, follows = None, dx
    for k, (l, g, names, ax_g, bufs, sems) in enumerate(sent):
        n = len(names)
        bufs = _exchange('wait', f"grads_ici_wait_{l}_{g}", bufs, _grads_ici_copies(ax_g), 3 * n, after=follows, sems=sems)
        halves = [_sum_chips(r, sm, chip, core, ax) for sm, r, ax in zip(bufs[:n], bufs[n:], ax_g)]
        jsems, halves, tok = _exchange('start', f"join_start_{l}_{g}", halves, _join_copies, n, after=core)
        if joining is not None:
            follows = update(joining, tok)
        joining = (l, g, names, halves, jsems)
    out.update(small_update(update(joining, follows)))

    return (loss, dx[None], *[out[n][0] for n in WEIGHTS], *[out[n][1] for n in WEIGHTS],
            *[out[n][2] for n in WEIGHTS], *[out[n][3] for n in WEIGHTS])
```

```python
import functools
import math

import jax
import jax.numpy as jnp
from jax import lax
from jax.experimental import pallas as pl
from jax.experimental.pallas import tpu as pltpu

F32 = jnp.float32
BF16 = jnp.bfloat16
EPS = 1e-6
HEAD_SB = 128
GROUP_GM = 128
CHUNK = 64
HEAD_XA = 256
CONV_TAPS = 3
N_CHIPS = 4
N_DEV = 8
LANES = 128
MIB = 1024 * 1024
VMEM_LIMIT = 48 * MIB
SPLITS = 1

ADAM_LR = 0.001
ADAM_B1 = 0.9
ADAM_B2 = 0.999
ADAM_EPS = 1e-08
ADAM_WD = 0.01
ADAM_STEP = 10

WEIGHTS = ['g_mix_pre', 'w_in', 'g_vnorm', 'w_s', 'b_s', 'g_mem', 'w_mem_kv', 'w_gate', 'b_gate', 'w_br_sb',
           'w_br_gm', 'w_br_xa', 'w_out', 'g_mix_post', 'g_ffn_pre', 'w_up', 'conv_w', 'conv_b', 'w_down',
           'g_ffn_post']
BIG_AXIS = {'w_in': 2, 'w_mem_kv': 1, 'w_gate': 2, 'w_br_sb': 2, 'w_br_gm': 2, 'w_br_xa': 2, 'w_out': 1,
            'w_up': 2, 'w_down': 1}
BIG = list(BIG_AXIS)
FWD_GROUPS = [['w_in'], ['w_mem_kv', 'w_gate'], ['w_br_sb', 'w_br_gm', 'w_br_xa', 'w_out'], ['w_up'], ['w_down']]
BWD_GROUPS = [['w_down', 'w_up'], ['w_out', 'w_br_sb', 'w_br_gm', 'w_br_xa', 'w_gate', 'w_mem_kv'], ['w_in']]
SMALL = ['g_mix_pre', 'g_vnorm', 'w_s', 'b_s', 'g_mem', 'b_gate', 'g_mix_post', 'g_ffn_pre', 'conv_b', 'g_ffn_post']
MESH = pl.DeviceIdType.MESH


def _pcall(body, **kw):
    return pl.pallas_call(body, **kw)


def _params(sem=None, vmem=VMEM_LIMIT):
    return pltpu.CompilerParams(dimension_semantics=sem, vmem_limit_bytes=vmem)


def _tile(n, cands):
    for c in cands:
        if n % c == 0:
            return c
    return n


_GELU_C = math.sqrt(2.0 / math.pi)
_GELU_A = 0.044715


def _gelu(x):
    return 0.5 * x * (1.0 + jnp.tanh(_GELU_C * (x + _GELU_A * (x * x * x))))


def _gelu_and_grad(x):
    x2 = x * x
    t = jnp.tanh(_GELU_C * (x + _GELU_A * (x2 * x)))
    val = 0.5 * x * (1.0 + t)
    grad = 0.5 * (1.0 + t) + 0.5 * x * (1.0 - t * t) * (_GELU_C * (1.0 + 3.0 * _GELU_A * x2))
    return val, grad


def _softplus(z):
    return jnp.maximum(z, 0.0) + jnp.log(1.0 + jnp.exp(-jnp.abs(z)))


def _dot(a, b):
    return jnp.dot(a, b, preferred_element_type=F32)


def _dot_nt(a, b):
    return lax.dot_general(a, b, (((1,), (1,)), ((), ())), preferred_element_type=F32)


def _dot_tn(a, b):
    return lax.dot_general(a, b, (((0,), (0,)), ((), ())), preferred_element_type=F32)


def _split_dot(a, m):
    out = None
    rest = a
    for _ in range(SPLITS):
        piece = rest.astype(BF16)
        rest = rest - piece.astype(F32)
        term = _dot(piece, m)
        out = term if out is None else out + term
    return out


def _mm(a, b, mode, out_dtype, name, tm=None, tn=None, tk=None, after=None):
    if mode == 'nn':
        (m, kc), (kc2, n) = a.shape, b.shape
    elif mode == 'nt':
        (m, kc), (n, kc2) = a.shape, b.shape
    else:
        (kc, m), (kc2, n) = a.shape, b.shape
    assert kc == kc2, (a.shape, b.shape, mode)
    tm = tm or _tile(m, (1024, 512, 256, 128))
    tn = tn or _tile(n, (1024, 512, 256, 128))
    tk = tk or (kc if kc <= 3072 else _tile(kc, (3072, 2816, 2048, 1536, 1408, 1024, 512)))
    nk = kc // tk
    dot = {'nn': _dot, 'nt': _dot_nt, 'tn': _dot_tn}[mode]
    a_spec = pl.BlockSpec((tk, tm), lambda i, j, k: (k, i)) if mode == 'tn' else pl.BlockSpec((tm, tk), lambda i, j, k: (i, k))
    b_spec = pl.BlockSpec((tn, tk), lambda i, j, k: (j, k)) if mode == 'nt' else pl.BlockSpec((tk, tn), lambda i, j, k: (k, j))

    extra = [] if after is None else [after]
    extra_specs = [pl.BlockSpec(memory_space=pl.ANY)] * len(extra)

    if nk == 1:
        def body(a_ref, b_ref, *rest):
            o_ref = rest[-1]
            o_ref[...] = dot(a_ref[...].astype(BF16), b_ref[...].astype(BF16)).astype(o_ref.dtype)
        scratch = []
    else:
        def body(a_ref, b_ref, *rest):
            o_ref, acc_ref = rest[-2], rest[-1]
            k = pl.program_id(2)
            part = dot(a_ref[...].astype(BF16), b_ref[...].astype(BF16))

            @pl.when(k == 0)
            def _():
                acc_ref[...] = part

            @pl.when(k > 0)
            def _():
                acc_ref[...] += part

            @pl.when(k == nk - 1)
            def _():
                o_ref[...] = acc_ref[...].astype(o_ref.dtype)
        scratch = [pltpu.VMEM((tm, tn), F32)]

    return _pcall(
        body, grid=(m // tm, n // tn, nk), in_specs=[a_spec, b_spec] + extra_specs,
        out_specs=pl.BlockSpec((tm, tn), lambda i, j, k: (i, j)),
        out_shape=jax.ShapeDtypeStruct((m, n), out_dtype), scratch_shapes=scratch, name=name,
        compiler_params=_params(("parallel", "parallel", "arbitrary")))(a, b, *extra)


def _norm_fwd(x, g, res, out_dtype, name, after=None):
    s, d = x.shape
    tr = _tile(s, (256, 128))
    has_res = res is not None
    has_after = after is not None

    def body(*refs):
        x_ref, g_ref = refs[0], refs[1]
        o_ref = refs[-1]
        xv = x_ref[...]
        y = xv * lax.rsqrt(jnp.mean(xv * xv, axis=-1, keepdims=True) + EPS) * g_ref[...]
        if has_res:
            y = y + refs[2][...]
        o_ref[...] = y.astype(o_ref.dtype)

    row = pl.BlockSpec((tr, d), lambda i: (i, 0))
    ins = [x, g] + ([res] if has_res else []) + ([after] if has_after else [])
    return _pcall(
        body, grid=(s // tr,),
        in_specs=[row, pl.BlockSpec((1, d), lambda i: (0, 0))] + ([row] if has_res else [])
        + ([pl.BlockSpec(memory_space=pl.ANY)] if has_after else []),
        out_specs=row, out_shape=jax.ShapeDtypeStruct((s, d), out_dtype), name=name,
        compiler_params=_params(("parallel",)))(*ins)


def _norm_bwd(x, g, douts, dres, out_dtype, name, after=None):
    s, d = x.shape
    tr = _tile(s, (256, 128))
    nd = len(douts)
    has_res = dres is not None
    has_after = after is not None

    def body(*refs):
        x_ref, g_ref = refs[0], refs[1]
        dx_ref, dg_ref = refs[-2], refs[-1]
        dout = refs[2][...].astype(F32)
        for r in refs[3:2 + nd]:
            dout = dout + r[...].astype(F32)
        xv = x_ref[...]
        r = lax.rsqrt(jnp.mean(xv * xv, axis=-1, keepdims=True) + EPS)
        n = xv * r
        dn = dout * g_ref[...]
        dx = r * (dn - n * jnp.mean(dn * n, axis=-1, keepdims=True))
        if has_res:
            dx = dx + refs[2 + nd][...]
        dx_ref[...] = dx.astype(dx_ref.dtype)

        @pl.when(pl.program_id(0) == 0)
        def _():
            dg_ref[...] = jnp.zeros_like(dg_ref)

        dg_ref[...] += jnp.sum(dout * n, axis=0, keepdims=True)

    row = pl.BlockSpec((tr, d), lambda i: (i, 0))
    vec = pl.BlockSpec((1, d), lambda i: (0, 0))
    ins = [x, g] + list(douts) + ([dres] if has_res else []) + ([after] if has_after else [])
    return _pcall(
        body, grid=(s // tr,),
        in_specs=[row, vec] + [row] * (nd + int(has_res)) + ([pl.BlockSpec(memory_space=pl.ANY)] if has_after else []),
        out_specs=[row, vec],
        out_shape=[jax.ShapeDtypeStruct((s, d), out_dtype), jax.ShapeDtypeStruct((1, d), F32)], name=name,
        compiler_params=_params(("arbitrary",)))(*ins)


def _loss_head(y, target):
    s, d = y.shape
    tr = _tile(s, (256, 128))

    def body(y_ref, t_ref, sq_ref, dy_ref):
        e = y_ref[...] - t_ref[...]
        dy_ref[...] = e * (1.0 / d)

        @pl.when(pl.program_id(0) == 0)
        def _():
            sq_ref[...] = jnp.zeros_like(sq_ref)

        sq_ref[...] += jnp.sum(e * e, axis=0, keepdims=True)

    row = pl.BlockSpec((tr, d), lambda i: (i, 0))
    return _pcall(
        body, grid=(s // tr,), in_specs=[row, row], out_specs=[pl.BlockSpec((1, d), lambda i: (0, 0)), row],
        out_shape=[jax.ShapeDtypeStruct((1, d), F32), jax.ShapeDtypeStruct((s, d), F32)], name="loss_head",
        compiler_params=_params(("arbitrary",)))(y, target)


NEVER = -1e30
SB_QUERIES = 512


def _sb_sum_matrix(later):
    r = lax.broadcasted_iota(jnp.int32, (HEAD_SB, 2 * HEAD_SB), 0)
    c = lax.broadcasted_iota(jnp.int32, (HEAD_SB, 2 * HEAD_SB), 1)
    tri = jnp.where((r > c) if later else (r < c), 1.0, 0.0)
    return jnp.where(c < HEAD_SB, tri, 1.0).astype(BF16)


def _sb_mask(tq, q0, k0):
    row = lax.broadcasted_iota(jnp.int32, (tq, HEAD_SB), 0)
    col = lax.broadcasted_iota(jnp.int32, (tq, HEAD_SB), 1)
    return (k0 + col) < (q0 + row)


def _sb_fwd(proj, n_heads):
    s = proj.shape[0]
    tq = min(SB_QUERIES, s)
    per = tq // HEAD_SB
    scale = HEAD_SB ** -0.5

    def body(q_ref, k_ref, v_ref, o_ref, a_ref, acc_ref, c_ref):
        i = pl.program_id(1)
        q = q_ref[...].astype(BF16)
        sums = _sb_sum_matrix(True)
        acc_ref[...] = jnp.zeros_like(acc_ref)
        c_ref[...] = jnp.zeros_like(c_ref)
        last = (i + 1) * per - 1

        def scores(j, masked):
            off = pl.multiple_of(j * HEAD_SB, HEAD_SB)
            z = _dot_nt(q, k_ref[pl.ds(off, HEAD_SB), :].astype(BF16)) * scale
            sp = _softplus(z)
            logb = z - sp
            if masked:
                mask = _sb_mask(tq, i * tq, off)
                logb = jnp.where(mask, logb, NEVER)
                sp = jnp.where(mask, sp, 0.0)
            return logb, _split_dot(sp, sums)

        def values(j, logb, both):
            off = pl.multiple_of(j * HEAD_SB, HEAD_SB)
            c = c_ref[...]
            a = jnp.exp(logb - both[:, :HEAD_SB] - c).astype(BF16)
            a_ref[0, 0, j] = a
            acc_ref[...] += _dot(a, v_ref[pl.ds(off, HEAD_SB), :].astype(BF16))
            c_ref[...] = c + both[:, HEAD_SB:]

        def step(jj, carry, masked):
            j = last - jj
            nxt = scores(j, masked)
            values(jnp.minimum(j + 1, last), *carry)
            return nxt

        idle = (jnp.full((tq, HEAD_SB), NEVER, F32), jnp.zeros((tq, 2 * HEAD_SB), F32))
        carry = idle
        for jj in range(per):
            carry = step(jj, carry, True)
        def pair(jp, carry):
            return step(2 * jp + 1, step(2 * jp, carry, False), False)

        carry = lax.fori_loop(per // 2, (last + 1) // 2, pair, carry)
        values(0, *carry)
        o_ref[...] = acc_ref[...].astype(o_ref.dtype)

    h = n_heads
    blk = pl.BlockSpec((tq, HEAD_SB), lambda hh, i: (i, hh))
    return _pcall(
        body, grid=(h, s // tq),
        in_specs=[blk, pl.BlockSpec((s, HEAD_SB), lambda hh, i: (0, h + hh)),
                  pl.BlockSpec((s, HEAD_SB), lambda hh, i: (0, 2 * h + hh))],
        out_specs=[blk, pl.BlockSpec((1, 1, s // HEAD_SB, tq, HEAD_SB), lambda hh, i: (hh, i, 0, 0, 0))],
        out_shape=[jax.ShapeDtypeStruct((s, h * HEAD_SB), BF16),
                   jax.ShapeDtypeStruct((h, s // tq, s // HEAD_SB, tq, HEAD_SB), BF16)],
        scratch_shapes=[pltpu.VMEM((tq, HEAD_SB), F32), pltpu.VMEM((tq, HEAD_SB), F32)],
        name="sb_fwd", compiler_params=_params(("parallel", "arbitrary")))(proj, proj, proj)


def _sb_bwd(proj, a_saved, do, n_heads, after=None):
    s = proj.shape[0]
    tq = min(SB_QUERIES, s)
    per = tq // HEAD_SB
    scale = HEAD_SB ** -0.5
    follow = [] if after is None else [after]

    def body(q_ref, k_ref, v_ref, do_ref, a_ref, *rest):
        dq_ref, dk_ref, dv_ref, run_ref, acc_ref = rest[len(follow):]
        i = pl.program_id(1)

        @pl.when(i == 0)
        def _():
            dk_ref[...] = jnp.zeros_like(dk_ref)
            dv_ref[...] = jnp.zeros_like(dv_ref)

        q = q_ref[...].astype(BF16)
        dob = do_ref[...].astype(BF16)
        run_ref[...] = jnp.zeros_like(run_ref)
        acc_ref[...] = jnp.zeros_like(acc_ref)
        earlier = _sb_sum_matrix(False)
        first_diagonal = i * per

        def step(j, masked):
            off = pl.multiple_of(j * HEAD_SB, HEAD_SB)
            kb = k_ref[pl.ds(off, HEAD_SB), :].astype(BF16)
            vb = v_ref[pl.ds(off, HEAD_SB), :].astype(BF16)
            a = a_ref[0, 0, j]
            g = a.astype(F32) * _dot_nt(dob, vb)
            dv_ref[pl.ds(off, HEAD_SB), :] += _dot_tn(a, dob)
            z = _dot_nt(q, kb) * scale
            beta = 1.0 / (1.0 + jnp.exp(-z))
            both = _split_dot(g, earlier)
            p = run_ref[...]
            dz = (g * (1.0 - beta) - beta * (both[:, :HEAD_SB] + p)) * scale
            if masked:
                dz = jnp.where(_sb_mask(tq, i * tq, off), dz, 0.0)
            dzb = dz.astype(BF16)
            dk_ref[pl.ds(off, HEAD_SB), :] += _dot_tn(dzb, q)
            acc_ref[...] += _dot(dzb, kb)
            run_ref[...] = p + both[:, HEAD_SB:]

        def pair(jp, carry):
            step(2 * jp, False)
            step(2 * jp + 1, False)
            return carry

        lax.fori_loop(0, first_diagonal // 2, pair, 0)
        for u in range(per):
            step(first_diagonal + u, True)
        dq_ref[...] = acc_ref[...]

    h = n_heads
    blk = pl.BlockSpec((tq, HEAD_SB), lambda hh, i: (i, hh))
    col_blk = pl.BlockSpec((s, HEAD_SB), lambda hh, i: (0, hh))
    shape = jax.ShapeDtypeStruct((s, h * HEAD_SB), F32)
    return _pcall(
        body, grid=(h, s // tq),
        in_specs=[blk, pl.BlockSpec((s, HEAD_SB), lambda hh, i: (0, h + hh)),
                  pl.BlockSpec((s, HEAD_SB), lambda hh, i: (0, 2 * h + hh)), blk,
                  pl.BlockSpec((1, 1, s // HEAD_SB, tq, HEAD_SB), lambda hh, i: (hh, i, 0, 0, 0))]
        + [pl.BlockSpec(memory_space=pl.ANY)] * len(follow),
        out_specs=[blk, col_blk, col_blk], out_shape=[shape, shape, shape],
        scratch_shapes=[pltpu.VMEM((tq, HEAD_SB), F32), pltpu.VMEM((tq, HEAD_SB), F32)],
        name="sb_bwd", compiler_params=_params(("parallel", "arbitrary")))(proj, proj, proj, do, a_saved, *follow)


def _gm_mask():
    t = lax.broadcasted_iota(jnp.int32, (GROUP_GM, GROUP_GM), 0)
    s = lax.broadcasted_iota(jnp.int32, (GROUP_GM, GROUP_GM), 1)
    shift = CHUNK.bit_length() - 1
    return (s >> shift) <= (t >> shift)


def _gm_fwd(proj, g_vnorm, w_s, b_st, u_blk):
    s = proj.shape[0]
    groups = w_s.shape[0]
    w = groups * GROUP_GM

    def body(u_ref, v_ref, gv_ref, ws_ref, bst_ref, o_ref):
        ug = _gelu(u_ref[...])
        vg = _gelu(v_ref[...])
        vn = vg * lax.rsqrt(jnp.mean(vg * vg, axis=-1, keepdims=True) + EPS) * gv_ref[...]
        vnb = vn.astype(BF16)
        mask = _gm_mask()
        for g in range(groups):
            sl = slice(g * GROUP_GM, (g + 1) * GROUP_GM)
            wm = jnp.where(mask, ws_ref[g], 0.0).astype(BF16)
            mixed = _dot(wm, vnb[:, sl]) + bst_ref[:, g:g + 1]
            o_ref[:, sl] = (ug[:, sl] * mixed).astype(o_ref.dtype)

    return _pcall(
        body, grid=(s // GROUP_GM,),
        in_specs=[pl.BlockSpec((GROUP_GM, w), lambda c: (c, u_blk)), pl.BlockSpec((GROUP_GM, w), lambda c: (c, u_blk + 1)),
                  pl.BlockSpec((1, w), lambda c: (0, 0)), pl.BlockSpec((groups, GROUP_GM, GROUP_GM), lambda c: (0, 0, 0)),
                  pl.BlockSpec((GROUP_GM, groups), lambda c: (0, 0))],
        out_specs=pl.BlockSpec((GROUP_GM, w), lambda c: (c, 0)),
        out_shape=jax.ShapeDtypeStruct((s, w), BF16), name="gm_fwd",
        compiler_params=_params(("parallel",)))(proj, proj, g_vnorm, w_s, b_st)


def _gm_bwd(proj, g_vnorm, w_s, b_st, do, u_blk):
    s = proj.shape[0]
    groups = w_s.shape[0]
    w = groups * GROUP_GM

    def body(u_ref, v_ref, gv_ref, ws_ref, bst_ref, do_ref, du_ref, dv_ref, dgv_ref, dws_ref, dbst_ref, dvn_ref):
        @pl.when(pl.program_id(0) == 0)
        def _():
            dgv_ref[...] = jnp.zeros_like(dgv_ref)
            dws_ref[...] = jnp.zeros_like(dws_ref)
            dbst_ref[...] = jnp.zeros_like(dbst_ref)

        ug, ugrad = _gelu_and_grad(u_ref[...])
        vg, vgrad = _gelu_and_grad(v_ref[...])
        r = lax.rsqrt(jnp.mean(vg * vg, axis=-1, keepdims=True) + EPS)
        n = vg * r
        gv = gv_ref[...]
        vnb = (n * gv).astype(BF16)
        dout = do_ref[...]
        mask = _gm_mask()
        for g in range(groups):
            sl = slice(g * GROUP_GM, (g + 1) * GROUP_GM)
            wm = jnp.where(mask, ws_ref[g], 0.0).astype(BF16)
            mixed = _dot(wm, vnb[:, sl]) + bst_ref[:, g:g + 1]
            dmixed = dout[:, sl] * ug[:, sl]
            du_ref[:, sl] = dout[:, sl] * mixed * ugrad[:, sl]
            dbst_ref[:, g:g + 1] += jnp.sum(dmixed, axis=1, keepdims=True)
            dmb = dmixed.astype(BF16)
            dws_ref[g] += jnp.where(mask, _dot_nt(dmb, vnb[:, sl]), 0.0)
            dvn_ref[:, sl] = _dot_tn(wm, dmb)
        dvn = dvn_ref[...]
        dgv_ref[...] += jnp.sum(dvn * n, axis=0, keepdims=True)
        dn = dvn * gv
        dvg = r * (dn - n * jnp.mean(dn * n, axis=-1, keepdims=True))
        dv_ref[...] = dvg * vgrad

    rowb = pl.BlockSpec((GROUP_GM, w), lambda c: (c, 0))
    vec = pl.BlockSpec((1, w), lambda c: (0, 0))
    wsb = pl.BlockSpec((groups, GROUP_GM, GROUP_GM), lambda c: (0, 0, 0))
    bsb = pl.BlockSpec((GROUP_GM, groups), lambda c: (0, 0))
    return _pcall(
        body, grid=(s // GROUP_GM,),
        in_specs=[pl.BlockSpec((GROUP_GM, w), lambda c: (c, u_blk)), pl.BlockSpec((GROUP_GM, w), lambda c: (c, u_blk + 1)),
                  vec, wsb, bsb, rowb],
        out_specs=[rowb, rowb, vec, wsb, bsb],
        out_shape=[jax.ShapeDtypeStruct((s, w), F32), jax.ShapeDtypeStruct((s, w), F32), jax.ShapeDtypeStruct((1, w), F32),
                   jax.ShapeDtypeStruct((groups, GROUP_GM, GROUP_GM), F32), jax.ShapeDtypeStruct((GROUP_GM, groups), F32)],
        scratch_shapes=[pltpu.VMEM((GROUP_GM, w), F32)], name="gm_bwd",
        compiler_params=_params(("arbitrary",)))(proj, proj, g_vnorm, w_s, b_st, do)


def _xa_fwd(proj, mem_kv, q_blk, n_heads):
    s = proj.shape[0]
    nm = mem_kv.shape[0]
    tq = _tile(s, (512, 256, 128))
    scale = HEAD_XA ** -0.5

    def body(q_ref, k_ref, v_ref, o_ref):
        z = _dot_nt(q_ref[...].astype(BF16), k_ref[...].astype(BF16)) * scale
        z = z - jnp.max(z, axis=-1, keepdims=True)
        e = jnp.exp(z)
        p = e / jnp.sum(e, axis=-1, keepdims=True)
        o_ref[...] = _dot(p.astype(BF16), v_ref[...].astype(BF16)).astype(o_ref.dtype)

    h = n_heads
    return _pcall(
        body, grid=(h, s // tq),
        in_specs=[pl.BlockSpec((tq, HEAD_XA), lambda hh, i: (i, q_blk + hh)),
                  pl.BlockSpec((nm, HEAD_XA), lambda hh, i: (0, hh)), pl.BlockSpec((nm, HEAD_XA), lambda hh, i: (0, h + hh))],
        out_specs=pl.BlockSpec((tq, HEAD_XA), lambda hh, i: (i, hh)),
        out_shape=jax.ShapeDtypeStruct((s, h * HEAD_XA), BF16), name="xa_fwd",
        compiler_params=_params(("parallel", "parallel")))(proj, mem_kv, mem_kv)


def _xa_bwd(proj, mem_kv, do, q_blk, n_heads):
    s = proj.shape[0]
    nm = mem_kv.shape[0]
    tq = _tile(s, (512, 256, 128))
    scale = HEAD_XA ** -0.5
    h = n_heads

    def body(q_ref, k_ref, v_ref, do_ref, dq_ref, dk_ref, dv_ref):
        @pl.when(pl.program_id(1) == 0)
        def _():
            dk_ref[...] = jnp.zeros_like(dk_ref)
            dv_ref[...] = jnp.zeros_like(dv_ref)

        qb = q_ref[...].astype(BF16)
        kb = k_ref[...].astype(BF16)
        vb = v_ref[...].astype(BF16)
        dob = do_ref[...].astype(BF16)
        z = _dot_nt(qb, kb) * scale
        z = z - jnp.max(z, axis=-1, keepdims=True)
        e = jnp.exp(z)
        p = e / jnp.sum(e, axis=-1, keepdims=True)
        dp = _dot_nt(dob, vb)
        dz = (p * (dp - jnp.sum(dp * p, axis=-1, keepdims=True)) * scale).astype(BF16)
        dq_ref[...] = _dot(dz, kb)
        dk_ref[...] += _dot_tn(dz, qb)
        dv_ref[...] += _dot_tn(p.astype(BF16), dob)

    qspec = pl.BlockSpec((tq, HEAD_XA), lambda hh, i: (i, hh))
    dk, dv = None, None
    dq, dk, dv = _pcall(
        body, grid=(h, s // tq),
        in_specs=[pl.BlockSpec((tq, HEAD_XA), lambda hh, i: (i, q_blk + hh)),
                  pl.BlockSpec((nm, HEAD_XA), lambda hh, i: (0, hh)), pl.BlockSpec((nm, HEAD_XA), lambda hh, i: (0, h + hh)),
                  qspec],
        out_specs=[qspec, pl.BlockSpec((nm, HEAD_XA), lambda hh, i: (0, hh)), pl.BlockSpec((nm, HEAD_XA), lambda hh, i: (0, hh))],
        out_shape=[jax.ShapeDtypeStruct((s, h * HEAD_XA), F32), jax.ShapeDtypeStruct((nm, h * HEAD_XA), F32),
                   jax.ShapeDtypeStruct((nm, h * HEAD_XA), F32)],
        name="xa_bwd", compiler_params=_params(("parallel", "arbitrary")))(proj, mem_kv, mem_kv, do)
    return dq, dk, dv


def _merge_fwd(zg, b_gate, branches):
    s, d = branches[0].shape
    tr = _tile(s, (128,))

    def body(z0, z1, z2, g0, g1, g2, b0, b1, b2, o_ref):
        acc = None
        for z, g, b in ((z0, g0, b0), (z1, g1, b1), (z2, g2, b2)):
            term = jax.nn.sigmoid(z[...].astype(F32) + g[...]) * b[...]
            acc = term if acc is None else acc + term
        o_ref[...] = acc.astype(o_ref.dtype)

    zs = [pl.BlockSpec((tr, d), functools.partial(lambda i, k: (i, k), k=k)) for k in range(3)]
    gs = [pl.BlockSpec((1, d), functools.partial(lambda i, k: (0, k), k=k)) for k in range(3)]
    row = pl.BlockSpec((tr, d), lambda i: (i, 0))
    return _pcall(
        body, grid=(s // tr,), in_specs=zs + gs + [row] * 3, out_specs=row,
        out_shape=jax.ShapeDtypeStruct((s, d), BF16), name="merge_fwd",
        compiler_params=_params(("parallel",)))(zg, zg, zg, b_gate, b_gate, b_gate, *branches)


def _merge_bwd(zg, b_gate, branches, dmerged):
    s, d = branches[0].shape
    tr = _tile(s, (128,))

    def body(z0, z1, z2, g0, g1, g2, b0, b1, b2, dm_ref, dz_ref, d0, d1, d2, dbg_ref):
        @pl.when(pl.program_id(0) == 0)
        def _():
            dbg_ref[...] = jnp.zeros_like(dbg_ref)

        dm = dm_ref[...]
        for k, (z, g, b, dbr) in enumerate(((z0, g0, b0, d0), (z1, g1, b1, d1), (z2, g2, b2, d2))):
            sg = jax.nn.sigmoid(z[...].astype(F32) + g[...])
            dbr[...] = (dm * sg).astype(dbr.dtype)
            dz = dm * b[...] * sg * (1.0 - sg)
            dz_ref[:, k * d:(k + 1) * d] = dz.astype(dz_ref.dtype)
            dbg_ref[:, k * d:(k + 1) * d] += jnp.sum(dz, axis=0, keepdims=True)

    zs = [pl.BlockSpec((tr, d), functools.partial(lambda i, k: (i, k), k=k)) for k in range(3)]
    gs = [pl.BlockSpec((1, d), functools.partial(lambda i, k: (0, k), k=k)) for k in range(3)]
    row = pl.BlockSpec((tr, d), lambda i: (i, 0))
    outs = _pcall(
        body, grid=(s // tr,), in_specs=zs + gs + [row] * 4,
        out_specs=[pl.BlockSpec((tr, 3 * d), lambda i: (i, 0)), row, row, row, pl.BlockSpec((1, 3 * d), lambda i: (0, 0))],
        out_shape=[jax.ShapeDtypeStruct((s, 3 * d), BF16)] + [jax.ShapeDtypeStruct((s, d), BF16)] * 3
        + [jax.ShapeDtypeStruct((1, 3 * d), F32)],
        name="merge_bwd", compiler_params=_params(("arbitrary",)))(zg, zg, zg, b_gate, b_gate, b_gate, *branches, dmerged)
    return outs[0], list(outs[1:4]), outs[4]


def _shift_down(x, k, row):
    return jnp.where(row >= k, pltpu.roll(x, k, 0), 0.0)


def _shift_up(x, k, row, s):
    return jnp.where(row < s - k, pltpu.roll(x, s - k, 0), 0.0)


def _conv_pre(gate, cw_ref, cb_ref, row):
    conv = cb_ref[...] + cw_ref[CONV_TAPS - 1:CONV_TAPS, :] * gate
    for k in range(1, CONV_TAPS):
        conv = conv + cw_ref[CONV_TAPS - 1 - k:CONV_TAPS - k, :] * _shift_down(gate, k, row)
    return conv


def _cg_fwd(up, conv_w, conv_b):
    s = up.shape[0]
    f = conv_w.shape[1]
    tc = _tile(f, (256, 128))
    nb = f // tc

    def body(g_ref, v_ref, cw_ref, cb_ref, o_ref):
        row = lax.broadcasted_iota(jnp.int32, (s, tc), 0)
        conv = _conv_pre(g_ref[...].astype(F32), cw_ref, cb_ref, row)
        o_ref[...] = (_gelu(conv) * v_ref[...].astype(F32)).astype(o_ref.dtype)

    return _pcall(
        body, grid=(nb,),
        in_specs=[pl.BlockSpec((s, tc), lambda j: (0, j)), pl.BlockSpec((s, tc), lambda j: (0, nb + j)),
                  pl.BlockSpec((CONV_TAPS, tc), lambda j: (0, j)), pl.BlockSpec((1, tc), lambda j: (0, j))],
        out_specs=pl.BlockSpec((s, tc), lambda j: (0, j)),
        out_shape=jax.ShapeDtypeStruct((s, f), BF16), name="cg_fwd",
        compiler_params=_params(("parallel",)))(up, up, conv_w, conv_b)


def _cg_bwd(up, conv_w, conv_b, dact):
    s = up.shape[0]
    f = conv_w.shape[1]
    tc = _tile(f, (256, 128))
    nb = f // tc

    def body(g_ref, v_ref, cw_ref, cb_ref, da_ref, dg_ref, dv_ref, dcw_ref, dcb_ref):
        row = lax.broadcasted_iota(jnp.int32, (s, tc), 0)
        gate = g_ref[...].astype(F32)
        conv = _conv_pre(gate, cw_ref, cb_ref, row)
        gel, ggrad = _gelu_and_grad(conv)
        da = da_ref[...]
        dv_ref[...] = (da * gel).astype(dv_ref.dtype)
        dconv = da * v_ref[...].astype(F32) * ggrad
        dgate = cw_ref[CONV_TAPS - 1:CONV_TAPS, :] * dconv
        dcw_ref[CONV_TAPS - 1:CONV_TAPS, :] = jnp.sum(dconv * gate, axis=0, keepdims=True)
        for k in range(1, CONV_TAPS):
            dgate = dgate + cw_ref[CONV_TAPS - 1 - k:CONV_TAPS - k, :] * _shift_up(dconv, k, row, s)
            dcw_ref[CONV_TAPS - 1 - k:CONV_TAPS - k, :] = jnp.sum(dconv * _shift_down(gate, k, row), axis=0, keepdims=True)
        dg_ref[...] = dgate.astype(dg_ref.dtype)
        dcb_ref[...] = jnp.sum(dconv, axis=0, keepdims=True)

    colb = pl.BlockSpec((s, tc), lambda j: (0, j))
    return _pcall(
        body, grid=(nb,),
        in_specs=[colb, pl.BlockSpec((s, tc), lambda j: (0, nb + j)), pl.BlockSpec((CONV_TAPS, tc), lambda j: (0, j)),
                  pl.BlockSpec((1, tc), lambda j: (0, j)), colb],
        out_specs=[colb, colb, pl.BlockSpec((CONV_TAPS, tc), lambda j: (0, j)), pl.BlockSpec((1, tc), lambda j: (0, j))],
        out_shape=[jax.ShapeDtypeStruct((s, f), BF16), jax.ShapeDtypeStruct((s, f), BF16),
                   jax.ShapeDtypeStruct((CONV_TAPS, f), F32), jax.ShapeDtypeStruct((1, f), F32)],
        name="cg_bwd", compiler_params=_params(("parallel",)))(up, up, conv_w, conv_b, dact)


def _row_tile(rows, cols, elems=256 * 1024):
    want = max(16, elems // cols)
    for c in (512, 256, 128, 64, 32, 16):
        if c <= want and rows % c == 0:
            return c
    return rows


def _sum_halves(dwv, recv, core, name):
    nj, _, a, c = dwv.shape
    tr = _row_tile(a, c, 1024 * 1024)

    def body(core_ref, d_ref, r_ref, o_ref):
        o_ref[0] = (d_ref[0, 0].astype(F32) + r_ref[0].astype(F32)).astype(o_ref.dtype)

    grid_spec = pltpu.PrefetchScalarGridSpec(
        num_scalar_prefetch=1, grid=(nj, a // tr),
        in_specs=[pl.BlockSpec((1, 1, tr, c), lambda j, i, cr: (j, cr[0], i, 0)),
                  pl.BlockSpec((1, tr, c), lambda j, i, cr: (j, i, 0))],
        out_specs=pl.BlockSpec((1, tr, c), lambda j, i, cr: (j, i, 0)))
    return _pcall(body, grid_spec=grid_spec, out_shape=jax.ShapeDtypeStruct((nj, a, c), BF16), name=name,
                  compiler_params=_params(("parallel", "parallel")))(core, dwv, recv)


def _sum_chips(recv, own, chip, core, ax):
    _, a, b = recv.shape
    tr = _row_tile(a, b, 512 * 1024)

    def body(chip_ref, core_ref, r_ref, own_ref, o_ref):
        me = chip_ref[0]
        mine = own_ref[0].astype(F32)
        acc = None
        for k in range(N_CHIPS):
            term = jnp.where(me == k, mine, r_ref[k].astype(F32))
            acc = term if acc is None else acc + term
        o_ref[0] = acc

    own_spec = (pl.BlockSpec((1, tr, b), lambda i, ch, co: (0, i, ch[0])) if ax == 2
                else pl.BlockSpec((1, tr, b), lambda i, ch, co: (ch[0], i, 0)))
    grid_spec = pltpu.PrefetchScalarGridSpec(
        num_scalar_prefetch=2, grid=(a // tr,),
        in_specs=[pl.BlockSpec((N_CHIPS, tr, b), lambda i, ch, co: (0, i, 0)), own_spec],
        out_specs=pl.BlockSpec((1, tr, b), lambda i, ch, co: (co[0], i, 0)))
    return _pcall(body, grid_spec=grid_spec, out_shape=jax.ShapeDtypeStruct((2, a, b), F32),
                  name="sum_chips", compiler_params=_params(("parallel",)))(chip, core, recv, own)


def _place_own(wt, layer, chip, ax, after):
    nl, r, c = wt.shape
    half = r // 2
    tr = _row_tile(half, c, 512 * 1024)
    nb = half // tr

    def body(chip_ref, w_ref, after_ref, o_ref):
        o_ref[...] = w_ref[...].astype(BF16).reshape(o_ref.shape)

    if ax == 2:
        out_spec = pl.BlockSpec((1, tr, c), lambda h, i, ch: (h, i, ch[0]))
    else:
        out_spec = pl.BlockSpec((1, 1, tr, c), lambda h, i, ch: (ch[0], h, i, 0))
    grid_spec = pltpu.PrefetchScalarGridSpec(
        num_scalar_prefetch=1, grid=(2, nb),
        in_specs=[pl.BlockSpec((1, tr, c), lambda h, i, ch: (layer, h * nb + i, 0)), pl.BlockSpec(memory_space=pl.ANY)],
        out_specs=out_spec)
    return _pcall(body, grid_spec=grid_spec, out_shape=jax.ShapeDtypeStruct(_full_view_shape(wt.shape, ax), BF16),
                  name="place_own", compiler_params=_params(("parallel", "parallel")))(chip, wt, after)


def _adamw(w, m, v, g, layer, prev, name, after=None):
    nl, r, c = w.shape
    tr = _row_tile(r, c, 512 * 1024)
    c1 = 1.0 - ADAM_B1 ** ADAM_STEP
    c2 = 1.0 - ADAM_B2 ** ADAM_STEP

    follow = [] if after is None else [after]

    def body(w_ref, m_ref, v_ref, gin_ref, *rest):
        g_ref, d_ref, nm_ref, nv_ref = rest[-4:]
        g = gin_ref[...]
        mm = ADAM_B1 * m_ref[0] + (1.0 - ADAM_B1) * g
        vv = ADAM_B2 * v_ref[0] + (1.0 - ADAM_B2) * (g * g)
        g_ref[0] = g
        nm_ref[0] = mm
        nv_ref[0] = vv
        d_ref[0] = -ADAM_LR * ((mm / c1) / (jnp.sqrt(vv / c2) + ADAM_EPS) + ADAM_WD * w_ref[0])

    blk = pl.BlockSpec((1, tr, c), lambda i: (layer, i, 0))
    shape = jax.ShapeDtypeStruct((nl, r, c), F32)
    extra = [] if prev is None else list(prev)
    return _pcall(
        body, grid=(r // tr,),
        in_specs=[blk] * 3 + [pl.BlockSpec((tr, c), lambda i: (i, 0))] + [pl.BlockSpec(memory_space=pl.ANY)] * (len(extra) + len(follow)),
        out_specs=[blk] * 4, out_shape=[shape] * 4, input_output_aliases={4 + k: k for k in range(len(extra))}, name=name,
        compiler_params=_params(("parallel",)))(w, m, v, g, *extra, *follow)


HBM_SPEC = pl.BlockSpec(memory_space=pltpu.HBM)
COMM = pltpu.CompilerParams(has_side_effects=True)


def _position():
    x, y, c = lax.axis_index("x"), lax.axis_index("y"), lax.axis_index("c")
    chips = [(1 - x, y), (x, 1 - y), (1 - x, 1 - y)]
    return x, y, c, chips


def _remote(src, dst, send_sem, recv_sem, dev):
    return pltpu.make_async_remote_copy(src_ref=src, dst_ref=dst, send_sem=send_sem, recv_sem=recv_sem,
                                        device_id=dev, device_id_type=MESH)


def _full_view_shape(shard_shape, ax):
    _, r, c = shard_shape
    return (2, r // 2, c * N_CHIPS) if ax == 2 else (N_CHIPS, 2, r // 2, c)


def _piece(ref, ax, j, h, cs):
    if ax == 2:
        return ref.at[h, :, pl.ds(pl.multiple_of(j * cs, cs), cs)]
    return ref.at[j, h]


def _chip_block(ref, ax, j, cs):
    if ax == 2:
        return ref.at[:, :, pl.ds(pl.multiple_of(j * cs, cs), cs)]
    return ref.at[j]


SEM_SPEC = pl.BlockSpec(memory_space=pltpu.SEMAPHORE)
ANY_SPEC = pl.BlockSpec(memory_space=pl.ANY)
SPLIT = pltpu.CompilerParams(has_side_effects=pltpu.SideEffectType.DATAFLOW_SIDE_EFFECTING)


def _exchange(kind, name, bufs, build, n_sems, after=None, sems=None):
    n = len(bufs)
    if kind == 'sync':
        def body(*refs):
            mine, theirs = build(refs[n:2 * n], refs[2 * n], refs[2 * n + 1])
            for cp in mine:
                cp.start()
            for cp in theirs:
                cp.wait_recv()
            for cp in mine:
                cp.wait_send()

        return list(_pcall(
            body, in_specs=[HBM_SPEC] * n, out_specs=[HBM_SPEC] * n,
            out_shape=[jax.ShapeDtypeStruct(v.shape, v.dtype) for v in bufs], input_output_aliases={t: t for t in range(n)},
            scratch_shapes=[pltpu.SemaphoreType.DMA((n_sems,)), pltpu.SemaphoreType.DMA((n_sems,))],
            name=name, compiler_params=COMM)(*bufs))
    if kind == 'start':
        def body(*refs):
            mine, _ = build(refs[n + 3:2 * n + 3], refs[n + 1], refs[n + 2])
            for cp in mine:
                cp.start()
            refs[2 * n + 3][...] = jnp.zeros_like(refs[2 * n + 3])

        outs = _pcall(
            body, in_specs=[HBM_SPEC] * n + [ANY_SPEC],
            out_specs=[SEM_SPEC, SEM_SPEC] + [HBM_SPEC] * n + [pl.BlockSpec(memory_space=pltpu.VMEM)],
            out_shape=[pltpu.SemaphoreType.DMA((n_sems,)), pltpu.SemaphoreType.DMA((n_sems,))]
            + [pltpu.HBM(v.shape, v.dtype) for v in bufs] + [jax.ShapeDtypeStruct((8, LANES), F32)],
            input_output_aliases={t: 2 + t for t in range(n)}, name=name,
            compiler_params=SPLIT)(*[pltpu.with_memory_space_constraint(v, pltpu.HBM) for v in bufs], after)
        return (outs[0], outs[1]), list(outs[2:2 + n]), outs[2 + n]

    def body(*refs):
        mine, theirs = build(refs[:n], refs[n], refs[n + 1])
        for cp in mine:
            cp.wait_send()
        for cp in theirs:
            cp.wait_recv()

    return list(_pcall(
        body, in_specs=[HBM_SPEC] * n + [SEM_SPEC, SEM_SPEC, ANY_SPEC], out_specs=[HBM_SPEC] * n,
        out_shape=[pltpu.HBM(v.shape, v.dtype) for v in bufs], input_output_aliases={t: t for t in range(n)},
        name=name, compiler_params=SPLIT)(*bufs, sems[0], sems[1], after))


def _gather_ici_copies(axes, shard_cols):
    def build(bufs, send_sems, recv_sems):
        x, y, c, chips = _position()
        me = 2 * x + y
        mine, theirs = [], []
        for t, ax in enumerate(axes):
            own = _piece(bufs[t], ax, me, c, shard_cols[t])
            for p, (px, py) in enumerate(chips):
                k = t * 3 + p
                got = _piece(bufs[t], ax, 2 * px + py, c, shard_cols[t])
                mine.append(_remote(own, own, send_sems.at[k], recv_sems.at[k], (px, py, c)))
                theirs.append(_remote(got, got, send_sems.at[k], recv_sems.at[k], (px, py, c)))
        return mine, theirs
    return build


def _gather_d2d_copies(axes, shard_cols):
    def build(bufs, send_sems, recv_sems):
        x, y, c, chips = _position()
        mine, theirs = [], []
        for t, ax in enumerate(axes):
            for p, (px, py) in enumerate(chips):
                k = t * 3 + p
                had = _piece(bufs[t], ax, 2 * px + py, c, shard_cols[t])
                got = _piece(bufs[t], ax, 2 * px + py, 1 - c, shard_cols[t])
                mine.append(_remote(had, had, send_sems.at[k], recv_sems.at[k], (x, y, 1 - c)))
                theirs.append(_remote(got, got, send_sems.at[k], recv_sems.at[k], (x, y, 1 - c)))
        return mine, theirs
    return build


def _grads_d2d_copies(n):
    def build(bufs, send_sems, recv_sems):
        x, y, c, _ = _position()
        mine = [_remote(bufs[t].at[:, 1 - c], bufs[n + t], send_sems.at[t], recv_sems.at[t], (x, y, 1 - c)) for t in range(n)]
        return mine, mine
    return build


def _grads_ici_copies(axes):
    n = len(axes)

    def build(bufs, send_sems, recv_sems):
        x, y, c, chips = _position()
        me = 2 * x + y

        def block(t, j):
            if axes[t] == 2:
                cs = bufs[n + t].shape[2]
                return bufs[t].at[0, :, pl.ds(pl.multiple_of(j * cs, cs), cs)]
            return bufs[t].at[j]

        mine, theirs = [], []
        for t in range(n):
            for p, (px, py) in enumerate(chips):
                k = t * 3 + p
                peer = 2 * px + py
                mine.append(_remote(block(t, peer), bufs[n + t].at[me], send_sems.at[k], recv_sems.at[k], (px, py, c)))
                theirs.append(_remote(block(t, peer), bufs[n + t].at[peer], send_sems.at[k], recv_sems.at[k], (px, py, c)))
        return mine, theirs
    return build


def _join_copies(bufs, send_sems, recv_sems):
    x, y, c, _ = _position()
    mine = [_remote(b.at[c], b.at[c], send_sems.at[t], recv_sems.at[t], (x, y, 1 - c)) for t, b in enumerate(bufs)]
    theirs = [_remote(b.at[1 - c], b.at[1 - c], send_sems.at[t], recv_sems.at[t], (x, y, 1 - c)) for t, b in enumerate(bufs)]
    return mine, theirs


def _grads_recv_shape(sm, ax):
    _, a, c = sm.shape
    return (N_CHIPS, a, c // N_CHIPS if ax == 2 else c)


def _gather_small(shard):
    nl, r, cs = shard.shape

    def body(in_ref, out_ref, send_sems, recv_sems, local_sem):
        x, y, c, chips = _position()

        def cols(j):
            return out_ref.at[:, :, pl.ds(pl.multiple_of(j * cs, cs), cs)]

        me = 2 * x + y
        loc = pltpu.make_async_copy(in_ref, cols(me), local_sem)
        loc.start()
        remote = [_remote(in_ref, cols(me), send_sems.at[p], recv_sems.at[p], (px, py, c)) for p, (px, py) in enumerate(chips)]
        for cp in remote:
            cp.start()
        for p, (px, py) in enumerate(chips):
            _remote(in_ref, cols(2 * px + py), send_sems.at[p], recv_sems.at[p], (px, py, c)).wait_recv()
        for cp in remote:
            cp.wait_send()
        loc.wait()

    return _pcall(
        body, in_specs=[HBM_SPEC], out_specs=HBM_SPEC, out_shape=jax.ShapeDtypeStruct((nl, r, cs * N_CHIPS), shard.dtype),
        scratch_shapes=[pltpu.SemaphoreType.DMA((3,)), pltpu.SemaphoreType.DMA((3,)), pltpu.SemaphoreType.DMA(())],
        name="gather_small", compiler_params=COMM)(shard)


def _small_copies(bufs, send_sems, recv_sems):
    pack, land = bufs
    x, y, cc, _ = _position()
    me = 4 * x + 2 * y + cc
    mine, theirs = [], []
    for k in range(1, N_DEV):
        px, py, pc = x ^ ((k >> 2) & 1), y ^ ((k >> 1) & 1), cc ^ (k & 1)
        mine.append(_remote(pack, land.at[me], send_sems.at[k - 1], recv_sems.at[k - 1], (px, py, pc)))
        theirs.append(_remote(pack, land.at[4 * px + 2 * py + pc], send_sems.at[k - 1], recv_sems.at[k - 1], (px, py, pc)))
    return mine, theirs


def _sum_devices(land, pack, dev):
    _, r, c = land.shape
    tr = _tile(r, (672, 512, 256, 128, 64, 8))

    def body(dev_ref, l_ref, p_ref, o_ref):
        me = dev_ref[0]
        acc = None
        for k in range(N_DEV):
            term = jnp.where(me == k, p_ref[...], l_ref[k])
            acc = term if acc is None else acc + term
        o_ref[...] = acc

    grid_spec = pltpu.PrefetchScalarGridSpec(
        num_scalar_prefetch=1, grid=(r // tr,),
        in_specs=[pl.BlockSpec((N_DEV, tr, c), lambda i, dv: (0, i, 0)), pl.BlockSpec((tr, c), lambda i, dv: (i, 0))],
        out_specs=pl.BlockSpec((tr, c), lambda i, dv: (i, 0)))
    return _pcall(body, grid_spec=grid_spec, out_shape=jax.ShapeDtypeStruct((r, c), F32), name="sum_devices",
                  compiler_params=_params(("parallel",)))(dev, land, pack)


def _dims(d):
    half = d // 2
    return half // HEAD_SB, half // HEAD_XA, 3, (5 * half) // HEAD_XA


def _layer_fwd(x, mem, weight, small, l, after=None):
    h_sb, h_xa, u_blk, q_blk = _dims(x.shape[1])

    def vec(name):
        return small[name][l].reshape(1, -1)

    def use(a, name, follows, mm_name, dtype=F32):
        wt = weight(name, follows)
        return _mm(a, wt, 'nn', dtype, mm_name, after=weight.token())

    h1 = _norm_fwd(x, vec('g_mix_pre'), None, BF16, "norm_mix_pre", after)
    proj = use(h1, 'w_in', h1, "mm_proj")
    o_sb, a_sb = _sb_fwd(proj, h_sb)
    b_st = small['b_s'][l].T
    o_gm = _gm_fwd(proj, vec('g_vnorm'), small['w_s'][l], b_st, u_blk)
    memn = _norm_fwd(mem, vec('g_mem'), None, BF16, "norm_mem")
    mem_kv = use(memn, 'w_mem_kv', o_sb, "mm_mem_kv")
    o_xa = _xa_fwd(proj, mem_kv, q_blk, h_xa)
    zg = use(h1, 'w_gate', o_sb, "mm_gate", BF16)
    branches = [use(o, wn, zg, "mm_branch") for o, wn in ((o_sb, 'w_br_sb'), (o_gm, 'w_br_gm'), (o_xa, 'w_br_xa'))]
    merged = _merge_fwd(zg, vec('b_gate'), branches)
    y1 = use(merged, 'w_out', zg, "mm_out")
    x1 = _norm_fwd(y1, vec('g_mix_post'), x, F32, "norm_mix_post")
    h2 = _norm_fwd(x1, vec('g_ffn_pre'), None, BF16, "norm_ffn_pre")
    up = use(h2, 'w_up', h2, "mm_up", BF16)
    act = _cg_fwd(up, weight('conv_w', up), vec('conv_b'))
    y2 = use(act, 'w_down', act, "mm_down")
    x2 = _norm_fwd(y2, vec('g_ffn_post'), x1, F32, "norm_ffn_post")
    saved = dict(x0=x, h1=h1, proj=proj, o_sb=o_sb, a_sb=a_sb, o_gm=o_gm, o_xa=o_xa, memn=memn, mem_kv=mem_kv, zg=zg,
                 branches=branches, merged=merged, y1=y1, x1=x1, h2=h2, up=up, act=act, y2=y2, b_st=b_st)
    return x2, saved


def _layer_bwd(dx, mem, sv, full, small, l, after, emit):
    h_sb, h_xa, u_blk, q_blk = _dims(dx.shape[1])

    def vec(name):
        return small[name][l].reshape(1, -1)

    gb, gs = {}, {}
    dy2, gs['g_ffn_post'] = _norm_bwd(sv['y2'], vec('g_ffn_post'), [dx], None, BF16, "norm_ffn_post_bwd", after)
    gb['w_down'] = _mm(sv['act'], dy2, 'tn', BF16, "mm_down_dw")
    dact = _mm(dy2, full['w_down'], 'nt', F32, "mm_down_dx")
    dgate, dval, gs['conv_w'], gs['conv_b'] = _cg_bwd(sv['up'], full['conv_w'], vec('conv_b'), dact)
    dup = jnp.concatenate([dgate, dval], axis=1)
    gb['w_up'] = _mm(sv['h2'], dup, 'tn', BF16, "mm_up_dw")
    token = emit(0, gb)
    dh2 = _mm(dup, full['w_up'], 'nt', F32, "mm_up_dx", after=token)
    token = emit.flush(dh2)
    dx1, gs['g_ffn_pre'] = _norm_bwd(sv['x1'], vec('g_ffn_pre'), [dh2], dx, F32, "norm_ffn_pre_bwd", token)
    dy1, gs['g_mix_post'] = _norm_bwd(sv['y1'], vec('g_mix_post'), [dx1], None, BF16, "norm_mix_post_bwd")
    gb['w_out'] = _mm(sv['merged'], dy1, 'tn', BF16, "mm_out_dw")
    dmerged = _mm(dy1, full['w_out'], 'nt', F32, "mm_out_dx")
    dzg, dbr, gs['b_gate'] = _merge_bwd(sv['zg'], vec('b_gate'), sv['branches'], dmerged)
    douts = []
    for o, db, wn in ((sv['o_sb'], dbr[0], 'w_br_sb'), (sv['o_gm'], dbr[1], 'w_br_gm'), (sv['o_xa'], dbr[2], 'w_br_xa')):
        gb[wn] = _mm(o, db, 'tn', BF16, "mm_branch_dw")
        douts.append(_mm(db, full[wn], 'nt', F32, "mm_branch_dx"))
    gb['w_gate'] = _mm(sv['h1'], dzg, 'tn', BF16, "mm_gate_dw")
    dq_xa, dk_xa, dv_xa = _xa_bwd(sv['proj'], sv['mem_kv'], douts[2], q_blk, h_xa)
    dmem_kv = jnp.concatenate([dk_xa, dv_xa], axis=1).astype(BF16)
    gb['w_mem_kv'] = _mm(sv['memn'], dmem_kv, 'tn', BF16, "mm_mem_kv_dw")
    token = emit(1, gb)
    dh1_gate = _mm(dzg, full['w_gate'], 'nt', F32, "mm_gate_dx", after=token)
    token = emit.flush(dh1_gate)
    dmemn = _mm(dmem_kv, full['w_mem_kv'], 'nt', F32, "mm_mem_kv_dx")
    _, gs['g_mem'] = _norm_bwd(mem, vec('g_mem'), [dmemn], None, BF16, "norm_mem_bwd")
    du, dv, gs['g_vnorm'], gs['w_s'], db_st = _gm_bwd(sv['proj'], vec('g_vnorm'), small['w_s'][l], sv['b_st'], douts[1], u_blk)
    gs['b_s'] = db_st.T
    dq, dk, dvv = _sb_bwd(sv['proj'], sv['a_sb'], douts[0], h_sb, token)
    dproj = jnp.concatenate([dq, dk, dvv, du, dv, dq_xa], axis=1).astype(BF16)
    gb['w_in'] = _mm(sv['h1'], dproj, 'tn', BF16, "mm_proj_dw")
    token = emit(2, gb)
    dh1_proj = _mm(dproj, full['w_in'], 'nt', F32, "mm_proj_dx")
    dx0, gs['g_mix_pre'] = _norm_bwd(sv['x0'], vec('g_mix_pre'), [dh1_gate, dh1_proj], dx1, F32, "norm_mix_pre_bwd", token)
    return dx0, gs


class _Given:
    def __init__(self, full):
        self.full = full

    def __call__(self, name, follows):
        return self.full[name]

    def token(self):
        return None


def _local_step(x, mem, target, full, small):
    n_layers = len(full['w_in'])
    saved = []
    for l in range(n_layers):
        x, sv = _layer_fwd(x, mem, _Given({n: full[n][l] for n in full}), small, l)
        saved.append(sv)
    sq, dx = _loss_head(x, target)
    gbig = {n: [None] * n_layers for n in BIG}
    gsmall = {n: [None] * n_layers for n in SMALL + ['conv_w']}
    class Collect:
        def __init__(self, l):
            self.l = l

        def __call__(self, g, gb):
            for n in BWD_GROUPS[g]:
                gbig[n][self.l] = gb[n]

        def flush(self, follows):
            return None

    for l in reversed(range(n_layers)):
        dx, gs = _layer_bwd(dx, mem, saved[l], {n: full[n][l] for n in full}, small, l, None, Collect(l))
        for n in gs:
            gsmall[n][l] = gs[n]
    return sq, dx, gbig, gsmall


def _pack(arrays, rows_multiple):
    flat = jnp.concatenate([a.reshape(-1).astype(F32) for a in arrays])
    rows = -(-flat.shape[0] // LANES)
    rows = -(-rows // rows_multiple) * rows_multiple
    return jnp.pad(flat, (0, rows * LANES - flat.shape[0])).reshape(rows, LANES)


def _unpack(pack, like):
    flat = pack.reshape(-1)
    out, off = [], 0
    for a in like:
        out.append(flat[off:off + a.size].reshape(a.shape))
        off += a.size
    return out


def _grad_view(g, ax):
    r, c = g.shape
    return g.reshape(1, 2, r // 2, c) if ax == 2 else g.reshape(N_CHIPS, 2, r // (2 * N_CHIPS), c)


def kernel(x, mem, g_mix_pre, w_in, g_vnorm, w_s, b_s, g_mem, w_mem_kv, w_gate, b_gate, w_br_sb, w_br_gm, w_br_xa, w_out, g_mix_post, g_ffn_pre, w_up, conv_w, conv_b, w_down, g_ffn_post, loss_target, m_g_mix_pre, m_w_in, m_g_vnorm, m_w_s, m_b_s, m_g_mem, m_w_mem_kv, m_w_gate, m_b_gate, m_w_br_sb, m_w_br_gm, m_w_br_xa, m_w_out, m_g_mix_post, m_g_ffn_pre, m_w_up, m_conv_w, m_conv_b, m_w_down, m_g_ffn_post, v_g_mix_pre, v_w_in, v_g_vnorm, v_w_s, v_b_s, v_g_mem, v_w_mem_kv, v_w_gate, v_b_gate, v_w_br_sb, v_w_br_gm, v_w_br_xa, v_w_out, v_g_mix_post, v_g_ffn_pre, v_w_up, v_conv_w, v_conv_b, v_w_down, v_g_ffn_post):
    w = dict(g_mix_pre=g_mix_pre, w_in=w_in, g_vnorm=g_vnorm, w_s=w_s, b_s=b_s, g_mem=g_mem, w_mem_kv=w_mem_kv,
             w_gate=w_gate, b_gate=b_gate, w_br_sb=w_br_sb, w_br_gm=w_br_gm, w_br_xa=w_br_xa, w_out=w_out,
             g_mix_post=g_mix_post, g_ffn_pre=g_ffn_pre, w_up=w_up, conv_w=conv_w, conv_b=conv_b, w_down=w_down,
             g_ffn_post=g_ffn_post)
    m = dict(g_mix_pre=m_g_mix_pre, w_in=m_w_in, g_vnorm=m_g_vnorm, w_s=m_w_s, b_s=m_b_s, g_mem=m_g_mem,
             w_mem_kv=m_w_mem_kv, w_gate=m_w_gate, b_gate=m_b_gate, w_br_sb=m_w_br_sb, w_br_gm=m_w_br_gm,
             w_br_xa=m_w_br_xa, w_out=m_w_out, g_mix_post=m_g_mix_post, g_ffn_pre=m_g_ffn_pre, w_up=m_w_up,
             conv_w=m_conv_w, conv_b=m_conv_b, w_down=m_w_down, g_ffn_post=m_g_ffn_post)
    v = dict(g_mix_pre=v_g_mix_pre, w_in=v_w_in, g_vnorm=v_g_vnorm, w_s=v_w_s, b_s=v_b_s, g_mem=v_g_mem,
             w_mem_kv=v_w_mem_kv, w_gate=v_w_gate, b_gate=v_b_gate, w_br_sb=v_w_br_sb, w_br_gm=v_w_br_gm,
             w_br_xa=v_w_br_xa, w_out=v_w_out, g_mix_post=v_g_mix_post, g_ffn_pre=v_g_ffn_pre, w_up=v_w_up,
             conv_w=v_conv_w, conv_b=v_conv_b, w_down=v_w_down, g_ffn_post=v_g_ffn_post)
    n_layers = w_in.shape[0]
    d = x.shape[-1]
    core = lax.axis_index("c").astype(jnp.int32).reshape(1)
    chip = (2 * lax.axis_index("x") + lax.axis_index("y")).astype(jnp.int32).reshape(1)
    small = {n: w[n] for n in SMALL}
    xs, mems, target = x[0], mem[0], loss_target[0]

    conv_w_full = _gather_small(conv_w)

    def as_full(vw, ax):
        return vw.reshape(-1, vw.shape[-1]) if ax == 1 else vw.reshape(vw.shape[0] * vw.shape[1], vw.shape[2])

    stages, token = {}, conv_w_full
    keys = [(l, g) for l in range(n_layers) for g in range(len(FWD_GROUPS))]
    for l, g in keys:
        names = FWD_GROUPS[g]
        ax_g = [BIG_AXIS[n] for n in names]
        cols_g = [w[n].shape[2] for n in names]
        views = [_place_own(w[n], l, chip, ax, token) for n, ax in zip(names, ax_g)]
        sems, views, token = _exchange('start', f"gather_ici_start_{l}_{g}", views, _gather_ici_copies(ax_g, cols_g),
                                       3 * len(names), after=token)
        stages[l, g] = dict(names=names, ax=ax_g, cols=cols_g, views=views, ici=sems, d2d=None, full=None)

    def cross_cores(key, follows):
        st, (l, g) = stages[key], key
        n3 = 3 * len(st['names'])
        views = _exchange('wait', f"gather_ici_wait_{l}_{g}", st['views'], _gather_ici_copies(st['ax'], st['cols']), n3,
                          after=follows, sems=st['ici'])
        st['d2d'], st['views'], tok = _exchange('start', f"gather_d2d_start_{l}_{g}", views,
                                                _gather_d2d_copies(st['ax'], st['cols']), n3, after=core)
        return tok

    class Weights:
        def __init__(self, l):
            self.l, self.tok = l, None

        def __call__(self, name, follows):
            if name == 'conv_w':
                return conv_w_full[self.l]
            key = (self.l, [g for g, names in enumerate(FWD_GROUPS) if name in names][0])
            st = stages[key]
            if st['full'] is None:
                if st['d2d'] is None:
                    cross_cores(key, follows)
                views = _exchange('wait', f"gather_d2d_wait_{key[0]}_{key[1]}", st['views'],
                                  _gather_d2d_copies(st['ax'], st['cols']), 3 * len(st['names']), after=follows, sems=st['d2d'])
                st['full'] = {n: as_full(vw, ax) for n, vw, ax in zip(st['names'], views, st['ax'])}
            nxt = keys.index(key) + 1
            if name == st['names'][-1] and nxt < len(keys) and key != keys[0] and stages[keys[nxt]]['d2d'] is None:
                self.tok = cross_cores(keys[nxt], follows)
            return st['full'][name]

        def token(self):
            return self.tok

    fulls, saved = [], []
    for l in range(n_layers):
        xs, sv = _layer_fwd(xs, mems, Weights(l), small, l, token if l == 0 else None)
        fulls.append({n: stages[l, g]['full'][n] for g, names in enumerate(FWD_GROUPS) for n in names} | {'conv_w': conv_w_full[l]})
        saved.append(sv)
    sq, dx = _loss_head(xs, target)
    loss = lax.psum(0.5 * jnp.sum(sq) / d, ("x", "y", "c"))

    sent = []

    def to_chips(l, g, names, ax_g, bufs, after):
        n = len(names)
        sums = [_sum_halves(dv, th, core, "sum_halves") for dv, th in zip(bufs[:n], bufs[n:])]
        lands = [lax.empty(_grads_recv_shape(sm, ax), sm.dtype) for sm, ax in zip(sums, ax_g)]
        sems, bufs, tok = _exchange('start', f"grads_ici_start_{l}_{g}", sums + lands, _grads_ici_copies(ax_g), 3 * n,
                                    after=core if after is None else after)
        sent.append((l, g, names, ax_g, bufs, sems))
        return tok

    class Grads:
        def __init__(self, l):
            self.l, self.crossing = l, None

        def __call__(self, g, gb):
            names = BWD_GROUPS[g]
            ax_g = [BIG_AXIS[n] for n in names]
            n = len(names)
            dwvs = [_grad_view(gb[nm], ax) for nm, ax in zip(names, ax_g)]
            lands = [lax.empty((dv.shape[0],) + dv.shape[2:], dv.dtype) for dv in dwvs]
            if g + 1 < len(BWD_GROUPS):
                sems, bufs, tok = _exchange('start', f"grads_d2d_start_{self.l}_{g}", dwvs + lands, _grads_d2d_copies(n), n,
                                            after=core)
                self.crossing = (g, names, ax_g, bufs, sems)
                return tok
            return to_chips(self.l, g, names, ax_g, _exchange('sync', "grads_d2d", dwvs + lands, _grads_d2d_copies(n), n), None)

        def flush(self, follows):
            if self.crossing is None:
                return None
            (g, names, ax_g, bufs, sems), self.crossing = self.crossing, None
            bufs = _exchange('wait', f"grads_d2d_wait_{self.l}_{g}", bufs, _grads_d2d_copies(len(names)), len(names),
                             after=follows, sems=sems)
            return to_chips(self.l, g, names, ax_g, bufs, None)

    gsmall = {n: [None] * n_layers for n in SMALL + ['conv_w']}
    for l in reversed(range(n_layers)):
        dx, gs = _layer_bwd(dx, mems, saved[l], fulls[l], small, l, None, Grads(l))
        for n in gs:
            gsmall[n][l] = gs[n]

    names_small = SMALL + ['conv_w']
    small_full = [jnp.stack(gsmall[n]).reshape(w[n].shape) for n in SMALL]
    conv_w_grad = jnp.stack(gsmall['conv_w'])
    pack = _pack(small_full + [conv_w_grad], 8)
    small_sems, small_bufs, _ = _exchange('start', "small_start", [pack, lax.empty((N_DEV,) + pack.shape, F32)], _small_copies,
                                          N_DEV - 1, after=dx)

    def small_update(follows):
        pk, land = _exchange('wait', "small_wait", small_bufs, _small_copies, N_DEV - 1, after=follows, sems=small_sems)
        device = (4 * lax.axis_index("x") + 2 * lax.axis_index("y") + lax.axis_index("c")).astype(jnp.int32).reshape(1)
        summed = _sum_devices(land, pk, device)
        *small_g, conv_w_g = _unpack(summed, small_full + [conv_w_grad])
        shard = conv_w.shape[-1]
        conv_w_g = lax.dynamic_slice_in_dim(conv_w_g, chip[0] * shard, shard, axis=2)
        packed = [_pack([p[n] for n in names_small], 256) for p in (w, m, v)]
        gpack = _pack(small_g + [conv_w_g], 256)
        res = _adamw(packed[0][None], packed[1][None], packed[2][None], gpack, 0, None, "adamw_small")
        like = [w[n] for n in names_small]
        unpacked = [_unpack(r[0], like) for r in res]
        return {n: tuple(u[i] for u in unpacked) for i, n in enumerate(names_small)}

    out = {}

    def update(joining, follows):
        l, g, names, halves, sems = joining
        halves = _exchange('wait', f"join_wait_{l}_{g}", halves, _join_copies, len(names), after=follows, sems=sems)
        last = None
        for n, hv in zip(names, halves):
            out[n] = _adamw(w[n], m[n], v[n], hv.reshape(w[n].shape[1:]), l, out.get(n), "adamw_big", last)
            last = out[n][0]
        return last

    joining, follows = None, dx
    for k, (l, g, names, ax_g, bufs, sems) in enumerate(sent):
        n = len(names)
        bufs = _exchange('wait', f"grads_ici_wait_{l}_{g}", bufs, _grads_ici_copies(ax_g), 3 * n, after=follows, sems=sems)
        halves = [_sum_chips(r, sm, chip, core, ax) for sm, r, ax in zip(bufs[:n], bufs[n:], ax_g)]
        jsems, halves, tok = _exchange('start', f"join_start_{l}_{g}", halves, _join_copies, n, after=core)
        if joining is not None:
            follows = update(joining, tok)
        joining = (l, g, names, halves, jsems)
    out.update(small_update(update(joining, follows)))

    return (loss, dx[None], *[out[n][0] for n in WEIGHTS], *[out[n][1] for n in WEIGHTS],
            *[out[n][2] for n in WEIGHTS], *[out[n][3] for n in WEIGHTS])
```

```python
import functools
import math

import jax
import jax.numpy as jnp
from jax import lax
from jax.experimental import pallas as pl
from jax.experimental.pallas import tpu as pltpu

F32 = jnp.float32
BF16 = jnp.bfloat16
EPS = 1e-6
HEAD_SB = 128
GROUP_GM = 128
CHUNK = 64
HEAD_XA = 256
CONV_TAPS = 3
N_CHIPS = 4
N_DEV = 8
LANES = 128
MIB = 1024 * 1024
VMEM_LIMIT = 48 * MIB
SPLITS = 1

ADAM_LR = 0.001
ADAM_B1 = 0.9
ADAM_B2 = 0.999
ADAM_EPS = 1e-08
ADAM_WD = 0.01
ADAM_STEP = 10

WEIGHTS = ['g_mix_pre', 'w_in', 'g_vnorm', 'w_s', 'b_s', 'g_mem', 'w_mem_kv', 'w_gate', 'b_gate', 'w_br_sb',
           'w_br_gm', 'w_br_xa', 'w_out', 'g_mix_post', 'g_ffn_pre', 'w_up', 'conv_w', 'conv_b', 'w_down',
           'g_ffn_post']
BIG_AXIS = {'w_in': 2, 'w_mem_kv': 1, 'w_gate': 2, 'w_br_sb': 2, 'w_br_gm': 2, 'w_br_xa': 2, 'w_out': 1,
            'w_up': 2, 'w_down': 1}
BIG = list(BIG_AXIS)
FWD_GROUPS = [['w_in'], ['w_mem_kv', 'w_gate'], ['w_br_sb', 'w_br_gm', 'w_br_xa', 'w_out'], ['w_up'], ['w_down']]
BWD_GROUPS = [['w_down', 'w_up'], ['w_out', 'w_br_sb', 'w_br_gm', 'w_br_xa', 'w_gate', 'w_mem_kv'], ['w_in']]
SMALL = ['g_mix_pre', 'g_vnorm', 'w_s', 'b_s', 'g_mem', 'b_gate', 'g_mix_post', 'g_ffn_pre', 'conv_b', 'g_ffn_post']
MESH = pl.DeviceIdType.MESH


def _pcall(body, **kw):
    return pl.pallas_call(body, **kw)


def _params(sem=None, vmem=VMEM_LIMIT):
    return pltpu.CompilerParams(dimension_semantics=sem, vmem_limit_bytes=vmem)


def _tile(n, cands):
    for c in cands:
        if n % c == 0:
            return c
    return n


_GELU_C = math.sqrt(2.0 / math.pi)
_GELU_A = 0.044715


def _gelu(x):
    return 0.5 * x * (1.0 + jnp.tanh(_GELU_C * (x + _GELU_A * (x * x * x))))


def _gelu_and_grad(x):
    x2 = x * x
    t = jnp.tanh(_GELU_C * (x + _GELU_A * (x2 * x)))
    val = 0.5 * x * (1.0 + t)
    grad = 0.5 * (1.0 + t) + 0.5 * x * (1.0 - t * t) * (_GELU_C * (1.0 + 3.0 * _GELU_A * x2))
    return val, grad


def _softplus(z):
    return jnp.maximum(z, 0.0) + jnp.log(1.0 + jnp.exp(-jnp.abs(z)))


def _dot(a, b):
    return jnp.dot(a, b, preferred_element_type=F32)


def _dot_nt(a, b):
    return lax.dot_general(a, b, (((1,), (1,)), ((), ())), preferred_element_type=F32)


def _dot_tn(a, b):
    return lax.dot_general(a, b, (((0,), (0,)), ((), ())), preferred_element_type=F32)


def _split_dot(a, m):
    out = None
    rest = a
    for _ in range(SPLITS):
        piece = rest.astype(BF16)
        rest = rest - piece.astype(F32)
        term = _dot(piece, m)
        out = term if out is None else out + term
    return out


def _mm(a, b, mode, out_dtype, name, tm=None, tn=None, tk=None, after=None):
    if mode == 'nn':
        (m, kc), (kc2, n) = a.shape, b.shape
    elif mode == 'nt':
        (m, kc), (n, kc2) = a.shape, b.shape
    else:
        (kc, m), (kc2, n) = a.shape, b.shape
    assert kc == kc2, (a.shape, b.shape, mode)
    tm = tm or _tile(m, (1024, 512, 256, 128))
    tn = tn or _tile(n, (1024, 512, 256, 128))
    tk = tk or (kc if kc <= 3072 else _tile(kc, (3072, 2816, 2048, 1536, 1408, 1024, 512)))
    nk = kc // tk
    dot = {'nn': _dot, 'nt': _dot_nt, 'tn': _dot_tn}[mode]
    a_spec = pl.BlockSpec((tk, tm), lambda i, j, k: (k, i)) if mode == 'tn' else pl.BlockSpec((tm, tk), lambda i, j, k: (i, k))
    b_spec = pl.BlockSpec((tn, tk), lambda i, j, k: (j, k)) if mode == 'nt' else pl.BlockSpec((tk, tn), lambda i, j, k: (k, j))

    extra = [] if after is None else [after]
    extra_specs = [pl.BlockSpec(memory_space=pl.ANY)] * len(extra)

    if nk == 1:
        def body(a_ref, b_ref, *rest):
            o_ref = rest[-1]
            o_ref[...] = dot(a_ref[...].astype(BF16), b_ref[...].astype(BF16)).astype(o_ref.dtype)
        scratch = []
    else:
        def body(a_ref, b_ref, *rest):
            o_ref, acc_ref = rest[-2], rest[-1]
            k = pl.program_id(2)
            part = dot(a_ref[...].astype(BF16), b_ref[...].astype(BF16))

            @pl.when(k == 0)
            def _():
                acc_ref[...] = part

            @pl.when(k > 0)
            def _():
                acc_ref[...] += part

            @pl.when(k == nk - 1)
            def _():
                o_ref[...] = acc_ref[...].astype(o_ref.dtype)
        scratch = [pltpu.VMEM((tm, tn), F32)]

    return _pcall(
        body, grid=(m // tm, n // tn, nk), in_specs=[a_spec, b_spec] + extra_specs,
        out_specs=pl.BlockSpec((tm, tn), lambda i, j, k: (i, j)),
        out_shape=jax.ShapeDtypeStruct((m, n), out_dtype), scratch_shapes=scratch, name=name,
        compiler_params=_params(("parallel", "parallel", "arbitrary")))(a, b, *extra)


def _norm_fwd(x, g, res, out_dtype, name, after=None):
    s, d = x.shape
    tr = _tile(s, (256, 128))
    has_res = res is not None
    has_after = after is not None

    def body(*refs):
        x_ref, g_ref = refs[0], refs[1]
        o_ref = refs[-1]
        xv = x_ref[...]
        y = xv * lax.rsqrt(jnp.mean(xv * xv, axis=-1, keepdims=True) + EPS) * g_ref[...]
        if has_res:
            y = y + refs[2][...]
        o_ref[...] = y.astype(o_ref.dtype)

    row = pl.BlockSpec((tr, d), lambda i: (i, 0))
    ins = [x, g] + ([res] if has_res else []) + ([after] if has_after else [])
    return _pcall(
        body, grid=(s // tr,),
        in_specs=[row, pl.BlockSpec((1, d), lambda i: (0, 0))] + ([row] if has_res else [])
        + ([pl.BlockSpec(memory_space=pl.ANY)] if has_after else []),
        out_specs=row, out_shape=jax.ShapeDtypeStruct((s, d), out_dtype), name=name,
        compiler_params=_params(("parallel",)))(*ins)


def _norm_bwd(x, g, douts, dres, out_dtype, name, after=None):
    s, d = x.shape
    tr = _tile(s, (256, 128))
    nd = len(douts)
    has_res = dres is not None
    has_after = after is not None

    def body(*refs):
        x_ref, g_ref = refs[0], refs[1]
        dx_ref, dg_ref = refs[-2], refs[-1]
        dout = refs[2][...].astype(F32)
        for r in refs[3:2 + nd]:
            dout = dout + r[...].astype(F32)
        xv = x_ref[...]
        r = lax.rsqrt(jnp.mean(xv * xv, axis=-1, keepdims=True) + EPS)
        n = xv * r
        dn = dout * g_ref[...]
        dx = r * (dn - n * jnp.mean(dn * n, axis=-1, keepdims=True))
        if has_res:
            dx = dx + refs[2 + nd][...]
        dx_ref[...] = dx.astype(dx_ref.dtype)

        @pl.when(pl.program_id(0) == 0)
        def _():
            dg_ref[...] = jnp.zeros_like(dg_ref)

        dg_ref[...] += jnp.sum(dout * n, axis=0, keepdims=True)

    row = pl.BlockSpec((tr, d), lambda i: (i, 0))
    vec = pl.BlockSpec((1, d), lambda i: (0, 0))
    ins = [x, g] + list(douts) + ([dres] if has_res else []) + ([after] if has_after else [])
    return _pcall(
        body, grid=(s // tr,),
        in_specs=[row, vec] + [row] * (nd + int(has_res)) + ([pl.BlockSpec(memory_space=pl.ANY)] if has_after else []),
        out_specs=[row, vec],
        out_shape=[jax.ShapeDtypeStruct((s, d), out_dtype), jax.ShapeDtypeStruct((1, d), F32)], name=name,
        compiler_params=_params(("arbitrary",)))(*ins)


def _loss_head(y, target):
    s, d = y.shape
    tr = _tile(s, (256, 128))

    def body(y_ref, t_ref, sq_ref, dy_ref):
        e = y_ref[...] - t_ref[...]
        dy_ref[...] = e * (1.0 / d)

        @pl.when(pl.program_id(0) == 0)
        def _():
            sq_ref[...] = jnp.zeros_like(sq_ref)

        sq_ref[...] += jnp.sum(e * e, axis=0, keepdims=True)

    row = pl.BlockSpec((tr, d), lambda i: (i, 0))
    return _pcall(
        body, grid=(s // tr,), in_specs=[row, row], out_specs=[pl.BlockSpec((1, d), lambda i: (0, 0)), row],
        out_shape=[jax.ShapeDtypeStruct((1, d), F32), jax.ShapeDtypeStruct((s, d), F32)], name="loss_head",
        compiler_params=_params(("arbitrary",)))(y, target)


NEVER = -1e30
SB_QUERIES = 512


def _sb_sum_matrix(later):
    r = lax.broadcasted_iota(jnp.int32, (HEAD_SB, 2 * HEAD_SB), 0)
    c = lax.broadcasted_iota(jnp.int32, (HEAD_SB, 2 * HEAD_SB), 1)
    tri = jnp.where((r > c) if later else (r < c), 1.0, 0.0)
    return jnp.where(c < HEAD_SB, tri, 1.0).astype(BF16)


def _sb_mask(tq, q0, k0):
    row = lax.broadcasted_iota(jnp.int32, (tq, HEAD_SB), 0)
    col = lax.broadcasted_iota(jnp.int32, (tq, HEAD_SB), 1)
    return (k0 + col) < (q0 + row)


def _sb_fwd(proj, n_heads):
    s = proj.shape[0]
    tq = min(SB_QUERIES, s)
    per = tq // HEAD_SB
    scale = HEAD_SB ** -0.5

    def body(q_ref, k_ref, v_ref, o_ref, a_ref, acc_ref, c_ref):
        i = pl.program_id(1)
        q = q_ref[...].astype(BF16)
        sums = _sb_sum_matrix(True)
        acc_ref[...] = jnp.zeros_like(acc_ref)
        c_ref[...] = jnp.zeros_like(c_ref)
        last = (i + 1) * per - 1

        def scores(j, masked):
            off = pl.multiple_of(j * HEAD_SB, HEAD_SB)
            z = _dot_nt(q, k_ref[pl.ds(off, HEAD_SB), :].astype(BF16)) * scale
            sp = _softplus(z)
            logb = z - sp
            if masked:
                mask = _sb_mask(tq, i * tq, off)
                logb = jnp.where(mask, logb, NEVER)
                sp = jnp.where(mask, sp, 0.0)
            return logb, _split_dot(sp, sums)

        def values(j, logb, both):
            off = pl.multiple_of(j * HEAD_SB, HEAD_SB)
            c = c_ref[...]
            a = jnp.exp(logb - both[:, :HEAD_SB] - c).astype(BF16)
            a_ref[0, 0, j] = a
            acc_ref[...] += _dot(a, v_ref[pl.ds(off, HEAD_SB), :].astype(BF16))
            c_ref[...] = c + both[:, HEAD_SB:]

        def step(jj, carry, masked):
            j = last - jj
            nxt = scores(j, masked)
            values(jnp.minimum(j + 1, last), *carry)
            return nxt

        idle = (jnp.full((tq, HEAD_SB), NEVER, F32), jnp.zeros((tq, 2 * HEAD_SB), F32))
        carry = idle
        for jj in range(per):
            carry = step(jj, carry, True)
        def pair(jp, carry):
            return step(2 * jp + 1, step(2 * jp, carry, False), False)

        carry = lax.fori_loop(per // 2, (last + 1) // 2, pair, carry)
        values(0, *carry)
        o_ref[...] = acc_ref[...].astype(o_ref.dtype)

    h = n_heads
    blk = pl.BlockSpec((tq, HEAD_SB), lambda hh, i: (i, hh))
    return _pcall(
        body, grid=(h, s // tq),
        in_specs=[blk, pl.BlockSpec((s, HEAD_SB), lambda hh, i: (0, h + hh)),
                  pl.BlockSpec((s, HEAD_SB), lambda hh, i: (0, 2 * h + hh))],
        out_specs=[blk, pl.BlockSpec((1, 1, s // HEAD_SB, tq, HEAD_SB), lambda hh, i: (hh, i, 0, 0, 0))],
        out_shape=[jax.ShapeDtypeStruct((s, h * HEAD_SB), BF16),
                   jax.ShapeDtypeStruct((h, s // tq, s // HEAD_SB, tq, HEAD_SB), BF16)],
        scratch_shapes=[pltpu.VMEM((tq, HEAD_SB), F32), pltpu.VMEM((tq, HEAD_SB), F32)],
        name="sb_fwd", compiler_params=_params(("parallel", "arbitrary")))(proj, proj, proj)


def _sb_bwd(proj, a_saved, do, n_heads, after=None):
    s = proj.shape[0]
    tq = min(SB_QUERIES, s)
    per = tq // HEAD_SB
    scale = HEAD_SB ** -0.5
    follow = [] if after is None else [after]

    def body(q_ref, k_ref, v_ref, do_ref, a_ref, *rest):
        dq_ref, dk_ref, dv_ref, run_ref, acc_ref = rest[len(follow):]
        i = pl.program_id(1)

        @pl.when(i == 0)
        def _():
            dk_ref[...] = jnp.zeros_like(dk_ref)
            dv_ref[...] = jnp.zeros_like(dv_ref)

        q = q_ref[...].astype(BF16)
        dob = do_ref[...].astype(BF16)
        run_ref[...] = jnp.zeros_like(run_ref)
        acc_ref[...] = jnp.zeros_like(acc_ref)
        earlier = _sb_sum_matrix(False)
        first_diagonal = i * per

        def step(j, masked):
            off = pl.multiple_of(j * HEAD_SB, HEAD_SB)
            kb = k_ref[pl.ds(off, HEAD_SB), :].astype(BF16)
            vb = v_ref[pl.ds(off, HEAD_SB), :].astype(BF16)
            a = a_ref[0, 0, j]
            g = a.astype(F32) * _dot_nt(dob, vb)
            dv_ref[pl.ds(off, HEAD_SB), :] += _dot_tn(a, dob)
            z = _dot_nt(q, kb) * scale
            beta = 1.0 / (1.0 + jnp.exp(-z))
            both = _split_dot(g, earlier)
            p = run_ref[...]
            dz = (g * (1.0 - beta) - beta * (both[:, :HEAD_SB] + p)) * scale
            if masked:
                dz = jnp.where(_sb_mask(tq, i * tq, off), dz, 0.0)
            dzb = dz.astype(BF16)
            dk_ref[pl.ds(off, HEAD_SB), :] += _dot_tn(dzb, q)
            acc_ref[...] += _dot(dzb, kb)
            run_ref[...] = p + both[:, HEAD_SB:]

        def pair(jp, carry):
            step(2 * jp, False)
            step(2 * jp + 1, False)
            return carry

        lax.fori_loop(0, first_diagonal // 2, pair, 0)
        for u in range(per):
            step(first_diagonal + u, True)
        dq_ref[...] = acc_ref[...]

    h = n_heads
    blk = pl.BlockSpec((tq, HEAD_SB), lambda hh, i: (i, hh))
    col_blk = pl.BlockSpec((s, HEAD_SB), lambda hh, i: (0, hh))
    shape = jax.ShapeDtypeStruct((s, h * HEAD_SB), F32)
    return _pcall(
        body, grid=(h, s // tq),
        in_specs=[blk, pl.BlockSpec((s, HEAD_SB), lambda hh, i: (0, h + hh)),
                  pl.BlockSpec((s, HEAD_SB), lambda hh, i: (0, 2 * h + hh)), blk,
                  pl.BlockSpec((1, 1, s // HEAD_SB, tq, HEAD_SB), lambda hh, i: (hh, i, 0, 0, 0))]
        + [pl.BlockSpec(memory_space=pl.ANY)] * len(follow),
        out_specs=[blk, col_blk, col_blk], out_shape=[shape, shape, shape],
        scratch_shapes=[pltpu.VMEM((tq, HEAD_SB), F32), pltpu.VMEM((tq, HEAD_SB), F32)],
        name="sb_bwd", compiler_params=_params(("parallel", "arbitrary")))(proj, proj, proj, do, a_saved, *follow)


def _gm_mask():
    t = lax.broadcasted_iota(jnp.int32, (GROUP_GM, GROUP_GM), 0)
    s = lax.broadcasted_iota(jnp.int32, (GROUP_GM, GROUP_GM), 1)
    shift = CHUNK.bit_length() - 1
    return (s >> shift) <= (t >> shift)


def _gm_fwd(proj, g_vnorm, w_s, b_st, u_blk):
    s = proj.shape[0]
    groups = w_s.shape[0]
    w = groups * GROUP_GM

    def body(u_ref, v_ref, gv_ref, ws_ref, bst_ref, o_ref):
        ug = _gelu(u_ref[...])
        vg = _gelu(v_ref[...])
        vn = vg * lax.rsqrt(jnp.mean(vg * vg, axis=-1, keepdims=True) + EPS) * gv_ref[...]
        vnb = vn.astype(BF16)
        mask = _gm_mask()
        for g in range(groups):
            sl = slice(g * GROUP_GM, (g + 1) * GROUP_GM)
            wm = jnp.where(mask, ws_ref[g], 0.0).astype(BF16)
            mixed = _dot(wm, vnb[:, sl]) + bst_ref[:, g:g + 1]
            o_ref[:, sl] = (ug[:, sl] * mixed).astype(o_ref.dtype)

    return _pcall(
        body, grid=(s // GROUP_GM,),
        in_specs=[pl.BlockSpec((GROUP_GM, w), lambda c: (c, u_blk)), pl.BlockSpec((GROUP_GM, w), lambda c: (c, u_blk + 1)),
                  pl.BlockSpec((1, w), lambda c: (0, 0)), pl.BlockSpec((groups, GROUP_GM, GROUP_GM), lambda c: (0, 0, 0)),
                  pl.BlockSpec((GROUP_GM, groups), lambda c: (0, 0))],
        out_specs=pl.BlockSpec((GROUP_GM, w), lambda c: (c, 0)),
        out_shape=jax.ShapeDtypeStruct((s, w), BF16), name="gm_fwd",
        compiler_params=_params(("parallel",)))(proj, proj, g_vnorm, w_s, b_st)


def _gm_bwd(proj, g_vnorm, w_s, b_st, do, u_blk):
    s = proj.shape[0]
    groups = w_s.shape[0]
    w = groups * GROUP_GM

    def body(u_ref, v_ref, gv_ref, ws_ref, bst_ref, do_ref, du_ref, dv_ref, dgv_ref, dws_ref, dbst_ref, dvn_ref):
        @pl.when(pl.program_id(0) == 0)
        def _():
            dgv_ref[...] = jnp.zeros_like(dgv_ref)
            dws_ref[...] = jnp.zeros_like(dws_ref)
            dbst_ref[...] = jnp.zeros_like(dbst_ref)

        ug, ugrad = _gelu_and_grad(u_ref[...])
        vg, vgrad = _gelu_and_grad(v_ref[...])
        r = lax.rsqrt(jnp.mean(vg * vg, axis=-1, keepdims=True) + EPS)
        n = vg * r
        gv = gv_ref[...]
        vnb = (n * gv).astype(BF16)
        dout = do_ref[...]
        mask = _gm_mask()
        for g in range(groups):
            sl = slice(g * GROUP_GM, (g + 1) * GROUP_GM)
            wm = jnp.where(mask, ws_ref[g], 0.0).astype(BF16)
            mixed = _dot(wm, vnb[:, sl]) + bst_ref[:, g:g + 1]
            dmixed = dout[:, sl] * ug[:, sl]
            du_ref[:, sl] = dout[:, sl] * mixed * ugrad[:, sl]
            dbst_ref[:, g:g + 1] += jnp.sum(dmixed, axis=1, keepdims=True)
            dmb = dmixed.astype(BF16)
            dws_ref[g] += jnp.where(mask, _dot_nt(dmb, vnb[:, sl]), 0.0)
            dvn_ref[:, sl] = _dot_tn(wm, dmb)
        dvn = dvn_ref[...]
        dgv_ref[...] += jnp.sum(dvn * n, axis=0, keepdims=True)
        dn = dvn * gv
        dvg = r * (dn - n * jnp.mean(dn * n, axis=-1, keepdims=True))
        dv_ref[...] = dvg * vgrad

    rowb = pl.BlockSpec((GROUP_GM, w), lambda c: (c, 0))
    vec = pl.BlockSpec((1, w), lambda c: (0, 0))
    wsb = pl.BlockSpec((groups, GROUP_GM, GROUP_GM), lambda c: (0, 0, 0))
    bsb = pl.BlockSpec((GROUP_GM, groups), lambda c: (0, 0))
    return _pcall(
        body, grid=(s // GROUP_GM,),
        in_specs=[pl.BlockSpec((GROUP_GM, w), lambda c: (c, u_blk)), pl.BlockSpec((GROUP_GM, w), lambda c: (c, u_blk + 1)),
                  vec, wsb, bsb, rowb],
        out_specs=[rowb, rowb, vec, wsb, bsb],
        out_shape=[jax.ShapeDtypeStruct((s, w), F32), jax.ShapeDtypeStruct((s, w), F32), jax.ShapeDtypeStruct((1, w), F32),
                   jax.ShapeDtypeStruct((groups, GROUP_GM, GROUP_GM), F32), jax.ShapeDtypeStruct((GROUP_GM, groups), F32)],
        scratch_shapes=[pltpu.VMEM((GROUP_GM, w), F32)], name="gm_bwd",
        compiler_params=_params(("arbitrary",)))(proj, proj, g_vnorm, w_s, b_st, do)


def _xa_fwd(proj, mem_kv, q_blk, n_heads):
    s = proj.shape[0]
    nm = mem_kv.shape[0]
    tq = _tile(s, (512, 256, 128))
    scale = HEAD_XA ** -0.5

    def body(q_ref, k_ref, v_ref, o_ref):
        z = _dot_nt(q_ref[...].astype(BF16), k_ref[...].astype(BF16)) * scale
        z = z - jnp.max(z, axis=-1, keepdims=True)
        e = jnp.exp(z)
        p = e / jnp.sum(e, axis=-1, keepdims=True)
        o_ref[...] = _dot(p.astype(BF16), v_ref[...].astype(BF16)).astype(o_ref.dtype)

    h = n_heads
    return _pcall(
        body, grid=(h, s // tq),
        in_specs=[pl.BlockSpec((tq, HEAD_XA), lambda hh, i: (i, q_blk + hh)),
                  pl.BlockSpec((nm, HEAD_XA), lambda hh, i: (0, hh)), pl.BlockSpec((nm, HEAD_XA), lambda hh, i: (0, h + hh))],
        out_specs=pl.BlockSpec((tq, HEAD_XA), lambda hh, i: (i, hh)),
        out_shape=jax.ShapeDtypeStruct((s, h * HEAD_XA), BF16), name="xa_fwd",
        compiler_params=_params(("parallel", "parallel")))(proj, mem_kv, mem_kv)


def _xa_bwd(proj, mem_kv, do, q_blk, n_heads):
    s = proj.shape[0]
    nm = mem_kv.shape[0]
    tq = _tile(s, (512, 256, 128))
    scale = HEAD_XA ** -0.5
    h = n_heads

    def body(q_ref, k_ref, v_ref, do_ref, dq_ref, dk_ref, dv_ref):
        @pl.when(pl.program_id(1) == 0)
        def _():
            dk_ref[...] = jnp.zeros_like(dk_ref)
            dv_ref[...] = jnp.zeros_like(dv_ref)

        qb = q_ref[...].astype(BF16)
        kb = k_ref[...].astype(BF16)
        vb = v_ref[...].astype(BF16)
        dob = do_ref[...].astype(BF16)
        z = _dot_nt(qb, kb) * scale
        z = z - jnp.max(z, axis=-1, keepdims=True)
        e = jnp.exp(z)
        p = e / jnp.sum(e, axis=-1, keepdims=True)
        dp = _dot_nt(dob, vb)
        dz = (p * (dp - jnp.sum(dp * p, axis=-1, keepdims=True)) * scale).astype(BF16)
        dq_ref[...] = _dot(dz, kb)
        dk_ref[...] += _dot_tn(dz, qb)
        dv_ref[...] += _dot_tn(p.astype(BF16), dob)

    qspec = pl.BlockSpec((tq, HEAD_XA), lambda hh, i: (i, hh))
    dk, dv = None, None
    dq, dk, dv = _pcall(
        body, grid=(h, s // tq),
        in_specs=[pl.BlockSpec((tq, HEAD_XA), lambda hh, i: (i, q_blk + hh)),
                  pl.BlockSpec((nm, HEAD_XA), lambda hh, i: (0, hh)), pl.BlockSpec((nm, HEAD_XA), lambda hh, i: (0, h + hh)),
                  qspec],
        out_specs=[qspec, pl.BlockSpec((nm, HEAD_XA), lambda hh, i: (0, hh)), pl.BlockSpec((nm, HEAD_XA), lambda hh, i: (0, hh))],
        out_shape=[jax.ShapeDtypeStruct((s, h * HEAD_XA), F32), jax.ShapeDtypeStruct((nm, h * HEAD_XA), F32),
                   jax.ShapeDtypeStruct((nm, h * HEAD_XA), F32)],
        name="xa_bwd", compiler_params=_params(("parallel", "arbitrary")))(proj, mem_kv, mem_kv, do)
    return dq, dk, dv


def _merge_fwd(zg, b_gate, branches):
    s, d = branches[0].shape
    tr = _tile(s, (128,))

    def body(z0, z1, z2, g0, g1, g2, b0, b1, b2, o_ref):
        acc = None
        for z, g, b in ((z0, g0, b0), (z1, g1, b1), (z2, g2, b2)):
            term = jax.nn.sigmoid(z[...].astype(F32) + g[...]) * b[...]
            acc = term if acc is None else acc + term
        o_ref[...] = acc.astype(o_ref.dtype)

    zs = [pl.BlockSpec((tr, d), functools.partial(lambda i, k: (i, k), k=k)) for k in range(3)]
    gs = [pl.BlockSpec((1, d), functools.partial(lambda i, k: (0, k), k=k)) for k in range(3)]
    row = pl.BlockSpec((tr, d), lambda i: (i, 0))
    return _pcall(
        body, grid=(s // tr,), in_specs=zs + gs + [row] * 3, out_specs=row,
        out_shape=jax.ShapeDtypeStruct((s, d), BF16), name="merge_fwd",
        compiler_params=_params(("parallel",)))(zg, zg, zg, b_gate, b_gate, b_gate, *branches)


def _merge_bwd(zg, b_gate, branches, dmerged):
    s, d = branches[0].shape
    tr = _tile(s, (128,))

    def body(z0, z1, z2, g0, g1, g2, b0, b1, b2, dm_ref, dz_ref, d0, d1, d2, dbg_ref):
        @pl.when(pl.program_id(0) == 0)
        def _():
            dbg_ref[...] = jnp.zeros_like(dbg_ref)

        dm = dm_ref[...]
        for k, (z, g, b, dbr) in enumerate(((z0, g0, b0, d0), (z1, g1, b1, d1), (z2, g2, b2, d2))):
            sg = jax.nn.sigmoid(z[...].astype(F32) + g[...])
            dbr[...] = (dm * sg).astype(dbr.dtype)
            dz = dm * b[...] * sg * (1.0 - sg)
            dz_ref[:, k * d:(k + 1) * d] = dz.astype(dz_ref.dtype)
            dbg_ref[:, k * d:(k + 1) * d] += jnp.sum(dz, axis=0, keepdims=True)

    zs = [pl.BlockSpec((tr, d), functools.partial(lambda i, k: (i, k), k=k)) for k in range(3)]
    gs = [pl.BlockSpec((1, d), functools.partial(lambda i, k: (0, k), k=k)) for k in range(3)]
    row = pl.BlockSpec((tr, d), lambda i: (i, 0))
    outs = _pcall(
        body, grid=(s // tr,), in_specs=zs + gs + [row] * 4,
        out_specs=[pl.BlockSpec((tr, 3 * d), lambda i: (i, 0)), row, row, row, pl.BlockSpec((1, 3 * d), lambda i: (0, 0))],
        out_shape=[jax.ShapeDtypeStruct((s, 3 * d), BF16)] + [jax.ShapeDtypeStruct((s, d), BF16)] * 3
        + [jax.ShapeDtypeStruct((1, 3 * d), F32)],
        name="merge_bwd", compiler_params=_params(("arbitrary",)))(zg, zg, zg, b_gate, b_gate, b_gate, *branches, dmerged)
    return outs[0], list(outs[1:4]), outs[4]


def _shift_down(x, k, row):
    return jnp.where(row >= k, pltpu.roll(x, k, 0), 0.0)


def _shift_up(x, k, row, s):
    return jnp.where(row < s - k, pltpu.roll(x, s - k, 0), 0.0)


def _conv_pre(gate, cw_ref, cb_ref, row):
    conv = cb_ref[...] + cw_ref[CONV_TAPS - 1:CONV_TAPS, :] * gate
    for k in range(1, CONV_TAPS):
        conv = conv + cw_ref[CONV_TAPS - 1 - k:CONV_TAPS - k, :] * _shift_down(gate, k, row)
    return conv


def _cg_fwd(up, conv_w, conv_b):
    s = up.shape[0]
    f = conv_w.shape[1]
    tc = _tile(f, (256, 128))
    nb = f // tc

    def body(g_ref, v_ref, cw_ref, cb_ref, o_ref):
        row = lax.broadcasted_iota(jnp.int32, (s, tc), 0)
        conv = _conv_pre(g_ref[...].astype(F32), cw_ref, cb_ref, row)
        o_ref[...] = (_gelu(conv) * v_ref[...].astype(F32)).astype(o_ref.dtype)

    return _pcall(
        body, grid=(nb,),
        in_specs=[pl.BlockSpec((s, tc), lambda j: (0, j)), pl.BlockSpec((s, tc), lambda j: (0, nb + j)),
                  pl.BlockSpec((CONV_TAPS, tc), lambda j: (0, j)), pl.BlockSpec((1, tc), lambda j: (0, j))],
        out_specs=pl.BlockSpec((s, tc), lambda j: (0, j)),
        out_shape=jax.ShapeDtypeStruct((s, f), BF16), name="cg_fwd",
        compiler_params=_params(("parallel",)))(up, up, conv_w, conv_b)


def _cg_bwd(up, conv_w, conv_b, dact):
    s = up.shape[0]
    f = conv_w.shape[1]
    tc = _tile(f, (256, 128))
    nb = f // tc

    def body(g_ref, v_ref, cw_ref, cb_ref, da_ref, dg_ref, dv_ref, dcw_ref, dcb_ref):
        row = lax.broadcasted_iota(jnp.int32, (s, tc), 0)
        gate = g_ref[...].astype(F32)
        conv = _conv_pre(gate, cw_ref, cb_ref, row)
        gel, ggrad = _gelu_and_grad(conv)
        da = da_ref[...]
        dv_ref[...] = (da * gel).astype(dv_ref.dtype)
        dconv = da * v_ref[...].astype(F32) * ggrad
        dgate = cw_ref[CONV_TAPS - 1:CONV_TAPS, :] * dconv
        dcw_ref[CONV_TAPS - 1:CONV_TAPS, :] = jnp.sum(dconv * gate, axis=0, keepdims=True)
        for k in range(1, CONV_TAPS):
            dgate = dgate + cw_ref[CONV_TAPS - 1 - k:CONV_TAPS - k, :] * _shift_up(dconv, k, row, s)
            dcw_ref[CONV_TAPS - 1 - k:CONV_TAPS - k, :] = jnp.sum(dconv * _shift_down(gate, k, row), axis=0, keepdims=True)
        dg_ref[...] = dgate.astype(dg_ref.dtype)
        dcb_ref[...] = jnp.sum(dconv, axis=0, keepdims=True)

    colb = pl.BlockSpec((s, tc), lambda j: (0, j))
    return _pcall(
        body, grid=(nb,),
        in_specs=[colb, pl.BlockSpec((s, tc), lambda j: (0, nb + j)), pl.BlockSpec((CONV_TAPS, tc), lambda j: (0, j)),
                  pl.BlockSpec((1, tc), lambda j: (0, j)), colb],
        out_specs=[colb, colb, pl.BlockSpec((CONV_TAPS, tc), lambda j: (0, j)), pl.BlockSpec((1, tc), lambda j: (0, j))],
        out_shape=[jax.ShapeDtypeStruct((s, f), BF16), jax.ShapeDtypeStruct((s, f), BF16),
                   jax.ShapeDtypeStruct((CONV_TAPS, f), F32), jax.ShapeDtypeStruct((1, f), F32)],
        name="cg_bwd", compiler_params=_params(("parallel",)))(up, up, conv_w, conv_b, dact)


def _row_tile(rows, cols, elems=256 * 1024):
    want = max(16, elems // cols)
    for c in (512, 256, 128, 64, 32, 16):
        if c <= want and rows % c == 0:
            return c
    return rows


def _sum_halves(dwv, recv, core, name):
    nj, _, a, c = dwv.shape
    tr = _row_tile(a, c, 1024 * 1024)

    def body(core_ref, d_ref, r_ref, o_ref):
        o_ref[0] = (d_ref[0, 0].astype(F32) + r_ref[0].astype(F32)).astype(o_ref.dtype)

    grid_spec = pltpu.PrefetchScalarGridSpec(
        num_scalar_prefetch=1, grid=(nj, a // tr),
        in_specs=[pl.BlockSpec((1, 1, tr, c), lambda j, i, cr: (j, cr[0], i, 0)),
                  pl.BlockSpec((1, tr, c), lambda j, i, cr: (j, i, 0))],
        out_specs=pl.BlockSpec((1, tr, c), lambda j, i, cr: (j, i, 0)))
    return _pcall(body, grid_spec=grid_spec, out_shape=jax.ShapeDtypeStruct((nj, a, c), BF16), name=name,
                  compiler_params=_params(("parallel", "parallel")))(core, dwv, recv)


def _sum_chips(recv, own, chip, core, ax):
    _, a, b = recv.shape
    tr = _row_tile(a, b, 512 * 1024)

    def body(chip_ref, core_ref, r_ref, own_ref, o_ref):
        me = chip_ref[0]
        mine = own_ref[0].astype(F32)
        acc = None
        for k in range(N_CHIPS):
            term = jnp.where(me == k, mine, r_ref[k].astype(F32))
            acc = term if acc is None else acc + term
        o_ref[0] = acc

    own_spec = (pl.BlockSpec((1, tr, b), lambda i, ch, co: (0, i, ch[0])) if ax == 2
                else pl.BlockSpec((1, tr, b), lambda i, ch, co: (ch[0], i, 0)))
    grid_spec = pltpu.PrefetchScalarGridSpec(
        num_scalar_prefetch=2, grid=(a // tr,),
        in_specs=[pl.BlockSpec((N_CHIPS, tr, b), lambda i, ch, co: (0, i, 0)), own_spec],
        out_specs=pl.BlockSpec((1, tr, b), lambda i, ch, co: (co[0], i, 0)))
    return _pcall(body, grid_spec=grid_spec, out_shape=jax.ShapeDtypeStruct((2, a, b), F32),
                  name="sum_chips", compiler_params=_params(("parallel",)))(chip, core, recv, own)


def _place_own(wt, layer, chip, ax, after):
    nl, r, c = wt.shape
    half = r // 2
    tr = _row_tile(half, c, 512 * 1024)
    nb = half // tr

    def body(chip_ref, w_ref, after_ref, o_ref):
        o_ref[...] = w_ref[...].astype(BF16).reshape(o_ref.shape)

    if ax == 2:
        out_spec = pl.BlockSpec((1, tr, c), lambda h, i, ch: (h, i, ch[0]))
    else:
        out_spec = pl.BlockSpec((1, 1, tr, c), lambda h, i, ch: (ch[0], h, i, 0))
    grid_spec = pltpu.PrefetchScalarGridSpec(
        num_scalar_prefetch=1, grid=(2, nb),
        in_specs=[pl.BlockSpec((1, tr, c), lambda h, i, ch: (layer, h * nb + i, 0)), pl.BlockSpec(memory_space=pl.ANY)],
        out_specs=out_spec)
    return _pcall(body, grid_spec=grid_spec, out_shape=jax.ShapeDtypeStruct(_full_view_shape(wt.shape, ax), BF16),
                  name="place_own", compiler_params=_params(("parallel", "parallel")))(chip, wt, after)


def _adamw(w, m, v, g, layer, prev, name, after=None):
    nl, r, c = w.shape
    tr = _row_tile(r, c, 512 * 1024)
    c1 = 1.0 - ADAM_B1 ** ADAM_STEP
    c2 = 1.0 - ADAM_B2 ** ADAM_STEP

    follow = [] if after is None else [after]

    def body(w_ref, m_ref, v_ref, gin_ref, *rest):
        g_ref, d_ref, nm_ref, nv_ref = rest[-4:]
        g = gin_ref[...]
        mm = ADAM_B1 * m_ref[0] + (1.0 - ADAM_B1) * g
        vv = ADAM_B2 * v_ref[0] + (1.0 - ADAM_B2) * (g * g)
        g_ref[0] = g
        nm_ref[0] = mm
        nv_ref[0] = vv
        d_ref[0] = -ADAM_LR * ((mm / c1) / (jnp.sqrt(vv / c2) + ADAM_EPS) + ADAM_WD * w_ref[0])

    blk = pl.BlockSpec((1, tr, c), lambda i: (layer, i, 0))
    shape = jax.ShapeDtypeStruct((nl, r, c), F32)
    extra = [] if prev is None else list(prev)
    return _pcall(
        body, grid=(r // tr,),
        in_specs=[blk] * 3 + [pl.BlockSpec((tr, c), lambda i: (i, 0))] + [pl.BlockSpec(memory_space=pl.ANY)] * (len(extra) + len(follow)),
        out_specs=[blk] * 4, out_shape=[shape] * 4, input_output_aliases={4 + k: k for k in range(len(extra))}, name=name,
        compiler_params=_params(("parallel",)))(w, m, v, g, *extra, *follow)


HBM_SPEC = pl.BlockSpec(memory_space=pltpu.HBM)
COMM = pltpu.CompilerParams(has_side_effects=True)


def _position():
    x, y, c = lax.axis_index("x"), lax.axis_index("y"), lax.axis_index("c")
    chips = [(1 - x, y), (x, 1 - y), (1 - x, 1 - y)]
    return x, y, c, chips


def _remote(src, dst, send_sem, recv_sem, dev):
    return pltpu.make_async_remote_copy(src_ref=src, dst_ref=dst, send_sem=send_sem, recv_sem=recv_sem,
                                        device_id=dev, device_id_type=MESH)


def _full_view_shape(shard_shape, ax):
    _, r, c = shard_shape
    return (2, r // 2, c * N_CHIPS) if ax == 2 else (N_CHIPS, 2, r // 2, c)


def _piece(ref, ax, j, h, cs):
    if ax == 2:
        return ref.at[h, :, pl.ds(pl.multiple_of(j * cs, cs), cs)]
    return ref.at[j, h]


def _chip_block(ref, ax, j, cs):
    if ax == 2:
        return ref.at[:, :, pl.ds(pl.multiple_of(j * cs, cs), cs)]
    return ref.at[j]


SEM_SPEC = pl.BlockSpec(memory_space=pltpu.SEMAPHORE)
ANY_SPEC = pl.BlockSpec(memory_space=pl.ANY)
SPLIT = pltpu.CompilerParams(has_side_effects=pltpu.SideEffectType.DATAFLOW_SIDE_EFFECTING)


def _exchange(kind, name, bufs, build, n_sems, after=None, sems=None):
    n = len(bufs)
    if kind == 'sync':
        def body(*refs):
            mine, theirs = build(refs[n:2 * n], refs[2 * n], refs[2 * n + 1])
            for cp in mine:
                cp.start()
            for cp in theirs:
                cp.wait_recv()
            for cp in mine:
                cp.wait_send()

        return list(_pcall(
            body, in_specs=[HBM_SPEC] * n, out_specs=[HBM_SPEC] * n,
            out_shape=[jax.ShapeDtypeStruct(v.shape, v.dtype) for v in bufs], input_output_aliases={t: t for t in range(n)},
            scratch_shapes=[pltpu.SemaphoreType.DMA((n_sems,)), pltpu.SemaphoreType.DMA((n_sems,))],
            name=name, compiler_params=COMM)(*bufs))
    if kind == 'start':
        def body(*refs):
            mine, _ = build(refs[n + 3:2 * n + 3], refs[n + 1], refs[n + 2])
            for cp in mine:
                cp.start()
            refs[2 * n + 3][...] = jnp.zeros_like(refs[2 * n + 3])

        outs = _pcall(
            body, in_specs=[HBM_SPEC] * n + [ANY_SPEC],
            out_specs=[SEM_SPEC, SEM_SPEC] + [HBM_SPEC] * n + [pl.BlockSpec(memory_space=pltpu.VMEM)],
            out_shape=[pltpu.SemaphoreType.DMA((n_sems,)), pltpu.SemaphoreType.DMA((n_sems,))]
            + [pltpu.HBM(v.shape, v.dtype) for v in bufs] + [jax.ShapeDtypeStruct((8, LANES), F32)],
            input_output_aliases={t: 2 + t for t in range(n)}, name=name,
            compiler_params=SPLIT)(*[pltpu.with_memory_space_constraint(v, pltpu.HBM) for v in bufs], after)
        return (outs[0], outs[1]), list(outs[2:2 + n]), outs[2 + n]

    def body(*refs):
        mine, theirs = build(refs[:n], refs[n], refs[n + 1])
        for cp in mine:
            cp.wait_send()
        for cp in theirs:
            cp.wait_recv()

    return list(_pcall(
        body, in_specs=[HBM_SPEC] * n + [SEM_SPEC, SEM_SPEC, ANY_SPEC], out_specs=[HBM_SPEC] * n,
        out_shape=[pltpu.HBM(v.shape, v.dtype) for v in bufs], input_output_aliases={t: t for t in range(n)},
        name=name, compiler_params=SPLIT)(*bufs, sems[0], sems[1], after))


def _gather_ici_copies(axes, shard_cols):
    def build(bufs, send_sems, recv_sems):
        x, y, c, chips = _position()
        me = 2 * x + y
        mine, theirs = [], []
        for t, ax in enumerate(axes):
            own = _piece(bufs[t], ax, me, c, shard_cols[t])
            for p, (px, py) in enumerate(chips):
                k = t * 3 + p
                got = _piece(bufs[t], ax, 2 * px + py, c, shard_cols[t])
                mine.append(_remote(own, own, send_sems.at[k], recv_sems.at[k], (px, py, c)))
                theirs.append(_remote(got, got, send_sems.at[k], recv_sems.at[k], (px, py, c)))
        return mine, theirs
    return build


def _gather_d2d_copies(axes, shard_cols):
    def build(bufs, send_sems, recv_sems):
        x, y, c, chips = _position()
        mine, theirs = [], []
        for t, ax in enumerate(axes):
            for p, (px, py) in enumerate(chips):
                k = t * 3 + p
                had = _piece(bufs[t], ax, 2 * px + py, c, shard_cols[t])
                got = _piece(bufs[t], ax, 2 * px + py, 1 - c, shard_cols[t])
                mine.append(_remote(had, had, send_sems.at[k], recv_sems.at[k], (x, y, 1 - c)))
                theirs.append(_remote(got, got, send_sems.at[k], recv_sems.at[k], (x, y, 1 - c)))
        return mine, theirs
    return build


def _grads_d2d_copies(n):
    def build(bufs, send_sems, recv_sems):
        x, y, c, _ = _position()
        mine = [_remote(bufs[t].at[:, 1 - c], bufs[n + t], send_sems.at[t], recv_sems.at[t], (x, y, 1 - c)) for t in range(n)]
        return mine, mine
    return build


def _grads_ici_copies(axes):
    n = len(axes)

    def build(bufs, send_sems, recv_sems):
        x, y, c, chips = _position()
        me = 2 * x + y

        def block(t, j):
            if axes[t] == 2:
                cs = bufs[n + t].shape[2]
                return bufs[t].at[0, :, pl.ds(pl.multiple_of(j * cs, cs), cs)]
            return bufs[t].at[j]

        mine, theirs = [], []
        for t in range(n):
            for p, (px, py) in enumerate(chips):
                k = t * 3 + p
                peer = 2 * px + py
                mine.append(_remote(block(t, peer), bufs[n + t].at[me], send_sems.at[k], recv_sems.at[k], (px, py, c)))
                theirs.append(_remote(block(t, peer), bufs[n + t].at[peer], send_sems.at[k], recv_sems.at[k], (px, py, c)))
        return mine, theirs
    return build


def _join_copies(bufs, send_sems, recv_sems):
    x, y, c, _ = _position()
    mine = [_remote(b.at[c], b.at[c], send_sems.at[t], recv_sems.at[t], (x, y, 1 - c)) for t, b in enumerate(bufs)]
    theirs = [_remote(b.at[1 - c], b.at[1 - c], send_sems.at[t], recv_sems.at[t], (x, y, 1 - c)) for t, b in enumerate(bufs)]
    return mine, theirs


def _grads_recv_shape(sm, ax):
    _, a, c = sm.shape
    return (N_CHIPS, a, c // N_CHIPS if ax == 2 else c)


def _gather_small(shard):
    nl, r, cs = shard.shape

    def body(in_ref, out_ref, send_sems, recv_sems, local_sem):
        x, y, c, chips = _position()

        def cols(j):
            return out_ref.at[:, :, pl.ds(pl.multiple_of(j * cs, cs), cs)]

        me = 2 * x + y
        loc = pltpu.make_async_copy(in_ref, cols(me), local_sem)
        loc.start()
        remote = [_remote(in_ref, cols(me), send_sems.at[p], recv_sems.at[p], (px, py, c)) for p, (px, py) in enumerate(chips)]
        for cp in remote:
            cp.start()
        for p, (px, py) in enumerate(chips):
            _remote(in_ref, cols(2 * px + py), send_sems.at[p], recv_sems.at[p], (px, py, c)).wait_recv()
        for cp in remote:
            cp.wait_send()
        loc.wait()

    return _pcall(
        body, in_specs=[HBM_SPEC], out_specs=HBM_SPEC, out_shape=jax.ShapeDtypeStruct((nl, r, cs * N_CHIPS), shard.dtype),
        scratch_shapes=[pltpu.SemaphoreType.DMA((3,)), pltpu.SemaphoreType.DMA((3,)), pltpu.SemaphoreType.DMA(())],
        name="gather_small", compiler_params=COMM)(shard)


def _small_copies(bufs, send_sems, recv_sems):
    pack, land = bufs
    x, y, cc, _ = _position()
    me = 4 * x + 2 * y + cc
    mine, theirs = [], []
    for k in range(1, N_DEV):
        px, py, pc = x ^ ((k >> 2) & 1), y ^ ((k >> 1) & 1), cc ^ (k & 1)
        mine.append(_remote(pack, land.at[me], send_sems.at[k - 1], recv_sems.at[k - 1], (px, py, pc)))
        theirs.append(_remote(pack, land.at[4 * px + 2 * py + pc], send_sems.at[k - 1], recv_sems.at[k - 1], (px, py, pc)))
    return mine, theirs


def _sum_devices(land, pack, dev):
    _, r, c = land.shape
    tr = _tile(r, (672, 512, 256, 128, 64, 8))

    def body(dev_ref, l_ref, p_ref, o_ref):
        me = dev_ref[0]
        acc = None
        for k in range(N_DEV):
            term = jnp.where(me == k, p_ref[...], l_ref[k])
            acc = term if acc is None else acc + term
        o_ref[...] = acc

    grid_spec = pltpu.PrefetchScalarGridSpec(
        num_scalar_prefetch=1, grid=(r // tr,),
        in_specs=[pl.BlockSpec((N_DEV, tr, c), lambda i, dv: (0, i, 0)), pl.BlockSpec((tr, c), lambda i, dv: (i, 0))],
        out_specs=pl.BlockSpec((tr, c), lambda i, dv: (i, 0)))
    return _pcall(body, grid_spec=grid_spec, out_shape=jax.ShapeDtypeStruct((r, c), F32), name="sum_devices",
                  compiler_params=_params(("parallel",)))(dev, land, pack)


def _dims(d):
    half = d // 2
    return half // HEAD_SB, half // HEAD_XA, 3, (5 * half) // HEAD_XA


def _layer_fwd(x, mem, weight, small, l, after=None):
    h_sb, h_xa, u_blk, q_blk = _dims(x.shape[1])

    def vec(name):
        return small[name][l].reshape(1, -1)

    def use(a, name, follows, mm_name, dtype=F32):
        wt = weight(name, follows)
        return _mm(a, wt, 'nn', dtype, mm_name, after=weight.token())

    h1 = _norm_fwd(x, vec('g_mix_pre'), None, BF16, "norm_mix_pre", after)
    proj = use(h1, 'w_in', h1, "mm_proj")
    o_sb, a_sb = _sb_fwd(proj, h_sb)
    b_st = small['b_s'][l].T
    o_gm = _gm_fwd(proj, vec('g_vnorm'), small['w_s'][l], b_st, u_blk)
    memn = _norm_fwd(mem, vec('g_mem'), None, BF16, "norm_mem")
    mem_kv = use(memn, 'w_mem_kv', o_sb, "mm_mem_kv")
    o_xa = _xa_fwd(proj, mem_kv, q_blk, h_xa)
    zg = use(h1, 'w_gate', o_sb, "mm_gate", BF16)
    branches = [use(o, wn, zg, "mm_branch") for o, wn in ((o_sb, 'w_br_sb'), (o_gm, 'w_br_gm'), (o_xa, 'w_br_xa'))]
    merged = _merge_fwd(zg, vec('b_gate'), branches)
    y1 = use(merged, 'w_out', zg, "mm_out")
    x1 = _norm_fwd(y1, vec('g_mix_post'), x, F32, "norm_mix_post")
    h2 = _norm_fwd(x1, vec('g_ffn_pre'), None, BF16, "norm_ffn_pre")
    up = use(h2, 'w_up', h2, "mm_up", BF16)
    act = _cg_fwd(up, weight('conv_w', up), vec('conv_b'))
    y2 = use(act, 'w_down', act, "mm_down")
    x2 = _norm_fwd(y2, vec('g_ffn_post'), x1, F32, "norm_ffn_post")
    saved = dict(x0=x, h1=h1, proj=proj, o_sb=o_sb, a_sb=a_sb, o_gm=o_gm, o_xa=o_xa, memn=memn, mem_kv=mem_kv, zg=zg,
                 branches=branches, merged=merged, y1=y1, x1=x1, h2=h2, up=up, act=act, y2=y2, b_st=b_st)
    return x2, saved


def _layer_bwd(dx, mem, sv, full, small, l, after, emit):
    h_sb, h_xa, u_blk, q_blk = _dims(dx.shape[1])

    def vec(name):
        return small[name][l].reshape(1, -1)

    gb, gs = {}, {}
    dy2, gs['g_ffn_post'] = _norm_bwd(sv['y2'], vec('g_ffn_post'), [dx], None, BF16, "norm_ffn_post_bwd", after)
    gb['w_down'] = _mm(sv['act'], dy2, 'tn', BF16, "mm_down_dw")
    dact = _mm(dy2, full['w_down'], 'nt', F32, "mm_down_dx")
    dgate, dval, gs['conv_w'], gs['conv_b'] = _cg_bwd(sv['up'], full['conv_w'], vec('conv_b'), dact)
    dup = jnp.concatenate([dgate, dval], axis=1)
    gb['w_up'] = _mm(sv['h2'], dup, 'tn', BF16, "mm_up_dw")
    token = emit(0, gb)
    dh2 = _mm(dup, full['w_up'], 'nt', F32, "mm_up_dx", after=token)
    token = emit.flush(dh2)
    dx1, gs['g_ffn_pre'] = _norm_bwd(sv['x1'], vec('g_ffn_pre'), [dh2], dx, F32, "norm_ffn_pre_bwd", token)
    dy1, gs['g_mix_post'] = _norm_bwd(sv['y1'], vec('g_mix_post'), [dx1], None, BF16, "norm_mix_post_bwd")
    gb['w_out'] = _mm(sv['merged'], dy1, 'tn', BF16, "mm_out_dw")
    dmerged = _mm(dy1, full['w_out'], 'nt', F32, "mm_out_dx")
    dzg, dbr, gs['b_gate'] = _merge_bwd(sv['zg'], vec('b_gate'), sv['branches'], dmerged)
    douts = []
    for o, db, wn in ((sv['o_sb'], dbr[0], 'w_br_sb'), (sv['o_gm'], dbr[1], 'w_br_gm'), (sv['o_xa'], dbr[2], 'w_br_xa')):
        gb[wn] = _mm(o, db, 'tn', BF16, "mm_branch_dw")
        douts.append(_mm(db, full[wn], 'nt', F32, "mm_branch_dx"))
    gb['w_gate'] = _mm(sv['h1'], dzg, 'tn', BF16, "mm_gate_dw")
    dq_xa, dk_xa, dv_xa = _xa_bwd(sv['proj'], sv['mem_kv'], douts[2], q_blk, h_xa)
    dmem_kv = jnp.concatenate([dk_xa, dv_xa], axis=1).astype(BF16)
    gb['w_mem_kv'] = _mm(sv['memn'], dmem_kv, 'tn', BF16, "mm_mem_kv_dw")
    token = emit(1, gb)
    dh1_gate = _mm(dzg, full['w_gate'], 'nt', F32, "mm_gate_dx", after=token)
    token = emit.flush(dh1_gate)
    dmemn = _mm(dmem_kv, full['w_mem_kv'], 'nt', F32, "mm_mem_kv_dx")
    _, gs['g_mem'] = _norm_bwd(mem, vec('g_mem'), [dmemn], None, BF16, "norm_mem_bwd")
    du, dv, gs['g_vnorm'], gs['w_s'], db_st = _gm_bwd(sv['proj'], vec('g_vnorm'), small['w_s'][l], sv['b_st'], douts[1], u_blk)
    gs['b_s'] = db_st.T
    dq, dk, dvv = _sb_bwd(sv['proj'], sv['a_sb'], douts[0], h_sb, token)
    dproj = jnp.concatenate([dq, dk, dvv, du, dv, dq_xa], axis=1).astype(BF16)
    gb['w_in'] = _mm(sv['h1'], dproj, 'tn', BF16, "mm_proj_dw")
    token = emit(2, gb)
    dh1_proj = _mm(dproj, full['w_in'], 'nt', F32, "mm_proj_dx")
    dx0, gs['g_mix_pre'] = _norm_bwd(sv['x0'], vec('g_mix_pre'), [dh1_gate, dh1_proj], dx1, F32, "norm_mix_pre_bwd", token)
    return dx0, gs


class _Given:
    def __init__(self, full):
        self.full = full

    def __call__(self, name, follows):
        return self.full[name]

    def token(self):
        return None


def _local_step(x, mem, target, full, small):
    n_layers = len(full['w_in'])
    saved = []
    for l in range(n_layers):
        x, sv = _layer_fwd(x, mem, _Given({n: full[n][l] for n in full}), small, l)
        saved.append(sv)
    sq, dx = _loss_head(x, target)
    gbig = {n: [None] * n_layers for n in BIG}
    gsmall = {n: [None] * n_layers for n in SMALL + ['conv_w']}
    class Collect:
        def __init__(self, l):
            self.l = l

        def __call__(self, g, gb):
            for n in BWD_GROUPS[g]:
                gbig[n][self.l] = gb[n]

        def flush(self, follows):
            return None

    for l in reversed(range(n_layers)):
        dx, gs = _layer_bwd(dx, mem, saved[l], {n: full[n][l] for n in full}, small, l, None, Collect(l))
        for n in gs:
            gsmall[n][l] = gs[n]
    return sq, dx, gbig, gsmall


def _pack(arrays, rows_multiple):
    flat = jnp.concatenate([a.reshape(-1).astype(F32) for a in arrays])
    rows = -(-flat.shape[0] // LANES)
    rows = -(-rows // rows_multiple) * rows_multiple
    return jnp.pad(flat, (0, rows * LANES - flat.shape[0])).reshape(rows, LANES)


def _unpack(pack, like):
    flat = pack.reshape(-1)
    out, off = [], 0
    for a in like:
        out.append(flat[off:off + a.size].reshape(a.shape))
        off += a.size
    return out


def _grad_view(g, ax):
    r, c = g.shape
    return g.reshape(1, 2, r // 2, c) if ax == 2 else g.reshape(N_CHIPS, 2, r // (2 * N_CHIPS), c)


def kernel(x, mem, g_mix_pre, w_in, g_vnorm, w_s, b_s, g_mem, w_mem_kv, w_gate, b_gate, w_br_sb, w_br_gm, w_br_xa, w_out, g_mix_post, g_ffn_pre, w_up, conv_w, conv_b, w_down, g_ffn_post, loss_target, m_g_mix_pre, m_w_in, m_g_vnorm, m_w_s, m_b_s, m_g_mem, m_w_mem_kv, m_w_gate, m_b_gate, m_w_br_sb, m_w_br_gm, m_w_br_xa, m_w_out, m_g_mix_post, m_g_ffn_pre, m_w_up, m_conv_w, m_conv_b, m_w_down, m_g_ffn_post, v_g_mix_pre, v_w_in, v_g_vnorm, v_w_s, v_b_s, v_g_mem, v_w_mem_kv, v_w_gate, v_b_gate, v_w_br_sb, v_w_br_gm, v_w_br_xa, v_w_out, v_g_mix_post, v_g_ffn_pre, v_w_up, v_conv_w, v_conv_b, v_w_down, v_g_ffn_post):
    w = dict(g_mix_pre=g_mix_pre, w_in=w_in, g_vnorm=g_vnorm, w_s=w_s, b_s=b_s, g_mem=g_mem, w_mem_kv=w_mem_kv,
             w_gate=w_gate, b_gate=b_gate, w_br_sb=w_br_sb, w_br_gm=w_br_gm, w_br_xa=w_br_xa, w_out=w_out,
             g_mix_post=g_mix_post, g_ffn_pre=g_ffn_pre, w_up=w_up, conv_w=conv_w, conv_b=conv_b, w_down=w_down,
             g_ffn_post=g_ffn_post)
    m = dict(g_mix_pre=m_g_mix_pre, w_in=m_w_in, g_vnorm=m_g_vnorm, w_s=m_w_s, b_s=m_b_s, g_mem=m_g_mem,
             w_mem_kv=m_w_mem_kv, w_gate=m_w_gate, b_gate=m_b_gate, w_br_sb=m_w_br_sb, w_br_gm=m_w_br_gm,
             w_br_xa=m_w_br_xa, w_out=m_w_out, g_mix_post=m_g_mix_post, g_ffn_pre=m_g_ffn_pre, w_up=m_w_up,
             conv_w=m_conv_w, conv_b=m_conv_b, w_down=m_w_down, g_ffn_post=m_g_ffn_post)
    v = dict(g_mix_pre=v_g_mix_pre, w_in=v_w_in, g_vnorm=v_g_vnorm, w_s=v_w_s, b_s=v_b_s, g_mem=v_g_mem,
             w_mem_kv=v_w_mem_kv, w_gate=v_w_gate, b_gate=v_b_gate, w_br_sb=v_w_br_sb, w_br_gm=v_w_br_gm,
             w_br_xa=v_w_br_xa, w_out=v_w_out, g_mix_post=v_g_mix_post, g_ffn_pre=v_g_ffn_pre, w_up=v_w_up,
             conv_w=v_conv_w, conv_b=v_conv_b, w_down=v_w_down, g_ffn_post=v_g_ffn_post)
    n_layers = w_in.shape[0]
    d = x.shape[-1]
    core = lax.axis_index("c").astype(jnp.int32).reshape(1)
    chip = (2 * lax.axis_index("x") + lax.axis_index("y")).astype(jnp.int32).reshape(1)
    small = {n: w[n] for n in SMALL}
    xs, mems, target = x[0], mem[0], loss_target[0]

    conv_w_full = _gather_small(conv_w)

    def as_full(vw, ax):
        return vw.reshape(-1, vw.shape[-1]) if ax == 1 else vw.reshape(vw.shape[0] * vw.shape[1], vw.shape[2])

    stages = {}
    keys = [(l, g) for l in range(n_layers) for g in range(len(FWD_GROUPS))]

    def start_gathers(l, token):
        for g, names in enumerate(FWD_GROUPS):
            ax_g = [BIG_AXIS[n] for n in names]
            cols_g = [w[n].shape[2] for n in names]
            views = [_place_own(w[n], l, chip, ax, token) for n, ax in zip(names, ax_g)]
            sems, views, token = _exchange('start', f"gather_ici_start_{l}_{g}", views, _gather_ici_copies(ax_g, cols_g),
                                           3 * len(names), after=token)
            stages[l, g] = dict(names=names, ax=ax_g, cols=cols_g, views=views, ici=sems, d2d=None, full=None)
        return token

    token = start_gathers(0, conv_w_full)

    def cross_cores(key, follows):
        st, (l, g) = stages[key], key
        n3 = 3 * len(st['names'])
        views = _exchange('wait', f"gather_ici_wait_{l}_{g}", st['views'], _gather_ici_copies(st['ax'], st['cols']), n3,
                          after=follows, sems=st['ici'])
        st['d2d'], st['views'], tok = _exchange('start', f"gather_d2d_start_{l}_{g}", views,
                                                _gather_d2d_copies(st['ax'], st['cols']), n3, after=core)
        return tok

    class Weights:
        def __init__(self, l):
            self.l, self.tok = l, None

        def __call__(self, name, follows):
            if name == 'conv_w':
                return conv_w_full[self.l]
            key = (self.l, [g for g, names in enumerate(FWD_GROUPS) if name in names][0])
            if key[1] == 1 and (self.l + 1, 0) not in stages and self.l + 1 < n_layers:
                self.tok = start_gathers(self.l + 1, follows)
            st = stages[key]
            if st['full'] is None:
                if st['d2d'] is None:
                    cross_cores(key, follows)
                views = _exchange('wait', f"gather_d2d_wait_{key[0]}_{key[1]}", st['views'],
                                  _gather_d2d_copies(st['ax'], st['cols']), 3 * len(st['names']), after=follows, sems=st['d2d'])
                st['full'] = {n: as_full(vw, ax) for n, vw, ax in zip(st['names'], views, st['ax'])}
            nxt = keys.index(key) + 1
            if name == st['names'][-1] and nxt < len(keys) and key != keys[0] and stages[keys[nxt]]['d2d'] is None:
                self.tok = cross_cores(keys[nxt], follows)
            return st['full'][name]

        def token(self):
            return self.tok

    fulls, saved = [], []
    for l in range(n_layers):
        xs, sv = _layer_fwd(xs, mems, Weights(l), small, l, token if l == 0 else None)
        fulls.append({n: stages[l, g]['full'][n] for g, names in enumerate(FWD_GROUPS) for n in names} | {'conv_w': conv_w_full[l]})
        saved.append(sv)
    sq, dx = _loss_head(xs, target)
    loss = lax.psum(0.5 * jnp.sum(sq) / d, ("x", "y", "c"))

    sent = []

    def to_chips(l, g, names, ax_g, bufs, after):
        n = len(names)
        sums = [_sum_halves(dv, th, core, "sum_halves") for dv, th in zip(bufs[:n], bufs[n:])]
        lands = [lax.empty(_grads_recv_shape(sm, ax), sm.dtype) for sm, ax in zip(sums, ax_g)]
        sems, bufs, tok = _exchange('start', f"grads_ici_start_{l}_{g}", sums + lands, _grads_ici_copies(ax_g), 3 * n,
                                    after=core if after is None else after)
        sent.append((l, g, names, ax_g, bufs, sems))
        return tok

    class Grads:
        def __init__(self, l):
            self.l, self.crossing = l, None

        def __call__(self, g, gb):
            names = BWD_GROUPS[g]
            ax_g = [BIG_AXIS[n] for n in names]
            n = len(names)
            dwvs = [_grad_view(gb[nm], ax) for nm, ax in zip(names, ax_g)]
            lands = [lax.empty((dv.shape[0],) + dv.shape[2:], dv.dtype) for dv in dwvs]
            if g + 1 < len(BWD_GROUPS):
                sems, bufs, tok = _exchange('start', f"grads_d2d_start_{self.l}_{g}", dwvs + lands, _grads_d2d_copies(n), n,
                                            after=core)
                self.crossing = (g, names, ax_g, bufs, sems)
                return tok
            return to_chips(self.l, g, names, ax_g, _exchange('sync', "grads_d2d", dwvs + lands, _grads_d2d_copies(n), n), None)

        def flush(self, follows):
            if self.crossing is None:
                return None
            (g, names, ax_g, bufs, sems), self.crossing = self.crossing, None
            bufs = _exchange('wait', f"grads_d2d_wait_{self.l}_{g}", bufs, _grads_d2d_copies(len(names)), len(names),
                             after=follows, sems=sems)
            return to_chips(self.l, g, names, ax_g, bufs, None)

    gsmall = {n: [None] * n_layers for n in SMALL + ['conv_w']}
    for l in reversed(range(n_layers)):
        dx, gs = _layer_bwd(dx, mems, saved[l], fulls[l], small, l, None, Grads(l))
        for n in gs:
            gsmall[n][l] = gs[n]

    names_small = SMALL + ['conv_w']
    small_full = [jnp.stack(gsmall[n]).reshape(w[n].shape) for n in SMALL]
    conv_w_grad = jnp.stack(gsmall['conv_w'])
    pack = _pack(small_full + [conv_w_grad], 8)
    small_sems, small_bufs, _ = _exchange('start', "small_start", [pack, lax.empty((N_DEV,) + pack.shape, F32)], _small_copies,
                                          N_DEV - 1, after=dx)

    def small_update(follows):
        pk, land = _exchange('wait', "small_wait", small_bufs, _small_copies, N_DEV - 1, after=follows, sems=small_sems)
        device = (4 * lax.axis_index("x") + 2 * lax.axis_index("y") + lax.axis_index("c")).astype(jnp.int32).reshape(1)
        summed = _sum_devices(land, pk, device)
        *small_g, conv_w_g = _unpack(summed, small_full + [conv_w_grad])
        shard = conv_w.shape[-1]
        conv_w_g = lax.dynamic_slice_in_dim(conv_w_g, chip[0] * shard, shard, axis=2)
        packed = [_pack([p[n] for n in names_small], 256) for p in (w, m, v)]
        gpack = _pack(small_g + [conv_w_g], 256)
        res = _adamw(packed[0][None], packed[1][None], packed[2][None], gpack, 0, None, "adamw_small")
        like = [w[n] for n in names_small]
        unpacked = [_unpack(r[0], like) for r in res]
        return {n: tuple(u[i] for u in unpacked) for i, n in enumerate(names_small)}

    out = {}

    def update(joining, follows):
        l, g, names, halves, sems = joining
        halves = _exchange('wait', f"join_wait_{l}_{g}", halves, _join_copies, len(names), after=follows, sems=sems)
        last = None
        for n, hv in zip(names, halves):
            out[n] = _adamw(w[n], m[n], v[n], hv.reshape(w[n].shape[1:]), l, out.get(n), "adamw_big", last)
            last = out[n][0]
        return last

    joining, follows = None, dx
    for k, (l, g, names, ax_g, bufs, sems) in enumerate(sent):
        n = len(names)
        bufs = _exchange('wait', f"grads_ici_wait_{l}_{g}", bufs, _grads_ici_copies(ax_g), 3 * n, after=follows, sems=sems)
        halves = [_sum_chips(r, sm, chip, core, ax) for sm, r, ax in zip(bufs[:n], bufs[n:], ax_g)]
        jsems, halves, tok = _exchange('start', f"join_start_{l}_{g}", halves, _join_copies, n, after=core)
        if joining is not None:
            follows = update(joining, tok)
        joining = (l, g, names, halves, jsems)
    out.update(small_update(update(joining, follows)))

    return (loss, dx[None], *[out[n][0] for n in WEIGHTS], *[out[n][1] for n in WEIGHTS],
            *[out[n][2] for n in WEIGHTS], *[out[n][3] for n in WEIGHTS])
```

```python
import functools
import math

import jax
import jax.numpy as jnp
from jax import lax
from jax.experimental import pallas as pl
from jax.experimental.pallas import tpu as pltpu

F32 = jnp.float32
BF16 = jnp.bfloat16
EPS = 1e-6
HEAD_SB = 128
GROUP_GM = 128
CHUNK = 64
HEAD_XA = 256
CONV_TAPS = 3
N_CHIPS = 4
N_DEV = 8
LANES = 128
MIB = 1024 * 1024
VMEM_LIMIT = 48 * MIB
SPLITS = 1

ADAM_LR = 0.001
ADAM_B1 = 0.9
ADAM_B2 = 0.999
ADAM_EPS = 1e-08
ADAM_WD = 0.01
ADAM_STEP = 10

WEIGHTS = ['g_mix_pre', 'w_in', 'g_vnorm', 'w_s', 'b_s', 'g_mem', 'w_mem_kv', 'w_gate', 'b_gate', 'w_br_sb',
           'w_br_gm', 'w_br_xa', 'w_out', 'g_mix_post', 'g_ffn_pre', 'w_up', 'conv_w', 'conv_b', 'w_down',
           'g_ffn_post']
BIG_AXIS = {'w_in': 2, 'w_mem_kv': 1, 'w_gate': 2, 'w_br_sb': 2, 'w_br_gm': 2, 'w_br_xa': 2, 'w_out': 1,
            'w_up': 2, 'w_down': 1}
BIG = list(BIG_AXIS)
FWD_GROUPS = [['w_in'], ['w_mem_kv', 'w_gate'], ['w_br_sb', 'w_br_gm', 'w_br_xa', 'w_out'], ['w_up'], ['w_down']]
BWD_GROUPS = [['w_down', 'w_up'], ['w_out', 'w_br_sb', 'w_br_gm', 'w_br_xa', 'w_gate', 'w_mem_kv'], ['w_in']]
SMALL = ['g_mix_pre', 'g_vnorm', 'w_s', 'b_s', 'g_mem', 'b_gate', 'g_mix_post', 'g_ffn_pre', 'conv_b', 'g_ffn_post']
MESH = pl.DeviceIdType.MESH


def _pcall(body, **kw):
    return pl.pallas_call(body, **kw)


def _params(sem=None, vmem=VMEM_LIMIT):
    return pltpu.CompilerParams(dimension_semantics=sem, vmem_limit_bytes=vmem)


def _tile(n, cands):
    for c in cands:
        if n % c == 0:
            return c
    return n


_GELU_C = math.sqrt(2.0 / math.pi)
_GELU_A = 0.044715


def _gelu(x):
    return 0.5 * x * (1.0 + jnp.tanh(_GELU_C * (x + _GELU_A * (x * x * x))))


def _gelu_and_grad(x):
    x2 = x * x
    t = jnp.tanh(_GELU_C * (x + _GELU_A * (x2 * x)))
    val = 0.5 * x * (1.0 + t)
    grad = 0.5 * (1.0 + t) + 0.5 * x * (1.0 - t * t) * (_GELU_C * (1.0 + 3.0 * _GELU_A * x2))
    return val, grad


def _softplus(z):
    return jnp.maximum(z, 0.0) + jnp.log(1.0 + jnp.exp(-jnp.abs(z)))


def _dot(a, b):
    return jnp.dot(a, b, preferred_element_type=F32)


def _dot_nt(a, b):
    return lax.dot_general(a, b, (((1,), (1,)), ((), ())), preferred_element_type=F32)


def _dot_tn(a, b):
    return lax.dot_general(a, b, (((0,), (0,)), ((), ())), preferred_element_type=F32)


def _split_dot(a, m):
    out = None
    rest = a
    for _ in range(SPLITS):
        piece = rest.astype(BF16)
        rest = rest - piece.astype(F32)
        term = _dot(piece, m)
        out = term if out is None else out + term
    return out


def _mm(a, b, mode, out_dtype, name, tm=None, tn=None, tk=None, after=None):
    if mode == 'nn':
        (m, kc), (kc2, n) = a.shape, b.shape
    elif mode == 'nt':
        (m, kc), (n, kc2) = a.shape, b.shape
    else:
        (kc, m), (kc2, n) = a.shape, b.shape
    assert kc == kc2, (a.shape, b.shape, mode)
    tm = tm or _tile(m, (1024, 512, 256, 128))
    tn = tn or _tile(n, (1024, 512, 256, 128))
    tk = tk or (kc if kc <= 3072 else _tile(kc, (3072, 2816, 2048, 1536, 1408, 1024, 512)))
    nk = kc // tk
    dot = {'nn': _dot, 'nt': _dot_nt, 'tn': _dot_tn}[mode]
    a_spec = pl.BlockSpec((tk, tm), lambda i, j, k: (k, i)) if mode == 'tn' else pl.BlockSpec((tm, tk), lambda i, j, k: (i, k))
    b_spec = pl.BlockSpec((tn, tk), lambda i, j, k: (j, k)) if mode == 'nt' else pl.BlockSpec((tk, tn), lambda i, j, k: (k, j))

    extra = [] if after is None else [after]
    extra_specs = [pl.BlockSpec(memory_space=pl.ANY)] * len(extra)

    if nk == 1:
        def body(a_ref, b_ref, *rest):
            o_ref = rest[-1]
            o_ref[...] = dot(a_ref[...].astype(BF16), b_ref[...].astype(BF16)).astype(o_ref.dtype)
        scratch = []
    else:
        def body(a_ref, b_ref, *rest):
            o_ref, acc_ref = rest[-2], rest[-1]
            k = pl.program_id(2)
            part = dot(a_ref[...].astype(BF16), b_ref[...].astype(BF16))

            @pl.when(k == 0)
            def _():
                acc_ref[...] = part

            @pl.when(k > 0)
            def _():
                acc_ref[...] += part

            @pl.when(k == nk - 1)
            def _():
                o_ref[...] = acc_ref[...].astype(o_ref.dtype)
        scratch = [pltpu.VMEM((tm, tn), F32)]

    return _pcall(
        body, grid=(m // tm, n // tn, nk), in_specs=[a_spec, b_spec] + extra_specs,
        out_specs=pl.BlockSpec((tm, tn), lambda i, j, k: (i, j)),
        out_shape=jax.ShapeDtypeStruct((m, n), out_dtype), scratch_shapes=scratch, name=name,
        compiler_params=_params(("parallel", "parallel", "arbitrary")))(a, b, *extra)


def _norm_fwd(x, g, res, out_dtype, name, after=None):
    s, d = x.shape
    tr = _tile(s, (256, 128))
    has_res = res is not None
    has_after = after is not None

    def body(*refs):
        x_ref, g_ref = refs[0], refs[1]
        o_ref = refs[-1]
        xv = x_ref[...]
        y = xv * lax.rsqrt(jnp.mean(xv * xv, axis=-1, keepdims=True) + EPS) * g_ref[...]
        if has_res:
            y = y + refs[2][...]
        o_ref[...] = y.astype(o_ref.dtype)

    row = pl.BlockSpec((tr, d), lambda i: (i, 0))
    ins = [x, g] + ([res] if has_res else []) + ([after] if has_after else [])
    return _pcall(
        body, grid=(s // tr,),
        in_specs=[row, pl.BlockSpec((1, d), lambda i: (0, 0))] + ([row] if has_res else [])
        + ([pl.BlockSpec(memory_space=pl.ANY)] if has_after else []),
        out_specs=row, out_shape=jax.ShapeDtypeStruct((s, d), out_dtype), name=name,
        compiler_params=_params(("parallel",)))(*ins)


def _norm_bwd(x, g, douts, dres, out_dtype, name, after=None):
    s, d = x.shape
    tr = _tile(s, (256, 128))
    nd = len(douts)
    has_res = dres is not None
    has_after = after is not None

    def body(*refs):
        x_ref, g_ref = refs[0], refs[1]
        dx_ref, dg_ref = refs[-2], refs[-1]
        dout = refs[2][...].astype(F32)
        for r in refs[3:2 + nd]:
            dout = dout + r[...].astype(F32)
        xv = x_ref[...]
        r = lax.rsqrt(jnp.mean(xv * xv, axis=-1, keepdims=True) + EPS)
        n = xv * r
        dn = dout * g_ref[...]
        dx = r * (dn - n * jnp.mean(dn * n, axis=-1, keepdims=True))
        if has_res:
            dx = dx + refs[2 + nd][...]
        dx_ref[...] = dx.astype(dx_ref.dtype)

        @pl.when(pl.program_id(0) == 0)
        def _():
            dg_ref[...] = jnp.zeros_like(dg_ref)

        dg_ref[...] += jnp.sum(dout * n, axis=0, keepdims=True)

    row = pl.BlockSpec((tr, d), lambda i: (i, 0))
    vec = pl.BlockSpec((1, d), lambda i: (0, 0))
    ins = [x, g] + list(douts) + ([dres] if has_res else []) + ([after] if has_after else [])
    return _pcall(
        body, grid=(s // tr,),
        in_specs=[row, vec] + [row] * (nd + int(has_res)) + ([pl.BlockSpec(memory_space=pl.ANY)] if has_after else []),
        out_specs=[row, vec],
        out_shape=[jax.ShapeDtypeStruct((s, d), out_dtype), jax.ShapeDtypeStruct((1, d), F32)], name=name,
        compiler_params=_params(("arbitrary",)))(*ins)


def _loss_head(y, target):
    s, d = y.shape
    tr = _tile(s, (256, 128))

    def body(y_ref, t_ref, sq_ref, dy_ref):
        e = y_ref[...] - t_ref[...]
        dy_ref[...] = e * (1.0 / d)

        @pl.when(pl.program_id(0) == 0)
        def _():
            sq_ref[...] = jnp.zeros_like(sq_ref)

        sq_ref[...] += jnp.sum(e * e, axis=0, keepdims=True)

    row = pl.BlockSpec((tr, d), lambda i: (i, 0))
    return _pcall(
        body, grid=(s // tr,), in_specs=[row, row], out_specs=[pl.BlockSpec((1, d), lambda i: (0, 0)), row],
        out_shape=[jax.ShapeDtypeStruct((1, d), F32), jax.ShapeDtypeStruct((s, d), F32)], name="loss_head",
        compiler_params=_params(("arbitrary",)))(y, target)


NEVER = -1e30
SB_QUERIES = 512


def _sb_sum_matrix(later):
    r = lax.broadcasted_iota(jnp.int32, (HEAD_SB, 2 * HEAD_SB), 0)
    c = lax.broadcasted_iota(jnp.int32, (HEAD_SB, 2 * HEAD_SB), 1)
    tri = jnp.where((r > c) if later else (r < c), 1.0, 0.0)
    return jnp.where(c < HEAD_SB, tri, 1.0).astype(BF16)


def _sb_mask(tq, q0, k0):
    row = lax.broadcasted_iota(jnp.int32, (tq, HEAD_SB), 0)
    col = lax.broadcasted_iota(jnp.int32, (tq, HEAD_SB), 1)
    return (k0 + col) < (q0 + row)


def _sb_fwd(proj, n_heads):
    s = proj.shape[0]
    tq = min(SB_QUERIES, s)
    per = tq // HEAD_SB
    scale = HEAD_SB ** -0.5

    def body(q_ref, k_ref, v_ref, o_ref, a_ref, acc_ref, c_ref):
        i = pl.program_id(1)
        q = q_ref[...].astype(BF16)
        sums = _sb_sum_matrix(True)
        acc_ref[...] = jnp.zeros_like(acc_ref)
        c_ref[...] = jnp.zeros_like(c_ref)
        last = (i + 1) * per - 1

        def scores(j, masked):
            off = pl.multiple_of(j * HEAD_SB, HEAD_SB)
            z = _dot_nt(q, k_ref[pl.ds(off, HEAD_SB), :].astype(BF16)) * scale
            sp = _softplus(z)
            logb = z - sp
            if masked:
                mask = _sb_mask(tq, i * tq, off)
                logb = jnp.where(mask, logb, NEVER)
                sp = jnp.where(mask, sp, 0.0)
            return logb, _split_dot(sp, sums)

        def values(j, logb, both):
            off = pl.multiple_of(j * HEAD_SB, HEAD_SB)
            c = c_ref[...]
            a = jnp.exp(logb - both[:, :HEAD_SB] - c).astype(BF16)
            a_ref[0, 0, j] = a
            acc_ref[...] += _dot(a, v_ref[pl.ds(off, HEAD_SB), :].astype(BF16))
            c_ref[...] = c + both[:, HEAD_SB:]

        def step(jj, carry, masked):
            j = last - jj
            nxt = scores(j, masked)
            values(jnp.minimum(j + 1, last), *carry)
            return nxt

        idle = (jnp.full((tq, HEAD_SB), NEVER, F32), jnp.zeros((tq, 2 * HEAD_SB), F32))
        carry = idle
        for jj in range(per):
            carry = step(jj, carry, True)
        def group(jg, carry):
            for u in range(per):
                carry = step(per * jg + u, carry, False)
            return carry

        carry = lax.fori_loop(1, i + 1, group, carry)
        values(0, *carry)
        o_ref[...] = acc_ref[...].astype(o_ref.dtype)

    h = n_heads
    blk = pl.BlockSpec((tq, HEAD_SB), lambda hh, i: (i, hh))
    return _pcall(
        body, grid=(h, s // tq),
        in_specs=[blk, pl.BlockSpec((s, HEAD_SB), lambda hh, i: (0, h + hh)),
                  pl.BlockSpec((s, HEAD_SB), lambda hh, i: (0, 2 * h + hh))],
        out_specs=[blk, pl.BlockSpec((1, 1, s // HEAD_SB, tq, HEAD_SB), lambda hh, i: (hh, i, 0, 0, 0))],
        out_shape=[jax.ShapeDtypeStruct((s, h * HEAD_SB), BF16),
                   jax.ShapeDtypeStruct((h, s // tq, s // HEAD_SB, tq, HEAD_SB), BF16)],
        scratch_shapes=[pltpu.VMEM((tq, HEAD_SB), F32), pltpu.VMEM((tq, HEAD_SB), F32)],
        name="sb_fwd", compiler_params=_params(("parallel", "arbitrary")))(proj, proj, proj)


def _sb_bwd(proj, a_saved, do, n_heads, after=None):
    s = proj.shape[0]
    tq = min(SB_QUERIES, s)
    per = tq // HEAD_SB
    scale = HEAD_SB ** -0.5
    follow = [] if after is None else [after]

    def body(q_ref, k_ref, v_ref, do_ref, a_ref, *rest):
        dq_ref, dk_ref, dv_ref, run_ref, acc_ref = rest[len(follow):]
        i = pl.program_id(1)

        @pl.when(i == 0)
        def _():
            dk_ref[...] = jnp.zeros_like(dk_ref)
            dv_ref[...] = jnp.zeros_like(dv_ref)

        q = q_ref[...].astype(BF16)
        dob = do_ref[...].astype(BF16)
        run_ref[...] = jnp.zeros_like(run_ref)
        acc_ref[...] = jnp.zeros_like(acc_ref)
        earlier = _sb_sum_matrix(False)
        first_diagonal = i * per

        def step(j, masked):
            off = pl.multiple_of(j * HEAD_SB, HEAD_SB)
            kb = k_ref[pl.ds(off, HEAD_SB), :].astype(BF16)
            vb = v_ref[pl.ds(off, HEAD_SB), :].astype(BF16)
            a = a_ref[0, 0, j]
            g = a.astype(F32) * _dot_nt(dob, vb)
            dv_ref[pl.ds(off, HEAD_SB), :] += _dot_tn(a, dob)
            z = _dot_nt(q, kb) * scale
            beta = 1.0 / (1.0 + jnp.exp(-z))
            both = _split_dot(g, earlier)
            p = run_ref[...]
            dz = (g * (1.0 - beta) - beta * (both[:, :HEAD_SB] + p)) * scale
            if masked:
                dz = jnp.where(_sb_mask(tq, i * tq, off), dz, 0.0)
            dzb = dz.astype(BF16)
            dk_ref[pl.ds(off, HEAD_SB), :] += _dot_tn(dzb, q)
            acc_ref[...] += _dot(dzb, kb)
            run_ref[...] = p + both[:, HEAD_SB:]

        def group(jg, carry):
            for u in range(per):
                step(per * jg + u, False)
            return carry

        lax.fori_loop(0, i, group, 0)
        for u in range(per):
            step(first_diagonal + u, True)
        dq_ref[...] = acc_ref[...]

    h = n_heads
    blk = pl.BlockSpec((tq, HEAD_SB), lambda hh, i: (i, hh))
    col_blk = pl.BlockSpec((s, HEAD_SB), lambda hh, i: (0, hh))
    shape = jax.ShapeDtypeStruct((s, h * HEAD_SB), F32)
    return _pcall(
        body, grid=(h, s // tq),
        in_specs=[blk, pl.BlockSpec((s, HEAD_SB), lambda hh, i: (0, h + hh)),
                  pl.BlockSpec((s, HEAD_SB), lambda hh, i: (0, 2 * h + hh)), blk,
                  pl.BlockSpec((1, 1, s // HEAD_SB, tq, HEAD_SB), lambda hh, i: (hh, i, 0, 0, 0))]
        + [pl.BlockSpec(memory_space=pl.ANY)] * len(follow),
        out_specs=[blk, col_blk, col_blk], out_shape=[shape, shape, shape],
        scratch_shapes=[pltpu.VMEM((tq, HEAD_SB), F32), pltpu.VMEM((tq, HEAD_SB), F32)],
        name="sb_bwd", compiler_params=_params(("parallel", "arbitrary")))(proj, proj, proj, do, a_saved, *follow)


def _gm_mask():
    t = lax.broadcasted_iota(jnp.int32, (GROUP_GM, GROUP_GM), 0)
    s = lax.broadcasted_iota(jnp.int32, (GROUP_GM, GROUP_GM), 1)
    shift = CHUNK.bit_length() - 1
    return (s >> shift) <= (t >> shift)


def _gm_fwd(proj, g_vnorm, w_s, b_st, u_blk):
    s = proj.shape[0]
    groups = w_s.shape[0]
    w = groups * GROUP_GM

    def body(u_ref, v_ref, gv_ref, ws_ref, bst_ref, o_ref):
        ug = _gelu(u_ref[...])
        vg = _gelu(v_ref[...])
        vn = vg * lax.rsqrt(jnp.mean(vg * vg, axis=-1, keepdims=True) + EPS) * gv_ref[...]
        vnb = vn.astype(BF16)
        mask = _gm_mask()
        for g in range(groups):
            sl = slice(g * GROUP_GM, (g + 1) * GROUP_GM)
            wm = jnp.where(mask, ws_ref[g], 0.0).astype(BF16)
            mixed = _dot(wm, vnb[:, sl]) + bst_ref[:, g:g + 1]
            o_ref[:, sl] = (ug[:, sl] * mixed).astype(o_ref.dtype)

    return _pcall(
        body, grid=(s // GROUP_GM,),
        in_specs=[pl.BlockSpec((GROUP_GM, w), lambda c: (c, u_blk)), pl.BlockSpec((GROUP_GM, w), lambda c: (c, u_blk + 1)),
                  pl.BlockSpec((1, w), lambda c: (0, 0)), pl.BlockSpec((groups, GROUP_GM, GROUP_GM), lambda c: (0, 0, 0)),
                  pl.BlockSpec((GROUP_GM, groups), lambda c: (0, 0))],
        out_specs=pl.BlockSpec((GROUP_GM, w), lambda c: (c, 0)),
        out_shape=jax.ShapeDtypeStruct((s, w), BF16), name="gm_fwd",
        compiler_params=_params(("parallel",)))(proj, proj, g_vnorm, w_s, b_st)


def _gm_bwd(proj, g_vnorm, w_s, b_st, do, u_blk):
    s = proj.shape[0]
    groups = w_s.shape[0]
    w = groups * GROUP_GM

    def body(u_ref, v_ref, gv_ref, ws_ref, bst_ref, do_ref, du_ref, dv_ref, dgv_ref, dws_ref, dbst_ref, dvn_ref):
        @pl.when(pl.program_id(0) == 0)
        def _():
            dgv_ref[...] = jnp.zeros_like(dgv_ref)
            dws_ref[...] = jnp.zeros_like(dws_ref)
            dbst_ref[...] = jnp.zeros_like(dbst_ref)

        ug, ugrad = _gelu_and_grad(u_ref[...])
        vg, vgrad = _gelu_and_grad(v_ref[...])
        r = lax.rsqrt(jnp.mean(vg * vg, axis=-1, keepdims=True) + EPS)
        n = vg * r
        gv = gv_ref[...]
        vnb = (n * gv).astype(BF16)
        dout = do_ref[...]
        mask = _gm_mask()
        for g in range(groups):
            sl = slice(g * GROUP_GM, (g + 1) * GROUP_GM)
            wm = jnp.where(mask, ws_ref[g], 0.0).astype(BF16)
            mixed = _dot(wm, vnb[:, sl]) + bst_ref[:, g:g + 1]
            dmixed = dout[:, sl] * ug[:, sl]
            du_ref[:, sl] = dout[:, sl] * mixed * ugrad[:, sl]
            dbst_ref[:, g:g + 1] += jnp.sum(dmixed, axis=1, keepdims=True)
            dmb = dmixed.astype(BF16)
            dws_ref[g] += jnp.where(mask, _dot_nt(dmb, vnb[:, sl]), 0.0)
            dvn_ref[:, sl] = _dot_tn(wm, dmb)
        dvn = dvn_ref[...]
        dgv_ref[...] += jnp.sum(dvn * n, axis=0, keepdims=True)
        dn = dvn * gv
        dvg = r * (dn - n * jnp.mean(dn * n, axis=-1, keepdims=True))
        dv_ref[...] = dvg * vgrad

    rowb = pl.BlockSpec((GROUP_GM, w), lambda c: (c, 0))
    vec = pl.BlockSpec((1, w), lambda c: (0, 0))
    wsb = pl.BlockSpec((groups, GROUP_GM, GROUP_GM), lambda c: (0, 0, 0))
    bsb = pl.BlockSpec((GROUP_GM, groups), lambda c: (0, 0))
    return _pcall(
        body, grid=(s // GROUP_GM,),
        in_specs=[pl.BlockSpec((GROUP_GM, w), lambda c: (c, u_blk)), pl.BlockSpec((GROUP_GM, w), lambda c: (c, u_blk + 1)),
                  vec, wsb, bsb, rowb],
        out_specs=[rowb, rowb, vec, wsb, bsb],
        out_shape=[jax.ShapeDtypeStruct((s, w), F32), jax.ShapeDtypeStruct((s, w), F32), jax.ShapeDtypeStruct((1, w), F32),
                   jax.ShapeDtypeStruct((groups, GROUP_GM, GROUP_GM), F32), jax.ShapeDtypeStruct((GROUP_GM, groups), F32)],
        scratch_shapes=[pltpu.VMEM((GROUP_GM, w), F32)], name="gm_bwd",
        compiler_params=_params(("arbitrary",)))(proj, proj, g_vnorm, w_s, b_st, do)


def _xa_fwd(proj, mem_kv, q_blk, n_heads):
    s = proj.shape[0]
    nm = mem_kv.shape[0]
    tq = _tile(s, (512, 256, 128))
    scale = HEAD_XA ** -0.5

    def body(q_ref, k_ref, v_ref, o_ref):
        z = _dot_nt(q_ref[...].astype(BF16), k_ref[...].astype(BF16)) * scale
        z = z - jnp.max(z, axis=-1, keepdims=True)
        e = jnp.exp(z)
        p = e / jnp.sum(e, axis=-1, keepdims=True)
        o_ref[...] = _dot(p.astype(BF16), v_ref[...].astype(BF16)).astype(o_ref.dtype)

    h = n_heads
    return _pcall(
        body, grid=(h, s // tq),
        in_specs=[pl.BlockSpec((tq, HEAD_XA), lambda hh, i: (i, q_blk + hh)),
                  pl.BlockSpec((nm, HEAD_XA), lambda hh, i: (0, hh)), pl.BlockSpec((nm, HEAD_XA), lambda hh, i: (0, h + hh))],
        out_specs=pl.BlockSpec((tq, HEAD_XA), lambda hh, i: (i, hh)),
        out_shape=jax.ShapeDtypeStruct((s, h * HEAD_XA), BF16), name="xa_fwd",
        compiler_params=_params(("parallel", "parallel")))(proj, mem_kv, mem_kv)


def _xa_bwd(proj, mem_kv, do, q_blk, n_heads):
    s = proj.shape[0]
    nm = mem_kv.shape[0]
    tq = _tile(s, (512, 256, 128))
    scale = HEAD_XA ** -0.5
    h = n_heads

    def body(q_ref, k_ref, v_ref, do_ref, dq_ref, dk_ref, dv_ref):
        @pl.when(pl.program_id(1) == 0)
        def _():
            dk_ref[...] = jnp.zeros_like(dk_ref)
            dv_ref[...] = jnp.zeros_like(dv_ref)

        qb = q_ref[...].astype(BF16)
        kb = k_ref[...].astype(BF16)
        vb = v_ref[...].astype(BF16)
        dob = do_ref[...].astype(BF16)
        z = _dot_nt(qb, kb) * scale
        z = z - jnp.max(z, axis=-1, keepdims=True)
        e = jnp.exp(z)
        p = e / jnp.sum(e, axis=-1, keepdims=True)
        dp = _dot_nt(dob, vb)
        dz = (p * (dp - jnp.sum(dp * p, axis=-1, keepdims=True)) * scale).astype(BF16)
        dq_ref[...] = _dot(dz, kb)
        dk_ref[...] += _dot_tn(dz, qb)
        dv_ref[...] += _dot_tn(p.astype(BF16), dob)

    qspec = pl.BlockSpec((tq, HEAD_XA), lambda hh, i: (i, hh))
    dk, dv = None, None
    dq, dk, dv = _pcall(
        body, grid=(h, s // tq),
        in_specs=[pl.BlockSpec((tq, HEAD_XA), lambda hh, i: (i, q_blk + hh)),
                  pl.BlockSpec((nm, HEAD_XA), lambda hh, i: (0, hh)), pl.BlockSpec((nm, HEAD_XA), lambda hh, i: (0, h + hh)),
                  qspec],
        out_specs=[qspec, pl.BlockSpec((nm, HEAD_XA), lambda hh, i: (0, hh)), pl.BlockSpec((nm, HEAD_XA), lambda hh, i: (0, hh))],
        out_shape=[jax.ShapeDtypeStruct((s, h * HEAD_XA), F32), jax.ShapeDtypeStruct((nm, h * HEAD_XA), F32),
                   jax.ShapeDtypeStruct((nm, h * HEAD_XA), F32)],
        name="xa_bwd", compiler_params=_params(("parallel", "arbitrary")))(proj, mem_kv, mem_kv, do)
    return dq, dk, dv


def _merge_fwd(zg, b_gate, branches):
    s, d = branches[0].shape
    tr = _tile(s, (128,))

    def body(z0, z1, z2, g0, g1, g2, b0, b1, b2, o_ref):
        acc = None
        for z, g, b in ((z0, g0, b0), (z1, g1, b1), (z2, g2, b2)):
            term = jax.nn.sigmoid(z[...].astype(F32) + g[...]) * b[...]
            acc = term if acc is None else acc + term
        o_ref[...] = acc.astype(o_ref.dtype)

    zs = [pl.BlockSpec((tr, d), functools.partial(lambda i, k: (i, k), k=k)) for k in range(3)]
    gs = [pl.BlockSpec((1, d), functools.partial(lambda i, k: (0, k), k=k)) for k in range(3)]
    row = pl.BlockSpec((tr, d), lambda i: (i, 0))
    return _pcall(
        body, grid=(s // tr,), in_specs=zs + gs + [row] * 3, out_specs=row,
        out_shape=jax.ShapeDtypeStruct((s, d), BF16), name="merge_fwd",
        compiler_params=_params(("parallel",)))(zg, zg, zg, b_gate, b_gate, b_gate, *branches)


def _merge_bwd(zg, b_gate, branches, dmerged):
    s, d = branches[0].shape
    tr = _tile(s, (128,))

    def body(z0, z1, z2, g0, g1, g2, b0, b1, b2, dm_ref, dz_ref, d0, d1, d2, dbg_ref):
        @pl.when(pl.program_id(0) == 0)
        def _():
            dbg_ref[...] = jnp.zeros_like(dbg_ref)

        dm = dm_ref[...]
        for k, (z, g, b, dbr) in enumerate(((z0, g0, b0, d0), (z1, g1, b1, d1), (z2, g2, b2, d2))):
            sg = jax.nn.sigmoid(z[...].astype(F32) + g[...])
            dbr[...] = (dm * sg).astype(dbr.dtype)
            dz = dm * b[...] * sg * (1.0 - sg)
            dz_ref[:, k * d:(k + 1) * d] = dz.astype(dz_ref.dtype)
            dbg_ref[:, k * d:(k + 1) * d] += jnp.sum(dz, axis=0, keepdims=True)

    zs = [pl.BlockSpec((tr, d), functools.partial(lambda i, k: (i, k), k=k)) for k in range(3)]
    gs = [pl.BlockSpec((1, d), functools.partial(lambda i, k: (0, k), k=k)) for k in range(3)]
    row = pl.BlockSpec((tr, d), lambda i: (i, 0))
    outs = _pcall(
        body, grid=(s // tr,), in_specs=zs + gs + [row] * 4,
        out_specs=[pl.BlockSpec((tr, 3 * d), lambda i: (i, 0)), row, row, row, pl.BlockSpec((1, 3 * d), lambda i: (0, 0))],
        out_shape=[jax.ShapeDtypeStruct((s, 3 * d), BF16)] + [jax.ShapeDtypeStruct((s, d), BF16)] * 3
        + [jax.ShapeDtypeStruct((1, 3 * d), F32)],
        name="merge_bwd", compiler_params=_params(("arbitrary",)))(zg, zg, zg, b_gate, b_gate, b_gate, *branches, dmerged)
    return outs[0], list(outs[1:4]), outs[4]


def _shift_down(x, k, row):
    return jnp.where(row >= k, pltpu.roll(x, k, 0), 0.0)


def _shift_up(x, k, row, s):
    return jnp.where(row < s - k, pltpu.roll(x, s - k, 0), 0.0)


def _conv_pre(gate, cw_ref, cb_ref, row):
    conv = cb_ref[...] + cw_ref[CONV_TAPS - 1:CONV_TAPS, :] * gate
    for k in range(1, CONV_TAPS):
        conv = conv + cw_ref[CONV_TAPS - 1 - k:CONV_TAPS - k, :] * _shift_down(gate, k, row)
    return conv


def _cg_fwd(up, conv_w, conv_b):
    s = up.shape[0]
    f = conv_w.shape[1]
    tc = _tile(f, (256, 128))
    nb = f // tc

    def body(g_ref, v_ref, cw_ref, cb_ref, o_ref):
        row = lax.broadcasted_iota(jnp.int32, (s, tc), 0)
        conv = _conv_pre(g_ref[...].astype(F32), cw_ref, cb_ref, row)
        o_ref[...] = (_gelu(conv) * v_ref[...].astype(F32)).astype(o_ref.dtype)

    return _pcall(
        body, grid=(nb,),
        in_specs=[pl.BlockSpec((s, tc), lambda j: (0, j)), pl.BlockSpec((s, tc), lambda j: (0, nb + j)),
                  pl.BlockSpec((CONV_TAPS, tc), lambda j: (0, j)), pl.BlockSpec((1, tc), lambda j: (0, j))],
        out_specs=pl.BlockSpec((s, tc), lambda j: (0, j)),
        out_shape=jax.ShapeDtypeStruct((s, f), BF16), name="cg_fwd",
        compiler_params=_params(("parallel",)))(up, up, conv_w, conv_b)


def _cg_bwd(up, conv_w, conv_b, dact):
    s = up.shape[0]
    f = conv_w.shape[1]
    tc = _tile(f, (256, 128))
    nb = f // tc

    def body(g_ref, v_ref, cw_ref, cb_ref, da_ref, dg_ref, dv_ref, dcw_ref, dcb_ref):
        row = lax.broadcasted_iota(jnp.int32, (s, tc), 0)
        gate = g_ref[...].astype(F32)
        conv = _conv_pre(gate, cw_ref, cb_ref, row)
        gel, ggrad = _gelu_and_grad(conv)
        da = da_ref[...]
        dv_ref[...] = (da * gel).astype(dv_ref.dtype)
        dconv = da * v_ref[...].astype(F32) * ggrad
        dgate = cw_ref[CONV_TAPS - 1:CONV_TAPS, :] * dconv
        dcw_ref[CONV_TAPS - 1:CONV_TAPS, :] = jnp.sum(dconv * gate, axis=0, keepdims=True)
        for k in range(1, CONV_TAPS):
            dgate = dgate + cw_ref[CONV_TAPS - 1 - k:CONV_TAPS - k, :] * _shift_up(dconv, k, row, s)
            dcw_ref[CONV_TAPS - 1 - k:CONV_TAPS - k, :] = jnp.sum(dconv * _shift_down(gate, k, row), axis=0, keepdims=True)
        dg_ref[...] = dgate.astype(dg_ref.dtype)
        dcb_ref[...] = jnp.sum(dconv, axis=0, keepdims=True)

    colb = pl.BlockSpec((s, tc), lambda j: (0, j))
    return _pcall(
        body, grid=(nb,),
        in_specs=[colb, pl.BlockSpec((s, tc), lambda j: (0, nb + j)), pl.BlockSpec((CONV_TAPS, tc), lambda j: (0, j)),
                  pl.BlockSpec((1, tc), lambda j: (0, j)), colb],
        out_specs=[colb, colb, pl.BlockSpec((CONV_TAPS, tc), lambda j: (0, j)), pl.BlockSpec((1, tc), lambda j: (0, j))],
        out_shape=[jax.ShapeDtypeStruct((s, f), BF16), jax.ShapeDtypeStruct((s, f), BF16),
                   jax.ShapeDtypeStruct((CONV_TAPS, f), F32), jax.ShapeDtypeStruct((1, f), F32)],
        name="cg_bwd", compiler_params=_params(("parallel",)))(up, up, conv_w, conv_b, dact)


def _row_tile(rows, cols, elems=256 * 1024):
    want = max(16, elems // cols)
    for c in (512, 256, 128, 64, 32, 16):
        if c <= want and rows % c == 0:
            return c
    return rows


def _sum_halves(dwv, recv, core, name):
    nj, _, a, c = dwv.shape
    tr = _row_tile(a, c, 1024 * 1024)

    def body(core_ref, d_ref, r_ref, o_ref):
        o_ref[0] = (d_ref[0, 0].astype(F32) + r_ref[0].astype(F32)).astype(o_ref.dtype)

    grid_spec = pltpu.PrefetchScalarGridSpec(
        num_scalar_prefetch=1, grid=(nj, a // tr),
        in_specs=[pl.BlockSpec((1, 1, tr, c), lambda j, i, cr: (j, cr[0], i, 0)),
                  pl.BlockSpec((1, tr, c), lambda j, i, cr: (j, i, 0))],
        out_specs=pl.BlockSpec((1, tr, c), lambda j, i, cr: (j, i, 0)))
    return _pcall(body, grid_spec=grid_spec, out_shape=jax.ShapeDtypeStruct((nj, a, c), BF16), name=name,
                  compiler_params=_params(("parallel", "parallel")))(core, dwv, recv)


def _sum_chips(recv, own, chip, core, ax):
    _, a, b = recv.shape
    tr = _row_tile(a, b, 512 * 1024)

    def body(chip_ref, core_ref, r_ref, own_ref, o_ref):
        me = chip_ref[0]
        mine = own_ref[0].astype(F32)
        acc = None
        for k in range(N_CHIPS):
            term = jnp.where(me == k, mine, r_ref[k].astype(F32))
            acc = term if acc is None else acc + term
        o_ref[0] = acc

    own_spec = (pl.BlockSpec((1, tr, b), lambda i, ch, co: (0, i, ch[0])) if ax == 2
                else pl.BlockSpec((1, tr, b), lambda i, ch, co: (ch[0], i, 0)))
    grid_spec = pltpu.PrefetchScalarGridSpec(
        num_scalar_prefetch=2, grid=(a // tr,),
        in_specs=[pl.BlockSpec((N_CHIPS, tr, b), lambda i, ch, co: (0, i, 0)), own_spec],
        out_specs=pl.BlockSpec((1, tr, b), lambda i, ch, co: (co[0], i, 0)))
    return _pcall(body, grid_spec=grid_spec, out_shape=jax.ShapeDtypeStruct((2, a, b), F32),
                  name="sum_chips", compiler_params=_params(("parallel",)))(chip, core, recv, own)


def _place_own(wt, layer, chip, ax, after):
    nl, r, c = wt.shape
    half = r // 2
    tr = _row_tile(half, c, 512 * 1024)
    nb = half // tr

    def body(chip_ref, w_ref, after_ref, o_ref):
        o_ref[...] = w_ref[...].astype(BF16).reshape(o_ref.shape)

    if ax == 2:
        out_spec = pl.BlockSpec((1, tr, c), lambda h, i, ch: (h, i, ch[0]))
    else:
        out_spec = pl.BlockSpec((1, 1, tr, c), lambda h, i, ch: (ch[0], h, i, 0))
    grid_spec = pltpu.PrefetchScalarGridSpec(
        num_scalar_prefetch=1, grid=(2, nb),
        in_specs=[pl.BlockSpec((1, tr, c), lambda h, i, ch: (layer, h * nb + i, 0)), pl.BlockSpec(memory_space=pl.ANY)],
        out_specs=out_spec)
    return _pcall(body, grid_spec=grid_spec, out_shape=jax.ShapeDtypeStruct(_full_view_shape(wt.shape, ax), BF16),
                  name="place_own", compiler_params=_params(("parallel", "parallel")))(chip, wt, after)


def _adamw(w, m, v, g, layer, prev, name, after=None):
    nl, r, c = w.shape
    tr = _row_tile(r, c, 512 * 1024)
    c1 = 1.0 - ADAM_B1 ** ADAM_STEP
    c2 = 1.0 - ADAM_B2 ** ADAM_STEP

    follow = [] if after is None else [after]

    def body(w_ref, m_ref, v_ref, gin_ref, *rest):
        g_ref, d_ref, nm_ref, nv_ref = rest[-4:]
        g = gin_ref[...]
        mm = ADAM_B1 * m_ref[0] + (1.0 - ADAM_B1) * g
        vv = ADAM_B2 * v_ref[0] + (1.0 - ADAM_B2) * (g * g)
        g_ref[0] = g
        nm_ref[0] = mm
        nv_ref[0] = vv
        d_ref[0] = -ADAM_LR * ((mm / c1) / (jnp.sqrt(vv / c2) + ADAM_EPS) + ADAM_WD * w_ref[0])

    blk = pl.BlockSpec((1, tr, c), lambda i: (layer, i, 0))
    shape = jax.ShapeDtypeStruct((nl, r, c), F32)
    extra = [] if prev is None else list(prev)
    return _pcall(
        body, grid=(r // tr,),
        in_specs=[blk] * 3 + [pl.BlockSpec((tr, c), lambda i: (i, 0))] + [pl.BlockSpec(memory_space=pl.ANY)] * (len(extra) + len(follow)),
        out_specs=[blk] * 4, out_shape=[shape] * 4, input_output_aliases={4 + k: k for k in range(len(extra))}, name=name,
        compiler_params=_params(("parallel",)))(w, m, v, g, *extra, *follow)


HBM_SPEC = pl.BlockSpec(memory_space=pltpu.HBM)
COMM = pltpu.CompilerParams(has_side_effects=True)


def _position():
    x, y, c = lax.axis_index("x"), lax.axis_index("y"), lax.axis_index("c")
    chips = [(1 - x, y), (x, 1 - y), (1 - x, 1 - y)]
    return x, y, c, chips


def _remote(src, dst, send_sem, recv_sem, dev):
    return pltpu.make_async_remote_copy(src_ref=src, dst_ref=dst, send_sem=send_sem, recv_sem=recv_sem,
                                        device_id=dev, device_id_type=MESH)


def _full_view_shape(shard_shape, ax):
    _, r, c = shard_shape
    return (2, r // 2, c * N_CHIPS) if ax == 2 else (N_CHIPS, 2, r // 2, c)


def _piece(ref, ax, j, h, cs):
    if ax == 2:
        return ref.at[h, :, pl.ds(pl.multiple_of(j * cs, cs), cs)]
    return ref.at[j, h]


def _chip_block(ref, ax, j, cs):
    if ax == 2:
        return ref.at[:, :, pl.ds(pl.multiple_of(j * cs, cs), cs)]
    return ref.at[j]


SEM_SPEC = pl.BlockSpec(memory_space=pltpu.SEMAPHORE)
ANY_SPEC = pl.BlockSpec(memory_space=pl.ANY)
SPLIT = pltpu.CompilerParams(has_side_effects=pltpu.SideEffectType.DATAFLOW_SIDE_EFFECTING)


def _exchange(kind, name, bufs, build, n_sems, after=None, sems=None):
    n = len(bufs)
    if kind == 'sync':
        def body(*refs):
            mine, theirs = build(refs[n:2 * n], refs[2 * n], refs[2 * n + 1])
            for cp in mine:
                cp.start()
            for cp in theirs:
                cp.wait_recv()
            for cp in mine:
                cp.wait_send()

        return list(_pcall(
            body, in_specs=[HBM_SPEC] * n, out_specs=[HBM_SPEC] * n,
            out_shape=[jax.ShapeDtypeStruct(v.shape, v.dtype) for v in bufs], input_output_aliases={t: t for t in range(n)},
            scratch_shapes=[pltpu.SemaphoreType.DMA((n_sems,)), pltpu.SemaphoreType.DMA((n_sems,))],
            name=name, compiler_params=COMM)(*bufs))
    if kind == 'start':
        def body(*refs):
            mine, _ = build(refs[n + 3:2 * n + 3], refs[n + 1], refs[n + 2])
            for cp in mine:
                cp.start()
            refs[2 * n + 3][...] = jnp.zeros_like(refs[2 * n + 3])

        outs = _pcall(
            body, in_specs=[HBM_SPEC] * n + [ANY_SPEC],
            out_specs=[SEM_SPEC, SEM_SPEC] + [HBM_SPEC] * n + [pl.BlockSpec(memory_space=pltpu.VMEM)],
            out_shape=[pltpu.SemaphoreType.DMA((n_sems,)), pltpu.SemaphoreType.DMA((n_sems,))]
            + [pltpu.HBM(v.shape, v.dtype) for v in bufs] + [jax.ShapeDtypeStruct((8, LANES), F32)],
            input_output_aliases={t: 2 + t for t in range(n)}, name=name,
            compiler_params=SPLIT)(*[pltpu.with_memory_space_constraint(v, pltpu.HBM) for v in bufs], after)
        return (outs[0], outs[1]), list(outs[2:2 + n]), outs[2 + n]

    def body(*refs):
        mine, theirs = build(refs[:n], refs[n], refs[n + 1])
        for cp in mine:
            cp.wait_send()
        for cp in theirs:
            cp.wait_recv()

    return list(_pcall(
        body, in_specs=[HBM_SPEC] * n + [SEM_SPEC, SEM_SPEC, ANY_SPEC], out_specs=[HBM_SPEC] * n,
        out_shape=[pltpu.HBM(v.shape, v.dtype) for v in bufs], input_output_aliases={t: t for t in range(n)},
        name=name, compiler_params=SPLIT)(*bufs, sems[0], sems[1], after))


def _gather_ici_copies(axes, shard_cols):
    def build(bufs, send_sems, recv_sems):
        x, y, c, chips = _position()
        me = 2 * x + y
        mine, theirs = [], []
        for t, ax in enumerate(axes):
            own = _piece(bufs[t], ax, me, c, shard_cols[t])
            for p, (px, py) in enumerate(chips):
                k = t * 3 + p
                got = _piece(bufs[t], ax, 2 * px + py, c, shard_cols[t])
                mine.append(_remote(own, own, send_sems.at[k], recv_sems.at[k], (px, py, c)))
                theirs.append(_remote(got, got, send_sems.at[k], recv_sems.at[k], (px, py, c)))
        return mine, theirs
    return build


def _gather_d2d_copies(axes, shard_cols):
    def build(bufs, send_sems, recv_sems):
        x, y, c, chips = _position()
        mine, theirs = [], []
        for t, ax in enumerate(axes):
            for p, (px, py) in enumerate(chips):
                k = t * 3 + p
                had = _piece(bufs[t], ax, 2 * px + py, c, shard_cols[t])
                got = _piece(bufs[t], ax, 2 * px + py, 1 - c, shard_cols[t])
                mine.append(_remote(had, had, send_sems.at[k], recv_sems.at[k], (x, y, 1 - c)))
                theirs.append(_remote(got, got, send_sems.at[k], recv_sems.at[k], (x, y, 1 - c)))
        return mine, theirs
    return build


def _grads_d2d_copies(n):
    def build(bufs, send_sems, recv_sems):
        x, y, c, _ = _position()
        mine = [_remote(bufs[t].at[:, 1 - c], bufs[n + t], send_sems.at[t], recv_sems.at[t], (x, y, 1 - c)) for t in range(n)]
        return mine, mine
    return build


def _grads_ici_copies(axes):
    n = len(axes)

    def build(bufs, send_sems, recv_sems):
        x, y, c, chips = _position()
        me = 2 * x + y

        def block(t, j):
            if axes[t] == 2:
                cs = bufs[n + t].shape[2]
                return bufs[t].at[0, :, pl.ds(pl.multiple_of(j * cs, cs), cs)]
            return bufs[t].at[j]

        mine, theirs = [], []
        for t in range(n):
            for p, (px, py) in enumerate(chips):
                k = t * 3 + p
                peer = 2 * px + py
                mine.append(_remote(block(t, peer), bufs[n + t].at[me], send_sems.at[k], recv_sems.at[k], (px, py, c)))
                theirs.append(_remote(block(t, peer), bufs[n + t].at[peer], send_sems.at[k], recv_sems.at[k], (px, py, c)))
        return mine, theirs
    return build


def _join_copies(bufs, send_sems, recv_sems):
    x, y, c, _ = _position()
    mine = [_remote(b.at[c], b.at[c], send_sems.at[t], recv_sems.at[t], (x, y, 1 - c)) for t, b in enumerate(bufs)]
    theirs = [_remote(b.at[1 - c], b.at[1 - c], send_sems.at[t], recv_sems.at[t], (x, y, 1 - c)) for t, b in enumerate(bufs)]
    return mine, theirs


def _grads_recv_shape(sm, ax):
    _, a, c = sm.shape
    return (N_CHIPS, a, c // N_CHIPS if ax == 2 else c)


def _gather_small(shard):
    nl, r, cs = shard.shape

    def body(in_ref, out_ref, send_sems, recv_sems, local_sem):
        x, y, c, chips = _position()

        def cols(j):
            return out_ref.at[:, :, pl.ds(pl.multiple_of(j * cs, cs), cs)]

        me = 2 * x + y
        loc = pltpu.make_async_copy(in_ref, cols(me), local_sem)
        loc.start()
        remote = [_remote(in_ref, cols(me), send_sems.at[p], recv_sems.at[p], (px, py, c)) for p, (px, py) in enumerate(chips)]
        for cp in remote:
            cp.start()
        for p, (px, py) in enumerate(chips):
            _remote(in_ref, cols(2 * px + py), send_sems.at[p], recv_sems.at[p], (px, py, c)).wait_recv()
        for cp in remote:
            cp.wait_send()
        loc.wait()

    return _pcall(
        body, in_specs=[HBM_SPEC], out_specs=HBM_SPEC, out_shape=jax.ShapeDtypeStruct((nl, r, cs * N_CHIPS), shard.dtype),
        scratch_shapes=[pltpu.SemaphoreType.DMA((3,)), pltpu.SemaphoreType.DMA((3,)), pltpu.SemaphoreType.DMA(())],
        name="gather_small", compiler_params=COMM)(shard)


def _small_copies(bufs, send_sems, recv_sems):
    pack, land = bufs
    x, y, cc, _ = _position()
    me = 4 * x + 2 * y + cc
    mine, theirs = [], []
    for k in range(1, N_DEV):
        px, py, pc = x ^ ((k >> 2) & 1), y ^ ((k >> 1) & 1), cc ^ (k & 1)
        mine.append(_remote(pack, land.at[me], send_sems.at[k - 1], recv_sems.at[k - 1], (px, py, pc)))
        theirs.append(_remote(pack, land.at[4 * px + 2 * py + pc], send_sems.at[k - 1], recv_sems.at[k - 1], (px, py, pc)))
    return mine, theirs


def _sum_devices(land, pack, dev):
    _, r, c = land.shape
    tr = _tile(r, (672, 512, 256, 128, 64, 8))

    def body(dev_ref, l_ref, p_ref, o_ref):
        me = dev_ref[0]
        acc = None
        for k in range(N_DEV):
            term = jnp.where(me == k, p_ref[...], l_ref[k])
            acc = term if acc is None else acc + term
        o_ref[...] = acc

    grid_spec = pltpu.PrefetchScalarGridSpec(
        num_scalar_prefetch=1, grid=(r // tr,),
        in_specs=[pl.BlockSpec((N_DEV, tr, c), lambda i, dv: (0, i, 0)), pl.BlockSpec((tr, c), lambda i, dv: (i, 0))],
        out_specs=pl.BlockSpec((tr, c), lambda i, dv: (i, 0)))
    return _pcall(body, grid_spec=grid_spec, out_shape=jax.ShapeDtypeStruct((r, c), F32), name="sum_devices",
                  compiler_params=_params(("parallel",)))(dev, land, pack)


def _dims(d):
    half = d // 2
    return half // HEAD_SB, half // HEAD_XA, 3, (5 * half) // HEAD_XA


def _layer_fwd(x, mem, weight, small, l, after=None):
    h_sb, h_xa, u_blk, q_blk = _dims(x.shape[1])

    def vec(name):
        return small[name][l].reshape(1, -1)

    def use(a, name, follows, mm_name, dtype=F32):
        wt = weight(name, follows)
        return _mm(a, wt, 'nn', dtype, mm_name, after=weight.token())

    h1 = _norm_fwd(x, vec('g_mix_pre'), None, BF16, "norm_mix_pre", after)
    proj = use(h1, 'w_in', h1, "mm_proj")
    o_sb, a_sb = _sb_fwd(proj, h_sb)
    b_st = small['b_s'][l].T
    o_gm = _gm_fwd(proj, vec('g_vnorm'), small['w_s'][l], b_st, u_blk)
    memn = _norm_fwd(mem, vec('g_mem'), None, BF16, "norm_mem")
    mem_kv = use(memn, 'w_mem_kv', o_sb, "mm_mem_kv")
    o_xa = _xa_fwd(proj, mem_kv, q_blk, h_xa)
    zg = use(h1, 'w_gate', o_sb, "mm_gate", BF16)
    branches = [use(o, wn, zg, "mm_branch") for o, wn in ((o_sb, 'w_br_sb'), (o_gm, 'w_br_gm'), (o_xa, 'w_br_xa'))]
    merged = _merge_fwd(zg, vec('b_gate'), branches)
    y1 = use(merged, 'w_out', zg, "mm_out")
    x1 = _norm_fwd(y1, vec('g_mix_post'), x, F32, "norm_mix_post")
    h2 = _norm_fwd(x1, vec('g_ffn_pre'), None, BF16, "norm_ffn_pre")
    up = use(h2, 'w_up', h2, "mm_up", BF16)
    act = _cg_fwd(up, weight('conv_w', up), vec('conv_b'))
    y2 = use(act, 'w_down', act, "mm_down")
    x2 = _norm_fwd(y2, vec('g_ffn_post'), x1, F32, "norm_ffn_post")
    saved = dict(x0=x, h1=h1, proj=proj, o_sb=o_sb, a_sb=a_sb, o_gm=o_gm, o_xa=o_xa, memn=memn, mem_kv=mem_kv, zg=zg,
                 branches=branches, merged=merged, y1=y1, x1=x1, h2=h2, up=up, act=act, y2=y2, b_st=b_st)
    return x2, saved


def _layer_bwd(dx, mem, sv, full, small, l, after, emit):
    h_sb, h_xa, u_blk, q_blk = _dims(dx.shape[1])

    def vec(name):
        return small[name][l].reshape(1, -1)

    gb, gs = {}, {}
    dy2, gs['g_ffn_post'] = _norm_bwd(sv['y2'], vec('g_ffn_post'), [dx], None, BF16, "norm_ffn_post_bwd", after)
    gb['w_down'] = _mm(sv['act'], dy2, 'tn', BF16, "mm_down_dw")
    dact = _mm(dy2, full['w_down'], 'nt', F32, "mm_down_dx")
    dgate, dval, gs['conv_w'], gs['conv_b'] = _cg_bwd(sv['up'], full['conv_w'], vec('conv_b'), dact)
    dup = jnp.concatenate([dgate, dval], axis=1)
    gb['w_up'] = _mm(sv['h2'], dup, 'tn', BF16, "mm_up_dw")
    token = emit(0, gb)
    dh2 = _mm(dup, full['w_up'], 'nt', F32, "mm_up_dx", after=token)
    token = emit.flush(dh2)
    dx1, gs['g_ffn_pre'] = _norm_bwd(sv['x1'], vec('g_ffn_pre'), [dh2], dx, F32, "norm_ffn_pre_bwd", token)
    dy1, gs['g_mix_post'] = _norm_bwd(sv['y1'], vec('g_mix_post'), [dx1], None, BF16, "norm_mix_post_bwd")
    gb['w_out'] = _mm(sv['merged'], dy1, 'tn', BF16, "mm_out_dw")
    dmerged = _mm(dy1, full['w_out'], 'nt', F32, "mm_out_dx")
    dzg, dbr, gs['b_gate'] = _merge_bwd(sv['zg'], vec('b_gate'), sv['branches'], dmerged)
    douts = []
    for o, db, wn in ((sv['o_sb'], dbr[0], 'w_br_sb'), (sv['o_gm'], dbr[1], 'w_br_gm'), (sv['o_xa'], dbr[2], 'w_br_xa')):
        gb[wn] = _mm(o, db, 'tn', BF16, "mm_branch_dw")
        douts.append(_mm(db, full[wn], 'nt', F32, "mm_branch_dx"))
    gb['w_gate'] = _mm(sv['h1'], dzg, 'tn', BF16, "mm_gate_dw")
    dq_xa, dk_xa, dv_xa = _xa_bwd(sv['proj'], sv['mem_kv'], douts[2], q_blk, h_xa)
    dmem_kv = jnp.concatenate([dk_xa, dv_xa], axis=1).astype(BF16)
    gb['w_mem_kv'] = _mm(sv['memn'], dmem_kv, 'tn', BF16, "mm_mem_kv_dw")
    token = emit(1, gb)
    dh1_gate = _mm(dzg, full['w_gate'], 'nt', F32, "mm_gate_dx", after=token)
    token = emit.flush(dh1_gate)
    dmemn = _mm(dmem_kv, full['w_mem_kv'], 'nt', F32, "mm_mem_kv_dx")
    _, gs['g_mem'] = _norm_bwd(mem, vec('g_mem'), [dmemn], None, BF16, "norm_mem_bwd")
    du, dv, gs['g_vnorm'], gs['w_s'], db_st = _gm_bwd(sv['proj'], vec('g_vnorm'), small['w_s'][l], sv['b_st'], douts[1], u_blk)
    gs['b_s'] = db_st.T
    dq, dk, dvv = _sb_bwd(sv['proj'], sv['a_sb'], douts[0], h_sb, token)
    dproj = jnp.concatenate([dq, dk, dvv, du, dv, dq_xa], axis=1).astype(BF16)
    gb['w_in'] = _mm(sv['h1'], dproj, 'tn', BF16, "mm_proj_dw")
    token = emit(2, gb)
    dh1_proj = _mm(dproj, full['w_in'], 'nt', F32, "mm_proj_dx")
    dx0, gs['g_mix_pre'] = _norm_bwd(sv['x0'], vec('g_mix_pre'), [dh1_gate, dh1_proj], dx1, F32, "norm_mix_pre_bwd", token)
    return dx0, gs


class _Given:
    def __init__(self, full):
        self.full = full

    def __call__(self, name, follows):
        return self.full[name]

    def token(self):
        return None


def _local_step(x, mem, target, full, small):
    n_layers = len(full['w_in'])
    saved = []
    for l in range(n_layers):
        x, sv = _layer_fwd(x, mem, _Given({n: full[n][l] for n in full}), small, l)
        saved.append(sv)
    sq, dx = _loss_head(x, target)
    gbig = {n: [None] * n_layers for n in BIG}
    gsmall = {n: [None] * n_layers for n in SMALL + ['conv_w']}
    class Collect:
        def __init__(self, l):
            self.l = l

        def __call__(self, g, gb):
            for n in BWD_GROUPS[g]:
                gbig[n][self.l] = gb[n]

        def flush(self, follows):
            return None

    for l in reversed(range(n_layers)):
        dx, gs = _layer_bwd(dx, mem, saved[l], {n: full[n][l] for n in full}, small, l, None, Collect(l))
        for n in gs:
            gsmall[n][l] = gs[n]
    return sq, dx, gbig, gsmall


def _pack(arrays, rows_multiple):
    flat = jnp.concatenate([a.reshape(-1).astype(F32) for a in arrays])
    rows = -(-flat.shape[0] // LANES)
    rows = -(-rows // rows_multiple) * rows_multiple
    return jnp.pad(flat, (0, rows * LANES - flat.shape[0])).reshape(rows, LANES)


def _unpack(pack, like):
    flat = pack.reshape(-1)
    out, off = [], 0
    for a in like:
        out.append(flat[off:off + a.size].reshape(a.shape))
        off += a.size
    return out


def _grad_view(g, ax):
    r, c = g.shape
    return g.reshape(1, 2, r // 2, c) if ax == 2 else g.reshape(N_CHIPS, 2, r // (2 * N_CHIPS), c)


def kernel(x, mem, g_mix_pre, w_in, g_vnorm, w_s, b_s, g_mem, w_mem_kv, w_gate, b_gate, w_br_sb, w_br_gm, w_br_xa, w_out, g_mix_post, g_ffn_pre, w_up, conv_w, conv_b, w_down, g_ffn_post, loss_target, m_g_mix_pre, m_w_in, m_g_vnorm, m_w_s, m_b_s, m_g_mem, m_w_mem_kv, m_w_gate, m_b_gate, m_w_br_sb, m_w_br_gm, m_w_br_xa, m_w_out, m_g_mix_post, m_g_ffn_pre, m_w_up, m_conv_w, m_conv_b, m_w_down, m_g_ffn_post, v_g_mix_pre, v_w_in, v_g_vnorm, v_w_s, v_b_s, v_g_mem, v_w_mem_kv, v_w_gate, v_b_gate, v_w_br_sb, v_w_br_gm, v_w_br_xa, v_w_out, v_g_mix_post, v_g_ffn_pre, v_w_up, v_conv_w, v_conv_b, v_w_down, v_g_ffn_post):
    w = dict(g_mix_pre=g_mix_pre, w_in=w_in, g_vnorm=g_vnorm, w_s=w_s, b_s=b_s, g_mem=g_mem, w_mem_kv=w_mem_kv,
             w_gate=w_gate, b_gate=b_gate, w_br_sb=w_br_sb, w_br_gm=w_br_gm, w_br_xa=w_br_xa, w_out=w_out,
             g_mix_post=g_mix_post, g_ffn_pre=g_ffn_pre, w_up=w_up, conv_w=conv_w, conv_b=conv_b, w_down=w_down,
             g_ffn_post=g_ffn_post)
    m = dict(g_mix_pre=m_g_mix_pre, w_in=m_w_in, g_vnorm=m_g_vnorm, w_s=m_w_s, b_s=m_b_s, g_mem=m_g_mem,
             w_mem_kv=m_w_mem_kv, w_gate=m_w_gate, b_gate=m_b_gate, w_br_sb=m_w_br_sb, w_br_gm=m_w_br_gm,
             w_br_xa=m_w_br_xa, w_out=m_w_out, g_mix_post=m_g_mix_post, g_ffn_pre=m_g_ffn_pre, w_up=m_w_up,
             conv_w=m_conv_w, conv_b=m_conv_b, w_down=m_w_down, g_ffn_post=m_g_ffn_post)
    v = dict(g_mix_pre=v_g_mix_pre, w_in=v_w_in, g_vnorm=v_g_vnorm, w_s=v_w_s, b_s=v_b_s, g_mem=v_g_mem,
             w_mem_kv=v_w_mem_kv, w_gate=v_w_gate, b_gate=v_b_gate, w_br_sb=v_w_br_sb, w_br_gm=v_w_br_gm,
             w_br_xa=v_w_br_xa, w_out=v_w_out, g_mix_post=v_g_mix_post, g_ffn_pre=v_g_ffn_pre, w_up=v_w_up,
             conv_w=v_conv_w, conv_b=v_conv_b, w_down=v_w_down, g_ffn_post=v_g_ffn_post)
    n_layers = w_in.shape[0]
    d = x.shape[-1]
    core = lax.axis_index("c").astype(jnp.int32).reshape(1)
    chip = (2 * lax.axis_index("x") + lax.axis_index("y")).astype(jnp.int32).reshape(1)
    small = {n: w[n] for n in SMALL}
    xs, mems, target = x[0], mem[0], loss_target[0]

    conv_w_full = _gather_small(conv_w)

    def as_full(vw, ax):
        return vw.reshape(-1, vw.shape[-1]) if ax == 1 else vw.reshape(vw.shape[0] * vw.shape[1], vw.shape[2])

    stages = {}
    keys = [(l, g) for l in range(n_layers) for g in range(len(FWD_GROUPS))]

    def start_gathers(l, token):
        for g, names in enumerate(FWD_GROUPS):
            ax_g = [BIG_AXIS[n] for n in names]
            cols_g = [w[n].shape[2] for n in names]
            views = [_place_own(w[n], l, chip, ax, token) for n, ax in zip(names, ax_g)]
            sems, views, token = _exchange('start', f"gather_ici_start_{l}_{g}", views, _gather_ici_copies(ax_g, cols_g),
                                           3 * len(names), after=token)
            stages[l, g] = dict(names=names, ax=ax_g, cols=cols_g, views=views, ici=sems, d2d=None, full=None)
        return token

    token = start_gathers(0, conv_w_full)

    def cross_cores(key, follows):
        st, (l, g) = stages[key], key
        n3 = 3 * len(st['names'])
        views = _exchange('wait', f"gather_ici_wait_{l}_{g}", st['views'], _gather_ici_copies(st['ax'], st['cols']), n3,
                          after=follows, sems=st['ici'])
        st['d2d'], st['views'], tok = _exchange('start', f"gather_d2d_start_{l}_{g}", views,
                                                _gather_d2d_copies(st['ax'], st['cols']), n3, after=core)
        return tok

    class Weights:
        def __init__(self, l):
            self.l, self.tok = l, None

        def __call__(self, name, follows):
            if name == 'conv_w':
                return conv_w_full[self.l]
            key = (self.l, [g for g, names in enumerate(FWD_GROUPS) if name in names][0])
            if key[1] == 1 and (self.l + 1, 0) not in stages and self.l + 1 < n_layers:
                self.tok = start_gathers(self.l + 1, follows)
            st = stages[key]
            if st['full'] is None:
                if st['d2d'] is None:
                    cross_cores(key, follows)
                views = _exchange('wait', f"gather_d2d_wait_{key[0]}_{key[1]}", st['views'],
                                  _gather_d2d_copies(st['ax'], st['cols']), 3 * len(st['names']), after=follows, sems=st['d2d'])
                st['full'] = {n: as_full(vw, ax) for n, vw, ax in zip(st['names'], views, st['ax'])}
            nxt = keys.index(key) + 1
            if name == st['names'][-1] and nxt < len(keys) and key != keys[0] and stages[keys[nxt]]['d2d'] is None:
                self.tok = cross_cores(keys[nxt], follows)
            return st['full'][name]

        def token(self):
            return self.tok

    fulls, saved = [], []
    for l in range(n_layers):
        xs, sv = _layer_fwd(xs, mems, Weights(l), small, l, token if l == 0 else None)
        fulls.append({n: stages[l, g]['full'][n] for g, names in enumerate(FWD_GROUPS) for n in names} | {'conv_w': conv_w_full[l]})
        saved.append(sv)
    sq, dx = _loss_head(xs, target)
    loss = lax.psum(0.5 * jnp.sum(sq) / d, ("x", "y", "c"))

    sent = []

    def to_chips(l, g, names, ax_g, bufs, after):
        n = len(names)
        sums = [_sum_halves(dv, th, core, "sum_halves") for dv, th in zip(bufs[:n], bufs[n:])]
        lands = [lax.empty(_grads_recv_shape(sm, ax), sm.dtype) for sm, ax in zip(sums, ax_g)]
        sems, bufs, tok = _exchange('start', f"grads_ici_start_{l}_{g}", sums + lands, _grads_ici_copies(ax_g), 3 * n,
                                    after=core if after is None else after)
        sent.append((l, g, names, ax_g, bufs, sems))
        return tok

    class Grads:
        def __init__(self, l):
            self.l, self.crossing = l, None

        def __call__(self, g, gb):
            names = BWD_GROUPS[g]
            ax_g = [BIG_AXIS[n] for n in names]
            n = len(names)
            dwvs = [_grad_view(gb[nm], ax) for nm, ax in zip(names, ax_g)]
            lands = [lax.empty((dv.shape[0],) + dv.shape[2:], dv.dtype) for dv in dwvs]
            if g + 1 < len(BWD_GROUPS):
                sems, bufs, tok = _exchange('start', f"grads_d2d_start_{self.l}_{g}", dwvs + lands, _grads_d2d_copies(n), n,
                                            after=core)
                self.crossing = (g, names, ax_g, bufs, sems)
                return tok
            return to_chips(self.l, g, names, ax_g, _exchange('sync', "grads_d2d", dwvs + lands, _grads_d2d_copies(n), n), None)

        def flush(self, follows):
            if self.crossing is None:
                return None
            (g, names, ax_g, bufs, sems), self.crossing = self.crossing, None
            bufs = _exchange('wait', f"grads_d2d_wait_{self.l}_{g}", bufs, _grads_d2d_copies(len(names)), len(names),
                             after=follows, sems=sems)
            return to_chips(self.l, g, names, ax_g, bufs, None)

    gsmall = {n: [None] * n_layers for n in SMALL + ['conv_w']}
    for l in reversed(range(n_layers)):
        dx, gs = _layer_bwd(dx, mems, saved[l], fulls[l], small, l, None, Grads(l))
        for n in gs:
            gsmall[n][l] = gs[n]

    names_small = SMALL + ['conv_w']
    small_full = [jnp.stack(gsmall[n]).reshape(w[n].shape) for n in SMALL]
    conv_w_grad = jnp.stack(gsmall['conv_w'])
    pack = _pack(small_full + [conv_w_grad], 8)
    small_sems, small_bufs, _ = _exchange('start', "small_start", [pack, lax.empty((N_DEV,) + pack.shape, F32)], _small_copies,
                                          N_DEV - 1, after=dx)

    def small_update(follows):
        pk, land = _exchange('wait', "small_wait", small_bufs, _small_copies, N_DEV - 1, after=follows, sems=small_sems)
        device = (4 * lax.axis_index("x") + 2 * lax.axis_index("y") + lax.axis_index("c")).astype(jnp.int32).reshape(1)
        summed = _sum_devices(land, pk, device)
        *small_g, conv_w_g = _unpack(summed, small_full + [conv_w_grad])
        shard = conv_w.shape[-1]
        conv_w_g = lax.dynamic_slice_in_dim(conv_w_g, chip[0] * shard, shard, axis=2)
        packed = [_pack([p[n] for n in names_small], 256) for p in (w, m, v)]
        gpack = _pack(small_g + [conv_w_g], 256)
        res = _adamw(packed[0][None], packed[1][None], packed[2][None], gpack, 0, None, "adamw_small")
        like = [w[n] for n in names_small]
        unpacked = [_unpack(r[0], like) for r in res]
        return {n: tuple(u[i] for u in unpacked) for i, n in enumerate(names_small)}

    out = {}

    def update(joining, follows):
        l, g, names, halves, sems = joining
        halves = _exchange('wait', f"join_wait_{l}_{g}", halves, _join_copies, len(names), after=follows, sems=sems)
        last = None
        for n, hv in zip(names, halves):
            out[n] = _adamw(w[n], m[n], v[n], hv.reshape(w[n].shape[1:]), l, out.get(n), "adamw_big", last)
            last = out[n][0]
        return last

    joining, follows = None, dx
    for k, (l, g, names, ax_g, bufs, sems) in enumerate(sent):
        n = len(names)
        bufs = _exchange('wait', f"grads_ici_wait_{l}_{g}", bufs, _grads_ici_copies(ax_g), 3 * n, after=follows, sems=sems)
        halves = [_sum_chips(r, sm, chip, core, ax) for sm, r, ax in zip(bufs[:n], bufs[n:], ax_g)]
        jsems, halves, tok = _exchange('start', f"join_start_{l}_{g}", halves, _join_copies, n, after=core)
        if joining is not None:
            follows = update(joining, tok)
        joining = (l, g, names, halves, jsems)
    out.update(small_update(update(joining, follows)))

    return (loss, dx[None], *[out[n][0] for n in WEIGHTS], *[out[n][1] for n in WEIGHTS],
            *[out[n][2] for n in WEIGHTS], *[out[n][3] for n in WEIGHTS])
```

```python
import functools
import math

import jax
import jax.numpy as jnp
from jax import lax
from jax.experimental import pallas as pl
from jax.experimental.pallas import tpu as pltpu

F32 = jnp.float32
BF16 = jnp.bfloat16
EPS = 1e-6
HEAD_SB = 128
GROUP_GM = 128
CHUNK = 64
HEAD_XA = 256
CONV_TAPS = 3
N_CHIPS = 4
N_DEV = 8
LANES = 128
MIB = 1024 * 1024
VMEM_LIMIT = 48 * MIB
SPLITS = 1

ADAM_LR = 0.001
ADAM_B1 = 0.9
ADAM_B2 = 0.999
ADAM_EPS = 1e-08
ADAM_WD = 0.01
ADAM_STEP = 10

WEIGHTS = ['g_mix_pre', 'w_in', 'g_vnorm', 'w_s', 'b_s', 'g_mem', 'w_mem_kv', 'w_gate', 'b_gate', 'w_br_sb',
           'w_br_gm', 'w_br_xa', 'w_out', 'g_mix_post', 'g_ffn_pre', 'w_up', 'conv_w', 'conv_b', 'w_down',
           'g_ffn_post']
BIG_AXIS = {'w_in': 2, 'w_mem_kv': 1, 'w_gate': 2, 'w_br_sb': 2, 'w_br_gm': 2, 'w_br_xa': 2, 'w_out': 1,
            'w_up': 2, 'w_down': 1}
BIG = list(BIG_AXIS)
FWD_GROUPS = [['w_in'], ['w_mem_kv', 'w_gate'], ['w_br_sb', 'w_br_gm', 'w_br_xa', 'w_out'], ['w_up'], ['w_down']]
BWD_GROUPS = [['w_down', 'w_up'], ['w_out', 'w_br_sb', 'w_br_gm', 'w_br_xa', 'w_gate', 'w_mem_kv'], ['w_in']]
SMALL = ['g_mix_pre', 'g_vnorm', 'w_s', 'b_s', 'g_mem', 'b_gate', 'g_mix_post', 'g_ffn_pre', 'conv_b', 'g_ffn_post']
MESH = pl.DeviceIdType.MESH


def _pcall(body, **kw):
    return pl.pallas_call(body, **kw)


def _params(sem=None, vmem=VMEM_LIMIT):
    return pltpu.CompilerParams(dimension_semantics=sem, vmem_limit_bytes=vmem)


def _tile(n, cands):
    for c in cands:
        if n % c == 0:
            return c
    return n


_GELU_C = math.sqrt(2.0 / math.pi)
_GELU_A = 0.044715


def _gelu(x):
    return 0.5 * x * (1.0 + jnp.tanh(_GELU_C * (x + _GELU_A * (x * x * x))))


def _gelu_and_grad(x):
    x2 = x * x
    t = jnp.tanh(_GELU_C * (x + _GELU_A * (x2 * x)))
    val = 0.5 * x * (1.0 + t)
    grad = 0.5 * (1.0 + t) + 0.5 * x * (1.0 - t * t) * (_GELU_C * (1.0 + 3.0 * _GELU_A * x2))
    return val, grad


def _softplus(z):
    return jnp.maximum(z, 0.0) + jnp.log(1.0 + jnp.exp(-jnp.abs(z)))


def _dot(a, b):
    return jnp.dot(a, b, preferred_element_type=F32)


def _dot_nt(a, b):
    return lax.dot_general(a, b, (((1,), (1,)), ((), ())), preferred_element_type=F32)


def _dot_tn(a, b):
    return lax.dot_general(a, b, (((0,), (0,)), ((), ())), preferred_element_type=F32)


def _split_dot(a, m):
    out = None
    rest = a
    for _ in range(SPLITS):
        piece = rest.astype(BF16)
        rest = rest - piece.astype(F32)
        term = _dot(piece, m)
        out = term if out is None else out + term
    return out


def _mm(a, b, mode, out_dtype, name, tm=None, tn=None, tk=None, after=None):
    a_parts = list(a) if isinstance(a, (list, tuple)) else [a]
    b_parts = list(b) if isinstance(b, (list, tuple)) else [b]
    assert len(a_parts) == 1 or mode == 'nt'
    assert len(b_parts) == 1 or mode == 'tn'
    na, nb = len(a_parts), len(b_parts)
    if mode == 'nn':
        (m, kc), (kc2, n) = a_parts[0].shape, b_parts[0].shape
    elif mode == 'nt':
        (m, kp), (n, kc2) = a_parts[0].shape, b_parts[0].shape
        kc = kp * na
    else:
        (kc, m), (kc2, npiece) = a_parts[0].shape, b_parts[0].shape
        n = npiece * nb
    assert kc == kc2, (a_parts[0].shape, b_parts[0].shape, mode)
    tm = tm or _tile(m, (1024, 512, 256, 128))
    tn = tn or _tile(n // nb, (1024, 512, 256, 128))
    k_max = 3072 // na
    tk = tk or ((kc // na) if kc // na <= k_max else _tile(kc // na, [c for c in (3072, 2816, 2048, 1536, 1408, 1024, 512) if c <= k_max]))
    nk = kc // tk
    k_per = nk // na
    n_per = (n // tn) // nb
    dot = {'nn': _dot, 'nt': _dot_nt, 'tn': _dot_tn}[mode]

    def within(idx, p, per):
        return jnp.clip(idx - p * per, 0, per - 1)

    if mode == 'tn':
        a_specs = [pl.BlockSpec((tk, tm), lambda i, j, k: (k, i))]
        b_specs = [pl.BlockSpec((tk, tn), functools.partial(lambda i, j, k, p: (k, within(j, p, n_per)), p=p)) for p in range(nb)]
    else:
        a_specs = [pl.BlockSpec((tm, tk), functools.partial(lambda i, j, k, p: (i, within(k, p, k_per)), p=p)) for p in range(na)]
        b_specs = [pl.BlockSpec((tn, tk), lambda i, j, k: (j, k)) if mode == 'nt' else pl.BlockSpec((tk, tn), lambda i, j, k: (k, j))]

    extra = [] if after is None else [after]
    extra_specs = [pl.BlockSpec(memory_space=pl.ANY)] * len(extra)

    def product(a_refs, b_refs, store):
        if na == 1 and nb == 1:
            store(dot(a_refs[0][...].astype(BF16), b_refs[0][...].astype(BF16)))
            return
        which = (pl.program_id(2) // k_per) if na > 1 else (pl.program_id(1) // n_per)
        for p in range(max(na, nb)):
            @pl.when(which == p)
            def _(p=p):
                store(dot(a_refs[p if na > 1 else 0][...].astype(BF16), b_refs[p if nb > 1 else 0][...].astype(BF16)))

    if nk == 1:
        def body(*refs):
            o_ref = refs[-1]

            def store(part):
                o_ref[...] = part.astype(o_ref.dtype)

            product(refs[:na], refs[na:na + nb], store)
        scratch = []
    else:
        def body(*refs):
            o_ref, acc_ref = refs[-2], refs[-1]
            k = pl.program_id(2)

            def store(part):
                @pl.when(k == 0)
                def _():
                    acc_ref[...] = part

                @pl.when(k > 0)
                def _():
                    acc_ref[...] += part

            product(refs[:na], refs[na:na + nb], store)

            @pl.when(k == nk - 1)
            def _():
                o_ref[...] = acc_ref[...].astype(o_ref.dtype)
        scratch = [pltpu.VMEM((tm, tn), F32)]

    return _pcall(
        body, grid=(m // tm, n // tn, nk), in_specs=a_specs + b_specs + extra_specs,
        out_specs=pl.BlockSpec((tm, tn), lambda i, j, k: (i, j)),
        out_shape=jax.ShapeDtypeStruct((m, n), out_dtype), scratch_shapes=scratch, name=name,
        compiler_params=_params(("parallel", "parallel", "arbitrary")))(*a_parts, *b_parts, *extra)


def _norm_fwd(x, g, res, out_dtype, name, after=None):
    s, d = x.shape
    tr = _tile(s, (256, 128))
    has_res = res is not None
    has_after = after is not None

    def body(*refs):
        x_ref, g_ref = refs[0], refs[1]
        o_ref = refs[-1]
        xv = x_ref[...]
        y = xv * lax.rsqrt(jnp.mean(xv * xv, axis=-1, keepdims=True) + EPS) * g_ref[...]
        if has_res:
            y = y + refs[2][...]
        o_ref[...] = y.astype(o_ref.dtype)

    row = pl.BlockSpec((tr, d), lambda i: (i, 0))
    ins = [x, g] + ([res] if has_res else []) + ([after] if has_after else [])
    return _pcall(
        body, grid=(s // tr,),
        in_specs=[row, pl.BlockSpec((1, d), lambda i: (0, 0))] + ([row] if has_res else [])
        + ([pl.BlockSpec(memory_space=pl.ANY)] if has_after else []),
        out_specs=row, out_shape=jax.ShapeDtypeStruct((s, d), out_dtype), name=name,
        compiler_params=_params(("parallel",)))(*ins)


def _norm_bwd(x, g, douts, dres, out_dtype, name, after=None):
    s, d = x.shape
    tr = _tile(s, (256, 128))
    nd = len(douts)
    has_res = dres is not None
    has_after = after is not None

    def body(*refs):
        x_ref, g_ref = refs[0], refs[1]
        dx_ref, dg_ref = refs[-2], refs[-1]
        dout = refs[2][...].astype(F32)
        for r in refs[3:2 + nd]:
            dout = dout + r[...].astype(F32)
        xv = x_ref[...]
        r = lax.rsqrt(jnp.mean(xv * xv, axis=-1, keepdims=True) + EPS)
        n = xv * r
        dn = dout * g_ref[...]
        dx = r * (dn - n * jnp.mean(dn * n, axis=-1, keepdims=True))
        if has_res:
            dx = dx + refs[2 + nd][...]
        dx_ref[...] = dx.astype(dx_ref.dtype)

        @pl.when(pl.program_id(0) == 0)
        def _():
            dg_ref[...] = jnp.zeros_like(dg_ref)

        dg_ref[...] += jnp.sum(dout * n, axis=0, keepdims=True)

    row = pl.BlockSpec((tr, d), lambda i: (i, 0))
    vec = pl.BlockSpec((1, d), lambda i: (0, 0))
    ins = [x, g] + list(douts) + ([dres] if has_res else []) + ([after] if has_after else [])
    return _pcall(
        body, grid=(s // tr,),
        in_specs=[row, vec] + [row] * (nd + int(has_res)) + ([pl.BlockSpec(memory_space=pl.ANY)] if has_after else []),
        out_specs=[row, vec],
        out_shape=[jax.ShapeDtypeStruct((s, d), out_dtype), jax.ShapeDtypeStruct((1, d), F32)], name=name,
        compiler_params=_params(("arbitrary",)))(*ins)


def _loss_head(y, target):
    s, d = y.shape
    tr = _tile(s, (256, 128))

    def body(y_ref, t_ref, sq_ref, dy_ref):
        e = y_ref[...] - t_ref[...]
        dy_ref[...] = e * (1.0 / d)

        @pl.when(pl.program_id(0) == 0)
        def _():
            sq_ref[...] = jnp.zeros_like(sq_ref)

        sq_ref[...] += jnp.sum(e * e, axis=0, keepdims=True)

    row = pl.BlockSpec((tr, d), lambda i: (i, 0))
    return _pcall(
        body, grid=(s // tr,), in_specs=[row, row], out_specs=[pl.BlockSpec((1, d), lambda i: (0, 0)), row],
        out_shape=[jax.ShapeDtypeStruct((1, d), F32), jax.ShapeDtypeStruct((s, d), F32)], name="loss_head",
        compiler_params=_params(("arbitrary",)))(y, target)


NEVER = -1e30
SB_QUERIES = 512


def _sb_sum_matrix(later):
    r = lax.broadcasted_iota(jnp.int32, (HEAD_SB, 2 * HEAD_SB), 0)
    c = lax.broadcasted_iota(jnp.int32, (HEAD_SB, 2 * HEAD_SB), 1)
    tri = jnp.where((r > c) if later else (r < c), 1.0, 0.0)
    return jnp.where(c < HEAD_SB, tri, 1.0).astype(BF16)


def _sb_mask(tq, q0, k0):
    row = lax.broadcasted_iota(jnp.int32, (tq, HEAD_SB), 0)
    col = lax.broadcasted_iota(jnp.int32, (tq, HEAD_SB), 1)
    return (k0 + col) < (q0 + row)


def _sb_fwd(proj, n_heads):
    s = proj.shape[0]
    tq = min(SB_QUERIES, s)
    per = tq // HEAD_SB
    scale = HEAD_SB ** -0.5

    def body(q_ref, k_ref, v_ref, o_ref, a_ref, acc_ref, c_ref):
        i = pl.program_id(1)
        q = q_ref[...].astype(BF16)
        sums = _sb_sum_matrix(True)
        acc_ref[...] = jnp.zeros_like(acc_ref)
        c_ref[...] = jnp.zeros_like(c_ref)
        last = (i + 1) * per - 1

        def scores(j, masked):
            off = pl.multiple_of(j * HEAD_SB, HEAD_SB)
            z = _dot_nt(q, k_ref[pl.ds(off, HEAD_SB), :].astype(BF16)) * scale
            sp = _softplus(z)
            logb = z - sp
            if masked:
                mask = _sb_mask(tq, i * tq, off)
                logb = jnp.where(mask, logb, NEVER)
                sp = jnp.where(mask, sp, 0.0)
            return logb, _split_dot(sp, sums)

        def values(j, logb, both):
            off = pl.multiple_of(j * HEAD_SB, HEAD_SB)
            c = c_ref[...]
            a = jnp.exp(logb - both[:, :HEAD_SB] - c).astype(BF16)
            a_ref[0, 0, j] = a
            acc_ref[...] += _dot(a, v_ref[pl.ds(off, HEAD_SB), :].astype(BF16))
            c_ref[...] = c + both[:, HEAD_SB:]

        def step(jj, carry, masked):
            j = last - jj
            nxt = scores(j, masked)
            values(jnp.minimum(j + 1, last), *carry)
            return nxt

        idle = (jnp.full((tq, HEAD_SB), NEVER, F32), jnp.zeros((tq, 2 * HEAD_SB), F32))
        carry = idle
        for jj in range(per):
            carry = step(jj, carry, True)
        def group(jg, carry):
            for u in range(per):
                carry = step(per * jg + u, carry, False)
            return carry

        carry = lax.fori_loop(1, i + 1, group, carry)
        values(0, *carry)
        o_ref[...] = acc_ref[...].astype(o_ref.dtype)

    h = n_heads
    blk = pl.BlockSpec((tq, HEAD_SB), lambda hh, i: (i, hh))
    return _pcall(
        body, grid=(h, s // tq),
        in_specs=[blk, pl.BlockSpec((s, HEAD_SB), lambda hh, i: (0, h + hh)),
                  pl.BlockSpec((s, HEAD_SB), lambda hh, i: (0, 2 * h + hh))],
        out_specs=[blk, pl.BlockSpec((1, 1, s // HEAD_SB, tq, HEAD_SB), lambda hh, i: (hh, i, 0, 0, 0))],
        out_shape=[jax.ShapeDtypeStruct((s, h * HEAD_SB), BF16),
                   jax.ShapeDtypeStruct((h, s // tq, s // HEAD_SB, tq, HEAD_SB), BF16)],
        scratch_shapes=[pltpu.VMEM((tq, HEAD_SB), F32), pltpu.VMEM((tq, HEAD_SB), F32)],
        name="sb_fwd", compiler_params=_params(("parallel", "arbitrary")))(proj, proj, proj)


def _sb_bwd(proj, a_saved, do, n_heads, after=None):
    s = proj.shape[0]
    tq = min(SB_QUERIES, s)
    per = tq // HEAD_SB
    scale = HEAD_SB ** -0.5
    follow = [] if after is None else [after]

    def body(q_ref, k_ref, v_ref, do_ref, a_ref, *rest):
        dq_ref, dk_ref, dv_ref, run_ref, acc_ref = rest[len(follow):]
        i = pl.program_id(1)

        @pl.when(i == 0)
        def _():
            dk_ref[...] = jnp.zeros_like(dk_ref)
            dv_ref[...] = jnp.zeros_like(dv_ref)

        q = q_ref[...].astype(BF16)
        dob = do_ref[...].astype(BF16)
        run_ref[...] = jnp.zeros_like(run_ref)
        acc_ref[...] = jnp.zeros_like(acc_ref)
        earlier = _sb_sum_matrix(False)
        first_diagonal = i * per

        def step(j, masked):
            off = pl.multiple_of(j * HEAD_SB, HEAD_SB)
            kb = k_ref[pl.ds(off, HEAD_SB), :].astype(BF16)
            vb = v_ref[pl.ds(off, HEAD_SB), :].astype(BF16)
            a = a_ref[0, 0, j]
            g = a.astype(F32) * _dot_nt(dob, vb)
            dv_ref[pl.ds(off, HEAD_SB), :] += _dot_tn(a, dob)
            z = _dot_nt(q, kb) * scale
            beta = 1.0 / (1.0 + jnp.exp(-z))
            both = _split_dot(g, earlier)
            p = run_ref[...]
            dz = (g * (1.0 - beta) - beta * (both[:, :HEAD_SB] + p)) * scale
            if masked:
                dz = jnp.where(_sb_mask(tq, i * tq, off), dz, 0.0)
            dzb = dz.astype(BF16)
            dk_ref[pl.ds(off, HEAD_SB), :] += _dot_tn(dzb, q)
            acc_ref[...] += _dot(dzb, kb)
            run_ref[...] = p + both[:, HEAD_SB:]

        def group(jg, carry):
            for u in range(per):
                step(per * jg + u, False)
            return carry

        lax.fori_loop(0, i, group, 0)
        for u in range(per):
            step(first_diagonal + u, True)
        dq_ref[...] = acc_ref[...]

    h = n_heads
    blk = pl.BlockSpec((tq, HEAD_SB), lambda hh, i: (i, hh))
    col_blk = pl.BlockSpec((s, HEAD_SB), lambda hh, i: (0, hh))
    shape = jax.ShapeDtypeStruct((s, h * HEAD_SB), F32)
    return _pcall(
        body, grid=(h, s // tq),
        in_specs=[blk, pl.BlockSpec((s, HEAD_SB), lambda hh, i: (0, h + hh)),
                  pl.BlockSpec((s, HEAD_SB), lambda hh, i: (0, 2 * h + hh)), blk,
                  pl.BlockSpec((1, 1, s // HEAD_SB, tq, HEAD_SB), lambda hh, i: (hh, i, 0, 0, 0))]
        + [pl.BlockSpec(memory_space=pl.ANY)] * len(follow),
        out_specs=[blk, col_blk, col_blk], out_shape=[shape, shape, shape],
        scratch_shapes=[pltpu.VMEM((tq, HEAD_SB), F32), pltpu.VMEM((tq, HEAD_SB), F32)],
        name="sb_bwd", compiler_params=_params(("parallel", "arbitrary")))(proj, proj, proj, do, a_saved, *follow)


def _gm_mask():
    t = lax.broadcasted_iota(jnp.int32, (GROUP_GM, GROUP_GM), 0)
    s = lax.broadcasted_iota(jnp.int32, (GROUP_GM, GROUP_GM), 1)
    shift = CHUNK.bit_length() - 1
    return (s >> shift) <= (t >> shift)


def _gm_fwd(proj, g_vnorm, w_s, b_st, u_blk):
    s = proj.shape[0]
    groups = w_s.shape[0]
    w = groups * GROUP_GM

    def body(u_ref, v_ref, gv_ref, ws_ref, bst_ref, o_ref):
        ug = _gelu(u_ref[...])
        vg = _gelu(v_ref[...])
        vn = vg * lax.rsqrt(jnp.mean(vg * vg, axis=-1, keepdims=True) + EPS) * gv_ref[...]
        vnb = vn.astype(BF16)
        mask = _gm_mask()
        for g in range(groups):
            sl = slice(g * GROUP_GM, (g + 1) * GROUP_GM)
            wm = jnp.where(mask, ws_ref[g], 0.0).astype(BF16)
            mixed = _dot(wm, vnb[:, sl]) + bst_ref[:, g:g + 1]
            o_ref[:, sl] = (ug[:, sl] * mixed).astype(o_ref.dtype)

    return _pcall(
        body, grid=(s // GROUP_GM,),
        in_specs=[pl.BlockSpec((GROUP_GM, w), lambda c: (c, u_blk)), pl.BlockSpec((GROUP_GM, w), lambda c: (c, u_blk + 1)),
                  pl.BlockSpec((1, w), lambda c: (0, 0)), pl.BlockSpec((groups, GROUP_GM, GROUP_GM), lambda c: (0, 0, 0)),
                  pl.BlockSpec((GROUP_GM, groups), lambda c: (0, 0))],
        out_specs=pl.BlockSpec((GROUP_GM, w), lambda c: (c, 0)),
        out_shape=jax.ShapeDtypeStruct((s, w), BF16), name="gm_fwd",
        compiler_params=_params(("parallel",)))(proj, proj, g_vnorm, w_s, b_st)


def _gm_bwd(proj, g_vnorm, w_s, b_st, do, u_blk):
    s = proj.shape[0]
    groups = w_s.shape[0]
    w = groups * GROUP_GM

    def body(u_ref, v_ref, gv_ref, ws_ref, bst_ref, do_ref, du_ref, dv_ref, dgv_ref, dws_ref, dbst_ref, dvn_ref):
        @pl.when(pl.program_id(0) == 0)
        def _():
            dgv_ref[...] = jnp.zeros_like(dgv_ref)
            dws_ref[...] = jnp.zeros_like(dws_ref)
            dbst_ref[...] = jnp.zeros_like(dbst_ref)

        ug, ugrad = _gelu_and_grad(u_ref[...])
        vg, vgrad = _gelu_and_grad(v_ref[...])
        r = lax.rsqrt(jnp.mean(vg * vg, axis=-1, keepdims=True) + EPS)
        n = vg * r
        gv = gv_ref[...]
        vnb = (n * gv).astype(BF16)
        dout = do_ref[...]
        mask = _gm_mask()
        for g in range(groups):
            sl = slice(g * GROUP_GM, (g + 1) * GROUP_GM)
            wm = jnp.where(mask, ws_ref[g], 0.0).astype(BF16)
            mixed = _dot(wm, vnb[:, sl]) + bst_ref[:, g:g + 1]
            dmixed = dout[:, sl] * ug[:, sl]
            du_ref[:, sl] = dout[:, sl] * mixed * ugrad[:, sl]
            dbst_ref[:, g:g + 1] += jnp.sum(dmixed, axis=1, keepdims=True)
            dmb = dmixed.astype(BF16)
            dws_ref[g] += jnp.where(mask, _dot_nt(dmb, vnb[:, sl]), 0.0)
            dvn_ref[:, sl] = _dot_tn(wm, dmb)
        dvn = dvn_ref[...]
        dgv_ref[...] += jnp.sum(dvn * n, axis=0, keepdims=True)
        dn = dvn * gv
        dvg = r * (dn - n * jnp.mean(dn * n, axis=-1, keepdims=True))
        dv_ref[...] = dvg * vgrad

    rowb = pl.BlockSpec((GROUP_GM, w), lambda c: (c, 0))
    vec = pl.BlockSpec((1, w), lambda c: (0, 0))
    wsb = pl.BlockSpec((groups, GROUP_GM, GROUP_GM), lambda c: (0, 0, 0))
    bsb = pl.BlockSpec((GROUP_GM, groups), lambda c: (0, 0))
    return _pcall(
        body, grid=(s // GROUP_GM,),
        in_specs=[pl.BlockSpec((GROUP_GM, w), lambda c: (c, u_blk)), pl.BlockSpec((GROUP_GM, w), lambda c: (c, u_blk + 1)),
                  vec, wsb, bsb, rowb],
        out_specs=[rowb, rowb, vec, wsb, bsb],
        out_shape=[jax.ShapeDtypeStruct((s, w), F32), jax.ShapeDtypeStruct((s, w), F32), jax.ShapeDtypeStruct((1, w), F32),
                   jax.ShapeDtypeStruct((groups, GROUP_GM, GROUP_GM), F32), jax.ShapeDtypeStruct((GROUP_GM, groups), F32)],
        scratch_shapes=[pltpu.VMEM((GROUP_GM, w), F32)], name="gm_bwd",
        compiler_params=_params(("arbitrary",)))(proj, proj, g_vnorm, w_s, b_st, do)


def _xa_fwd(proj, mem_kv, q_blk, n_heads):
    s = proj.shape[0]
    nm = mem_kv.shape[0]
    tq = _tile(s, (512, 256, 128))
    scale = HEAD_XA ** -0.5

    def body(q_ref, k_ref, v_ref, o_ref):
        z = _dot_nt(q_ref[...].astype(BF16), k_ref[...].astype(BF16)) * scale
        z = z - jnp.max(z, axis=-1, keepdims=True)
        e = jnp.exp(z)
        p = e / jnp.sum(e, axis=-1, keepdims=True)
        o_ref[...] = _dot(p.astype(BF16), v_ref[...].astype(BF16)).astype(o_ref.dtype)

    h = n_heads
    return _pcall(
        body, grid=(h, s // tq),
        in_specs=[pl.BlockSpec((tq, HEAD_XA), lambda hh, i: (i, q_blk + hh)),
                  pl.BlockSpec((nm, HEAD_XA), lambda hh, i: (0, hh)), pl.BlockSpec((nm, HEAD_XA), lambda hh, i: (0, h + hh))],
        out_specs=pl.BlockSpec((tq, HEAD_XA), lambda hh, i: (i, hh)),
        out_shape=jax.ShapeDtypeStruct((s, h * HEAD_XA), BF16), name="xa_fwd",
        compiler_params=_params(("parallel", "parallel")))(proj, mem_kv, mem_kv)


def _xa_bwd(proj, mem_kv, do, q_blk, n_heads):
    s = proj.shape[0]
    nm = mem_kv.shape[0]
    tq = _tile(s, (512, 256, 128))
    scale = HEAD_XA ** -0.5
    h = n_heads

    def body(q_ref, k_ref, v_ref, do_ref, dq_ref, dk_ref, dv_ref):
        @pl.when(pl.program_id(1) == 0)
        def _():
            dk_ref[...] = jnp.zeros_like(dk_ref)
            dv_ref[...] = jnp.zeros_like(dv_ref)

        qb = q_ref[...].astype(BF16)
        kb = k_ref[...].astype(BF16)
        vb = v_ref[...].astype(BF16)
        dob = do_ref[...].astype(BF16)
        z = _dot_nt(qb, kb) * scale
        z = z - jnp.max(z, axis=-1, keepdims=True)
        e = jnp.exp(z)
        p = e / jnp.sum(e, axis=-1, keepdims=True)
        dp = _dot_nt(dob, vb)
        dz = (p * (dp - jnp.sum(dp * p, axis=-1, keepdims=True)) * scale).astype(BF16)
        dq_ref[...] = _dot(dz, kb)
        dk_ref[...] += _dot_tn(dz, qb)
        dv_ref[...] += _dot_tn(p.astype(BF16), dob)

    qspec = pl.BlockSpec((tq, HEAD_XA), lambda hh, i: (i, hh))
    dk, dv = None, None
    dq, dk, dv = _pcall(
        body, grid=(h, s // tq),
        in_specs=[pl.BlockSpec((tq, HEAD_XA), lambda hh, i: (i, q_blk + hh)),
                  pl.BlockSpec((nm, HEAD_XA), lambda hh, i: (0, hh)), pl.BlockSpec((nm, HEAD_XA), lambda hh, i: (0, h + hh)),
                  qspec],
        out_specs=[qspec, pl.BlockSpec((nm, HEAD_XA), lambda hh, i: (0, hh)), pl.BlockSpec((nm, HEAD_XA), lambda hh, i: (0, hh))],
        out_shape=[jax.ShapeDtypeStruct((s, h * HEAD_XA), F32), jax.ShapeDtypeStruct((nm, h * HEAD_XA), F32),
                   jax.ShapeDtypeStruct((nm, h * HEAD_XA), F32)],
        name="xa_bwd", compiler_params=_params(("parallel", "arbitrary")))(proj, mem_kv, mem_kv, do)
    return dq, dk, dv


def _merge_fwd(zg, b_gate, branches):
    s, d = branches[0].shape
    tr = _tile(s, (128,))

    def body(z0, z1, z2, g0, g1, g2, b0, b1, b2, o_ref):
        acc = None
        for z, g, b in ((z0, g0, b0), (z1, g1, b1), (z2, g2, b2)):
            term = jax.nn.sigmoid(z[...].astype(F32) + g[...]) * b[...]
            acc = term if acc is None else acc + term
        o_ref[...] = acc.astype(o_ref.dtype)

    zs = [pl.BlockSpec((tr, d), functools.partial(lambda i, k: (i, k), k=k)) for k in range(3)]
    gs = [pl.BlockSpec((1, d), functools.partial(lambda i, k: (0, k), k=k)) for k in range(3)]
    row = pl.BlockSpec((tr, d), lambda i: (i, 0))
    return _pcall(
        body, grid=(s // tr,), in_specs=zs + gs + [row] * 3, out_specs=row,
        out_shape=jax.ShapeDtypeStruct((s, d), BF16), name="merge_fwd",
        compiler_params=_params(("parallel",)))(zg, zg, zg, b_gate, b_gate, b_gate, *branches)


def _merge_bwd(zg, b_gate, branches, dmerged):
    s, d = branches[0].shape
    tr = _tile(s, (128,))

    def body(z0, z1, z2, g0, g1, g2, b0, b1, b2, dm_ref, dz_ref, d0, d1, d2, dbg_ref):
        @pl.when(pl.program_id(0) == 0)
        def _():
            dbg_ref[...] = jnp.zeros_like(dbg_ref)

        dm = dm_ref[...]
        for k, (z, g, b, dbr) in enumerate(((z0, g0, b0, d0), (z1, g1, b1, d1), (z2, g2, b2, d2))):
            sg = jax.nn.sigmoid(z[...].astype(F32) + g[...])
            dbr[...] = (dm * sg).astype(dbr.dtype)
            dz = dm * b[...] * sg * (1.0 - sg)
            dz_ref[:, k * d:(k + 1) * d] = dz.astype(dz_ref.dtype)
            dbg_ref[:, k * d:(k + 1) * d] += jnp.sum(dz, axis=0, keepdims=True)

    zs = [pl.BlockSpec((tr, d), functools.partial(lambda i, k: (i, k), k=k)) for k in range(3)]
    gs = [pl.BlockSpec((1, d), functools.partial(lambda i, k: (0, k), k=k)) for k in range(3)]
    row = pl.BlockSpec((tr, d), lambda i: (i, 0))
    outs = _pcall(
        body, grid=(s // tr,), in_specs=zs + gs + [row] * 4,
        out_specs=[pl.BlockSpec((tr, 3 * d), lambda i: (i, 0)), row, row, row, pl.BlockSpec((1, 3 * d), lambda i: (0, 0))],
        out_shape=[jax.ShapeDtypeStruct((s, 3 * d), BF16)] + [jax.ShapeDtypeStruct((s, d), BF16)] * 3
        + [jax.ShapeDtypeStruct((1, 3 * d), F32)],
        name="merge_bwd", compiler_params=_params(("arbitrary",)))(zg, zg, zg, b_gate, b_gate, b_gate, *branches, dmerged)
    return outs[0], list(outs[1:4]), outs[4]


def _shift_down(x, k, row):
    return jnp.where(row >= k, pltpu.roll(x, k, 0), 0.0)


def _shift_up(x, k, row, s):
    return jnp.where(row < s - k, pltpu.roll(x, s - k, 0), 0.0)


def _conv_pre(gate, cw_ref, cb_ref, row):
    conv = cb_ref[...] + cw_ref[CONV_TAPS - 1:CONV_TAPS, :] * gate
    for k in range(1, CONV_TAPS):
        conv = conv + cw_ref[CONV_TAPS - 1 - k:CONV_TAPS - k, :] * _shift_down(gate, k, row)
    return conv


def _cg_fwd(up, conv_w, conv_b):
    s = up.shape[0]
    f = conv_w.shape[1]
    tc = _tile(f, (256, 128))
    nb = f // tc

    def body(g_ref, v_ref, cw_ref, cb_ref, o_ref):
        row = lax.broadcasted_iota(jnp.int32, (s, tc), 0)
        conv = _conv_pre(g_ref[...].astype(F32), cw_ref, cb_ref, row)
        o_ref[...] = (_gelu(conv) * v_ref[...].astype(F32)).astype(o_ref.dtype)

    return _pcall(
        body, grid=(nb,),
        in_specs=[pl.BlockSpec((s, tc), lambda j: (0, j)), pl.BlockSpec((s, tc), lambda j: (0, nb + j)),
                  pl.BlockSpec((CONV_TAPS, tc), lambda j: (0, j)), pl.BlockSpec((1, tc), lambda j: (0, j))],
        out_specs=pl.BlockSpec((s, tc), lambda j: (0, j)),
        out_shape=jax.ShapeDtypeStruct((s, f), BF16), name="cg_fwd",
        compiler_params=_params(("parallel",)))(up, up, conv_w, conv_b)


def _cg_bwd(up, conv_w, conv_b, dact):
    s = up.shape[0]
    f = conv_w.shape[1]
    tc = _tile(f, (256, 128))
    nb = f // tc

    def body(g_ref, v_ref, cw_ref, cb_ref, da_ref, dg_ref, dv_ref, dcw_ref, dcb_ref):
        row = lax.broadcasted_iota(jnp.int32, (s, tc), 0)
        gate = g_ref[...].astype(F32)
        conv = _conv_pre(gate, cw_ref, cb_ref, row)
        gel, ggrad = _gelu_and_grad(conv)
        da = da_ref[...]
        dv_ref[...] = (da * gel).astype(dv_ref.dtype)
        dconv = da * v_ref[...].astype(F32) * ggrad
        dgate = cw_ref[CONV_TAPS - 1:CONV_TAPS, :] * dconv
        dcw_ref[CONV_TAPS - 1:CONV_TAPS, :] = jnp.sum(dconv * gate, axis=0, keepdims=True)
        for k in range(1, CONV_TAPS):
            dgate = dgate + cw_ref[CONV_TAPS - 1 - k:CONV_TAPS - k, :] * _shift_up(dconv, k, row, s)
            dcw_ref[CONV_TAPS - 1 - k:CONV_TAPS - k, :] = jnp.sum(dconv * _shift_down(gate, k, row), axis=0, keepdims=True)
        dg_ref[...] = dgate.astype(dg_ref.dtype)
        dcb_ref[...] = jnp.sum(dconv, axis=0, keepdims=True)

    colb = pl.BlockSpec((s, tc), lambda j: (0, j))
    return _pcall(
        body, grid=(nb,),
        in_specs=[colb, pl.BlockSpec((s, tc), lambda j: (0, nb + j)), pl.BlockSpec((CONV_TAPS, tc), lambda j: (0, j)),
                  pl.BlockSpec((1, tc), lambda j: (0, j)), colb],
        out_specs=[colb, colb, pl.BlockSpec((CONV_TAPS, tc), lambda j: (0, j)), pl.BlockSpec((1, tc), lambda j: (0, j))],
        out_shape=[jax.ShapeDtypeStruct((s, f), BF16), jax.ShapeDtypeStruct((s, f), BF16),
                   jax.ShapeDtypeStruct((CONV_TAPS, f), F32), jax.ShapeDtypeStruct((1, f), F32)],
        name="cg_bwd", compiler_params=_params(("parallel",)))(up, up, conv_w, conv_b, dact)


def _row_tile(rows, cols, elems=256 * 1024):
    want = max(16, elems // cols)
    for c in (512, 256, 128, 64, 32, 16):
        if c <= want and rows % c == 0:
            return c
    return rows


def _sum_halves(dwv, recv, core, name):
    nj, _, a, c = dwv.shape
    tr = _row_tile(a, c, 1024 * 1024)

    def body(core_ref, d_ref, r_ref, o_ref):
        o_ref[0] = (d_ref[0, 0].astype(F32) + r_ref[0].astype(F32)).astype(o_ref.dtype)

    grid_spec = pltpu.PrefetchScalarGridSpec(
        num_scalar_prefetch=1, grid=(nj, a // tr),
        in_specs=[pl.BlockSpec((1, 1, tr, c), lambda j, i, cr: (j, cr[0], i, 0)),
                  pl.BlockSpec((1, tr, c), lambda j, i, cr: (j, i, 0))],
        out_specs=pl.BlockSpec((1, tr, c), lambda j, i, cr: (j, i, 0)))
    return _pcall(body, grid_spec=grid_spec, out_shape=jax.ShapeDtypeStruct((nj, a, c), BF16), name=name,
                  compiler_params=_params(("parallel", "parallel")))(core, dwv, recv)


def _sum_chips(recv, own, chip, core, ax):
    _, a, b = recv.shape
    tr = _row_tile(a, b, 512 * 1024)

    def body(chip_ref, core_ref, r_ref, own_ref, o_ref):
        me = chip_ref[0]
        mine = own_ref[0].astype(F32)
        acc = None
        for k in range(N_CHIPS):
            term = jnp.where(me == k, mine, r_ref[k].astype(F32))
            acc = term if acc is None else acc + term
        o_ref[0] = acc

    own_spec = (pl.BlockSpec((1, tr, b), lambda i, ch, co: (0, i, ch[0])) if ax == 2
                else pl.BlockSpec((1, tr, b), lambda i, ch, co: (ch[0], i, 0)))
    grid_spec = pltpu.PrefetchScalarGridSpec(
        num_scalar_prefetch=2, grid=(a // tr,),
        in_specs=[pl.BlockSpec((N_CHIPS, tr, b), lambda i, ch, co: (0, i, 0)), own_spec],
        out_specs=pl.BlockSpec((1, tr, b), lambda i, ch, co: (co[0], i, 0)))
    return _pcall(body, grid_spec=grid_spec, out_shape=jax.ShapeDtypeStruct((2, a, b), F32),
                  name="sum_chips", compiler_params=_params(("parallel",)))(chip, core, recv, own)


def _place_own(wt, layer, chip, ax, after):
    nl, r, c = wt.shape
    half = r // 2
    tr = _row_tile(half, c, 512 * 1024)
    nb = half // tr

    def body(chip_ref, w_ref, after_ref, o_ref):
        o_ref[...] = w_ref[...].astype(BF16).reshape(o_ref.shape)

    if ax == 2:
        out_spec = pl.BlockSpec((1, tr, c), lambda h, i, ch: (h, i, ch[0]))
    else:
        out_spec = pl.BlockSpec((1, 1, tr, c), lambda h, i, ch: (ch[0], h, i, 0))
    grid_spec = pltpu.PrefetchScalarGridSpec(
        num_scalar_prefetch=1, grid=(2, nb),
        in_specs=[pl.BlockSpec((1, tr, c), lambda h, i, ch: (layer, h * nb + i, 0)), pl.BlockSpec(memory_space=pl.ANY)],
        out_specs=out_spec)
    return _pcall(body, grid_spec=grid_spec, out_shape=jax.ShapeDtypeStruct(_full_view_shape(wt.shape, ax), BF16),
                  name="place_own", compiler_params=_params(("parallel", "parallel")))(chip, wt, after)


def _adamw(w, m, v, g, layer, prev, name, after=None):
    nl, r, c = w.shape
    tr = _row_tile(r, c, 512 * 1024)
    c1 = 1.0 - ADAM_B1 ** ADAM_STEP
    c2 = 1.0 - ADAM_B2 ** ADAM_STEP

    follow = [] if after is None else [after]

    def body(w_ref, m_ref, v_ref, gin_ref, *rest):
        g_ref, d_ref, nm_ref, nv_ref = rest[-4:]
        g = gin_ref[...]
        mm = ADAM_B1 * m_ref[0] + (1.0 - ADAM_B1) * g
        vv = ADAM_B2 * v_ref[0] + (1.0 - ADAM_B2) * (g * g)
        g_ref[0] = g
        nm_ref[0] = mm
        nv_ref[0] = vv
        d_ref[0] = -ADAM_LR * ((mm / c1) / (jnp.sqrt(vv / c2) + ADAM_EPS) + ADAM_WD * w_ref[0])

    blk = pl.BlockSpec((1, tr, c), lambda i: (layer, i, 0))
    shape = jax.ShapeDtypeStruct((nl, r, c), F32)
    extra = [] if prev is None else list(prev)
    return _pcall(
        body, grid=(r // tr,),
        in_specs=[blk] * 3 + [pl.BlockSpec((tr, c), lambda i: (i, 0))] + [pl.BlockSpec(memory_space=pl.ANY)] * (len(extra) + len(follow)),
        out_specs=[blk] * 4, out_shape=[shape] * 4, input_output_aliases={4 + k: k for k in range(len(extra))}, name=name,
        compiler_params=_params(("parallel",)))(w, m, v, g, *extra, *follow)


HBM_SPEC = pl.BlockSpec(memory_space=pltpu.HBM)
COMM = pltpu.CompilerParams(has_side_effects=True)


def _position():
    x, y, c = lax.axis_index("x"), lax.axis_index("y"), lax.axis_index("c")
    chips = [(1 - x, y), (x, 1 - y), (1 - x, 1 - y)]
    return x, y, c, chips


def _remote(src, dst, send_sem, recv_sem, dev):
    return pltpu.make_async_remote_copy(src_ref=src, dst_ref=dst, send_sem=send_sem, recv_sem=recv_sem,
                                        device_id=dev, device_id_type=MESH)


def _full_view_shape(shard_shape, ax):
    _, r, c = shard_shape
    return (2, r // 2, c * N_CHIPS) if ax == 2 else (N_CHIPS, 2, r // 2, c)


def _piece(ref, ax, j, h, cs):
    if ax == 2:
        return ref.at[h, :, pl.ds(pl.multiple_of(j * cs, cs), cs)]
    return ref.at[j, h]


def _chip_block(ref, ax, j, cs):
    if ax == 2:
        return ref.at[:, :, pl.ds(pl.multiple_of(j * cs, cs), cs)]
    return ref.at[j]


SEM_SPEC = pl.BlockSpec(memory_space=pltpu.SEMAPHORE)
ANY_SPEC = pl.BlockSpec(memory_space=pl.ANY)
SPLIT = pltpu.CompilerParams(has_side_effects=pltpu.SideEffectType.DATAFLOW_SIDE_EFFECTING)


def _exchange(kind, name, bufs, build, n_sems, after=None, sems=None):
    n = len(bufs)
    if kind == 'sync':
        def body(*refs):
            mine, theirs = build(refs[n:2 * n], refs[2 * n], refs[2 * n + 1])
            for cp in mine:
                cp.start()
            for cp in theirs:
                cp.wait_recv()
            for cp in mine:
                cp.wait_send()

        return list(_pcall(
            body, in_specs=[HBM_SPEC] * n, out_specs=[HBM_SPEC] * n,
            out_shape=[jax.ShapeDtypeStruct(v.shape, v.dtype) for v in bufs], input_output_aliases={t: t for t in range(n)},
            scratch_shapes=[pltpu.SemaphoreType.DMA((n_sems,)), pltpu.SemaphoreType.DMA((n_sems,))],
            name=name, compiler_params=COMM)(*bufs))
    if kind == 'start':
        def body(*refs):
            mine, _ = build(refs[n + 3:2 * n + 3], refs[n + 1], refs[n + 2])
            for cp in mine:
                cp.start()
            refs[2 * n + 3][...] = jnp.zeros_like(refs[2 * n + 3])

        outs = _pcall(
            body, in_specs=[HBM_SPEC] * n + [ANY_SPEC],
            out_specs=[SEM_SPEC, SEM_SPEC] + [HBM_SPEC] * n + [pl.BlockSpec(memory_space=pltpu.VMEM)],
            out_shape=[pltpu.SemaphoreType.DMA((n_sems,)), pltpu.SemaphoreType.DMA((n_sems,))]
            + [pltpu.HBM(v.shape, v.dtype) for v in bufs] + [jax.ShapeDtypeStruct((8, LANES), F32)],
            input_output_aliases={t: 2 + t for t in range(n)}, name=name,
            compiler_params=SPLIT)(*[pltpu.with_memory_space_constraint(v, pltpu.HBM) for v in bufs], after)
        return (outs[0], outs[1]), list(outs[2:2 + n]), outs[2 + n]

    def body(*refs):
        mine, theirs = build(refs[:n], refs[n], refs[n + 1])
        for cp in mine:
            cp.wait_send()
        for cp in theirs:
            cp.wait_recv()

    return list(_pcall(
        body, in_specs=[HBM_SPEC] * n + [SEM_SPEC, SEM_SPEC, ANY_SPEC], out_specs=[HBM_SPEC] * n,
        out_shape=[pltpu.HBM(v.shape, v.dtype) for v in bufs], input_output_aliases={t: t for t in range(n)},
        name=name, compiler_params=SPLIT)(*bufs, sems[0], sems[1], after))


def _gather_ici_copies(axes, shard_cols):
    def build(bufs, send_sems, recv_sems):
        x, y, c, chips = _position()
        me = 2 * x + y
        mine, theirs = [], []
        for t, ax in enumerate(axes):
            own = _piece(bufs[t], ax, me, c, shard_cols[t])
            for p, (px, py) in enumerate(chips):
                k = t * 3 + p
                got = _piece(bufs[t], ax, 2 * px + py, c, shard_cols[t])
                mine.append(_remote(own, own, send_sems.at[k], recv_sems.at[k], (px, py, c)))
                theirs.append(_remote(got, got, send_sems.at[k], recv_sems.at[k], (px, py, c)))
        return mine, theirs
    return build


def _gather_d2d_copies(axes, shard_cols):
    def build(bufs, send_sems, recv_sems):
        x, y, c, chips = _position()
        mine, theirs = [], []
        for t, ax in enumerate(axes):
            for p, (px, py) in enumerate(chips):
                k = t * 3 + p
                had = _piece(bufs[t], ax, 2 * px + py, c, shard_cols[t])
                got = _piece(bufs[t], ax, 2 * px + py, 1 - c, shard_cols[t])
                mine.append(_remote(had, had, send_sems.at[k], recv_sems.at[k], (x, y, 1 - c)))
                theirs.append(_remote(got, got, send_sems.at[k], recv_sems.at[k], (x, y, 1 - c)))
        return mine, theirs
    return build


def _grads_d2d_copies(n):
    def build(bufs, send_sems, recv_sems):
        x, y, c, _ = _position()
        mine = [_remote(bufs[t].at[:, 1 - c], bufs[n + t], send_sems.at[t], recv_sems.at[t], (x, y, 1 - c)) for t in range(n)]
        return mine, mine
    return build


def _grads_ici_copies(axes):
    n = len(axes)

    def build(bufs, send_sems, recv_sems):
        x, y, c, chips = _position()
        me = 2 * x + y

        def block(t, j):
            if axes[t] == 2:
                cs = bufs[n + t].shape[2]
                return bufs[t].at[0, :, pl.ds(pl.multiple_of(j * cs, cs), cs)]
            return bufs[t].at[j]

        mine, theirs = [], []
        for t in range(n):
            for p, (px, py) in enumerate(chips):
                k = t * 3 + p
                peer = 2 * px + py
                mine.append(_remote(block(t, peer), bufs[n + t].at[me], send_sems.at[k], recv_sems.at[k], (px, py, c)))
                theirs.append(_remote(block(t, peer), bufs[n + t].at[peer], send_sems.at[k], recv_sems.at[k], (px, py, c)))
        return mine, theirs
    return build


def _join_copies(bufs, send_sems, recv_sems):
    x, y, c, _ = _position()
    mine = [_remote(b.at[c], b.at[c], send_sems.at[t], recv_sems.at[t], (x, y, 1 - c)) for t, b in enumerate(bufs)]
    theirs = [_remote(b.at[1 - c], b.at[1 - c], send_sems.at[t], recv_sems.at[t], (x, y, 1 - c)) for t, b in enumerate(bufs)]
    return mine, theirs


def _grads_recv_shape(sm, ax):
    _, a, c = sm.shape
    return (N_CHIPS, a, c // N_CHIPS if ax == 2 else c)


def _gather_small(shard):
    nl, r, cs = shard.shape

    def body(in_ref, out_ref, send_sems, recv_sems, local_sem):
        x, y, c, chips = _position()

        def cols(j):
            return out_ref.at[:, :, pl.ds(pl.multiple_of(j * cs, cs), cs)]

        me = 2 * x + y
        loc = pltpu.make_async_copy(in_ref, cols(me), local_sem)
        loc.start()
        remote = [_remote(in_ref, cols(me), send_sems.at[p], recv_sems.at[p], (px, py, c)) for p, (px, py) in enumerate(chips)]
        for cp in remote:
            cp.start()
        for p, (px, py) in enumerate(chips):
            _remote(in_ref, cols(2 * px + py), send_sems.at[p], recv_sems.at[p], (px, py, c)).wait_recv()
        for cp in remote:
            cp.wait_send()
        loc.wait()

    return _pcall(
        body, in_specs=[HBM_SPEC], out_specs=HBM_SPEC, out_shape=jax.ShapeDtypeStruct((nl, r, cs * N_CHIPS), shard.dtype),
        scratch_shapes=[pltpu.SemaphoreType.DMA((3,)), pltpu.SemaphoreType.DMA((3,)), pltpu.SemaphoreType.DMA(())],
        name="gather_small", compiler_params=COMM)(shard)


def _small_copies(bufs, send_sems, recv_sems):
    pack, land = bufs
    x, y, cc, _ = _position()
    me = 4 * x + 2 * y + cc
    mine, theirs = [], []
    for k in range(1, N_DEV):
        px, py, pc = x ^ ((k >> 2) & 1), y ^ ((k >> 1) & 1), cc ^ (k & 1)
        mine.append(_remote(pack, land.at[me], send_sems.at[k - 1], recv_sems.at[k - 1], (px, py, pc)))
        theirs.append(_remote(pack, land.at[4 * px + 2 * py + pc], send_sems.at[k - 1], recv_sems.at[k - 1], (px, py, pc)))
    return mine, theirs


def _sum_devices(land, pack, dev):
    _, r, c = land.shape
    tr = _tile(r, (672, 512, 256, 128, 64, 8))

    def body(dev_ref, l_ref, p_ref, o_ref):
        me = dev_ref[0]
        acc = None
        for k in range(N_DEV):
            term = jnp.where(me == k, p_ref[...], l_ref[k])
            acc = term if acc is None else acc + term
        o_ref[...] = acc

    grid_spec = pltpu.PrefetchScalarGridSpec(
        num_scalar_prefetch=1, grid=(r // tr,),
        in_specs=[pl.BlockSpec((N_DEV, tr, c), lambda i, dv: (0, i, 0)), pl.BlockSpec((tr, c), lambda i, dv: (i, 0))],
        out_specs=pl.BlockSpec((tr, c), lambda i, dv: (i, 0)))
    return _pcall(body, grid_spec=grid_spec, out_shape=jax.ShapeDtypeStruct((r, c), F32), name="sum_devices",
                  compiler_params=_params(("parallel",)))(dev, land, pack)


def _dims(d):
    half = d // 2
    return half // HEAD_SB, half // HEAD_XA, 3, (5 * half) // HEAD_XA


def _layer_fwd(x, mem, weight, small, l, after=None):
    h_sb, h_xa, u_blk, q_blk = _dims(x.shape[1])

    def vec(name):
        return small[name][l].reshape(1, -1)

    def use(a, name, follows, mm_name, dtype=F32):
        wt = weight(name, follows)
        return _mm(a, wt, 'nn', dtype, mm_name, after=weight.token())

    h1 = _norm_fwd(x, vec('g_mix_pre'), None, BF16, "norm_mix_pre", after)
    proj = use(h1, 'w_in', h1, "mm_proj")
    o_sb, a_sb = _sb_fwd(proj, h_sb)
    b_st = small['b_s'][l].T
    o_gm = _gm_fwd(proj, vec('g_vnorm'), small['w_s'][l], b_st, u_blk)
    memn = _norm_fwd(mem, vec('g_mem'), None, BF16, "norm_mem")
    mem_kv = use(memn, 'w_mem_kv', o_sb, "mm_mem_kv")
    o_xa = _xa_fwd(proj, mem_kv, q_blk, h_xa)
    zg = use(h1, 'w_gate', o_sb, "mm_gate", BF16)
    branches = [use(o, wn, zg, "mm_branch") for o, wn in ((o_sb, 'w_br_sb'), (o_gm, 'w_br_gm'), (o_xa, 'w_br_xa'))]
    merged = _merge_fwd(zg, vec('b_gate'), branches)
    y1 = use(merged, 'w_out', zg, "mm_out")
    x1 = _norm_fwd(y1, vec('g_mix_post'), x, F32, "norm_mix_post")
    h2 = _norm_fwd(x1, vec('g_ffn_pre'), None, BF16, "norm_ffn_pre")
    up = use(h2, 'w_up', h2, "mm_up", BF16)
    act = _cg_fwd(up, weight('conv_w', up), vec('conv_b'))
    y2 = use(act, 'w_down', act, "mm_down")
    x2 = _norm_fwd(y2, vec('g_ffn_post'), x1, F32, "norm_ffn_post")
    saved = dict(x0=x, h1=h1, proj=proj, o_sb=o_sb, a_sb=a_sb, o_gm=o_gm, o_xa=o_xa, memn=memn, mem_kv=mem_kv, zg=zg,
                 branches=branches, merged=merged, y1=y1, x1=x1, h2=h2, up=up, act=act, y2=y2, b_st=b_st)
    return x2, saved


def _layer_bwd(dx, mem, sv, full, small, l, after, emit):
    h_sb, h_xa, u_blk, q_blk = _dims(dx.shape[1])

    def vec(name):
        return small[name][l].reshape(1, -1)

    gb, gs = {}, {}
    dy2, gs['g_ffn_post'] = _norm_bwd(sv['y2'], vec('g_ffn_post'), [dx], None, BF16, "norm_ffn_post_bwd", after)
    gb['w_down'] = _mm(sv['act'], dy2, 'tn', BF16, "mm_down_dw")
    dact = _mm(dy2, full['w_down'], 'nt', F32, "mm_down_dx")
    dgate, dval, gs['conv_w'], gs['conv_b'] = _cg_bwd(sv['up'], full['conv_w'], vec('conv_b'), dact)
    gb['w_up'] = _mm(sv['h2'], [dgate, dval], 'tn', BF16, "mm_up_dw")
    token = emit(0, gb)
    dh2 = _mm([dgate, dval], full['w_up'], 'nt', F32, "mm_up_dx", after=token)
    token = emit.flush(dh2)
    dx1, gs['g_ffn_pre'] = _norm_bwd(sv['x1'], vec('g_ffn_pre'), [dh2], dx, F32, "norm_ffn_pre_bwd", token)
    dy1, gs['g_mix_post'] = _norm_bwd(sv['y1'], vec('g_mix_post'), [dx1], None, BF16, "norm_mix_post_bwd")
    gb['w_out'] = _mm(sv['merged'], dy1, 'tn', BF16, "mm_out_dw")
    dmerged = _mm(dy1, full['w_out'], 'nt', F32, "mm_out_dx")
    dzg, dbr, gs['b_gate'] = _merge_bwd(sv['zg'], vec('b_gate'), sv['branches'], dmerged)
    douts = []
    for o, db, wn in ((sv['o_sb'], dbr[0], 'w_br_sb'), (sv['o_gm'], dbr[1], 'w_br_gm'), (sv['o_xa'], dbr[2], 'w_br_xa')):
        gb[wn] = _mm(o, db, 'tn', BF16, "mm_branch_dw")
        douts.append(_mm(db, full[wn], 'nt', F32, "mm_branch_dx"))
    gb['w_gate'] = _mm(sv['h1'], dzg, 'tn', BF16, "mm_gate_dw")
    dq_xa, dk_xa, dv_xa = _xa_bwd(sv['proj'], sv['mem_kv'], douts[2], q_blk, h_xa)
    dmem_kv = jnp.concatenate([dk_xa, dv_xa], axis=1).astype(BF16)
    gb['w_mem_kv'] = _mm(sv['memn'], dmem_kv, 'tn', BF16, "mm_mem_kv_dw")
    token = emit(1, gb)
    dh1_gate = _mm(dzg, full['w_gate'], 'nt', F32, "mm_gate_dx", after=token)
    token = emit.flush(dh1_gate)
    dmemn = _mm(dmem_kv, full['w_mem_kv'], 'nt', F32, "mm_mem_kv_dx")
    _, gs['g_mem'] = _norm_bwd(mem, vec('g_mem'), [dmemn], None, BF16, "norm_mem_bwd")
    du, dv, gs['g_vnorm'], gs['w_s'], db_st = _gm_bwd(sv['proj'], vec('g_vnorm'), small['w_s'][l], sv['b_st'], douts[1], u_blk)
    gs['b_s'] = db_st.T
    dq, dk, dvv = _sb_bwd(sv['proj'], sv['a_sb'], douts[0], h_sb, token)
    dproj = jnp.concatenate([dq, dk, dvv, du, dv, dq_xa], axis=1).astype(BF16)
    gb['w_in'] = _mm(sv['h1'], dproj, 'tn', BF16, "mm_proj_dw")
    token = emit(2, gb)
    dh1_proj = _mm(dproj, full['w_in'], 'nt', F32, "mm_proj_dx")
    dx0, gs['g_mix_pre'] = _norm_bwd(sv['x0'], vec('g_mix_pre'), [dh1_gate, dh1_proj], dx1, F32, "norm_mix_pre_bwd", token)
    return dx0, gs


class _Given:
    def __init__(self, full):
        self.full = full

    def __call__(self, name, follows):
        return self.full[name]

    def token(self):
        return None


def _local_step(x, mem, target, full, small):
    n_layers = len(full['w_in'])
    saved = []
    for l in range(n_layers):
        x, sv = _layer_fwd(x, mem, _Given({n: full[n][l] for n in full}), small, l)
        saved.append(sv)
    sq, dx = _loss_head(x, target)
    gbig = {n: [None] * n_layers for n in BIG}
    gsmall = {n: [None] * n_layers for n in SMALL + ['conv_w']}
    class Collect:
        def __init__(self, l):
            self.l = l

        def __call__(self, g, gb):
            for n in BWD_GROUPS[g]:
                gbig[n][self.l] = gb[n]

        def flush(self, follows):
            return None

    for l in reversed(range(n_layers)):
        dx, gs = _layer_bwd(dx, mem, saved[l], {n: full[n][l] for n in full}, small, l, None, Collect(l))
        for n in gs:
            gsmall[n][l] = gs[n]
    return sq, dx, gbig, gsmall


def _pack(arrays, rows_multiple):
    flat = jnp.concatenate([a.reshape(-1).astype(F32) for a in arrays])
    rows = -(-flat.shape[0] // LANES)
    rows = -(-rows // rows_multiple) * rows_multiple
    return jnp.pad(flat, (0, rows * LANES - flat.shape[0])).reshape(rows, LANES)


def _unpack(pack, like):
    flat = pack.reshape(-1)
    out, off = [], 0
    for a in like:
        out.append(flat[off:off + a.size].reshape(a.shape))
        off += a.size
    return out


def _grad_view(g, ax):
    r, c = g.shape
    return g.reshape(1, 2, r // 2, c) if ax == 2 else g.reshape(N_CHIPS, 2, r // (2 * N_CHIPS), c)


def kernel(x, mem, g_mix_pre, w_in, g_vnorm, w_s, b_s, g_mem, w_mem_kv, w_gate, b_gate, w_br_sb, w_br_gm, w_br_xa, w_out, g_mix_post, g_ffn_pre, w_up, conv_w, conv_b, w_down, g_ffn_post, loss_target, m_g_mix_pre, m_w_in, m_g_vnorm, m_w_s, m_b_s, m_g_mem, m_w_mem_kv, m_w_gate, m_b_gate, m_w_br_sb, m_w_br_gm, m_w_br_xa, m_w_out, m_g_mix_post, m_g_ffn_pre, m_w_up, m_conv_w, m_conv_b, m_w_down, m_g_ffn_post, v_g_mix_pre, v_w_in, v_g_vnorm, v_w_s, v_b_s, v_g_mem, v_w_mem_kv, v_w_gate, v_b_gate, v_w_br_sb, v_w_br_gm, v_w_br_xa, v_w_out, v_g_mix_post, v_g_ffn_pre, v_w_up, v_conv_w, v_conv_b, v_w_down, v_g_ffn_post):
    w = dict(g_mix_pre=g_mix_pre, w_in=w_in, g_vnorm=g_vnorm, w_s=w_s, b_s=b_s, g_mem=g_mem, w_mem_kv=w_mem_kv,
             w_gate=w_gate, b_gate=b_gate, w_br_sb=w_br_sb, w_br_gm=w_br_gm, w_br_xa=w_br_xa, w_out=w_out,
             g_mix_post=g_mix_post, g_ffn_pre=g_ffn_pre, w_up=w_up, conv_w=conv_w, conv_b=conv_b, w_down=w_down,
             g_ffn_post=g_ffn_post)
    m = dict(g_mix_pre=m_g_mix_pre, w_in=m_w_in, g_vnorm=m_g_vnorm, w_s=m_w_s, b_s=m_b_s, g_mem=m_g_mem,
             w_mem_kv=m_w_mem_kv, w_gate=m_w_gate, b_gate=m_b_gate, w_br_sb=m_w_br_sb, w_br_gm=m_w_br_gm,
             w_br_xa=m_w_br_xa, w_out=m_w_out, g_mix_post=m_g_mix_post, g_ffn_pre=m_g_ffn_pre, w_up=m_w_up,
             conv_w=m_conv_w, conv_b=m_conv_b, w_down=m_w_down, g_ffn_post=m_g_ffn_post)
    v = dict(g_mix_pre=v_g_mix_pre, w_in=v_w_in, g_vnorm=v_g_vnorm, w_s=v_w_s, b_s=v_b_s, g_mem=v_g_mem,
             w_mem_kv=v_w_mem_kv, w_gate=v_w_gate, b_gate=v_b_gate, w_br_sb=v_w_br_sb, w_br_gm=v_w_br_gm,
             w_br_xa=v_w_br_xa, w_out=v_w_out, g_mix_post=v_g_mix_post, g_ffn_pre=v_g_ffn_pre, w_up=v_w_up,
             conv_w=v_conv_w, conv_b=v_conv_b, w_down=v_w_down, g_ffn_post=v_g_ffn_post)
    n_layers = w_in.shape[0]
    d = x.shape[-1]
    core = lax.axis_index("c").astype(jnp.int32).reshape(1)
    chip = (2 * lax.axis_index("x") + lax.axis_index("y")).astype(jnp.int32).reshape(1)
    small = {n: w[n] for n in SMALL}
    xs, mems, target = x[0], mem[0], loss_target[0]

    conv_w_full = _gather_small(conv_w)

    def as_full(vw, ax):
        return vw.reshape(-1, vw.shape[-1]) if ax == 1 else vw.reshape(vw.shape[0] * vw.shape[1], vw.shape[2])

    stages = {}
    keys = [(l, g) for l in range(n_layers) for g in range(len(FWD_GROUPS))]

    def start_gathers(l, token):
        for g, names in enumerate(FWD_GROUPS):
            ax_g = [BIG_AXIS[n] for n in names]
            cols_g = [w[n].shape[2] for n in names]
            views = [_place_own(w[n], l, chip, ax, token) for n, ax in zip(names, ax_g)]
            sems, views, token = _exchange('start', f"gather_ici_start_{l}_{g}", views, _gather_ici_copies(ax_g, cols_g),
                                           3 * len(names), after=token)
            stages[l, g] = dict(names=names, ax=ax_g, cols=cols_g, views=views, ici=sems, d2d=None, full=None)
        return token

    token = start_gathers(0, conv_w_full)

    def cross_cores(key, follows):
        st, (l, g) = stages[key], key
        n3 = 3 * len(st['names'])
        views = _exchange('wait', f"gather_ici_wait_{l}_{g}", st['views'], _gather_ici_copies(st['ax'], st['cols']), n3,
                          after=follows, sems=st['ici'])
        st['d2d'], st['views'], tok = _exchange('start', f"gather_d2d_start_{l}_{g}", views,
                                                _gather_d2d_copies(st['ax'], st['cols']), n3, after=core)
        return tok

    class Weights:
        def __init__(self, l):
            self.l, self.tok = l, None

        def __call__(self, name, follows):
            if name == 'conv_w':
                return conv_w_full[self.l]
            key = (self.l, [g for g, names in enumerate(FWD_GROUPS) if name in names][0])
            if key[1] == 1 and (self.l + 1, 0) not in stages and self.l + 1 < n_layers:
                self.tok = start_gathers(self.l + 1, follows)
            st = stages[key]
            if st['full'] is None:
                if st['d2d'] is None:
                    cross_cores(key, follows)
                views = _exchange('wait', f"gather_d2d_wait_{key[0]}_{key[1]}", st['views'],
                                  _gather_d2d_copies(st['ax'], st['cols']), 3 * len(st['names']), after=follows, sems=st['d2d'])
                st['full'] = {n: as_full(vw, ax) for n, vw, ax in zip(st['names'], views, st['ax'])}
            nxt = keys.index(key) + 1
            if name == st['names'][-1] and nxt < len(keys) and key != keys[0] and stages[keys[nxt]]['d2d'] is None:
                self.tok = cross_cores(keys[nxt], follows)
            return st['full'][name]

        def token(self):
            return self.tok

    fulls, saved = [], []
    for l in range(n_layers):
        xs, sv = _layer_fwd(xs, mems, Weights(l), small, l, token if l == 0 else None)
        fulls.append({n: stages[l, g]['full'][n] for g, names in enumerate(FWD_GROUPS) for n in names} | {'conv_w': conv_w_full[l]})
        saved.append(sv)
    sq, dx = _loss_head(xs, target)
    loss = lax.psum(0.5 * jnp.sum(sq) / d, ("x", "y", "c"))

    sent = []

    def to_chips(l, g, names, ax_g, bufs, after):
        n = len(names)
        sums = [_sum_halves(dv, th, core, "sum_halves") for dv, th in zip(bufs[:n], bufs[n:])]
        lands = [lax.empty(_grads_recv_shape(sm, ax), sm.dtype) for sm, ax in zip(sums, ax_g)]
        sems, bufs, tok = _exchange('start', f"grads_ici_start_{l}_{g}", sums + lands, _grads_ici_copies(ax_g), 3 * n,
                                    after=core if after is None else after)
        sent.append((l, g, names, ax_g, bufs, sems))
        return tok

    class Grads:
        def __init__(self, l):
            self.l, self.crossing = l, None

        def __call__(self, g, gb):
            names = BWD_GROUPS[g]
            ax_g = [BIG_AXIS[n] for n in names]
            n = len(names)
            dwvs = [_grad_view(gb[nm], ax) for nm, ax in zip(names, ax_g)]
            lands = [lax.empty((dv.shape[0],) + dv.shape[2:], dv.dtype) for dv in dwvs]
            if g + 1 < len(BWD_GROUPS):
                sems, bufs, tok = _exchange('start', f"grads_d2d_start_{self.l}_{g}", dwvs + lands, _grads_d2d_copies(n), n,
                                            after=core)
                self.crossing = (g, names, ax_g, bufs, sems)
                return tok
            return to_chips(self.l, g, names, ax_g, _exchange('sync', "grads_d2d", dwvs + lands, _grads_d2d_copies(n), n), None)

        def flush(self, follows):
            if self.crossing is None:
                return None
            (g, names, ax_g, bufs, sems), self.crossing = self.crossing, None
            bufs = _exchange('wait', f"grads_d2d_wait_{self.l}_{g}", bufs, _grads_d2d_copies(len(names)), len(names),
                             after=follows, sems=sems)
            return to_chips(self.l, g, names, ax_g, bufs, None)

    gsmall = {n: [None] * n_layers for n in SMALL + ['conv_w']}
    for l in reversed(range(n_layers)):
        dx, gs = _layer_bwd(dx, mems, saved[l], fulls[l], small, l, None, Grads(l))
        for n in gs:
            gsmall[n][l] = gs[n]

    names_small = SMALL + ['conv_w']
    small_full = [jnp.stack(gsmall[n]).reshape(w[n].shape) for n in SMALL]
    conv_w_grad = jnp.stack(gsmall['conv_w'])
    pack = _pack(small_full + [conv_w_grad], 8)
    small_sems, small_bufs, _ = _exchange('start', "small_start", [pack, lax.empty((N_DEV,) + pack.shape, F32)], _small_copies,
                                          N_DEV - 1, after=dx)

    def small_update(follows):
        pk, land = _exchange('wait', "small_wait", small_bufs, _small_copies, N_DEV - 1, after=follows, sems=small_sems)
        device = (4 * lax.axis_index("x") + 2 * lax.axis_index("y") + lax.axis_index("c")).astype(jnp.int32).reshape(1)
        summed = _sum_devices(land, pk, device)
        *small_g, conv_w_g = _unpack(summed, small_full + [conv_w_grad])
        shard = conv_w.shape[-1]
        conv_w_g = lax.dynamic_slice_in_dim(conv_w_g, chip[0] * shard, shard, axis=2)
        packed = [_pack([p[n] for n in names_small], 256) for p in (w, m, v)]
        gpack = _pack(small_g + [conv_w_g], 256)
        res = _adamw(packed[0][None], packed[1][None], packed[2][None], gpack, 0, None, "adamw_small")
        like = [w[n] for n in names_small]
        unpacked = [_unpack(r[0], like) for r in res]
        return {n: tuple(u[i] for u in unpacked) for i, n in enumerate(names_small)}

    out = {}

    def update(joining, follows):
        l, g, names, halves, sems = joining
        halves = _exchange('wait', f"join_wait_{l}_{g}", halves, _join_copies, len(names), after=follows, sems=sems)
        last = None
        for n, hv in zip(names, halves):
            out[n] = _adamw(w[n], m[n], v[n], hv.reshape(w[n].shape[1:]), l, out.get(n), "adamw_big", last)
            last = out[n][0]
        return last

    joining, follows = None, dx
    for k, (l, g, names, ax_g, bufs, sems) in enumerate(sent):
        n = len(names)
        bufs = _exchange('wait', f"grads_ici_wait_{l}_{g}", bufs, _grads_ici_copies(ax_g), 3 * n, after=follows, sems=sems)
        halves = [_sum_chips(r, sm, chip, core, ax) for sm, r, ax in zip(bufs[:n], bufs[n:], ax_g)]
        jsems, halves, tok = _exchange('start', f"join_start_{l}_{g}", halves, _join_copies, n, after=core)
        if joining is not None:
            follows = update(joining, tok)
        joining = (l, g, names, halves, jsems)
    out.update(small_update(update(joining, follows)))

    return (loss, dx[None], *[out[n][0] for n in WEIGHTS], *[out[n][1] for n in WEIGHTS],
            *[out[n][2] for n in WEIGHTS], *[out[n][3] for n in WEIGHTS])
```

```python
import functools
import math

import jax
import jax.numpy as jnp
from jax import lax
from jax.experimental import pallas as pl
from jax.experimental.pallas import tpu as pltpu

F32 = jnp.float32
BF16 = jnp.bfloat16
EPS = 1e-6
HEAD_SB = 128
GROUP_GM = 128
CHUNK = 64
HEAD_XA = 256
CONV_TAPS = 3
N_CHIPS = 4
N_DEV = 8
LANES = 128
MIB = 1024 * 1024
VMEM_LIMIT = 48 * MIB
SPLITS = 1

ADAM_LR = 0.001
ADAM_B1 = 0.9
ADAM_B2 = 0.999
ADAM_EPS = 1e-08
ADAM_WD = 0.01
ADAM_STEP = 10

WEIGHTS = ['g_mix_pre', 'w_in', 'g_vnorm', 'w_s', 'b_s', 'g_mem', 'w_mem_kv', 'w_gate', 'b_gate', 'w_br_sb',
           'w_br_gm', 'w_br_xa', 'w_out', 'g_mix_post', 'g_ffn_pre', 'w_up', 'conv_w', 'conv_b', 'w_down',
           'g_ffn_post']
BIG_AXIS = {'w_in': 2, 'w_mem_kv': 1, 'w_gate': 2, 'w_br_sb': 2, 'w_br_gm': 2, 'w_br_xa': 2, 'w_out': 1,
            'w_up': 2, 'w_down': 1}
BIG = list(BIG_AXIS)
FWD_GROUPS = [['w_in'], ['w_mem_kv', 'w_gate'], ['w_br_sb', 'w_br_gm', 'w_br_xa', 'w_out'], ['w_up'], ['w_down']]
BWD_GROUPS = [['w_down', 'w_up'], ['w_out', 'w_br_sb', 'w_br_gm', 'w_br_xa', 'w_gate', 'w_mem_kv'], ['w_in']]
SMALL = ['g_mix_pre', 'g_vnorm', 'w_s', 'b_s', 'g_mem', 'b_gate', 'g_mix_post', 'g_ffn_pre', 'conv_b', 'g_ffn_post']
MESH = pl.DeviceIdType.MESH


def _pcall(body, **kw):
    return pl.pallas_call(body, **kw)


def _params(sem=None, vmem=VMEM_LIMIT):
    return pltpu.CompilerParams(dimension_semantics=sem, vmem_limit_bytes=vmem)


def _tile(n, cands):
    for c in cands:
        if n % c == 0:
            return c
    return n


_GELU_C = math.sqrt(2.0 / math.pi)
_GELU_A = 0.044715


def _gelu(x):
    return 0.5 * x * (1.0 + jnp.tanh(_GELU_C * (x + _GELU_A * (x * x * x))))


def _gelu_and_grad(x):
    x2 = x * x
    t = jnp.tanh(_GELU_C * (x + _GELU_A * (x2 * x)))
    val = 0.5 * x * (1.0 + t)
    grad = 0.5 * (1.0 + t) + 0.5 * x * (1.0 - t * t) * (_GELU_C * (1.0 + 3.0 * _GELU_A * x2))
    return val, grad


def _softplus(z):
    return jnp.maximum(z, 0.0) + jnp.log(1.0 + jnp.exp(-jnp.abs(z)))


def _dot(a, b):
    return jnp.dot(a, b, preferred_element_type=F32)


def _dot_nt(a, b):
    return lax.dot_general(a, b, (((1,), (1,)), ((), ())), preferred_element_type=F32)


def _dot_tn(a, b):
    return lax.dot_general(a, b, (((0,), (0,)), ((), ())), preferred_element_type=F32)


def _split_dot(a, m):
    out = None
    rest = a
    for _ in range(SPLITS):
        piece = rest.astype(BF16)
        rest = rest - piece.astype(F32)
        term = _dot(piece, m)
        out = term if out is None else out + term
    return out


def _mm(a, b, mode, out_dtype, name, tm=None, tn=None, tk=None, after=None):
    a_parts = list(a) if isinstance(a, (list, tuple)) else [a]
    b_parts = list(b) if isinstance(b, (list, tuple)) else [b]
    assert len(a_parts) == 1 or mode == 'nt'
    assert len(b_parts) == 1 or mode == 'tn'
    na, nb = len(a_parts), len(b_parts)
    if mode == 'nn':
        (m, kc), (kc2, n) = a_parts[0].shape, b_parts[0].shape
    elif mode == 'nt':
        (m, kp), (n, kc2) = a_parts[0].shape, b_parts[0].shape
        kc = kp * na
    else:
        (kc, m), (kc2, npiece) = a_parts[0].shape, b_parts[0].shape
        n = npiece * nb
    assert kc == kc2, (a_parts[0].shape, b_parts[0].shape, mode)
    tm = tm or _tile(m, (1024, 512, 256, 128))
    tn = tn or _tile(n // nb, (1024, 512, 256, 128) if na == 1 else (512, 256, 128))
    k_max = min(3072, (24 * MIB) // (4 * tm * na))
    tk = tk or ((kc // na) if kc // na <= k_max else _tile(kc // na, [c for c in (3072, 2816, 2048, 1536, 1408, 1024, 512) if c <= k_max]))
    nk = kc // tk
    k_per = nk // na
    n_per = (n // tn) // nb
    dot = {'nn': _dot, 'nt': _dot_nt, 'tn': _dot_tn}[mode]

    def within(idx, p, per):
        return jnp.clip(idx - p * per, 0, per - 1)

    if mode == 'tn':
        a_specs = [pl.BlockSpec((tk, tm), lambda i, j, k: (k, i))]
        b_specs = [pl.BlockSpec((tk, tn), functools.partial(lambda i, j, k, p: (k, within(j, p, n_per)), p=p)) for p in range(nb)]
    else:
        a_specs = [pl.BlockSpec((tm, tk), functools.partial(lambda i, j, k, p: (i, within(k, p, k_per)), p=p)) for p in range(na)]
        b_specs = [pl.BlockSpec((tn, tk), lambda i, j, k: (j, k)) if mode == 'nt' else pl.BlockSpec((tk, tn), lambda i, j, k: (k, j))]

    extra = [] if after is None else [after]
    extra_specs = [pl.BlockSpec(memory_space=pl.ANY)] * len(extra)

    def product(a_refs, b_refs, store):
        if na == 1 and nb == 1:
            store(dot(a_refs[0][...].astype(BF16), b_refs[0][...].astype(BF16)))
            return
        which = (pl.program_id(2) // k_per) if na > 1 else (pl.program_id(1) // n_per)
        for p in range(max(na, nb)):
            @pl.when(which == p)
            def _(p=p):
                store(dot(a_refs[p if na > 1 else 0][...].astype(BF16), b_refs[p if nb > 1 else 0][...].astype(BF16)))

    if nk == 1:
        def body(*refs):
            o_ref = refs[-1]

            def store(part):
                o_ref[...] = part.astype(o_ref.dtype)

            product(refs[:na], refs[na:na + nb], store)
        scratch = []
    else:
        def body(*refs):
            o_ref, acc_ref = refs[-2], refs[-1]
            k = pl.program_id(2)

            def store(part):
                @pl.when(k == 0)
                def _():
                    acc_ref[...] = part

                @pl.when(k > 0)
                def _():
                    acc_ref[...] += part

            product(refs[:na], refs[na:na + nb], store)

            @pl.when(k == nk - 1)
            def _():
                o_ref[...] = acc_ref[...].astype(o_ref.dtype)
        scratch = [pltpu.VMEM((tm, tn), F32)]

    return _pcall(
        body, grid=(m // tm, n // tn, nk), in_specs=a_specs + b_specs + extra_specs,
        out_specs=pl.BlockSpec((tm, tn), lambda i, j, k: (i, j)),
        out_shape=jax.ShapeDtypeStruct((m, n), out_dtype), scratch_shapes=scratch, name=name,
        compiler_params=_params(("parallel", "parallel", "arbitrary")))(*a_parts, *b_parts, *extra)


def _norm_fwd(x, g, res, out_dtype, name, after=None):
    s, d = x.shape
    tr = _tile(s, (256, 128))
    has_res = res is not None
    has_after = after is not None

    def body(*refs):
        x_ref, g_ref = refs[0], refs[1]
        o_ref = refs[-1]
        xv = x_ref[...]
        y = xv * lax.rsqrt(jnp.mean(xv * xv, axis=-1, keepdims=True) + EPS) * g_ref[...]
        if has_res:
            y = y + refs[2][...]
        o_ref[...] = y.astype(o_ref.dtype)

    row = pl.BlockSpec((tr, d), lambda i: (i, 0))
    ins = [x, g] + ([res] if has_res else []) + ([after] if has_after else [])
    return _pcall(
        body, grid=(s // tr,),
        in_specs=[row, pl.BlockSpec((1, d), lambda i: (0, 0))] + ([row] if has_res else [])
        + ([pl.BlockSpec(memory_space=pl.ANY)] if has_after else []),
        out_specs=row, out_shape=jax.ShapeDtypeStruct((s, d), out_dtype), name=name,
        compiler_params=_params(("parallel",)))(*ins)


def _norm_bwd(x, g, douts, dres, out_dtype, name, after=None):
    s, d = x.shape
    tr = _tile(s, (256, 128))
    nd = len(douts)
    has_res = dres is not None
    has_after = after is not None

    def body(*refs):
        x_ref, g_ref = refs[0], refs[1]
        dx_ref, dg_ref = refs[-2], refs[-1]
        dout = refs[2][...].astype(F32)
        for r in refs[3:2 + nd]:
            dout = dout + r[...].astype(F32)
        xv = x_ref[...]
        r = lax.rsqrt(jnp.mean(xv * xv, axis=-1, keepdims=True) + EPS)
        n = xv * r
        dn = dout * g_ref[...]
        dx = r * (dn - n * jnp.mean(dn * n, axis=-1, keepdims=True))
        if has_res:
            dx = dx + refs[2 + nd][...]
        dx_ref[...] = dx.astype(dx_ref.dtype)

        @pl.when(pl.program_id(0) == 0)
        def _():
            dg_ref[...] = jnp.zeros_like(dg_ref)

        dg_ref[...] += jnp.sum(dout * n, axis=0, keepdims=True)

    row = pl.BlockSpec((tr, d), lambda i: (i, 0))
    vec = pl.BlockSpec((1, d), lambda i: (0, 0))
    ins = [x, g] + list(douts) + ([dres] if has_res else []) + ([after] if has_after else [])
    return _pcall(
        body, grid=(s // tr,),
        in_specs=[row, vec] + [row] * (nd + int(has_res)) + ([pl.BlockSpec(memory_space=pl.ANY)] if has_after else []),
        out_specs=[row, vec],
        out_shape=[jax.ShapeDtypeStruct((s, d), out_dtype), jax.ShapeDtypeStruct((1, d), F32)], name=name,
        compiler_params=_params(("arbitrary",)))(*ins)


def _loss_head(y, target):
    s, d = y.shape
    tr = _tile(s, (256, 128))

    def body(y_ref, t_ref, sq_ref, dy_ref):
        e = y_ref[...] - t_ref[...]
        dy_ref[...] = e * (1.0 / d)

        @pl.when(pl.program_id(0) == 0)
        def _():
            sq_ref[...] = jnp.zeros_like(sq_ref)

        sq_ref[...] += jnp.sum(e * e, axis=0, keepdims=True)

    row = pl.BlockSpec((tr, d), lambda i: (i, 0))
    return _pcall(
        body, grid=(s // tr,), in_specs=[row, row], out_specs=[pl.BlockSpec((1, d), lambda i: (0, 0)), row],
        out_shape=[jax.ShapeDtypeStruct((1, d), F32), jax.ShapeDtypeStruct((s, d), F32)], name="loss_head",
        compiler_params=_params(("arbitrary",)))(y, target)


NEVER = -1e30
SB_QUERIES = 512


def _sb_sum_matrix(later):
    r = lax.broadcasted_iota(jnp.int32, (HEAD_SB, 2 * HEAD_SB), 0)
    c = lax.broadcasted_iota(jnp.int32, (HEAD_SB, 2 * HEAD_SB), 1)
    tri = jnp.where((r > c) if later else (r < c), 1.0, 0.0)
    return jnp.where(c < HEAD_SB, tri, 1.0).astype(BF16)


def _sb_mask(tq, q0, k0):
    row = lax.broadcasted_iota(jnp.int32, (tq, HEAD_SB), 0)
    col = lax.broadcasted_iota(jnp.int32, (tq, HEAD_SB), 1)
    return (k0 + col) < (q0 + row)


def _sb_fwd(proj, n_heads):
    s = proj.shape[0]
    tq = min(SB_QUERIES, s)
    per = tq // HEAD_SB
    scale = HEAD_SB ** -0.5

    def body(q_ref, k_ref, v_ref, o_ref, a_ref, acc_ref, c_ref):
        i = pl.program_id(1)
        q = q_ref[...].astype(BF16)
        sums = _sb_sum_matrix(True)
        acc_ref[...] = jnp.zeros_like(acc_ref)
        c_ref[...] = jnp.zeros_like(c_ref)
        last = (i + 1) * per - 1

        def scores(j, masked):
            off = pl.multiple_of(j * HEAD_SB, HEAD_SB)
            z = _dot_nt(q, k_ref[pl.ds(off, HEAD_SB), :].astype(BF16)) * scale
            sp = _softplus(z)
            logb = z - sp
            if masked:
                mask = _sb_mask(tq, i * tq, off)
                logb = jnp.where(mask, logb, NEVER)
                sp = jnp.where(mask, sp, 0.0)
            return logb, _split_dot(sp, sums)

        def values(j, logb, both):
            off = pl.multiple_of(j * HEAD_SB, HEAD_SB)
            c = c_ref[...]
            a = jnp.exp(logb - both[:, :HEAD_SB] - c).astype(BF16)
            a_ref[0, 0, j] = a
            acc_ref[...] += _dot(a, v_ref[pl.ds(off, HEAD_SB), :].astype(BF16))
            c_ref[...] = c + both[:, HEAD_SB:]

        def step(jj, carry, masked):
            j = last - jj
            nxt = scores(j, masked)
            values(jnp.minimum(j + 1, last), *carry)
            return nxt

        idle = (jnp.full((tq, HEAD_SB), NEVER, F32), jnp.zeros((tq, 2 * HEAD_SB), F32))
        carry = idle
        for jj in range(per):
            carry = step(jj, carry, True)
        def group(jg, carry):
            for u in range(per):
                carry = step(per * jg + u, carry, False)
            return carry

        carry = lax.fori_loop(1, i + 1, group, carry)
        values(0, *carry)
        o_ref[...] = acc_ref[...].astype(o_ref.dtype)

    h = n_heads
    blk = pl.BlockSpec((tq, HEAD_SB), lambda hh, i: (i, hh))
    return _pcall(
        body, grid=(h, s // tq),
        in_specs=[blk, pl.BlockSpec((s, HEAD_SB), lambda hh, i: (0, h + hh)),
                  pl.BlockSpec((s, HEAD_SB), lambda hh, i: (0, 2 * h + hh))],
        out_specs=[blk, pl.BlockSpec((1, 1, s // HEAD_SB, tq, HEAD_SB), lambda hh, i: (hh, i, 0, 0, 0))],
        out_shape=[jax.ShapeDtypeStruct((s, h * HEAD_SB), BF16),
                   jax.ShapeDtypeStruct((h, s // tq, s // HEAD_SB, tq, HEAD_SB), BF16)],
        scratch_shapes=[pltpu.VMEM((tq, HEAD_SB), F32), pltpu.VMEM((tq, HEAD_SB), F32)],
        name="sb_fwd", compiler_params=_params(("parallel", "arbitrary")))(proj, proj, proj)


def _sb_bwd(proj, a_saved, do, n_heads, after=None):
    s = proj.shape[0]
    tq = min(SB_QUERIES, s)
    per = tq // HEAD_SB
    scale = HEAD_SB ** -0.5
    follow = [] if after is None else [after]

    def body(q_ref, k_ref, v_ref, do_ref, a_ref, *rest):
        dq_ref, dk_ref, dv_ref, run_ref, acc_ref = rest[len(follow):]
        i = pl.program_id(1)

        @pl.when(i == 0)
        def _():
            dk_ref[...] = jnp.zeros_like(dk_ref)
            dv_ref[...] = jnp.zeros_like(dv_ref)

        q = q_ref[...].astype(BF16)
        dob = do_ref[...].astype(BF16)
        run_ref[...] = jnp.zeros_like(run_ref)
        acc_ref[...] = jnp.zeros_like(acc_ref)
        earlier = _sb_sum_matrix(False)
        first_diagonal = i * per

        def step(j, masked):
            off = pl.multiple_of(j * HEAD_SB, HEAD_SB)
            kb = k_ref[pl.ds(off, HEAD_SB), :].astype(BF16)
            vb = v_ref[pl.ds(off, HEAD_SB), :].astype(BF16)
            a = a_ref[0, 0, j]
            g = a.astype(F32) * _dot_nt(dob, vb)
            dv_ref[pl.ds(off, HEAD_SB), :] += _dot_tn(a, dob)
            z = _dot_nt(q, kb) * scale
            beta = 1.0 / (1.0 + jnp.exp(-z))
            both = _split_dot(g, earlier)
            p = run_ref[...]
            dz = (g * (1.0 - beta) - beta * (both[:, :HEAD_SB] + p)) * scale
            if masked:
                dz = jnp.where(_sb_mask(tq, i * tq, off), dz, 0.0)
            dzb = dz.astype(BF16)
            dk_ref[pl.ds(off, HEAD_SB), :] += _dot_tn(dzb, q)
            acc_ref[...] += _dot(dzb, kb)
            run_ref[...] = p + both[:, HEAD_SB:]

        def group(jg, carry):
            for u in range(per):
                step(per * jg + u, False)
            return carry

        lax.fori_loop(0, i, group, 0)
        for u in range(per):
            step(first_diagonal + u, True)
        dq_ref[...] = acc_ref[...]

    h = n_heads
    blk = pl.BlockSpec((tq, HEAD_SB), lambda hh, i: (i, hh))
    col_blk = pl.BlockSpec((s, HEAD_SB), lambda hh, i: (0, hh))
    shape = jax.ShapeDtypeStruct((s, h * HEAD_SB), F32)
    return _pcall(
        body, grid=(h, s // tq),
        in_specs=[blk, pl.BlockSpec((s, HEAD_SB), lambda hh, i: (0, h + hh)),
                  pl.BlockSpec((s, HEAD_SB), lambda hh, i: (0, 2 * h + hh)), blk,
                  pl.BlockSpec((1, 1, s // HEAD_SB, tq, HEAD_SB), lambda hh, i: (hh, i, 0, 0, 0))]
        + [pl.BlockSpec(memory_space=pl.ANY)] * len(follow),
        out_specs=[blk, col_blk, col_blk], out_shape=[shape, shape, shape],
        scratch_shapes=[pltpu.VMEM((tq, HEAD_SB), F32), pltpu.VMEM((tq, HEAD_SB), F32)],
        name="sb_bwd", compiler_params=_params(("parallel", "arbitrary")))(proj, proj, proj, do, a_saved, *follow)


def _gm_mask():
    t = lax.broadcasted_iota(jnp.int32, (GROUP_GM, GROUP_GM), 0)
    s = lax.broadcasted_iota(jnp.int32, (GROUP_GM, GROUP_GM), 1)
    shift = CHUNK.bit_length() - 1
    return (s >> shift) <= (t >> shift)


def _gm_fwd(proj, g_vnorm, w_s, b_st, u_blk):
    s = proj.shape[0]
    groups = w_s.shape[0]
    w = groups * GROUP_GM

    def body(u_ref, v_ref, gv_ref, ws_ref, bst_ref, o_ref):
        ug = _gelu(u_ref[...])
        vg = _gelu(v_ref[...])
        vn = vg * lax.rsqrt(jnp.mean(vg * vg, axis=-1, keepdims=True) + EPS) * gv_ref[...]
        vnb = vn.astype(BF16)
        mask = _gm_mask()
        for g in range(groups):
            sl = slice(g * GROUP_GM, (g + 1) * GROUP_GM)
            wm = jnp.where(mask, ws_ref[g], 0.0).astype(BF16)
            mixed = _dot(wm, vnb[:, sl]) + bst_ref[:, g:g + 1]
            o_ref[:, sl] = (ug[:, sl] * mixed).astype(o_ref.dtype)

    return _pcall(
        body, grid=(s // GROUP_GM,),
        in_specs=[pl.BlockSpec((GROUP_GM, w), lambda c: (c, u_blk)), pl.BlockSpec((GROUP_GM, w), lambda c: (c, u_blk + 1)),
                  pl.BlockSpec((1, w), lambda c: (0, 0)), pl.BlockSpec((groups, GROUP_GM, GROUP_GM), lambda c: (0, 0, 0)),
                  pl.BlockSpec((GROUP_GM, groups), lambda c: (0, 0))],
        out_specs=pl.BlockSpec((GROUP_GM, w), lambda c: (c, 0)),
        out_shape=jax.ShapeDtypeStruct((s, w), BF16), name="gm_fwd",
        compiler_params=_params(("parallel",)))(proj, proj, g_vnorm, w_s, b_st)


def _gm_bwd(proj, g_vnorm, w_s, b_st, do, u_blk):
    s = proj.shape[0]
    groups = w_s.shape[0]
    w = groups * GROUP_GM

    def body(u_ref, v_ref, gv_ref, ws_ref, bst_ref, do_ref, du_ref, dv_ref, dgv_ref, dws_ref, dbst_ref, dvn_ref):
        @pl.when(pl.program_id(0) == 0)
        def _():
            dgv_ref[...] = jnp.zeros_like(dgv_ref)
            dws_ref[...] = jnp.zeros_like(dws_ref)
            dbst_ref[...] = jnp.zeros_like(dbst_ref)

        ug, ugrad = _gelu_and_grad(u_ref[...])
        vg, vgrad = _gelu_and_grad(v_ref[...])
        r = lax.rsqrt(jnp.mean(vg * vg, axis=-1, keepdims=True) + EPS)
        n = vg * r
        gv = gv_ref[...]
        vnb = (n * gv).astype(BF16)
        dout = do_ref[...]
        mask = _gm_mask()
        for g in range(groups):
            sl = slice(g * GROUP_GM, (g + 1) * GROUP_GM)
            wm = jnp.where(mask, ws_ref[g], 0.0).astype(BF16)
            mixed = _dot(wm, vnb[:, sl]) + bst_ref[:, g:g + 1]
            dmixed = dout[:, sl] * ug[:, sl]
            du_ref[:, sl] = dout[:, sl] * mixed * ugrad[:, sl]
            dbst_ref[:, g:g + 1] += jnp.sum(dmixed, axis=1, keepdims=True)
            dmb = dmixed.astype(BF16)
            dws_ref[g] += jnp.where(mask, _dot_nt(dmb, vnb[:, sl]), 0.0)
            dvn_ref[:, sl] = _dot_tn(wm, dmb)
        dvn = dvn_ref[...]
        dgv_ref[...] += jnp.sum(dvn * n, axis=0, keepdims=True)
        dn = dvn * gv
        dvg = r * (dn - n * jnp.mean(dn * n, axis=-1, keepdims=True))
        dv_ref[...] = dvg * vgrad

    rowb = pl.BlockSpec((GROUP_GM, w), lambda c: (c, 0))
    vec = pl.BlockSpec((1, w), lambda c: (0, 0))
    wsb = pl.BlockSpec((groups, GROUP_GM, GROUP_GM), lambda c: (0, 0, 0))
    bsb = pl.BlockSpec((GROUP_GM, groups), lambda c: (0, 0))
    return _pcall(
        body, grid=(s // GROUP_GM,),
        in_specs=[pl.BlockSpec((GROUP_GM, w), lambda c: (c, u_blk)), pl.BlockSpec((GROUP_GM, w), lambda c: (c, u_blk + 1)),
                  vec, wsb, bsb, rowb],
        out_specs=[rowb, rowb, vec, wsb, bsb],
        out_shape=[jax.ShapeDtypeStruct((s, w), F32), jax.ShapeDtypeStruct((s, w), F32), jax.ShapeDtypeStruct((1, w), F32),
                   jax.ShapeDtypeStruct((groups, GROUP_GM, GROUP_GM), F32), jax.ShapeDtypeStruct((GROUP_GM, groups), F32)],
        scratch_shapes=[pltpu.VMEM((GROUP_GM, w), F32)], name="gm_bwd",
        compiler_params=_params(("arbitrary",)))(proj, proj, g_vnorm, w_s, b_st, do)


def _xa_fwd(proj, mem_kv, q_blk, n_heads):
    s = proj.shape[0]
    nm = mem_kv.shape[0]
    tq = _tile(s, (512, 256, 128))
    scale = HEAD_XA ** -0.5

    def body(q_ref, k_ref, v_ref, o_ref):
        z = _dot_nt(q_ref[...].astype(BF16), k_ref[...].astype(BF16)) * scale
        z = z - jnp.max(z, axis=-1, keepdims=True)
        e = jnp.exp(z)
        p = e / jnp.sum(e, axis=-1, keepdims=True)
        o_ref[...] = _dot(p.astype(BF16), v_ref[...].astype(BF16)).astype(o_ref.dtype)

    h = n_heads
    return _pcall(
        body, grid=(h, s // tq),
        in_specs=[pl.BlockSpec((tq, HEAD_XA), lambda hh, i: (i, q_blk + hh)),
                  pl.BlockSpec((nm, HEAD_XA), lambda hh, i: (0, hh)), pl.BlockSpec((nm, HEAD_XA), lambda hh, i: (0, h + hh))],
        out_specs=pl.BlockSpec((tq, HEAD_XA), lambda hh, i: (i, hh)),
        out_shape=jax.ShapeDtypeStruct((s, h * HEAD_XA), BF16), name="xa_fwd",
        compiler_params=_params(("parallel", "parallel")))(proj, mem_kv, mem_kv)


def _xa_bwd(proj, mem_kv, do, q_blk, n_heads):
    s = proj.shape[0]
    nm = mem_kv.shape[0]
    tq = _tile(s, (512, 256, 128))
    scale = HEAD_XA ** -0.5
    h = n_heads

    def body(q_ref, k_ref, v_ref, do_ref, dq_ref, dk_ref, dv_ref):
        @pl.when(pl.program_id(1) == 0)
        def _():
            dk_ref[...] = jnp.zeros_like(dk_ref)
            dv_ref[...] = jnp.zeros_like(dv_ref)

        qb = q_ref[...].astype(BF16)
        kb = k_ref[...].astype(BF16)
        vb = v_ref[...].astype(BF16)
        dob = do_ref[...].astype(BF16)
        z = _dot_nt(qb, kb) * scale
        z = z - jnp.max(z, axis=-1, keepdims=True)
        e = jnp.exp(z)
        p = e / jnp.sum(e, axis=-1, keepdims=True)
        dp = _dot_nt(dob, vb)
        dz = (p * (dp - jnp.sum(dp * p, axis=-1, keepdims=True)) * scale).astype(BF16)
        dq_ref[...] = _dot(dz, kb)
        dk_ref[...] += _dot_tn(dz, qb)
        dv_ref[...] += _dot_tn(p.astype(BF16), dob)

    qspec = pl.BlockSpec((tq, HEAD_XA), lambda hh, i: (i, hh))
    dk, dv = None, None
    dq, dk, dv = _pcall(
        body, grid=(h, s // tq),
        in_specs=[pl.BlockSpec((tq, HEAD_XA), lambda hh, i: (i, q_blk + hh)),
                  pl.BlockSpec((nm, HEAD_XA), lambda hh, i: (0, hh)), pl.BlockSpec((nm, HEAD_XA), lambda hh, i: (0, h + hh)),
                  qspec],
        out_specs=[qspec, pl.BlockSpec((nm, HEAD_XA), lambda hh, i: (0, hh)), pl.BlockSpec((nm, HEAD_XA), lambda hh, i: (0, hh))],
        out_shape=[jax.ShapeDtypeStruct((s, h * HEAD_XA), F32), jax.ShapeDtypeStruct((nm, h * HEAD_XA), F32),
                   jax.ShapeDtypeStruct((nm, h * HEAD_XA), F32)],
        name="xa_bwd", compiler_params=_params(("parallel", "arbitrary")))(proj, mem_kv, mem_kv, do)
    return dq, dk, dv


def _merge_fwd(zg, b_gate, branches):
    s, d = branches[0].shape
    tr = _tile(s, (128,))

    def body(z0, z1, z2, g0, g1, g2, b0, b1, b2, o_ref):
        acc = None
        for z, g, b in ((z0, g0, b0), (z1, g1, b1), (z2, g2, b2)):
            term = jax.nn.sigmoid(z[...].astype(F32) + g[...]) * b[...]
            acc = term if acc is None else acc + term
        o_ref[...] = acc.astype(o_ref.dtype)

    zs = [pl.BlockSpec((tr, d), functools.partial(lambda i, k: (i, k), k=k)) for k in range(3)]
    gs = [pl.BlockSpec((1, d), functools.partial(lambda i, k: (0, k), k=k)) for k in range(3)]
    row = pl.BlockSpec((tr, d), lambda i: (i, 0))
    return _pcall(
        body, grid=(s // tr,), in_specs=zs + gs + [row] * 3, out_specs=row,
        out_shape=jax.ShapeDtypeStruct((s, d), BF16), name="merge_fwd",
        compiler_params=_params(("parallel",)))(zg, zg, zg, b_gate, b_gate, b_gate, *branches)


def _merge_bwd(zg, b_gate, branches, dmerged):
    s, d = branches[0].shape
    tr = _tile(s, (128,))

    def body(z0, z1, z2, g0, g1, g2, b0, b1, b2, dm_ref, dz_ref, d0, d1, d2, dbg_ref):
        @pl.when(pl.program_id(0) == 0)
        def _():
            dbg_ref[...] = jnp.zeros_like(dbg_ref)

        dm = dm_ref[...]
        for k, (z, g, b, dbr) in enumerate(((z0, g0, b0, d0), (z1, g1, b1, d1), (z2, g2, b2, d2))):
            sg = jax.nn.sigmoid(z[...].astype(F32) + g[...])
            dbr[...] = (dm * sg).astype(dbr.dtype)
            dz = dm * b[...] * sg * (1.0 - sg)
            dz_ref[:, k * d:(k + 1) * d] = dz.astype(dz_ref.dtype)
            dbg_ref[:, k * d:(k + 1) * d] += jnp.sum(dz, axis=0, keepdims=True)

    zs = [pl.BlockSpec((tr, d), functools.partial(lambda i, k: (i, k), k=k)) for k in range(3)]
    gs = [pl.BlockSpec((1, d), functools.partial(lambda i, k: (0, k), k=k)) for k in range(3)]
    row = pl.BlockSpec((tr, d), lambda i: (i, 0))
    outs = _pcall(
        body, grid=(s // tr,), in_specs=zs + gs + [row] * 4,
        out_specs=[pl.BlockSpec((tr, 3 * d), lambda i: (i, 0)), row, row, row, pl.BlockSpec((1, 3 * d), lambda i: (0, 0))],
        out_shape=[jax.ShapeDtypeStruct((s, 3 * d), BF16)] + [jax.ShapeDtypeStruct((s, d), BF16)] * 3
        + [jax.ShapeDtypeStruct((1, 3 * d), F32)],
        name="merge_bwd", compiler_params=_params(("arbitrary",)))(zg, zg, zg, b_gate, b_gate, b_gate, *branches, dmerged)
    return outs[0], list(outs[1:4]), outs[4]


def _shift_down(x, k, row):
    return jnp.where(row >= k, pltpu.roll(x, k, 0), 0.0)


def _shift_up(x, k, row, s):
    return jnp.where(row < s - k, pltpu.roll(x, s - k, 0), 0.0)


def _conv_pre(gate, cw_ref, cb_ref, row):
    conv = cb_ref[...] + cw_ref[CONV_TAPS - 1:CONV_TAPS, :] * gate
    for k in range(1, CONV_TAPS):
        conv = conv + cw_ref[CONV_TAPS - 1 - k:CONV_TAPS - k, :] * _shift_down(gate, k, row)
    return conv


def _cg_fwd(up, conv_w, conv_b):
    s = up.shape[0]
    f = conv_w.shape[1]
    tc = _tile(f, (256, 128))
    nb = f // tc

    def body(g_ref, v_ref, cw_ref, cb_ref, o_ref):
        row = lax.broadcasted_iota(jnp.int32, (s, tc), 0)
        conv = _conv_pre(g_ref[...].astype(F32), cw_ref, cb_ref, row)
        o_ref[...] = (_gelu(conv) * v_ref[...].astype(F32)).astype(o_ref.dtype)

    return _pcall(
        body, grid=(nb,),
        in_specs=[pl.BlockSpec((s, tc), lambda j: (0, j)), pl.BlockSpec((s, tc), lambda j: (0, nb + j)),
                  pl.BlockSpec((CONV_TAPS, tc), lambda j: (0, j)), pl.BlockSpec((1, tc), lambda j: (0, j))],
        out_specs=pl.BlockSpec((s, tc), lambda j: (0, j)),
        out_shape=jax.ShapeDtypeStruct((s, f), BF16), name="cg_fwd",
        compiler_params=_params(("parallel",)))(up, up, conv_w, conv_b)


def _cg_bwd(up, conv_w, conv_b, dact):
    s = up.shape[0]
    f = conv_w.shape[1]
    tc = _tile(f, (256, 128))
    nb = f // tc

    def body(g_ref, v_ref, cw_ref, cb_ref, da_ref, dg_ref, dv_ref, dcw_ref, dcb_ref):
        row = lax.broadcasted_iota(jnp.int32, (s, tc), 0)
        gate = g_ref[...].astype(F32)
        conv = _conv_pre(gate, cw_ref, cb_ref, row)
        gel, ggrad = _gelu_and_grad(conv)
        da = da_ref[...]
        dv_ref[...] = (da * gel).astype(dv_ref.dtype)
        dconv = da * v_ref[...].astype(F32) * ggrad
        dgate = cw_ref[CONV_TAPS - 1:CONV_TAPS, :] * dconv
        dcw_ref[CONV_TAPS - 1:CONV_TAPS, :] = jnp.sum(dconv * gate, axis=0, keepdims=True)
        for k in range(1, CONV_TAPS):
            dgate = dgate + cw_ref[CONV_TAPS - 1 - k:CONV_TAPS - k, :] * _shift_up(dconv, k, row, s)
            dcw_ref[CONV_TAPS - 1 - k:CONV_TAPS - k, :] = jnp.sum(dconv * _shift_down(gate, k, row), axis=0, keepdims=True)
        dg_ref[...] = dgate.astype(dg_ref.dtype)
        dcb_ref[...] = jnp.sum(dconv, axis=0, keepdims=True)

    colb = pl.BlockSpec((s, tc), lambda j: (0, j))
    return _pcall(
        body, grid=(nb,),
        in_specs=[colb, pl.BlockSpec((s, tc), lambda j: (0, nb + j)), pl.BlockSpec((CONV_TAPS, tc), lambda j: (0, j)),
                  pl.BlockSpec((1, tc), lambda j: (0, j)), colb],
        out_specs=[colb, colb, pl.BlockSpec((CONV_TAPS, tc), lambda j: (0, j)), pl.BlockSpec((1, tc), lambda j: (0, j))],
        out_shape=[jax.ShapeDtypeStruct((s, f), BF16), jax.ShapeDtypeStruct((s, f), BF16),
                   jax.ShapeDtypeStruct((CONV_TAPS, f), F32), jax.ShapeDtypeStruct((1, f), F32)],
        name="cg_bwd", compiler_params=_params(("parallel",)))(up, up, conv_w, conv_b, dact)


def _row_tile(rows, cols, elems=256 * 1024):
    want = max(16, elems // cols)
    for c in (512, 256, 128, 64, 32, 16):
        if c <= want and rows % c == 0:
            return c
    return rows


def _sum_halves(dwv, recv, core, name):
    nj, _, a, c = dwv.shape
    tr = _row_tile(a, c, 1024 * 1024)

    def body(core_ref, d_ref, r_ref, o_ref):
        o_ref[0] = (d_ref[0, 0].astype(F32) + r_ref[0].astype(F32)).astype(o_ref.dtype)

    grid_spec = pltpu.PrefetchScalarGridSpec(
        num_scalar_prefetch=1, grid=(nj, a // tr),
        in_specs=[pl.BlockSpec((1, 1, tr, c), lambda j, i, cr: (j, cr[0], i, 0)),
                  pl.BlockSpec((1, tr, c), lambda j, i, cr: (j, i, 0))],
        out_specs=pl.BlockSpec((1, tr, c), lambda j, i, cr: (j, i, 0)))
    return _pcall(body, grid_spec=grid_spec, out_shape=jax.ShapeDtypeStruct((nj, a, c), BF16), name=name,
                  compiler_params=_params(("parallel", "parallel")))(core, dwv, recv)


def _sum_chips(recv, own, chip, core, ax):
    _, a, b = recv.shape
    tr = _row_tile(a, b, 512 * 1024)

    def body(chip_ref, core_ref, r_ref, own_ref, o_ref):
        me = chip_ref[0]
        mine = own_ref[0].astype(F32)
        acc = None
        for k in range(N_CHIPS):
            term = jnp.where(me == k, mine, r_ref[k].astype(F32))
            acc = term if acc is None else acc + term
        o_ref[0] = acc

    own_spec = (pl.BlockSpec((1, tr, b), lambda i, ch, co: (0, i, ch[0])) if ax == 2
                else pl.BlockSpec((1, tr, b), lambda i, ch, co: (ch[0], i, 0)))
    grid_spec = pltpu.PrefetchScalarGridSpec(
        num_scalar_prefetch=2, grid=(a // tr,),
        in_specs=[pl.BlockSpec((N_CHIPS, tr, b), lambda i, ch, co: (0, i, 0)), own_spec],
        out_specs=pl.BlockSpec((1, tr, b), lambda i, ch, co: (co[0], i, 0)))
    return _pcall(body, grid_spec=grid_spec, out_shape=jax.ShapeDtypeStruct((2, a, b), F32),
                  name="sum_chips", compiler_params=_params(("parallel",)))(chip, core, recv, own)


def _place_own(wt, layer, chip, ax, after):
    nl, r, c = wt.shape
    half = r // 2
    tr = _row_tile(half, c, 512 * 1024)
    nb = half // tr

    def body(chip_ref, w_ref, after_ref, o_ref):
        o_ref[...] = w_ref[...].astype(BF16).reshape(o_ref.shape)

    if ax == 2:
        out_spec = pl.BlockSpec((1, tr, c), lambda h, i, ch: (h, i, ch[0]))
    else:
        out_spec = pl.BlockSpec((1, 1, tr, c), lambda h, i, ch: (ch[0], h, i, 0))
    grid_spec = pltpu.PrefetchScalarGridSpec(
        num_scalar_prefetch=1, grid=(2, nb),
        in_specs=[pl.BlockSpec((1, tr, c), lambda h, i, ch: (layer, h * nb + i, 0)), pl.BlockSpec(memory_space=pl.ANY)],
        out_specs=out_spec)
    return _pcall(body, grid_spec=grid_spec, out_shape=jax.ShapeDtypeStruct(_full_view_shape(wt.shape, ax), BF16),
                  name="place_own", compiler_params=_params(("parallel", "parallel")))(chip, wt, after)


def _adamw(w, m, v, g, layer, prev, name, after=None):
    nl, r, c = w.shape
    tr = _row_tile(r, c, 512 * 1024)
    c1 = 1.0 - ADAM_B1 ** ADAM_STEP
    c2 = 1.0 - ADAM_B2 ** ADAM_STEP

    follow = [] if after is None else [after]

    def body(w_ref, m_ref, v_ref, gin_ref, *rest):
        g_ref, d_ref, nm_ref, nv_ref = rest[-4:]
        g = gin_ref[...]
        mm = ADAM_B1 * m_ref[0] + (1.0 - ADAM_B1) * g
        vv = ADAM_B2 * v_ref[0] + (1.0 - ADAM_B2) * (g * g)
        g_ref[0] = g
        nm_ref[0] = mm
        nv_ref[0] = vv
        d_ref[0] = -ADAM_LR * ((mm / c1) / (jnp.sqrt(vv / c2) + ADAM_EPS) + ADAM_WD * w_ref[0])

    blk = pl.BlockSpec((1, tr, c), lambda i: (layer, i, 0))
    shape = jax.ShapeDtypeStruct((nl, r, c), F32)
    extra = [] if prev is None else list(prev)
    return _pcall(
        body, grid=(r // tr,),
        in_specs=[blk] * 3 + [pl.BlockSpec((tr, c), lambda i: (i, 0))] + [pl.BlockSpec(memory_space=pl.ANY)] * (len(extra) + len(follow)),
        out_specs=[blk] * 4, out_shape=[shape] * 4, input_output_aliases={4 + k: k for k in range(len(extra))}, name=name,
        compiler_params=_params(("parallel",)))(w, m, v, g, *extra, *follow)


HBM_SPEC = pl.BlockSpec(memory_space=pltpu.HBM)
COMM = pltpu.CompilerParams(has_side_effects=True)


def _position():
    x, y, c = lax.axis_index("x"), lax.axis_index("y"), lax.axis_index("c")
    chips = [(1 - x, y), (x, 1 - y), (1 - x, 1 - y)]
    return x, y, c, chips


def _remote(src, dst, send_sem, recv_sem, dev):
    return pltpu.make_async_remote_copy(src_ref=src, dst_ref=dst, send_sem=send_sem, recv_sem=recv_sem,
                                        device_id=dev, device_id_type=MESH)


def _full_view_shape(shard_shape, ax):
    _, r, c = shard_shape
    return (2, r // 2, c * N_CHIPS) if ax == 2 else (N_CHIPS, 2, r // 2, c)


def _piece(ref, ax, j, h, cs):
    if ax == 2:
        return ref.at[h, :, pl.ds(pl.multiple_of(j * cs, cs), cs)]
    return ref.at[j, h]


def _chip_block(ref, ax, j, cs):
    if ax == 2:
        return ref.at[:, :, pl.ds(pl.multiple_of(j * cs, cs), cs)]
    return ref.at[j]


SEM_SPEC = pl.BlockSpec(memory_space=pltpu.SEMAPHORE)
ANY_SPEC = pl.BlockSpec(memory_space=pl.ANY)
SPLIT = pltpu.CompilerParams(has_side_effects=pltpu.SideEffectType.DATAFLOW_SIDE_EFFECTING)


def _exchange(kind, name, bufs, build, n_sems, after=None, sems=None):
    n = len(bufs)
    if kind == 'sync':
        def body(*refs):
            mine, theirs = build(refs[n:2 * n], refs[2 * n], refs[2 * n + 1])
            for cp in mine:
                cp.start()
            for cp in theirs:
                cp.wait_recv()
            for cp in mine:
                cp.wait_send()

        return list(_pcall(
            body, in_specs=[HBM_SPEC] * n, out_specs=[HBM_SPEC] * n,
            out_shape=[jax.ShapeDtypeStruct(v.shape, v.dtype) for v in bufs], input_output_aliases={t: t for t in range(n)},
            scratch_shapes=[pltpu.SemaphoreType.DMA((n_sems,)), pltpu.SemaphoreType.DMA((n_sems,))],
            name=name, compiler_params=COMM)(*bufs))
    if kind == 'start':
        def body(*refs):
            mine, _ = build(refs[n + 3:2 * n + 3], refs[n + 1], refs[n + 2])
            for cp in mine:
                cp.start()
            refs[2 * n + 3][...] = jnp.zeros_like(refs[2 * n + 3])

        outs = _pcall(
            body, in_specs=[HBM_SPEC] * n + [ANY_SPEC],
            out_specs=[SEM_SPEC, SEM_SPEC] + [HBM_SPEC] * n + [pl.BlockSpec(memory_space=pltpu.VMEM)],
            out_shape=[pltpu.SemaphoreType.DMA((n_sems,)), pltpu.SemaphoreType.DMA((n_sems,))]
            + [pltpu.HBM(v.shape, v.dtype) for v in bufs] + [jax.ShapeDtypeStruct((8, LANES), F32)],
            input_output_aliases={t: 2 + t for t in range(n)}, name=name,
            compiler_params=SPLIT)(*[pltpu.with_memory_space_constraint(v, pltpu.HBM) for v in bufs], after)
        return (outs[0], outs[1]), list(outs[2:2 + n]), outs[2 + n]

    def body(*refs):
        mine, theirs = build(refs[:n], refs[n], refs[n + 1])
        for cp in mine:
            cp.wait_send()
        for cp in theirs:
            cp.wait_recv()

    return list(_pcall(
        body, in_specs=[HBM_SPEC] * n + [SEM_SPEC, SEM_SPEC, ANY_SPEC], out_specs=[HBM_SPEC] * n,
        out_shape=[pltpu.HBM(v.shape, v.dtype) for v in bufs], input_output_aliases={t: t for t in range(n)},
        name=name, compiler_params=SPLIT)(*bufs, sems[0], sems[1], after))


def _gather_ici_copies(axes, shard_cols):
    def build(bufs, send_sems, recv_sems):
        x, y, c, chips = _position()
        me = 2 * x + y
        mine, theirs = [], []
        for t, ax in enumerate(axes):
            own = _piece(bufs[t], ax, me, c, shard_cols[t])
            for p, (px, py) in enumerate(chips):
                k = t * 3 + p
                got = _piece(bufs[t], ax, 2 * px + py, c, shard_cols[t])
                mine.append(_remote(own, own, send_sems.at[k], recv_sems.at[k], (px, py, c)))
                theirs.append(_remote(got, got, send_sems.at[k], recv_sems.at[k], (px, py, c)))
        return mine, theirs
    return build


def _gather_d2d_copies(axes, shard_cols):
    def build(bufs, send_sems, recv_sems):
        x, y, c, chips = _position()
        mine, theirs = [], []
        for t, ax in enumerate(axes):
            for p, (px, py) in enumerate(chips):
                k = t * 3 + p
                had = _piece(bufs[t], ax, 2 * px + py, c, shard_cols[t])
                got = _piece(bufs[t], ax, 2 * px + py, 1 - c, shard_cols[t])
                mine.append(_remote(had, had, send_sems.at[k], recv_sems.at[k], (x, y, 1 - c)))
                theirs.append(_remote(got, got, send_sems.at[k], recv_sems.at[k], (x, y, 1 - c)))
        return mine, theirs
    return build


def _grads_d2d_copies(n):
    def build(bufs, send_sems, recv_sems):
        x, y, c, _ = _position()
        mine = [_remote(bufs[t].at[:, 1 - c], bufs[n + t], send_sems.at[t], recv_sems.at[t], (x, y, 1 - c)) for t in range(n)]
        return mine, mine
    return build


def _grads_ici_copies(axes):
    n = len(axes)

    def build(bufs, send_sems, recv_sems):
        x, y, c, chips = _position()
        me = 2 * x + y

        def block(t, j):
            if axes[t] == 2:
                cs = bufs[n + t].shape[2]
                return bufs[t].at[0, :, pl.ds(pl.multiple_of(j * cs, cs), cs)]
            return bufs[t].at[j]

        mine, theirs = [], []
        for t in range(n):
            for p, (px, py) in enumerate(chips):
                k = t * 3 + p
                peer = 2 * px + py
                mine.append(_remote(block(t, peer), bufs[n + t].at[me], send_sems.at[k], recv_sems.at[k], (px, py, c)))
                theirs.append(_remote(block(t, peer), bufs[n + t].at[peer], send_sems.at[k], recv_sems.at[k], (px, py, c)))
        return mine, theirs
    return build


def _join_copies(bufs, send_sems, recv_sems):
    x, y, c, _ = _position()
    mine = [_remote(b.at[c], b.at[c], send_sems.at[t], recv_sems.at[t], (x, y, 1 - c)) for t, b in enumerate(bufs)]
    theirs = [_remote(b.at[1 - c], b.at[1 - c], send_sems.at[t], recv_sems.at[t], (x, y, 1 - c)) for t, b in enumerate(bufs)]
    return mine, theirs


def _grads_recv_shape(sm, ax):
    _, a, c = sm.shape
    return (N_CHIPS, a, c // N_CHIPS if ax == 2 else c)


def _gather_small(shard):
    nl, r, cs = shard.shape

    def body(in_ref, out_ref, send_sems, recv_sems, local_sem):
        x, y, c, chips = _position()

        def cols(j):
            return out_ref.at[:, :, pl.ds(pl.multiple_of(j * cs, cs), cs)]

        me = 2 * x + y
        loc = pltpu.make_async_copy(in_ref, cols(me), local_sem)
        loc.start()
        remote = [_remote(in_ref, cols(me), send_sems.at[p], recv_sems.at[p], (px, py, c)) for p, (px, py) in enumerate(chips)]
        for cp in remote:
            cp.start()
        for p, (px, py) in enumerate(chips):
            _remote(in_ref, cols(2 * px + py), send_sems.at[p], recv_sems.at[p], (px, py, c)).wait_recv()
        for cp in remote:
            cp.wait_send()
        loc.wait()

    return _pcall(
        body, in_specs=[HBM_SPEC], out_specs=HBM_SPEC, out_shape=jax.ShapeDtypeStruct((nl, r, cs * N_CHIPS), shard.dtype),
        scratch_shapes=[pltpu.SemaphoreType.DMA((3,)), pltpu.SemaphoreType.DMA((3,)), pltpu.SemaphoreType.DMA(())],
        name="gather_small", compiler_params=COMM)(shard)


def _small_copies(bufs, send_sems, recv_sems):
    pack, land = bufs
    x, y, cc, _ = _position()
    me = 4 * x + 2 * y + cc
    mine, theirs = [], []
    for k in range(1, N_DEV):
        px, py, pc = x ^ ((k >> 2) & 1), y ^ ((k >> 1) & 1), cc ^ (k & 1)
        mine.append(_remote(pack, land.at[me], send_sems.at[k - 1], recv_sems.at[k - 1], (px, py, pc)))
        theirs.append(_remote(pack, land.at[4 * px + 2 * py + pc], send_sems.at[k - 1], recv_sems.at[k - 1], (px, py, pc)))
    return mine, theirs


def _sum_devices(land, pack, dev):
    _, r, c = land.shape
    tr = _tile(r, (672, 512, 256, 128, 64, 8))

    def body(dev_ref, l_ref, p_ref, o_ref):
        me = dev_ref[0]
        acc = None
        for k in range(N_DEV):
            term = jnp.where(me == k, p_ref[...], l_ref[k])
            acc = term if acc is None else acc + term
        o_ref[...] = acc

    grid_spec = pltpu.PrefetchScalarGridSpec(
        num_scalar_prefetch=1, grid=(r // tr,),
        in_specs=[pl.BlockSpec((N_DEV, tr, c), lambda i, dv: (0, i, 0)), pl.BlockSpec((tr, c), lambda i, dv: (i, 0))],
        out_specs=pl.BlockSpec((tr, c), lambda i, dv: (i, 0)))
    return _pcall(body, grid_spec=grid_spec, out_shape=jax.ShapeDtypeStruct((r, c), F32), name="sum_devices",
                  compiler_params=_params(("parallel",)))(dev, land, pack)


def _dims(d):
    half = d // 2
    return half // HEAD_SB, half // HEAD_XA, 3, (5 * half) // HEAD_XA


def _layer_fwd(x, mem, weight, small, l, after=None):
    h_sb, h_xa, u_blk, q_blk = _dims(x.shape[1])

    def vec(name):
        return small[name][l].reshape(1, -1)

    def use(a, name, follows, mm_name, dtype=F32):
        wt = weight(name, follows)
        return _mm(a, wt, 'nn', dtype, mm_name, after=weight.token())

    h1 = _norm_fwd(x, vec('g_mix_pre'), None, BF16, "norm_mix_pre", after)
    proj = use(h1, 'w_in', h1, "mm_proj")
    o_sb, a_sb = _sb_fwd(proj, h_sb)
    b_st = small['b_s'][l].T
    o_gm = _gm_fwd(proj, vec('g_vnorm'), small['w_s'][l], b_st, u_blk)
    memn = _norm_fwd(mem, vec('g_mem'), None, BF16, "norm_mem")
    mem_kv = use(memn, 'w_mem_kv', o_sb, "mm_mem_kv")
    o_xa = _xa_fwd(proj, mem_kv, q_blk, h_xa)
    zg = use(h1, 'w_gate', o_sb, "mm_gate", BF16)
    branches = [use(o, wn, zg, "mm_branch") for o, wn in ((o_sb, 'w_br_sb'), (o_gm, 'w_br_gm'), (o_xa, 'w_br_xa'))]
    merged = _merge_fwd(zg, vec('b_gate'), branches)
    y1 = use(merged, 'w_out', zg, "mm_out")
    x1 = _norm_fwd(y1, vec('g_mix_post'), x, F32, "norm_mix_post")
    h2 = _norm_fwd(x1, vec('g_ffn_pre'), None, BF16, "norm_ffn_pre")
    up = use(h2, 'w_up', h2, "mm_up", BF16)
    act = _cg_fwd(up, weight('conv_w', up), vec('conv_b'))
    y2 = use(act, 'w_down', act, "mm_down")
    x2 = _norm_fwd(y2, vec('g_ffn_post'), x1, F32, "norm_ffn_post")
    saved = dict(x0=x, h1=h1, proj=proj, o_sb=o_sb, a_sb=a_sb, o_gm=o_gm, o_xa=o_xa, memn=memn, mem_kv=mem_kv, zg=zg,
                 branches=branches, merged=merged, y1=y1, x1=x1, h2=h2, up=up, act=act, y2=y2, b_st=b_st)
    return x2, saved


def _layer_bwd(dx, mem, sv, full, small, l, after, emit):
    h_sb, h_xa, u_blk, q_blk = _dims(dx.shape[1])

    def vec(name):
        return small[name][l].reshape(1, -1)

    gb, gs = {}, {}
    dy2, gs['g_ffn_post'] = _norm_bwd(sv['y2'], vec('g_ffn_post'), [dx], None, BF16, "norm_ffn_post_bwd", after)
    gb['w_down'] = _mm(sv['act'], dy2, 'tn', BF16, "mm_down_dw")
    dact = _mm(dy2, full['w_down'], 'nt', F32, "mm_down_dx")
    dgate, dval, gs['conv_w'], gs['conv_b'] = _cg_bwd(sv['up'], full['conv_w'], vec('conv_b'), dact)
    gb['w_up'] = _mm(sv['h2'], [dgate, dval], 'tn', BF16, "mm_up_dw")
    token = emit(0, gb)
    dh2 = _mm([dgate, dval], full['w_up'], 'nt', F32, "mm_up_dx", after=token)
    token = emit.flush(dh2)
    dx1, gs['g_ffn_pre'] = _norm_bwd(sv['x1'], vec('g_ffn_pre'), [dh2], dx, F32, "norm_ffn_pre_bwd", token)
    dy1, gs['g_mix_post'] = _norm_bwd(sv['y1'], vec('g_mix_post'), [dx1], None, BF16, "norm_mix_post_bwd")
    gb['w_out'] = _mm(sv['merged'], dy1, 'tn', BF16, "mm_out_dw")
    dmerged = _mm(dy1, full['w_out'], 'nt', F32, "mm_out_dx")
    dzg, dbr, gs['b_gate'] = _merge_bwd(sv['zg'], vec('b_gate'), sv['branches'], dmerged)
    douts = []
    for o, db, wn in ((sv['o_sb'], dbr[0], 'w_br_sb'), (sv['o_gm'], dbr[1], 'w_br_gm'), (sv['o_xa'], dbr[2], 'w_br_xa')):
        gb[wn] = _mm(o, db, 'tn', BF16, "mm_branch_dw")
        douts.append(_mm(db, full[wn], 'nt', F32, "mm_branch_dx"))
    gb['w_gate'] = _mm(sv['h1'], dzg, 'tn', BF16, "mm_gate_dw")
    dq_xa, dk_xa, dv_xa = _xa_bwd(sv['proj'], sv['mem_kv'], douts[2], q_blk, h_xa)
    dmem_kv = jnp.concatenate([dk_xa, dv_xa], axis=1).astype(BF16)
    gb['w_mem_kv'] = _mm(sv['memn'], dmem_kv, 'tn', BF16, "mm_mem_kv_dw")
    token = emit(1, gb)
    dh1_gate = _mm(dzg, full['w_gate'], 'nt', F32, "mm_gate_dx", after=token)
    token = emit.flush(dh1_gate)
    dmemn = _mm(dmem_kv, full['w_mem_kv'], 'nt', F32, "mm_mem_kv_dx")
    _, gs['g_mem'] = _norm_bwd(mem, vec('g_mem'), [dmemn], None, BF16, "norm_mem_bwd")
    du, dv, gs['g_vnorm'], gs['w_s'], db_st = _gm_bwd(sv['proj'], vec('g_vnorm'), small['w_s'][l], sv['b_st'], douts[1], u_blk)
    gs['b_s'] = db_st.T
    dq, dk, dvv = _sb_bwd(sv['proj'], sv['a_sb'], douts[0], h_sb, token)
    dproj = jnp.concatenate([dq, dk, dvv, du, dv, dq_xa], axis=1).astype(BF16)
    gb['w_in'] = _mm(sv['h1'], dproj, 'tn', BF16, "mm_proj_dw")
    token = emit(2, gb)
    dh1_proj = _mm(dproj, full['w_in'], 'nt', F32, "mm_proj_dx")
    dx0, gs['g_mix_pre'] = _norm_bwd(sv['x0'], vec('g_mix_pre'), [dh1_gate, dh1_proj], dx1, F32, "norm_mix_pre_bwd", token)
    return dx0, gs


class _Given:
    def __init__(self, full):
        self.full = full

    def __call__(self, name, follows):
        return self.full[name]

    def token(self):
        return None


def _local_step(x, mem, target, full, small):
    n_layers = len(full['w_in'])
    saved = []
    for l in range(n_layers):
        x, sv = _layer_fwd(x, mem, _Given({n: full[n][l] for n in full}), small, l)
        saved.append(sv)
    sq, dx = _loss_head(x, target)
    gbig = {n: [None] * n_layers for n in BIG}
    gsmall = {n: [None] * n_layers for n in SMALL + ['conv_w']}
    class Collect:
        def __init__(self, l):
            self.l = l

        def __call__(self, g, gb):
            for n in BWD_GROUPS[g]:
                gbig[n][self.l] = gb[n]

        def flush(self, follows):
            return None

    for l in reversed(range(n_layers)):
        dx, gs = _layer_bwd(dx, mem, saved[l], {n: full[n][l] for n in full}, small, l, None, Collect(l))
        for n in gs:
            gsmall[n][l] = gs[n]
    return sq, dx, gbig, gsmall


def _pack(arrays, rows_multiple):
    flat = jnp.concatenate([a.reshape(-1).astype(F32) for a in arrays])
    rows = -(-flat.shape[0] // LANES)
    rows = -(-rows // rows_multiple) * rows_multiple
    return jnp.pad(flat, (0, rows * LANES - flat.shape[0])).reshape(rows, LANES)


def _unpack(pack, like):
    flat = pack.reshape(-1)
    out, off = [], 0
    for a in like:
        out.append(flat[off:off + a.size].reshape(a.shape))
        off += a.size
    return out


def _grad_view(g, ax):
    r, c = g.shape
    return g.reshape(1, 2, r // 2, c) if ax == 2 else g.reshape(N_CHIPS, 2, r // (2 * N_CHIPS), c)


def kernel(x, mem, g_mix_pre, w_in, g_vnorm, w_s, b_s, g_mem, w_mem_kv, w_gate, b_gate, w_br_sb, w_br_gm, w_br_xa, w_out, g_mix_post, g_ffn_pre, w_up, conv_w, conv_b, w_down, g_ffn_post, loss_target, m_g_mix_pre, m_w_in, m_g_vnorm, m_w_s, m_b_s, m_g_mem, m_w_mem_kv, m_w_gate, m_b_gate, m_w_br_sb, m_w_br_gm, m_w_br_xa, m_w_out, m_g_mix_post, m_g_ffn_pre, m_w_up, m_conv_w, m_conv_b, m_w_down, m_g_ffn_post, v_g_mix_pre, v_w_in, v_g_vnorm, v_w_s, v_b_s, v_g_mem, v_w_mem_kv, v_w_gate, v_b_gate, v_w_br_sb, v_w_br_gm, v_w_br_xa, v_w_out, v_g_mix_post, v_g_ffn_pre, v_w_up, v_conv_w, v_conv_b, v_w_down, v_g_ffn_post):
    w = dict(g_mix_pre=g_mix_pre, w_in=w_in, g_vnorm=g_vnorm, w_s=w_s, b_s=b_s, g_mem=g_mem, w_mem_kv=w_mem_kv,
             w_gate=w_gate, b_gate=b_gate, w_br_sb=w_br_sb, w_br_gm=w_br_gm, w_br_xa=w_br_xa, w_out=w_out,
             g_mix_post=g_mix_post, g_ffn_pre=g_ffn_pre, w_up=w_up, conv_w=conv_w, conv_b=conv_b, w_down=w_down,
             g_ffn_post=g_ffn_post)
    m = dict(g_mix_pre=m_g_mix_pre, w_in=m_w_in, g_vnorm=m_g_vnorm, w_s=m_w_s, b_s=m_b_s, g_mem=m_g_mem,
             w_mem_kv=m_w_mem_kv, w_gate=m_w_gate, b_gate=m_b_gate, w_br_sb=m_w_br_sb, w_br_gm=m_w_br_gm,
             w_br_xa=m_w_br_xa, w_out=m_w_out, g_mix_post=m_g_mix_post, g_ffn_pre=m_g_ffn_pre, w_up=m_w_up,
             conv_w=m_conv_w, conv_b=m_conv_b, w_down=m_w_down, g_ffn_post=m_g_ffn_post)
    v = dict(g_mix_pre=v_g_mix_pre, w_in=v_w_in, g_vnorm=v_g_vnorm, w_s=v_w_s, b_s=v_b_s, g_mem=v_g_mem,
             w_mem_kv=v_w_mem_kv, w_gate=v_w_gate, b_gate=v_b_gate, w_br_sb=v_w_br_sb, w_br_gm=v_w_br_gm,
             w_br_xa=v_w_br_xa, w_out=v_w_out, g_mix_post=v_g_mix_post, g_ffn_pre=v_g_ffn_pre, w_up=v_w_up,
             conv_w=v_conv_w, conv_b=v_conv_b, w_down=v_w_down, g_ffn_post=v_g_ffn_post)
    n_layers = w_in.shape[0]
    d = x.shape[-1]
    core = lax.axis_index("c").astype(jnp.int32).reshape(1)
    chip = (2 * lax.axis_index("x") + lax.axis_index("y")).astype(jnp.int32).reshape(1)
    small = {n: w[n] for n in SMALL}
    xs, mems, target = x[0], mem[0], loss_target[0]

    conv_w_full = _gather_small(conv_w)

    def as_full(vw, ax):
        return vw.reshape(-1, vw.shape[-1]) if ax == 1 else vw.reshape(vw.shape[0] * vw.shape[1], vw.shape[2])

    stages = {}
    keys = [(l, g) for l in range(n_layers) for g in range(len(FWD_GROUPS))]

    def start_gathers(l, token):
        for g, names in enumerate(FWD_GROUPS):
            ax_g = [BIG_AXIS[n] for n in names]
            cols_g = [w[n].shape[2] for n in names]
            views = [_place_own(w[n], l, chip, ax, token) for n, ax in zip(names, ax_g)]
            sems, views, token = _exchange('start', f"gather_ici_start_{l}_{g}", views, _gather_ici_copies(ax_g, cols_g),
                                           3 * len(names), after=token)
            stages[l, g] = dict(names=names, ax=ax_g, cols=cols_g, views=views, ici=sems, d2d=None, full=None)
        return token

    token = start_gathers(0, conv_w_full)

    def cross_cores(key, follows):
        st, (l, g) = stages[key], key
        n3 = 3 * len(st['names'])
        views = _exchange('wait', f"gather_ici_wait_{l}_{g}", st['views'], _gather_ici_copies(st['ax'], st['cols']), n3,
                          after=follows, sems=st['ici'])
        st['d2d'], st['views'], tok = _exchange('start', f"gather_d2d_start_{l}_{g}", views,
                                                _gather_d2d_copies(st['ax'], st['cols']), n3, after=core)
        return tok

    class Weights:
        def __init__(self, l):
            self.l, self.tok = l, None

        def __call__(self, name, follows):
            if name == 'conv_w':
                return conv_w_full[self.l]
            key = (self.l, [g for g, names in enumerate(FWD_GROUPS) if name in names][0])
            if key[1] == 1 and (self.l + 1, 0) not in stages and self.l + 1 < n_layers:
                self.tok = start_gathers(self.l + 1, follows)
            st = stages[key]
            if st['full'] is None:
                if st['d2d'] is None:
                    cross_cores(key, follows)
                views = _exchange('wait', f"gather_d2d_wait_{key[0]}_{key[1]}", st['views'],
                                  _gather_d2d_copies(st['ax'], st['cols']), 3 * len(st['names']), after=follows, sems=st['d2d'])
                st['full'] = {n: as_full(vw, ax) for n, vw, ax in zip(st['names'], views, st['ax'])}
            nxt = keys.index(key) + 1
            if name == st['names'][-1] and nxt < len(keys) and key != keys[0] and stages[keys[nxt]]['d2d'] is None:
                self.tok = cross_cores(keys[nxt], follows)
            return st['full'][name]

        def token(self):
            return self.tok

    fulls, saved = [], []
    for l in range(n_layers):
        xs, sv = _layer_fwd(xs, mems, Weights(l), small, l, token if l == 0 else None)
        fulls.append({n: stages[l, g]['full'][n] for g, names in enumerate(FWD_GROUPS) for n in names} | {'conv_w': conv_w_full[l]})
        saved.append(sv)
    sq, dx = _loss_head(xs, target)
    loss = lax.psum(0.5 * jnp.sum(sq) / d, ("x", "y", "c"))

    sent = []

    def to_chips(l, g, names, ax_g, bufs, after):
        n = len(names)
        sums = [_sum_halves(dv, th, core, "sum_halves") for dv, th in zip(bufs[:n], bufs[n:])]
        lands = [lax.empty(_grads_recv_shape(sm, ax), sm.dtype) for sm, ax in zip(sums, ax_g)]
        sems, bufs, tok = _exchange('start', f"grads_ici_start_{l}_{g}", sums + lands, _grads_ici_copies(ax_g), 3 * n,
                                    after=core if after is None else after)
        sent.append((l, g, names, ax_g, bufs, sems))
        return tok

    class Grads:
        def __init__(self, l):
            self.l, self.crossing = l, None

        def __call__(self, g, gb):
            names = BWD_GROUPS[g]
            ax_g = [BIG_AXIS[n] for n in names]
            n = len(names)
            dwvs = [_grad_view(gb[nm], ax) for nm, ax in zip(names, ax_g)]
            lands = [lax.empty((dv.shape[0],) + dv.shape[2:], dv.dtype) for dv in dwvs]
            if g + 1 < len(BWD_GROUPS):
                sems, bufs, tok = _exchange('start', f"grads_d2d_start_{self.l}_{g}", dwvs + lands, _grads_d2d_copies(n), n,
                                            after=core)
                self.crossing = (g, names, ax_g, bufs, sems)
                return tok
            return to_chips(self.l, g, names, ax_g, _exchange('sync', "grads_d2d", dwvs + lands, _grads_d2d_copies(n), n), None)

        def flush(self, follows):
            if self.crossing is None:
                return None
            (g, names, ax_g, bufs, sems), self.crossing = self.crossing, None
            bufs = _exchange('wait', f"grads_d2d_wait_{self.l}_{g}", bufs, _grads_d2d_copies(len(names)), len(names),
                             after=follows, sems=sems)
            return to_chips(self.l, g, names, ax_g, bufs, None)

    gsmall = {n: [None] * n_layers for n in SMALL + ['conv_w']}
    for l in reversed(range(n_layers)):
        dx, gs = _layer_bwd(dx, mems, saved[l], fulls[l], small, l, None, Grads(l))
        for n in gs:
            gsmall[n][l] = gs[n]

    names_small = SMALL + ['conv_w']
    small_full = [jnp.stack(gsmall[n]).reshape(w[n].shape) for n in SMALL]
    conv_w_grad = jnp.stack(gsmall['conv_w'])
    pack = _pack(small_full + [conv_w_grad], 8)
    small_sems, small_bufs, _ = _exchange('start', "small_start", [pack, lax.empty((N_DEV,) + pack.shape, F32)], _small_copies,
                                          N_DEV - 1, after=dx)

    def small_update(follows):
        pk, land = _exchange('wait', "small_wait", small_bufs, _small_copies, N_DEV - 1, after=follows, sems=small_sems)
        device = (4 * lax.axis_index("x") + 2 * lax.axis_index("y") + lax.axis_index("c")).astype(jnp.int32).reshape(1)
        summed = _sum_devices(land, pk, device)
        *small_g, conv_w_g = _unpack(summed, small_full + [conv_w_grad])
        shard = conv_w.shape[-1]
        conv_w_g = lax.dynamic_slice_in_dim(conv_w_g, chip[0] * shard, shard, axis=2)
        packed = [_pack([p[n] for n in names_small], 256) for p in (w, m, v)]
        gpack = _pack(small_g + [conv_w_g], 256)
        res = _adamw(packed[0][None], packed[1][None], packed[2][None], gpack, 0, None, "adamw_small")
        like = [w[n] for n in names_small]
        unpacked = [_unpack(r[0], like) for r in res]
        return {n: tuple(u[i] for u in unpacked) for i, n in enumerate(names_small)}

    out = {}

    def update(joining, follows):
        l, g, names, halves, sems = joining
        halves = _exchange('wait', f"join_wait_{l}_{g}", halves, _join_copies, len(names), after=follows, sems=sems)
        last = None
        for n, hv in zip(names, halves):
            out[n] = _adamw(w[n], m[n], v[n], hv.reshape(w[n].shape[1:]), l, out.get(n), "adamw_big", last)
            last = out[n][0]
        return last

    joining, follows = None, dx
    for k, (l, g, names, ax_g, bufs, sems) in enumerate(sent):
        n = len(names)
        bufs = _exchange('wait', f"grads_ici_wait_{l}_{g}", bufs, _grads_ici_copies(ax_g), 3 * n, after=follows, sems=sems)
        halves = [_sum_chips(r, sm, chip, core, ax) for sm, r, ax in zip(bufs[:n], bufs[n:], ax_g)]
        jsems, halves, tok = _exchange('start', f"join_start_{l}_{g}", halves, _join_copies, n, after=core)
        if joining is not None:
            follows = update(joining, tok)
        joining = (l, g, names, halves, jsems)
    out.update(small_update(update(joining, follows)))

    return (loss, dx[None], *[out[n][0] for n in WEIGHTS], *[out[n][1] for n in WEIGHTS],
            *[out[n][2] for n in WEIGHTS], *[out[n][3] for n in WEIGHTS])
```

```python
import functools
import math

import jax
import jax.numpy as jnp
from jax import lax
from jax.experimental import pallas as pl
from jax.experimental.pallas import tpu as pltpu

F32 = jnp.float32
BF16 = jnp.bfloat16
EPS = 1e-6
HEAD_SB = 128
GROUP_GM = 128
CHUNK = 64
HEAD_XA = 256
CONV_TAPS = 3
N_CHIPS = 4
N_DEV = 8
LANES = 128
MIB = 1024 * 1024
VMEM_LIMIT = 48 * MIB
SPLITS = 1

ADAM_LR = 0.001
ADAM_B1 = 0.9
ADAM_B2 = 0.999
ADAM_EPS = 1e-08
ADAM_WD = 0.01
ADAM_STEP = 10

WEIGHTS = ['g_mix_pre', 'w_in', 'g_vnorm', 'w_s', 'b_s', 'g_mem', 'w_mem_kv', 'w_gate', 'b_gate', 'w_br_sb',
           'w_br_gm', 'w_br_xa', 'w_out', 'g_mix_post', 'g_ffn_pre', 'w_up', 'conv_w', 'conv_b', 'w_down',
           'g_ffn_post']
BIG_AXIS = {'w_in': 2, 'w_mem_kv': 1, 'w_gate': 2, 'w_br_sb': 2, 'w_br_gm': 2, 'w_br_xa': 2, 'w_out': 1,
            'w_up': 2, 'w_down': 1}
BIG = list(BIG_AXIS)
FWD_GROUPS = [['w_in'], ['w_mem_kv', 'w_gate'], ['w_br_sb', 'w_br_gm', 'w_br_xa', 'w_out'], ['w_up'], ['w_down']]
BWD_GROUPS = [['w_down', 'w_up'], ['w_out', 'w_br_sb', 'w_br_gm', 'w_br_xa', 'w_gate', 'w_mem_kv'], ['w_in']]
SMALL = ['g_mix_pre', 'g_vnorm', 'w_s', 'b_s', 'g_mem', 'b_gate', 'g_mix_post', 'g_ffn_pre', 'conv_b', 'g_ffn_post']
MESH = pl.DeviceIdType.MESH


def _pcall(body, **kw):
    return pl.pallas_call(body, **kw)


def _params(sem=None, vmem=VMEM_LIMIT):
    return pltpu.CompilerParams(dimension_semantics=sem, vmem_limit_bytes=vmem)


def _tile(n, cands):
    for c in cands:
        if n % c == 0:
            return c
    return n


_GELU_C = math.sqrt(2.0 / math.pi)
_GELU_A = 0.044715


def _gelu(x):
    return 0.5 * x * (1.0 + jnp.tanh(_GELU_C * (x + _GELU_A * (x * x * x))))


def _gelu_and_grad(x):
    x2 = x * x
    t = jnp.tanh(_GELU_C * (x + _GELU_A * (x2 * x)))
    val = 0.5 * x * (1.0 + t)
    grad = 0.5 * (1.0 + t) + 0.5 * x * (1.0 - t * t) * (_GELU_C * (1.0 + 3.0 * _GELU_A * x2))
    return val, grad


def _softplus(z):
    return jnp.maximum(z, 0.0) + jnp.log(1.0 + jnp.exp(-jnp.abs(z)))


def _dot(a, b):
    return jnp.dot(a, b, preferred_element_type=F32)


def _dot_nt(a, b):
    return lax.dot_general(a, b, (((1,), (1,)), ((), ())), preferred_element_type=F32)


def _dot_tn(a, b):
    return lax.dot_general(a, b, (((0,), (0,)), ((), ())), preferred_element_type=F32)


def _split_dot(a, m):
    out = None
    rest = a
    for _ in range(SPLITS):
        piece = rest.astype(BF16)
        rest = rest - piece.astype(F32)
        term = _dot(piece, m)
        out = term if out is None else out + term
    return out


def _mm(a, b, mode, out_dtype, name, tm=None, tn=None, tk=None, after=None):
    a_parts = list(a) if isinstance(a, (list, tuple)) else [a]
    b_parts = list(b) if isinstance(b, (list, tuple)) else [b]
    assert len(a_parts) == 1 or mode == 'nt'
    assert len(b_parts) == 1 or mode == 'tn'
    na, nb = len(a_parts), len(b_parts)
    if mode == 'nn':
        (m, kc), (kc2, n) = a_parts[0].shape, b_parts[0].shape
    elif mode == 'nt':
        (m, kp), (n, kc2) = a_parts[0].shape, b_parts[0].shape
        kc = kp * na
    else:
        (kc, m), (kc2, npiece) = a_parts[0].shape, b_parts[0].shape
        n = npiece * nb
    assert kc == kc2, (a_parts[0].shape, b_parts[0].shape, mode)
    tm = tm or _tile(m, (1024, 512, 256, 128))
    tn = tn or _tile(n // nb, (1024, 512, 256, 128))
    k_max = 3072 // na
    tk = tk or ((kc // na) if kc // na <= k_max else _tile(kc // na, [c for c in (3072, 2816, 2048, 1536, 1408, 1024, 512) if c <= k_max]))
    nk = kc // tk
    k_per = nk // na
    n_per = (n // tn) // nb
    dot = {'nn': _dot, 'nt': _dot_nt, 'tn': _dot_tn}[mode]

    def within(idx, p, per):
        return jnp.clip(idx - p * per, 0, per - 1)

    if mode == 'tn':
        a_specs = [pl.BlockSpec((tk, tm), lambda i, j, k: (k, i))]
        b_specs = [pl.BlockSpec((tk, tn), functools.partial(lambda i, j, k, p: (k, within(j, p, n_per)), p=p)) for p in range(nb)]
    else:
        a_specs = [pl.BlockSpec((tm, tk), functools.partial(lambda i, j, k, p: (i, within(k, p, k_per)), p=p)) for p in range(na)]
        b_specs = [pl.BlockSpec((tn, tk), lambda i, j, k: (j, k)) if mode == 'nt' else pl.BlockSpec((tk, tn), lambda i, j, k: (k, j))]

    extra = [] if after is None else [after]
    extra_specs = [pl.BlockSpec(memory_space=pl.ANY)] * len(extra)

    def product(a_refs, b_refs, store):
        if na == 1 and nb == 1:
            store(dot(a_refs[0][...].astype(BF16), b_refs[0][...].astype(BF16)))
            return
        which = (pl.program_id(2) // k_per) if na > 1 else (pl.program_id(1) // n_per)
        for p in range(max(na, nb)):
            @pl.when(which == p)
            def _(p=p):
                store(dot(a_refs[p if na > 1 else 0][...].astype(BF16), b_refs[p if nb > 1 else 0][...].astype(BF16)))

    if nk == 1:
        def body(*refs):
            o_ref = refs[-1]

            def store(part):
                o_ref[...] = part.astype(o_ref.dtype)

            product(refs[:na], refs[na:na + nb], store)
        scratch = []
    else:
        def body(*refs):
            o_ref, acc_ref = refs[-2], refs[-1]
            k = pl.program_id(2)

            def store(part):
                @pl.when(k == 0)
                def _():
                    acc_ref[...] = part

                @pl.when(k > 0)
                def _():
                    acc_ref[...] += part

            product(refs[:na], refs[na:na + nb], store)

            @pl.when(k == nk - 1)
            def _():
                o_ref[...] = acc_ref[...].astype(o_ref.dtype)
        scratch = [pltpu.VMEM((tm, tn), F32)]

    return _pcall(
        body, grid=(m // tm, n // tn, nk), in_specs=a_specs + b_specs + extra_specs,
        out_specs=pl.BlockSpec((tm, tn), lambda i, j, k: (i, j)),
        out_shape=jax.ShapeDtypeStruct((m, n), out_dtype), scratch_shapes=scratch, name=name,
        compiler_params=_params(("parallel", "parallel", "arbitrary")))(*a_parts, *b_parts, *extra)


def _norm_fwd(x, g, res, out_dtype, name, after=None):
    s, d = x.shape
    tr = _tile(s, (256, 128))
    has_res = res is not None
    has_after = after is not None

    def body(*refs):
        x_ref, g_ref = refs[0], refs[1]
        o_ref = refs[-1]
        xv = x_ref[...]
        y = xv * lax.rsqrt(jnp.mean(xv * xv, axis=-1, keepdims=True) + EPS) * g_ref[...]
        if has_res:
            y = y + refs[2][...]
        o_ref[...] = y.astype(o_ref.dtype)

    row = pl.BlockSpec((tr, d), lambda i: (i, 0))
    ins = [x, g] + ([res] if has_res else []) + ([after] if has_after else [])
    return _pcall(
        body, grid=(s // tr,),
        in_specs=[row, pl.BlockSpec((1, d), lambda i: (0, 0))] + ([row] if has_res else [])
        + ([pl.BlockSpec(memory_space=pl.ANY)] if has_after else []),
        out_specs=row, out_shape=jax.ShapeDtypeStruct((s, d), out_dtype), name=name,
        compiler_params=_params(("parallel",)))(*ins)


def _norm_bwd(x, g, douts, dres, out_dtype, name, after=None):
    s, d = x.shape
    tr = _tile(s, (256, 128))
    nd = len(douts)
    has_res = dres is not None
    has_after = after is not None

    def body(*refs):
        x_ref, g_ref = refs[0], refs[1]
        dx_ref, dg_ref = refs[-2], refs[-1]
        dout = refs[2][...].astype(F32)
        for r in refs[3:2 + nd]:
            dout = dout + r[...].astype(F32)
        xv = x_ref[...]
        r = lax.rsqrt(jnp.mean(xv * xv, axis=-1, keepdims=True) + EPS)
        n = xv * r
        dn = dout * g_ref[...]
        dx = r * (dn - n * jnp.mean(dn * n, axis=-1, keepdims=True))
        if has_res:
            dx = dx + refs[2 + nd][...]
        dx_ref[...] = dx.astype(dx_ref.dtype)

        @pl.when(pl.program_id(0) == 0)
        def _():
            dg_ref[...] = jnp.zeros_like(dg_ref)

        dg_ref[...] += jnp.sum(dout * n, axis=0, keepdims=True)

    row = pl.BlockSpec((tr, d), lambda i: (i, 0))
    vec = pl.BlockSpec((1, d), lambda i: (0, 0))
    ins = [x, g] + list(douts) + ([dres] if has_res else []) + ([after] if has_after else [])
    return _pcall(
        body, grid=(s // tr,),
        in_specs=[row, vec] + [row] * (nd + int(has_res)) + ([pl.BlockSpec(memory_space=pl.ANY)] if has_after else []),
        out_specs=[row, vec],
        out_shape=[jax.ShapeDtypeStruct((s, d), out_dtype), jax.ShapeDtypeStruct((1, d), F32)], name=name,
        compiler_params=_params(("arbitrary",)))(*ins)


def _loss_head(y, target):
    s, d = y.shape
    tr = _tile(s, (256, 128))

    def body(y_ref, t_ref, sq_ref, dy_ref):
        e = y_ref[...] - t_ref[...]
        dy_ref[...] = e * (1.0 / d)

        @pl.when(pl.program_id(0) == 0)
        def _():
            sq_ref[...] = jnp.zeros_like(sq_ref)

        sq_ref[...] += jnp.sum(e * e, axis=0, keepdims=True)

    row = pl.BlockSpec((tr, d), lambda i: (i, 0))
    return _pcall(
        body, grid=(s // tr,), in_specs=[row, row], out_specs=[pl.BlockSpec((1, d), lambda i: (0, 0)), row],
        out_shape=[jax.ShapeDtypeStruct((1, d), F32), jax.ShapeDtypeStruct((s, d), F32)], name="loss_head",
        compiler_params=_params(("arbitrary",)))(y, target)


NEVER = -1e30
SB_QUERIES = 512


def _sb_sum_matrix(later):
    r = lax.broadcasted_iota(jnp.int32, (HEAD_SB, 2 * HEAD_SB), 0)
    c = lax.broadcasted_iota(jnp.int32, (HEAD_SB, 2 * HEAD_SB), 1)
    tri = jnp.where((r > c) if later else (r < c), 1.0, 0.0)
    return jnp.where(c < HEAD_SB, tri, 1.0).astype(BF16)


def _sb_mask(tq, q0, k0):
    row = lax.broadcasted_iota(jnp.int32, (tq, HEAD_SB), 0)
    col = lax.broadcasted_iota(jnp.int32, (tq, HEAD_SB), 1)
    return (k0 + col) < (q0 + row)


def _sb_fwd(proj, n_heads):
    s = proj.shape[0]
    tq = min(SB_QUERIES, s)
    per = tq // HEAD_SB
    scale = HEAD_SB ** -0.5

    def body(q_ref, k_ref, v_ref, o_ref, a_ref, acc_ref, c_ref):
        i = pl.program_id(1)
        q = q_ref[...].astype(BF16)
        sums = _sb_sum_matrix(True)
        acc_ref[...] = jnp.zeros_like(acc_ref)
        c_ref[...] = jnp.zeros_like(c_ref)
        last = (i + 1) * per - 1

        def scores(j, masked):
            off = pl.multiple_of(j * HEAD_SB, HEAD_SB)
            z = _dot_nt(q, k_ref[pl.ds(off, HEAD_SB), :].astype(BF16)) * scale
            sp = _softplus(z)
            logb = z - sp
            if masked:
                mask = _sb_mask(tq, i * tq, off)
                logb = jnp.where(mask, logb, NEVER)
                sp = jnp.where(mask, sp, 0.0)
            return logb, _split_dot(sp, sums)

        def values(j, logb, both):
            off = pl.multiple_of(j * HEAD_SB, HEAD_SB)
            c = c_ref[...]
            a = jnp.exp(logb - both[:, :HEAD_SB] - c).astype(BF16)
            a_ref[0, 0, j] = a
            acc_ref[...] += _dot(a, v_ref[pl.ds(off, HEAD_SB), :].astype(BF16))
            c_ref[...] = c + both[:, HEAD_SB:]

        def step(jj, carry, masked):
            j = last - jj
            nxt = scores(j, masked)
            values(jnp.minimum(j + 1, last), *carry)
            return nxt

        idle = (jnp.full((tq, HEAD_SB), NEVER, F32), jnp.zeros((tq, 2 * HEAD_SB), F32))
        carry = idle
        for jj in range(per):
            carry = step(jj, carry, True)
        def group(jg, carry):
            for u in range(per):
                carry = step(per * jg + u, carry, False)
            return carry

        carry = lax.fori_loop(1, i + 1, group, carry)
        values(0, *carry)
        o_ref[...] = acc_ref[...].astype(o_ref.dtype)

    h = n_heads
    blk = pl.BlockSpec((tq, HEAD_SB), lambda hh, i: (i, hh))
    return _pcall(
        body, grid=(h, s // tq),
        in_specs=[blk, pl.BlockSpec((s, HEAD_SB), lambda hh, i: (0, h + hh)),
                  pl.BlockSpec((s, HEAD_SB), lambda hh, i: (0, 2 * h + hh))],
        out_specs=[blk, pl.BlockSpec((1, 1, s // HEAD_SB, tq, HEAD_SB), lambda hh, i: (hh, i, 0, 0, 0))],
        out_shape=[jax.ShapeDtypeStruct((s, h * HEAD_SB), BF16),
                   jax.ShapeDtypeStruct((h, s // tq, s // HEAD_SB, tq, HEAD_SB), BF16)],
        scratch_shapes=[pltpu.VMEM((tq, HEAD_SB), F32), pltpu.VMEM((tq, HEAD_SB), F32)],
        name="sb_fwd", compiler_params=_params(("parallel", "arbitrary")))(proj, proj, proj)


def _sb_bwd(proj, a_saved, do, n_heads, after=None):
    s = proj.shape[0]
    tq = min(SB_QUERIES, s)
    per = tq // HEAD_SB
    scale = HEAD_SB ** -0.5
    follow = [] if after is None else [after]

    def body(q_ref, k_ref, v_ref, do_ref, a_ref, *rest):
        dq_ref, dk_ref, dv_ref, run_ref, acc_ref = rest[len(follow):]
        i = pl.program_id(1)

        @pl.when(i == 0)
        def _():
            dk_ref[...] = jnp.zeros_like(dk_ref)
            dv_ref[...] = jnp.zeros_like(dv_ref)

        q = q_ref[...].astype(BF16)
        dob = do_ref[...].astype(BF16)
        run_ref[...] = jnp.zeros_like(run_ref)
        acc_ref[...] = jnp.zeros_like(acc_ref)
        earlier = _sb_sum_matrix(False)
        first_diagonal = i * per

        def step(j, masked):
            off = pl.multiple_of(j * HEAD_SB, HEAD_SB)
            kb = k_ref[pl.ds(off, HEAD_SB), :].astype(BF16)
            vb = v_ref[pl.ds(off, HEAD_SB), :].astype(BF16)
            a = a_ref[0, 0, j]
            g = a.astype(F32) * _dot_nt(dob, vb)
            dv_ref[pl.ds(off, HEAD_SB), :] += _dot_tn(a, dob)
            z = _dot_nt(q, kb) * scale
            beta = 1.0 / (1.0 + jnp.exp(-z))
            both = _split_dot(g, earlier)
            p = run_ref[...]
            dz = (g * (1.0 - beta) - beta * (both[:, :HEAD_SB] + p)) * scale
            if masked:
                dz = jnp.where(_sb_mask(tq, i * tq, off), dz, 0.0)
            dzb = dz.astype(BF16)
            dk_ref[pl.ds(off, HEAD_SB), :] += _dot_tn(dzb, q)
            acc_ref[...] += _dot(dzb, kb)
            run_ref[...] = p + both[:, HEAD_SB:]

        def group(jg, carry):
            for u in range(per):
                step(per * jg + u, False)
            return carry

        lax.fori_loop(0, i, group, 0)
        for u in range(per):
            step(first_diagonal + u, True)
        dq_ref[...] = acc_ref[...]

    h = n_heads
    blk = pl.BlockSpec((tq, HEAD_SB), lambda hh, i: (i, hh))
    col_blk = pl.BlockSpec((s, HEAD_SB), lambda hh, i: (0, hh))
    shape = jax.ShapeDtypeStruct((s, h * HEAD_SB), F32)
    return _pcall(
        body, grid=(h, s // tq),
        in_specs=[blk, pl.BlockSpec((s, HEAD_SB), lambda hh, i: (0, h + hh)),
                  pl.BlockSpec((s, HEAD_SB), lambda hh, i: (0, 2 * h + hh)), blk,
                  pl.BlockSpec((1, 1, s // HEAD_SB, tq, HEAD_SB), lambda hh, i: (hh, i, 0, 0, 0))]
        + [pl.BlockSpec(memory_space=pl.ANY)] * len(follow),
        out_specs=[blk, col_blk, col_blk], out_shape=[shape, shape, shape],
        scratch_shapes=[pltpu.VMEM((tq, HEAD_SB), F32), pltpu.VMEM((tq, HEAD_SB), F32)],
        name="sb_bwd", compiler_params=_params(("parallel", "arbitrary")))(proj, proj, proj, do, a_saved, *follow)


def _gm_mask():
    t = lax.broadcasted_iota(jnp.int32, (GROUP_GM, GROUP_GM), 0)
    s = lax.broadcasted_iota(jnp.int32, (GROUP_GM, GROUP_GM), 1)
    shift = CHUNK.bit_length() - 1
    return (s >> shift) <= (t >> shift)


def _gm_fwd(proj, g_vnorm, w_s, b_st, u_blk):
    s = proj.shape[0]
    groups = w_s.shape[0]
    w = groups * GROUP_GM

    def body(u_ref, v_ref, gv_ref, ws_ref, bst_ref, o_ref):
        ug = _gelu(u_ref[...])
        vg = _gelu(v_ref[...])
        vn = vg * lax.rsqrt(jnp.mean(vg * vg, axis=-1, keepdims=True) + EPS) * gv_ref[...]
        vnb = vn.astype(BF16)
        mask = _gm_mask()
        for g in range(groups):
            sl = slice(g * GROUP_GM, (g + 1) * GROUP_GM)
            wm = jnp.where(mask, ws_ref[g], 0.0).astype(BF16)
            mixed = _dot(wm, vnb[:, sl]) + bst_ref[:, g:g + 1]
            o_ref[:, sl] = (ug[:, sl] * mixed).astype(o_ref.dtype)

    return _pcall(
        body, grid=(s // GROUP_GM,),
        in_specs=[pl.BlockSpec((GROUP_GM, w), lambda c: (c, u_blk)), pl.BlockSpec((GROUP_GM, w), lambda c: (c, u_blk + 1)),
                  pl.BlockSpec((1, w), lambda c: (0, 0)), pl.BlockSpec((groups, GROUP_GM, GROUP_GM), lambda c: (0, 0, 0)),
                  pl.BlockSpec((GROUP_GM, groups), lambda c: (0, 0))],
        out_specs=pl.BlockSpec((GROUP_GM, w), lambda c: (c, 0)),
        out_shape=jax.ShapeDtypeStruct((s, w), BF16), name="gm_fwd",
        compiler_params=_params(("parallel",)))(proj, proj, g_vnorm, w_s, b_st)


def _gm_bwd(proj, g_vnorm, w_s, b_st, do, u_blk):
    s = proj.shape[0]
    groups = w_s.shape[0]
    w = groups * GROUP_GM

    def body(u_ref, v_ref, gv_ref, ws_ref, bst_ref, do_ref, du_ref, dv_ref, dgv_ref, dws_ref, dbst_ref, dvn_ref):
        @pl.when(pl.program_id(0) == 0)
        def _():
            dgv_ref[...] = jnp.zeros_like(dgv_ref)
            dws_ref[...] = jnp.zeros_like(dws_ref)
            dbst_ref[...] = jnp.zeros_like(dbst_ref)

        ug, ugrad = _gelu_and_grad(u_ref[...])
        vg, vgrad = _gelu_and_grad(v_ref[...])
        r = lax.rsqrt(jnp.mean(vg * vg, axis=-1, keepdims=True) + EPS)
        n = vg * r
        gv = gv_ref[...]
        vnb = (n * gv).astype(BF16)
        dout = do_ref[...]
        mask = _gm_mask()
        for g in range(groups):
            sl = slice(g * GROUP_GM, (g + 1) * GROUP_GM)
            wm = jnp.where(mask, ws_ref[g], 0.0).astype(BF16)
            mixed = _dot(wm, vnb[:, sl]) + bst_ref[:, g:g + 1]
            dmixed = dout[:, sl] * ug[:, sl]
            du_ref[:, sl] = dout[:, sl] * mixed * ugrad[:, sl]
            dbst_ref[:, g:g + 1] += jnp.sum(dmixed, axis=1, keepdims=True)
            dmb = dmixed.astype(BF16)
            dws_ref[g] += jnp.where(mask, _dot_nt(dmb, vnb[:, sl]), 0.0)
            dvn_ref[:, sl] = _dot_tn(wm, dmb)
        dvn = dvn_ref[...]
        dgv_ref[...] += jnp.sum(dvn * n, axis=0, keepdims=True)
        dn = dvn * gv
        dvg = r * (dn - n * jnp.mean(dn * n, axis=-1, keepdims=True))
        dv_ref[...] = dvg * vgrad

    rowb = pl.BlockSpec((GROUP_GM, w), lambda c: (c, 0))
    vec = pl.BlockSpec((1, w), lambda c: (0, 0))
    wsb = pl.BlockSpec((groups, GROUP_GM, GROUP_GM), lambda c: (0, 0, 0))
    bsb = pl.BlockSpec((GROUP_GM, groups), lambda c: (0, 0))
    return _pcall(
        body, grid=(s // GROUP_GM,),
        in_specs=[pl.BlockSpec((GROUP_GM, w), lambda c: (c, u_blk)), pl.BlockSpec((GROUP_GM, w), lambda c: (c, u_blk + 1)),
                  vec, wsb, bsb, rowb],
        out_specs=[rowb, rowb, vec, wsb, bsb],
        out_shape=[jax.ShapeDtypeStruct((s, w), F32), jax.ShapeDtypeStruct((s, w), F32), jax.ShapeDtypeStruct((1, w), F32),
                   jax.ShapeDtypeStruct((groups, GROUP_GM, GROUP_GM), F32), jax.ShapeDtypeStruct((GROUP_GM, groups), F32)],
        scratch_shapes=[pltpu.VMEM((GROUP_GM, w), F32)], name="gm_bwd",
        compiler_params=_params(("arbitrary",)))(proj, proj, g_vnorm, w_s, b_st, do)


def _xa_fwd(proj, mem_kv, q_blk, n_heads):
    s = proj.shape[0]
    nm = mem_kv.shape[0]
    tq = _tile(s, (512, 256, 128))
    scale = HEAD_XA ** -0.5

    def body(q_ref, k_ref, v_ref, o_ref):
        z = _dot_nt(q_ref[...].astype(BF16), k_ref[...].astype(BF16)) * scale
        z = z - jnp.max(z, axis=-1, keepdims=True)
        e = jnp.exp(z)
        p = e / jnp.sum(e, axis=-1, keepdims=True)
        o_ref[...] = _dot(p.astype(BF16), v_ref[...].astype(BF16)).astype(o_ref.dtype)

    h = n_heads
    return _pcall(
        body, grid=(h, s // tq),
        in_specs=[pl.BlockSpec((tq, HEAD_XA), lambda hh, i: (i, q_blk + hh)),
                  pl.BlockSpec((nm, HEAD_XA), lambda hh, i: (0, hh)), pl.BlockSpec((nm, HEAD_XA), lambda hh, i: (0, h + hh))],
        out_specs=pl.BlockSpec((tq, HEAD_XA), lambda hh, i: (i, hh)),
        out_shape=jax.ShapeDtypeStruct((s, h * HEAD_XA), BF16), name="xa_fwd",
        compiler_params=_params(("parallel", "parallel")))(proj, mem_kv, mem_kv)


def _xa_bwd(proj, mem_kv, do, q_blk, n_heads):
    s = proj.shape[0]
    nm = mem_kv.shape[0]
    tq = _tile(s, (512, 256, 128))
    scale = HEAD_XA ** -0.5
    h = n_heads

    def body(q_ref, k_ref, v_ref, do_ref, dq_ref, dk_ref, dv_ref):
        @pl.when(pl.program_id(1) == 0)
        def _():
            dk_ref[...] = jnp.zeros_like(dk_ref)
            dv_ref[...] = jnp.zeros_like(dv_ref)

        qb = q_ref[...].astype(BF16)
        kb = k_ref[...].astype(BF16)
        vb = v_ref[...].astype(BF16)
        dob = do_ref[...].astype(BF16)
        z = _dot_nt(qb, kb) * scale
        z = z - jnp.max(z, axis=-1, keepdims=True)
        e = jnp.exp(z)
        p = e / jnp.sum(e, axis=-1, keepdims=True)
        dp = _dot_nt(dob, vb)
        dz = (p * (dp - jnp.sum(dp * p, axis=-1, keepdims=True)) * scale).astype(BF16)
        dq_ref[...] = _dot(dz, kb)
        dk_ref[...] += _dot_tn(dz, qb)
        dv_ref[...] += _dot_tn(p.astype(BF16), dob)

    qspec = pl.BlockSpec((tq, HEAD_XA), lambda hh, i: (i, hh))
    dk, dv = None, None
    dq, dk, dv = _pcall(
        body, grid=(h, s // tq),
        in_specs=[pl.BlockSpec((tq, HEAD_XA), lambda hh, i: (i, q_blk + hh)),
                  pl.BlockSpec((nm, HEAD_XA), lambda hh, i: (0, hh)), pl.BlockSpec((nm, HEAD_XA), lambda hh, i: (0, h + hh)),
                  qspec],
        out_specs=[qspec, pl.BlockSpec((nm, HEAD_XA), lambda hh, i: (0, hh)), pl.BlockSpec((nm, HEAD_XA), lambda hh, i: (0, hh))],
        out_shape=[jax.ShapeDtypeStruct((s, h * HEAD_XA), F32), jax.ShapeDtypeStruct((nm, h * HEAD_XA), F32),
                   jax.ShapeDtypeStruct((nm, h * HEAD_XA), F32)],
        name="xa_bwd", compiler_params=_params(("parallel", "arbitrary")))(proj, mem_kv, mem_kv, do)
    return dq, dk, dv


def _merge_fwd(zg, b_gate, branches):
    s, d = branches[0].shape
    tr = _tile(s, (128,))

    def body(z0, z1, z2, g0, g1, g2, b0, b1, b2, o_ref):
        acc = None
        for z, g, b in ((z0, g0, b0), (z1, g1, b1), (z2, g2, b2)):
            term = jax.nn.sigmoid(z[...].astype(F32) + g[...]) * b[...]
            acc = term if acc is None else acc + term
        o_ref[...] = acc.astype(o_ref.dtype)

    zs = [pl.BlockSpec((tr, d), functools.partial(lambda i, k: (i, k), k=k)) for k in range(3)]
    gs = [pl.BlockSpec((1, d), functools.partial(lambda i, k: (0, k), k=k)) for k in range(3)]
    row = pl.BlockSpec((tr, d), lambda i: (i, 0))
    return _pcall(
        body, grid=(s // tr,), in_specs=zs + gs + [row] * 3, out_specs=row,
        out_shape=jax.ShapeDtypeStruct((s, d), BF16), name="merge_fwd",
        compiler_params=_params(("parallel",)))(zg, zg, zg, b_gate, b_gate, b_gate, *branches)


def _merge_bwd(zg, b_gate, branches, dmerged):
    s, d = branches[0].shape
    tr = _tile(s, (128,))

    def body(z0, z1, z2, g0, g1, g2, b0, b1, b2, dm_ref, dz_ref, d0, d1, d2, dbg_ref):
        @pl.when(pl.program_id(0) == 0)
        def _():
            dbg_ref[...] = jnp.zeros_like(dbg_ref)

        dm = dm_ref[...]
        for k, (z, g, b, dbr) in enumerate(((z0, g0, b0, d0), (z1, g1, b1, d1), (z2, g2, b2, d2))):
            sg = jax.nn.sigmoid(z[...].astype(F32) + g[...])
            dbr[...] = (dm * sg).astype(dbr.dtype)
            dz = dm * b[...] * sg * (1.0 - sg)
            dz_ref[:, k * d:(k + 1) * d] = dz.astype(dz_ref.dtype)
            dbg_ref[:, k * d:(k + 1) * d] += jnp.sum(dz, axis=0, keepdims=True)

    zs = [pl.BlockSpec((tr, d), functools.partial(lambda i, k: (i, k), k=k)) for k in range(3)]
    gs = [pl.BlockSpec((1, d), functools.partial(lambda i, k: (0, k), k=k)) for k in range(3)]
    row = pl.BlockSpec((tr, d), lambda i: (i, 0))
    outs = _pcall(
        body, grid=(s // tr,), in_specs=zs + gs + [row] * 4,
        out_specs=[pl.BlockSpec((tr, 3 * d), lambda i: (i, 0)), row, row, row, pl.BlockSpec((1, 3 * d), lambda i: (0, 0))],
        out_shape=[jax.ShapeDtypeStruct((s, 3 * d), BF16)] + [jax.ShapeDtypeStruct((s, d), BF16)] * 3
        + [jax.ShapeDtypeStruct((1, 3 * d), F32)],
        name="merge_bwd", compiler_params=_params(("arbitrary",)))(zg, zg, zg, b_gate, b_gate, b_gate, *branches, dmerged)
    return outs[0], list(outs[1:4]), outs[4]


def _shift_down(x, k, row):
    return jnp.where(row >= k, pltpu.roll(x, k, 0), 0.0)


def _shift_up(x, k, row, s):
    return jnp.where(row < s - k, pltpu.roll(x, s - k, 0), 0.0)


def _conv_pre(gate, cw_ref, cb_ref, row):
    conv = cb_ref[...] + cw_ref[CONV_TAPS - 1:CONV_TAPS, :] * gate
    for k in range(1, CONV_TAPS):
        conv = conv + cw_ref[CONV_TAPS - 1 - k:CONV_TAPS - k, :] * _shift_down(gate, k, row)
    return conv


def _cg_fwd(up, conv_w, conv_b):
    s = up.shape[0]
    f = conv_w.shape[1]
    tc = _tile(f, (256, 128))
    nb = f // tc

    def body(g_ref, v_ref, cw_ref, cb_ref, o_ref, gel_ref, ggrad_ref):
        row = lax.broadcasted_iota(jnp.int32, (s, tc), 0)
        conv = _conv_pre(g_ref[...].astype(F32), cw_ref, cb_ref, row)
        gel, ggrad = _gelu_and_grad(conv)
        o_ref[...] = (gel * v_ref[...].astype(F32)).astype(o_ref.dtype)
        gel_ref[...] = gel.astype(gel_ref.dtype)
        ggrad_ref[...] = ggrad.astype(ggrad_ref.dtype)

    colb = pl.BlockSpec((s, tc), lambda j: (0, j))
    shape = jax.ShapeDtypeStruct((s, f), BF16)
    return _pcall(
        body, grid=(nb,),
        in_specs=[colb, pl.BlockSpec((s, tc), lambda j: (0, nb + j)),
                  pl.BlockSpec((CONV_TAPS, tc), lambda j: (0, j)), pl.BlockSpec((1, tc), lambda j: (0, j))],
        out_specs=[colb, colb, colb], out_shape=[shape, shape, shape], name="cg_fwd",
        compiler_params=_params(("parallel",)))(up, up, conv_w, conv_b)


def _cg_bwd(up, gel, ggrad, conv_w, dact):
    s = up.shape[0]
    f = conv_w.shape[1]
    tc = _tile(f, (256, 128))
    nb = f // tc

    def body(g_ref, v_ref, gel_ref, gg_ref, cw_ref, da_ref, dg_ref, dv_ref, dcw_ref, dcb_ref):
        row = lax.broadcasted_iota(jnp.int32, (s, tc), 0)
        gate = g_ref[...].astype(F32)
        da = da_ref[...]
        dv_ref[...] = (da * gel_ref[...].astype(F32)).astype(dv_ref.dtype)
        dconv = da * v_ref[...].astype(F32) * gg_ref[...].astype(F32)
        dgate = cw_ref[CONV_TAPS - 1:CONV_TAPS, :] * dconv
        dcw_ref[CONV_TAPS - 1:CONV_TAPS, :] = jnp.sum(dconv * gate, axis=0, keepdims=True)
        for k in range(1, CONV_TAPS):
            dgate = dgate + cw_ref[CONV_TAPS - 1 - k:CONV_TAPS - k, :] * _shift_up(dconv, k, row, s)
            dcw_ref[CONV_TAPS - 1 - k:CONV_TAPS - k, :] = jnp.sum(dconv * _shift_down(gate, k, row), axis=0, keepdims=True)
        dg_ref[...] = dgate.astype(dg_ref.dtype)
        dcb_ref[...] = jnp.sum(dconv, axis=0, keepdims=True)

    colb = pl.BlockSpec((s, tc), lambda j: (0, j))
    return _pcall(
        body, grid=(nb,),
        in_specs=[colb, pl.BlockSpec((s, tc), lambda j: (0, nb + j)), colb, colb,
                  pl.BlockSpec((CONV_TAPS, tc), lambda j: (0, j)), colb],
        out_specs=[colb, colb, pl.BlockSpec((CONV_TAPS, tc), lambda j: (0, j)), pl.BlockSpec((1, tc), lambda j: (0, j))],
        out_shape=[jax.ShapeDtypeStruct((s, f), BF16), jax.ShapeDtypeStruct((s, f), BF16),
                   jax.ShapeDtypeStruct((CONV_TAPS, f), F32), jax.ShapeDtypeStruct((1, f), F32)],
        name="cg_bwd", compiler_params=_params(("parallel",)))(up, up, gel, ggrad, conv_w, dact)


def _row_tile(rows, cols, elems=256 * 1024):
    want = max(16, elems // cols)
    for c in (512, 256, 128, 64, 32, 16):
        if c <= want and rows % c == 0:
            return c
    return rows


def _sum_halves(dwv, recv, core, name):
    nj, _, a, c = dwv.shape
    tr = _row_tile(a, c, 1024 * 1024)

    def body(core_ref, d_ref, r_ref, o_ref):
        o_ref[0] = (d_ref[0, 0].astype(F32) + r_ref[0].astype(F32)).astype(o_ref.dtype)

    grid_spec = pltpu.PrefetchScalarGridSpec(
        num_scalar_prefetch=1, grid=(nj, a // tr),
        in_specs=[pl.BlockSpec((1, 1, tr, c), lambda j, i, cr: (j, cr[0], i, 0)),
                  pl.BlockSpec((1, tr, c), lambda j, i, cr: (j, i, 0))],
        out_specs=pl.BlockSpec((1, tr, c), lambda j, i, cr: (j, i, 0)))
    return _pcall(body, grid_spec=grid_spec, out_shape=jax.ShapeDtypeStruct((nj, a, c), BF16), name=name,
                  compiler_params=_params(("parallel", "parallel")))(core, dwv, recv)


def _sum_chips(recv, own, chip, core, ax):
    _, a, b = recv.shape
    tr = _row_tile(a, b, 512 * 1024)

    def body(chip_ref, core_ref, r_ref, own_ref, o_ref):
        me = chip_ref[0]
        mine = own_ref[0].astype(F32)
        acc = None
        for k in range(N_CHIPS):
            term = jnp.where(me == k, mine, r_ref[k].astype(F32))
            acc = term if acc is None else acc + term
        o_ref[0] = acc

    own_spec = (pl.BlockSpec((1, tr, b), lambda i, ch, co: (0, i, ch[0])) if ax == 2
                else pl.BlockSpec((1, tr, b), lambda i, ch, co: (ch[0], i, 0)))
    grid_spec = pltpu.PrefetchScalarGridSpec(
        num_scalar_prefetch=2, grid=(a // tr,),
        in_specs=[pl.BlockSpec((N_CHIPS, tr, b), lambda i, ch, co: (0, i, 0)), own_spec],
        out_specs=pl.BlockSpec((1, tr, b), lambda i, ch, co: (co[0], i, 0)))
    return _pcall(body, grid_spec=grid_spec, out_shape=jax.ShapeDtypeStruct((2, a, b), F32),
                  name="sum_chips", compiler_params=_params(("parallel",)))(chip, core, recv, own)


def _place_own(wt, layer, chip, ax, after):
    nl, r, c = wt.shape
    half = r // 2
    tr = _row_tile(half, c, 512 * 1024)
    nb = half // tr

    def body(chip_ref, w_ref, after_ref, o_ref):
        o_ref[...] = w_ref[...].astype(BF16).reshape(o_ref.shape)

    if ax == 2:
        out_spec = pl.BlockSpec((1, tr, c), lambda h, i, ch: (h, i, ch[0]))
    else:
        out_spec = pl.BlockSpec((1, 1, tr, c), lambda h, i, ch: (ch[0], h, i, 0))
    grid_spec = pltpu.PrefetchScalarGridSpec(
        num_scalar_prefetch=1, grid=(2, nb),
        in_specs=[pl.BlockSpec((1, tr, c), lambda h, i, ch: (layer, h * nb + i, 0)), pl.BlockSpec(memory_space=pl.ANY)],
        out_specs=out_spec)
    return _pcall(body, grid_spec=grid_spec, out_shape=jax.ShapeDtypeStruct(_full_view_shape(wt.shape, ax), BF16),
                  name="place_own", compiler_params=_params(("parallel", "parallel")))(chip, wt, after)


def _adamw(w, m, v, g, layer, prev, name, after=None):
    nl, r, c = w.shape
    tr = _row_tile(r, c, 512 * 1024)
    c1 = 1.0 - ADAM_B1 ** ADAM_STEP
    c2 = 1.0 - ADAM_B2 ** ADAM_STEP

    follow = [] if after is None else [after]

    def body(w_ref, m_ref, v_ref, gin_ref, *rest):
        g_ref, d_ref, nm_ref, nv_ref = rest[-4:]
        g = gin_ref[...]
        mm = ADAM_B1 * m_ref[0] + (1.0 - ADAM_B1) * g
        vv = ADAM_B2 * v_ref[0] + (1.0 - ADAM_B2) * (g * g)
        g_ref[0] = g
        nm_ref[0] = mm
        nv_ref[0] = vv
        d_ref[0] = -ADAM_LR * ((mm / c1) / (jnp.sqrt(vv / c2) + ADAM_EPS) + ADAM_WD * w_ref[0])

    blk = pl.BlockSpec((1, tr, c), lambda i: (layer, i, 0))
    shape = jax.ShapeDtypeStruct((nl, r, c), F32)
    extra = [] if prev is None else list(prev)
    return _pcall(
        body, grid=(r // tr,),
        in_specs=[blk] * 3 + [pl.BlockSpec((tr, c), lambda i: (i, 0))] + [pl.BlockSpec(memory_space=pl.ANY)] * (len(extra) + len(follow)),
        out_specs=[blk] * 4, out_shape=[shape] * 4, input_output_aliases={4 + k: k for k in range(len(extra))}, name=name,
        compiler_params=_params(("parallel",)))(w, m, v, g, *extra, *follow)


HBM_SPEC = pl.BlockSpec(memory_space=pltpu.HBM)
COMM = pltpu.CompilerParams(has_side_effects=True)


def _position():
    x, y, c = lax.axis_index("x"), lax.axis_index("y"), lax.axis_index("c")
    chips = [(1 - x, y), (x, 1 - y), (1 - x, 1 - y)]
    return x, y, c, chips


def _remote(src, dst, send_sem, recv_sem, dev):
    return pltpu.make_async_remote_copy(src_ref=src, dst_ref=dst, send_sem=send_sem, recv_sem=recv_sem,
                                        device_id=dev, device_id_type=MESH)


def _full_view_shape(shard_shape, ax):
    _, r, c = shard_shape
    return (2, r // 2, c * N_CHIPS) if ax == 2 else (N_CHIPS, 2, r // 2, c)


def _piece(ref, ax, j, h, cs):
    if ax == 2:
        return ref.at[h, :, pl.ds(pl.multiple_of(j * cs, cs), cs)]
    return ref.at[j, h]


def _chip_block(ref, ax, j, cs):
    if ax == 2:
        return ref.at[:, :, pl.ds(pl.multiple_of(j * cs, cs), cs)]
    return ref.at[j]


SEM_SPEC = pl.BlockSpec(memory_space=pltpu.SEMAPHORE)
ANY_SPEC = pl.BlockSpec(memory_space=pl.ANY)
SPLIT = pltpu.CompilerParams(has_side_effects=pltpu.SideEffectType.DATAFLOW_SIDE_EFFECTING)


def _exchange(kind, name, bufs, build, n_sems, after=None, sems=None):
    n = len(bufs)
    if kind == 'sync':
        def body(*refs):
            mine, theirs = build(refs[n:2 * n], refs[2 * n], refs[2 * n + 1])
            for cp in mine:
                cp.start()
            for cp in theirs:
                cp.wait_recv()
            for cp in mine:
                cp.wait_send()

        return list(_pcall(
            body, in_specs=[HBM_SPEC] * n, out_specs=[HBM_SPEC] * n,
            out_shape=[jax.ShapeDtypeStruct(v.shape, v.dtype) for v in bufs], input_output_aliases={t: t for t in range(n)},
            scratch_shapes=[pltpu.SemaphoreType.DMA((n_sems,)), pltpu.SemaphoreType.DMA((n_sems,))],
            name=name, compiler_params=COMM)(*bufs))
    if kind == 'start':
        def body(*refs):
            mine, _ = build(refs[n + 3:2 * n + 3], refs[n + 1], refs[n + 2])
            for cp in mine:
                cp.start()
            refs[2 * n + 3][...] = jnp.zeros_like(refs[2 * n + 3])

        outs = _pcall(
            body, in_specs=[HBM_SPEC] * n + [ANY_SPEC],
            out_specs=[SEM_SPEC, SEM_SPEC] + [HBM_SPEC] * n + [pl.BlockSpec(memory_space=pltpu.VMEM)],
            out_shape=[pltpu.SemaphoreType.DMA((n_sems,)), pltpu.SemaphoreType.DMA((n_sems,))]
            + [pltpu.HBM(v.shape, v.dtype) for v in bufs] + [jax.ShapeDtypeStruct((8, LANES), F32)],
            input_output_aliases={t: 2 + t for t in range(n)}, name=name,
            compiler_params=SPLIT)(*[pltpu.with_memory_space_constraint(v, pltpu.HBM) for v in bufs], after)
        return (outs[0], outs[1]), list(outs[2:2 + n]), outs[2 + n]

    def body(*refs):
        mine, theirs = build(refs[:n], refs[n], refs[n + 1])
        for cp in mine:
            cp.wait_send()
        for cp in theirs:
            cp.wait_recv()

    return list(_pcall(
        body, in_specs=[HBM_SPEC] * n + [SEM_SPEC, SEM_SPEC, ANY_SPEC], out_specs=[HBM_SPEC] * n,
        out_shape=[pltpu.HBM(v.shape, v.dtype) for v in bufs], input_output_aliases={t: t for t in range(n)},
        name=name, compiler_params=SPLIT)(*bufs, sems[0], sems[1], after))


def _gather_ici_copies(axes, shard_cols):
    def build(bufs, send_sems, recv_sems):
        x, y, c, chips = _position()
        me = 2 * x + y
        mine, theirs = [], []
        for t, ax in enumerate(axes):
            own = _piece(bufs[t], ax, me, c, shard_cols[t])
            for p, (px, py) in enumerate(chips):
                k = t * 3 + p
                got = _piece(bufs[t], ax, 2 * px + py, c, shard_cols[t])
                mine.append(_remote(own, own, send_sems.at[k], recv_sems.at[k], (px, py, c)))
                theirs.append(_remote(got, got, send_sems.at[k], recv_sems.at[k], (px, py, c)))
        return mine, theirs
    return build


def _gather_d2d_copies(axes, shard_cols):
    def build(bufs, send_sems, recv_sems):
        x, y, c, chips = _position()
        mine, theirs = [], []
        for t, ax in enumerate(axes):
            for p, (px, py) in enumerate(chips):
                k = t * 3 + p
                had = _piece(bufs[t], ax, 2 * px + py, c, shard_cols[t])
                got = _piece(bufs[t], ax, 2 * px + py, 1 - c, shard_cols[t])
                mine.append(_remote(had, had, send_sems.at[k], recv_sems.at[k], (x, y, 1 - c)))
                theirs.append(_remote(got, got, send_sems.at[k], recv_sems.at[k], (x, y, 1 - c)))
        return mine, theirs
    return build


def _grads_d2d_copies(n):
    def build(bufs, send_sems, recv_sems):
        x, y, c, _ = _position()
        mine = [_remote(bufs[t].at[:, 1 - c], bufs[n + t], send_sems.at[t], recv_sems.at[t], (x, y, 1 - c)) for t in range(n)]
        return mine, mine
    return build


def _grads_ici_copies(axes):
    n = len(axes)

    def build(bufs, send_sems, recv_sems):
        x, y, c, chips = _position()
        me = 2 * x + y

        def block(t, j):
            if axes[t] == 2:
                cs = bufs[n + t].shape[2]
                return bufs[t].at[0, :, pl.ds(pl.multiple_of(j * cs, cs), cs)]
            return bufs[t].at[j]

        mine, theirs = [], []
        for t in range(n):
            for p, (px, py) in enumerate(chips):
                k = t * 3 + p
                peer = 2 * px + py
                mine.append(_remote(block(t, peer), bufs[n + t].at[me], send_sems.at[k], recv_sems.at[k], (px, py, c)))
                theirs.append(_remote(block(t, peer), bufs[n + t].at[peer], send_sems.at[k], recv_sems.at[k], (px, py, c)))
        return mine, theirs
    return build


def _join_copies(bufs, send_sems, recv_sems):
    x, y, c, _ = _position()
    mine = [_remote(b.at[c], b.at[c], send_sems.at[t], recv_sems.at[t], (x, y, 1 - c)) for t, b in enumerate(bufs)]
    theirs = [_remote(b.at[1 - c], b.at[1 - c], send_sems.at[t], recv_sems.at[t], (x, y, 1 - c)) for t, b in enumerate(bufs)]
    return mine, theirs


def _grads_recv_shape(sm, ax):
    _, a, c = sm.shape
    return (N_CHIPS, a, c // N_CHIPS if ax == 2 else c)


def _gather_small(shard):
    nl, r, cs = shard.shape

    def body(in_ref, out_ref, send_sems, recv_sems, local_sem):
        x, y, c, chips = _position()

        def cols(j):
            return out_ref.at[:, :, pl.ds(pl.multiple_of(j * cs, cs), cs)]

        me = 2 * x + y
        loc = pltpu.make_async_copy(in_ref, cols(me), local_sem)
        loc.start()
        remote = [_remote(in_ref, cols(me), send_sems.at[p], recv_sems.at[p], (px, py, c)) for p, (px, py) in enumerate(chips)]
        for cp in remote:
            cp.start()
        for p, (px, py) in enumerate(chips):
            _remote(in_ref, cols(2 * px + py), send_sems.at[p], recv_sems.at[p], (px, py, c)).wait_recv()
        for cp in remote:
            cp.wait_send()
        loc.wait()

    return _pcall(
        body, in_specs=[HBM_SPEC], out_specs=HBM_SPEC, out_shape=jax.ShapeDtypeStruct((nl, r, cs * N_CHIPS), shard.dtype),
        scratch_shapes=[pltpu.SemaphoreType.DMA((3,)), pltpu.SemaphoreType.DMA((3,)), pltpu.SemaphoreType.DMA(())],
        name="gather_small", compiler_params=COMM)(shard)


def _small_copies(bufs, send_sems, recv_sems):
    pack, land = bufs
    x, y, cc, _ = _position()
    me = 4 * x + 2 * y + cc
    mine, theirs = [], []
    for k in range(1, N_DEV):
        px, py, pc = x ^ ((k >> 2) & 1), y ^ ((k >> 1) & 1), cc ^ (k & 1)
        mine.append(_remote(pack, land.at[me], send_sems.at[k - 1], recv_sems.at[k - 1], (px, py, pc)))
        theirs.append(_remote(pack, land.at[4 * px + 2 * py + pc], send_sems.at[k - 1], recv_sems.at[k - 1], (px, py, pc)))
    return mine, theirs


def _sum_devices(land, pack, dev):
    _, r, c = land.shape
    tr = _tile(r, (672, 512, 256, 128, 64, 8))

    def body(dev_ref, l_ref, p_ref, o_ref):
        me = dev_ref[0]
        acc = None
        for k in range(N_DEV):
            term = jnp.where(me == k, p_ref[...], l_ref[k])
            acc = term if acc is None else acc + term
        o_ref[...] = acc

    grid_spec = pltpu.PrefetchScalarGridSpec(
        num_scalar_prefetch=1, grid=(r // tr,),
        in_specs=[pl.BlockSpec((N_DEV, tr, c), lambda i, dv: (0, i, 0)), pl.BlockSpec((tr, c), lambda i, dv: (i, 0))],
        out_specs=pl.BlockSpec((tr, c), lambda i, dv: (i, 0)))
    return _pcall(body, grid_spec=grid_spec, out_shape=jax.ShapeDtypeStruct((r, c), F32), name="sum_devices",
                  compiler_params=_params(("parallel",)))(dev, land, pack)


def _dims(d):
    half = d // 2
    return half // HEAD_SB, half // HEAD_XA, 3, (5 * half) // HEAD_XA


def _layer_fwd(x, mem, weight, small, l, after=None):
    h_sb, h_xa, u_blk, q_blk = _dims(x.shape[1])

    def vec(name):
        return small[name][l].reshape(1, -1)

    def use(a, name, follows, mm_name, dtype=F32):
        wt = weight(name, follows)
        return _mm(a, wt, 'nn', dtype, mm_name, after=weight.token())

    h1 = _norm_fwd(x, vec('g_mix_pre'), None, BF16, "norm_mix_pre", after)
    proj = use(h1, 'w_in', h1, "mm_proj")
    o_sb, a_sb = _sb_fwd(proj, h_sb)
    b_st = small['b_s'][l].T
    o_gm = _gm_fwd(proj, vec('g_vnorm'), small['w_s'][l], b_st, u_blk)
    memn = _norm_fwd(mem, vec('g_mem'), None, BF16, "norm_mem")
    mem_kv = use(memn, 'w_mem_kv', o_sb, "mm_mem_kv")
    o_xa = _xa_fwd(proj, mem_kv, q_blk, h_xa)
    zg = use(h1, 'w_gate', o_sb, "mm_gate", BF16)
    branches = [use(o, wn, zg, "mm_branch") for o, wn in ((o_sb, 'w_br_sb'), (o_gm, 'w_br_gm'), (o_xa, 'w_br_xa'))]
    merged = _merge_fwd(zg, vec('b_gate'), branches)
    y1 = use(merged, 'w_out', zg, "mm_out")
    x1 = _norm_fwd(y1, vec('g_mix_post'), x, F32, "norm_mix_post")
    h2 = _norm_fwd(x1, vec('g_ffn_pre'), None, BF16, "norm_ffn_pre")
    up = use(h2, 'w_up', h2, "mm_up", BF16)
    act, gel, ggrad = _cg_fwd(up, weight('conv_w', up), vec('conv_b'))
    y2 = use(act, 'w_down', act, "mm_down")
    x2 = _norm_fwd(y2, vec('g_ffn_post'), x1, F32, "norm_ffn_post")
    saved = dict(x0=x, h1=h1, proj=proj, o_sb=o_sb, a_sb=a_sb, o_gm=o_gm, o_xa=o_xa, memn=memn, mem_kv=mem_kv, zg=zg,
                 branches=branches, merged=merged, y1=y1, x1=x1, h2=h2, up=up, act=act, gel=gel, ggrad=ggrad, y2=y2, b_st=b_st)
    return x2, saved


def _layer_bwd(dx, mem, sv, full, small, l, after, emit):
    h_sb, h_xa, u_blk, q_blk = _dims(dx.shape[1])

    def vec(name):
        return small[name][l].reshape(1, -1)

    gb, gs = {}, {}
    dy2, gs['g_ffn_post'] = _norm_bwd(sv['y2'], vec('g_ffn_post'), [dx], None, BF16, "norm_ffn_post_bwd", after)
    gb['w_down'] = _mm(sv['act'], dy2, 'tn', BF16, "mm_down_dw")
    dact = _mm(dy2, full['w_down'], 'nt', F32, "mm_down_dx")
    dgate, dval, gs['conv_w'], gs['conv_b'] = _cg_bwd(sv['up'], sv['gel'], sv['ggrad'], full['conv_w'], dact)
    gb['w_up'] = _mm(sv['h2'], [dgate, dval], 'tn', BF16, "mm_up_dw")
    token = emit(0, gb)
    dh2 = _mm([dgate, dval], full['w_up'], 'nt', F32, "mm_up_dx", after=token)
    token = emit.flush(dh2)
    dx1, gs['g_ffn_pre'] = _norm_bwd(sv['x1'], vec('g_ffn_pre'), [dh2], dx, F32, "norm_ffn_pre_bwd", token)
    dy1, gs['g_mix_post'] = _norm_bwd(sv['y1'], vec('g_mix_post'), [dx1], None, BF16, "norm_mix_post_bwd")
    gb['w_out'] = _mm(sv['merged'], dy1, 'tn', BF16, "mm_out_dw")
    dmerged = _mm(dy1, full['w_out'], 'nt', F32, "mm_out_dx")
    dzg, dbr, gs['b_gate'] = _merge_bwd(sv['zg'], vec('b_gate'), sv['branches'], dmerged)
    douts = []
    for o, db, wn in ((sv['o_sb'], dbr[0], 'w_br_sb'), (sv['o_gm'], dbr[1], 'w_br_gm'), (sv['o_xa'], dbr[2], 'w_br_xa')):
        gb[wn] = _mm(o, db, 'tn', BF16, "mm_branch_dw")
        douts.append(_mm(db, full[wn], 'nt', F32, "mm_branch_dx"))
    gb['w_gate'] = _mm(sv['h1'], dzg, 'tn', BF16, "mm_gate_dw")
    dq_xa, dk_xa, dv_xa = _xa_bwd(sv['proj'], sv['mem_kv'], douts[2], q_blk, h_xa)
    dmem_kv = jnp.concatenate([dk_xa, dv_xa], axis=1).astype(BF16)
    gb['w_mem_kv'] = _mm(sv['memn'], dmem_kv, 'tn', BF16, "mm_mem_kv_dw")
    token = emit(1, gb)
    dh1_gate = _mm(dzg, full['w_gate'], 'nt', F32, "mm_gate_dx", after=token)
    token = emit.flush(dh1_gate)
    dmemn = _mm(dmem_kv, full['w_mem_kv'], 'nt', F32, "mm_mem_kv_dx")
    _, gs['g_mem'] = _norm_bwd(mem, vec('g_mem'), [dmemn], None, BF16, "norm_mem_bwd")
    du, dv, gs['g_vnorm'], gs['w_s'], db_st = _gm_bwd(sv['proj'], vec('g_vnorm'), small['w_s'][l], sv['b_st'], douts[1], u_blk)
    gs['b_s'] = db_st.T
    dq, dk, dvv = _sb_bwd(sv['proj'], sv['a_sb'], douts[0], h_sb, token)
    dproj = jnp.concatenate([dq, dk, dvv, du, dv, dq_xa], axis=1).astype(BF16)
    gb['w_in'] = _mm(sv['h1'], dproj, 'tn', BF16, "mm_proj_dw")
    token = emit(2, gb)
    dh1_proj = _mm(dproj, full['w_in'], 'nt', F32, "mm_proj_dx")
    dx0, gs['g_mix_pre'] = _norm_bwd(sv['x0'], vec('g_mix_pre'), [dh1_gate, dh1_proj], dx1, F32, "norm_mix_pre_bwd", token)
    return dx0, gs


class _Given:
    def __init__(self, full):
        self.full = full

    def __call__(self, name, follows):
        return self.full[name]

    def token(self):
        return None


def _local_step(x, mem, target, full, small):
    n_layers = len(full['w_in'])
    saved = []
    for l in range(n_layers):
        x, sv = _layer_fwd(x, mem, _Given({n: full[n][l] for n in full}), small, l)
        saved.append(sv)
    sq, dx = _loss_head(x, target)
    gbig = {n: [None] * n_layers for n in BIG}
    gsmall = {n: [None] * n_layers for n in SMALL + ['conv_w']}
    class Collect:
        def __init__(self, l):
            self.l = l

        def __call__(self, g, gb):
            for n in BWD_GROUPS[g]:
                gbig[n][self.l] = gb[n]

        def flush(self, follows):
            return None

    for l in reversed(range(n_layers)):
        dx, gs = _layer_bwd(dx, mem, saved[l], {n: full[n][l] for n in full}, small, l, None, Collect(l))
        for n in gs:
            gsmall[n][l] = gs[n]
    return sq, dx, gbig, gsmall


def _pack(arrays, rows_multiple):
    flat = jnp.concatenate([a.reshape(-1).astype(F32) for a in arrays])
    rows = -(-flat.shape[0] // LANES)
    rows = -(-rows // rows_multiple) * rows_multiple
    return jnp.pad(flat, (0, rows * LANES - flat.shape[0])).reshape(rows, LANES)


def _unpack(pack, like):
    flat = pack.reshape(-1)
    out, off = [], 0
    for a in like:
        out.append(flat[off:off + a.size].reshape(a.shape))
        off += a.size
    return out


def _grad_view(g, ax):
    r, c = g.shape
    return g.reshape(1, 2, r // 2, c) if ax == 2 else g.reshape(N_CHIPS, 2, r // (2 * N_CHIPS), c)


def kernel(x, mem, g_mix_pre, w_in, g_vnorm, w_s, b_s, g_mem, w_mem_kv, w_gate, b_gate, w_br_sb, w_br_gm, w_br_xa, w_out, g_mix_post, g_ffn_pre, w_up, conv_w, conv_b, w_down, g_ffn_post, loss_target, m_g_mix_pre, m_w_in, m_g_vnorm, m_w_s, m_b_s, m_g_mem, m_w_mem_kv, m_w_gate, m_b_gate, m_w_br_sb, m_w_br_gm, m_w_br_xa, m_w_out, m_g_mix_post, m_g_ffn_pre, m_w_up, m_conv_w, m_conv_b, m_w_down, m_g_ffn_post, v_g_mix_pre, v_w_in, v_g_vnorm, v_w_s, v_b_s, v_g_mem, v_w_mem_kv, v_w_gate, v_b_gate, v_w_br_sb, v_w_br_gm, v_w_br_xa, v_w_out, v_g_mix_post, v_g_ffn_pre, v_w_up, v_conv_w, v_conv_b, v_w_down, v_g_ffn_post):
    w = dict(g_mix_pre=g_mix_pre, w_in=w_in, g_vnorm=g_vnorm, w_s=w_s, b_s=b_s, g_mem=g_mem, w_mem_kv=w_mem_kv,
             w_gate=w_gate, b_gate=b_gate, w_br_sb=w_br_sb, w_br_gm=w_br_gm, w_br_xa=w_br_xa, w_out=w_out,
             g_mix_post=g_mix_post, g_ffn_pre=g_ffn_pre, w_up=w_up, conv_w=conv_w, conv_b=conv_b, w_down=w_down,
             g_ffn_post=g_ffn_post)
    m = dict(g_mix_pre=m_g_mix_pre, w_in=m_w_in, g_vnorm=m_g_vnorm, w_s=m_w_s, b_s=m_b_s, g_mem=m_g_mem,
             w_mem_kv=m_w_mem_kv, w_gate=m_w_gate, b_gate=m_b_gate, w_br_sb=m_w_br_sb, w_br_gm=m_w_br_gm,
             w_br_xa=m_w_br_xa, w_out=m_w_out, g_mix_post=m_g_mix_post, g_ffn_pre=m_g_ffn_pre, w_up=m_w_up,
             conv_w=m_conv_w, conv_b=m_conv_b, w_down=m_w_down, g_ffn_post=m_g_ffn_post)
    v = dict(g_mix_pre=v_g_mix_pre, w_in=v_w_in, g_vnorm=v_g_vnorm, w_s=v_w_s, b_s=v_b_s, g_mem=v_g_mem,
             w_mem_kv=v_w_mem_kv, w_gate=v_w_gate, b_gate=v_b_gate, w_br_sb=v_w_br_sb, w_br_gm=v_w_br_gm,
             w_br_xa=v_w_br_xa, w_out=v_w_out, g_mix_post=v_g_mix_post, g_ffn_pre=v_g_ffn_pre, w_up=v_w_up,
             conv_w=v_conv_w, conv_b=v_conv_b, w_down=v_w_down, g_ffn_post=v_g_ffn_post)
    n_layers = w_in.shape[0]
    d = x.shape[-1]
    core = lax.axis_index("c").astype(jnp.int32).reshape(1)
    chip = (2 * lax.axis_index("x") + lax.axis_index("y")).astype(jnp.int32).reshape(1)
    small = {n: w[n] for n in SMALL}
    xs, mems, target = x[0], mem[0], loss_target[0]

    conv_w_full = _gather_small(conv_w)

    def as_full(vw, ax):
        return vw.reshape(-1, vw.shape[-1]) if ax == 1 else vw.reshape(vw.shape[0] * vw.shape[1], vw.shape[2])

    stages = {}
    keys = [(l, g) for l in range(n_layers) for g in range(len(FWD_GROUPS))]

    def start_gathers(l, token):
        for g, names in enumerate(FWD_GROUPS):
            ax_g = [BIG_AXIS[n] for n in names]
            cols_g = [w[n].shape[2] for n in names]
            views = [_place_own(w[n], l, chip, ax, token) for n, ax in zip(names, ax_g)]
            sems, views, token = _exchange('start', f"gather_ici_start_{l}_{g}", views, _gather_ici_copies(ax_g, cols_g),
                                           3 * len(names), after=token)
            stages[l, g] = dict(names=names, ax=ax_g, cols=cols_g, views=views, ici=sems, d2d=None, full=None)
        return token

    token = start_gathers(0, conv_w_full)

    def cross_cores(key, follows):
        st, (l, g) = stages[key], key
        n3 = 3 * len(st['names'])
        views = _exchange('wait', f"gather_ici_wait_{l}_{g}", st['views'], _gather_ici_copies(st['ax'], st['cols']), n3,
                          after=follows, sems=st['ici'])
        st['d2d'], st['views'], tok = _exchange('start', f"gather_d2d_start_{l}_{g}", views,
                                                _gather_d2d_copies(st['ax'], st['cols']), n3, after=core)
        return tok

    class Weights:
        def __init__(self, l):
            self.l, self.tok = l, None

        def __call__(self, name, follows):
            if name == 'conv_w':
                return conv_w_full[self.l]
            key = (self.l, [g for g, names in enumerate(FWD_GROUPS) if name in names][0])
            if key[1] == 1 and (self.l + 1, 0) not in stages and self.l + 1 < n_layers:
                self.tok = start_gathers(self.l + 1, follows)
            st = stages[key]
            if st['full'] is None:
                if st['d2d'] is None:
                    cross_cores(key, follows)
                views = _exchange('wait', f"gather_d2d_wait_{key[0]}_{key[1]}", st['views'],
                                  _gather_d2d_copies(st['ax'], st['cols']), 3 * len(st['names']), after=follows, sems=st['d2d'])
                st['full'] = {n: as_full(vw, ax) for n, vw, ax in zip(st['names'], views, st['ax'])}
            nxt = keys.index(key) + 1
            if name == st['names'][-1] and nxt < len(keys) and key != keys[0] and stages[keys[nxt]]['d2d'] is None:
                self.tok = cross_cores(keys[nxt], follows)
            return st['full'][name]

        def token(self):
            return self.tok

    fulls, saved = [], []
    for l in range(n_layers):
        xs, sv = _layer_fwd(xs, mems, Weights(l), small, l, token if l == 0 else None)
        fulls.append({n: stages[l, g]['full'][n] for g, names in enumerate(FWD_GROUPS) for n in names} | {'conv_w': conv_w_full[l]})
        saved.append(sv)
    sq, dx = _loss_head(xs, target)
    loss = lax.psum(0.5 * jnp.sum(sq) / d, ("x", "y", "c"))

    sent = []

    def to_chips(l, g, names, ax_g, bufs, after):
        n = len(names)
        sums = [_sum_halves(dv, th, core, "sum_halves") for dv, th in zip(bufs[:n], bufs[n:])]
        lands = [lax.empty(_grads_recv_shape(sm, ax), sm.dtype) for sm, ax in zip(sums, ax_g)]
        sems, bufs, tok = _exchange('start', f"grads_ici_start_{l}_{g}", sums + lands, _grads_ici_copies(ax_g), 3 * n,
                                    after=core if after is None else after)
        sent.append((l, g, names, ax_g, bufs, sems))
        return tok

    class Grads:
        def __init__(self, l):
            self.l, self.crossing = l, None

        def __call__(self, g, gb):
            names = BWD_GROUPS[g]
            ax_g = [BIG_AXIS[n] for n in names]
            n = len(names)
            dwvs = [_grad_view(gb[nm], ax) for nm, ax in zip(names, ax_g)]
            lands = [lax.empty((dv.shape[0],) + dv.shape[2:], dv.dtype) for dv in dwvs]
            if g + 1 < len(BWD_GROUPS):
                sems, bufs, tok = _exchange('start', f"grads_d2d_start_{self.l}_{g}", dwvs + lands, _grads_d2d_copies(n), n,
                                            after=core)
                self.crossing = (g, names, ax_g, bufs, sems)
                return tok
            return to_chips(self.l, g, names, ax_g, _exchange('sync', "grads_d2d", dwvs + lands, _grads_d2d_copies(n), n), None)

        def flush(self, follows):
            if self.crossing is None:
                return None
            (g, names, ax_g, bufs, sems), self.crossing = self.crossing, None
            bufs = _exchange('wait', f"grads_d2d_wait_{self.l}_{g}", bufs, _grads_d2d_copies(len(names)), len(names),
                             after=follows, sems=sems)
            return to_chips(self.l, g, names, ax_g, bufs, None)

    gsmall = {n: [None] * n_layers for n in SMALL + ['conv_w']}
    for l in reversed(range(n_layers)):
        dx, gs = _layer_bwd(dx, mems, saved[l], fulls[l], small, l, None, Grads(l))
        for n in gs:
            gsmall[n][l] = gs[n]

    names_small = SMALL + ['conv_w']
    small_full = [jnp.stack(gsmall[n]).reshape(w[n].shape) for n in SMALL]
    conv_w_grad = jnp.stack(gsmall['conv_w'])
    pack = _pack(small_full + [conv_w_grad], 8)
    small_sems, small_bufs, _ = _exchange('start', "small_start", [pack, lax.empty((N_DEV,) + pack.shape, F32)], _small_copies,
                                          N_DEV - 1, after=dx)

    def small_update(follows):
        pk, land = _exchange('wait', "small_wait", small_bufs, _small_copies, N_DEV - 1, after=follows, sems=small_sems)
        device = (4 * lax.axis_index("x") + 2 * lax.axis_index("y") + lax.axis_index("c")).astype(jnp.int32).reshape(1)
        summed = _sum_devices(land, pk, device)
        *small_g, conv_w_g = _unpack(summed, small_full + [conv_w_grad])
        shard = conv_w.shape[-1]
        conv_w_g = lax.dynamic_slice_in_dim(conv_w_g, chip[0] * shard, shard, axis=2)
        packed = [_pack([p[n] for n in names_small], 256) for p in (w, m, v)]
        gpack = _pack(small_g + [conv_w_g], 256)
        res = _adamw(packed[0][None], packed[1][None], packed[2][None], gpack, 0, None, "adamw_small")
        like = [w[n] for n in names_small]
        unpacked = [_unpack(r[0], like) for r in res]
        return {n: tuple(u[i] for u in unpacked) for i, n in enumerate(names_small)}

    out = {}

    def update(joining, follows):
        l, g, names, halves, sems = joining
        halves = _exchange('wait', f"join_wait_{l}_{g}", halves, _join_copies, len(names), after=follows, sems=sems)
        last = None
        for n, hv in zip(names, halves):
            out[n] = _adamw(w[n], m[n], v[n], hv.reshape(w[n].shape[1:]), l, out.get(n), "adamw_big", last)
            last = out[n][0]
        return last

    joining, follows = None, dx
    for k, (l, g, names, ax_g, bufs, sems) in enumerate(sent):
        n = len(names)
        bufs = _exchange('wait', f"grads_ici_wait_{l}_{g}", bufs, _grads_ici_copies(ax_g), 3 * n, after=follows, sems=sems)
        halves = [_sum_chips(r, sm, chip, core, ax) for sm, r, ax in zip(bufs[:n], bufs[n:], ax_g)]
        jsems, halves, tok = _exchange('start', f"join_start_{l}_{g}", halves, _join_copies, n, after=core)
        if joining is not None:
            follows = update(joining, tok)
        joining = (l, g, names, halves, jsems)
    out.update(small_update(update(joining, follows)))

    return (loss, dx[None], *[out[n][0] for n in WEIGHTS], *[out[n][1] for n in WEIGHTS],
            *[out[n][2] for n in WEIGHTS], *[out[n][3] for n in WEIGHTS])
```

```python
import functools
import math

import jax
import jax.numpy as jnp
from jax import lax
from jax.experimental import pallas as pl
from jax.experimental.pallas import tpu as pltpu

F32 = jnp.float32
BF16 = jnp.bfloat16
EPS = 1e-6
HEAD_SB = 128
GROUP_GM = 128
CHUNK = 64
HEAD_XA = 256
CONV_TAPS = 3
N_CHIPS = 4
N_DEV = 8
LANES = 128
MIB = 1024 * 1024
VMEM_LIMIT = 48 * MIB
SPLITS = 1

ADAM_LR = 0.001
ADAM_B1 = 0.9
ADAM_B2 = 0.999
ADAM_EPS = 1e-08
ADAM_WD = 0.01
ADAM_STEP = 10

WEIGHTS = ['g_mix_pre', 'w_in', 'g_vnorm', 'w_s', 'b_s', 'g_mem', 'w_mem_kv', 'w_gate', 'b_gate', 'w_br_sb',
           'w_br_gm', 'w_br_xa', 'w_out', 'g_mix_post', 'g_ffn_pre', 'w_up', 'conv_w', 'conv_b', 'w_down',
           'g_ffn_post']
BIG_AXIS = {'w_in': 2, 'w_mem_kv': 1, 'w_gate': 2, 'w_br_sb': 2, 'w_br_gm': 2, 'w_br_xa': 2, 'w_out': 1,
            'w_up': 2, 'w_down': 1}
BIG = list(BIG_AXIS)
FWD_GROUPS = [['w_in'], ['w_mem_kv', 'w_gate'], ['w_br_sb', 'w_br_gm', 'w_br_xa', 'w_out'], ['w_up'], ['w_down']]
BWD_GROUPS = [['w_down', 'w_up'], ['w_out', 'w_br_sb', 'w_br_gm', 'w_br_xa', 'w_gate', 'w_mem_kv'], ['w_in']]
SMALL = ['g_mix_pre', 'g_vnorm', 'w_s', 'b_s', 'g_mem', 'b_gate', 'g_mix_post', 'g_ffn_pre', 'conv_b', 'g_ffn_post']
MESH = pl.DeviceIdType.MESH


def _pcall(body, **kw):
    return pl.pallas_call(body, **kw)


def _params(sem=None, vmem=VMEM_LIMIT):
    return pltpu.CompilerParams(dimension_semantics=sem, vmem_limit_bytes=vmem)


def _tile(n, cands):
    for c in cands:
        if n % c == 0:
            return c
    return n


_GELU_C = math.sqrt(2.0 / math.pi)
_GELU_A = 0.044715


def _gelu(x):
    return 0.5 * x * (1.0 + jnp.tanh(_GELU_C * (x + _GELU_A * (x * x * x))))


def _gelu_and_grad(x):
    x2 = x * x
    t = jnp.tanh(_GELU_C * (x + _GELU_A * (x2 * x)))
    val = 0.5 * x * (1.0 + t)
    grad = 0.5 * (1.0 + t) + 0.5 * x * (1.0 - t * t) * (_GELU_C * (1.0 + 3.0 * _GELU_A * x2))
    return val, grad


def _softplus(z):
    return jnp.maximum(z, 0.0) + jnp.log(1.0 + jnp.exp(-jnp.abs(z)))


def _dot(a, b):
    return jnp.dot(a, b, preferred_element_type=F32)


def _dot_nt(a, b):
    return lax.dot_general(a, b, (((1,), (1,)), ((), ())), preferred_element_type=F32)


def _dot_tn(a, b):
    return lax.dot_general(a, b, (((0,), (0,)), ((), ())), preferred_element_type=F32)


def _split_dot(a, m):
    out = None
    rest = a
    for _ in range(SPLITS):
        piece = rest.astype(BF16)
        rest = rest - piece.astype(F32)
        term = _dot(piece, m)
        out = term if out is None else out + term
    return out


def _mm(a, b, mode, out_dtype, name, tm=None, tn=None, tk=None, after=None):
    a_parts = list(a) if isinstance(a, (list, tuple)) else [a]
    b_parts = list(b) if isinstance(b, (list, tuple)) else [b]
    assert len(a_parts) == 1 or mode == 'nt'
    assert len(b_parts) == 1 or mode == 'tn'
    na, nb = len(a_parts), len(b_parts)
    if mode == 'nn':
        (m, kc), (kc2, n) = a_parts[0].shape, b_parts[0].shape
    elif mode == 'nt':
        (m, kp), (n, kc2) = a_parts[0].shape, b_parts[0].shape
        kc = kp * na
    else:
        (kc, m), (kc2, npiece) = a_parts[0].shape, b_parts[0].shape
        n = npiece * nb
    assert kc == kc2, (a_parts[0].shape, b_parts[0].shape, mode)
    tm = tm or _tile(m, (1024, 512, 256, 128))
    tn = tn or _tile(n // nb, (1024, 512, 256, 128))
    k_max = 3072 // na
    tk = tk or ((kc // na) if kc // na <= k_max else _tile(kc // na, [c for c in (3072, 2816, 2048, 1536, 1408, 1024, 512) if c <= k_max]))
    nk = kc // tk
    k_per = nk // na
    n_per = (n // tn) // nb
    dot = {'nn': _dot, 'nt': _dot_nt, 'tn': _dot_tn}[mode]

    def within(idx, p, per):
        return jnp.clip(idx - p * per, 0, per - 1)

    if mode == 'tn':
        a_specs = [pl.BlockSpec((tk, tm), lambda i, j, k: (k, i))]
        b_specs = [pl.BlockSpec((tk, tn), functools.partial(lambda i, j, k, p: (k, within(j, p, n_per)), p=p)) for p in range(nb)]
    else:
        a_specs = [pl.BlockSpec((tm, tk), functools.partial(lambda i, j, k, p: (i, within(k, p, k_per)), p=p)) for p in range(na)]
        b_specs = [pl.BlockSpec((tn, tk), lambda i, j, k: (j, k)) if mode == 'nt' else pl.BlockSpec((tk, tn), lambda i, j, k: (k, j))]

    extra = [] if after is None else [after]
    extra_specs = [pl.BlockSpec(memory_space=pl.ANY)] * len(extra)

    def product(a_refs, b_refs, store):
        if na == 1 and nb == 1:
            store(dot(a_refs[0][...].astype(BF16), b_refs[0][...].astype(BF16)))
            return
        which = (pl.program_id(2) // k_per) if na > 1 else (pl.program_id(1) // n_per)
        for p in range(max(na, nb)):
            @pl.when(which == p)
            def _(p=p):
                store(dot(a_refs[p if na > 1 else 0][...].astype(BF16), b_refs[p if nb > 1 else 0][...].astype(BF16)))

    if nk == 1:
        def body(*refs):
            o_ref = refs[-1]

            def store(part):
                o_ref[...] = part.astype(o_ref.dtype)

            product(refs[:na], refs[na:na + nb], store)
        scratch = []
    else:
        def body(*refs):
            o_ref, acc_ref = refs[-2], refs[-1]
            k = pl.program_id(2)

            def store(part):
                @pl.when(k == 0)
                def _():
                    acc_ref[...] = part

                @pl.when(k > 0)
                def _():
                    acc_ref[...] += part

            product(refs[:na], refs[na:na + nb], store)

            @pl.when(k == nk - 1)
            def _():
                o_ref[...] = acc_ref[...].astype(o_ref.dtype)
        scratch = [pltpu.VMEM((tm, tn), F32)]

    return _pcall(
        body, grid=(m // tm, n // tn, nk), in_specs=a_specs + b_specs + extra_specs,
        out_specs=pl.BlockSpec((tm, tn), lambda i, j, k: (i, j)),
        out_shape=jax.ShapeDtypeStruct((m, n), out_dtype), scratch_shapes=scratch, name=name,
        compiler_params=_params(("parallel", "parallel", "arbitrary")))(*a_parts, *b_parts, *extra)


def _norm_fwd(x, g, res, out_dtype, name, after=None):
    s, d = x.shape
    tr = _tile(s, (256, 128))
    has_res = res is not None
    has_after = after is not None

    def body(*refs):
        x_ref, g_ref = refs[0], refs[1]
        o_ref = refs[-1]
        xv = x_ref[...]
        y = xv * lax.rsqrt(jnp.mean(xv * xv, axis=-1, keepdims=True) + EPS) * g_ref[...]
        if has_res:
            y = y + refs[2][...]
        o_ref[...] = y.astype(o_ref.dtype)

    row = pl.BlockSpec((tr, d), lambda i: (i, 0))
    ins = [x, g] + ([res] if has_res else []) + ([after] if has_after else [])
    return _pcall(
        body, grid=(s // tr,),
        in_specs=[row, pl.BlockSpec((1, d), lambda i: (0, 0))] + ([row] if has_res else [])
        + ([pl.BlockSpec(memory_space=pl.ANY)] if has_after else []),
        out_specs=row, out_shape=jax.ShapeDtypeStruct((s, d), out_dtype), name=name,
        compiler_params=_params(("parallel",)))(*ins)


def _norm_bwd(x, g, douts, dres, out_dtype, name, after=None):
    s, d = x.shape
    tr = _tile(s, (256, 128))
    nd = len(douts)
    has_res = dres is not None
    has_after = after is not None

    def body(*refs):
        x_ref, g_ref = refs[0], refs[1]
        dx_ref, dg_ref = refs[-2], refs[-1]
        dout = refs[2][...].astype(F32)
        for r in refs[3:2 + nd]:
            dout = dout + r[...].astype(F32)
        xv = x_ref[...]
        r = lax.rsqrt(jnp.mean(xv * xv, axis=-1, keepdims=True) + EPS)
        n = xv * r
        dn = dout * g_ref[...]
        dx = r * (dn - n * jnp.mean(dn * n, axis=-1, keepdims=True))
        if has_res:
            dx = dx + refs[2 + nd][...]
        dx_ref[...] = dx.astype(dx_ref.dtype)

        @pl.when(pl.program_id(0) == 0)
        def _():
            dg_ref[...] = jnp.zeros_like(dg_ref)

        dg_ref[...] += jnp.sum(dout * n, axis=0, keepdims=True)

    row = pl.BlockSpec((tr, d), lambda i: (i, 0))
    vec = pl.BlockSpec((1, d), lambda i: (0, 0))
    ins = [x, g] + list(douts) + ([dres] if has_res else []) + ([after] if has_after else [])
    return _pcall(
        body, grid=(s // tr,),
        in_specs=[row, vec] + [row] * (nd + int(has_res)) + ([pl.BlockSpec(memory_space=pl.ANY)] if has_after else []),
        out_specs=[row, vec],
        out_shape=[jax.ShapeDtypeStruct((s, d), out_dtype), jax.ShapeDtypeStruct((1, d), F32)], name=name,
        compiler_params=_params(("arbitrary",)))(*ins)


def _loss_head(y, target):
    s, d = y.shape
    tr = _tile(s, (256, 128))

    def body(y_ref, t_ref, sq_ref, dy_ref):
        e = y_ref[...] - t_ref[...]
        dy_ref[...] = e * (1.0 / d)

        @pl.when(pl.program_id(0) == 0)
        def _():
            sq_ref[...] = jnp.zeros_like(sq_ref)

        sq_ref[...] += jnp.sum(e * e, axis=0, keepdims=True)

    row = pl.BlockSpec((tr, d), lambda i: (i, 0))
    return _pcall(
        body, grid=(s // tr,), in_specs=[row, row], out_specs=[pl.BlockSpec((1, d), lambda i: (0, 0)), row],
        out_shape=[jax.ShapeDtypeStruct((1, d), F32), jax.ShapeDtypeStruct((s, d), F32)], name="loss_head",
        compiler_params=_params(("arbitrary",)))(y, target)


NEVER = -1e30
SB_QUERIES = 512


def _sb_sum_matrix(later):
    r = lax.broadcasted_iota(jnp.int32, (HEAD_SB, 2 * HEAD_SB), 0)
    c = lax.broadcasted_iota(jnp.int32, (HEAD_SB, 2 * HEAD_SB), 1)
    tri = jnp.where((r > c) if later else (r < c), 1.0, 0.0)
    return jnp.where(c < HEAD_SB, tri, 1.0).astype(BF16)


def _sb_mask(tq, q0, k0):
    row = lax.broadcasted_iota(jnp.int32, (tq, HEAD_SB), 0)
    col = lax.broadcasted_iota(jnp.int32, (tq, HEAD_SB), 1)
    return (k0 + col) < (q0 + row)


def _sb_fwd(proj, n_heads):
    s = proj.shape[0]
    tq = min(SB_QUERIES, s)
    per = tq // HEAD_SB
    scale = HEAD_SB ** -0.5

    def body(q_ref, k_ref, v_ref, o_ref, a_ref, b_ref, acc_ref, c_ref):
        i = pl.program_id(1)
        q = q_ref[...].astype(BF16)
        sums = _sb_sum_matrix(True)
        acc_ref[...] = jnp.zeros_like(acc_ref)
        c_ref[...] = jnp.zeros_like(c_ref)
        last = (i + 1) * per - 1

        def scores(j, masked):
            off = pl.multiple_of(j * HEAD_SB, HEAD_SB)
            z = _dot_nt(q, k_ref[pl.ds(off, HEAD_SB), :].astype(BF16)) * scale
            sp = _softplus(z)
            logb = z - sp
            if masked:
                mask = _sb_mask(tq, i * tq, off)
                logb = jnp.where(mask, logb, NEVER)
                sp = jnp.where(mask, sp, 0.0)
            return logb, _split_dot(sp, sums)

        def values(j, logb, both):
            off = pl.multiple_of(j * HEAD_SB, HEAD_SB)
            c = c_ref[...]
            a = jnp.exp(logb - both[:, :HEAD_SB] - c).astype(BF16)
            a_ref[0, 0, j] = a
            b_ref[0, 0, j] = jnp.exp(logb).astype(BF16)
            acc_ref[...] += _dot(a, v_ref[pl.ds(off, HEAD_SB), :].astype(BF16))
            c_ref[...] = c + both[:, HEAD_SB:]

        def step(jj, carry, masked):
            j = last - jj
            nxt = scores(j, masked)
            values(jnp.minimum(j + 1, last), *carry)
            return nxt

        idle = (jnp.full((tq, HEAD_SB), NEVER, F32), jnp.zeros((tq, 2 * HEAD_SB), F32))
        carry = idle
        for jj in range(per):
            carry = step(jj, carry, True)
        def group(jg, carry):
            for u in range(per):
                carry = step(per * jg + u, carry, False)
            return carry

        carry = lax.fori_loop(1, i + 1, group, carry)
        values(0, *carry)
        o_ref[...] = acc_ref[...].astype(o_ref.dtype)

    h = n_heads
    blk = pl.BlockSpec((tq, HEAD_SB), lambda hh, i: (i, hh))
    return _pcall(
        body, grid=(h, s // tq),
        in_specs=[blk, pl.BlockSpec((s, HEAD_SB), lambda hh, i: (0, h + hh)),
                  pl.BlockSpec((s, HEAD_SB), lambda hh, i: (0, 2 * h + hh))],
        out_specs=[blk] + [pl.BlockSpec((1, 1, s // HEAD_SB, tq, HEAD_SB), lambda hh, i: (hh, i, 0, 0, 0))] * 2,
        out_shape=[jax.ShapeDtypeStruct((s, h * HEAD_SB), BF16)]
        + [jax.ShapeDtypeStruct((h, s // tq, s // HEAD_SB, tq, HEAD_SB), BF16)] * 2,
        scratch_shapes=[pltpu.VMEM((tq, HEAD_SB), F32), pltpu.VMEM((tq, HEAD_SB), F32)],
        name="sb_fwd", compiler_params=_params(("parallel", "arbitrary")))(proj, proj, proj)


def _sb_bwd(proj, a_saved, b_saved, do, n_heads, after=None):
    s = proj.shape[0]
    tq = min(SB_QUERIES, s)
    per = tq // HEAD_SB
    scale = HEAD_SB ** -0.5
    follow = [] if after is None else [after]

    def body(q_ref, k_ref, v_ref, do_ref, a_ref, b_ref, *rest):
        dq_ref, dk_ref, dv_ref, run_ref, acc_ref = rest[len(follow):]
        i = pl.program_id(1)

        @pl.when(i == 0)
        def _():
            dk_ref[...] = jnp.zeros_like(dk_ref)
            dv_ref[...] = jnp.zeros_like(dv_ref)

        q = q_ref[...].astype(BF16)
        dob = do_ref[...].astype(BF16)
        run_ref[...] = jnp.zeros_like(run_ref)
        acc_ref[...] = jnp.zeros_like(acc_ref)
        earlier = _sb_sum_matrix(False)
        first_diagonal = i * per

        def step(j, masked):
            off = pl.multiple_of(j * HEAD_SB, HEAD_SB)
            kb = k_ref[pl.ds(off, HEAD_SB), :].astype(BF16)
            vb = v_ref[pl.ds(off, HEAD_SB), :].astype(BF16)
            a = a_ref[0, 0, j]
            g = a.astype(F32) * _dot_nt(dob, vb)
            dv_ref[pl.ds(off, HEAD_SB), :] += _dot_tn(a, dob)
            beta = b_ref[0, 0, j].astype(F32)
            both = _split_dot(g, earlier)
            p = run_ref[...]
            dzb = ((g * (1.0 - beta) - beta * (both[:, :HEAD_SB] + p)) * scale).astype(BF16)
            dk_ref[pl.ds(off, HEAD_SB), :] += _dot_tn(dzb, q)
            acc_ref[...] += _dot(dzb, kb)
            run_ref[...] = p + both[:, HEAD_SB:]

        def group(jg, carry):
            for u in range(per):
                step(per * jg + u, False)
            return carry

        lax.fori_loop(0, i, group, 0)
        for u in range(per):
            step(first_diagonal + u, True)
        dq_ref[...] = acc_ref[...]

    h = n_heads
    blk = pl.BlockSpec((tq, HEAD_SB), lambda hh, i: (i, hh))
    col_blk = pl.BlockSpec((s, HEAD_SB), lambda hh, i: (0, hh))
    shape = jax.ShapeDtypeStruct((s, h * HEAD_SB), F32)
    return _pcall(
        body, grid=(h, s // tq),
        in_specs=[blk, pl.BlockSpec((s, HEAD_SB), lambda hh, i: (0, h + hh)),
                  pl.BlockSpec((s, HEAD_SB), lambda hh, i: (0, 2 * h + hh)), blk]
        + [pl.BlockSpec((1, 1, s // HEAD_SB, tq, HEAD_SB), lambda hh, i: (hh, i, 0, 0, 0))] * 2
        + [pl.BlockSpec(memory_space=pl.ANY)] * len(follow),
        out_specs=[blk, col_blk, col_blk], out_shape=[shape, shape, shape],
        scratch_shapes=[pltpu.VMEM((tq, HEAD_SB), F32), pltpu.VMEM((tq, HEAD_SB), F32)],
        name="sb_bwd", compiler_params=_params(("parallel", "arbitrary")))(proj, proj, proj, do, a_saved, b_saved, *follow)


def _gm_mask():
    t = lax.broadcasted_iota(jnp.int32, (GROUP_GM, GROUP_GM), 0)
    s = lax.broadcasted_iota(jnp.int32, (GROUP_GM, GROUP_GM), 1)
    shift = CHUNK.bit_length() - 1
    return (s >> shift) <= (t >> shift)


def _gm_fwd(proj, g_vnorm, w_s, b_st, u_blk):
    s = proj.shape[0]
    groups = w_s.shape[0]
    w = groups * GROUP_GM

    def body(u_ref, v_ref, gv_ref, ws_ref, bst_ref, o_ref):
        ug = _gelu(u_ref[...])
        vg = _gelu(v_ref[...])
        vn = vg * lax.rsqrt(jnp.mean(vg * vg, axis=-1, keepdims=True) + EPS) * gv_ref[...]
        vnb = vn.astype(BF16)
        mask = _gm_mask()
        for g in range(groups):
            sl = slice(g * GROUP_GM, (g + 1) * GROUP_GM)
            wm = jnp.where(mask, ws_ref[g], 0.0).astype(BF16)
            mixed = _dot(wm, vnb[:, sl]) + bst_ref[:, g:g + 1]
            o_ref[:, sl] = (ug[:, sl] * mixed).astype(o_ref.dtype)

    return _pcall(
        body, grid=(s // GROUP_GM,),
        in_specs=[pl.BlockSpec((GROUP_GM, w), lambda c: (c, u_blk)), pl.BlockSpec((GROUP_GM, w), lambda c: (c, u_blk + 1)),
                  pl.BlockSpec((1, w), lambda c: (0, 0)), pl.BlockSpec((groups, GROUP_GM, GROUP_GM), lambda c: (0, 0, 0)),
                  pl.BlockSpec((GROUP_GM, groups), lambda c: (0, 0))],
        out_specs=pl.BlockSpec((GROUP_GM, w), lambda c: (c, 0)),
        out_shape=jax.ShapeDtypeStruct((s, w), BF16), name="gm_fwd",
        compiler_params=_params(("parallel",)))(proj, proj, g_vnorm, w_s, b_st)


def _gm_bwd(proj, g_vnorm, w_s, b_st, do, u_blk):
    s = proj.shape[0]
    groups = w_s.shape[0]
    w = groups * GROUP_GM

    def body(u_ref, v_ref, gv_ref, ws_ref, bst_ref, do_ref, du_ref, dv_ref, dgv_ref, dws_ref, dbst_ref, dvn_ref):
        @pl.when(pl.program_id(0) == 0)
        def _():
            dgv_ref[...] = jnp.zeros_like(dgv_ref)
            dws_ref[...] = jnp.zeros_like(dws_ref)
            dbst_ref[...] = jnp.zeros_like(dbst_ref)

        ug, ugrad = _gelu_and_grad(u_ref[...])
        vg, vgrad = _gelu_and_grad(v_ref[...])
        r = lax.rsqrt(jnp.mean(vg * vg, axis=-1, keepdims=True) + EPS)
        n = vg * r
        gv = gv_ref[...]
        vnb = (n * gv).astype(BF16)
        dout = do_ref[...]
        mask = _gm_mask()
        for g in range(groups):
            sl = slice(g * GROUP_GM, (g + 1) * GROUP_GM)
            wm = jnp.where(mask, ws_ref[g], 0.0).astype(BF16)
            mixed = _dot(wm, vnb[:, sl]) + bst_ref[:, g:g + 1]
            dmixed = dout[:, sl] * ug[:, sl]
            du_ref[:, sl] = dout[:, sl] * mixed * ugrad[:, sl]
            dbst_ref[:, g:g + 1] += jnp.sum(dmixed, axis=1, keepdims=True)
            dmb = dmixed.astype(BF16)
            dws_ref[g] += jnp.where(mask, _dot_nt(dmb, vnb[:, sl]), 0.0)
            dvn_ref[:, sl] = _dot_tn(wm, dmb)
        dvn = dvn_ref[...]
        dgv_ref[...] += jnp.sum(dvn * n, axis=0, keepdims=True)
        dn = dvn * gv
        dvg = r * (dn - n * jnp.mean(dn * n, axis=-1, keepdims=True))
        dv_ref[...] = dvg * vgrad

    rowb = pl.BlockSpec((GROUP_GM, w), lambda c: (c, 0))
    vec = pl.BlockSpec((1, w), lambda c: (0, 0))
    wsb = pl.BlockSpec((groups, GROUP_GM, GROUP_GM), lambda c: (0, 0, 0))
    bsb = pl.BlockSpec((GROUP_GM, groups), lambda c: (0, 0))
    return _pcall(
        body, grid=(s // GROUP_GM,),
        in_specs=[pl.BlockSpec((GROUP_GM, w), lambda c: (c, u_blk)), pl.BlockSpec((GROUP_GM, w), lambda c: (c, u_blk + 1)),
                  vec, wsb, bsb, rowb],
        out_specs=[rowb, rowb, vec, wsb, bsb],
        out_shape=[jax.ShapeDtypeStruct((s, w), F32), jax.ShapeDtypeStruct((s, w), F32), jax.ShapeDtypeStruct((1, w), F32),
                   jax.ShapeDtypeStruct((groups, GROUP_GM, GROUP_GM), F32), jax.ShapeDtypeStruct((GROUP_GM, groups), F32)],
        scratch_shapes=[pltpu.VMEM((GROUP_GM, w), F32)], name="gm_bwd",
        compiler_params=_params(("arbitrary",)))(proj, proj, g_vnorm, w_s, b_st, do)


def _xa_fwd(proj, mem_kv, q_blk, n_heads):
    s = proj.shape[0]
    nm = mem_kv.shape[0]
    tq = _tile(s, (512, 256, 128))
    scale = HEAD_XA ** -0.5

    def body(q_ref, k_ref, v_ref, o_ref):
        z = _dot_nt(q_ref[...].astype(BF16), k_ref[...].astype(BF16)) * scale
        z = z - jnp.max(z, axis=-1, keepdims=True)
        e = jnp.exp(z)
        p = e / jnp.sum(e, axis=-1, keepdims=True)
        o_ref[...] = _dot(p.astype(BF16), v_ref[...].astype(BF16)).astype(o_ref.dtype)

    h = n_heads
    return _pcall(
        body, grid=(h, s // tq),
        in_specs=[pl.BlockSpec((tq, HEAD_XA), lambda hh, i: (i, q_blk + hh)),
                  pl.BlockSpec((nm, HEAD_XA), lambda hh, i: (0, hh)), pl.BlockSpec((nm, HEAD_XA), lambda hh, i: (0, h + hh))],
        out_specs=pl.BlockSpec((tq, HEAD_XA), lambda hh, i: (i, hh)),
        out_shape=jax.ShapeDtypeStruct((s, h * HEAD_XA), BF16), name="xa_fwd",
        compiler_params=_params(("parallel", "parallel")))(proj, mem_kv, mem_kv)


def _xa_bwd(proj, mem_kv, do, q_blk, n_heads):
    s = proj.shape[0]
    nm = mem_kv.shape[0]
    tq = _tile(s, (512, 256, 128))
    scale = HEAD_XA ** -0.5
    h = n_heads

    def body(q_ref, k_ref, v_ref, do_ref, dq_ref, dk_ref, dv_ref):
        @pl.when(pl.program_id(1) == 0)
        def _():
            dk_ref[...] = jnp.zeros_like(dk_ref)
            dv_ref[...] = jnp.zeros_like(dv_ref)

        qb = q_ref[...].astype(BF16)
        kb = k_ref[...].astype(BF16)
        vb = v_ref[...].astype(BF16)
        dob = do_ref[...].astype(BF16)
        z = _dot_nt(qb, kb) * scale
        z = z - jnp.max(z, axis=-1, keepdims=True)
        e = jnp.exp(z)
        p = e / jnp.sum(e, axis=-1, keepdims=True)
        dp = _dot_nt(dob, vb)
        dz = (p * (dp - jnp.sum(dp * p, axis=-1, keepdims=True)) * scale).astype(BF16)
        dq_ref[...] = _dot(dz, kb)
        dk_ref[...] += _dot_tn(dz, qb)
        dv_ref[...] += _dot_tn(p.astype(BF16), dob)

    qspec = pl.BlockSpec((tq, HEAD_XA), lambda hh, i: (i, hh))
    dk, dv = None, None
    dq, dk, dv = _pcall(
        body, grid=(h, s // tq),
        in_specs=[pl.BlockSpec((tq, HEAD_XA), lambda hh, i: (i, q_blk + hh)),
                  pl.BlockSpec((nm, HEAD_XA), lambda hh, i: (0, hh)), pl.BlockSpec((nm, HEAD_XA), lambda hh, i: (0, h + hh)),
                  qspec],
        out_specs=[qspec, pl.BlockSpec((nm, HEAD_XA), lambda hh, i: (0, hh)), pl.BlockSpec((nm, HEAD_XA), lambda hh, i: (0, hh))],
        out_shape=[jax.ShapeDtypeStruct((s, h * HEAD_XA), F32), jax.ShapeDtypeStruct((nm, h * HEAD_XA), F32),
                   jax.ShapeDtypeStruct((nm, h * HEAD_XA), F32)],
        name="xa_bwd", compiler_params=_params(("parallel", "arbitrary")))(proj, mem_kv, mem_kv, do)
    return dq, dk, dv


def _merge_fwd(zg, b_gate, branches):
    s, d = branches[0].shape
    tr = _tile(s, (128,))

    def body(z0, z1, z2, g0, g1, g2, b0, b1, b2, o_ref):
        acc = None
        for z, g, b in ((z0, g0, b0), (z1, g1, b1), (z2, g2, b2)):
            term = jax.nn.sigmoid(z[...].astype(F32) + g[...]) * b[...]
            acc = term if acc is None else acc + term
        o_ref[...] = acc.astype(o_ref.dtype)

    zs = [pl.BlockSpec((tr, d), functools.partial(lambda i, k: (i, k), k=k)) for k in range(3)]
    gs = [pl.BlockSpec((1, d), functools.partial(lambda i, k: (0, k), k=k)) for k in range(3)]
    row = pl.BlockSpec((tr, d), lambda i: (i, 0))
    return _pcall(
        body, grid=(s // tr,), in_specs=zs + gs + [row] * 3, out_specs=row,
        out_shape=jax.ShapeDtypeStruct((s, d), BF16), name="merge_fwd",
        compiler_params=_params(("parallel",)))(zg, zg, zg, b_gate, b_gate, b_gate, *branches)


def _merge_bwd(zg, b_gate, branches, dmerged):
    s, d = branches[0].shape
    tr = _tile(s, (128,))

    def body(z0, z1, z2, g0, g1, g2, b0, b1, b2, dm_ref, dz_ref, d0, d1, d2, dbg_ref):
        @pl.when(pl.program_id(0) == 0)
        def _():
            dbg_ref[...] = jnp.zeros_like(dbg_ref)

        dm = dm_ref[...]
        for k, (z, g, b, dbr) in enumerate(((z0, g0, b0, d0), (z1, g1, b1, d1), (z2, g2, b2, d2))):
            sg = jax.nn.sigmoid(z[...].astype(F32) + g[...])
            dbr[...] = (dm * sg).astype(dbr.dtype)
            dz = dm * b[...] * sg * (1.0 - sg)
            dz_ref[:, k * d:(k + 1) * d] = dz.astype(dz_ref.dtype)
            dbg_ref[:, k * d:(k + 1) * d] += jnp.sum(dz, axis=0, keepdims=True)

    zs = [pl.BlockSpec((tr, d), functools.partial(lambda i, k: (i, k), k=k)) for k in range(3)]
    gs = [pl.BlockSpec((1, d), functools.partial(lambda i, k: (0, k), k=k)) for k in range(3)]
    row = pl.BlockSpec((tr, d), lambda i: (i, 0))
    outs = _pcall(
        body, grid=(s // tr,), in_specs=zs + gs + [row] * 4,
        out_specs=[pl.BlockSpec((tr, 3 * d), lambda i: (i, 0)), row, row, row, pl.BlockSpec((1, 3 * d), lambda i: (0, 0))],
        out_shape=[jax.ShapeDtypeStruct((s, 3 * d), BF16)] + [jax.ShapeDtypeStruct((s, d), BF16)] * 3
        + [jax.ShapeDtypeStruct((1, 3 * d), F32)],
        name="merge_bwd", compiler_params=_params(("arbitrary",)))(zg, zg, zg, b_gate, b_gate, b_gate, *branches, dmerged)
    return outs[0], list(outs[1:4]), outs[4]


def _shift_down(x, k, row):
    return jnp.where(row >= k, pltpu.roll(x, k, 0), 0.0)


def _shift_up(x, k, row, s):
    return jnp.where(row < s - k, pltpu.roll(x, s - k, 0), 0.0)


def _conv_pre(gate, cw_ref, cb_ref, row):
    conv = cb_ref[...] + cw_ref[CONV_TAPS - 1:CONV_TAPS, :] * gate
    for k in range(1, CONV_TAPS):
        conv = conv + cw_ref[CONV_TAPS - 1 - k:CONV_TAPS - k, :] * _shift_down(gate, k, row)
    return conv


def _cg_fwd(up, conv_w, conv_b):
    s = up.shape[0]
    f = conv_w.shape[1]
    tc = _tile(f, (256, 128))
    nb = f // tc

    def body(g_ref, v_ref, cw_ref, cb_ref, o_ref, gel_ref, ggrad_ref):
        row = lax.broadcasted_iota(jnp.int32, (s, tc), 0)
        conv = _conv_pre(g_ref[...].astype(F32), cw_ref, cb_ref, row)
        gel, ggrad = _gelu_and_grad(conv)
        o_ref[...] = (gel * v_ref[...].astype(F32)).astype(o_ref.dtype)
        gel_ref[...] = gel.astype(gel_ref.dtype)
        ggrad_ref[...] = ggrad.astype(ggrad_ref.dtype)

    colb = pl.BlockSpec((s, tc), lambda j: (0, j))
    shape = jax.ShapeDtypeStruct((s, f), BF16)
    return _pcall(
        body, grid=(nb,),
        in_specs=[colb, pl.BlockSpec((s, tc), lambda j: (0, nb + j)),
                  pl.BlockSpec((CONV_TAPS, tc), lambda j: (0, j)), pl.BlockSpec((1, tc), lambda j: (0, j))],
        out_specs=[colb, colb, colb], out_shape=[shape, shape, shape], name="cg_fwd",
        compiler_params=_params(("parallel",)))(up, up, conv_w, conv_b)


def _cg_bwd(up, gel, ggrad, conv_w, dact):
    s = up.shape[0]
    f = conv_w.shape[1]
    tc = _tile(f, (256, 128))
    nb = f // tc

    def body(g_ref, v_ref, gel_ref, gg_ref, cw_ref, da_ref, dg_ref, dv_ref, dcw_ref, dcb_ref):
        row = lax.broadcasted_iota(jnp.int32, (s, tc), 0)
        gate = g_ref[...].astype(F32)
        da = da_ref[...]
        dv_ref[...] = (da * gel_ref[...].astype(F32)).astype(dv_ref.dtype)
        dconv = da * v_ref[...].astype(F32) * gg_ref[...].astype(F32)
        dgate = cw_ref[CONV_TAPS - 1:CONV_TAPS, :] * dconv
        dcw_ref[CONV_TAPS - 1:CONV_TAPS, :] = jnp.sum(dconv * gate, axis=0, keepdims=True)
        for k in range(1, CONV_TAPS):
            dgate = dgate + cw_ref[CONV_TAPS - 1 - k:CONV_TAPS - k, :] * _shift_up(dconv, k, row, s)
            dcw_ref[CONV_TAPS - 1 - k:CONV_TAPS - k, :] = jnp.sum(dconv * _shift_down(gate, k, row), axis=0, keepdims=True)
        dg_ref[...] = dgate.astype(dg_ref.dtype)
        dcb_ref[...] = jnp.sum(dconv, axis=0, keepdims=True)

    colb = pl.BlockSpec((s, tc), lambda j: (0, j))
    return _pcall(
        body, grid=(nb,),
        in_specs=[colb, pl.BlockSpec((s, tc), lambda j: (0, nb + j)), colb, colb,
                  pl.BlockSpec((CONV_TAPS, tc), lambda j: (0, j)), colb],
        out_specs=[colb, colb, pl.BlockSpec((CONV_TAPS, tc), lambda j: (0, j)), pl.BlockSpec((1, tc), lambda j: (0, j))],
        out_shape=[jax.ShapeDtypeStruct((s, f), BF16), jax.ShapeDtypeStruct((s, f), BF16),
                   jax.ShapeDtypeStruct((CONV_TAPS, f), F32), jax.ShapeDtypeStruct((1, f), F32)],
        name="cg_bwd", compiler_params=_params(("parallel",)))(up, up, gel, ggrad, conv_w, dact)


def _row_tile(rows, cols, elems=256 * 1024):
    want = max(16, elems // cols)
    for c in (512, 256, 128, 64, 32, 16):
        if c <= want and rows % c == 0:
            return c
    return rows


def _sum_halves(dwv, recv, core, name):
    nj, _, a, c = dwv.shape
    tr = _row_tile(a, c, 1024 * 1024)

    def body(core_ref, d_ref, r_ref, o_ref):
        o_ref[0] = (d_ref[0, 0].astype(F32) + r_ref[0].astype(F32)).astype(o_ref.dtype)

    grid_spec = pltpu.PrefetchScalarGridSpec(
        num_scalar_prefetch=1, grid=(nj, a // tr),
        in_specs=[pl.BlockSpec((1, 1, tr, c), lambda j, i, cr: (j, cr[0], i, 0)),
                  pl.BlockSpec((1, tr, c), lambda j, i, cr: (j, i, 0))],
        out_specs=pl.BlockSpec((1, tr, c), lambda j, i, cr: (j, i, 0)))
    return _pcall(body, grid_spec=grid_spec, out_shape=jax.ShapeDtypeStruct((nj, a, c), BF16), name=name,
                  compiler_params=_params(("parallel", "parallel")))(core, dwv, recv)


def _sum_chips(recv, own, chip, core, ax):
    _, a, b = recv.shape
    tr = _row_tile(a, b, 512 * 1024)

    def body(chip_ref, core_ref, r_ref, own_ref, o_ref):
        me = chip_ref[0]
        mine = own_ref[0].astype(F32)
        acc = None
        for k in range(N_CHIPS):
            term = jnp.where(me == k, mine, r_ref[k].astype(F32))
            acc = term if acc is None else acc + term
        o_ref[0] = acc

    own_spec = (pl.BlockSpec((1, tr, b), lambda i, ch, co: (0, i, ch[0])) if ax == 2
                else pl.BlockSpec((1, tr, b), lambda i, ch, co: (ch[0], i, 0)))
    grid_spec = pltpu.PrefetchScalarGridSpec(
        num_scalar_prefetch=2, grid=(a // tr,),
        in_specs=[pl.BlockSpec((N_CHIPS, tr, b), lambda i, ch, co: (0, i, 0)), own_spec],
        out_specs=pl.BlockSpec((1, tr, b), lambda i, ch, co: (co[0], i, 0)))
    return _pcall(body, grid_spec=grid_spec, out_shape=jax.ShapeDtypeStruct((2, a, b), F32),
                  name="sum_chips", compiler_params=_params(("parallel",)))(chip, core, recv, own)


def _place_own(wt, layer, chip, ax, after):
    nl, r, c = wt.shape
    half = r // 2
    tr = _row_tile(half, c, 512 * 1024)
    nb = half // tr

    def body(chip_ref, w_ref, after_ref, o_ref):
        o_ref[...] = w_ref[...].astype(BF16).reshape(o_ref.shape)

    if ax == 2:
        out_spec = pl.BlockSpec((1, tr, c), lambda h, i, ch: (h, i, ch[0]))
    else:
        out_spec = pl.BlockSpec((1, 1, tr, c), lambda h, i, ch: (ch[0], h, i, 0))
    grid_spec = pltpu.PrefetchScalarGridSpec(
        num_scalar_prefetch=1, grid=(2, nb),
        in_specs=[pl.BlockSpec((1, tr, c), lambda h, i, ch: (layer, h * nb + i, 0)), pl.BlockSpec(memory_space=pl.ANY)],
        out_specs=out_spec)
    return _pcall(body, grid_spec=grid_spec, out_shape=jax.ShapeDtypeStruct(_full_view_shape(wt.shape, ax), BF16),
                  name="place_own", compiler_params=_params(("parallel", "parallel")))(chip, wt, after)


def _adamw(w, m, v, g, layer, prev, name, after=None):
    nl, r, c = w.shape
    tr = _row_tile(r, c, 512 * 1024)
    c1 = 1.0 - ADAM_B1 ** ADAM_STEP
    c2 = 1.0 - ADAM_B2 ** ADAM_STEP

    follow = [] if after is None else [after]

    def body(w_ref, m_ref, v_ref, gin_ref, *rest):
        g_ref, d_ref, nm_ref, nv_ref = rest[-4:]
        g = gin_ref[...]
        mm = ADAM_B1 * m_ref[0] + (1.0 - ADAM_B1) * g
        vv = ADAM_B2 * v_ref[0] + (1.0 - ADAM_B2) * (g * g)
        g_ref[0] = g
        nm_ref[0] = mm
        nv_ref[0] = vv
        d_ref[0] = -ADAM_LR * ((mm / c1) / (jnp.sqrt(vv / c2) + ADAM_EPS) + ADAM_WD * w_ref[0])

    blk = pl.BlockSpec((1, tr, c), lambda i: (layer, i, 0))
    shape = jax.ShapeDtypeStruct((nl, r, c), F32)
    extra = [] if prev is None else list(prev)
    return _pcall(
        body, grid=(r // tr,),
        in_specs=[blk] * 3 + [pl.BlockSpec((tr, c), lambda i: (i, 0))] + [pl.BlockSpec(memory_space=pl.ANY)] * (len(extra) + len(follow)),
        out_specs=[blk] * 4, out_shape=[shape] * 4, input_output_aliases={4 + k: k for k in range(len(extra))}, name=name,
        compiler_params=_params(("parallel",)))(w, m, v, g, *extra, *follow)


HBM_SPEC = pl.BlockSpec(memory_space=pltpu.HBM)
COMM = pltpu.CompilerParams(has_side_effects=True)


def _position():
    x, y, c = lax.axis_index("x"), lax.axis_index("y"), lax.axis_index("c")
    chips = [(1 - x, y), (x, 1 - y), (1 - x, 1 - y)]
    return x, y, c, chips


def _remote(src, dst, send_sem, recv_sem, dev):
    return pltpu.make_async_remote_copy(src_ref=src, dst_ref=dst, send_sem=send_sem, recv_sem=recv_sem,
                                        device_id=dev, device_id_type=MESH)


def _full_view_shape(shard_shape, ax):
    _, r, c = shard_shape
    return (2, r // 2, c * N_CHIPS) if ax == 2 else (N_CHIPS, 2, r // 2, c)


def _piece(ref, ax, j, h, cs):
    if ax == 2:
        return ref.at[h, :, pl.ds(pl.multiple_of(j * cs, cs), cs)]
    return ref.at[j, h]


def _chip_block(ref, ax, j, cs):
    if ax == 2:
        return ref.at[:, :, pl.ds(pl.multiple_of(j * cs, cs), cs)]
    return ref.at[j]


SEM_SPEC = pl.BlockSpec(memory_space=pltpu.SEMAPHORE)
ANY_SPEC = pl.BlockSpec(memory_space=pl.ANY)
SPLIT = pltpu.CompilerParams(has_side_effects=pltpu.SideEffectType.DATAFLOW_SIDE_EFFECTING)


def _exchange(kind, name, bufs, build, n_sems, after=None, sems=None):
    n = len(bufs)
    if kind == 'sync':
        def body(*refs):
            mine, theirs = build(refs[n:2 * n], refs[2 * n], refs[2 * n + 1])
            for cp in mine:
                cp.start()
            for cp in theirs:
                cp.wait_recv()
            for cp in mine:
                cp.wait_send()

        return list(_pcall(
            body, in_specs=[HBM_SPEC] * n, out_specs=[HBM_SPEC] * n,
            out_shape=[jax.ShapeDtypeStruct(v.shape, v.dtype) for v in bufs], input_output_aliases={t: t for t in range(n)},
            scratch_shapes=[pltpu.SemaphoreType.DMA((n_sems,)), pltpu.SemaphoreType.DMA((n_sems,))],
            name=name, compiler_params=COMM)(*bufs))
    if kind == 'start':
        def body(*refs):
            mine, _ = build(refs[n + 3:2 * n + 3], refs[n + 1], refs[n + 2])
            for cp in mine:
                cp.start()
            refs[2 * n + 3][...] = jnp.zeros_like(refs[2 * n + 3])

        outs = _pcall(
            body, in_specs=[HBM_SPEC] * n + [ANY_SPEC],
            out_specs=[SEM_SPEC, SEM_SPEC] + [HBM_SPEC] * n + [pl.BlockSpec(memory_space=pltpu.VMEM)],
            out_shape=[pltpu.SemaphoreType.DMA((n_sems,)), pltpu.SemaphoreType.DMA((n_sems,))]
            + [pltpu.HBM(v.shape, v.dtype) for v in bufs] + [jax.ShapeDtypeStruct((8, LANES), F32)],
            input_output_aliases={t: 2 + t for t in range(n)}, name=name,
            compiler_params=SPLIT)(*[pltpu.with_memory_space_constraint(v, pltpu.HBM) for v in bufs], after)
        return (outs[0], outs[1]), list(outs[2:2 + n]), outs[2 + n]

    def body(*refs):
        mine, theirs = build(refs[:n], refs[n], refs[n + 1])
        for cp in mine:
            cp.wait_send()
        for cp in theirs:
            cp.wait_recv()

    return list(_pcall(
        body, in_specs=[HBM_SPEC] * n + [SEM_SPEC, SEM_SPEC, ANY_SPEC], out_specs=[HBM_SPEC] * n,
        out_shape=[pltpu.HBM(v.shape, v.dtype) for v in bufs], input_output_aliases={t: t for t in range(n)},
        name=name, compiler_params=SPLIT)(*bufs, sems[0], sems[1], after))


def _gather_ici_copies(axes, shard_cols):
    def build(bufs, send_sems, recv_sems):
        x, y, c, chips = _position()
        me = 2 * x + y
        mine, theirs = [], []
        for t, ax in enumerate(axes):
            own = _piece(bufs[t], ax, me, c, shard_cols[t])
            for p, (px, py) in enumerate(chips):
                k = t * 3 + p
                got = _piece(bufs[t], ax, 2 * px + py, c, shard_cols[t])
                mine.append(_remote(own, own, send_sems.at[k], recv_sems.at[k], (px, py, c)))
                theirs.append(_remote(got, got, send_sems.at[k], recv_sems.at[k], (px, py, c)))
        return mine, theirs
    return build


def _gather_d2d_copies(axes, shard_cols):
    def build(bufs, send_sems, recv_sems):
        x, y, c, chips = _position()
        mine, theirs = [], []
        for t, ax in enumerate(axes):
            for p, (px, py) in enumerate(chips):
                k = t * 3 + p
                had = _piece(bufs[t], ax, 2 * px + py, c, shard_cols[t])
                got = _piece(bufs[t], ax, 2 * px + py, 1 - c, shard_cols[t])
                mine.append(_remote(had, had, send_sems.at[k], recv_sems.at[k], (x, y, 1 - c)))
                theirs.append(_remote(got, got, send_sems.at[k], recv_sems.at[k], (x, y, 1 - c)))
        return mine, theirs
    return build


def _grads_d2d_copies(n):
    def build(bufs, send_sems, recv_sems):
        x, y, c, _ = _position()
        mine = [_remote(bufs[t].at[:, 1 - c], bufs[n + t], send_sems.at[t], recv_sems.at[t], (x, y, 1 - c)) for t in range(n)]
        return mine, mine
    return build


def _grads_ici_copies(axes):
    n = len(axes)

    def build(bufs, send_sems, recv_sems):
        x, y, c, chips = _position()
        me = 2 * x + y

        def block(t, j):
            if axes[t] == 2:
                cs = bufs[n + t].shape[2]
                return bufs[t].at[0, :, pl.ds(pl.multiple_of(j * cs, cs), cs)]
            return bufs[t].at[j]

        mine, theirs = [], []
        for t in range(n):
            for p, (px, py) in enumerate(chips):
                k = t * 3 + p
                peer = 2 * px + py
                mine.append(_remote(block(t, peer), bufs[n + t].at[me], send_sems.at[k], recv_sems.at[k], (px, py, c)))
                theirs.append(_remote(block(t, peer), bufs[n + t].at[peer], send_sems.at[k], recv_sems.at[k], (px, py, c)))
        return mine, theirs
    return build


def _join_copies(bufs, send_sems, recv_sems):
    x, y, c, _ = _position()
    mine = [_remote(b.at[c], b.at[c], send_sems.at[t], recv_sems.at[t], (x, y, 1 - c)) for t, b in enumerate(bufs)]
    theirs = [_remote(b.at[1 - c], b.at[1 - c], send_sems.at[t], recv_sems.at[t], (x, y, 1 - c)) for t, b in enumerate(bufs)]
    return mine, theirs


def _grads_recv_shape(sm, ax):
    _, a, c = sm.shape
    return (N_CHIPS, a, c // N_CHIPS if ax == 2 else c)


def _gather_small(shard):
    nl, r, cs = shard.shape

    def body(in_ref, out_ref, send_sems, recv_sems, local_sem):
        x, y, c, chips = _position()

        def cols(j):
            return out_ref.at[:, :, pl.ds(pl.multiple_of(j * cs, cs), cs)]

        me = 2 * x + y
        loc = pltpu.make_async_copy(in_ref, cols(me), local_sem)
        loc.start()
        remote = [_remote(in_ref, cols(me), send_sems.at[p], recv_sems.at[p], (px, py, c)) for p, (px, py) in enumerate(chips)]
        for cp in remote:
            cp.start()
        for p, (px, py) in enumerate(chips):
            _remote(in_ref, cols(2 * px + py), send_sems.at[p], recv_sems.at[p], (px, py, c)).wait_recv()
        for cp in remote:
            cp.wait_send()
        loc.wait()

    return _pcall(
        body, in_specs=[HBM_SPEC], out_specs=HBM_SPEC, out_shape=jax.ShapeDtypeStruct((nl, r, cs * N_CHIPS), shard.dtype),
        scratch_shapes=[pltpu.SemaphoreType.DMA((3,)), pltpu.SemaphoreType.DMA((3,)), pltpu.SemaphoreType.DMA(())],
        name="gather_small", compiler_params=COMM)(shard)


def _small_copies(bufs, send_sems, recv_sems):
    pack, land = bufs
    x, y, cc, _ = _position()
    me = 4 * x + 2 * y + cc
    mine, theirs = [], []
    for k in range(1, N_DEV):
        px, py, pc = x ^ ((k >> 2) & 1), y ^ ((k >> 1) & 1), cc ^ (k & 1)
        mine.append(_remote(pack, land.at[me], send_sems.at[k - 1], recv_sems.at[k - 1], (px, py, pc)))
        theirs.append(_remote(pack, land.at[4 * px + 2 * py + pc], send_sems.at[k - 1], recv_sems.at[k - 1], (px, py, pc)))
    return mine, theirs


def _sum_devices(land, pack, dev):
    _, r, c = land.shape
    tr = _tile(r, (672, 512, 256, 128, 64, 8))

    def body(dev_ref, l_ref, p_ref, o_ref):
        me = dev_ref[0]
        acc = None
        for k in range(N_DEV):
            term = jnp.where(me == k, p_ref[...], l_ref[k])
            acc = term if acc is None else acc + term
        o_ref[...] = acc

    grid_spec = pltpu.PrefetchScalarGridSpec(
        num_scalar_prefetch=1, grid=(r // tr,),
        in_specs=[pl.BlockSpec((N_DEV, tr, c), lambda i, dv: (0, i, 0)), pl.BlockSpec((tr, c), lambda i, dv: (i, 0))],
        out_specs=pl.BlockSpec((tr, c), lambda i, dv: (i, 0)))
    return _pcall(body, grid_spec=grid_spec, out_shape=jax.ShapeDtypeStruct((r, c), F32), name="sum_devices",
                  compiler_params=_params(("parallel",)))(dev, land, pack)


def _dims(d):
    half = d // 2
    return half // HEAD_SB, half // HEAD_XA, 3, (5 * half) // HEAD_XA


def _layer_fwd(x, mem, weight, small, l, after=None):
    h_sb, h_xa, u_blk, q_blk = _dims(x.shape[1])

    def vec(name):
        return small[name][l].reshape(1, -1)

    def use(a, name, follows, mm_name, dtype=F32):
        wt = weight(name, follows)
        return _mm(a, wt, 'nn', dtype, mm_name, after=weight.token())

    h1 = _norm_fwd(x, vec('g_mix_pre'), None, BF16, "norm_mix_pre", after)
    proj = use(h1, 'w_in', h1, "mm_proj")
    o_sb, a_sb, b_sb = _sb_fwd(proj, h_sb)
    b_st = small['b_s'][l].T
    o_gm = _gm_fwd(proj, vec('g_vnorm'), small['w_s'][l], b_st, u_blk)
    memn = _norm_fwd(mem, vec('g_mem'), None, BF16, "norm_mem")
    mem_kv = use(memn, 'w_mem_kv', o_sb, "mm_mem_kv")
    o_xa = _xa_fwd(proj, mem_kv, q_blk, h_xa)
    zg = use(h1, 'w_gate', o_sb, "mm_gate", BF16)
    branches = [use(o, wn, zg, "mm_branch") for o, wn in ((o_sb, 'w_br_sb'), (o_gm, 'w_br_gm'), (o_xa, 'w_br_xa'))]
    merged = _merge_fwd(zg, vec('b_gate'), branches)
    y1 = use(merged, 'w_out', zg, "mm_out")
    x1 = _norm_fwd(y1, vec('g_mix_post'), x, F32, "norm_mix_post")
    h2 = _norm_fwd(x1, vec('g_ffn_pre'), None, BF16, "norm_ffn_pre")
    up = use(h2, 'w_up', h2, "mm_up", BF16)
    act, gel, ggrad = _cg_fwd(up, weight('conv_w', up), vec('conv_b'))
    y2 = use(act, 'w_down', act, "mm_down")
    x2 = _norm_fwd(y2, vec('g_ffn_post'), x1, F32, "norm_ffn_post")
    saved = dict(x0=x, h1=h1, proj=proj, o_sb=o_sb, a_sb=a_sb, b_sb=b_sb, o_gm=o_gm, o_xa=o_xa, memn=memn, mem_kv=mem_kv, zg=zg,
                 branches=branches, merged=merged, y1=y1, x1=x1, h2=h2, up=up, act=act, gel=gel, ggrad=ggrad, y2=y2, b_st=b_st)
    return x2, saved


def _layer_bwd(dx, mem, sv, full, small, l, after, emit):
    h_sb, h_xa, u_blk, q_blk = _dims(dx.shape[1])

    def vec(name):
        return small[name][l].reshape(1, -1)

    gb, gs = {}, {}
    dy2, gs['g_ffn_post'] = _norm_bwd(sv['y2'], vec('g_ffn_post'), [dx], None, BF16, "norm_ffn_post_bwd", after)
    gb['w_down'] = _mm(sv['act'], dy2, 'tn', BF16, "mm_down_dw")
    dact = _mm(dy2, full['w_down'], 'nt', F32, "mm_down_dx")
    dgate, dval, gs['conv_w'], gs['conv_b'] = _cg_bwd(sv['up'], sv['gel'], sv['ggrad'], full['conv_w'], dact)
    gb['w_up'] = _mm(sv['h2'], [dgate, dval], 'tn', BF16, "mm_up_dw")
    token = emit(0, gb)
    dh2 = _mm([dgate, dval], full['w_up'], 'nt', F32, "mm_up_dx", after=token)
    token = emit.flush(dh2)
    dx1, gs['g_ffn_pre'] = _norm_bwd(sv['x1'], vec('g_ffn_pre'), [dh2], dx, F32, "norm_ffn_pre_bwd", token)
    dy1, gs['g_mix_post'] = _norm_bwd(sv['y1'], vec('g_mix_post'), [dx1], None, BF16, "norm_mix_post_bwd")
    gb['w_out'] = _mm(sv['merged'], dy1, 'tn', BF16, "mm_out_dw")
    dmerged = _mm(dy1, full['w_out'], 'nt', F32, "mm_out_dx")
    dzg, dbr, gs['b_gate'] = _merge_bwd(sv['zg'], vec('b_gate'), sv['branches'], dmerged)
    douts = []
    for o, db, wn in ((sv['o_sb'], dbr[0], 'w_br_sb'), (sv['o_gm'], dbr[1], 'w_br_gm'), (sv['o_xa'], dbr[2], 'w_br_xa')):
        gb[wn] = _mm(o, db, 'tn', BF16, "mm_branch_dw")
        douts.append(_mm(db, full[wn], 'nt', F32, "mm_branch_dx"))
    gb['w_gate'] = _mm(sv['h1'], dzg, 'tn', BF16, "mm_gate_dw")
    dq_xa, dk_xa, dv_xa = _xa_bwd(sv['proj'], sv['mem_kv'], douts[2], q_blk, h_xa)
    dmem_kv = jnp.concatenate([dk_xa, dv_xa], axis=1).astype(BF16)
    gb['w_mem_kv'] = _mm(sv['memn'], dmem_kv, 'tn', BF16, "mm_mem_kv_dw")
    token = emit(1, gb)
    dh1_gate = _mm(dzg, full['w_gate'], 'nt', F32, "mm_gate_dx", after=token)
    token = emit.flush(dh1_gate)
    dmemn = _mm(dmem_kv, full['w_mem_kv'], 'nt', F32, "mm_mem_kv_dx")
    _, gs['g_mem'] = _norm_bwd(mem, vec('g_mem'), [dmemn], None, BF16, "norm_mem_bwd")
    du, dv, gs['g_vnorm'], gs['w_s'], db_st = _gm_bwd(sv['proj'], vec('g_vnorm'), small['w_s'][l], sv['b_st'], douts[1], u_blk)
    gs['b_s'] = db_st.T
    dq, dk, dvv = _sb_bwd(sv['proj'], sv['a_sb'], sv['b_sb'], douts[0], h_sb, token)
    dproj = jnp.concatenate([dq, dk, dvv, du, dv, dq_xa], axis=1).astype(BF16)
    gb['w_in'] = _mm(sv['h1'], dproj, 'tn', BF16, "mm_proj_dw")
    token = emit(2, gb)
    dh1_proj = _mm(dproj, full['w_in'], 'nt', F32, "mm_proj_dx")
    dx0, gs['g_mix_pre'] = _norm_bwd(sv['x0'], vec('g_mix_pre'), [dh1_gate, dh1_proj], dx1, F32, "norm_mix_pre_bwd", token)
    return dx0, gs


class _Given:
    def __init__(self, full):
        self.full = full

    def __call__(self, name, follows):
        return self.full[name]

    def token(self):
        return None


def _local_step(x, mem, target, full, small):
    n_layers = len(full['w_in'])
    saved = []
    for l in range(n_layers):
        x, sv = _layer_fwd(x, mem, _Given({n: full[n][l] for n in full}), small, l)
        saved.append(sv)
    sq, dx = _loss_head(x, target)
    gbig = {n: [None] * n_layers for n in BIG}
    gsmall = {n: [None] * n_layers for n in SMALL + ['conv_w']}
    class Collect:
        def __init__(self, l):
            self.l = l

        def __call__(self, g, gb):
            for n in BWD_GROUPS[g]:
                gbig[n][self.l] = gb[n]

        def flush(self, follows):
            return None

    for l in reversed(range(n_layers)):
        dx, gs = _layer_bwd(dx, mem, saved[l], {n: full[n][l] for n in full}, small, l, None, Collect(l))
        for n in gs:
            gsmall[n][l] = gs[n]
    return sq, dx, gbig, gsmall


def _pack(arrays, rows_multiple):
    flat = jnp.concatenate([a.reshape(-1).astype(F32) for a in arrays])
    rows = -(-flat.shape[0] // LANES)
    rows = -(-rows // rows_multiple) * rows_multiple
    return jnp.pad(flat, (0, rows * LANES - flat.shape[0])).reshape(rows, LANES)


def _unpack(pack, like):
    flat = pack.reshape(-1)
    out, off = [], 0
    for a in like:
        out.append(flat[off:off + a.size].reshape(a.shape))
        off += a.size
    return out


def _grad_view(g, ax):
    r, c = g.shape
    return g.reshape(1, 2, r // 2, c) if ax == 2 else g.reshape(N_CHIPS, 2, r // (2 * N_CHIPS), c)


def kernel(x, mem, g_mix_pre, w_in, g_vnorm, w_s, b_s, g_mem, w_mem_kv, w_gate, b_gate, w_br_sb, w_br_gm, w_br_xa, w_out, g_mix_post, g_ffn_pre, w_up, conv_w, conv_b, w_down, g_ffn_post, loss_target, m_g_mix_pre, m_w_in, m_g_vnorm, m_w_s, m_b_s, m_g_mem, m_w_mem_kv, m_w_gate, m_b_gate, m_w_br_sb, m_w_br_gm, m_w_br_xa, m_w_out, m_g_mix_post, m_g_ffn_pre, m_w_up, m_conv_w, m_conv_b, m_w_down, m_g_ffn_post, v_g_mix_pre, v_w_in, v_g_vnorm, v_w_s, v_b_s, v_g_mem, v_w_mem_kv, v_w_gate, v_b_gate, v_w_br_sb, v_w_br_gm, v_w_br_xa, v_w_out, v_g_mix_post, v_g_ffn_pre, v_w_up, v_conv_w, v_conv_b, v_w_down, v_g_ffn_post):
    w = dict(g_mix_pre=g_mix_pre, w_in=w_in, g_vnorm=g_vnorm, w_s=w_s, b_s=b_s, g_mem=g_mem, w_mem_kv=w_mem_kv,
             w_gate=w_gate, b_gate=b_gate, w_br_sb=w_br_sb, w_br_gm=w_br_gm, w_br_xa=w_br_xa, w_out=w_out,
             g_mix_post=g_mix_post, g_ffn_pre=g_ffn_pre, w_up=w_up, conv_w=conv_w, conv_b=conv_b, w_down=w_down,
             g_ffn_post=g_ffn_post)
    m = dict(g_mix_pre=m_g_mix_pre, w_in=m_w_in, g_vnorm=m_g_vnorm, w_s=m_w_s, b_s=m_b_s, g_mem=m_g_mem,
             w_mem_kv=m_w_mem_kv, w_gate=m_w_gate, b_gate=m_b_gate, w_br_sb=m_w_br_sb, w_br_gm=m_w_br_gm,
             w_br_xa=m_w_br_xa, w_out=m_w_out, g_mix_post=m_g_mix_post, g_ffn_pre=m_g_ffn_pre, w_up=m_w_up,
             conv_w=m_conv_w, conv_b=m_conv_b, w_down=m_w_down, g_ffn_post=m_g_ffn_post)
    v = dict(g_mix_pre=v_g_mix_pre, w_in=v_w_in, g_vnorm=v_g_vnorm, w_s=v_w_s, b_s=v_b_s, g_mem=v_g_mem,
             w_mem_kv=v_w_mem_kv, w_gate=v_w_gate, b_gate=v_b_gate, w_br_sb=v_w_br_sb, w_br_gm=v_w_br_gm,
             w_br_xa=v_w_br_xa, w_out=v_w_out, g_mix_post=v_g_mix_post, g_ffn_pre=v_g_ffn_pre, w_up=v_w_up,
             conv_w=v_conv_w, conv_b=v_conv_b, w_down=v_w_down, g_ffn_post=v_g_ffn_post)
    n_layers = w_in.shape[0]
    d = x.shape[-1]
    core = lax.axis_index("c").astype(jnp.int32).reshape(1)
    chip = (2 * lax.axis_index("x") + lax.axis_index("y")).astype(jnp.int32).reshape(1)
    small = {n: w[n] for n in SMALL}
    xs, mems, target = x[0], mem[0], loss_target[0]

    conv_w_full = _gather_small(conv_w)

    def as_full(vw, ax):
        return vw.reshape(-1, vw.shape[-1]) if ax == 1 else vw.reshape(vw.shape[0] * vw.shape[1], vw.shape[2])

    stages = {}
    keys = [(l, g) for l in range(n_layers) for g in range(len(FWD_GROUPS))]

    def start_gathers(l, token):
        for g, names in enumerate(FWD_GROUPS):
            ax_g = [BIG_AXIS[n] for n in names]
            cols_g = [w[n].shape[2] for n in names]
            views = [_place_own(w[n], l, chip, ax, token) for n, ax in zip(names, ax_g)]
            sems, views, token = _exchange('start', f"gather_ici_start_{l}_{g}", views, _gather_ici_copies(ax_g, cols_g),
                                           3 * len(names), after=token)
            stages[l, g] = dict(names=names, ax=ax_g, cols=cols_g, views=views, ici=sems, d2d=None, full=None)
        return token

    token = start_gathers(0, conv_w_full)

    def cross_cores(key, follows):
        st, (l, g) = stages[key], key
        n3 = 3 * len(st['names'])
        views = _exchange('wait', f"gather_ici_wait_{l}_{g}", st['views'], _gather_ici_copies(st['ax'], st['cols']), n3,
                          after=follows, sems=st['ici'])
        st['d2d'], st['views'], tok = _exchange('start', f"gather_d2d_start_{l}_{g}", views,
                                                _gather_d2d_copies(st['ax'], st['cols']), n3, after=core)
        return tok

    class Weights:
        def __init__(self, l):
            self.l, self.tok = l, None

        def __call__(self, name, follows):
            if name == 'conv_w':
                return conv_w_full[self.l]
            key = (self.l, [g for g, names in enumerate(FWD_GROUPS) if name in names][0])
            if key[1] == 1 and (self.l + 1, 0) not in stages and self.l + 1 < n_layers:
                self.tok = start_gathers(self.l + 1, follows)
            st = stages[key]
            if st['full'] is None:
                if st['d2d'] is None:
                    cross_cores(key, follows)
                views = _exchange('wait', f"gather_d2d_wait_{key[0]}_{key[1]}", st['views'],
                                  _gather_d2d_copies(st['ax'], st['cols']), 3 * len(st['names']), after=follows, sems=st['d2d'])
                st['full'] = {n: as_full(vw, ax) for n, vw, ax in zip(st['names'], views, st['ax'])}
            nxt = keys.index(key) + 1
            if name == st['names'][-1] and nxt < len(keys) and key != keys[0] and stages[keys[nxt]]['d2d'] is None:
                self.tok = cross_cores(keys[nxt], follows)
            return st['full'][name]

        def token(self):
            return self.tok

    fulls, saved = [], []
    for l in range(n_layers):
        xs, sv = _layer_fwd(xs, mems, Weights(l), small, l, token if l == 0 else None)
        fulls.append({n: stages[l, g]['full'][n] for g, names in enumerate(FWD_GROUPS) for n in names} | {'conv_w': conv_w_full[l]})
        saved.append(sv)
    sq, dx = _loss_head(xs, target)
    loss = lax.psum(0.5 * jnp.sum(sq) / d, ("x", "y", "c"))

    sent = []

    def to_chips(l, g, names, ax_g, bufs, after):
        n = len(names)
        sums = [_sum_halves(dv, th, core, "sum_halves") for dv, th in zip(bufs[:n], bufs[n:])]
        lands = [lax.empty(_grads_recv_shape(sm, ax), sm.dtype) for sm, ax in zip(sums, ax_g)]
        sems, bufs, tok = _exchange('start', f"grads_ici_start_{l}_{g}", sums + lands, _grads_ici_copies(ax_g), 3 * n,
                                    after=core if after is None else after)
        sent.append((l, g, names, ax_g, bufs, sems))
        return tok

    class Grads:
        def __init__(self, l):
            self.l, self.crossing = l, None

        def __call__(self, g, gb):
            names = BWD_GROUPS[g]
            ax_g = [BIG_AXIS[n] for n in names]
            n = len(names)
            dwvs = [_grad_view(gb[nm], ax) for nm, ax in zip(names, ax_g)]
            lands = [lax.empty((dv.shape[0],) + dv.shape[2:], dv.dtype) for dv in dwvs]
            if g + 1 < len(BWD_GROUPS):
                sems, bufs, tok = _exchange('start', f"grads_d2d_start_{self.l}_{g}", dwvs + lands, _grads_d2d_copies(n), n,
                                            after=core)
                self.crossing = (g, names, ax_g, bufs, sems)
                return tok
            return to_chips(self.l, g, names, ax_g, _exchange('sync', "grads_d2d", dwvs + lands, _grads_d2d_copies(n), n), None)

        def flush(self, follows):
            if self.crossing is None:
                return None
            (g, names, ax_g, bufs, sems), self.crossing = self.crossing, None
            bufs = _exchange('wait', f"grads_d2d_wait_{self.l}_{g}", bufs, _grads_d2d_copies(len(names)), len(names),
                             after=follows, sems=sems)
            return to_chips(self.l, g, names, ax_g, bufs, None)

    gsmall = {n: [None] * n_layers for n in SMALL + ['conv_w']}
    for l in reversed(range(n_layers)):
        dx, gs = _layer_bwd(dx, mems, saved[l], fulls[l], small, l, None, Grads(l))
        for n in gs:
            gsmall[n][l] = gs[n]

    names_small = SMALL + ['conv_w']
    small_full = [jnp.stack(gsmall[n]).reshape(w[n].shape) for n in SMALL]
    conv_w_grad = jnp.stack(gsmall['conv_w'])
    pack = _pack(small_full + [conv_w_grad], 8)
    small_sems, small_bufs, _ = _exchange('start', "small_start", [pack, lax.empty((N_DEV,) + pack.shape, F32)], _small_copies,
                                          N_DEV - 1, after=dx)

    def small_update(follows):
        pk, land = _exchange('wait', "small_wait", small_bufs, _small_copies, N_DEV - 1, after=follows, sems=small_sems)
        device = (4 * lax.axis_index("x") + 2 * lax.axis_index("y") + lax.axis_index("c")).astype(jnp.int32).reshape(1)
        summed = _sum_devices(land, pk, device)
        *small_g, conv_w_g = _unpack(summed, small_full + [conv_w_grad])
        shard = conv_w.shape[-1]
        conv_w_g = lax.dynamic_slice_in_dim(conv_w_g, chip[0] * shard, shard, axis=2)
        packed = [_pack([p[n] for n in names_small], 256) for p in (w, m, v)]
        gpack = _pack(small_g + [conv_w_g], 256)
        res = _adamw(packed[0][None], packed[1][None], packed[2][None], gpack, 0, None, "adamw_small")
        like = [w[n] for n in names_small]
        unpacked = [_unpack(r[0], like) for r in res]
        return {n: tuple(u[i] for u in unpacked) for i, n in enumerate(names_small)}

    out = {}

    def update(joining, follows):
        l, g, names, halves, sems = joining
        halves = _exchange('wait', f"join_wait_{l}_{g}", halves, _join_copies, len(names), after=follows, sems=sems)
        last = None
        for n, hv in zip(names, halves):
            out[n] = _adamw(w[n], m[n], v[n], hv.reshape(w[n].shape[1:]), l, out.get(n), "adamw_big", last)
            last = out[n][0]
        return last

    joining, follows = None, dx
    for k, (l, g, names, ax_g, bufs, sems) in enumerate(sent):
        n = len(names)
        bufs = _exchange('wait', f"grads_ici_wait_{l}_{g}", bufs, _grads_ici_copies(ax_g), 3 * n, after=follows, sems=sems)
        halves = [_sum_chips(r, sm, chip, core, ax) for sm, r, ax in zip(bufs[:n], bufs[n:], ax_g)]
        jsems, halves, tok = _exchange('start', f"join_start_{l}_{g}", halves, _join_copies, n, after=core)
        if joining is not None:
            follows = update(joining, tok)
        joining = (l, g, names, halves, jsems)
    out.update(small_update(update(joining, follows)))

    return (loss, dx[None], *[out[n][0] for n in WEIGHTS], *[out[n][1] for n in WEIGHTS],
            *[out[n][2] for n in WEIGHTS], *[out[n][3] for n in WEIGHTS])
```

```python
import functools
import math

import jax
import jax.numpy as jnp
from jax import lax
from jax.experimental import pallas as pl
from jax.experimental.pallas import tpu as pltpu

F32 = jnp.float32
BF16 = jnp.bfloat16
EPS = 1e-6
HEAD_SB = 128
GROUP_GM = 128
CHUNK = 64
HEAD_XA = 256
CONV_TAPS = 3
N_CHIPS = 4
N_DEV = 8
LANES = 128
MIB = 1024 * 1024
VMEM_LIMIT = 48 * MIB
SPLITS = 1

ADAM_LR = 0.001
ADAM_B1 = 0.9
ADAM_B2 = 0.999
ADAM_EPS = 1e-08
ADAM_WD = 0.01
ADAM_STEP = 10

WEIGHTS = ['g_mix_pre', 'w_in', 'g_vnorm', 'w_s', 'b_s', 'g_mem', 'w_mem_kv', 'w_gate', 'b_gate', 'w_br_sb',
           'w_br_gm', 'w_br_xa', 'w_out', 'g_mix_post', 'g_ffn_pre', 'w_up', 'conv_w', 'conv_b', 'w_down',
           'g_ffn_post']
BIG_AXIS = {'w_in': 2, 'w_mem_kv': 1, 'w_gate': 2, 'w_br_sb': 2, 'w_br_gm': 2, 'w_br_xa': 2, 'w_out': 1,
            'w_up': 2, 'w_down': 1}
BIG = list(BIG_AXIS)
FWD_GROUPS = [['w_in'], ['w_mem_kv', 'w_gate'], ['w_br_sb', 'w_br_gm', 'w_br_xa', 'w_out'], ['w_up'], ['w_down']]
BWD_GROUPS = [['w_down', 'w_up'], ['w_out', 'w_br_sb', 'w_br_gm', 'w_br_xa', 'w_gate', 'w_mem_kv'], ['w_in']]
SMALL = ['g_mix_pre', 'g_vnorm', 'w_s', 'b_s', 'g_mem', 'b_gate', 'g_mix_post', 'g_ffn_pre', 'conv_b', 'g_ffn_post']
MESH = pl.DeviceIdType.MESH


def _pcall(body, **kw):
    return pl.pallas_call(body, **kw)


def _params(sem=None, vmem=VMEM_LIMIT):
    return pltpu.CompilerParams(dimension_semantics=sem, vmem_limit_bytes=vmem)


def _tile(n, cands):
    for c in cands:
        if n % c == 0:
            return c
    return n


_GELU_C = math.sqrt(2.0 / math.pi)
_GELU_A = 0.044715


def _gelu(x):
    return 0.5 * x * (1.0 + jnp.tanh(_GELU_C * (x + _GELU_A * (x * x * x))))


def _gelu_and_grad(x):
    x2 = x * x
    t = jnp.tanh(_GELU_C * (x + _GELU_A * (x2 * x)))
    val = 0.5 * x * (1.0 + t)
    grad = 0.5 * (1.0 + t) + 0.5 * x * (1.0 - t * t) * (_GELU_C * (1.0 + 3.0 * _GELU_A * x2))
    return val, grad


def _softplus(z):
    return jnp.maximum(z, 0.0) + jnp.log(1.0 + jnp.exp(-jnp.abs(z)))


def _dot(a, b):
    return jnp.dot(a, b, preferred_element_type=F32)


def _dot_nt(a, b):
    return lax.dot_general(a, b, (((1,), (1,)), ((), ())), preferred_element_type=F32)


def _dot_tn(a, b):
    return lax.dot_general(a, b, (((0,), (0,)), ((), ())), preferred_element_type=F32)


def _split_dot(a, m):
    out = None
    rest = a
    for _ in range(SPLITS):
        piece = rest.astype(BF16)
        rest = rest - piece.astype(F32)
        term = _dot(piece, m)
        out = term if out is None else out + term
    return out


def _mm(a, b, mode, out_dtype, name, tm=None, tn=None, tk=None, after=None):
    a_parts = list(a) if isinstance(a, (list, tuple)) else [a]
    b_parts = list(b) if isinstance(b, (list, tuple)) else [b]
    assert len(a_parts) == 1 or mode == 'nt'
    assert len(b_parts) == 1 or mode == 'tn'
    na, nb = len(a_parts), len(b_parts)
    if mode == 'nn':
        (m, kc), (kc2, n) = a_parts[0].shape, b_parts[0].shape
    elif mode == 'nt':
        (m, kp), (n, kc2) = a_parts[0].shape, b_parts[0].shape
        kc = kp * na
    else:
        (kc, m), (kc2, npiece) = a_parts[0].shape, b_parts[0].shape
        n = npiece * nb
    assert kc == kc2, (a_parts[0].shape, b_parts[0].shape, mode)
    tm = tm or _tile(m, (1024, 512, 256, 128))
    tn = tn or _tile(n // nb, (1024, 512, 256, 128))
    k_max = 3072 // na
    tk = tk or ((kc // na) if kc // na <= k_max else _tile(kc // na, [c for c in (3072, 2816, 2048, 1536, 1408, 1024, 512) if c <= k_max]))
    nk = kc // tk
    k_per = nk // na
    n_per = (n // tn) // nb
    dot = {'nn': _dot, 'nt': _dot_nt, 'tn': _dot_tn}[mode]

    def within(idx, p, per):
        return jnp.clip(idx - p * per, 0, per - 1)

    if mode == 'tn':
        a_specs = [pl.BlockSpec((tk, tm), lambda i, j, k: (k, i))]
        b_specs = [pl.BlockSpec((tk, tn), functools.partial(lambda i, j, k, p: (k, within(j, p, n_per)), p=p)) for p in range(nb)]
    else:
        a_specs = [pl.BlockSpec((tm, tk), functools.partial(lambda i, j, k, p: (i, within(k, p, k_per)), p=p)) for p in range(na)]
        b_specs = [pl.BlockSpec((tn, tk), lambda i, j, k: (j, k)) if mode == 'nt' else pl.BlockSpec((tk, tn), lambda i, j, k: (k, j))]

    extra = [] if after is None else [after]
    extra_specs = [pl.BlockSpec(memory_space=pl.ANY)] * len(extra)

    def product(a_refs, b_refs, store):
        if na == 1 and nb == 1:
            store(dot(a_refs[0][...].astype(BF16), b_refs[0][...].astype(BF16)))
            return
        which = (pl.program_id(2) // k_per) if na > 1 else (pl.program_id(1) // n_per)
        for p in range(max(na, nb)):
            @pl.when(which == p)
            def _(p=p):
                store(dot(a_refs[p if na > 1 else 0][...].astype(BF16), b_refs[p if nb > 1 else 0][...].astype(BF16)))

    if nk == 1:
        def body(*refs):
            o_ref = refs[-1]

            def store(part):
                o_ref[...] = part.astype(o_ref.dtype)

            product(refs[:na], refs[na:na + nb], store)
        scratch = []
    else:
        def body(*refs):
            o_ref, acc_ref = refs[-2], refs[-1]
            k = pl.program_id(2)

            def store(part):
                @pl.when(k == 0)
                def _():
                    acc_ref[...] = part

                @pl.when(k > 0)
                def _():
                    acc_ref[...] += part

            product(refs[:na], refs[na:na + nb], store)

            @pl.when(k == nk - 1)
            def _():
                o_ref[...] = acc_ref[...].astype(o_ref.dtype)
        scratch = [pltpu.VMEM((tm, tn), F32)]

    return _pcall(
        body, grid=(m // tm, n // tn, nk), in_specs=a_specs + b_specs + extra_specs,
        out_specs=pl.BlockSpec((tm, tn), lambda i, j, k: (i, j)),
        out_shape=jax.ShapeDtypeStruct((m, n), out_dtype), scratch_shapes=scratch, name=name,
        compiler_params=_params(("parallel", "parallel", "arbitrary")))(*a_parts, *b_parts, *extra)


def _norm_fwd(x, g, res, out_dtype, name, after=None):
    s, d = x.shape
    tr = _tile(s, (256, 128))
    has_res = res is not None
    has_after = after is not None

    def body(*refs):
        x_ref, g_ref = refs[0], refs[1]
        o_ref = refs[-1]
        xv = x_ref[...]
        y = xv * lax.rsqrt(jnp.mean(xv * xv, axis=-1, keepdims=True) + EPS) * g_ref[...]
        if has_res:
            y = y + refs[2][...]
        o_ref[...] = y.astype(o_ref.dtype)

    row = pl.BlockSpec((tr, d), lambda i: (i, 0))
    ins = [x, g] + ([res] if has_res else []) + ([after] if has_after else [])
    return _pcall(
        body, grid=(s // tr,),
        in_specs=[row, pl.BlockSpec((1, d), lambda i: (0, 0))] + ([row] if has_res else [])
        + ([pl.BlockSpec(memory_space=pl.ANY)] if has_after else []),
        out_specs=row, out_shape=jax.ShapeDtypeStruct((s, d), out_dtype), name=name,
        compiler_params=_params(("parallel",)))(*ins)


def _norm_bwd(x, g, douts, dres, out_dtype, name, after=None):
    s, d = x.shape
    tr = _tile(s, (256, 128))
    nd = len(douts)
    has_res = dres is not None
    has_after = after is not None

    def body(*refs):
        x_ref, g_ref = refs[0], refs[1]
        dx_ref, dg_ref = refs[-2], refs[-1]
        dout = refs[2][...].astype(F32)
        for r in refs[3:2 + nd]:
            dout = dout + r[...].astype(F32)
        xv = x_ref[...]
        r = lax.rsqrt(jnp.mean(xv * xv, axis=-1, keepdims=True) + EPS)
        n = xv * r
        dn = dout * g_ref[...]
        dx = r * (dn - n * jnp.mean(dn * n, axis=-1, keepdims=True))
        if has_res:
            dx = dx + refs[2 + nd][...]
        dx_ref[...] = dx.astype(dx_ref.dtype)

        @pl.when(pl.program_id(0) == 0)
        def _():
            dg_ref[...] = jnp.zeros_like(dg_ref)

        dg_ref[...] += jnp.sum(dout * n, axis=0, keepdims=True)

    row = pl.BlockSpec((tr, d), lambda i: (i, 0))
    vec = pl.BlockSpec((1, d), lambda i: (0, 0))
    ins = [x, g] + list(douts) + ([dres] if has_res else []) + ([after] if has_after else [])
    return _pcall(
        body, grid=(s // tr,),
        in_specs=[row, vec] + [row] * (nd + int(has_res)) + ([pl.BlockSpec(memory_space=pl.ANY)] if has_after else []),
        out_specs=[row, vec],
        out_shape=[jax.ShapeDtypeStruct((s, d), out_dtype), jax.ShapeDtypeStruct((1, d), F32)], name=name,
        compiler_params=_params(("arbitrary",)))(*ins)


def _norm_pair_bwd(x1, g_pre, dh, dres, y1, g_post, name, after=None):
    s, d = x1.shape
    tr = _tile(s, (256, 128))
    follow = [] if after is None else [after]

    def grad(xv, g, dout):
        r = lax.rsqrt(jnp.mean(xv * xv, axis=-1, keepdims=True) + EPS)
        n = xv * r
        dn = dout * g
        return r * (dn - n * jnp.mean(dn * n, axis=-1, keepdims=True)), jnp.sum(dout * n, axis=0, keepdims=True)

    def body(x_ref, gp_ref, dh_ref, dr_ref, y_ref, gq_ref, *rest):
        dx_ref, dy_ref, dgp_ref, dgq_ref = rest[len(follow):]

        @pl.when(pl.program_id(0) == 0)
        def _():
            dgp_ref[...] = jnp.zeros_like(dgp_ref)
            dgq_ref[...] = jnp.zeros_like(dgq_ref)

        dxn, dgp = grad(x_ref[...], gp_ref[...], dh_ref[...])
        dx1 = dxn + dr_ref[...]
        dx_ref[...] = dx1
        dy, dgq = grad(y_ref[...], gq_ref[...], dx1)
        dy_ref[...] = dy.astype(dy_ref.dtype)
        dgp_ref[...] += dgp
        dgq_ref[...] += dgq

    row = pl.BlockSpec((tr, d), lambda i: (i, 0))
    vec = pl.BlockSpec((1, d), lambda i: (0, 0))
    return _pcall(
        body, grid=(s // tr,), in_specs=[row, vec, row, row, row, vec] + [pl.BlockSpec(memory_space=pl.ANY)] * len(follow),
        out_specs=[row, row, vec, vec],
        out_shape=[jax.ShapeDtypeStruct((s, d), F32), jax.ShapeDtypeStruct((s, d), BF16), jax.ShapeDtypeStruct((1, d), F32),
                   jax.ShapeDtypeStruct((1, d), F32)], name=name, compiler_params=_params(("arbitrary",)))(
        x1, g_pre, dh, dres, y1, g_post, *follow)


def _loss_head(y, target):
    s, d = y.shape
    tr = _tile(s, (256, 128))

    def body(y_ref, t_ref, sq_ref, dy_ref):
        e = y_ref[...] - t_ref[...]
        dy_ref[...] = e * (1.0 / d)

        @pl.when(pl.program_id(0) == 0)
        def _():
            sq_ref[...] = jnp.zeros_like(sq_ref)

        sq_ref[...] += jnp.sum(e * e, axis=0, keepdims=True)

    row = pl.BlockSpec((tr, d), lambda i: (i, 0))
    return _pcall(
        body, grid=(s // tr,), in_specs=[row, row], out_specs=[pl.BlockSpec((1, d), lambda i: (0, 0)), row],
        out_shape=[jax.ShapeDtypeStruct((1, d), F32), jax.ShapeDtypeStruct((s, d), F32)], name="loss_head",
        compiler_params=_params(("arbitrary",)))(y, target)


NEVER = -1e30
SB_QUERIES = 512


def _sb_sum_matrix(later):
    r = lax.broadcasted_iota(jnp.int32, (HEAD_SB, 2 * HEAD_SB), 0)
    c = lax.broadcasted_iota(jnp.int32, (HEAD_SB, 2 * HEAD_SB), 1)
    tri = jnp.where((r > c) if later else (r < c), 1.0, 0.0)
    return jnp.where(c < HEAD_SB, tri, 1.0).astype(BF16)


def _sb_mask(tq, q0, k0):
    row = lax.broadcasted_iota(jnp.int32, (tq, HEAD_SB), 0)
    col = lax.broadcasted_iota(jnp.int32, (tq, HEAD_SB), 1)
    return (k0 + col) < (q0 + row)


def _sb_fwd(proj, n_heads):
    s = proj.shape[0]
    tq = min(SB_QUERIES, s)
    per = tq // HEAD_SB
    scale = HEAD_SB ** -0.5

    def body(q_ref, k_ref, v_ref, o_ref, a_ref, b_ref, acc_ref, c_ref):
        i = pl.program_id(1)
        q = q_ref[...].astype(BF16)
        sums = _sb_sum_matrix(True)
        acc_ref[...] = jnp.zeros_like(acc_ref)
        c_ref[...] = jnp.zeros_like(c_ref)
        last = (i + 1) * per - 1

        def scores(j, masked):
            off = pl.multiple_of(j * HEAD_SB, HEAD_SB)
            z = _dot_nt(q, k_ref[pl.ds(off, HEAD_SB), :].astype(BF16)) * scale
            sp = _softplus(z)
            logb = z - sp
            if masked:
                mask = _sb_mask(tq, i * tq, off)
                logb = jnp.where(mask, logb, NEVER)
                sp = jnp.where(mask, sp, 0.0)
            return logb, _split_dot(sp, sums)

        def values(j, logb, both):
            off = pl.multiple_of(j * HEAD_SB, HEAD_SB)
            c = c_ref[...]
            a = jnp.exp(logb - both[:, :HEAD_SB] - c).astype(BF16)
            a_ref[0, 0, j] = a
            b_ref[0, 0, j] = jnp.exp(logb).astype(BF16)
            acc_ref[...] += _dot(a, v_ref[pl.ds(off, HEAD_SB), :].astype(BF16))
            c_ref[...] = c + both[:, HEAD_SB:]

        def step(jj, carry, masked):
            j = last - jj
            nxt = scores(j, masked)
            values(jnp.minimum(j + 1, last), *carry)
            return nxt

        idle = (jnp.full((tq, HEAD_SB), NEVER, F32), jnp.zeros((tq, 2 * HEAD_SB), F32))
        carry = idle
        for jj in range(per):
            carry = step(jj, carry, True)
        def group(jg, carry):
            for u in range(per):
                carry = step(per * jg + u, carry, False)
            return carry

        carry = lax.fori_loop(1, i + 1, group, carry)
        values(0, *carry)
        o_ref[...] = acc_ref[...].astype(o_ref.dtype)

    h = n_heads
    blk = pl.BlockSpec((tq, HEAD_SB), lambda hh, i: (i, hh))
    return _pcall(
        body, grid=(h, s // tq),
        in_specs=[blk, pl.BlockSpec((s, HEAD_SB), lambda hh, i: (0, h + hh)),
                  pl.BlockSpec((s, HEAD_SB), lambda hh, i: (0, 2 * h + hh))],
        out_specs=[blk] + [pl.BlockSpec((1, 1, s // HEAD_SB, tq, HEAD_SB), lambda hh, i: (hh, i, 0, 0, 0))] * 2,
        out_shape=[jax.ShapeDtypeStruct((s, h * HEAD_SB), BF16)]
        + [jax.ShapeDtypeStruct((h, s // tq, s // HEAD_SB, tq, HEAD_SB), BF16)] * 2,
        scratch_shapes=[pltpu.VMEM((tq, HEAD_SB), F32), pltpu.VMEM((tq, HEAD_SB), F32)],
        name="sb_fwd", compiler_params=_params(("parallel", "arbitrary")))(proj, proj, proj)


def _sb_bwd(proj, a_saved, b_saved, do, n_heads, after=None):
    s = proj.shape[0]
    tq = min(SB_QUERIES, s)
    per = tq // HEAD_SB
    scale = HEAD_SB ** -0.5
    follow = [] if after is None else [after]

    def body(q_ref, k_ref, v_ref, do_ref, a_ref, b_ref, *rest):
        dq_ref, dk_ref, dv_ref, run_ref, acc_ref = rest[len(follow):]
        i = pl.program_id(1)

        @pl.when(i == 0)
        def _():
            dk_ref[...] = jnp.zeros_like(dk_ref)
            dv_ref[...] = jnp.zeros_like(dv_ref)

        q = q_ref[...].astype(BF16)
        dob = do_ref[...].astype(BF16)
        run_ref[...] = jnp.zeros_like(run_ref)
        acc_ref[...] = jnp.zeros_like(acc_ref)
        earlier = _sb_sum_matrix(False)
        first_diagonal = i * per

        def step(j, masked):
            off = pl.multiple_of(j * HEAD_SB, HEAD_SB)
            kb = k_ref[pl.ds(off, HEAD_SB), :].astype(BF16)
            vb = v_ref[pl.ds(off, HEAD_SB), :].astype(BF16)
            a = a_ref[0, 0, j]
            g = a.astype(F32) * _dot_nt(dob, vb)
            dv_ref[pl.ds(off, HEAD_SB), :] += _dot_tn(a, dob)
            beta = b_ref[0, 0, j].astype(F32)
            both = _split_dot(g, earlier)
            p = run_ref[...]
            dzb = ((g * (1.0 - beta) - beta * (both[:, :HEAD_SB] + p)) * scale).astype(BF16)
            dk_ref[pl.ds(off, HEAD_SB), :] += _dot_tn(dzb, q)
            acc_ref[...] += _dot(dzb, kb)
            run_ref[...] = p + both[:, HEAD_SB:]

        def group(jg, carry):
            for u in range(per):
                step(per * jg + u, False)
            return carry

        lax.fori_loop(0, i, group, 0)
        for u in range(per):
            step(first_diagonal + u, True)
        dq_ref[...] = acc_ref[...]

    h = n_heads
    blk = pl.BlockSpec((tq, HEAD_SB), lambda hh, i: (i, hh))
    col_blk = pl.BlockSpec((s, HEAD_SB), lambda hh, i: (0, hh))
    shape = jax.ShapeDtypeStruct((s, h * HEAD_SB), F32)
    return _pcall(
        body, grid=(h, s // tq),
        in_specs=[blk, pl.BlockSpec((s, HEAD_SB), lambda hh, i: (0, h + hh)),
                  pl.BlockSpec((s, HEAD_SB), lambda hh, i: (0, 2 * h + hh)), blk]
        + [pl.BlockSpec((1, 1, s // HEAD_SB, tq, HEAD_SB), lambda hh, i: (hh, i, 0, 0, 0))] * 2
        + [pl.BlockSpec(memory_space=pl.ANY)] * len(follow),
        out_specs=[blk, col_blk, col_blk], out_shape=[shape, shape, shape],
        scratch_shapes=[pltpu.VMEM((tq, HEAD_SB), F32), pltpu.VMEM((tq, HEAD_SB), F32)],
        name="sb_bwd", compiler_params=_params(("parallel", "arbitrary")))(proj, proj, proj, do, a_saved, b_saved, *follow)


def _gm_mask():
    t = lax.broadcasted_iota(jnp.int32, (GROUP_GM, GROUP_GM), 0)
    s = lax.broadcasted_iota(jnp.int32, (GROUP_GM, GROUP_GM), 1)
    shift = CHUNK.bit_length() - 1
    return (s >> shift) <= (t >> shift)


def _gm_fwd(proj, g_vnorm, w_s, b_st, u_blk):
    s = proj.shape[0]
    groups = w_s.shape[0]
    w = groups * GROUP_GM

    def body(u_ref, v_ref, gv_ref, ws_ref, bst_ref, o_ref):
        ug = _gelu(u_ref[...])
        vg = _gelu(v_ref[...])
        vn = vg * lax.rsqrt(jnp.mean(vg * vg, axis=-1, keepdims=True) + EPS) * gv_ref[...]
        vnb = vn.astype(BF16)
        mask = _gm_mask()
        for g in range(groups):
            sl = slice(g * GROUP_GM, (g + 1) * GROUP_GM)
            wm = jnp.where(mask, ws_ref[g], 0.0).astype(BF16)
            mixed = _dot(wm, vnb[:, sl]) + bst_ref[:, g:g + 1]
            o_ref[:, sl] = (ug[:, sl] * mixed).astype(o_ref.dtype)

    return _pcall(
        body, grid=(s // GROUP_GM,),
        in_specs=[pl.BlockSpec((GROUP_GM, w), lambda c: (c, u_blk)), pl.BlockSpec((GROUP_GM, w), lambda c: (c, u_blk + 1)),
                  pl.BlockSpec((1, w), lambda c: (0, 0)), pl.BlockSpec((groups, GROUP_GM, GROUP_GM), lambda c: (0, 0, 0)),
                  pl.BlockSpec((GROUP_GM, groups), lambda c: (0, 0))],
        out_specs=pl.BlockSpec((GROUP_GM, w), lambda c: (c, 0)),
        out_shape=jax.ShapeDtypeStruct((s, w), BF16), name="gm_fwd",
        compiler_params=_params(("parallel",)))(proj, proj, g_vnorm, w_s, b_st)


def _gm_bwd(proj, g_vnorm, w_s, b_st, do, u_blk):
    s = proj.shape[0]
    groups = w_s.shape[0]
    w = groups * GROUP_GM

    def body(u_ref, v_ref, gv_ref, ws_ref, bst_ref, do_ref, du_ref, dv_ref, dgv_ref, dws_ref, dbst_ref, dvn_ref):
        @pl.when(pl.program_id(0) == 0)
        def _():
            dgv_ref[...] = jnp.zeros_like(dgv_ref)
            dws_ref[...] = jnp.zeros_like(dws_ref)
            dbst_ref[...] = jnp.zeros_like(dbst_ref)

        ug, ugrad = _gelu_and_grad(u_ref[...])
        vg, vgrad = _gelu_and_grad(v_ref[...])
        r = lax.rsqrt(jnp.mean(vg * vg, axis=-1, keepdims=True) + EPS)
        n = vg * r
        gv = gv_ref[...]
        vnb = (n * gv).astype(BF16)
        dout = do_ref[...]
        mask = _gm_mask()
        for g in range(groups):
            sl = slice(g * GROUP_GM, (g + 1) * GROUP_GM)
            wm = jnp.where(mask, ws_ref[g], 0.0).astype(BF16)
            mixed = _dot(wm, vnb[:, sl]) + bst_ref[:, g:g + 1]
            dmixed = dout[:, sl] * ug[:, sl]
            du_ref[:, sl] = dout[:, sl] * mixed * ugrad[:, sl]
            dbst_ref[:, g:g + 1] += jnp.sum(dmixed, axis=1, keepdims=True)
            dmb = dmixed.astype(BF16)
            dws_ref[g] += jnp.where(mask, _dot_nt(dmb, vnb[:, sl]), 0.0)
            dvn_ref[:, sl] = _dot_tn(wm, dmb)
        dvn = dvn_ref[...]
        dgv_ref[...] += jnp.sum(dvn * n, axis=0, keepdims=True)
        dn = dvn * gv
        dvg = r * (dn - n * jnp.mean(dn * n, axis=-1, keepdims=True))
        dv_ref[...] = dvg * vgrad

    rowb = pl.BlockSpec((GROUP_GM, w), lambda c: (c, 0))
    vec = pl.BlockSpec((1, w), lambda c: (0, 0))
    wsb = pl.BlockSpec((groups, GROUP_GM, GROUP_GM), lambda c: (0, 0, 0))
    bsb = pl.BlockSpec((GROUP_GM, groups), lambda c: (0, 0))
    return _pcall(
        body, grid=(s // GROUP_GM,),
        in_specs=[pl.BlockSpec((GROUP_GM, w), lambda c: (c, u_blk)), pl.BlockSpec((GROUP_GM, w), lambda c: (c, u_blk + 1)),
                  vec, wsb, bsb, rowb],
        out_specs=[rowb, rowb, vec, wsb, bsb],
        out_shape=[jax.ShapeDtypeStruct((s, w), F32), jax.ShapeDtypeStruct((s, w), F32), jax.ShapeDtypeStruct((1, w), F32),
                   jax.ShapeDtypeStruct((groups, GROUP_GM, GROUP_GM), F32), jax.ShapeDtypeStruct((GROUP_GM, groups), F32)],
        scratch_shapes=[pltpu.VMEM((GROUP_GM, w), F32)], name="gm_bwd",
        compiler_params=_params(("arbitrary",)))(proj, proj, g_vnorm, w_s, b_st, do)


def _xa_fwd(proj, mem_kv, q_blk, n_heads):
    s = proj.shape[0]
    nm = mem_kv.shape[0]
    tq = _tile(s, (512, 256, 128))
    scale = HEAD_XA ** -0.5

    def body(q_ref, k_ref, v_ref, o_ref):
        z = _dot_nt(q_ref[...].astype(BF16), k_ref[...].astype(BF16)) * scale
        z = z - jnp.max(z, axis=-1, keepdims=True)
        e = jnp.exp(z)
        p = e / jnp.sum(e, axis=-1, keepdims=True)
        o_ref[...] = _dot(p.astype(BF16), v_ref[...].astype(BF16)).astype(o_ref.dtype)

    h = n_heads
    return _pcall(
        body, grid=(h, s // tq),
        in_specs=[pl.BlockSpec((tq, HEAD_XA), lambda hh, i: (i, q_blk + hh)),
                  pl.BlockSpec((nm, HEAD_XA), lambda hh, i: (0, hh)), pl.BlockSpec((nm, HEAD_XA), lambda hh, i: (0, h + hh))],
        out_specs=pl.BlockSpec((tq, HEAD_XA), lambda hh, i: (i, hh)),
        out_shape=jax.ShapeDtypeStruct((s, h * HEAD_XA), BF16), name="xa_fwd",
        compiler_params=_params(("parallel", "parallel")))(proj, mem_kv, mem_kv)


def _xa_bwd(proj, mem_kv, do, q_blk, n_heads):
    s = proj.shape[0]
    nm = mem_kv.shape[0]
    tq = _tile(s, (512, 256, 128))
    scale = HEAD_XA ** -0.5
    h = n_heads

    def body(q_ref, k_ref, v_ref, do_ref, dq_ref, dk_ref, dv_ref):
        @pl.when(pl.program_id(1) == 0)
        def _():
            dk_ref[...] = jnp.zeros_like(dk_ref)
            dv_ref[...] = jnp.zeros_like(dv_ref)

        qb = q_ref[...].astype(BF16)
        kb = k_ref[...].astype(BF16)
        vb = v_ref[...].astype(BF16)
        dob = do_ref[...].astype(BF16)
        z = _dot_nt(qb, kb) * scale
        z = z - jnp.max(z, axis=-1, keepdims=True)
        e = jnp.exp(z)
        p = e / jnp.sum(e, axis=-1, keepdims=True)
        dp = _dot_nt(dob, vb)
        dz = (p * (dp - jnp.sum(dp * p, axis=-1, keepdims=True)) * scale).astype(BF16)
        dq_ref[...] = _dot(dz, kb)
        dk_ref[...] += _dot_tn(dz, qb)
        dv_ref[...] += _dot_tn(p.astype(BF16), dob)

    qspec = pl.BlockSpec((tq, HEAD_XA), lambda hh, i: (i, hh))
    dk, dv = None, None
    dq, dk, dv = _pcall(
        body, grid=(h, s // tq),
        in_specs=[pl.BlockSpec((tq, HEAD_XA), lambda hh, i: (i, q_blk + hh)),
                  pl.BlockSpec((nm, HEAD_XA), lambda hh, i: (0, hh)), pl.BlockSpec((nm, HEAD_XA), lambda hh, i: (0, h + hh)),
                  qspec],
        out_specs=[qspec, pl.BlockSpec((nm, HEAD_XA), lambda hh, i: (0, hh)), pl.BlockSpec((nm, HEAD_XA), lambda hh, i: (0, hh))],
        out_shape=[jax.ShapeDtypeStruct((s, h * HEAD_XA), F32), jax.ShapeDtypeStruct((nm, h * HEAD_XA), F32),
                   jax.ShapeDtypeStruct((nm, h * HEAD_XA), F32)],
        name="xa_bwd", compiler_params=_params(("parallel", "arbitrary")))(proj, mem_kv, mem_kv, do)
    return dq, dk, dv


def _merge_fwd(zg, b_gate, branches):
    s, d = branches[0].shape
    tr = _tile(s, (128,))

    def body(z0, z1, z2, g0, g1, g2, b0, b1, b2, o_ref):
        acc = None
        for z, g, b in ((z0, g0, b0), (z1, g1, b1), (z2, g2, b2)):
            term = jax.nn.sigmoid(z[...].astype(F32) + g[...]) * b[...]
            acc = term if acc is None else acc + term
        o_ref[...] = acc.astype(o_ref.dtype)

    zs = [pl.BlockSpec((tr, d), functools.partial(lambda i, k: (i, k), k=k)) for k in range(3)]
    gs = [pl.BlockSpec((1, d), functools.partial(lambda i, k: (0, k), k=k)) for k in range(3)]
    row = pl.BlockSpec((tr, d), lambda i: (i, 0))
    return _pcall(
        body, grid=(s // tr,), in_specs=zs + gs + [row] * 3, out_specs=row,
        out_shape=jax.ShapeDtypeStruct((s, d), BF16), name="merge_fwd",
        compiler_params=_params(("parallel",)))(zg, zg, zg, b_gate, b_gate, b_gate, *branches)


def _merge_bwd(zg, b_gate, branches, dmerged):
    s, d = branches[0].shape
    tr = _tile(s, (128,))

    def body(z0, z1, z2, g0, g1, g2, b0, b1, b2, dm_ref, dz_ref, d0, d1, d2, dbg_ref):
        @pl.when(pl.program_id(0) == 0)
        def _():
            dbg_ref[...] = jnp.zeros_like(dbg_ref)

        dm = dm_ref[...]
        for k, (z, g, b, dbr) in enumerate(((z0, g0, b0, d0), (z1, g1, b1, d1), (z2, g2, b2, d2))):
            sg = jax.nn.sigmoid(z[...].astype(F32) + g[...])
            dbr[...] = (dm * sg).astype(dbr.dtype)
            dz = dm * b[...] * sg * (1.0 - sg)
            dz_ref[:, k * d:(k + 1) * d] = dz.astype(dz_ref.dtype)
            dbg_ref[:, k * d:(k + 1) * d] += jnp.sum(dz, axis=0, keepdims=True)

    zs = [pl.BlockSpec((tr, d), functools.partial(lambda i, k: (i, k), k=k)) for k in range(3)]
    gs = [pl.BlockSpec((1, d), functools.partial(lambda i, k: (0, k), k=k)) for k in range(3)]
    row = pl.BlockSpec((tr, d), lambda i: (i, 0))
    outs = _pcall(
        body, grid=(s // tr,), in_specs=zs + gs + [row] * 4,
        out_specs=[pl.BlockSpec((tr, 3 * d), lambda i: (i, 0)), row, row, row, pl.BlockSpec((1, 3 * d), lambda i: (0, 0))],
        out_shape=[jax.ShapeDtypeStruct((s, 3 * d), BF16)] + [jax.ShapeDtypeStruct((s, d), BF16)] * 3
        + [jax.ShapeDtypeStruct((1, 3 * d), F32)],
        name="merge_bwd", compiler_params=_params(("arbitrary",)))(zg, zg, zg, b_gate, b_gate, b_gate, *branches, dmerged)
    return outs[0], list(outs[1:4]), outs[4]


def _shift_down(x, k, row):
    return jnp.where(row >= k, pltpu.roll(x, k, 0), 0.0)


def _shift_up(x, k, row, s):
    return jnp.where(row < s - k, pltpu.roll(x, s - k, 0), 0.0)


def _conv_pre(gate, cw_ref, cb_ref, row):
    conv = cb_ref[...] + cw_ref[CONV_TAPS - 1:CONV_TAPS, :] * gate
    for k in range(1, CONV_TAPS):
        conv = conv + cw_ref[CONV_TAPS - 1 - k:CONV_TAPS - k, :] * _shift_down(gate, k, row)
    return conv


def _cg_fwd(up, conv_w, conv_b):
    s = up.shape[0]
    f = conv_w.shape[1]
    tc = _tile(f, (256, 128))
    nb = f // tc

    def body(g_ref, v_ref, cw_ref, cb_ref, o_ref, gel_ref, ggrad_ref):
        row = lax.broadcasted_iota(jnp.int32, (s, tc), 0)
        conv = _conv_pre(g_ref[...].astype(F32), cw_ref, cb_ref, row)
        gel, ggrad = _gelu_and_grad(conv)
        o_ref[...] = (gel * v_ref[...].astype(F32)).astype(o_ref.dtype)
        gel_ref[...] = gel.astype(gel_ref.dtype)
        ggrad_ref[...] = ggrad.astype(ggrad_ref.dtype)

    colb = pl.BlockSpec((s, tc), lambda j: (0, j))
    shape = jax.ShapeDtypeStruct((s, f), BF16)
    return _pcall(
        body, grid=(nb,),
        in_specs=[colb, pl.BlockSpec((s, tc), lambda j: (0, nb + j)),
                  pl.BlockSpec((CONV_TAPS, tc), lambda j: (0, j)), pl.BlockSpec((1, tc), lambda j: (0, j))],
        out_specs=[colb, colb, colb], out_shape=[shape, shape, shape], name="cg_fwd",
        compiler_params=_params(("parallel",)))(up, up, conv_w, conv_b)


def _cg_bwd(up, gel, ggrad, conv_w, dact):
    s = up.shape[0]
    f = conv_w.shape[1]
    tc = _tile(f, (256, 128))
    nb = f // tc

    def body(g_ref, v_ref, gel_ref, gg_ref, cw_ref, da_ref, dg_ref, dv_ref, dcw_ref, dcb_ref):
        row = lax.broadcasted_iota(jnp.int32, (s, tc), 0)
        gate = g_ref[...].astype(F32)
        da = da_ref[...]
        dv_ref[...] = (da * gel_ref[...].astype(F32)).astype(dv_ref.dtype)
        dconv = da * v_ref[...].astype(F32) * gg_ref[...].astype(F32)
        dgate = cw_ref[CONV_TAPS - 1:CONV_TAPS, :] * dconv
        dcw_ref[CONV_TAPS - 1:CONV_TAPS, :] = jnp.sum(dconv * gate, axis=0, keepdims=True)
        for k in range(1, CONV_TAPS):
            dgate = dgate + cw_ref[CONV_TAPS - 1 - k:CONV_TAPS - k, :] * _shift_up(dconv, k, row, s)
            dcw_ref[CONV_TAPS - 1 - k:CONV_TAPS - k, :] = jnp.sum(dconv * _shift_down(gate, k, row), axis=0, keepdims=True)
        dg_ref[...] = dgate.astype(dg_ref.dtype)
        dcb_ref[...] = jnp.sum(dconv, axis=0, keepdims=True)

    colb = pl.BlockSpec((s, tc), lambda j: (0, j))
    return _pcall(
        body, grid=(nb,),
        in_specs=[colb, pl.BlockSpec((s, tc), lambda j: (0, nb + j)), colb, colb,
                  pl.BlockSpec((CONV_TAPS, tc), lambda j: (0, j)), colb],
        out_specs=[colb, colb, pl.BlockSpec((CONV_TAPS, tc), lambda j: (0, j)), pl.BlockSpec((1, tc), lambda j: (0, j))],
        out_shape=[jax.ShapeDtypeStruct((s, f), BF16), jax.ShapeDtypeStruct((s, f), BF16),
                   jax.ShapeDtypeStruct((CONV_TAPS, f), F32), jax.ShapeDtypeStruct((1, f), F32)],
        name="cg_bwd", compiler_params=_params(("parallel",)))(up, up, gel, ggrad, conv_w, dact)


def _row_tile(rows, cols, elems=256 * 1024):
    want = max(16, elems // cols)
    for c in (512, 256, 128, 64, 32, 16):
        if c <= want and rows % c == 0:
            return c
    return rows


def _sum_halves(dwv, recv, core, name):
    nj, _, a, c = dwv.shape
    tr = _row_tile(a, c, 1024 * 1024)

    def body(core_ref, d_ref, r_ref, o_ref):
        o_ref[0] = (d_ref[0, 0].astype(F32) + r_ref[0].astype(F32)).astype(o_ref.dtype)

    grid_spec = pltpu.PrefetchScalarGridSpec(
        num_scalar_prefetch=1, grid=(nj, a // tr),
        in_specs=[pl.BlockSpec((1, 1, tr, c), lambda j, i, cr: (j, cr[0], i, 0)),
                  pl.BlockSpec((1, tr, c), lambda j, i, cr: (j, i, 0))],
        out_specs=pl.BlockSpec((1, tr, c), lambda j, i, cr: (j, i, 0)))
    return _pcall(body, grid_spec=grid_spec, out_shape=jax.ShapeDtypeStruct((nj, a, c), BF16), name=name,
                  compiler_params=_params(("parallel", "parallel")))(core, dwv, recv)


def _sum_chips(recv, own, chip, core, ax):
    _, a, b = recv.shape
    tr = _row_tile(a, b, 512 * 1024)

    def body(chip_ref, core_ref, r_ref, own_ref, o_ref):
        me = chip_ref[0]
        mine = own_ref[0].astype(F32)
        acc = None
        for k in range(N_CHIPS):
            term = jnp.where(me == k, mine, r_ref[k].astype(F32))
            acc = term if acc is None else acc + term
        o_ref[0] = acc

    own_spec = (pl.BlockSpec((1, tr, b), lambda i, ch, co: (0, i, ch[0])) if ax == 2
                else pl.BlockSpec((1, tr, b), lambda i, ch, co: (ch[0], i, 0)))
    grid_spec = pltpu.PrefetchScalarGridSpec(
        num_scalar_prefetch=2, grid=(a // tr,),
        in_specs=[pl.BlockSpec((N_CHIPS, tr, b), lambda i, ch, co: (0, i, 0)), own_spec],
        out_specs=pl.BlockSpec((1, tr, b), lambda i, ch, co: (co[0], i, 0)))
    return _pcall(body, grid_spec=grid_spec, out_shape=jax.ShapeDtypeStruct((2, a, b), F32),
                  name="sum_chips", compiler_params=_params(("parallel",)))(chip, core, recv, own)


def _place_own(wt, layer, chip, ax, after):
    nl, r, c = wt.shape
    half = r // 2
    tr = _row_tile(half, c, 512 * 1024)
    nb = half // tr

    def body(chip_ref, w_ref, after_ref, o_ref):
        o_ref[...] = w_ref[...].astype(BF16).reshape(o_ref.shape)

    if ax == 2:
        out_spec = pl.BlockSpec((1, tr, c), lambda h, i, ch: (h, i, ch[0]))
    else:
        out_spec = pl.BlockSpec((1, 1, tr, c), lambda h, i, ch: (ch[0], h, i, 0))
    grid_spec = pltpu.PrefetchScalarGridSpec(
        num_scalar_prefetch=1, grid=(2, nb),
        in_specs=[pl.BlockSpec((1, tr, c), lambda h, i, ch: (layer, h * nb + i, 0)), pl.BlockSpec(memory_space=pl.ANY)],
        out_specs=out_spec)
    return _pcall(body, grid_spec=grid_spec, out_shape=jax.ShapeDtypeStruct(_full_view_shape(wt.shape, ax), BF16),
                  name="place_own", compiler_params=_params(("parallel", "parallel")))(chip, wt, after)


def _adamw(w, m, v, g, layer, prev, name, after=None):
    nl, r, c = w.shape
    tr = _row_tile(r, c, 512 * 1024)
    c1 = 1.0 - ADAM_B1 ** ADAM_STEP
    c2 = 1.0 - ADAM_B2 ** ADAM_STEP

    follow = [] if after is None else [after]

    def body(w_ref, m_ref, v_ref, gin_ref, *rest):
        g_ref, d_ref, nm_ref, nv_ref = rest[-4:]
        g = gin_ref[...]
        mm = ADAM_B1 * m_ref[0] + (1.0 - ADAM_B1) * g
        vv = ADAM_B2 * v_ref[0] + (1.0 - ADAM_B2) * (g * g)
        g_ref[0] = g
        nm_ref[0] = mm
        nv_ref[0] = vv
        d_ref[0] = -ADAM_LR * ((mm / c1) / (jnp.sqrt(vv / c2) + ADAM_EPS) + ADAM_WD * w_ref[0])

    blk = pl.BlockSpec((1, tr, c), lambda i: (layer, i, 0))
    shape = jax.ShapeDtypeStruct((nl, r, c), F32)
    extra = [] if prev is None else list(prev)
    return _pcall(
        body, grid=(r // tr,),
        in_specs=[blk] * 3 + [pl.BlockSpec((tr, c), lambda i: (i, 0))] + [pl.BlockSpec(memory_space=pl.ANY)] * (len(extra) + len(follow)),
        out_specs=[blk] * 4, out_shape=[shape] * 4, input_output_aliases={4 + k: k for k in range(len(extra))}, name=name,
        compiler_params=_params(("parallel",)))(w, m, v, g, *extra, *follow)


HBM_SPEC = pl.BlockSpec(memory_space=pltpu.HBM)
COMM = pltpu.CompilerParams(has_side_effects=True)


def _position():
    x, y, c = lax.axis_index("x"), lax.axis_index("y"), lax.axis_index("c")
    chips = [(1 - x, y), (x, 1 - y), (1 - x, 1 - y)]
    return x, y, c, chips


def _remote(src, dst, send_sem, recv_sem, dev):
    return pltpu.make_async_remote_copy(src_ref=src, dst_ref=dst, send_sem=send_sem, recv_sem=recv_sem,
                                        device_id=dev, device_id_type=MESH)


def _full_view_shape(shard_shape, ax):
    _, r, c = shard_shape
    return (2, r // 2, c * N_CHIPS) if ax == 2 else (N_CHIPS, 2, r // 2, c)


def _piece(ref, ax, j, h, cs):
    if ax == 2:
        return ref.at[h, :, pl.ds(pl.multiple_of(j * cs, cs), cs)]
    return ref.at[j, h]


def _chip_block(ref, ax, j, cs):
    if ax == 2:
        return ref.at[:, :, pl.ds(pl.multiple_of(j * cs, cs), cs)]
    return ref.at[j]


SEM_SPEC = pl.BlockSpec(memory_space=pltpu.SEMAPHORE)
ANY_SPEC = pl.BlockSpec(memory_space=pl.ANY)
SPLIT = pltpu.CompilerParams(has_side_effects=pltpu.SideEffectType.DATAFLOW_SIDE_EFFECTING)


def _exchange(kind, name, bufs, build, n_sems, after=None, sems=None):
    n = len(bufs)
    if kind == 'sync':
        def body(*refs):
            mine, theirs = build(refs[n:2 * n], refs[2 * n], refs[2 * n + 1])
            for cp in mine:
                cp.start()
            for cp in theirs:
                cp.wait_recv()
            for cp in mine:
                cp.wait_send()

        return list(_pcall(
            body, in_specs=[HBM_SPEC] * n, out_specs=[HBM_SPEC] * n,
            out_shape=[jax.ShapeDtypeStruct(v.shape, v.dtype) for v in bufs], input_output_aliases={t: t for t in range(n)},
            scratch_shapes=[pltpu.SemaphoreType.DMA((n_sems,)), pltpu.SemaphoreType.DMA((n_sems,))],
            name=name, compiler_params=COMM)(*bufs))
    if kind == 'start':
        def body(*refs):
            mine, _ = build(refs[n + 3:2 * n + 3], refs[n + 1], refs[n + 2])
            for cp in mine:
                cp.start()
            refs[2 * n + 3][...] = jnp.zeros_like(refs[2 * n + 3])

        outs = _pcall(
            body, in_specs=[HBM_SPEC] * n + [ANY_SPEC],
            out_specs=[SEM_SPEC, SEM_SPEC] + [HBM_SPEC] * n + [pl.BlockSpec(memory_space=pltpu.VMEM)],
            out_shape=[pltpu.SemaphoreType.DMA((n_sems,)), pltpu.SemaphoreType.DMA((n_sems,))]
            + [pltpu.HBM(v.shape, v.dtype) for v in bufs] + [jax.ShapeDtypeStruct((8, LANES), F32)],
            input_output_aliases={t: 2 + t for t in range(n)}, name=name,
            compiler_params=SPLIT)(*[pltpu.with_memory_space_constraint(v, pltpu.HBM) for v in bufs], after)
        return (outs[0], outs[1]), list(outs[2:2 + n]), outs[2 + n]

    def body(*refs):
        mine, theirs = build(refs[:n], refs[n], refs[n + 1])
        for cp in mine:
            cp.wait_send()
        for cp in theirs:
            cp.wait_recv()

    return list(_pcall(
        body, in_specs=[HBM_SPEC] * n + [SEM_SPEC, SEM_SPEC, ANY_SPEC], out_specs=[HBM_SPEC] * n,
        out_shape=[pltpu.HBM(v.shape, v.dtype) for v in bufs], input_output_aliases={t: t for t in range(n)},
        name=name, compiler_params=SPLIT)(*bufs, sems[0], sems[1], after))


def _gather_ici_copies(axes, shard_cols):
    def build(bufs, send_sems, recv_sems):
        x, y, c, chips = _position()
        me = 2 * x + y
        mine, theirs = [], []
        for t, ax in enumerate(axes):
            own = _piece(bufs[t], ax, me, c, shard_cols[t])
            for p, (px, py) in enumerate(chips):
                k = t * 3 + p
                got = _piece(bufs[t], ax, 2 * px + py, c, shard_cols[t])
                mine.append(_remote(own, own, send_sems.at[k], recv_sems.at[k], (px, py, c)))
                theirs.append(_remote(got, got, send_sems.at[k], recv_sems.at[k], (px, py, c)))
        return mine, theirs
    return build


def _gather_d2d_copies(axes, shard_cols):
    def build(bufs, send_sems, recv_sems):
        x, y, c, chips = _position()
        mine, theirs = [], []
        for t, ax in enumerate(axes):
            for p, (px, py) in enumerate(chips):
                k = t * 3 + p
                had = _piece(bufs[t], ax, 2 * px + py, c, shard_cols[t])
                got = _piece(bufs[t], ax, 2 * px + py, 1 - c, shard_cols[t])
                mine.append(_remote(had, had, send_sems.at[k], recv_sems.at[k], (x, y, 1 - c)))
                theirs.append(_remote(got, got, send_sems.at[k], recv_sems.at[k], (x, y, 1 - c)))
        return mine, theirs
    return build


def _grads_d2d_copies(n):
    def build(bufs, send_sems, recv_sems):
        x, y, c, _ = _position()
        mine = [_remote(bufs[t].at[:, 1 - c], bufs[n + t], send_sems.at[t], recv_sems.at[t], (x, y, 1 - c)) for t in range(n)]
        return mine, mine
    return build


def _grads_ici_copies(axes):
    n = len(axes)

    def build(bufs, send_sems, recv_sems):
        x, y, c, chips = _position()
        me = 2 * x + y

        def block(t, j):
            if axes[t] == 2:
                cs = bufs[n + t].shape[2]
                return bufs[t].at[0, :, pl.ds(pl.multiple_of(j * cs, cs), cs)]
            return bufs[t].at[j]

        mine, theirs = [], []
        for t in range(n):
            for p, (px, py) in enumerate(chips):
                k = t * 3 + p
                peer = 2 * px + py
                mine.append(_remote(block(t, peer), bufs[n + t].at[me], send_sems.at[k], recv_sems.at[k], (px, py, c)))
                theirs.append(_remote(block(t, peer), bufs[n + t].at[peer], send_sems.at[k], recv_sems.at[k], (px, py, c)))
        return mine, theirs
    return build


def _join_copies(bufs, send_sems, recv_sems):
    x, y, c, _ = _position()
    mine = [_remote(b.at[c], b.at[c], send_sems.at[t], recv_sems.at[t], (x, y, 1 - c)) for t, b in enumerate(bufs)]
    theirs = [_remote(b.at[1 - c], b.at[1 - c], send_sems.at[t], recv_sems.at[t], (x, y, 1 - c)) for t, b in enumerate(bufs)]
    return mine, theirs


def _grads_recv_shape(sm, ax):
    _, a, c = sm.shape
    return (N_CHIPS, a, c // N_CHIPS if ax == 2 else c)


def _gather_small(shard):
    nl, r, cs = shard.shape

    def body(in_ref, out_ref, send_sems, recv_sems, local_sem):
        x, y, c, chips = _position()

        def cols(j):
            return out_ref.at[:, :, pl.ds(pl.multiple_of(j * cs, cs), cs)]

        me = 2 * x + y
        loc = pltpu.make_async_copy(in_ref, cols(me), local_sem)
        loc.start()
        remote = [_remote(in_ref, cols(me), send_sems.at[p], recv_sems.at[p], (px, py, c)) for p, (px, py) in enumerate(chips)]
        for cp in remote:
            cp.start()
        for p, (px, py) in enumerate(chips):
            _remote(in_ref, cols(2 * px + py), send_sems.at[p], recv_sems.at[p], (px, py, c)).wait_recv()
        for cp in remote:
            cp.wait_send()
        loc.wait()

    return _pcall(
        body, in_specs=[HBM_SPEC], out_specs=HBM_SPEC, out_shape=jax.ShapeDtypeStruct((nl, r, cs * N_CHIPS), shard.dtype),
        scratch_shapes=[pltpu.SemaphoreType.DMA((3,)), pltpu.SemaphoreType.DMA((3,)), pltpu.SemaphoreType.DMA(())],
        name="gather_small", compiler_params=COMM)(shard)


def _small_copies(bufs, send_sems, recv_sems):
    pack, land = bufs
    x, y, cc, _ = _position()
    me = 4 * x + 2 * y + cc
    mine, theirs = [], []
    for k in range(1, N_DEV):
        px, py, pc = x ^ ((k >> 2) & 1), y ^ ((k >> 1) & 1), cc ^ (k & 1)
        mine.append(_remote(pack, land.at[me], send_sems.at[k - 1], recv_sems.at[k - 1], (px, py, pc)))
        theirs.append(_remote(pack, land.at[4 * px + 2 * py + pc], send_sems.at[k - 1], recv_sems.at[k - 1], (px, py, pc)))
    return mine, theirs


def _sum_devices(land, pack, dev):
    _, r, c = land.shape
    tr = _tile(r, (672, 512, 256, 128, 64, 8))

    def body(dev_ref, l_ref, p_ref, o_ref):
        me = dev_ref[0]
        acc = None
        for k in range(N_DEV):
            term = jnp.where(me == k, p_ref[...], l_ref[k])
            acc = term if acc is None else acc + term
        o_ref[...] = acc

    grid_spec = pltpu.PrefetchScalarGridSpec(
        num_scalar_prefetch=1, grid=(r // tr,),
        in_specs=[pl.BlockSpec((N_DEV, tr, c), lambda i, dv: (0, i, 0)), pl.BlockSpec((tr, c), lambda i, dv: (i, 0))],
        out_specs=pl.BlockSpec((tr, c), lambda i, dv: (i, 0)))
    return _pcall(body, grid_spec=grid_spec, out_shape=jax.ShapeDtypeStruct((r, c), F32), name="sum_devices",
                  compiler_params=_params(("parallel",)))(dev, land, pack)


def _dims(d):
    half = d // 2
    return half // HEAD_SB, half // HEAD_XA, 3, (5 * half) // HEAD_XA


def _layer_fwd(x, mem, weight, small, l, after=None):
    h_sb, h_xa, u_blk, q_blk = _dims(x.shape[1])

    def vec(name):
        return small[name][l].reshape(1, -1)

    def use(a, name, follows, mm_name, dtype=F32):
        wt = weight(name, follows)
        return _mm(a, wt, 'nn', dtype, mm_name, after=weight.token())

    h1 = _norm_fwd(x, vec('g_mix_pre'), None, BF16, "norm_mix_pre", after)
    proj = use(h1, 'w_in', h1, "mm_proj")
    o_sb, a_sb, b_sb = _sb_fwd(proj, h_sb)
    b_st = small['b_s'][l].T
    o_gm = _gm_fwd(proj, vec('g_vnorm'), small['w_s'][l], b_st, u_blk)
    memn = _norm_fwd(mem, vec('g_mem'), None, BF16, "norm_mem")
    mem_kv = use(memn, 'w_mem_kv', o_sb, "mm_mem_kv")
    o_xa = _xa_fwd(proj, mem_kv, q_blk, h_xa)
    zg = use(h1, 'w_gate', o_sb, "mm_gate", BF16)
    branches = [use(o, wn, zg, "mm_branch") for o, wn in ((o_sb, 'w_br_sb'), (o_gm, 'w_br_gm'), (o_xa, 'w_br_xa'))]
    merged = _merge_fwd(zg, vec('b_gate'), branches)
    y1 = use(merged, 'w_out', zg, "mm_out")
    x1 = _norm_fwd(y1, vec('g_mix_post'), x, F32, "norm_mix_post")
    h2 = _norm_fwd(x1, vec('g_ffn_pre'), None, BF16, "norm_ffn_pre")
    up = use(h2, 'w_up', h2, "mm_up", BF16)
    act, gel, ggrad = _cg_fwd(up, weight('conv_w', up), vec('conv_b'))
    y2 = use(act, 'w_down', act, "mm_down")
    x2 = _norm_fwd(y2, vec('g_ffn_post'), x1, F32, "norm_ffn_post")
    saved = dict(x0=x, h1=h1, proj=proj, o_sb=o_sb, a_sb=a_sb, b_sb=b_sb, o_gm=o_gm, o_xa=o_xa, memn=memn, mem_kv=mem_kv, zg=zg,
                 branches=branches, merged=merged, y1=y1, x1=x1, h2=h2, up=up, act=act, gel=gel, ggrad=ggrad, y2=y2, b_st=b_st)
    return x2, saved


def _layer_bwd(dx, mem, sv, full, small, l, after, emit):
    h_sb, h_xa, u_blk, q_blk = _dims(dx.shape[1])

    def vec(name):
        return small[name][l].reshape(1, -1)

    gb, gs = {}, {}
    dy2, gs['g_ffn_post'] = _norm_bwd(sv['y2'], vec('g_ffn_post'), [dx], None, BF16, "norm_ffn_post_bwd", after)
    gb['w_down'] = _mm(sv['act'], dy2, 'tn', BF16, "mm_down_dw")
    dact = _mm(dy2, full['w_down'], 'nt', F32, "mm_down_dx")
    dgate, dval, gs['conv_w'], gs['conv_b'] = _cg_bwd(sv['up'], sv['gel'], sv['ggrad'], full['conv_w'], dact)
    gb['w_up'] = _mm(sv['h2'], [dgate, dval], 'tn', BF16, "mm_up_dw")
    token = emit(0, gb)
    dh2 = _mm([dgate, dval], full['w_up'], 'nt', F32, "mm_up_dx", after=token)
    token = emit.flush(dh2)
    dx1, dy1, gs['g_ffn_pre'], gs['g_mix_post'] = _norm_pair_bwd(
        sv['x1'], vec('g_ffn_pre'), dh2, dx, sv['y1'], vec('g_mix_post'), "norm_pair_bwd", token)
    gb['w_out'] = _mm(sv['merged'], dy1, 'tn', BF16, "mm_out_dw")
    dmerged = _mm(dy1, full['w_out'], 'nt', F32, "mm_out_dx")
    dzg, dbr, gs['b_gate'] = _merge_bwd(sv['zg'], vec('b_gate'), sv['branches'], dmerged)
    douts = []
    for o, db, wn in ((sv['o_sb'], dbr[0], 'w_br_sb'), (sv['o_gm'], dbr[1], 'w_br_gm'), (sv['o_xa'], dbr[2], 'w_br_xa')):
        gb[wn] = _mm(o, db, 'tn', BF16, "mm_branch_dw")
        douts.append(_mm(db, full[wn], 'nt', F32, "mm_branch_dx"))
    gb['w_gate'] = _mm(sv['h1'], dzg, 'tn', BF16, "mm_gate_dw")
    dq_xa, dk_xa, dv_xa = _xa_bwd(sv['proj'], sv['mem_kv'], douts[2], q_blk, h_xa)
    dmem_kv = jnp.concatenate([dk_xa, dv_xa], axis=1).astype(BF16)
    gb['w_mem_kv'] = _mm(sv['memn'], dmem_kv, 'tn', BF16, "mm_mem_kv_dw")
    token = emit(1, gb)
    dh1_gate = _mm(dzg, full['w_gate'], 'nt', F32, "mm_gate_dx", after=token)
    token = emit.flush(dh1_gate)
    dmemn = _mm(dmem_kv, full['w_mem_kv'], 'nt', F32, "mm_mem_kv_dx")
    _, gs['g_mem'] = _norm_bwd(mem, vec('g_mem'), [dmemn], None, BF16, "norm_mem_bwd")
    du, dv, gs['g_vnorm'], gs['w_s'], db_st = _gm_bwd(sv['proj'], vec('g_vnorm'), small['w_s'][l], sv['b_st'], douts[1], u_blk)
    gs['b_s'] = db_st.T
    dq, dk, dvv = _sb_bwd(sv['proj'], sv['a_sb'], sv['b_sb'], douts[0], h_sb, token)
    dproj = jnp.concatenate([dq, dk, dvv, du, dv, dq_xa], axis=1).astype(BF16)
    gb['w_in'] = _mm(sv['h1'], dproj, 'tn', BF16, "mm_proj_dw")
    token = emit(2, gb)
    dh1_proj = _mm(dproj, full['w_in'], 'nt', F32, "mm_proj_dx")
    dx0, gs['g_mix_pre'] = _norm_bwd(sv['x0'], vec('g_mix_pre'), [dh1_gate, dh1_proj], dx1, F32, "norm_mix_pre_bwd", token)
    return dx0, gs


class _Given:
    def __init__(self, full):
        self.full = full

    def __call__(self, name, follows):
        return self.full[name]

    def token(self):
        return None


def _local_step(x, mem, target, full, small):
    n_layers = len(full['w_in'])
    saved = []
    for l in range(n_layers):
        x, sv = _layer_fwd(x, mem, _Given({n: full[n][l] for n in full}), small, l)
        saved.append(sv)
    sq, dx = _loss_head(x, target)
    gbig = {n: [None] * n_layers for n in BIG}
    gsmall = {n: [None] * n_layers for n in SMALL + ['conv_w']}
    class Collect:
        def __init__(self, l):
            self.l = l

        def __call__(self, g, gb):
            for n in BWD_GROUPS[g]:
                gbig[n][self.l] = gb[n]

        def flush(self, follows):
            return None

    for l in reversed(range(n_layers)):
        dx, gs = _layer_bwd(dx, mem, saved[l], {n: full[n][l] for n in full}, small, l, None, Collect(l))
        for n in gs:
            gsmall[n][l] = gs[n]
    return sq, dx, gbig, gsmall


def _pack(arrays, rows_multiple):
    flat = jnp.concatenate([a.reshape(-1).astype(F32) for a in arrays])
    rows = -(-flat.shape[0] // LANES)
    rows = -(-rows // rows_multiple) * rows_multiple
    return jnp.pad(flat, (0, rows * LANES - flat.shape[0])).reshape(rows, LANES)


def _unpack(pack, like):
    flat = pack.reshape(-1)
    out, off = [], 0
    for a in like:
        out.append(flat[off:off + a.size].reshape(a.shape))
        off += a.size
    return out


def _grad_view(g, ax):
    r, c = g.shape
    return g.reshape(1, 2, r // 2, c) if ax == 2 else g.reshape(N_CHIPS, 2, r // (2 * N_CHIPS), c)


def kernel(x, mem, g_mix_pre, w_in, g_vnorm, w_s, b_s, g_mem, w_mem_kv, w_gate, b_gate, w_br_sb, w_br_gm, w_br_xa, w_out, g_mix_post, g_ffn_pre, w_up, conv_w, conv_b, w_down, g_ffn_post, loss_target, m_g_mix_pre, m_w_in, m_g_vnorm, m_w_s, m_b_s, m_g_mem, m_w_mem_kv, m_w_gate, m_b_gate, m_w_br_sb, m_w_br_gm, m_w_br_xa, m_w_out, m_g_mix_post, m_g_ffn_pre, m_w_up, m_conv_w, m_conv_b, m_w_down, m_g_ffn_post, v_g_mix_pre, v_w_in, v_g_vnorm, v_w_s, v_b_s, v_g_mem, v_w_mem_kv, v_w_gate, v_b_gate, v_w_br_sb, v_w_br_gm, v_w_br_xa, v_w_out, v_g_mix_post, v_g_ffn_pre, v_w_up, v_conv_w, v_conv_b, v_w_down, v_g_ffn_post):
    w = dict(g_mix_pre=g_mix_pre, w_in=w_in, g_vnorm=g_vnorm, w_s=w_s, b_s=b_s, g_mem=g_mem, w_mem_kv=w_mem_kv,
             w_gate=w_gate, b_gate=b_gate, w_br_sb=w_br_sb, w_br_gm=w_br_gm, w_br_xa=w_br_xa, w_out=w_out,
             g_mix_post=g_mix_post, g_ffn_pre=g_ffn_pre, w_up=w_up, conv_w=conv_w, conv_b=conv_b, w_down=w_down,
             g_ffn_post=g_ffn_post)
    m = dict(g_mix_pre=m_g_mix_pre, w_in=m_w_in, g_vnorm=m_g_vnorm, w_s=m_w_s, b_s=m_b_s, g_mem=m_g_mem,
             w_mem_kv=m_w_mem_kv, w_gate=m_w_gate, b_gate=m_b_gate, w_br_sb=m_w_br_sb, w_br_gm=m_w_br_gm,
             w_br_xa=m_w_br_xa, w_out=m_w_out, g_mix_post=m_g_mix_post, g_ffn_pre=m_g_ffn_pre, w_up=m_w_up,
             conv_w=m_conv_w, conv_b=m_conv_b, w_down=m_w_down, g_ffn_post=m_g_ffn_post)
    v = dict(g_mix_pre=v_g_mix_pre, w_in=v_w_in, g_vnorm=v_g_vnorm, w_s=v_w_s, b_s=v_b_s, g_mem=v_g_mem,
             w_mem_kv=v_w_mem_kv, w_gate=v_w_gate, b_gate=v_b_gate, w_br_sb=v_w_br_sb, w_br_gm=v_w_br_gm,
             w_br_xa=v_w_br_xa, w_out=v_w_out, g_mix_post=v_g_mix_post, g_ffn_pre=v_g_ffn_pre, w_up=v_w_up,
             conv_w=v_conv_w, conv_b=v_conv_b, w_down=v_w_down, g_ffn_post=v_g_ffn_post)
    n_layers = w_in.shape[0]
    d = x.shape[-1]
    core = lax.axis_index("c").astype(jnp.int32).reshape(1)
    chip = (2 * lax.axis_index("x") + lax.axis_index("y")).astype(jnp.int32).reshape(1)
    small = {n: w[n] for n in SMALL}
    xs, mems, target = x[0], mem[0], loss_target[0]

    conv_w_full = _gather_small(conv_w)

    def as_full(vw, ax):
        return vw.reshape(-1, vw.shape[-1]) if ax == 1 else vw.reshape(vw.shape[0] * vw.shape[1], vw.shape[2])

    stages = {}
    keys = [(l, g) for l in range(n_layers) for g in range(len(FWD_GROUPS))]

    def start_gathers(l, token):
        for g, names in enumerate(FWD_GROUPS):
            ax_g = [BIG_AXIS[n] for n in names]
            cols_g = [w[n].shape[2] for n in names]
            views = [_place_own(w[n], l, chip, ax, token) for n, ax in zip(names, ax_g)]
            sems, views, token = _exchange('start', f"gather_ici_start_{l}_{g}", views, _gather_ici_copies(ax_g, cols_g),
                                           3 * len(names), after=token)
            stages[l, g] = dict(names=names, ax=ax_g, cols=cols_g, views=views, ici=sems, d2d=None, full=None)
        return token

    token = start_gathers(0, conv_w_full)

    def cross_cores(key, follows):
        st, (l, g) = stages[key], key
        n3 = 3 * len(st['names'])
        views = _exchange('wait', f"gather_ici_wait_{l}_{g}", st['views'], _gather_ici_copies(st['ax'], st['cols']), n3,
                          after=follows, sems=st['ici'])
        st['d2d'], st['views'], tok = _exchange('start', f"gather_d2d_start_{l}_{g}", views,
                                                _gather_d2d_copies(st['ax'], st['cols']), n3, after=core)
        return tok

    class Weights:
        def __init__(self, l):
            self.l, self.tok = l, None

        def __call__(self, name, follows):
            if name == 'conv_w':
                return conv_w_full[self.l]
            key = (self.l, [g for g, names in enumerate(FWD_GROUPS) if name in names][0])
            if key[1] == 1 and (self.l + 1, 0) not in stages and self.l + 1 < n_layers:
                self.tok = start_gathers(self.l + 1, follows)
            st = stages[key]
            if st['full'] is None:
                if st['d2d'] is None:
                    cross_cores(key, follows)
                views = _exchange('wait', f"gather_d2d_wait_{key[0]}_{key[1]}", st['views'],
                                  _gather_d2d_copies(st['ax'], st['cols']), 3 * len(st['names']), after=follows, sems=st['d2d'])
                st['full'] = {n: as_full(vw, ax) for n, vw, ax in zip(st['names'], views, st['ax'])}
            nxt = keys.index(key) + 1
            if name == st['names'][-1] and nxt < len(keys) and key != keys[0] and stages[keys[nxt]]['d2d'] is None:
                self.tok = cross_cores(keys[nxt], follows)
            return st['full'][name]

        def token(self):
            return self.tok

    fulls, saved = [], []
    for l in range(n_layers):
        xs, sv = _layer_fwd(xs, mems, Weights(l), small, l, token if l == 0 else None)
        fulls.append({n: stages[l, g]['full'][n] for g, names in enumerate(FWD_GROUPS) for n in names} | {'conv_w': conv_w_full[l]})
        saved.append(sv)
    sq, dx = _loss_head(xs, target)
    loss = lax.psum(0.5 * jnp.sum(sq) / d, ("x", "y", "c"))

    sent = []

    def to_chips(l, g, names, ax_g, bufs, after):
        n = len(names)
        sums = [_sum_halves(dv, th, core, "sum_halves") for dv, th in zip(bufs[:n], bufs[n:])]
        lands = [lax.empty(_grads_recv_shape(sm, ax), sm.dtype) for sm, ax in zip(sums, ax_g)]
        sems, bufs, tok = _exchange('start', f"grads_ici_start_{l}_{g}", sums + lands, _grads_ici_copies(ax_g), 3 * n,
                                    after=core if after is None else after)
        sent.append((l, g, names, ax_g, bufs, sems))
        return tok

    class Grads:
        def __init__(self, l):
            self.l, self.crossing = l, None

        def __call__(self, g, gb):
            names = BWD_GROUPS[g]
            ax_g = [BIG_AXIS[n] for n in names]
            n = len(names)
            dwvs = [_grad_view(gb[nm], ax) for nm, ax in zip(names, ax_g)]
            lands = [lax.empty((dv.shape[0],) + dv.shape[2:], dv.dtype) for dv in dwvs]
            if g + 1 < len(BWD_GROUPS):
                sems, bufs, tok = _exchange('start', f"grads_d2d_start_{self.l}_{g}", dwvs + lands, _grads_d2d_copies(n), n,
                                            after=core)
                self.crossing = (g, names, ax_g, bufs, sems)
                return tok
            return to_chips(self.l, g, names, ax_g, _exchange('sync', "grads_d2d", dwvs + lands, _grads_d2d_copies(n), n), None)

        def flush(self, follows):
            if self.crossing is None:
                return None
            (g, names, ax_g, bufs, sems), self.crossing = self.crossing, None
            bufs = _exchange('wait', f"grads_d2d_wait_{self.l}_{g}", bufs, _grads_d2d_copies(len(names)), len(names),
                             after=follows, sems=sems)
            return to_chips(self.l, g, names, ax_g, bufs, None)

    gsmall = {n: [None] * n_layers for n in SMALL + ['conv_w']}
    for l in reversed(range(n_layers)):
        dx, gs = _layer_bwd(dx, mems, saved[l], fulls[l], small, l, None, Grads(l))
        for n in gs:
            gsmall[n][l] = gs[n]

    names_small = SMALL + ['conv_w']
    small_full = [jnp.stack(gsmall[n]).reshape(w[n].shape) for n in SMALL]
    conv_w_grad = jnp.stack(gsmall['conv_w'])
    pack = _pack(small_full + [conv_w_grad], 8)
    small_sems, small_bufs, _ = _exchange('start', "small_start", [pack, lax.empty((N_DEV,) + pack.shape, F32)], _small_copies,
                                          N_DEV - 1, after=dx)

    def small_update(follows):
        pk, land = _exchange('wait', "small_wait", small_bufs, _small_copies, N_DEV - 1, after=follows, sems=small_sems)
        device = (4 * lax.axis_index("x") + 2 * lax.axis_index("y") + lax.axis_index("c")).astype(jnp.int32).reshape(1)
        summed = _sum_devices(land, pk, device)
        *small_g, conv_w_g = _unpack(summed, small_full + [conv_w_grad])
        shard = conv_w.shape[-1]
        conv_w_g = lax.dynamic_slice_in_dim(conv_w_g, chip[0] * shard, shard, axis=2)
        packed = [_pack([p[n] for n in names_small], 256) for p in (w, m, v)]
        gpack = _pack(small_g + [conv_w_g], 256)
        res = _adamw(packed[0][None], packed[1][None], packed[2][None], gpack, 0, None, "adamw_small")
        like = [w[n] for n in names_small]
        unpacked = [_unpack(r[0], like) for r in res]
        return {n: tuple(u[i] for u in unpacked) for i, n in enumerate(names_small)}

    out = {}

    def update(joining, follows):
        l, g, names, halves, sems = joining
        halves = _exchange('wait', f"join_wait_{l}_{g}", halves, _join_copies, len(names), after=follows, sems=sems)
        last = None
        for n, hv in zip(names, halves):
            out[n] = _adamw(w[n], m[n], v[n], hv.reshape(w[n].shape[1:]), l, out.get(n), "adamw_big", last)
            last = out[n][0]
        return last

    joining, follows = None, dx
    for k, (l, g, names, ax_g, bufs, sems) in enumerate(sent):
        n = len(names)
        bufs = _exchange('wait', f"grads_ici_wait_{l}_{g}", bufs, _grads_ici_copies(ax_g), 3 * n, after=follows, sems=sems)
        halves = [_sum_chips(r, sm, chip, core, ax) for sm, r, ax in zip(bufs[:n], bufs[n:], ax_g)]
        jsems, halves, tok = _exchange('start', f"join_start_{l}_{g}", halves, _join_copies, n, after=core)
        if joining is not None:
            follows = update(joining, tok)
        joining = (l, g, names, halves, jsems)
    out.update(small_update(update(joining, follows)))

    return (loss, dx[None], *[out[n][0] for n in WEIGHTS], *[out[n][1] for n in WEIGHTS],
            *[out[n][2] for n in WEIGHTS], *[out[n][3] for n in WEIGHTS])
```
